```python
import math
import jax, jax.numpy as jnp
from jax import lax
import numpy as np

D_MODEL = 1024
BATCH = 16
SEQ = 4096
DEPTH = 1

HEAD_DIM = 64
N_HEADS_FOX = 8
N_HEADS_DIL = 8
WIDTH_FOX = N_HEADS_FOX * HEAD_DIM
WIDTH_DIL = N_HEADS_DIL * HEAD_DIM
D_MIX = WIDTH_FOX + WIDTH_DIL
D_IN = 3 * WIDTH_FOX + N_HEADS_FOX + 3 * WIDTH_DIL
BLOCK = 128
DILATED_PATTERNS = ((128, 1), (512, 4), (2048, 16))
ROPE_THETA = 500000.0
ROPE_DIMS = HEAD_DIM // 4
D_FF = 2816
CONV_WIDTH = 3
DEEPNORM_ALPHA = (2.0 * DEPTH) ** 0.25
DEEPNORM_BETA = (8.0 * DEPTH) ** -0.25
LN_EPS = 1e-5
RMS_EPS = 1e-6

kernel_name = "fox_dilated_hybrid_deepnorm_block"


def _layer_norm(x, g, b):
    xf = x.astype(jnp.float32)
    mu = jnp.mean(xf, axis=-1, keepdims=True)
    var = jnp.mean(jnp.square(xf - mu), axis=-1, keepdims=True)
    y = (xf - mu) * lax.rsqrt(var + LN_EPS)
    return (y * g.astype(jnp.float32) + b.astype(jnp.float32)).astype(x.dtype)


def _head_rms_norm(o, gain):
    of = o.astype(jnp.float32)
    of = of * lax.rsqrt(jnp.mean(jnp.square(of), axis=-1, keepdims=True) + RMS_EPS)
    B, S, H, Dh = o.shape
    return (of.reshape(B, S, H * Dh) * gain.astype(jnp.float32)).astype(o.dtype)


def _partial_rotary(t, positions):
    half = ROPE_DIMS // 2
    freqs = ROPE_THETA ** (-jnp.arange(0, ROPE_DIMS, 2, dtype=jnp.float32) / ROPE_DIMS)
    ang = positions.astype(jnp.float32)[:, :, None] * freqs
    cos = jnp.cos(ang)[:, :, None, :]
    sin = jnp.sin(ang)[:, :, None, :]
    tf = t.astype(jnp.float32)
    t1, t2, rest = tf[..., :half], tf[..., half:ROPE_DIMS], tf[..., ROPE_DIMS:]
    rot = jnp.concatenate([t1 * cos - t2 * sin, t2 * cos + t1 * sin, rest], axis=-1)
    return rot.astype(t.dtype)


def _forgetting_attention(q, k, v, log_f):
    B, S, H, Dh = q.shape
    nb = S // BLOCK
    scale = Dh ** -0.5
    F = jnp.cumsum(log_f, axis=1).transpose(0, 2, 1)
    qh, kh, vh = (t.transpose(0, 2, 1, 3) for t in (q, k, v))
    q_blocks = jnp.moveaxis(qh.reshape(B, H, nb, BLOCK, Dh), 2, 0)
    F_blocks = jnp.moveaxis(F.reshape(B, H, nb, BLOCK), 2, 0)
    k_pos = jnp.arange(S)

    def one_block(args):
        qb, Fq, n = args
        s = jnp.einsum('bhqd,bhkd->bhqk', qb, kh).astype(jnp.float32) * scale
        s = s + Fq[..., :, None] - F[:, :, None, :]
        q_pos = n * BLOCK + jnp.arange(BLOCK)
        s = jnp.where(k_pos[None, :] <= q_pos[:, None], s, -jnp.inf)
        p = jax.nn.softmax(s, axis=-1)
        return jnp.einsum('bhqk,bhkd->bhqd', p.astype(vh.dtype), vh)

    o = lax.map(one_block, (q_blocks, F_blocks, jnp.arange(nb)))
    o = jnp.moveaxis(o, 0, 2).reshape(B, H, S, Dh)
    return o.transpose(0, 2, 1, 3)


def _dilated_pattern(q, k, v, dilation, steps):
    B, S, H, Dh = q.shape
    L = S // dilation
    BB = B * dilation
    scale = Dh ** -0.5

    def to_sub(t):
        return t.reshape(B, L, dilation, H, Dh).transpose(0, 2, 3, 1, 4).reshape(BB, H, L, Dh)

    nb = -(-L // BLOCK)
    Lp = nb * BLOCK
    pad = ((0, 0), (0, 0), (0, Lp - L), (0, 0))
    qs, ks, vs = (jnp.pad(to_sub(t), pad).reshape(BB, H, nb, BLOCK, Dh) for t in (q, k, v))
    blk_pad = ((0, 0), (0, 0), (1, 0), (0, 0), (0, 0))
    k_cat = jnp.concatenate([jnp.pad(ks, blk_pad)[:, :, :-1], ks], axis=3)
    v_cat = jnp.concatenate([jnp.pad(vs, blk_pad)[:, :, :-1], vs], axis=3)

    s = jnp.einsum('bhnqd,bhnkd->bhnqk', qs, k_cat).astype(jnp.float32) * scale
    qi = jnp.arange(BLOCK)[:, None]
    kj = jnp.arange(2 * BLOCK)[None, :]
    dist = qi + BLOCK - kj
    k_pos = jnp.arange(nb)[:, None, None] * BLOCK + kj - BLOCK
    mask = (dist >= 0) & (dist <= steps) & (k_pos >= 0)
    s = jnp.where(mask, s, -jnp.inf)
    m = jnp.max(s, axis=-1, keepdims=True)
    p = jnp.exp(s - m)
    l = jnp.sum(p, axis=-1, keepdims=True)
    o = jnp.einsum('bhnqk,bhnkd->bhnqd', (p / l).astype(v.dtype), v_cat)
    lse = (m + jnp.log(l))[..., 0]

    o = o.reshape(BB, H, Lp, Dh)[:, :, :L]
    o = o.reshape(B, dilation, H, L, Dh).transpose(0, 3, 1, 2, 4).reshape(B, S, H, Dh)
    lse = lse.reshape(BB, H, Lp)[:, :, :L]
    lse = lse.reshape(B, dilation, H, L).transpose(0, 3, 1, 2).reshape(B, S, H)
    return o, lse


def _dilated_attention(q, k, v):
    outs, lses = [], []
    for window, dilation in DILATED_PATTERNS:
        o, lse = _dilated_pattern(q, k, v, dilation, window // dilation)
        outs.append(o)
        lses.append(lse)
    w = jax.nn.softmax(jnp.stack(lses, axis=0), axis=0)
    o = jnp.sum(w[..., None] * jnp.stack(outs, axis=0).astype(jnp.float32), axis=0)
    return o.astype(q.dtype)


def _token_mixer(h, positions, w_in, b_fgate, gn_a, gn_b, w_out):
    B, S, _ = h.shape
    z = h @ w_in
    o0 = 0
    qa = z[..., o0:o0 + WIDTH_FOX]; o0 += WIDTH_FOX
    ka = z[..., o0:o0 + WIDTH_FOX]; o0 += WIDTH_FOX
    va = z[..., o0:o0 + WIDTH_FOX]; o0 += WIDTH_FOX
    fa = z[..., o0:o0 + N_HEADS_FOX]; o0 += N_HEADS_FOX
    qb = z[..., o0:o0 + WIDTH_DIL]; o0 += WIDTH_DIL
    kb = z[..., o0:o0 + WIDTH_DIL]; o0 += WIDTH_DIL
    vb = z[..., o0:o0 + WIDTH_DIL]

    heads = lambda t, H: t.reshape(B, S, H, HEAD_DIM)
    log_f = jax.nn.log_sigmoid((fa + b_fgate).astype(jnp.float32))
    oa = _forgetting_attention(heads(qa, N_HEADS_FOX), heads(ka, N_HEADS_FOX),
                               heads(va, N_HEADS_FOX), log_f)
    qb = _partial_rotary(heads(qb, N_HEADS_DIL), positions)
    kb = _partial_rotary(heads(kb, N_HEADS_DIL), positions)
    ob = _dilated_attention(qb, kb, heads(vb, N_HEADS_DIL))

    merged = jnp.concatenate([_head_rms_norm(oa, gn_a), _head_rms_norm(ob, gn_b)], axis=-1)
    return merged @ w_out


def _conv_ffn(h, w_up, conv_w, conv_b, w_down):
    u = h @ w_up
    up = jnp.pad(u, ((0, 0), (CONV_WIDTH - 1, 0), (0, 0)))
    S = h.shape[1]
    y = conv_b + sum(up[:, i:i + S] * conv_w[i] for i in range(CONV_WIDTH))
    a, g = jnp.split(y, 2, axis=-1)
    return (jax.nn.silu(g) * a) @ w_down


def _fwd_setup_inputs(seed: int = 0) -> dict:
    key = jax.random.key(seed)
    ks = jax.random.split(key, 20)
    n = jax.random.normal
    f32 = jnp.float32
    x = n(ks[0], (BATCH, SEQ, D_MODEL), f32)
    c = n(ks[1], (BATCH, D_MODEL), f32)
    offset = jax.random.randint(ks[2], (BATCH, 1), 0, 1024, dtype=jnp.int32)
    positions = (offset + jnp.arange(SEQ, dtype=jnp.int32)[None, :]).astype(jnp.int32)
    w_ada = n(ks[3], (DEPTH, D_MODEL, 6 * D_MODEL), f32) * D_MODEL ** -0.5
    b_ada = 0.02 * n(ks[4], (DEPTH, 6 * D_MODEL), f32)
    w_in = n(ks[5], (DEPTH, D_MODEL, D_IN), f32) * D_MODEL ** -0.5
    b_fgate = jnp.linspace(1.0, 6.0, N_HEADS_FOX, dtype=f32)[None, :] + 0.1 * n(ks[6], (DEPTH, N_HEADS_FOX), f32)
    gn_a = 1.0 + 0.02 * n(ks[7], (DEPTH, WIDTH_FOX), f32)
    gn_b = 1.0 + 0.02 * n(ks[8], (DEPTH, WIDTH_DIL), f32)
    w_out = n(ks[9], (DEPTH, D_MIX, D_MODEL), f32) * D_MIX ** -0.5 * DEEPNORM_BETA
    ln1_g = 1.0 + 0.02 * n(ks[10], (DEPTH, D_MODEL), f32)
    ln1_b = 0.02 * n(ks[11], (DEPTH, D_MODEL), f32)
    w_up = n(ks[12], (DEPTH, D_MODEL, 2 * D_FF), f32) * D_MODEL ** -0.5
    conv_w = n(ks[13], (DEPTH, CONV_WIDTH, 2 * D_FF), f32) * CONV_WIDTH ** -0.5
    conv_b = 0.02 * n(ks[14], (DEPTH, 2 * D_FF), f32)
    w_down = n(ks[15], (DEPTH, D_FF, D_MODEL), f32) * D_FF ** -0.5 * DEEPNORM_BETA
    ln2_g = 1.0 + 0.02 * n(ks[16], (DEPTH, D_MODEL), f32)
    ln2_b = 0.02 * n(ks[17], (DEPTH, D_MODEL), f32)
    return {"x": x, "c": c, "positions": positions, "w_ada": w_ada, "b_ada": b_ada,
            "w_in": w_in, "b_fgate": b_fgate, "gn_a": gn_a, "gn_b": gn_b, "w_out": w_out,
            "ln1_g": ln1_g, "ln1_b": ln1_b, "w_up": w_up, "conv_w": conv_w, "conv_b": conv_b,
            "w_down": w_down, "ln2_g": ln2_g, "ln2_b": ln2_b}


def _fwd_reference(x, c, positions, w_ada, b_ada, w_in, b_fgate, gn_a, gn_b, w_out,
              ln1_g, ln1_b, w_up, conv_w, conv_b, w_down, ln2_g, ln2_b):
    for l in range(DEPTH):
        ada = jax.nn.silu(c) @ w_ada[l] + b_ada[l]
        sh_a, sc_a, g_a, sh_f, sc_f, g_f = (t[:, None, :] for t in jnp.split(ada, 6, axis=-1))
        h = x * (1.0 + sc_a) + sh_a
        mix = _token_mixer(h, positions, w_in[l], b_fgate[l], gn_a[l], gn_b[l], w_out[l])
        x = _layer_norm(DEEPNORM_ALPHA * x + g_a * mix, ln1_g[l], ln1_b[l])
        h = x * (1.0 + sc_f) + sh_f
        ffn = _conv_ffn(h, w_up[l], conv_w[l], conv_b[l], w_down[l])
        x = _layer_norm(DEEPNORM_ALPHA * x + g_f * ffn, ln2_g[l], ln2_b[l])
    return x


import jax as _jax
import jax.numpy as _jnp

TWIN_FORMAT = 'train_step'
FWD_PARAMS = ['x', 'c', 'positions', 'w_ada', 'b_ada', 'w_in', 'b_fgate', 'gn_a', 'gn_b', 'w_out', 'ln1_g', 'ln1_b', 'w_up', 'conv_w', 'conv_b', 'w_down', 'ln2_g', 'ln2_b']
TWIN_WEIGHTS = ['w_ada', 'b_ada', 'w_in', 'b_fgate', 'gn_a', 'gn_b', 'w_out', 'ln1_g', 'ln1_b', 'w_up', 'conv_w', 'conv_b', 'w_down', 'ln2_g', 'ln2_b']
TWIN_DIFF_INPUT = 'x'
TWIN_INPUTS = ['x', 'c', 'positions', 'w_ada', 'b_ada', 'w_in', 'b_fgate', 'gn_a', 'gn_b', 'w_out', 'ln1_g', 'ln1_b', 'w_up', 'conv_w', 'conv_b', 'w_down', 'ln2_g', 'ln2_b', 'loss_target', 'm_w_ada', 'm_b_ada', 'm_w_in', 'm_b_fgate', 'm_gn_a', 'm_gn_b', 'm_w_out', 'm_ln1_g', 'm_ln1_b', 'm_w_up', 'm_conv_w', 'm_conv_b', 'm_w_down', 'm_ln2_g', 'm_ln2_b', 'v_w_ada', 'v_b_ada', 'v_w_in', 'v_b_fgate', 'v_gn_a', 'v_gn_b', 'v_w_out', 'v_ln1_g', 'v_ln1_b', 'v_w_up', 'v_conv_w', 'v_conv_b', 'v_w_down', 'v_ln2_g', 'v_ln2_b']
TWIN_OUTPUTS = ['loss', 'grad_x', 'grad_w_ada', 'grad_b_ada', 'grad_w_in', 'grad_b_fgate', 'grad_gn_a', 'grad_gn_b', 'grad_w_out', 'grad_ln1_g', 'grad_ln1_b', 'grad_w_up', 'grad_conv_w', 'grad_conv_b', 'grad_w_down', 'grad_ln2_g', 'grad_ln2_b', 'delta_w_ada', 'delta_b_ada', 'delta_w_in', 'delta_b_fgate', 'delta_gn_a', 'delta_gn_b', 'delta_w_out', 'delta_ln1_g', 'delta_ln1_b', 'delta_w_up', 'delta_conv_w', 'delta_conv_b', 'delta_w_down', 'delta_ln2_g', 'delta_ln2_b', 'new_m_w_ada', 'new_m_b_ada', 'new_m_w_in', 'new_m_b_fgate', 'new_m_gn_a', 'new_m_gn_b', 'new_m_w_out', 'new_m_ln1_g', 'new_m_ln1_b', 'new_m_w_up', 'new_m_conv_w', 'new_m_conv_b', 'new_m_w_down', 'new_m_ln2_g', 'new_m_ln2_b', 'new_v_w_ada', 'new_v_b_ada', 'new_v_w_in', 'new_v_b_fgate', 'new_v_gn_a', 'new_v_gn_b', 'new_v_w_out', 'new_v_ln1_g', 'new_v_ln1_b', 'new_v_w_up', 'new_v_conv_w', 'new_v_conv_b', 'new_v_w_down', 'new_v_ln2_g', 'new_v_ln2_b']
TWIN_LEAF_KINDS = {'loss': 'loss', 'grad_x': 'grad_x', 'grad_w_ada': 'grad_w', 'grad_b_ada': 'grad_w', 'grad_w_in': 'grad_w', 'grad_b_fgate': 'grad_w', 'grad_gn_a': 'grad_w', 'grad_gn_b': 'grad_w', 'grad_w_out': 'grad_w', 'grad_ln1_g': 'grad_w', 'grad_ln1_b': 'grad_w', 'grad_w_up': 'grad_w', 'grad_conv_w': 'grad_w', 'grad_conv_b': 'grad_w', 'grad_w_down': 'grad_w', 'grad_ln2_g': 'grad_w', 'grad_ln2_b': 'grad_w', 'delta_w_ada': 'delta_w', 'delta_b_ada': 'delta_w', 'delta_w_in': 'delta_w', 'delta_b_fgate': 'delta_w', 'delta_gn_a': 'delta_w', 'delta_gn_b': 'delta_w', 'delta_w_out': 'delta_w', 'delta_ln1_g': 'delta_w', 'delta_ln1_b': 'delta_w', 'delta_w_up': 'delta_w', 'delta_conv_w': 'delta_w', 'delta_conv_b': 'delta_w', 'delta_w_down': 'delta_w', 'delta_ln2_g': 'delta_w', 'delta_ln2_b': 'delta_w', 'new_m_w_ada': 'new_m', 'new_m_b_ada': 'new_m', 'new_m_w_in': 'new_m', 'new_m_b_fgate': 'new_m', 'new_m_gn_a': 'new_m', 'new_m_gn_b': 'new_m', 'new_m_w_out': 'new_m', 'new_m_ln1_g': 'new_m', 'new_m_ln1_b': 'new_m', 'new_m_w_up': 'new_m', 'new_m_conv_w': 'new_m', 'new_m_conv_b': 'new_m', 'new_m_w_down': 'new_m', 'new_m_ln2_g': 'new_m', 'new_m_ln2_b': 'new_m', 'new_v_w_ada': 'new_v', 'new_v_b_ada': 'new_v', 'new_v_w_in': 'new_v', 'new_v_b_fgate': 'new_v', 'new_v_gn_a': 'new_v', 'new_v_gn_b': 'new_v', 'new_v_w_out': 'new_v', 'new_v_ln1_g': 'new_v', 'new_v_ln1_b': 'new_v', 'new_v_w_up': 'new_v', 'new_v_conv_w': 'new_v', 'new_v_conv_b': 'new_v', 'new_v_w_down': 'new_v', 'new_v_ln2_g': 'new_v', 'new_v_ln2_b': 'new_v'}


def _forward(args):
    return _fwd_reference(*[args[k] for k in FWD_PARAMS])


def _output_shape():
    out = _jax.eval_shape(lambda: _forward(_fwd_setup_inputs(0)))
    return out.shape, out.dtype

N_MICROBATCH = 1
ADAM_LR = 0.001
ADAM_B1 = 0.9
ADAM_B2 = 0.999
ADAM_EPS = 1e-08
ADAM_WD = 0.01
ADAM_STEP = 10
PER_EXAMPLE_BATCH_AXIS = {'x': 0, 'c': 0, 'positions': 0, 'loss_target': 0}
SHARED_INPUTS = []
_WEIGHT_DTYPES = {'w_ada': _jnp.float32, 'b_ada': _jnp.float32, 'w_in': _jnp.float32, 'b_fgate': _jnp.float32, 'gn_a': _jnp.float32, 'gn_b': _jnp.float32, 'w_out': _jnp.float32, 'ln1_g': _jnp.float32, 'ln1_b': _jnp.float32, 'w_up': _jnp.float32, 'conv_w': _jnp.float32, 'conv_b': _jnp.float32, 'w_down': _jnp.float32, 'ln2_g': _jnp.float32, 'ln2_b': _jnp.float32}
MOMENT_SCALE = {'w_ada': 8.947464e-02, 'b_ada': 1.500704e-01, 'w_in': 6.960486e-02, 'b_fgate': 1.240065e-01, 'gn_a': 1.014232e-01, 'gn_b': 1.043131e-01, 'w_out': 1.833606e-01, 'ln1_g': 1.673572e+00, 'ln1_b': 6.610439e-01, 'w_up': 5.575236e-02, 'conv_w': 5.647929e-02, 'conv_b': 4.467164e-02, 'w_down': 1.534476e-01, 'ln2_g': 6.445859e+01, 'ln2_b': 8.853185e+00}


def _to_microbatches(a, axis):
    t = _jnp.moveaxis(a, axis, 0)
    t = t.reshape((N_MICROBATCH, t.shape[0] // N_MICROBATCH) + t.shape[1:])
    return _jnp.moveaxis(t, 1, axis + 1)


def setup_inputs(seed: int = 0) -> dict:
    inp = _fwd_setup_inputs(seed)
    key = _jax.random.fold_in(_jax.random.key(seed), 7919)
    shape, _ = _output_shape()
    out = dict(inp)
    out["loss_target"] = _jax.random.normal(_jax.random.fold_in(key, 0), shape, _jnp.float32)
    for i, name in enumerate(TWIN_WEIGHTS):
        w = inp[name].astype(_jnp.float32)
        if MOMENT_SCALE is None:
            s = _jnp.sqrt(_jnp.mean(_jnp.square(w)) + 1e-30)
        else:
            s = MOMENT_SCALE[name]
        km, kv = _jax.random.split(_jax.random.fold_in(key, i + 1))
        out[name] = w
        out["m_" + name] = s * _jax.random.normal(km, w.shape, _jnp.float32)
        out["v_" + name] = (s * s) * _jax.random.uniform(kv, w.shape, _jnp.float32, 0.5, 1.5)
    if N_MICROBATCH > 1:
        for name, axis in PER_EXAMPLE_BATCH_AXIS.items():
            out[name] = _to_microbatches(out[name], axis)
    return {'x': out['x'], 'c': out['c'], 'positions': out['positions'], 'w_ada': out['w_ada'], 'b_ada': out['b_ada'], 'w_in': out['w_in'], 'b_fgate': out['b_fgate'], 'gn_a': out['gn_a'], 'gn_b': out['gn_b'], 'w_out': out['w_out'], 'ln1_g': out['ln1_g'], 'ln1_b': out['ln1_b'], 'w_up': out['w_up'], 'conv_w': out['conv_w'], 'conv_b': out['conv_b'], 'w_down': out['w_down'], 'ln2_g': out['ln2_g'], 'ln2_b': out['ln2_b'], 'loss_target': out['loss_target'], 'm_w_ada': out['m_w_ada'], 'm_b_ada': out['m_b_ada'], 'm_w_in': out['m_w_in'], 'm_b_fgate': out['m_b_fgate'], 'm_gn_a': out['m_gn_a'], 'm_gn_b': out['m_gn_b'], 'm_w_out': out['m_w_out'], 'm_ln1_g': out['m_ln1_g'], 'm_ln1_b': out['m_ln1_b'], 'm_w_up': out['m_w_up'], 'm_conv_w': out['m_conv_w'], 'm_conv_b': out['m_conv_b'], 'm_w_down': out['m_w_down'], 'm_ln2_g': out['m_ln2_g'], 'm_ln2_b': out['m_ln2_b'], 'v_w_ada': out['v_w_ada'], 'v_b_ada': out['v_b_ada'], 'v_w_in': out['v_w_in'], 'v_b_fgate': out['v_b_fgate'], 'v_gn_a': out['v_gn_a'], 'v_gn_b': out['v_gn_b'], 'v_w_out': out['v_w_out'], 'v_ln1_g': out['v_ln1_g'], 'v_ln1_b': out['v_ln1_b'], 'v_w_up': out['v_w_up'], 'v_conv_w': out['v_conv_w'], 'v_conv_b': out['v_conv_b'], 'v_w_down': out['v_w_down'], 'v_ln2_g': out['v_ln2_g'], 'v_ln2_b': out['v_ln2_b']}


def _loss(weights, diff, rest, loss_target):
    with _jax.named_scope("forward"):
        args = {**rest, TWIN_DIFF_INPUT: diff, **{k: w.astype(_WEIGHT_DTYPES[k]) for k, w in weights.items()}}
        y = _forward(args)
    with _jax.named_scope("loss_head"):
        err = _jnp.square(y.astype(_jnp.float32) - loss_target)
        return 0.5 * _jnp.sum(_jnp.mean(err, axis=-1)) if err.ndim else 0.5 * err


def _adamw(w, g, m, v):
    m = ADAM_B1 * m + (1.0 - ADAM_B1) * g
    v = ADAM_B2 * v + (1.0 - ADAM_B2) * _jnp.square(g)
    m_hat = m / (1.0 - ADAM_B1 ** ADAM_STEP)
    v_hat = v / (1.0 - ADAM_B2 ** ADAM_STEP)
    delta = -ADAM_LR * (m_hat / (_jnp.sqrt(v_hat) + ADAM_EPS) + ADAM_WD * w)
    return delta, m, v


def reference(x, c, positions, w_ada, b_ada, w_in, b_fgate, gn_a, gn_b, w_out, ln1_g, ln1_b, w_up, conv_w, conv_b, w_down, ln2_g, ln2_b, loss_target, m_w_ada, m_b_ada, m_w_in, m_b_fgate, m_gn_a, m_gn_b, m_w_out, m_ln1_g, m_ln1_b, m_w_up, m_conv_w, m_conv_b, m_w_down, m_ln2_g, m_ln2_b, v_w_ada, v_b_ada, v_w_in, v_b_fgate, v_gn_a, v_gn_b, v_w_out, v_ln1_g, v_ln1_b, v_w_up, v_conv_w, v_conv_b, v_w_down, v_ln2_g, v_ln2_b):
    given = dict(x=x, c=c, positions=positions, w_ada=w_ada, b_ada=b_ada, w_in=w_in, b_fgate=b_fgate, gn_a=gn_a, gn_b=gn_b, w_out=w_out, ln1_g=ln1_g, ln1_b=ln1_b, w_up=w_up, conv_w=conv_w, conv_b=conv_b, w_down=w_down, ln2_g=ln2_g, ln2_b=ln2_b, loss_target=loss_target, m_w_ada=m_w_ada, m_b_ada=m_b_ada, m_w_in=m_w_in, m_b_fgate=m_b_fgate, m_gn_a=m_gn_a, m_gn_b=m_gn_b, m_w_out=m_w_out, m_ln1_g=m_ln1_g, m_ln1_b=m_ln1_b, m_w_up=m_w_up, m_conv_w=m_conv_w, m_conv_b=m_conv_b, m_w_down=m_w_down, m_ln2_g=m_ln2_g, m_ln2_b=m_ln2_b, v_w_ada=v_w_ada, v_b_ada=v_b_ada, v_w_in=v_w_in, v_b_fgate=v_b_fgate, v_gn_a=v_gn_a, v_gn_b=v_gn_b, v_w_out=v_w_out, v_ln1_g=v_ln1_g, v_ln1_b=v_ln1_b, v_w_up=v_w_up, v_conv_w=v_conv_w, v_conv_b=v_conv_b, v_w_down=v_w_down, v_ln2_g=v_ln2_g, v_ln2_b=v_ln2_b)
    weights = {n: given[n] for n in TWIN_WEIGHTS}
    shared = {n: given[n] for n in SHARED_INPUTS}
    per_example = {n: given[n] for n in ['x', 'c', 'positions']}
    grad_fn = _jax.value_and_grad(_loss, argnums=(0, 1))

    def one_microbatch(ex, loss_target):
        ex = dict(ex)
        diff = ex.pop(TWIN_DIFF_INPUT)
        return grad_fn(weights, diff, {**shared, **ex}, loss_target)

    if N_MICROBATCH == 1:
        loss, (grad_w, grad_x) = one_microbatch(per_example, given["loss_target"])
    else:
        def body(carry, xs):
            loss_sum, grad_sum = carry
            l_k, (gw_k, gx_k) = one_microbatch(xs[0], xs[1])
            with _jax.named_scope("update"):
                return (loss_sum + l_k, _jax.tree.map(_jnp.add, grad_sum, gw_k)), gx_k

        init = (_jnp.zeros((), _jnp.float32), _jax.tree.map(_jnp.zeros_like, weights))
        (loss, grad_w), grad_x = _jax.lax.scan(body, init, (per_example, given["loss_target"]))
    with _jax.named_scope("update"):
        delta_w, new_m, new_v = {}, {}, {}
        for n in TWIN_WEIGHTS:
            delta_w[n], new_m[n], new_v[n] = _adamw(weights[n], grad_w[n], given["m_" + n], given["v_" + n])
    return (loss, grad_x, *[grad_w[n] for n in TWIN_WEIGHTS], *[delta_w[n] for n in TWIN_WEIGHTS],
            *[new_m[n] for n in TWIN_WEIGHTS], *[new_v[n] for n in TWIN_WEIGHTS])
```

```python
import functools
import math

import numpy as np
import jax
import jax.numpy as jnp
from jax import lax
from jax.experimental import pallas as pl
from jax.experimental.pallas import tpu as pltpu

F32 = jnp.float32
BF16 = jnp.bfloat16

D = 1024
S = 4096
HD = 64
WA = 512
DFF = 2816
NCHIP = 4
NDEV = 8
PATTERNS = ((128, 1), (512, 4), (2048, 16))
ROPE_THETA = 500000.0
ROPE_DIMS = HD // 4
ALPHA = (2.0 * 1) ** 0.25
LN_EPS = 1e-5
RMS_EPS = 1e-6
ADAM_LR = 0.001
ADAM_B1 = 0.9
ADAM_B2 = 0.999
ADAM_EPS = 1e-08
ADAM_WD = 0.01
ADAM_STEP = 10

LANES = 128
SUBLANES = 8
TQ = 256
FPAD = LANES
NEG = -1e30
VMEM_LIMIT = 56 * 1024 * 1024
MESH = pl.DeviceIdType.MESH


def _cparams(sem):
    return pltpu.CompilerParams(dimension_semantics=sem, vmem_limit_bytes=VMEM_LIMIT)


def _pick(n, cands):
    for c in cands:
        if n % c == 0:
            return c
    return n


def _rsum8(v):
    tm, w = v.shape
    return jnp.sum(v.reshape(tm // SUBLANES, SUBLANES, w), axis=0)


def _sigmoid(x):
    return 1.0 / (1.0 + jnp.exp(-x))


def _dot(a, b):
    return jnp.dot(a, b, preferred_element_type=F32)


def _dot_nt(a, b):
    return lax.dot_general(a, b, (((1,), (1,)), ((), ())), preferred_element_type=F32)


def _dot_tn(a, b):
    return lax.dot_general(a, b, (((0,), (0,)), ((), ())), preferred_element_type=F32)


def _rowwise(name, fn, T, tm, *, tiles=(), halos=(), seqvecs=(), consts=(), outs=(), accs=(), seqaccs=(),
             seq_len=None):
    seq_len = S if seq_len is None else seq_len
    nb = T // tm
    spb = max(seq_len // tm, 1)
    nseq = max(T // seq_len, 1)
    n8 = T // SUBLANES
    r8 = tm // SUBLANES
    in_specs, args = [], []
    for a in tiles:
        in_specs.append(pl.BlockSpec((tm, a.shape[1]), lambda i: (i, 0)))
        args.append(a)
    for a, direction in halos:
        if direction < 0:
            idx = lambda i: (jnp.maximum(i * r8 - 1, 0), 0)
        else:
            idx = lambda i: (jnp.minimum((i + 1) * r8, n8 - 1), 0)
        in_specs.append(pl.BlockSpec((SUBLANES, a.shape[1]), idx))
        args.append(a)
    for a in seqvecs:
        in_specs.append(pl.BlockSpec((1, 1, a.shape[2]), lambda i: (i // spb, 0, 0)))
        args.append(a)
    for a in consts:
        in_specs.append(pl.BlockSpec(a.shape, lambda i, nd=a.ndim: (0,) * nd))
        args.append(a)
    out_shape, out_specs = [], []
    for w, dt in outs:
        out_shape.append(jax.ShapeDtypeStruct((T, w), dt))
        out_specs.append(pl.BlockSpec((tm, w), lambda i: (i, 0)))
    for w in accs:
        out_shape.append(jax.ShapeDtypeStruct((SUBLANES, w), F32))
        out_specs.append(pl.BlockSpec((SUBLANES, w), lambda i: (0, 0)))
    for w in seqaccs:
        out_shape.append(jax.ShapeDtypeStruct((nseq, SUBLANES, w), F32))
        out_specs.append(pl.BlockSpec((1, SUBLANES, w), lambda i: (i // spb, 0, 0)))
    n_t, n_h, n_s, n_c = len(tiles), len(halos), len(seqvecs), len(consts)
    n_o, n_a, n_sa = len(outs), len(accs), len(seqaccs)

    def body(*refs):
        i = pl.program_id(0)
        ins = refs[:n_t + n_h + n_s + n_c]
        orefs = refs[n_t + n_h + n_s + n_c:]
        vals = [r[...] for r in ins[:n_t + n_h]]
        vals += [r[0] for r in ins[n_t + n_h:n_t + n_h + n_s]]
        vals += list(ins[n_t + n_h + n_s:])
        res = fn(i, *vals)
        if not isinstance(res, (tuple, list)):
            res = (res,)
        for k in range(n_o):
            orefs[k][...] = res[k].astype(orefs[k].dtype)
        for k in range(n_a):
            r = orefs[n_o + k]

            @pl.when(i == 0)
            def _():
                r[...] = jnp.zeros_like(r)

            r[...] += res[n_o + k]

            @pl.when(i == nb - 1)
            def _():
                r[...] = jnp.broadcast_to(jnp.sum(r[...], axis=0, keepdims=True), r.shape)
        for k in range(n_sa):
            r = orefs[n_o + n_a + k]

            @pl.when(i % spb == 0)
            def _():
                r[...] = jnp.zeros_like(r)

            r[0] += res[n_o + n_a + k]

            @pl.when(i % spb == spb - 1)
            def _():
                r[0] = jnp.broadcast_to(jnp.sum(r[0], axis=0, keepdims=True), r.shape[1:])

    sem = ("arbitrary",) if (n_a or n_sa) else ("parallel",)
    res = pl.pallas_call(
        body, name=name, grid=(nb,), in_specs=in_specs, out_specs=out_specs, out_shape=out_shape,
        compiler_params=_cparams(sem),
    )(*args)
    return res


def _ln_fwd(r, g, b):
    mu = jnp.mean(r, axis=-1, keepdims=True)
    xc = r - mu
    var = jnp.mean(xc * xc, axis=-1, keepdims=True)
    rstd = lax.rsqrt(var + LN_EPS)
    n = xc * rstd
    return n * g + b, n, rstd


def _ln_bwd(dy, n, rstd, g):
    dn = dy * g
    return rstd * (dn - jnp.mean(dn, axis=-1, keepdims=True) - n * jnp.mean(dn * n, axis=-1, keepdims=True))


def _head_mean(t, g_ref):
    gw = g_ref.shape[0]
    hi = t.astype(BF16)
    lo = (t - hi.astype(F32)).astype(BF16)
    g = g_ref[...]
    parts = []
    for c in range(t.shape[1] // gw):
        sl = slice(c * gw, (c + 1) * gw)
        parts.append(_dot(hi[:, sl], g) + _dot(lo[:, sl], g))
    out = parts[0] if len(parts) == 1 else jnp.concatenate(parts, axis=1)
    return out * (1.0 / HD)


def _rope(z, c, s1, s2):
    w = z.shape[1]
    half = ROPE_DIMS // 2
    return z * c + pltpu.roll(z, half, 1) * s1 + pltpu.roll(z, w - half, 1) * s2


def _tile_lanes(t, w):
    reps = w // t.shape[1]
    return t if reps == 1 else jnp.concatenate([t] * reps, axis=1)


def _conv_taps(ext, prev, first):
    tm = ext.shape[0]
    prev = jnp.where(first, jnp.zeros_like(prev), prev)
    r8 = lax.broadcasted_iota(jnp.int32, (SUBLANES, 1), 0)
    top = ext[0:SUBLANES]
    s1_top = jnp.where(r8 < 1, pltpu.roll(prev, 1, 0), pltpu.roll(top, 1, 0))
    s2_top = jnp.where(r8 < 2, pltpu.roll(prev, 2, 0), pltpu.roll(top, 2, 0))
    s1 = jnp.concatenate([s1_top, pltpu.roll(ext, 1, 0)[SUBLANES:]], axis=0)
    s2 = jnp.concatenate([s2_top, pltpu.roll(ext, 2, 0)[SUBLANES:]], axis=0)
    return s1, s2


def _conv_taps_up(ext, nxt, last):
    tm = ext.shape[0]
    nxt = jnp.where(last, jnp.zeros_like(nxt), nxt)
    r8 = lax.broadcasted_iota(jnp.int32, (SUBLANES, 1), 0)
    bot = ext[tm - SUBLANES:tm]
    u1_bot = jnp.where(r8 >= 7, pltpu.roll(nxt, 7, 0), pltpu.roll(bot, 7, 0))
    u2_bot = jnp.where(r8 >= 6, pltpu.roll(nxt, 6, 0), pltpu.roll(bot, 6, 0))
    u1 = jnp.concatenate([pltpu.roll(ext, tm - 1, 0)[:tm - SUBLANES], u1_bot], axis=0)
    u2 = jnp.concatenate([pltpu.roll(ext, tm - 2, 0)[:tm - SUBLANES], u2_bot], axis=0)
    return u1, u2


def _mm_nt(name, a, w, tm=256):
    T = a.shape[0]
    n = w.shape[0]
    ch = _pick(n, (512, 256, 128))

    def fn(i, av, w_ref):
        ab = av.astype(BF16)
        parts = [_dot_nt(ab, w_ref[c * ch:(c + 1) * ch, :]) for c in range(n // ch)]
        return parts[0] if len(parts) == 1 else jnp.concatenate(parts, axis=1)

    return _rowwise(name, fn, T, tm, tiles=(a,), consts=(w,), outs=((n, F32),))[0]


def _mm_tn(name, a, b, *, mod=None, tt=512):
    T, k1 = a.shape
    k2 = b.shape[1]
    t1 = k1 if k1 <= 1536 else _pick(k1, (1408, 1024, 512, 256, 128))
    t2 = k2 if k2 <= 1536 else _pick(k2, (1408, 1024, 640, 512, 256, 128))
    tt = min(tt, S)
    spb = S // tt

    def body(*refs):
        if mod is not None:
            a_ref, sc_ref, sh_ref, b_ref, o_ref = refs
        else:
            a_ref, b_ref, o_ref = refs
        t = pl.program_id(2)

        @pl.when(t == 0)
        def _():
            o_ref[...] = jnp.zeros_like(o_ref)

        av = a_ref[...]
        if mod is not None:
            av = av * (1.0 + sc_ref[0]) + sh_ref[0]
        o_ref[...] += _dot_tn(av.astype(BF16), b_ref[...].astype(BF16))

    in_specs = [pl.BlockSpec((tt, t1), lambda p, q, t: (t, p))]
    args = [a]
    if mod is not None:
        for v in mod:
            in_specs.append(pl.BlockSpec((1, 1, t1), lambda p, q, t: (t // spb, 0, p)))
            args.append(v)
    in_specs.append(pl.BlockSpec((tt, t2), lambda p, q, t: (t, q)))
    args.append(b)
    return pl.pallas_call(
        body, name=name, grid=(k1 // t1, k2 // t2, T // tt), in_specs=in_specs,
        out_specs=pl.BlockSpec((t1, t2), lambda p, q, t: (p, q)),
        out_shape=jax.ShapeDtypeStruct((k1, k2), F32),
        compiler_params=_cparams(("parallel", "parallel", "arbitrary")),
    )(*args)


def _mod_mm(name, x, sc, sh, ws, out_dtypes, rope=None, rope_secs=(), tm=256):
    T = x.shape[0]
    nw = len(ws)

    def fn(i, xv, *rest):
        if rope is not None:
            cv, s1v, s2v = rest[:3]
            rest = rest[3:]
        scv, shv = rest[:2]
        w_refs = rest[2:]
        h = (xv * (1.0 + scv) + shv).astype(BF16)
        res = []
        for k, w_ref in enumerate(w_refs):
            n = w_ref.shape[1]
            ch = WA if (k == 0 and rope is not None) else _pick(n, (512, 256, 128))
            parts = []
            for c in range(n // ch):
                z = _dot(h, w_ref[:, c * ch:(c + 1) * ch])
                if k == 0 and c in rope_secs:
                    z = _rope(z, _tile_lanes(cv, ch), _tile_lanes(s1v, ch), _tile_lanes(s2v, ch))
                parts.append(z.astype(out_dtypes[k]))
            res.append(parts[0] if len(parts) == 1 else jnp.concatenate(parts, axis=1))
        return tuple(res)

    tiles = (x,) + (tuple(rope) if rope is not None else ())
    outs = tuple((w.shape[1], dt) for w, dt in zip(ws, out_dtypes))
    return _rowwise(name, fn, T, tm, tiles=tiles, seqvecs=(sc, sh), consts=tuple(ws), outs=outs)


def _tri(tb, lower):
    r = lax.broadcasted_iota(jnp.int32, (tb, tb), 0)
    c = lax.broadcasted_iota(jnp.int32, (tb, tb), 1)
    return jnp.where((r >= c) if lower else (r <= c), 1.0, 0.0).astype(F32)


def _cumsum_seq(name, v, fn_in, fn_out, reverse, extra=(), n_acc=0, tb=512):
    T = v.shape[0]
    tb = min(tb, S)
    nbs = S // tb
    nseq = T // S

    def blk(b, j):
        return (b * nbs + (nbs - 1 - j if reverse else j), 0)

    def body(*refs):
        v_ref = refs[0]
        e_refs = refs[1:1 + len(extra)]
        o_ref = refs[1 + len(extra)]
        acc_refs = refs[2 + len(extra):2 + len(extra) + n_acc]
        carry = refs[-1]
        b, j = pl.program_id(0), pl.program_id(1)

        @pl.when(j == 0)
        def _():
            carry[...] = jnp.zeros_like(carry)

        ev = [e[...] for e in e_refs]
        xin = fn_in(v_ref[...], *ev)
        cum = jnp.dot(_tri(tb, not reverse), xin, preferred_element_type=F32,
                      precision=lax.Precision.HIGHEST) + carry[0:1, :]
        carry[...] = carry[...] + jnp.sum(xin, axis=0, keepdims=True)
        out = fn_out(cum, v_ref[...], *ev)
        o_ref[...] = out
        for a in acc_refs:
            @pl.when((b == 0) & (j == 0))
            def _():
                a[...] = jnp.zeros_like(a)

            a[...] += _rsum8(out)

            @pl.when((b == nseq - 1) & (j == nbs - 1))
            def _():
                a[...] = jnp.broadcast_to(jnp.sum(a[...], axis=0, keepdims=True), a.shape)

    in_specs = [pl.BlockSpec((tb, FPAD), blk)]
    for e in extra:
        if e.shape[0] == T:
            in_specs.append(pl.BlockSpec((tb, FPAD), blk))
        else:
            in_specs.append(pl.BlockSpec(e.shape, lambda b, j: (0, 0)))
    out_shape = [jax.ShapeDtypeStruct((T, FPAD), F32)] + [jax.ShapeDtypeStruct((SUBLANES, FPAD), F32)] * n_acc
    out_specs = [pl.BlockSpec((tb, FPAD), blk)] + [pl.BlockSpec((SUBLANES, FPAD), lambda b, j: (0, 0))] * n_acc
    return pl.pallas_call(
        body, name=name, grid=(nseq, nbs), in_specs=in_specs, out_specs=out_specs, out_shape=out_shape,
        scratch_shapes=[pltpu.VMEM((SUBLANES, FPAD), F32)],
        compiler_params=_cparams(("arbitrary", "arbitrary")),
    )(v, *extra)


def _log_sigmoid(x):
    return jnp.minimum(x, 0.0) - jnp.log(1.0 + jnp.exp(-jnp.abs(x)))


def _dil_bias(tq):
    max_win = max(w for w, _ in PATTERNS)
    nd = (max_win + tq - 1) // tq + 1
    qi = np.arange(tq)[:, None]
    kj = np.arange(tq)[None, :]
    tabs = []
    for dlt in range(nd):
        dist = dlt * tq + qi - kj
        mult = np.zeros((tq, tq), np.float64)
        for win, dil in PATTERNS:
            mult += (dist >= 0) & (dist % dil == 0) & (dist // dil <= win // dil)
        tabs.append(np.where(mult > 0, np.log(np.maximum(mult, 1.0)), NEG))
    return np.stack(tabs).astype(np.float32)


def _head_cols(x2, h0):
    sw = pltpu.roll(x2, HD, 1)
    reps = TQ // LANES
    wide = lambda t: t if reps == 1 else jnp.concatenate([t] * reps, axis=1)
    return wide(jnp.where(h0, x2, sw)), wide(jnp.where(h0, sw, x2))


def _attn_fwd(name, qkv, secs, fox, fcol=None, frow=None, bias=None):
    T = qkv.shape[0]
    nq = S // TQ
    nbl = T // S
    hp_n = WA // LANES
    sq, sk, sv = (s * hp_n for s in secs)
    scale = HD ** -0.5
    nd = None if fox else bias.shape[0]

    def body(*refs):
        if fox:
            q_ref, k_ref, v_ref, fc_ref, fr_ref, o_ref, lse_ref = refs
        else:
            q_ref, k_ref, v_ref, b_ref, o_ref, lse_ref = refs
        i = pl.program_id(2)
        lane = lax.broadcasted_iota(jnp.int32, (1, LANES), 1)
        h0 = lane < HD
        q2 = q_ref[...]
        zq = jnp.zeros_like(q2)
        qh = (jnp.where(h0, q2, zq), jnp.where(h0, zq, q2))
        if fox:
            fq = _head_cols(fc_ref[...], h0)
            rows = lax.broadcasted_iota(jnp.int32, (TQ, TQ), 0)
            cols = lax.broadcasted_iota(jnp.int32, (TQ, TQ), 1)

        def step(t, carry):
            off = pl.multiple_of((i - t) * TQ, TQ)
            k2 = k_ref[pl.ds(off, TQ), :]
            v2 = v_ref[pl.ds(off, TQ), :]
            new = []
            for h in (0, 1):
                m, l, acc = carry[h]
                s = _dot_nt(qh[h], k2) * scale
                if fox:
                    fk = fr_ref[0, h, :, pl.ds(off, TQ)]
                    s = s + (fq[h] - fk)
                    s = jnp.where(cols <= rows + t * TQ, s, NEG)
                else:
                    s = s + b_ref[t]
                m_new = jnp.maximum(m, jnp.max(s, axis=1, keepdims=True))
                p = jnp.exp(s - m_new)
                a = jnp.exp(m - m_new)
                l = a * l + jnp.sum(p, axis=1, keepdims=True)
                acc = a * acc + _dot(p.astype(BF16), v2)
                new.append((m_new, l, acc))
            return tuple(new)

        init = tuple((jnp.full((TQ, 1), NEG, F32), jnp.zeros((TQ, 1), F32), jnp.zeros((TQ, LANES), F32))
                     for _ in (0, 1))
        n = i + 1 if fox else jnp.minimum(i + 1, nd)
        (m0, l0, a0), (m1, l1, a1) = lax.fori_loop(0, n, step, init)
        o_ref[...] = jnp.where(h0, a0 / l0, a1 / l1)
        lse_ref[...] = jnp.where(h0, m0 + jnp.log(l0), m1 + jnp.log(l1))

    in_specs = [
        pl.BlockSpec((TQ, LANES), lambda b, hp, i: (b * nq + i, sq + hp)),
        pl.BlockSpec((S, LANES), lambda b, hp, i: (b, sk + hp)),
        pl.BlockSpec((S, LANES), lambda b, hp, i: (b, sv + hp)),
    ]
    args = [qkv, qkv, qkv]
    if fox:
        in_specs += [pl.BlockSpec((TQ, LANES), lambda b, hp, i: (b * nq + i, hp)),
                     pl.BlockSpec((1, 2, 1, S), lambda b, hp, i: (b, hp, 0, 0))]
        args += [fcol, frow]
    else:
        in_specs.append(pl.BlockSpec(bias.shape, lambda b, hp, i: (0, 0, 0)))
        args.append(bias)
    ospec = pl.BlockSpec((TQ, LANES), lambda b, hp, i: (b * nq + i, hp))
    return pl.pallas_call(
        body, name=name, grid=(nbl, hp_n, nq), in_specs=in_specs, out_specs=[ospec, ospec],
        out_shape=[jax.ShapeDtypeStruct((T, WA), F32)] * 2,
        compiler_params=_cparams(("parallel", "parallel", "arbitrary")),
    )(*args)


def _attn_bwd(name, qkv, secs, o, do, do_sec, lse, fox, fcol=None, frow=None, bias=None):
    T = qkv.shape[0]
    nq = S // TQ
    nbl = T // S
    hp_n = WA // LANES
    sq, sk, sv = (s * hp_n for s in secs)
    dsec = do_sec * hp_n
    scale = HD ** -0.5
    nd = None if fox else bias.shape[0]

    def body(*refs):
        if fox:
            (q_ref, k_ref, v_ref, o_ref, do_ref, lse_ref, fc_ref, fr_ref,
             dq_ref, dk_ref, dv_ref, df_ref, dfq_ref, dl_ref) = refs
        else:
            q_ref, k_ref, v_ref, o_ref, do_ref, lse_ref, b_ref, dq_ref, dk_ref, dv_ref, dl_ref = refs
        j = pl.program_id(2)
        lane = lax.broadcasted_iota(jnp.int32, (1, LANES), 1)
        h0 = lane < HD
        hm = (h0, jnp.logical_not(h0))

        @pl.when(j == 0)
        def _():
            dq_ref[...] = jnp.zeros_like(dq_ref)
            if fox:
                dfq_ref[...] = jnp.zeros_like(dfq_ref)

            def dl_step(r, c):
                off = pl.multiple_of(r * TQ, TQ)
                d2 = do_ref[pl.ds(off, TQ), :] * o_ref[pl.ds(off, TQ), :]
                z2 = jnp.zeros_like(d2)
                dl0 = jnp.sum(jnp.where(h0, d2, z2), axis=1, keepdims=True)
                dl1 = jnp.sum(jnp.where(h0, z2, d2), axis=1, keepdims=True)
                dl_ref[pl.ds(off, TQ), :] = jnp.where(h0, dl0, dl1)
                return c

            lax.fori_loop(0, nq, dl_step, 0)

        k2 = k_ref[...]
        v2 = v_ref[...]
        zk = jnp.zeros_like(k2)
        kh = (jnp.where(h0, k2, zk), jnp.where(h0, zk, k2))
        if fox:
            rows = lax.broadcasted_iota(jnp.int32, (TQ, TQ), 0)
            cols = lax.broadcasted_iota(jnp.int32, (TQ, TQ), 1)
            fk = (fr_ref[0, 0], fr_ref[0, 1])

        def step(t, carry):
            dk2, dv2, dfs = carry
            off = pl.multiple_of((j + t) * TQ, TQ)
            q2 = q_ref[pl.ds(off, TQ), :]
            do2 = do_ref[pl.ds(off, TQ), :].astype(BF16)
            lse_h = _head_cols(lse_ref[pl.ds(off, TQ), :], h0)
            dl_h = _head_cols(dl_ref[pl.ds(off, TQ), :], h0)
            if fox:
                fq = _head_cols(fc_ref[pl.ds(off, TQ), :], h0)
            zq = jnp.zeros_like(q2)
            dq2 = jnp.zeros((TQ, LANES), F32)
            new_dfs, rsum = [], []
            for h in (0, 1):
                qh = jnp.where(hm[h], q2, zq)
                doh = jnp.where(hm[h], do2, zq)
                s = _dot_nt(qh, k2) * scale
                if fox:
                    s = s + (fq[h] - fk[h])
                    s = jnp.where(cols <= rows + t * TQ, s, NEG)
                else:
                    s = s + b_ref[t]
                p = jnp.exp(s - lse_h[h])
                dp = _dot_nt(doh, v2)
                ds = p * (dp - dl_h[h])
                dsb = (ds * scale).astype(BF16)
                dv2 = dv2 + _dot_tn(p.astype(BF16), doh)
                dk2 = dk2 + _dot_tn(dsb, qh)
                dq2 = dq2 + _dot(dsb, kh[h])
                if fox:
                    new_dfs.append(dfs[h] - jnp.sum(ds, axis=0, keepdims=True))
                    rsum.append(jnp.sum(ds, axis=1, keepdims=True))
            dq_ref[pl.ds(off, TQ), :] += dq2
            if fox:
                dfq_ref[pl.ds(off, TQ), :] += jnp.where(h0, rsum[0], rsum[1])
            return dk2, dv2, tuple(new_dfs)

        zf = jnp.zeros((TQ, LANES), F32)
        dfs0 = (jnp.zeros((1, TQ), F32), jnp.zeros((1, TQ), F32)) if fox else ()
        n = nq - j if fox else jnp.minimum(nq - j, nd)
        dk2, dv2, dfs = lax.fori_loop(0, n, step, (zf, zf, dfs0))
        dk_ref[...] = dk2
        dv_ref[...] = dv2
        if fox:
            df_ref[0, 0] = dfs[0]
            df_ref[0, 1] = dfs[1]

    seq = lambda c: pl.BlockSpec((S, LANES), lambda b, hp, j: (b, c + hp))
    blk = lambda c: pl.BlockSpec((TQ, LANES), lambda b, hp, j: (b * nq + j, c + hp))
    in_specs = [seq(sq), blk(sk), blk(sv), seq(0), seq(dsec), seq(0)]
    args = [qkv, qkv, qkv, o, do, lse]
    if fox:
        in_specs += [seq(0), pl.BlockSpec((1, 2, 1, TQ), lambda b, hp, j: (b, hp, 0, j))]
        args += [fcol, frow]
    else:
        in_specs.append(pl.BlockSpec(bias.shape, lambda b, hp, j: (0, 0, 0)))
        args.append(bias)
    out_specs = [seq(0), blk(0), blk(0)]
    out_shape = [jax.ShapeDtypeStruct((T, WA), F32)] * 3
    if fox:
        out_specs.append(pl.BlockSpec((1, 2, 1, TQ), lambda b, hp, j: (b, hp, 0, j)))
        out_shape.append(jax.ShapeDtypeStruct((nbl, 2 * hp_n, 1, S), F32))
        out_specs.append(seq(0))
        out_shape.append(jax.ShapeDtypeStruct((T, WA), F32))
    return pl.pallas_call(
        body, name=name, grid=(nbl, hp_n, nq), in_specs=in_specs, out_specs=out_specs, out_shape=out_shape,
        scratch_shapes=[pltpu.VMEM((S, LANES), F32)],
        compiler_params=_cparams(("parallel", "parallel", "arbitrary")),
    )(*args)


def _exchange(name, ins, out_shapes, remote, local):
    n_in, n_out = len(ins), len(out_shapes)
    nr, nl = len(remote), len(local)

    def body(*refs):
        in_refs = refs[:n_in]
        out_refs = refs[n_in:n_in + n_out]
        send_sems, recv_sems, loc_sems = refs[n_in + n_out:]
        me = (lax.axis_index("x"), lax.axis_index("y"), lax.axis_index("c"))

        def peer_of(flip):
            return tuple(1 - v if f else v for v, f in zip(me, flip))

        def at(ref, idx):
            return ref if idx is None else ref.at[idx]

        def rcopy(k, who):
            flip, a, sfn, b, dfn = remote[k]
            return pltpu.make_async_remote_copy(
                src_ref=at(in_refs[a], sfn(*who)), dst_ref=at(out_refs[b], dfn(*who)),
                send_sem=send_sems.at[k], recv_sem=recv_sems.at[k],
                device_id=peer_of(flip), device_id_type=MESH)

        locs = [pltpu.make_async_copy(at(in_refs[a], sfn(*me)), at(out_refs[b], dfn(*me)), loc_sems.at[k])
                for k, (a, sfn, b, dfn) in enumerate(local)]
        for cp in locs:
            cp.start()
        sends = [rcopy(k, me) for k in range(nr)]
        for cp in sends:
            cp.start()
        for k in range(nr):
            rcopy(k, peer_of(remote[k][0])).wait_recv()
        for cp in sends:
            cp.wait_send()
        for cp in locs:
            cp.wait()

    any_spec = pl.BlockSpec(memory_space=pl.ANY)
    return pl.pallas_call(
        body, name=name, in_specs=[any_spec] * n_in, out_specs=[any_spec] * n_out, out_shape=list(out_shapes),
        scratch_shapes=[pltpu.SemaphoreType.DMA((max(nr, 1),)), pltpu.SemaphoreType.DMA((max(nr, 1),)),
                        pltpu.SemaphoreType.DMA((max(nl, 1),))],
    )(*ins)


_FLIPS7 = [(0, 0, 1), (0, 1, 0), (0, 1, 1), (1, 0, 0), (1, 0, 1), (1, 1, 0), (1, 1, 1)]
_CHIP_FLIPS = [(1, 0, 0), (0, 1, 0), (1, 1, 0)]


def _dev_index(x, y, c):
    return 4 * x + 2 * y + c


def _chip_index(x, y, c):
    return 2 * x + y


def _all_gather8(name, v):
    remote = [(f, 0, lambda x, y, c: None, 0, _dev_index) for f in _FLIPS7]
    local = [(0, lambda x, y, c: None, 0, _dev_index)]
    return _exchange(name, [v], [jax.ShapeDtypeStruct((NDEV,) + v.shape, v.dtype)], remote, local)[0]


def _gather_chips(name, vs):
    remote, local = [], []
    for n in range(len(vs)):
        local.append((n, lambda x, y, c: None, n, _chip_index))
        for f in _CHIP_FLIPS:
            remote.append((f, n, lambda x, y, c: None, n, _chip_index))
    shapes = [jax.ShapeDtypeStruct((NCHIP,) + v.shape, v.dtype) for v in vs]
    return _exchange(name, list(vs), shapes, remote, local)


def _to_sibling(name, v):
    remote = [((0, 0, 1), 0, lambda x, y, c: None, 0, lambda x, y, c: None)]
    return _exchange(name, [v], [jax.ShapeDtypeStruct(v.shape, v.dtype)], remote, [])[0]


def _scatter_chips(name, v):
    remote = []
    for f in _CHIP_FLIPS:
        src = lambda x, y, c, f=f: _chip_index(1 - x if f[0] else x, 1 - y if f[1] else y, c)
        remote.append((f, 0, src, 0, _chip_index))
    local = [(0, _chip_index, 0, _chip_index)]
    return _exchange(name, [v], [jax.ShapeDtypeStruct(v.shape, v.dtype)], remote, local)[0]


def _pair_gather(name, v):
    remote = [((0, 0, 1), 0, lambda x, y, c: None, 0, lambda x, y, c: c)]
    local = [(0, lambda x, y, c: None, 0, lambda x, y, c: c)]
    return _exchange(name, [v], [jax.ShapeDtypeStruct((2,) + v.shape, v.dtype)], remote, local)[0]


def _sum_leading(name, v, tm=None):
    n, r, w = v.shape
    tm = _pick(r, (256, 128, 64, 32, 16, 8)) if tm is None else tm

    def body(v_ref, o_ref):
        acc = v_ref[0]
        for k in range(1, n):
            acc = acc + v_ref[k]
        o_ref[...] = acc

    return pl.pallas_call(
        body, name=name, grid=(r // tm,), in_specs=[pl.BlockSpec((n, tm, w), lambda i: (0, i, 0))],
        out_specs=pl.BlockSpec((tm, w), lambda i: (i, 0)), out_shape=jax.ShapeDtypeStruct((r, w), F32),
        compiler_params=_cparams(("parallel",)),
    )(v)


def _add2(name, a, b, tm=None):
    r, w = a.shape
    tm = _pick(r, (256, 128, 64, 32, 16, 8)) if tm is None else tm

    def body(a_ref, b_ref, o_ref):
        o_ref[...] = a_ref[...] + b_ref[...]

    spec = pl.BlockSpec((tm, w), lambda i: (i, 0))
    return pl.pallas_call(
        body, name=name, grid=(r // tm,), in_specs=[spec, spec], out_specs=spec,
        out_shape=jax.ShapeDtypeStruct((r, w), F32), compiler_params=_cparams(("parallel",)),
    )(a, b)


def _ada_fwd(call_all, w_shard):
    def body(c_ref, w_ref, o_ref):
        cv = c_ref[...]
        o_ref[...] = jnp.dot(cv * _sigmoid(cv), w_ref[...], preferred_element_type=F32,
                             precision=lax.Precision.HIGHEST)

    n = w_shard.shape[1]
    return pl.pallas_call(
        body, name="ada_fwd", out_shape=jax.ShapeDtypeStruct((call_all.shape[0], n), F32),
        compiler_params=pltpu.CompilerParams(vmem_limit_bytes=VMEM_LIMIT),
    )(call_all, w_shard)


def _ada_bwd(call_all, dada):
    def body(c_ref, d_ref, o_ref):
        cv = c_ref[...]
        o_ref[...] = lax.dot_general(cv * _sigmoid(cv), d_ref[...], (((0,), (0,)), ((), ())),
                                     preferred_element_type=F32, precision=lax.Precision.HIGHEST)

    return pl.pallas_call(
        body, name="ada_bwd", out_shape=jax.ShapeDtypeStruct((call_all.shape[1], dada.shape[1]), F32),
        compiler_params=pltpu.CompilerParams(vmem_limit_bytes=VMEM_LIMIT),
    )(call_all, dada)


def _adamw(name, w, g, m, v):
    r, wd = w.shape
    tm = _pick(r, (256, 128, 64, 32, 16, 8))
    bc1 = 1.0 - ADAM_B1 ** ADAM_STEP
    bc2 = 1.0 - ADAM_B2 ** ADAM_STEP

    def body(w_ref, g_ref, m_ref, v_ref, d_ref, mo_ref, vo_ref):
        gv = g_ref[...]
        mn = ADAM_B1 * m_ref[...] + (1.0 - ADAM_B1) * gv
        vn = ADAM_B2 * v_ref[...] + (1.0 - ADAM_B2) * (gv * gv)
        d_ref[...] = -ADAM_LR * ((mn / bc1) / (jnp.sqrt(vn / bc2) + ADAM_EPS) + ADAM_WD * w_ref[...])
        mo_ref[...] = mn
        vo_ref[...] = vn

    spec = pl.BlockSpec((tm, wd), lambda i: (i, 0))
    return pl.pallas_call(
        body, name=name, grid=(r // tm,), in_specs=[spec] * 4, out_specs=[spec] * 3,
        out_shape=[jax.ShapeDtypeStruct((r, wd), F32)] * 3, compiler_params=_cparams(("parallel",)),
    )(w, g, m, v)


def _rope_tables(positions):
    half = ROPE_DIMS // 2
    freqs = ROPE_THETA ** (-jnp.arange(0, ROPE_DIMS, 2, dtype=F32) / ROPE_DIMS)
    ang = positions.astype(F32).reshape(-1, 1) * freqs
    cos, sin = jnp.cos(ang), jnp.sin(ang)
    T = ang.shape[0]
    one = jnp.ones((T, HD - ROPE_DIMS), F32)
    zero = jnp.zeros((T, HD - ROPE_DIMS), F32)
    zh = jnp.zeros((T, half), F32)
    c64 = jnp.concatenate([cos, cos, one], axis=1)
    s1 = jnp.concatenate([zh, sin, zero], axis=1)
    s2 = jnp.concatenate([-sin, zh, zero], axis=1)
    rep = lambda t: jnp.concatenate([t] * (LANES // HD), axis=1)
    return rep(c64), rep(s1), rep(s2)


def _local_step(x, loss_target, positions, ada, w_qkv, w_f, w_out, w_up, conv_w8, w_down,
                b_fgate, gn, ln1_g, ln1_b, conv_b, ln2_g, ln2_b):
    T = x.shape[0]
    nbl = T // S
    nha = WA // HD
    sv = lambda k: ada[:, k:k + 1, :]
    sh_a, sc_a, g_a, sh_f, sc_f, g_f = (sv(k) for k in range(6))
    rope = _rope_tables(positions)
    neg_rope = (rope[0], -rope[1], -rope[2])
    gseg = jnp.asarray(np.kron(np.eye(min(256, 2 * WA) // HD), np.ones((HD, HD))), BF16)
    bias = jnp.asarray(_dil_bias(TQ))
    bf_pad = jnp.zeros((1, FPAD), F32).at[:, :nha].set(b_fgate)

    qkv, fa = _mod_mm("qkv_proj", x, sc_a, sh_a, (w_qkv, w_f), (BF16, F32), rope=rope, rope_secs=(3, 4))
    fcum = _cumsum_seq("fgate_fwd", fa, lambda f, b: _log_sigmoid(f + b), lambda cum, f, b: cum,
                       reverse=False, extra=(bf_pad,))[0]
    fcol = jnp.repeat(fcum[:, :nha], HD, axis=1)
    frow = fcum[:, :nha].reshape(nbl, S, nha).transpose(0, 2, 1)[:, :, None, :]
    oa, lse_a = _attn_fwd("fox_fwd", qkv, (0, 1, 2), True, fcol=fcol, frow=frow)
    ob, lse_b = _attn_fwd("dil_fwd", qkv, (3, 4, 5), False, bias=bias)

    def mix_fn(i, oav, obv, xv, gav, gn_ref, g_ref, wo_ref, l1g_ref, l1b_ref):
        o = jnp.concatenate([oav, obv], axis=1)
        rs = lax.rsqrt(_head_mean(o * o, g_ref) + RMS_EPS)
        merged = (o * rs * gn_ref[...]).astype(BF16)
        mix = _dot(merged, wo_ref[...])
        x1, _, _ = _ln_fwd(ALPHA * xv + gav * mix, l1g_ref[...], l1b_ref[...])
        return merged, mix, x1

    merged, mix, x1 = _rowwise("mix_out", mix_fn, T, 256, tiles=(oa, ob, x), seqvecs=(g_a,),
                               consts=(gn, gseg, w_out, ln1_g, ln1_b),
                               outs=((2 * WA, BF16), (D, F32), (D, F32)))
    u = _mod_mm("ffn_up", x1, sc_f, sh_f, (w_up,), (F32,))[0]

    def conv_y(i, uv, prev, cw_ref, cb_ref, tm):
        first = (i * tm) % S == 0
        s1, s2 = _conv_taps(uv, prev, first)
        y = cb_ref[...] + cw_ref[0:1, :] * s2 + cw_ref[1:2, :] * s1 + cw_ref[2:3, :] * uv
        return y, s1, s2

    tmc = 128

    def gate_fn(i, uv, prev, cw_ref, cb_ref):
        y, _, _ = conv_y(i, uv, prev, cw_ref, cb_ref, tmc)
        a, g = y[:, :DFF], y[:, DFF:]
        return g * _sigmoid(g) * a

    act = _rowwise("conv_gate", gate_fn, T, tmc, tiles=(u,), halos=((u, -1),), consts=(conv_w8, conv_b),
                   outs=((DFF, BF16),))[0]

    def down_fn(i, actv, x1v, tgt, gfv, wd_ref, g2_ref, b2_ref):
        ffn = _dot(actv, wd_ref[...])
        y, n2, rstd = _ln_fwd(ALPHA * x1v + gfv * ffn, g2_ref[...], b2_ref[...])
        err = y - tgt
        dy = err * (1.0 / D)
        dr2 = _ln_bwd(dy, n2, rstd, g2_ref[...])
        return (dr2, gfv * dr2, _rsum8(err * err), _rsum8(dy * n2), _rsum8(dy), _rsum8(dr2 * ffn))

    dr2, dffn, loss_acc, d_ln2g, d_ln2b, d_gf = _rowwise(
        "ffn_down_loss", down_fn, T, 256, tiles=(act, x1, loss_target), seqvecs=(g_f,),
        consts=(w_down, ln2_g, ln2_b), outs=((D, F32), (D, BF16)), accs=(D, D, D), seqaccs=(D,))

    dact = _mm_nt("dact", dffn, w_down)

    def gate_bwd_fn(i, uv, dav, prev, cw_ref, cb_ref):
        y, s1, s2 = conv_y(i, uv, prev, cw_ref, cb_ref, tmc)
        a, g = y[:, :DFF], y[:, DFF:]
        sg = _sigmoid(g)
        dyc = jnp.concatenate([dav * (g * sg), dav * a * (sg * (1.0 + g * (1.0 - sg)))], axis=1)
        return dyc, _rsum8(dyc), _rsum8(dyc * s2), _rsum8(dyc * s1), _rsum8(dyc * uv)

    dyc, d_cb, d_cw0, d_cw1, d_cw2 = _rowwise(
        "gate_bwd", gate_bwd_fn, T, tmc, tiles=(u, dact), halos=((u, -1),), consts=(conv_w8, conv_b),
        outs=((2 * DFF, F32),), accs=(2 * DFF,) * 4)

    def conv_bwd_fn(i, dv, nxt, cw_ref):
        last = ((i + 1) * tmc) % S == 0
        u1, u2 = _conv_taps_up(dv, nxt, last)
        return cw_ref[2:3, :] * dv + cw_ref[1:2, :] * u1 + cw_ref[0:1, :] * u2

    du = _rowwise("conv_bwd", conv_bwd_fn, T, tmc, tiles=(dyc,), halos=((dyc, 1),), consts=(conv_w8,),
                  outs=((2 * DFF, BF16),))[0]
    dh2 = _mm_nt("dh2", du, w_up)
    g_w_down = _mm_tn("dw_down", act, dffn)
    g_w_up = _mm_tn("dw_up", x1, du, mod=(sc_f, sh_f))

    def ln1_bwd_fn(i, dr2v, dh2v, xv, mixv, x1v, scfv, gav, l1g_ref):
        dx1 = ALPHA * dr2v + dh2v * (1.0 + scfv)
        _, n1, rstd = _ln_fwd(ALPHA * xv + gav * mixv, l1g_ref[...], 0.0)
        dr1 = _ln_bwd(dx1, n1, rstd, l1g_ref[...])
        return (dr1, gav * dr1, _rsum8(dx1 * n1), _rsum8(dx1),
                _rsum8(dh2v * x1v), _rsum8(dh2v), _rsum8(dr1 * mixv))

    dr1, dmix, d_ln1g, d_ln1b, d_scf, d_shf, d_ga = _rowwise(
        "ln1_bwd", ln1_bwd_fn, T, 256, tiles=(dr2, dh2, x, mix, x1), seqvecs=(sc_f, g_a), consts=(ln1_g,),
        outs=((D, F32), (D, BF16)), accs=(D, D), seqaccs=(D, D, D))

    dmerged = _mm_nt("dmerged", dmix, w_out)
    g_w_out = _mm_tn("dw_out", merged, dmix)

    def hn_bwd_fn(i, dmv, oav, obv, gn_ref, g_ref):
        o = jnp.concatenate([oav, obv], axis=1)
        rs = lax.rsqrt(_head_mean(o * o, g_ref) + RMS_EPS)
        nrm = o * rs
        dn = dmv * gn_ref[...]
        do = rs * (dn - nrm * _head_mean(dn * nrm, g_ref))
        return do, _rsum8(dmv * nrm)

    do, d_gn = _rowwise("headnorm_bwd", hn_bwd_fn, T, 256, tiles=(dmerged, oa, ob), consts=(gn, gseg),
                        outs=((2 * WA, F32),), accs=(2 * WA,))
    dqa, dka, dva, dfrow, dfq = _attn_bwd("fox_bwd", qkv, (0, 1, 2), oa, do, 0, lse_a, True, fcol=fcol, frow=frow)
    dqb, dkb, dvb = _attn_bwd("dil_bwd", qkv, (3, 4, 5), ob, do, 1, lse_b, False, bias=bias)
    dftok = dfrow[:, :, 0, :].transpose(0, 2, 1).reshape(T, nha) + dfq[:, ::HD]
    dftok = jnp.pad(dftok, ((0, 0), (0, FPAD - nha)))
    dfa, d_bf = _cumsum_seq("fgate_bwd", dftok, lambda d, f, b: d,
                            lambda cum, d, f, b: cum * _sigmoid(-(f + b)),
                            reverse=True, extra=(fa, bf_pad), n_acc=1)

    def dz_fn(i, a0, a1, a2, b0, b1, b2, fv, cv, s1v, s2v):
        ct, s1t, s2t = (_tile_lanes(t, WA) for t in (cv, s1v, s2v))
        return jnp.concatenate([a0, a1, a2, _rope(b0, ct, s1t, s2t), _rope(b1, ct, s1t, s2t), b2, fv], axis=1)

    dz = _rowwise("dz_pack", dz_fn, T, 256, tiles=(dqa, dka, dva, dqb, dkb, dvb, dfa) + neg_rope,
                  outs=((6 * WA + FPAD, BF16),))[0]
    w_cat = jnp.concatenate([w_qkv, w_f], axis=1)
    dh1 = _mm_nt("dh1", dz, w_cat)
    g_w_cat = _mm_tn("dw_in", x, dz, mod=(sc_a, sh_a))

    def dx_fn(i, dr1v, dh1v, xv, scav):
        return ALPHA * dr1v + dh1v * (1.0 + scav), _rsum8(dh1v * xv), _rsum8(dh1v)

    grad_x, d_sca, d_sha = _rowwise("dx_out", dx_fn, T, 256, tiles=(dr1, dh1, x), seqvecs=(sc_a,),
                                    outs=((D, F32),), seqaccs=(D, D))

    row0 = lambda a: a[..., 0, :]
    d_ada = jnp.stack([row0(d_sha), row0(d_sca), row0(d_ga), row0(d_shf), row0(d_scf), row0(d_gf)], axis=1)
    d_cw = jnp.stack([row0(d_cw0), row0(d_cw1), row0(d_cw2)], axis=0)
    loss_part = (0.5 / D) * jnp.sum(loss_acc[0])
    small = dict(b_fgate=row0(d_bf)[:nha], gn=row0(d_gn), ln1_g=row0(d_ln1g), ln1_b=row0(d_ln1b),
                 conv_b=row0(d_cb), ln2_g=row0(d_ln2g), ln2_b=row0(d_ln2b))
    big = dict(w_cat=g_w_cat, w_out=g_w_out, w_up=g_w_up, conv_w=d_cw, w_down=g_w_down)
    return loss_part, grad_x, d_ada, small, big


def _rows_of(n, w=None):
    return -(-n // (D if w is None else w))


def _as_rows(v):
    w = D
    k = v.shape[0]
    flat = v.reshape(k, -1)
    rows = _rows_of(_rows_of(flat.shape[1], w), SUBLANES) * SUBLANES
    flat = jnp.pad(flat, ((0, 0), (0, rows * w - flat.shape[1])))
    return flat.reshape(k, rows, w)


def kernel(x, c, positions, w_ada, b_ada, w_in, b_fgate, gn_a, gn_b, w_out, ln1_g, ln1_b, w_up, conv_w, conv_b, w_down, ln2_g, ln2_b, loss_target, m_w_ada, m_b_ada, m_w_in, m_b_fgate, m_gn_a, m_gn_b, m_w_out, m_ln1_g, m_ln1_b, m_w_up, m_conv_w, m_conv_b, m_w_down, m_ln2_g, m_ln2_b, v_w_ada, v_b_ada, v_w_in, v_b_fgate, v_gn_a, v_gn_b, v_w_out, v_ln1_g, v_ln1_b, v_w_up, v_conv_w, v_conv_b, v_w_down, v_ln2_g, v_ln2_b):
    mx, my, mc = lax.axis_index("x"), lax.axis_index("y"), lax.axis_index("c")
    dev = _dev_index(mx, my, mc)
    chip = _chip_index(mx, my, mc)
    nbl = x.shape[0]
    T = nbl * S
    nha = WA // HD
    d_in = w_in.shape[2] * NCHIP
    n_ada = w_ada.shape[2]

    c_pad = jnp.zeros((SUBLANES, D), F32).at[:nbl].set(c)
    c_all = _all_gather8("gather_c", c_pad)[:, :nbl].reshape(NDEV * nbl, D)
    ada_part = _ada_fwd(c_all, w_ada[0])
    ada_blocks = _all_gather8("gather_ada", ada_part)
    ada_all = jnp.concatenate([ada_blocks[2 * k] for k in range(NCHIP)], axis=1) + b_ada
    ada = lax.dynamic_slice_in_dim(ada_all, dev * nbl, nbl, axis=0).reshape(nbl, 6, D)

    w_in_sh = jnp.pad(w_in[0].astype(BF16), ((0, 0), (0, _rows_of(w_in.shape[2], LANES) * LANES - w_in.shape[2])))
    cw_sh = jnp.pad(conv_w[0], ((0, SUBLANES - conv_w.shape[1]), (0, 0)))
    g_in, g_out, g_up, g_cw, g_down = _gather_chips(
        "gather_w", [w_in_sh, w_out[0].astype(BF16), w_up[0].astype(BF16), cw_sh, w_down[0].astype(BF16)])
    w_in_full = jnp.concatenate([g_in[k][:, :w_in.shape[2]] for k in range(NCHIP)], axis=1)
    w_qkv = jnp.concatenate([w_in_full[:, :3 * WA], w_in_full[:, 3 * WA + nha:]], axis=1)
    w_f = jnp.pad(w_in_full[:, 3 * WA:3 * WA + nha], ((0, 0), (0, FPAD - nha)))
    w_out_full = g_out.reshape(NCHIP * w_out.shape[1], D)
    w_up_full = jnp.concatenate([g_up[k] for k in range(NCHIP)], axis=1)
    conv_w8 = jnp.concatenate([g_cw[k] for k in range(NCHIP)], axis=1)
    w_down_full = g_down.reshape(NCHIP * w_down.shape[1], D)

    gn = jnp.concatenate([gn_a, gn_b], axis=1)
    loss_part, grad_x, d_ada, small, big = _local_step(
        x.reshape(T, D), loss_target.reshape(T, D), positions, ada, w_qkv, w_f, w_out_full, w_up_full, conv_w8,
        w_down_full, b_fgate, gn, ln1_g, ln1_b, conv_b, ln2_g, ln2_b)

    def row_pad(v, rows):
        flat = v.reshape(-1)
        return jnp.pad(flat, (0, rows * D - flat.shape[0]))

    n_cb = _rows_of(2 * DFF)
    small_flat = jnp.concatenate([
        row_pad(small["b_fgate"], 1), row_pad(small["gn"], 1), row_pad(small["ln1_g"], 1),
        row_pad(small["ln1_b"], 1), row_pad(small["ln2_g"], 1), row_pad(small["ln2_b"], 1),
        row_pad(jnp.full((1,), loss_part, F32), 1), row_pad(small["conv_b"], n_cb)])
    n_small = _rows_of(small_flat.shape[0], SUBLANES * D) * SUBLANES
    small_rows = jnp.pad(small_flat, (0, n_small * D - small_flat.shape[0])).reshape(n_small, D)
    ada_rows = jnp.pad(d_ada.reshape(nbl, 6, D), ((0, 0), (0, SUBLANES - 6), (0, 0))).reshape(nbl * SUBLANES, D)
    gathered = _all_gather8("gather_small", jnp.concatenate([small_rows, ada_rows], axis=0))
    red = _sum_leading("sum_small", gathered, tm=SUBLANES)
    g_b_fgate = red[0:1, :nha]
    g_gn = red[1:2, :2 * WA]
    g_ln1_g, g_ln1_b, g_ln2_g, g_ln2_b = red[2:3], red[3:4], red[4:5], red[5:6]
    loss = red[6, 0]
    g_conv_b = red[7:7 + n_cb].reshape(1, -1)[:, :2 * DFF]
    g_b_ada = _add2("sum_b_ada", red[n_small:n_small + SUBLANES], red[n_small + SUBLANES:n_small + 2 * SUBLANES],
                    tm=SUBLANES)[:6].reshape(1, 6 * D)
    dada_all = gathered[:, n_small:].reshape(NDEV, nbl, SUBLANES, D)[:, :, :6].reshape(NDEV * nbl, 6 * D)
    g_w_ada = _ada_bwd(c_all, lax.dynamic_slice_in_dim(dada_all, chip * n_ada, n_ada, axis=1))

    g_cat = big["w_cat"]
    g_w_in_full = jnp.concatenate([g_cat[:, :3 * WA], g_cat[:, 6 * WA:6 * WA + nha], g_cat[:, 3 * WA:6 * WA]], axis=1)
    sh_in = g_w_in_full.reshape(D, NCHIP, -1).transpose(1, 0, 2)
    sh_out = big["w_out"].reshape(NCHIP, -1, D)
    sh_up = big["w_up"].reshape(D, NCHIP, -1).transpose(1, 0, 2)
    sh_cw = big["conv_w"].reshape(conv_w.shape[1], NCHIP, -1).transpose(1, 0, 2)
    sh_down = big["w_down"].reshape(NCHIP, -1, D)
    parts = [_as_rows(t) for t in (sh_in, sh_out, sh_up, sh_cw, sh_down)]
    part_rows = [p.shape[1] for p in parts]
    packed = jnp.concatenate(parts, axis=1)
    n_rows = _rows_of(packed.shape[1], 2 * SUBLANES) * 2 * SUBLANES
    half = n_rows // 2
    packed = jnp.pad(packed, ((0, 0), (0, n_rows - packed.shape[1]), (0, 0)))
    halves = packed.reshape(NCHIP, 2, half, D)
    mine = lax.dynamic_index_in_dim(halves, mc, axis=1, keepdims=False).reshape(NCHIP * half, D)
    theirs = lax.dynamic_index_in_dim(halves, 1 - mc, axis=1, keepdims=False).reshape(NCHIP * half, D)
    from_sib = _to_sibling("pair_swap", theirs)
    pair_sum = _add2("pair_sum", mine, from_sib).reshape(NCHIP, half, D)
    by_chip = _scatter_chips("scatter_grads", pair_sum)
    my_half = _sum_leading("chip_sum", by_chip)
    both = _pair_gather("pair_share", my_half)
    shard = both.reshape(n_rows, D)

    def unpack(k, shape):
        start = sum(part_rows[:k])
        n = int(np.prod(shape))
        return shard[start:start + part_rows[k]].reshape(-1)[:n].reshape(shape)

    g_w_in = unpack(0, w_in.shape[1:])
    g_w_out = unpack(1, w_out.shape[1:])
    g_w_up = unpack(2, w_up.shape[1:])
    g_conv_w = unpack(3, conv_w.shape[1:])
    g_w_down = unpack(4, w_down.shape[1:])

    grads = dict(w_ada=g_w_ada, b_ada=g_b_ada, w_in=g_w_in, b_fgate=g_b_fgate, gn_a=g_gn[:, :WA], gn_b=g_gn[:, WA:],
                 w_out=g_w_out, ln1_g=g_ln1_g, ln1_b=g_ln1_b, w_up=g_w_up, conv_w=g_conv_w, conv_b=g_conv_b,
                 w_down=g_w_down, ln2_g=g_ln2_g, ln2_b=g_ln2_b)
    weights = dict(w_ada=w_ada, b_ada=b_ada, w_in=w_in, b_fgate=b_fgate, gn_a=gn_a, gn_b=gn_b, w_out=w_out,
                   ln1_g=ln1_g, ln1_b=ln1_b, w_up=w_up, conv_w=conv_w, conv_b=conv_b, w_down=w_down,
                   ln2_g=ln2_g, ln2_b=ln2_b)
    ms = dict(w_ada=m_w_ada, b_ada=m_b_ada, w_in=m_w_in, b_fgate=m_b_fgate, gn_a=m_gn_a, gn_b=m_gn_b,
              w_out=m_w_out, ln1_g=m_ln1_g, ln1_b=m_ln1_b, w_up=m_w_up, conv_w=m_conv_w, conv_b=m_conv_b,
              w_down=m_w_down, ln2_g=m_ln2_g, ln2_b=m_ln2_b)
    vs = dict(w_ada=v_w_ada, b_ada=v_b_ada, w_in=v_w_in, b_fgate=v_b_fgate, gn_a=v_gn_a, gn_b=v_gn_b,
              w_out=v_w_out, ln1_g=v_ln1_g, ln1_b=v_ln1_b, w_up=v_w_up, conv_w=v_conv_w, conv_b=v_conv_b,
              w_down=v_w_down, ln2_g=v_ln2_g, ln2_b=v_ln2_b)
    names = list(weights)
    big_names = ("w_ada", "w_in", "w_out", "w_up", "w_down")
    delta, new_m, new_v = {}, {}, {}
    for n in big_names:
        shp = weights[n].shape
        d, m2, v2 = _adamw("adamw_" + n, weights[n][0], grads[n].reshape(shp[1:]), ms[n][0], vs[n][0])
        delta[n], new_m[n], new_v[n] = d.reshape(shp), m2.reshape(shp), v2.reshape(shp)
    small_names = [n for n in names if n not in big_names]

    def pack_small(src):
        flats = []
        for n in small_names:
            flat = src[n].reshape(-1)
            flats.append(jnp.pad(flat, (0, _rows_of(flat.shape[0]) * D - flat.shape[0])))
        allf = jnp.concatenate(flats)
        rows = _rows_of(allf.shape[0], SUBLANES * D) * SUBLANES
        return jnp.pad(allf, (0, rows * D - allf.shape[0])).reshape(rows, D)

    sd, sm, sv_ = _adamw("adamw_small", pack_small(weights), pack_small(grads), pack_small(ms), pack_small(vs))
    off = 0
    for n in small_names:
        shp = weights[n].shape
        cnt = int(np.prod(shp))
        r = _rows_of(cnt)
        for dst, src in ((delta, sd), (new_m, sm), (new_v, sv_)):
            dst[n] = src[off:off + r].reshape(-1)[:cnt].reshape(shp)
        off += r

    out_g = {n: grads[n].reshape(weights[n].shape) for n in names}
    return (loss, grad_x.reshape(x.shape), *[out_g[n] for n in names], *[delta[n] for n in names],
            *[new_m[n] for n in names], *[new_v[n] for n in names])
```

```python
import functools
import math

import numpy as np
import jax
import jax.numpy as jnp
from jax import lax
from jax.experimental import pallas as pl
from jax.experimental.pallas import tpu as pltpu

F32 = jnp.float32
BF16 = jnp.bfloat16

D = 1024
S = 4096
HD = 64
WA = 512
DFF = 2816
NCHIP = 4
NDEV = 8
PATTERNS = ((128, 1), (512, 4), (2048, 16))
ROPE_THETA = 500000.0
ROPE_DIMS = HD // 4
ALPHA = (2.0 * 1) ** 0.25
LN_EPS = 1e-5
RMS_EPS = 1e-6
ADAM_LR = 0.001
ADAM_B1 = 0.9
ADAM_B2 = 0.999
ADAM_EPS = 1e-08
ADAM_WD = 0.01
ADAM_STEP = 10

LANES = 128
SUBLANES = 8
TQ = 256
FPAD = LANES
NEG = -1e30
VMEM_LIMIT = 56 * 1024 * 1024
MESH = pl.DeviceIdType.MESH


def _cparams(sem):
    return pltpu.CompilerParams(dimension_semantics=sem, vmem_limit_bytes=VMEM_LIMIT)


def _pick(n, cands):
    for c in cands:
        if n % c == 0:
            return c
    return n


def _rsum8(v):
    tm, w = v.shape
    return jnp.sum(v.reshape(tm // SUBLANES, SUBLANES, w), axis=0)


def _sigmoid(x):
    return 1.0 / (1.0 + jnp.exp(-x))


def _dot(a, b):
    return jnp.dot(a, b, preferred_element_type=F32)


def _dot_nt(a, b):
    return lax.dot_general(a, b, (((1,), (1,)), ((), ())), preferred_element_type=F32)


def _dot_tn(a, b):
    return lax.dot_general(a, b, (((0,), (0,)), ((), ())), preferred_element_type=F32)


def _rowwise(name, fn, T, tm, *, tiles=(), halos=(), seqvecs=(), consts=(), outs=(), accs=(), seqaccs=(),
             seq_len=None):
    seq_len = S if seq_len is None else seq_len
    nb = T // tm
    spb = max(seq_len // tm, 1)
    nseq = max(T // seq_len, 1)
    n8 = T // SUBLANES
    r8 = tm // SUBLANES
    in_specs, args = [], []
    for a in tiles:
        in_specs.append(pl.BlockSpec((tm, a.shape[1]), lambda i: (i, 0)))
        args.append(a)
    for a, direction in halos:
        if direction < 0:
            idx = lambda i: (jnp.maximum(i * r8 - 1, 0), 0)
        else:
            idx = lambda i: (jnp.minimum((i + 1) * r8, n8 - 1), 0)
        in_specs.append(pl.BlockSpec((SUBLANES, a.shape[1]), idx))
        args.append(a)
    for a in seqvecs:
        in_specs.append(pl.BlockSpec((1, 1, a.shape[2]), lambda i: (i // spb, 0, 0)))
        args.append(a)
    for a in consts:
        in_specs.append(pl.BlockSpec(a.shape, lambda i, nd=a.ndim: (0,) * nd))
        args.append(a)
    out_shape, out_specs = [], []
    for w, dt in outs:
        out_shape.append(jax.ShapeDtypeStruct((T, w), dt))
        out_specs.append(pl.BlockSpec((tm, w), lambda i: (i, 0)))
    for w in accs:
        out_shape.append(jax.ShapeDtypeStruct((SUBLANES, w), F32))
        out_specs.append(pl.BlockSpec((SUBLANES, w), lambda i: (0, 0)))
    for w in seqaccs:
        out_shape.append(jax.ShapeDtypeStruct((nseq, SUBLANES, w), F32))
        out_specs.append(pl.BlockSpec((1, SUBLANES, w), lambda i: (i // spb, 0, 0)))
    n_t, n_h, n_s, n_c = len(tiles), len(halos), len(seqvecs), len(consts)
    n_o, n_a, n_sa = len(outs), len(accs), len(seqaccs)

    def body(*refs):
        i = pl.program_id(0)
        ins = refs[:n_t + n_h + n_s + n_c]
        orefs = refs[n_t + n_h + n_s + n_c:]
        vals = [r[...] for r in ins[:n_t + n_h]]
        vals += [r[0] for r in ins[n_t + n_h:n_t + n_h + n_s]]
        vals += list(ins[n_t + n_h + n_s:])
        res = fn(i, *vals)
        if not isinstance(res, (tuple, list)):
            res = (res,)
        for k in range(n_o):
            orefs[k][...] = res[k].astype(orefs[k].dtype)
        for k in range(n_a):
            r = orefs[n_o + k]

            @pl.when(i == 0)
            def _():
                r[...] = jnp.zeros_like(r)

            r[...] += res[n_o + k]

            @pl.when(i == nb - 1)
            def _():
                r[...] = jnp.broadcast_to(jnp.sum(r[...], axis=0, keepdims=True), r.shape)
        for k in range(n_sa):
            r = orefs[n_o + n_a + k]

            @pl.when(i % spb == 0)
            def _():
                r[...] = jnp.zeros_like(r)

            r[0] += res[n_o + n_a + k]

            @pl.when(i % spb == spb - 1)
            def _():
                r[0] = jnp.broadcast_to(jnp.sum(r[0], axis=0, keepdims=True), r.shape[1:])

    sem = ("arbitrary",) if (n_a or n_sa) else ("parallel",)
    res = pl.pallas_call(
        body, name=name, grid=(nb,), in_specs=in_specs, out_specs=out_specs, out_shape=out_shape,
        compiler_params=_cparams(sem),
    )(*args)
    return res


def _ln_fwd(r, g, b):
    mu = jnp.mean(r, axis=-1, keepdims=True)
    xc = r - mu
    var = jnp.mean(xc * xc, axis=-1, keepdims=True)
    rstd = lax.rsqrt(var + LN_EPS)
    n = xc * rstd
    return n * g + b, n, rstd


def _ln_bwd(dy, n, rstd, g):
    dn = dy * g
    return rstd * (dn - jnp.mean(dn, axis=-1, keepdims=True) - n * jnp.mean(dn * n, axis=-1, keepdims=True))


def _head_mean(t, g_ref):
    gw = g_ref.shape[0]
    hi = t.astype(BF16)
    lo = (t - hi.astype(F32)).astype(BF16)
    g = g_ref[...]
    parts = []
    for c in range(t.shape[1] // gw):
        sl = slice(c * gw, (c + 1) * gw)
        parts.append(_dot(hi[:, sl], g) + _dot(lo[:, sl], g))
    out = parts[0] if len(parts) == 1 else jnp.concatenate(parts, axis=1)
    return out * (1.0 / HD)


def _rope(z, c, s1, s2):
    w = z.shape[1]
    half = ROPE_DIMS // 2
    return z * c + pltpu.roll(z, half, 1) * s1 + pltpu.roll(z, w - half, 1) * s2


def _tile_lanes(t, w):
    reps = w // t.shape[1]
    return t if reps == 1 else jnp.concatenate([t] * reps, axis=1)


def _conv_taps(ext, prev, first):
    tm = ext.shape[0]
    prev = jnp.where(first, jnp.zeros_like(prev), prev)
    r8 = lax.broadcasted_iota(jnp.int32, (SUBLANES, 1), 0)
    top = ext[0:SUBLANES]
    s1_top = jnp.where(r8 < 1, pltpu.roll(prev, 1, 0), pltpu.roll(top, 1, 0))
    s2_top = jnp.where(r8 < 2, pltpu.roll(prev, 2, 0), pltpu.roll(top, 2, 0))
    s1 = jnp.concatenate([s1_top, pltpu.roll(ext, 1, 0)[SUBLANES:]], axis=0)
    s2 = jnp.concatenate([s2_top, pltpu.roll(ext, 2, 0)[SUBLANES:]], axis=0)
    return s1, s2


def _conv_taps_up(ext, nxt, last):
    tm = ext.shape[0]
    nxt = jnp.where(last, jnp.zeros_like(nxt), nxt)
    r8 = lax.broadcasted_iota(jnp.int32, (SUBLANES, 1), 0)
    bot = ext[tm - SUBLANES:tm]
    u1_bot = jnp.where(r8 >= 7, pltpu.roll(nxt, 7, 0), pltpu.roll(bot, 7, 0))
    u2_bot = jnp.where(r8 >= 6, pltpu.roll(nxt, 6, 0), pltpu.roll(bot, 6, 0))
    u1 = jnp.concatenate([pltpu.roll(ext, tm - 1, 0)[:tm - SUBLANES], u1_bot], axis=0)
    u2 = jnp.concatenate([pltpu.roll(ext, tm - 2, 0)[:tm - SUBLANES], u2_bot], axis=0)
    return u1, u2


def _mm_nt(name, a, w, tm=256):
    T = a.shape[0]
    n = w.shape[0]
    ch = _pick(n, (512, 256, 128))

    def fn(i, av, w_ref):
        ab = av.astype(BF16)
        parts = [_dot_nt(ab, w_ref[c * ch:(c + 1) * ch, :]) for c in range(n // ch)]
        return parts[0] if len(parts) == 1 else jnp.concatenate(parts, axis=1)

    return _rowwise(name, fn, T, tm, tiles=(a,), consts=(w,), outs=((n, F32),))[0]


def _mm_tn(name, a, b, *, mod=None, tt=512):
    T, k1 = a.shape
    k2 = b.shape[1]
    t1 = k1 if k1 <= 1536 else _pick(k1, (1408, 1024, 512, 256, 128))
    t2 = k2 if k2 <= 1536 else _pick(k2, (1408, 1024, 640, 512, 256, 128))
    tt = min(tt, S)
    spb = S // tt

    def body(*refs):
        if mod is not None:
            a_ref, sc_ref, sh_ref, b_ref, o_ref = refs
        else:
            a_ref, b_ref, o_ref = refs
        t = pl.program_id(2)

        @pl.when(t == 0)
        def _():
            o_ref[...] = jnp.zeros_like(o_ref)

        av = a_ref[...]
        if mod is not None:
            av = av * (1.0 + sc_ref[0]) + sh_ref[0]
        o_ref[...] += _dot_tn(av.astype(BF16), b_ref[...].astype(BF16))

    in_specs = [pl.BlockSpec((tt, t1), lambda p, q, t: (t, p))]
    args = [a]
    if mod is not None:
        for v in mod:
            in_specs.append(pl.BlockSpec((1, 1, t1), lambda p, q, t: (t // spb, 0, p)))
            args.append(v)
    in_specs.append(pl.BlockSpec((tt, t2), lambda p, q, t: (t, q)))
    args.append(b)
    return pl.pallas_call(
        body, name=name, grid=(k1 // t1, k2 // t2, T // tt), in_specs=in_specs,
        out_specs=pl.BlockSpec((t1, t2), lambda p, q, t: (p, q)),
        out_shape=jax.ShapeDtypeStruct((k1, k2), F32),
        compiler_params=_cparams(("parallel", "parallel", "arbitrary")),
    )(*args)


def _mod_mm(name, x, sc, sh, ws, out_dtypes, rope=None, rope_secs=(), tm=256):
    T = x.shape[0]
    nw = len(ws)

    def fn(i, xv, *rest):
        if rope is not None:
            cv, s1v, s2v = rest[:3]
            rest = rest[3:]
        scv, shv = rest[:2]
        w_refs = rest[2:]
        h = (xv * (1.0 + scv) + shv).astype(BF16)
        res = []
        for k, w_ref in enumerate(w_refs):
            n = w_ref.shape[1]
            ch = WA if (k == 0 and rope is not None) else _pick(n, (512, 256, 128))
            parts = []
            for c in range(n // ch):
                z = _dot(h, w_ref[:, c * ch:(c + 1) * ch])
                if k == 0 and c in rope_secs:
                    z = _rope(z, _tile_lanes(cv, ch), _tile_lanes(s1v, ch), _tile_lanes(s2v, ch))
                parts.append(z.astype(out_dtypes[k]))
            res.append(parts[0] if len(parts) == 1 else jnp.concatenate(parts, axis=1))
        return tuple(res)

    tiles = (x,) + (tuple(rope) if rope is not None else ())
    outs = tuple((w.shape[1], dt) for w, dt in zip(ws, out_dtypes))
    return _rowwise(name, fn, T, tm, tiles=tiles, seqvecs=(sc, sh), consts=tuple(ws), outs=outs)


def _tri(tb, lower):
    r = lax.broadcasted_iota(jnp.int32, (tb, tb), 0)
    c = lax.broadcasted_iota(jnp.int32, (tb, tb), 1)
    return jnp.where((r >= c) if lower else (r <= c), 1.0, 0.0).astype(F32)


def _cumsum_seq(name, v, fn_in, fn_out, reverse, extra=(), n_acc=0, tb=512):
    T = v.shape[0]
    tb = min(tb, S)
    nbs = S // tb
    nseq = T // S

    def blk(b, j):
        return (b * nbs + (nbs - 1 - j if reverse else j), 0)

    def body(*refs):
        v_ref = refs[0]
        e_refs = refs[1:1 + len(extra)]
        o_ref = refs[1 + len(extra)]
        acc_refs = refs[2 + len(extra):2 + len(extra) + n_acc]
        carry = refs[-1]
        b, j = pl.program_id(0), pl.program_id(1)

        @pl.when(j == 0)
        def _():
            carry[...] = jnp.zeros_like(carry)

        ev = [e[...] for e in e_refs]
        xin = fn_in(v_ref[...], *ev)
        cum = jnp.dot(_tri(tb, not reverse), xin, preferred_element_type=F32,
                      precision=lax.Precision.HIGHEST) + carry[0:1, :]
        carry[...] = carry[...] + jnp.sum(xin, axis=0, keepdims=True)
        out = fn_out(cum, v_ref[...], *ev)
        o_ref[...] = out
        for a in acc_refs:
            @pl.when((b == 0) & (j == 0))
            def _():
                a[...] = jnp.zeros_like(a)

            a[...] += _rsum8(out)

            @pl.when((b == nseq - 1) & (j == nbs - 1))
            def _():
                a[...] = jnp.broadcast_to(jnp.sum(a[...], axis=0, keepdims=True), a.shape)

    in_specs = [pl.BlockSpec((tb, FPAD), blk)]
    for e in extra:
        if e.shape[0] == T:
            in_specs.append(pl.BlockSpec((tb, FPAD), blk))
        else:
            in_specs.append(pl.BlockSpec(e.shape, lambda b, j: (0, 0)))
    out_shape = [jax.ShapeDtypeStruct((T, FPAD), F32)] + [jax.ShapeDtypeStruct((SUBLANES, FPAD), F32)] * n_acc
    out_specs = [pl.BlockSpec((tb, FPAD), blk)] + [pl.BlockSpec((SUBLANES, FPAD), lambda b, j: (0, 0))] * n_acc
    return pl.pallas_call(
        body, name=name, grid=(nseq, nbs), in_specs=in_specs, out_specs=out_specs, out_shape=out_shape,
        scratch_shapes=[pltpu.VMEM((SUBLANES, FPAD), F32)],
        compiler_params=_cparams(("arbitrary", "arbitrary")),
    )(v, *extra)


def _log_sigmoid(x):
    return jnp.minimum(x, 0.0) - jnp.log(1.0 + jnp.exp(-jnp.abs(x)))


def _dil_bias(tq):
    max_win = max(w for w, _ in PATTERNS)
    nd = (max_win + tq - 1) // tq + 1
    qi = np.arange(tq)[:, None]
    kj = np.arange(tq)[None, :]
    tabs = []
    for dlt in range(nd):
        dist = dlt * tq + qi - kj
        mult = np.zeros((tq, tq), np.float64)
        for win, dil in PATTERNS:
            mult += (dist >= 0) & (dist % dil == 0) & (dist // dil <= win // dil)
        tabs.append(np.where(mult > 0, np.log(np.maximum(mult, 1.0)), NEG))
    return np.stack(tabs).astype(np.float32)


def _fold_operands(fcum, nha):
    T = fcum.shape[0]

    def split3(f):
        hi = lax.reduce_precision(f, 8, 7)
        mid = lax.reduce_precision(f - hi, 8, 7)
        lo = lax.reduce_precision(f - hi - mid, 8, 7)
        return jnp.stack([hi, mid, lo], axis=1)

    one3 = jnp.ones((T, 3), F32)
    pad = lambda t, before, width: jnp.pad(t, ((0, 0), (before, width - before - t.shape[1])))
    eqs, eks = [], []
    for hp in range(nha // 2):
        p0, p1 = split3(fcum[:, 2 * hp]), split3(fcum[:, 2 * hp + 1])
        eqs += [pad(jnp.concatenate([p0, one3], axis=1), 0, LANES), pad(jnp.concatenate([p1, one3], axis=1), 8, LANES)]
        eks.append(pad(jnp.concatenate([one3, -p0, jnp.zeros((T, 2), F32), one3, -p1], axis=1), 0, LANES))
    return jnp.concatenate(eqs, axis=1).astype(BF16), jnp.concatenate(eks, axis=1).astype(BF16)


def _stack_heads(x2, h0, extra=None):
    z = jnp.zeros_like(x2)
    a, b = jnp.where(h0, x2, z), jnp.where(h0, z, x2)
    if extra is not None:
        a = jnp.concatenate([a, extra[:, :LANES]], axis=1)
        b = jnp.concatenate([b, extra[:, LANES:]], axis=1)
    return jnp.concatenate([a, b], axis=0)


def _attn_fwd(name, qkv, secs, fox, eq=None, ek=None, bias=None):
    T = qkv.shape[0]
    nq = S // TQ
    nbl = T // S
    hp_n = WA // LANES
    sq, sk, sv = (s * hp_n for s in secs)
    scale = HD ** -0.5
    nd = None if fox else bias.shape[0]

    def body(*refs):
        if fox:
            q_ref, k_ref, v_ref, eq_ref, ek_ref, o_ref, lse_ref = refs
        else:
            q_ref, k_ref, v_ref, b_ref, o_ref, lse_ref = refs
        i = pl.program_id(2)
        lane = lax.broadcasted_iota(jnp.int32, (1, LANES), 1)
        h0 = lane < HD
        q2 = (q_ref[...].astype(F32) * scale).astype(BF16)
        qs = _stack_heads(q2, h0, eq_ref[...] if fox else None)

        def scores(t, diag):
            off = pl.multiple_of((i - t) * TQ, TQ)
            kk = k_ref[pl.ds(off, TQ), :]
            if fox:
                kk = jnp.concatenate([kk, ek_ref[pl.ds(off, TQ), :]], axis=1)
            s = jnp.concatenate([_dot_nt(qs[:TQ], kk), _dot_nt(qs[TQ:], kk)], axis=0)
            if not fox:
                s = (s.reshape(2, TQ, TQ) + b_ref[t]).reshape(2 * TQ, TQ)
            elif diag:
                rows = lax.broadcasted_iota(jnp.int32, (2, TQ, TQ), 1).reshape(2 * TQ, TQ)
                cols = lax.broadcasted_iota(jnp.int32, (2 * TQ, TQ), 1)
                s = jnp.where(cols <= rows, s, NEG)
            return s

        def update(t, s, m, l, acc):
            off = pl.multiple_of((i - t) * TQ, TQ)
            v2 = v_ref[pl.ds(off, TQ), :]
            m_new = jnp.maximum(m, jnp.max(s, axis=1, keepdims=True))
            p = jnp.exp(s - m_new)
            a = jnp.exp(m - m_new)
            l = a * l + jnp.sum(p, axis=1, keepdims=True)
            pb = p.astype(BF16)
            acc = a * acc + jnp.concatenate([_dot(pb[:TQ], v2), _dot(pb[TQ:], v2)], axis=0)
            return m_new, l, acc

        def step(t, carry):
            s, m, l, acc = carry
            return (scores(t + 1, False),) + update(t, s, m, l, acc)

        init = (jnp.full((2 * TQ, 1), NEG, F32), jnp.zeros((2 * TQ, 1), F32), jnp.zeros((2 * TQ, LANES), F32))
        n = i + 1 if fox else jnp.minimum(i + 1, nd)
        s, m, l, acc = lax.fori_loop(0, n - 1, step, (scores(0, True),) + init)
        m, l, acc = update(n - 1, s, m, l, acc)
        on = acc / l
        o_ref[...] = jnp.where(h0, on[:TQ], on[TQ:])
        lse = jnp.broadcast_to(m + jnp.log(l), (2 * TQ, LANES))
        lse_ref[...] = jnp.concatenate([lse[:TQ], lse[TQ:]], axis=1)

    in_specs = [
        pl.BlockSpec((TQ, LANES), lambda b, hp, i: (b * nq + i, sq + hp)),
        pl.BlockSpec((S, LANES), lambda b, hp, i: (b, sk + hp)),
        pl.BlockSpec((S, LANES), lambda b, hp, i: (b, sv + hp)),
    ]
    args = [qkv, qkv, qkv]
    if fox:
        in_specs += [pl.BlockSpec((TQ, 2 * LANES), lambda b, hp, i: (b * nq + i, hp)),
                     pl.BlockSpec((S, LANES), lambda b, hp, i: (b, hp))]
        args += [eq, ek]
    else:
        in_specs.append(pl.BlockSpec(bias.shape, lambda b, hp, i: (0, 0, 0)))
        args.append(bias)
    return pl.pallas_call(
        body, name=name, grid=(nbl, hp_n, nq), in_specs=in_specs,
        out_specs=[pl.BlockSpec((TQ, LANES), lambda b, hp, i: (b * nq + i, hp)),
                   pl.BlockSpec((TQ, 2 * LANES), lambda b, hp, i: (b * nq + i, hp))],
        out_shape=[jax.ShapeDtypeStruct((T, WA), F32), jax.ShapeDtypeStruct((T, 2 * WA), F32)],
        compiler_params=_cparams(("parallel", "parallel", "arbitrary")),
    )(*args)


def _attn_bwd(name, qkv, secs, o, do, do_sec, lse, fox, eq=None, ek=None, bias=None):
    T = qkv.shape[0]
    nq = S // TQ
    nbl = T // S
    hp_n = WA // LANES
    sq, sk, sv = (s * hp_n for s in secs)
    dsec = do_sec * hp_n
    scale = HD ** -0.5
    nd = None if fox else bias.shape[0]
    kc = 2 * LANES if fox else LANES

    def body(*refs):
        if fox:
            (q_ref, k_ref, v_ref, o_ref, do_ref, lse_ref, eq_ref, ek_ref,
             dq_ref, dk_ref, dv_ref, dqe_ref, dek_ref, dl_ref) = refs
        else:
            q_ref, k_ref, v_ref, o_ref, do_ref, lse_ref, b_ref, dq_ref, dk_ref, dv_ref, dl_ref = refs
        j = pl.program_id(2)
        lane = lax.broadcasted_iota(jnp.int32, (1, LANES), 1)
        h0 = lane < HD

        @pl.when(j == 0)
        def _():
            dq_ref[...] = jnp.zeros_like(dq_ref)
            if fox:
                dqe_ref[...] = jnp.zeros_like(dqe_ref)

            def dl_step(r, c):
                off = pl.multiple_of(r * TQ, TQ)
                d2 = do_ref[pl.ds(off, TQ), :] * o_ref[pl.ds(off, TQ), :]
                z2 = jnp.zeros_like(d2)
                dl0 = jnp.sum(jnp.where(h0, d2, z2), axis=1, keepdims=True)
                dl1 = jnp.sum(jnp.where(h0, z2, d2), axis=1, keepdims=True)
                dl_ref[pl.ds(off, TQ), :] = jnp.concatenate(
                    [jnp.broadcast_to(dl0, (TQ, LANES)), jnp.broadcast_to(dl1, (TQ, LANES))], axis=1)
                return c

            lax.fori_loop(0, nq, dl_step, 0)

        kk = k_ref[...]
        if fox:
            kk = jnp.concatenate([kk, ek_ref[...]], axis=1)
        v2 = v_ref[...]

        def wide(x2):
            st = jnp.concatenate([x2[:, :LANES], x2[:, LANES:]], axis=0)
            return st if TQ == LANES else jnp.concatenate([st] * (TQ // LANES), axis=1)

        def step(t, carry, diag):
            dkk, dv2 = carry
            off = pl.multiple_of((j + t) * TQ, TQ)
            q2 = (q_ref[pl.ds(off, TQ), :].astype(F32) * scale).astype(BF16)
            qs = _stack_heads(q2, h0, eq_ref[pl.ds(off, TQ), :] if fox else None)
            dos = _stack_heads(do_ref[pl.ds(off, TQ), :].astype(BF16), h0)
            s = jnp.concatenate([_dot_nt(qs[:TQ], kk), _dot_nt(qs[TQ:], kk)], axis=0)
            if not fox:
                s = (s.reshape(2, TQ, TQ) + b_ref[t]).reshape(2 * TQ, TQ)
            elif diag:
                rows = lax.broadcasted_iota(jnp.int32, (2, TQ, TQ), 1).reshape(2 * TQ, TQ)
                cols = lax.broadcasted_iota(jnp.int32, (2 * TQ, TQ), 1)
                s = jnp.where(cols <= rows, s, NEG)
            p = jnp.exp(s - wide(lse_ref[pl.ds(off, TQ), :]))
            dp = jnp.concatenate([_dot_nt(dos[:TQ], v2), _dot_nt(dos[TQ:], v2)], axis=0)
            dsb = (p * (dp - wide(dl_ref[pl.ds(off, TQ), :]))).astype(BF16)
            dv2 = dv2 + _dot_tn(p.astype(BF16), dos)
            dkk = dkk + _dot_tn(dsb, qs)
            dqq = jnp.concatenate([_dot(dsb[:TQ], kk), _dot(dsb[TQ:], kk)], axis=0)
            dq_ref[pl.ds(off, TQ), :] += jnp.where(h0, dqq[:TQ, :LANES], dqq[TQ:, :LANES])
            if fox:
                dqe_ref[pl.ds(off, TQ), :] += jnp.where(lane < SUBLANES, dqq[:TQ, LANES:], dqq[TQ:, LANES:])
            return dkk, dv2

        zero = (jnp.zeros((TQ, kc), F32), jnp.zeros((TQ, LANES), F32))
        if fox:
            dkk, dv2 = lax.fori_loop(1, nq - j, lambda t, c: step(t, c, False), step(0, zero, True))
        else:
            dkk, dv2 = lax.fori_loop(0, jnp.minimum(nq - j, nd), lambda t, c: step(t, c, False), zero)
        dk_ref[...] = dkk[:, :LANES]
        dv_ref[...] = dv2
        if fox:
            dek_ref[...] = dkk[:, LANES:]

        @pl.when(j == nq - 1)
        def _():
            dq_ref[...] = dq_ref[...] * scale

    seq = lambda c, w=LANES: pl.BlockSpec((S, w), lambda b, hp, j: (b, c + hp))
    blk = lambda c: pl.BlockSpec((TQ, LANES), lambda b, hp, j: (b * nq + j, c + hp))
    in_specs = [seq(sq), blk(sk), blk(sv), seq(0), seq(dsec), seq(0, 2 * LANES)]
    args = [qkv, qkv, qkv, o, do, lse]
    if fox:
        in_specs += [seq(0, 2 * LANES), blk(0)]
        args += [eq, ek]
    else:
        in_specs.append(pl.BlockSpec(bias.shape, lambda b, hp, j: (0, 0, 0)))
        args.append(bias)
    out_specs = [seq(0), blk(0), blk(0)]
    out_shape = [jax.ShapeDtypeStruct((T, WA), F32)] * 3
    if fox:
        out_specs += [seq(0), blk(0)]
        out_shape += [jax.ShapeDtypeStruct((T, WA), F32)] * 2
    return pl.pallas_call(
        body, name=name, grid=(nbl, hp_n, nq), in_specs=in_specs, out_specs=out_specs, out_shape=out_shape,
        scratch_shapes=[pltpu.VMEM((S, 2 * LANES), F32)],
        compiler_params=_cparams(("parallel", "parallel", "arbitrary")),
    )(*args)


def _exchange(name, ins, out_shapes, remote, local):
    n_in, n_out = len(ins), len(out_shapes)
    nr, nl = len(remote), len(local)

    def body(*refs):
        in_refs = refs[:n_in]
        out_refs = refs[n_in:n_in + n_out]
        send_sems, recv_sems, loc_sems = refs[n_in + n_out:]
        me = (lax.axis_index("x"), lax.axis_index("y"), lax.axis_index("c"))

        def peer_of(flip):
            return tuple(1 - v if f else v for v, f in zip(me, flip))

        def at(ref, idx):
            return ref if idx is None else ref.at[idx]

        def rcopy(k, who):
            flip, a, sfn, b, dfn = remote[k]
            return pltpu.make_async_remote_copy(
                src_ref=at(in_refs[a], sfn(*who)), dst_ref=at(out_refs[b], dfn(*who)),
                send_sem=send_sems.at[k], recv_sem=recv_sems.at[k],
                device_id=peer_of(flip), device_id_type=MESH)

        locs = [pltpu.make_async_copy(at(in_refs[a], sfn(*me)), at(out_refs[b], dfn(*me)), loc_sems.at[k])
                for k, (a, sfn, b, dfn) in enumerate(local)]
        for cp in locs:
            cp.start()
        sends = [rcopy(k, me) for k in range(nr)]
        for cp in sends:
            cp.start()
        for k in range(nr):
            rcopy(k, peer_of(remote[k][0])).wait_recv()
        for cp in sends:
            cp.wait_send()
        for cp in locs:
            cp.wait()

    any_spec = pl.BlockSpec(memory_space=pl.ANY)
    return pl.pallas_call(
        body, name=name, in_specs=[any_spec] * n_in, out_specs=[any_spec] * n_out, out_shape=list(out_shapes),
        scratch_shapes=[pltpu.SemaphoreType.DMA((max(nr, 1),)), pltpu.SemaphoreType.DMA((max(nr, 1),)),
                        pltpu.SemaphoreType.DMA((max(nl, 1),))],
    )(*ins)


_FLIPS7 = [(0, 0, 1), (0, 1, 0), (0, 1, 1), (1, 0, 0), (1, 0, 1), (1, 1, 0), (1, 1, 1)]
_CHIP_FLIPS = [(1, 0, 0), (0, 1, 0), (1, 1, 0)]


def _dev_index(x, y, c):
    return 4 * x + 2 * y + c


def _chip_index(x, y, c):
    return 2 * x + y


def _all_gather8(name, v):
    remote = [(f, 0, lambda x, y, c: None, 0, _dev_index) for f in _FLIPS7]
    local = [(0, lambda x, y, c: None, 0, _dev_index)]
    return _exchange(name, [v], [jax.ShapeDtypeStruct((NDEV,) + v.shape, v.dtype)], remote, local)[0]


def _gather_chips(name, vs):
    remote, local = [], []
    for n in range(len(vs)):
        local.append((n, lambda x, y, c: None, n, _chip_index))
        for f in _CHIP_FLIPS:
            remote.append((f, n, lambda x, y, c: None, n, _chip_index))
    shapes = [jax.ShapeDtypeStruct((NCHIP,) + v.shape, v.dtype) for v in vs]
    return _exchange(name, list(vs), shapes, remote, local)


def _to_sibling(name, v):
    remote = [((0, 0, 1), 0, lambda x, y, c: None, 0, lambda x, y, c: None)]
    return _exchange(name, [v], [jax.ShapeDtypeStruct(v.shape, v.dtype)], remote, [])[0]


def _scatter_chips(name, v):
    remote = []
    for f in _CHIP_FLIPS:
        src = lambda x, y, c, f=f: _chip_index(1 - x if f[0] else x, 1 - y if f[1] else y, c)
        remote.append((f, 0, src, 0, _chip_index))
    local = [(0, _chip_index, 0, _chip_index)]
    return _exchange(name, [v], [jax.ShapeDtypeStruct(v.shape, v.dtype)], remote, local)[0]


def _pair_gather(name, v):
    remote = [((0, 0, 1), 0, lambda x, y, c: None, 0, lambda x, y, c: c)]
    local = [(0, lambda x, y, c: None, 0, lambda x, y, c: c)]
    return _exchange(name, [v], [jax.ShapeDtypeStruct((2,) + v.shape, v.dtype)], remote, local)[0]


def _sum_leading(name, v, tm=None):
    n, r, w = v.shape
    tm = _pick(r, (256, 128, 64, 32, 16, 8)) if tm is None else tm

    def body(v_ref, o_ref):
        acc = v_ref[0]
        for k in range(1, n):
            acc = acc + v_ref[k]
        o_ref[...] = acc

    return pl.pallas_call(
        body, name=name, grid=(r // tm,), in_specs=[pl.BlockSpec((n, tm, w), lambda i: (0, i, 0))],
        out_specs=pl.BlockSpec((tm, w), lambda i: (i, 0)), out_shape=jax.ShapeDtypeStruct((r, w), F32),
        compiler_params=_cparams(("parallel",)),
    )(v)


def _add2(name, a, b, tm=None):
    r, w = a.shape
    tm = _pick(r, (256, 128, 64, 32, 16, 8)) if tm is None else tm

    def body(a_ref, b_ref, o_ref):
        o_ref[...] = a_ref[...] + b_ref[...]

    spec = pl.BlockSpec((tm, w), lambda i: (i, 0))
    return pl.pallas_call(
        body, name=name, grid=(r // tm,), in_specs=[spec, spec], out_specs=spec,
        out_shape=jax.ShapeDtypeStruct((r, w), F32), compiler_params=_cparams(("parallel",)),
    )(a, b)


def _ada_fwd(call_all, w_shard):
    def body(c_ref, w_ref, o_ref):
        cv = c_ref[...]
        o_ref[...] = jnp.dot(cv * _sigmoid(cv), w_ref[...], preferred_element_type=F32,
                             precision=lax.Precision.HIGHEST)

    n = w_shard.shape[1]
    return pl.pallas_call(
        body, name="ada_fwd", out_shape=jax.ShapeDtypeStruct((call_all.shape[0], n), F32),
        compiler_params=pltpu.CompilerParams(vmem_limit_bytes=VMEM_LIMIT),
    )(call_all, w_shard)


def _ada_bwd(call_all, dada):
    def body(c_ref, d_ref, o_ref):
        cv = c_ref[...]
        o_ref[...] = lax.dot_general(cv * _sigmoid(cv), d_ref[...], (((0,), (0,)), ((), ())),
                                     preferred_element_type=F32, precision=lax.Precision.HIGHEST)

    return pl.pallas_call(
        body, name="ada_bwd", out_shape=jax.ShapeDtypeStruct((call_all.shape[1], dada.shape[1]), F32),
        compiler_params=pltpu.CompilerParams(vmem_limit_bytes=VMEM_LIMIT),
    )(call_all, dada)


def _adamw(name, w, g, m, v):
    r, wd = w.shape
    tm = _pick(r, (256, 128, 64, 32, 16, 8))
    bc1 = 1.0 - ADAM_B1 ** ADAM_STEP
    bc2 = 1.0 - ADAM_B2 ** ADAM_STEP

    def body(w_ref, g_ref, m_ref, v_ref, d_ref, mo_ref, vo_ref):
        gv = g_ref[...]
        mn = ADAM_B1 * m_ref[...] + (1.0 - ADAM_B1) * gv
        vn = ADAM_B2 * v_ref[...] + (1.0 - ADAM_B2) * (gv * gv)
        d_ref[...] = -ADAM_LR * ((mn / bc1) / (jnp.sqrt(vn / bc2) + ADAM_EPS) + ADAM_WD * w_ref[...])
        mo_ref[...] = mn
        vo_ref[...] = vn

    spec = pl.BlockSpec((tm, wd), lambda i: (i, 0))
    return pl.pallas_call(
        body, name=name, grid=(r // tm,), in_specs=[spec] * 4, out_specs=[spec] * 3,
        out_shape=[jax.ShapeDtypeStruct((r, wd), F32)] * 3, compiler_params=_cparams(("parallel",)),
    )(w, g, m, v)


def _rope_tables(positions):
    half = ROPE_DIMS // 2
    freqs = ROPE_THETA ** (-jnp.arange(0, ROPE_DIMS, 2, dtype=F32) / ROPE_DIMS)
    ang = positions.astype(F32).reshape(-1, 1) * freqs
    cos, sin = jnp.cos(ang), jnp.sin(ang)
    T = ang.shape[0]
    one = jnp.ones((T, HD - ROPE_DIMS), F32)
    zero = jnp.zeros((T, HD - ROPE_DIMS), F32)
    zh = jnp.zeros((T, half), F32)
    c64 = jnp.concatenate([cos, cos, one], axis=1)
    s1 = jnp.concatenate([zh, sin, zero], axis=1)
    s2 = jnp.concatenate([-sin, zh, zero], axis=1)
    rep = lambda t: jnp.concatenate([t] * (LANES // HD), axis=1)
    return rep(c64), rep(s1), rep(s2)


def _local_step(x, loss_target, positions, ada, w_qkv, w_f, w_out, w_up, conv_w8, w_down,
                b_fgate, gn, ln1_g, ln1_b, conv_b, ln2_g, ln2_b):
    T = x.shape[0]
    nbl = T // S
    nha = WA // HD
    sv = lambda k: ada[:, k:k + 1, :]
    sh_a, sc_a, g_a, sh_f, sc_f, g_f = (sv(k) for k in range(6))
    rope = _rope_tables(positions)
    neg_rope = (rope[0], -rope[1], -rope[2])
    gseg = jnp.asarray(np.kron(np.eye(min(256, 2 * WA) // HD), np.ones((HD, HD))), BF16)
    bias = jnp.asarray(_dil_bias(TQ))
    bf_pad = jnp.zeros((1, FPAD), F32).at[:, :nha].set(b_fgate)

    qkv, fa = _mod_mm("qkv_proj", x, sc_a, sh_a, (w_qkv, w_f), (BF16, F32), rope=rope, rope_secs=(3, 4))
    fcum = _cumsum_seq("fgate_fwd", fa, lambda f, b: _log_sigmoid(f + b), lambda cum, f, b: cum,
                       reverse=False, extra=(bf_pad,))[0]
    eq, ek = _fold_operands(fcum, nha)
    oa, lse_a = _attn_fwd("fox_fwd", qkv, (0, 1, 2), True, eq=eq, ek=ek)
    ob, lse_b = _attn_fwd("dil_fwd", qkv, (3, 4, 5), False, bias=bias)

    def mix_fn(i, oav, obv, xv, gav, gn_ref, g_ref, wo_ref, l1g_ref, l1b_ref):
        o = jnp.concatenate([oav, obv], axis=1)
        rs = lax.rsqrt(_head_mean(o * o, g_ref) + RMS_EPS)
        merged = (o * rs * gn_ref[...]).astype(BF16)
        mix = _dot(merged, wo_ref[...])
        x1, _, _ = _ln_fwd(ALPHA * xv + gav * mix, l1g_ref[...], l1b_ref[...])
        return merged, mix, x1

    merged, mix, x1 = _rowwise("mix_out", mix_fn, T, 256, tiles=(oa, ob, x), seqvecs=(g_a,),
                               consts=(gn, gseg, w_out, ln1_g, ln1_b),
                               outs=((2 * WA, BF16), (D, F32), (D, F32)))
    u = _mod_mm("ffn_up", x1, sc_f, sh_f, (w_up,), (F32,))[0]

    def conv_y(i, uv, prev, cw_ref, cb_ref, tm):
        first = (i * tm) % S == 0
        s1, s2 = _conv_taps(uv, prev, first)
        y = cb_ref[...] + cw_ref[0:1, :] * s2 + cw_ref[1:2, :] * s1 + cw_ref[2:3, :] * uv
        return y, s1, s2

    tmc = 128

    def gate_fn(i, uv, prev, cw_ref, cb_ref):
        y, _, _ = conv_y(i, uv, prev, cw_ref, cb_ref, tmc)
        a, g = y[:, :DFF], y[:, DFF:]
        return g * _sigmoid(g) * a

    act = _rowwise("conv_gate", gate_fn, T, tmc, tiles=(u,), halos=((u, -1),), consts=(conv_w8, conv_b),
                   outs=((DFF, BF16),))[0]

    def down_fn(i, actv, x1v, tgt, gfv, wd_ref, g2_ref, b2_ref):
        ffn = _dot(actv, wd_ref[...])
        y, n2, rstd = _ln_fwd(ALPHA * x1v + gfv * ffn, g2_ref[...], b2_ref[...])
        err = y - tgt
        dy = err * (1.0 / D)
        dr2 = _ln_bwd(dy, n2, rstd, g2_ref[...])
        return (dr2, gfv * dr2, _rsum8(err * err), _rsum8(dy * n2), _rsum8(dy), _rsum8(dr2 * ffn))

    dr2, dffn, loss_acc, d_ln2g, d_ln2b, d_gf = _rowwise(
        "ffn_down_loss", down_fn, T, 256, tiles=(act, x1, loss_target), seqvecs=(g_f,),
        consts=(w_down, ln2_g, ln2_b), outs=((D, F32), (D, BF16)), accs=(D, D, D), seqaccs=(D,))

    dact = _mm_nt("dact", dffn, w_down)

    def gate_bwd_fn(i, uv, dav, prev, cw_ref, cb_ref):
        y, s1, s2 = conv_y(i, uv, prev, cw_ref, cb_ref, tmc)
        a, g = y[:, :DFF], y[:, DFF:]
        sg = _sigmoid(g)
        dyc = jnp.concatenate([dav * (g * sg), dav * a * (sg * (1.0 + g * (1.0 - sg)))], axis=1)
        return dyc, _rsum8(dyc), _rsum8(dyc * s2), _rsum8(dyc * s1), _rsum8(dyc * uv)

    dyc, d_cb, d_cw0, d_cw1, d_cw2 = _rowwise(
        "gate_bwd", gate_bwd_fn, T, tmc, tiles=(u, dact), halos=((u, -1),), consts=(conv_w8, conv_b),
        outs=((2 * DFF, F32),), accs=(2 * DFF,) * 4)

    def conv_bwd_fn(i, dv, nxt, cw_ref):
        last = ((i + 1) * tmc) % S == 0
        u1, u2 = _conv_taps_up(dv, nxt, last)
        return cw_ref[2:3, :] * dv + cw_ref[1:2, :] * u1 + cw_ref[0:1, :] * u2

    du = _rowwise("conv_bwd", conv_bwd_fn, T, tmc, tiles=(dyc,), halos=((dyc, 1),), consts=(conv_w8,),
                  outs=((2 * DFF, BF16),))[0]
    dh2 = _mm_nt("dh2", du, w_up)
    g_w_down = _mm_tn("dw_down", act, dffn)
    g_w_up = _mm_tn("dw_up", x1, du, mod=(sc_f, sh_f))

    def ln1_bwd_fn(i, dr2v, dh2v, xv, mixv, x1v, scfv, gav, l1g_ref):
        dx1 = ALPHA * dr2v + dh2v * (1.0 + scfv)
        _, n1, rstd = _ln_fwd(ALPHA * xv + gav * mixv, l1g_ref[...], 0.0)
        dr1 = _ln_bwd(dx1, n1, rstd, l1g_ref[...])
        return (dr1, gav * dr1, _rsum8(dx1 * n1), _rsum8(dx1),
                _rsum8(dh2v * x1v), _rsum8(dh2v), _rsum8(dr1 * mixv))

    dr1, dmix, d_ln1g, d_ln1b, d_scf, d_shf, d_ga = _rowwise(
        "ln1_bwd", ln1_bwd_fn, T, 256, tiles=(dr2, dh2, x, mix, x1), seqvecs=(sc_f, g_a), consts=(ln1_g,),
        outs=((D, F32), (D, BF16)), accs=(D, D), seqaccs=(D, D, D))

    dmerged = _mm_nt("dmerged", dmix, w_out)
    g_w_out = _mm_tn("dw_out", merged, dmix)

    def hn_bwd_fn(i, dmv, oav, obv, gn_ref, g_ref):
        o = jnp.concatenate([oav, obv], axis=1)
        rs = lax.rsqrt(_head_mean(o * o, g_ref) + RMS_EPS)
        nrm = o * rs
        dn = dmv * gn_ref[...]
        do = rs * (dn - nrm * _head_mean(dn * nrm, g_ref))
        return do, _rsum8(dmv * nrm)

    do, d_gn = _rowwise("headnorm_bwd", hn_bwd_fn, T, 256, tiles=(dmerged, oa, ob), consts=(gn, gseg),
                        outs=((2 * WA, F32),), accs=(2 * WA,))
    dqa, dka, dva, dqe, dek = _attn_bwd("fox_bwd", qkv, (0, 1, 2), oa, do, 0, lse_a, True, eq=eq, ek=ek)
    dqb, dkb, dvb = _attn_bwd("dil_bwd", qkv, (3, 4, 5), ob, do, 1, lse_b, False, bias=bias)
    dftok = jnp.stack([dqe[:, (h // 2) * LANES + 8 * (h % 2)] - dek[:, (h // 2) * LANES + 3 + 8 * (h % 2)]
                       for h in range(nha)], axis=1)
    dftok = jnp.pad(dftok, ((0, 0), (0, FPAD - nha)))
    dfa, d_bf = _cumsum_seq("fgate_bwd", dftok, lambda d, f, b: d,
                            lambda cum, d, f, b: cum * _sigmoid(-(f + b)),
                            reverse=True, extra=(fa, bf_pad), n_acc=1)

    def dz_fn(i, a0, a1, a2, b0, b1, b2, fv, cv, s1v, s2v):
        ct, s1t, s2t = (_tile_lanes(t, WA) for t in (cv, s1v, s2v))
        return jnp.concatenate([a0, a1, a2, _rope(b0, ct, s1t, s2t), _rope(b1, ct, s1t, s2t), b2, fv], axis=1)

    dz = _rowwise("dz_pack", dz_fn, T, 256, tiles=(dqa, dka, dva, dqb, dkb, dvb, dfa) + neg_rope,
                  outs=((6 * WA + FPAD, BF16),))[0]
    w_cat = jnp.concatenate([w_qkv, w_f], axis=1)
    dh1 = _mm_nt("dh1", dz, w_cat)
    g_w_cat = _mm_tn("dw_in", x, dz, mod=(sc_a, sh_a))

    def dx_fn(i, dr1v, dh1v, xv, scav):
        return ALPHA * dr1v + dh1v * (1.0 + scav), _rsum8(dh1v * xv), _rsum8(dh1v)

    grad_x, d_sca, d_sha = _rowwise("dx_out", dx_fn, T, 256, tiles=(dr1, dh1, x), seqvecs=(sc_a,),
                                    outs=((D, F32),), seqaccs=(D, D))

    row0 = lambda a: a[..., 0, :]
    d_ada = jnp.stack([row0(d_sha), row0(d_sca), row0(d_ga), row0(d_shf), row0(d_scf), row0(d_gf)], axis=1)
    d_cw = jnp.stack([row0(d_cw0), row0(d_cw1), row0(d_cw2)], axis=0)
    loss_part = (0.5 / D) * jnp.sum(loss_acc[0])
    small = dict(b_fgate=row0(d_bf)[:nha], gn=row0(d_gn), ln1_g=row0(d_ln1g), ln1_b=row0(d_ln1b),
                 conv_b=row0(d_cb), ln2_g=row0(d_ln2g), ln2_b=row0(d_ln2b))
    big = dict(w_cat=g_w_cat, w_out=g_w_out, w_up=g_w_up, conv_w=d_cw, w_down=g_w_down)
    return loss_part, grad_x, d_ada, small, big


def _rows_of(n, w=None):
    return -(-n // (D if w is None else w))


def _as_rows(v):
    w = D
    k = v.shape[0]
    flat = v.reshape(k, -1)
    rows = _rows_of(_rows_of(flat.shape[1], w), SUBLANES) * SUBLANES
    flat = jnp.pad(flat, ((0, 0), (0, rows * w - flat.shape[1])))
    return flat.reshape(k, rows, w)


def kernel(x, c, positions, w_ada, b_ada, w_in, b_fgate, gn_a, gn_b, w_out, ln1_g, ln1_b, w_up, conv_w, conv_b, w_down, ln2_g, ln2_b, loss_target, m_w_ada, m_b_ada, m_w_in, m_b_fgate, m_gn_a, m_gn_b, m_w_out, m_ln1_g, m_ln1_b, m_w_up, m_conv_w, m_conv_b, m_w_down, m_ln2_g, m_ln2_b, v_w_ada, v_b_ada, v_w_in, v_b_fgate, v_gn_a, v_gn_b, v_w_out, v_ln1_g, v_ln1_b, v_w_up, v_conv_w, v_conv_b, v_w_down, v_ln2_g, v_ln2_b):
    mx, my, mc = lax.axis_index("x"), lax.axis_index("y"), lax.axis_index("c")
    dev = _dev_index(mx, my, mc)
    chip = _chip_index(mx, my, mc)
    nbl = x.shape[0]
    T = nbl * S
    nha = WA // HD
    d_in = w_in.shape[2] * NCHIP
    n_ada = w_ada.shape[2]

    c_pad = jnp.zeros((SUBLANES, D), F32).at[:nbl].set(c)
    c_all = _all_gather8("gather_c", c_pad)[:, :nbl].reshape(NDEV * nbl, D)
    ada_part = _ada_fwd(c_all, w_ada[0])
    ada_blocks = _all_gather8("gather_ada", ada_part)
    ada_all = jnp.concatenate([ada_blocks[2 * k] for k in range(NCHIP)], axis=1) + b_ada
    ada = lax.dynamic_slice_in_dim(ada_all, dev * nbl, nbl, axis=0).reshape(nbl, 6, D)

    w_in_sh = jnp.pad(w_in[0].astype(BF16), ((0, 0), (0, _rows_of(w_in.shape[2], LANES) * LANES - w_in.shape[2])))
    cw_sh = jnp.pad(conv_w[0], ((0, SUBLANES - conv_w.shape[1]), (0, 0)))
    g_in, g_out, g_up, g_cw, g_down = _gather_chips(
        "gather_w", [w_in_sh, w_out[0].astype(BF16), w_up[0].astype(BF16), cw_sh, w_down[0].astype(BF16)])
    w_in_full = jnp.concatenate([g_in[k][:, :w_in.shape[2]] for k in range(NCHIP)], axis=1)
    w_qkv = jnp.concatenate([w_in_full[:, :3 * WA], w_in_full[:, 3 * WA + nha:]], axis=1)
    w_f = jnp.pad(w_in_full[:, 3 * WA:3 * WA + nha], ((0, 0), (0, FPAD - nha)))
    w_out_full = g_out.reshape(NCHIP * w_out.shape[1], D)
    w_up_full = jnp.concatenate([g_up[k] for k in range(NCHIP)], axis=1)
    conv_w8 = jnp.concatenate([g_cw[k] for k in range(NCHIP)], axis=1)
    w_down_full = g_down.reshape(NCHIP * w_down.shape[1], D)

    gn = jnp.concatenate([gn_a, gn_b], axis=1)
    loss_part, grad_x, d_ada, small, big = _local_step(
        x.reshape(T, D), loss_target.reshape(T, D), positions, ada, w_qkv, w_f, w_out_full, w_up_full, conv_w8,
        w_down_full, b_fgate, gn, ln1_g, ln1_b, conv_b, ln2_g, ln2_b)

    def row_pad(v, rows):
        flat = v.reshape(-1)
        return jnp.pad(flat, (0, rows * D - flat.shape[0]))

    n_cb = _rows_of(2 * DFF)
    small_flat = jnp.concatenate([
        row_pad(small["b_fgate"], 1), row_pad(small["gn"], 1), row_pad(small["ln1_g"], 1),
        row_pad(small["ln1_b"], 1), row_pad(small["ln2_g"], 1), row_pad(small["ln2_b"], 1),
        row_pad(jnp.full((1,), loss_part, F32), 1), row_pad(small["conv_b"], n_cb)])
    n_small = _rows_of(small_flat.shape[0], SUBLANES * D) * SUBLANES
    small_rows = jnp.pad(small_flat, (0, n_small * D - small_flat.shape[0])).reshape(n_small, D)
    ada_rows = jnp.pad(d_ada.reshape(nbl, 6, D), ((0, 0), (0, SUBLANES - 6), (0, 0))).reshape(nbl * SUBLANES, D)
    gathered = _all_gather8("gather_small", jnp.concatenate([small_rows, ada_rows], axis=0))
    red = _sum_leading("sum_small", gathered, tm=SUBLANES)
    g_b_fgate = red[0:1, :nha]
    g_gn = red[1:2, :2 * WA]
    g_ln1_g, g_ln1_b, g_ln2_g, g_ln2_b = red[2:3], red[3:4], red[4:5], red[5:6]
    loss = red[6, 0]
    g_conv_b = red[7:7 + n_cb].reshape(1, -1)[:, :2 * DFF]
    g_b_ada = _add2("sum_b_ada", red[n_small:n_small + SUBLANES], red[n_small + SUBLANES:n_small + 2 * SUBLANES],
                    tm=SUBLANES)[:6].reshape(1, 6 * D)
    dada_all = gathered[:, n_small:].reshape(NDEV, nbl, SUBLANES, D)[:, :, :6].reshape(NDEV * nbl, 6 * D)
    g_w_ada = _ada_bwd(c_all, lax.dynamic_slice_in_dim(dada_all, chip * n_ada, n_ada, axis=1))

    g_cat = big["w_cat"]
    g_w_in_full = jnp.concatenate([g_cat[:, :3 * WA], g_cat[:, 6 * WA:6 * WA + nha], g_cat[:, 3 * WA:6 * WA]], axis=1)
    sh_in = g_w_in_full.reshape(D, NCHIP, -1).transpose(1, 0, 2)
    sh_out = big["w_out"].reshape(NCHIP, -1, D)
    sh_up = big["w_up"].reshape(D, NCHIP, -1).transpose(1, 0, 2)
    sh_cw = big["conv_w"].reshape(conv_w.shape[1], NCHIP, -1).transpose(1, 0, 2)
    sh_down = big["w_down"].reshape(NCHIP, -1, D)
    parts = [_as_rows(t) for t in (sh_in, sh_out, sh_up, sh_cw, sh_down)]
    part_rows = [p.shape[1] for p in parts]
    packed = jnp.concatenate(parts, axis=1)
    n_rows = _rows_of(packed.shape[1], 2 * SUBLANES) * 2 * SUBLANES
    half = n_rows // 2
    packed = jnp.pad(packed, ((0, 0), (0, n_rows - packed.shape[1]), (0, 0)))
    halves = packed.reshape(NCHIP, 2, half, D)
    mine = lax.dynamic_index_in_dim(halves, mc, axis=1, keepdims=False).reshape(NCHIP * half, D)
    theirs = lax.dynamic_index_in_dim(halves, 1 - mc, axis=1, keepdims=False).reshape(NCHIP * half, D)
    from_sib = _to_sibling("pair_swap", theirs)
    pair_sum = _add2("pair_sum", mine, from_sib).reshape(NCHIP, half, D)
    by_chip = _scatter_chips("scatter_grads", pair_sum)
    my_half = _sum_leading("chip_sum", by_chip)
    both = _pair_gather("pair_share", my_half)
    shard = both.reshape(n_rows, D)

    def unpack(k, shape):
        start = sum(part_rows[:k])
        n = int(np.prod(shape))
        return shard[start:start + part_rows[k]].reshape(-1)[:n].reshape(shape)

    g_w_in = unpack(0, w_in.shape[1:])
    g_w_out = unpack(1, w_out.shape[1:])
    g_w_up = unpack(2, w_up.shape[1:])
    g_conv_w = unpack(3, conv_w.shape[1:])
    g_w_down = unpack(4, w_down.shape[1:])

    grads = dict(w_ada=g_w_ada, b_ada=g_b_ada, w_in=g_w_in, b_fgate=g_b_fgate, gn_a=g_gn[:, :WA], gn_b=g_gn[:, WA:],
                 w_out=g_w_out, ln1_g=g_ln1_g, ln1_b=g_ln1_b, w_up=g_w_up, conv_w=g_conv_w, conv_b=g_conv_b,
                 w_down=g_w_down, ln2_g=g_ln2_g, ln2_b=g_ln2_b)
    weights = dict(w_ada=w_ada, b_ada=b_ada, w_in=w_in, b_fgate=b_fgate, gn_a=gn_a, gn_b=gn_b, w_out=w_out,
                   ln1_g=ln1_g, ln1_b=ln1_b, w_up=w_up, conv_w=conv_w, conv_b=conv_b, w_down=w_down,
                   ln2_g=ln2_g, ln2_b=ln2_b)
    ms = dict(w_ada=m_w_ada, b_ada=m_b_ada, w_in=m_w_in, b_fgate=m_b_fgate, gn_a=m_gn_a, gn_b=m_gn_b,
              w_out=m_w_out, ln1_g=m_ln1_g, ln1_b=m_ln1_b, w_up=m_w_up, conv_w=m_conv_w, conv_b=m_conv_b,
              w_down=m_w_down, ln2_g=m_ln2_g, ln2_b=m_ln2_b)
    vs = dict(w_ada=v_w_ada, b_ada=v_b_ada, w_in=v_w_in, b_fgate=v_b_fgate, gn_a=v_gn_a, gn_b=v_gn_b,
              w_out=v_w_out, ln1_g=v_ln1_g, ln1_b=v_ln1_b, w_up=v_w_up, conv_w=v_conv_w, conv_b=v_conv_b,
              w_down=v_w_down, ln2_g=v_ln2_g, ln2_b=v_ln2_b)
    names = list(weights)
    big_names = ("w_ada", "w_in", "w_out", "w_up", "w_down")
    delta, new_m, new_v = {}, {}, {}
    for n in big_names:
        shp = weights[n].shape
        d, m2, v2 = _adamw("adamw_" + n, weights[n][0], grads[n].reshape(shp[1:]), ms[n][0], vs[n][0])
        delta[n], new_m[n], new_v[n] = d.reshape(shp), m2.reshape(shp), v2.reshape(shp)
    small_names = [n for n in names if n not in big_names]

    def pack_small(src):
        flats = []
        for n in small_names:
            flat = src[n].reshape(-1)
            flats.append(jnp.pad(flat, (0, _rows_of(flat.shape[0]) * D - flat.shape[0])))
        allf = jnp.concatenate(flats)
        rows = _rows_of(allf.shape[0], SUBLANES * D) * SUBLANES
        return jnp.pad(allf, (0, rows * D - allf.shape[0])).reshape(rows, D)

    sd, sm, sv_ = _adamw("adamw_small", pack_small(weights), pack_small(grads), pack_small(ms), pack_small(vs))
    off = 0
    for n in small_names:
        shp = weights[n].shape
        cnt = int(np.prod(shp))
        r = _rows_of(cnt)
        for dst, src in ((delta, sd), (new_m, sm), (new_v, sv_)):
            dst[n] = src[off:off + r].reshape(-1)[:cnt].reshape(shp)
        off += r

    out_g = {n: grads[n].reshape(weights[n].shape) for n in names}
    return (loss, grad_x.reshape(x.shape), *[out_g[n] for n in names], *[delta[n] for n in names],
            *[new_m[n] for n in names], *[new_v[n] for n in names])
```

```python
import functools
import math

import numpy as np
import jax
import jax.numpy as jnp
from jax import lax
from jax.experimental import pallas as pl
from jax.experimental.pallas import tpu as pltpu

F32 = jnp.float32
BF16 = jnp.bfloat16

D = 1024
S = 4096
HD = 64
WA = 512
DFF = 2816
NCHIP = 4
NDEV = 8
PATTERNS = ((128, 1), (512, 4), (2048, 16))
ROPE_THETA = 500000.0
ROPE_DIMS = HD // 4
ALPHA = (2.0 * 1) ** 0.25
LN_EPS = 1e-5
RMS_EPS = 1e-6
ADAM_LR = 0.001
ADAM_B1 = 0.9
ADAM_B2 = 0.999
ADAM_EPS = 1e-08
ADAM_WD = 0.01
ADAM_STEP = 10

LANES = 128
SUBLANES = 8
TQ = 256
FPAD = LANES
NEG = -1e30
VMEM_LIMIT = 56 * 1024 * 1024
MESH = pl.DeviceIdType.MESH


def _cparams(sem):
    return pltpu.CompilerParams(dimension_semantics=sem, vmem_limit_bytes=VMEM_LIMIT)


def _pick(n, cands):
    for c in cands:
        if n % c == 0:
            return c
    return n


def _rsum8(v):
    tm, w = v.shape
    return jnp.sum(v.reshape(tm // SUBLANES, SUBLANES, w), axis=0)


def _sigmoid(x):
    return 1.0 / (1.0 + jnp.exp(-x))


def _dot(a, b):
    return jnp.dot(a, b, preferred_element_type=F32)


def _dot_nt(a, b):
    return lax.dot_general(a, b, (((1,), (1,)), ((), ())), preferred_element_type=F32)


def _dot_tn(a, b):
    return lax.dot_general(a, b, (((0,), (0,)), ((), ())), preferred_element_type=F32)


def _rowwise(name, fn, T, tm, *, tiles=(), halos=(), seqvecs=(), consts=(), outs=(), accs=(), seqaccs=(),
             seq_len=None):
    seq_len = S if seq_len is None else seq_len
    nb = T // tm
    spb = max(seq_len // tm, 1)
    nseq = max(T // seq_len, 1)
    n8 = T // SUBLANES
    r8 = tm // SUBLANES
    in_specs, args = [], []
    for a in tiles:
        in_specs.append(pl.BlockSpec((tm, a.shape[1]), lambda i: (i, 0)))
        args.append(a)
    for a, direction in halos:
        if direction < 0:
            idx = lambda i: (jnp.maximum(i * r8 - 1, 0), 0)
        else:
            idx = lambda i: (jnp.minimum((i + 1) * r8, n8 - 1), 0)
        in_specs.append(pl.BlockSpec((SUBLANES, a.shape[1]), idx))
        args.append(a)
    for a in seqvecs:
        in_specs.append(pl.BlockSpec((1, 1, a.shape[2]), lambda i: (i // spb, 0, 0)))
        args.append(a)
    for a in consts:
        in_specs.append(pl.BlockSpec(a.shape, lambda i, nd=a.ndim: (0,) * nd))
        args.append(a)
    out_shape, out_specs = [], []
    for w, dt in outs:
        out_shape.append(jax.ShapeDtypeStruct((T, w), dt))
        out_specs.append(pl.BlockSpec((tm, w), lambda i: (i, 0)))
    for w in accs:
        out_shape.append(jax.ShapeDtypeStruct((SUBLANES, w), F32))
        out_specs.append(pl.BlockSpec((SUBLANES, w), lambda i: (0, 0)))
    for w in seqaccs:
        out_shape.append(jax.ShapeDtypeStruct((nseq, SUBLANES, w), F32))
        out_specs.append(pl.BlockSpec((1, SUBLANES, w), lambda i: (i // spb, 0, 0)))
    n_t, n_h, n_s, n_c = len(tiles), len(halos), len(seqvecs), len(consts)
    n_o, n_a, n_sa = len(outs), len(accs), len(seqaccs)

    def body(*refs):
        i = pl.program_id(0)
        ins = refs[:n_t + n_h + n_s + n_c]
        orefs = refs[n_t + n_h + n_s + n_c:]
        vals = [r[...] for r in ins[:n_t + n_h]]
        vals += [r[0] for r in ins[n_t + n_h:n_t + n_h + n_s]]
        vals += list(ins[n_t + n_h + n_s:])
        res = fn(i, *vals)
        if not isinstance(res, (tuple, list)):
            res = (res,)
        for k in range(n_o):
            orefs[k][...] = res[k].astype(orefs[k].dtype)
        for k in range(n_a):
            r = orefs[n_o + k]

            @pl.when(i == 0)
            def _():
                r[...] = jnp.zeros_like(r)

            r[...] += res[n_o + k]

            @pl.when(i == nb - 1)
            def _():
                r[...] = jnp.broadcast_to(jnp.sum(r[...], axis=0, keepdims=True), r.shape)
        for k in range(n_sa):
            r = orefs[n_o + n_a + k]

            @pl.when(i % spb == 0)
            def _():
                r[...] = jnp.zeros_like(r)

            r[0] += res[n_o + n_a + k]

            @pl.when(i % spb == spb - 1)
            def _():
                r[0] = jnp.broadcast_to(jnp.sum(r[0], axis=0, keepdims=True), r.shape[1:])

    sem = ("arbitrary",) if (n_a or n_sa) else ("parallel",)
    res = pl.pallas_call(
        body, name=name, grid=(nb,), in_specs=in_specs, out_specs=out_specs, out_shape=out_shape,
        compiler_params=_cparams(sem),
    )(*args)
    return res


def _ln_fwd(r, g, b):
    mu = jnp.mean(r, axis=-1, keepdims=True)
    xc = r - mu
    var = jnp.mean(xc * xc, axis=-1, keepdims=True)
    rstd = lax.rsqrt(var + LN_EPS)
    n = xc * rstd
    return n * g + b, n, rstd


def _ln_bwd(dy, n, rstd, g):
    dn = dy * g
    return rstd * (dn - jnp.mean(dn, axis=-1, keepdims=True) - n * jnp.mean(dn * n, axis=-1, keepdims=True))


def _head_mean(t, g_ref):
    gw = g_ref.shape[0]
    hi = t.astype(BF16)
    lo = (t - hi.astype(F32)).astype(BF16)
    g = g_ref[...]
    parts = []
    for c in range(t.shape[1] // gw):
        sl = slice(c * gw, (c + 1) * gw)
        parts.append(_dot(hi[:, sl], g) + _dot(lo[:, sl], g))
    out = parts[0] if len(parts) == 1 else jnp.concatenate(parts, axis=1)
    return out * (1.0 / HD)


def _rope(z, c, s1, s2):
    w = z.shape[1]
    half = ROPE_DIMS // 2
    return z * c + pltpu.roll(z, half, 1) * s1 + pltpu.roll(z, w - half, 1) * s2


def _tile_lanes(t, w):
    reps = w // t.shape[1]
    return t if reps == 1 else jnp.concatenate([t] * reps, axis=1)


def _conv_taps(ext, prev, first):
    tm = ext.shape[0]
    prev = jnp.where(first, jnp.zeros_like(prev), prev)
    r8 = lax.broadcasted_iota(jnp.int32, (SUBLANES, 1), 0)
    top = ext[0:SUBLANES]
    s1_top = jnp.where(r8 < 1, pltpu.roll(prev, 1, 0), pltpu.roll(top, 1, 0))
    s2_top = jnp.where(r8 < 2, pltpu.roll(prev, 2, 0), pltpu.roll(top, 2, 0))
    s1 = jnp.concatenate([s1_top, pltpu.roll(ext, 1, 0)[SUBLANES:]], axis=0)
    s2 = jnp.concatenate([s2_top, pltpu.roll(ext, 2, 0)[SUBLANES:]], axis=0)
    return s1, s2


def _conv_taps_up(ext, nxt, last):
    tm = ext.shape[0]
    nxt = jnp.where(last, jnp.zeros_like(nxt), nxt)
    r8 = lax.broadcasted_iota(jnp.int32, (SUBLANES, 1), 0)
    bot = ext[tm - SUBLANES:tm]
    u1_bot = jnp.where(r8 >= 7, pltpu.roll(nxt, 7, 0), pltpu.roll(bot, 7, 0))
    u2_bot = jnp.where(r8 >= 6, pltpu.roll(nxt, 6, 0), pltpu.roll(bot, 6, 0))
    u1 = jnp.concatenate([pltpu.roll(ext, tm - 1, 0)[:tm - SUBLANES], u1_bot], axis=0)
    u2 = jnp.concatenate([pltpu.roll(ext, tm - 2, 0)[:tm - SUBLANES], u2_bot], axis=0)
    return u1, u2


def _mm_nt(name, a, w, tm=256):
    T = a.shape[0]
    n = w.shape[0]
    ch = _pick(n, (512, 256, 128))

    def fn(i, av, w_ref):
        ab = av.astype(BF16)
        parts = [_dot_nt(ab, w_ref[c * ch:(c + 1) * ch, :]) for c in range(n // ch)]
        return parts[0] if len(parts) == 1 else jnp.concatenate(parts, axis=1)

    return _rowwise(name, fn, T, tm, tiles=(a,), consts=(w,), outs=((n, F32),))[0]


def _mm_tn(name, a, b, *, mod=None, tt=512):
    T, k1 = a.shape
    k2 = b.shape[1]
    t1 = k1 if k1 <= 1536 else _pick(k1, (1408, 1024, 512, 256, 128))
    t2 = k2 if k2 <= 1536 else _pick(k2, (1408, 1024, 640, 512, 256, 128))
    tt = min(tt, S)
    spb = S // tt

    def body(*refs):
        if mod is not None:
            a_ref, sc_ref, sh_ref, b_ref, o_ref = refs
        else:
            a_ref, b_ref, o_ref = refs
        t = pl.program_id(2)

        @pl.when(t == 0)
        def _():
            o_ref[...] = jnp.zeros_like(o_ref)

        av = a_ref[...]
        if mod is not None:
            av = av * (1.0 + sc_ref[0]) + sh_ref[0]
        o_ref[...] += _dot_tn(av.astype(BF16), b_ref[...].astype(BF16))

    in_specs = [pl.BlockSpec((tt, t1), lambda p, q, t: (t, p))]
    args = [a]
    if mod is not None:
        for v in mod:
            in_specs.append(pl.BlockSpec((1, 1, t1), lambda p, q, t: (t // spb, 0, p)))
            args.append(v)
    in_specs.append(pl.BlockSpec((tt, t2), lambda p, q, t: (t, q)))
    args.append(b)
    return pl.pallas_call(
        body, name=name, grid=(k1 // t1, k2 // t2, T // tt), in_specs=in_specs,
        out_specs=pl.BlockSpec((t1, t2), lambda p, q, t: (p, q)),
        out_shape=jax.ShapeDtypeStruct((k1, k2), F32),
        compiler_params=_cparams(("parallel", "parallel", "arbitrary")),
    )(*args)


def _mod_mm(name, x, sc, sh, ws, out_dtypes, rope=None, rope_secs=(), tm=256):
    T = x.shape[0]
    nw = len(ws)

    def fn(i, xv, *rest):
        if rope is not None:
            cv, s1v, s2v = rest[:3]
            rest = rest[3:]
        scv, shv = rest[:2]
        w_refs = rest[2:]
        h = (xv * (1.0 + scv) + shv).astype(BF16)
        res = []
        for k, w_ref in enumerate(w_refs):
            n = w_ref.shape[1]
            ch = WA if (k == 0 and rope is not None) else _pick(n, (512, 256, 128))
            parts = []
            for c in range(n // ch):
                z = _dot(h, w_ref[:, c * ch:(c + 1) * ch])
                if k == 0 and c in rope_secs:
                    z = _rope(z, _tile_lanes(cv, ch), _tile_lanes(s1v, ch), _tile_lanes(s2v, ch))
                parts.append(z.astype(out_dtypes[k]))
            res.append(parts[0] if len(parts) == 1 else jnp.concatenate(parts, axis=1))
        return tuple(res)

    tiles = (x,) + (tuple(rope) if rope is not None else ())
    outs = tuple((w.shape[1], dt) for w, dt in zip(ws, out_dtypes))
    return _rowwise(name, fn, T, tm, tiles=tiles, seqvecs=(sc, sh), consts=tuple(ws), outs=outs)


def _tri(tb, lower):
    r = lax.broadcasted_iota(jnp.int32, (tb, tb), 0)
    c = lax.broadcasted_iota(jnp.int32, (tb, tb), 1)
    return jnp.where((r >= c) if lower else (r <= c), 1.0, 0.0).astype(F32)


def _cumsum_seq(name, v, fn_in, fn_out, reverse, extra=(), n_acc=0, tb=512):
    T = v.shape[0]
    tb = min(tb, S)
    nbs = S // tb
    nseq = T // S

    def blk(b, j):
        return (b * nbs + (nbs - 1 - j if reverse else j), 0)

    def body(*refs):
        v_ref = refs[0]
        e_refs = refs[1:1 + len(extra)]
        o_ref = refs[1 + len(extra)]
        acc_refs = refs[2 + len(extra):2 + len(extra) + n_acc]
        carry = refs[-1]
        b, j = pl.program_id(0), pl.program_id(1)

        @pl.when(j == 0)
        def _():
            carry[...] = jnp.zeros_like(carry)

        ev = [e[...] for e in e_refs]
        xin = fn_in(v_ref[...], *ev)
        cum = jnp.dot(_tri(tb, not reverse), xin, preferred_element_type=F32,
                      precision=lax.Precision.HIGHEST) + carry[0:1, :]
        carry[...] = carry[...] + jnp.sum(xin, axis=0, keepdims=True)
        out = fn_out(cum, v_ref[...], *ev)
        o_ref[...] = out
        for a in acc_refs:
            @pl.when((b == 0) & (j == 0))
            def _():
                a[...] = jnp.zeros_like(a)

            a[...] += _rsum8(out)

            @pl.when((b == nseq - 1) & (j == nbs - 1))
            def _():
                a[...] = jnp.broadcast_to(jnp.sum(a[...], axis=0, keepdims=True), a.shape)

    in_specs = [pl.BlockSpec((tb, FPAD), blk)]
    for e in extra:
        if e.shape[0] == T:
            in_specs.append(pl.BlockSpec((tb, FPAD), blk))
        else:
            in_specs.append(pl.BlockSpec(e.shape, lambda b, j: (0, 0)))
    out_shape = [jax.ShapeDtypeStruct((T, FPAD), F32)] + [jax.ShapeDtypeStruct((SUBLANES, FPAD), F32)] * n_acc
    out_specs = [pl.BlockSpec((tb, FPAD), blk)] + [pl.BlockSpec((SUBLANES, FPAD), lambda b, j: (0, 0))] * n_acc
    return pl.pallas_call(
        body, name=name, grid=(nseq, nbs), in_specs=in_specs, out_specs=out_specs, out_shape=out_shape,
        scratch_shapes=[pltpu.VMEM((SUBLANES, FPAD), F32)],
        compiler_params=_cparams(("arbitrary", "arbitrary")),
    )(v, *extra)


def _log_sigmoid(x):
    return jnp.minimum(x, 0.0) - jnp.log(1.0 + jnp.exp(-jnp.abs(x)))


def _dil_bias(tq):
    max_win = max(w for w, _ in PATTERNS)
    nd = (max_win + tq - 1) // tq + 1
    qi = np.arange(tq)[:, None]
    kj = np.arange(tq)[None, :]
    tabs = []
    for dlt in range(nd):
        dist = dlt * tq + qi - kj
        mult = np.zeros((tq, tq), np.float64)
        for win, dil in PATTERNS:
            mult += (dist >= 0) & (dist % dil == 0) & (dist // dil <= win // dil)
        tabs.append(np.where(mult > 0, np.log(np.maximum(mult, 1.0)), NEG))
    return np.stack(tabs).astype(np.float32)


def _fold_operands(fcum, nha):
    T = fcum.shape[0]

    def split3(f):
        hi = lax.reduce_precision(f, 8, 7)
        mid = lax.reduce_precision(f - hi, 8, 7)
        lo = lax.reduce_precision(f - hi - mid, 8, 7)
        return jnp.stack([hi, mid, lo], axis=1)

    one3 = jnp.ones((T, 3), F32)
    pad = lambda t, before, width: jnp.pad(t, ((0, 0), (before, width - before - t.shape[1])))
    eqs, eks = [], []
    for hp in range(nha // 2):
        p0, p1 = split3(fcum[:, 2 * hp]), split3(fcum[:, 2 * hp + 1])
        eqs += [pad(jnp.concatenate([p0, one3], axis=1), 0, LANES), pad(jnp.concatenate([p1, one3], axis=1), 8, LANES)]
        eks.append(pad(jnp.concatenate([one3, -p0, jnp.zeros((T, 2), F32), one3, -p1], axis=1), 0, LANES))
    return jnp.concatenate(eqs, axis=1).astype(BF16), jnp.concatenate(eks, axis=1).astype(BF16)


def _stack_heads(x2, h0, extra=None):
    z = jnp.zeros_like(x2)
    a, b = jnp.where(h0, x2, z), jnp.where(h0, z, x2)
    if extra is not None:
        a = jnp.concatenate([a, extra[:, :LANES]], axis=1)
        b = jnp.concatenate([b, extra[:, LANES:]], axis=1)
    return jnp.concatenate([a, b], axis=0)


def _attn_fwd(name, qkv, secs, fox, eq=None, ek=None, bias=None):
    T = qkv.shape[0]
    nq = S // TQ
    nbl = T // S
    hp_n = WA // LANES
    sq, sk, sv = (s * hp_n for s in secs)
    scale = HD ** -0.5
    nd = None if fox else bias.shape[0]

    def body(*refs):
        if fox:
            q_ref, k_ref, v_ref, eq_ref, ek_ref, o_ref, lse_ref = refs
        else:
            q_ref, k_ref, v_ref, b_ref, o_ref, lse_ref = refs
        i = pl.program_id(2)
        lane = lax.broadcasted_iota(jnp.int32, (1, LANES), 1)
        h0 = lane < HD
        q2 = (q_ref[...].astype(F32) * scale).astype(BF16)
        qs = _stack_heads(q2, h0, eq_ref[...] if fox else None)

        def scores(t, diag):
            off = pl.multiple_of((i - t) * TQ, TQ)
            kk = k_ref[pl.ds(off, TQ), :]
            if fox:
                kk = jnp.concatenate([kk, ek_ref[pl.ds(off, TQ), :]], axis=1)
            s = jnp.concatenate([_dot_nt(qs[:TQ], kk), _dot_nt(qs[TQ:], kk)], axis=0)
            if not fox:
                s = (s.reshape(2, TQ, TQ) + b_ref[t]).reshape(2 * TQ, TQ)
            elif diag:
                rows = lax.broadcasted_iota(jnp.int32, (2, TQ, TQ), 1).reshape(2 * TQ, TQ)
                cols = lax.broadcasted_iota(jnp.int32, (2 * TQ, TQ), 1)
                s = jnp.where(cols <= rows, s, NEG)
            return s

        def update(t, s, m, l, acc):
            off = pl.multiple_of((i - t) * TQ, TQ)
            v2 = v_ref[pl.ds(off, TQ), :]
            m_new = jnp.maximum(m, jnp.max(s, axis=1, keepdims=True))
            p = jnp.exp(s - m_new)
            a = jnp.exp(m - m_new)
            l = a * l + jnp.sum(p, axis=1, keepdims=True)
            pb = p.astype(BF16)
            acc = a * acc + jnp.concatenate([_dot(pb[:TQ], v2), _dot(pb[TQ:], v2)], axis=0)
            return m_new, l, acc

        def step(t, carry):
            s, m, l, acc = carry
            return (scores(t + 1, False),) + update(t, s, m, l, acc)

        init = (jnp.full((2 * TQ, 1), NEG, F32), jnp.zeros((2 * TQ, 1), F32), jnp.zeros((2 * TQ, LANES), F32))
        n = i + 1 if fox else jnp.minimum(i + 1, nd)
        s, m, l, acc = lax.fori_loop(0, n - 1, step, (scores(0, True),) + init)
        m, l, acc = update(n - 1, s, m, l, acc)
        on = acc / l
        o_ref[...] = jnp.where(h0, on[:TQ], on[TQ:])
        lse = jnp.broadcast_to(m + jnp.log(l), (2 * TQ, LANES))
        lse_ref[...] = jnp.concatenate([lse[:TQ], lse[TQ:]], axis=1)

    in_specs = [
        pl.BlockSpec((TQ, LANES), lambda b, hp, i: (b * nq + i, sq + hp)),
        pl.BlockSpec((S, LANES), lambda b, hp, i: (b, sk + hp)),
        pl.BlockSpec((S, LANES), lambda b, hp, i: (b, sv + hp)),
    ]
    args = [qkv, qkv, qkv]
    if fox:
        in_specs += [pl.BlockSpec((TQ, 2 * LANES), lambda b, hp, i: (b * nq + i, hp)),
                     pl.BlockSpec((S, LANES), lambda b, hp, i: (b, hp))]
        args += [eq, ek]
    else:
        in_specs.append(pl.BlockSpec(bias.shape, lambda b, hp, i: (0, 0, 0)))
        args.append(bias)
    return pl.pallas_call(
        body, name=name, grid=(nbl, hp_n, nq), in_specs=in_specs,
        out_specs=[pl.BlockSpec((TQ, LANES), lambda b, hp, i: (b * nq + i, hp)),
                   pl.BlockSpec((TQ, 2 * LANES), lambda b, hp, i: (b * nq + i, hp))],
        out_shape=[jax.ShapeDtypeStruct((T, WA), F32), jax.ShapeDtypeStruct((T, 2 * WA), F32)],
        compiler_params=_cparams(("parallel", "parallel", "arbitrary")),
    )(*args)


def _attn_bwd(name, qkv, secs, o, do, do_sec, lse, fox, eq=None, ek=None, bias=None):
    T = qkv.shape[0]
    nq = S // TQ
    nbl = T // S
    hp_n = WA // LANES
    sq, sk, sv = (s * hp_n for s in secs)
    dsec = do_sec * hp_n
    scale = HD ** -0.5
    nd = None if fox else bias.shape[0]
    kc = 2 * LANES if fox else LANES

    def body(*refs):
        if fox:
            (q_ref, k_ref, v_ref, o_ref, do_ref, lse_ref, eq_ref, ek_ref,
             dq_ref, dk_ref, dv_ref, dqe_ref, dek_ref, dl_ref) = refs
        else:
            q_ref, k_ref, v_ref, o_ref, do_ref, lse_ref, b_ref, dq_ref, dk_ref, dv_ref, dl_ref = refs
        j = pl.program_id(2)
        lane = lax.broadcasted_iota(jnp.int32, (1, LANES), 1)
        h0 = lane < HD

        @pl.when(j == 0)
        def _():
            dq_ref[...] = jnp.zeros_like(dq_ref)
            if fox:
                dqe_ref[...] = jnp.zeros_like(dqe_ref)

            def dl_step(r, c):
                off = pl.multiple_of(r * TQ, TQ)
                d2 = do_ref[pl.ds(off, TQ), :] * o_ref[pl.ds(off, TQ), :]
                z2 = jnp.zeros_like(d2)
                dl0 = jnp.sum(jnp.where(h0, d2, z2), axis=1, keepdims=True)
                dl1 = jnp.sum(jnp.where(h0, z2, d2), axis=1, keepdims=True)
                dl_ref[pl.ds(off, TQ), :] = jnp.concatenate(
                    [jnp.broadcast_to(dl0, (TQ, LANES)), jnp.broadcast_to(dl1, (TQ, LANES))], axis=1)
                return c

            lax.fori_loop(0, nq, dl_step, 0)

        kk = k_ref[...]
        if fox:
            kk = jnp.concatenate([kk, ek_ref[...]], axis=1)
        v2 = v_ref[...]

        def wide(x2):
            st = jnp.concatenate([x2[:, :LANES], x2[:, LANES:]], axis=0)
            return st if TQ == LANES else jnp.concatenate([st] * (TQ // LANES), axis=1)

        def step(t, carry, diag):
            dkk, dv2 = carry
            off = pl.multiple_of((j + t) * TQ, TQ)
            q2 = (q_ref[pl.ds(off, TQ), :].astype(F32) * scale).astype(BF16)
            qs = _stack_heads(q2, h0, eq_ref[pl.ds(off, TQ), :] if fox else None)
            dos = _stack_heads(do_ref[pl.ds(off, TQ), :].astype(BF16), h0)
            s = jnp.concatenate([_dot_nt(qs[:TQ], kk), _dot_nt(qs[TQ:], kk)], axis=0)
            if not fox:
                s = (s.reshape(2, TQ, TQ) + b_ref[t]).reshape(2 * TQ, TQ)
            elif diag:
                rows = lax.broadcasted_iota(jnp.int32, (2, TQ, TQ), 1).reshape(2 * TQ, TQ)
                cols = lax.broadcasted_iota(jnp.int32, (2 * TQ, TQ), 1)
                s = jnp.where(cols <= rows, s, NEG)
            p = jnp.exp(s - wide(lse_ref[pl.ds(off, TQ), :]))
            dp = jnp.concatenate([_dot_nt(dos[:TQ], v2), _dot_nt(dos[TQ:], v2)], axis=0)
            dsb = (p * (dp - wide(dl_ref[pl.ds(off, TQ), :]))).astype(BF16)
            dv2 = dv2 + _dot_tn(p.astype(BF16), dos)
            dkk = dkk + _dot_tn(dsb, qs)
            dqq = jnp.concatenate([_dot(dsb[:TQ], kk), _dot(dsb[TQ:], kk)], axis=0)
            dq_ref[pl.ds(off, TQ), :] += jnp.where(h0, dqq[:TQ, :LANES], dqq[TQ:, :LANES])
            if fox:
                dqe_ref[pl.ds(off, TQ), :] += jnp.where(lane < SUBLANES, dqq[:TQ, LANES:], dqq[TQ:, LANES:])
            return dkk, dv2

        zero = (jnp.zeros((TQ, kc), F32), jnp.zeros((TQ, LANES), F32))
        if fox:
            dkk, dv2 = lax.fori_loop(1, nq - j, lambda t, c: step(t, c, False), step(0, zero, True))
        else:
            dkk, dv2 = lax.fori_loop(0, jnp.minimum(nq - j, nd), lambda t, c: step(t, c, False), zero)
        dk_ref[...] = dkk[:, :LANES]
        dv_ref[...] = dv2
        if fox:
            dek_ref[...] = dkk[:, LANES:]

        @pl.when(j == nq - 1)
        def _():
            dq_ref[...] = dq_ref[...] * scale

    seq = lambda c, w=LANES: pl.BlockSpec((S, w), lambda b, hp, j: (b, c + hp))
    blk = lambda c: pl.BlockSpec((TQ, LANES), lambda b, hp, j: (b * nq + j, c + hp))
    in_specs = [seq(sq), blk(sk), blk(sv), seq(0), seq(dsec), seq(0, 2 * LANES)]
    args = [qkv, qkv, qkv, o, do, lse]
    if fox:
        in_specs += [seq(0, 2 * LANES), blk(0)]
        args += [eq, ek]
    else:
        in_specs.append(pl.BlockSpec(bias.shape, lambda b, hp, j: (0, 0, 0)))
        args.append(bias)
    out_specs = [seq(0), blk(0), blk(0)]
    out_shape = [jax.ShapeDtypeStruct((T, WA), F32)] * 3
    if fox:
        out_specs += [seq(0), blk(0)]
        out_shape += [jax.ShapeDtypeStruct((T, WA), F32)] * 2
    return pl.pallas_call(
        body, name=name, grid=(nbl, hp_n, nq), in_specs=in_specs, out_specs=out_specs, out_shape=out_shape,
        scratch_shapes=[pltpu.VMEM((S, 2 * LANES), F32)],
        compiler_params=_cparams(("parallel", "parallel", "arbitrary")),
    )(*args)


def _exchange(name, ins, out_shapes, remote, local):
    n_in, n_out = len(ins), len(out_shapes)
    nr, nl = len(remote), len(local)

    def body(*refs):
        in_refs = refs[:n_in]
        out_refs = refs[n_in:n_in + n_out]
        send_sems, recv_sems, loc_sems = refs[n_in + n_out:]
        me = (lax.axis_index("x"), lax.axis_index("y"), lax.axis_index("c"))

        def peer_of(flip):
            return tuple(1 - v if f else v for v, f in zip(me, flip))

        def at(ref, idx):
            return ref if idx is None else ref.at[idx]

        def rcopy(k, who):
            flip, a, sfn, b, dfn = remote[k]
            return pltpu.make_async_remote_copy(
                src_ref=at(in_refs[a], sfn(*who)), dst_ref=at(out_refs[b], dfn(*who)),
                send_sem=send_sems.at[k], recv_sem=recv_sems.at[k],
                device_id=peer_of(flip), device_id_type=MESH)

        locs = [pltpu.make_async_copy(at(in_refs[a], sfn(*me)), at(out_refs[b], dfn(*me)), loc_sems.at[k])
                for k, (a, sfn, b, dfn) in enumerate(local)]
        for cp in locs:
            cp.start()
        sends = [rcopy(k, me) for k in range(nr)]
        for cp in sends:
            cp.start()
        for k in range(nr):
            rcopy(k, peer_of(remote[k][0])).wait_recv()
        for cp in sends:
            cp.wait_send()
        for cp in locs:
            cp.wait()

    any_spec = pl.BlockSpec(memory_space=pl.ANY)
    return pl.pallas_call(
        body, name=name, in_specs=[any_spec] * n_in, out_specs=[any_spec] * n_out, out_shape=list(out_shapes),
        scratch_shapes=[pltpu.SemaphoreType.DMA((max(nr, 1),)), pltpu.SemaphoreType.DMA((max(nr, 1),)),
                        pltpu.SemaphoreType.DMA((max(nl, 1),))],
    )(*ins)


_FLIPS7 = [(0, 0, 1), (0, 1, 0), (0, 1, 1), (1, 0, 0), (1, 0, 1), (1, 1, 0), (1, 1, 1)]
_CHIP_FLIPS = [(1, 0, 0), (0, 1, 0), (1, 1, 0)]


def _dev_index(x, y, c):
    return 4 * x + 2 * y + c


def _chip_index(x, y, c):
    return 2 * x + y


def _all_gather8(name, v):
    remote = [(f, 0, lambda x, y, c: None, 0, _dev_index) for f in _FLIPS7]
    local = [(0, lambda x, y, c: None, 0, _dev_index)]
    return _exchange(name, [v], [jax.ShapeDtypeStruct((NDEV,) + v.shape, v.dtype)], remote, local)[0]


def _gather_halves(name, vs):
    n_v = len(vs)

    def body(*refs):
        in_refs, out_refs = refs[:n_v], refs[n_v:2 * n_v]
        send_sems, recv_sems = refs[2 * n_v:]
        x, y, c = lax.axis_index("x"), lax.axis_index("y"), lax.axis_index("c")
        sibling = (x, y, 1 - c)
        chips = [(1 - x, y), (x, 1 - y), (1 - x, 1 - y)]

        def copy(k, n, src, blk, half, to):
            return pltpu.make_async_remote_copy(
                src_ref=src, dst_ref=out_refs[n].at[blk, half], send_sem=send_sems.at[k], recv_sem=recv_sems.at[k],
                device_id=to, device_id_type=MESH)

        first = [copy(6 * n + j, n, in_refs[n].at[c], j, c, (*chip, c))
                 for n in range(n_v) for j, chip in enumerate(chips)]
        for cp in first:
            cp.start()
        passed = []
        for n in range(n_v):
            for j, chip in enumerate(chips):
                copy(6 * n + j, n, in_refs[n].at[c], j, c, (*chip, c)).wait_recv()
                fw = copy(6 * n + 3 + j, n, out_refs[n].at[j, c], j, c, sibling)
                fw.start()
                passed.append(fw)
        for n in range(n_v):
            for j in range(len(chips)):
                copy(6 * n + 3 + j, n, out_refs[n].at[j, 1 - c], j, 1 - c, sibling).wait_recv()
        for cp in first + passed:
            cp.wait_send()

    any_spec = pl.BlockSpec(memory_space=pl.ANY)
    return pl.pallas_call(
        body, name=name, in_specs=[any_spec] * n_v, out_specs=[any_spec] * n_v,
        out_shape=[jax.ShapeDtypeStruct((NCHIP - 1,) + v.shape, v.dtype) for v in vs],
        scratch_shapes=[pltpu.SemaphoreType.DMA((6 * n_v,)), pltpu.SemaphoreType.DMA((6 * n_v,))],
    )(*vs)


def _to_sibling(name, v):
    remote = [((0, 0, 1), 0, lambda x, y, c: None, 0, lambda x, y, c: None)]
    return _exchange(name, [v], [jax.ShapeDtypeStruct(v.shape, v.dtype)], remote, [])[0]


def _scatter_chips(name, v):
    remote = []
    for j, f in enumerate(_CHIP_FLIPS):
        src = lambda x, y, c, f=f: _chip_index(1 - x if f[0] else x, 1 - y if f[1] else y, c)
        remote.append((f, 0, src, 0, lambda x, y, c, j=j: j))
    return _exchange(name, [v], [jax.ShapeDtypeStruct((NCHIP - 1,) + v.shape[1:], v.dtype)], remote, [])[0]


def _by_chip(own, others, chip):
    stacked = jnp.concatenate([own[None], others], axis=0)
    blocks = []
    for k in range(NCHIP):
        d = k ^ chip
        place = jnp.where(d == 0, 0, jnp.where(d == 2, 1, jnp.where(d == 1, 2, 3)))
        blocks.append(lax.dynamic_index_in_dim(stacked, place, axis=0, keepdims=False))
    return jnp.stack(blocks)


def _sum_leading(name, v, tm=None):
    n, r, w = v.shape
    tm = _pick(r, (256, 128, 64, 32, 16, 8)) if tm is None else tm

    def body(v_ref, o_ref):
        acc = v_ref[0]
        for k in range(1, n):
            acc = acc + v_ref[k]
        o_ref[...] = acc

    return pl.pallas_call(
        body, name=name, grid=(r // tm,), in_specs=[pl.BlockSpec((n, tm, w), lambda i: (0, i, 0))],
        out_specs=pl.BlockSpec((tm, w), lambda i: (i, 0)), out_shape=jax.ShapeDtypeStruct((r, w), F32),
        compiler_params=_cparams(("parallel",)),
    )(v)


def _add2(name, a, b, tm=None):
    r, w = a.shape
    tm = _pick(r, (256, 128, 64, 32, 16, 8)) if tm is None else tm

    def body(a_ref, b_ref, o_ref):
        o_ref[...] = a_ref[...] + b_ref[...]

    spec = pl.BlockSpec((tm, w), lambda i: (i, 0))
    return pl.pallas_call(
        body, name=name, grid=(r // tm,), in_specs=[spec, spec], out_specs=spec,
        out_shape=jax.ShapeDtypeStruct((r, w), F32), compiler_params=_cparams(("parallel",)),
    )(a, b)


def _ada_fwd(call_all, w_shard):
    def body(c_ref, w_ref, o_ref):
        cv = c_ref[...]
        o_ref[...] = jnp.dot(cv * _sigmoid(cv), w_ref[...], preferred_element_type=F32,
                             precision=lax.Precision.HIGHEST)

    n = w_shard.shape[1]
    return pl.pallas_call(
        body, name="ada_fwd", out_shape=jax.ShapeDtypeStruct((call_all.shape[0], n), F32),
        compiler_params=pltpu.CompilerParams(vmem_limit_bytes=VMEM_LIMIT),
    )(call_all, w_shard)


def _ada_bwd(call_all, dada):
    def body(c_ref, d_ref, o_ref):
        cv = c_ref[...]
        o_ref[...] = lax.dot_general(cv * _sigmoid(cv), d_ref[...], (((0,), (0,)), ((), ())),
                                     preferred_element_type=F32, precision=lax.Precision.HIGHEST)

    return pl.pallas_call(
        body, name="ada_bwd", out_shape=jax.ShapeDtypeStruct((call_all.shape[1], dada.shape[1]), F32),
        compiler_params=pltpu.CompilerParams(vmem_limit_bytes=VMEM_LIMIT),
    )(call_all, dada)


def _adamw(name, w, g, m, v):
    r, wd = w.shape
    tm = _pick(r, (256, 128, 64, 32, 16, 8))
    bc1 = 1.0 - ADAM_B1 ** ADAM_STEP
    bc2 = 1.0 - ADAM_B2 ** ADAM_STEP

    def body(w_ref, g_ref, m_ref, v_ref, d_ref, mo_ref, vo_ref):
        gv = g_ref[...]
        mn = ADAM_B1 * m_ref[...] + (1.0 - ADAM_B1) * gv
        vn = ADAM_B2 * v_ref[...] + (1.0 - ADAM_B2) * (gv * gv)
        d_ref[...] = -ADAM_LR * ((mn / bc1) / (jnp.sqrt(vn / bc2) + ADAM_EPS) + ADAM_WD * w_ref[...])
        mo_ref[...] = mn
        vo_ref[...] = vn

    spec = pl.BlockSpec((tm, wd), lambda i: (i, 0))
    return pl.pallas_call(
        body, name=name, grid=(r // tm,), in_specs=[spec] * 4, out_specs=[spec] * 3,
        out_shape=[jax.ShapeDtypeStruct((r, wd), F32)] * 3, compiler_params=_cparams(("parallel",)),
    )(w, g, m, v)


def _rope_tables(positions):
    half = ROPE_DIMS // 2
    freqs = ROPE_THETA ** (-jnp.arange(0, ROPE_DIMS, 2, dtype=F32) / ROPE_DIMS)
    ang = positions.astype(F32).reshape(-1, 1) * freqs
    cos, sin = jnp.cos(ang), jnp.sin(ang)
    T = ang.shape[0]
    one = jnp.ones((T, HD - ROPE_DIMS), F32)
    zero = jnp.zeros((T, HD - ROPE_DIMS), F32)
    zh = jnp.zeros((T, half), F32)
    c64 = jnp.concatenate([cos, cos, one], axis=1)
    s1 = jnp.concatenate([zh, sin, zero], axis=1)
    s2 = jnp.concatenate([-sin, zh, zero], axis=1)
    rep = lambda t: jnp.concatenate([t] * (LANES // HD), axis=1)
    return rep(c64), rep(s1), rep(s2)


def _local_step(x, loss_target, positions, ada, w_qkv, w_f, w_out, w_up, conv_w8, w_down,
                b_fgate, gn, ln1_g, ln1_b, conv_b, ln2_g, ln2_b):
    T = x.shape[0]
    nbl = T // S
    nha = WA // HD
    sv = lambda k: ada[:, k:k + 1, :]
    sh_a, sc_a, g_a, sh_f, sc_f, g_f = (sv(k) for k in range(6))
    rope = _rope_tables(positions)
    neg_rope = (rope[0], -rope[1], -rope[2])
    gseg = jnp.asarray(np.kron(np.eye(min(256, 2 * WA) // HD), np.ones((HD, HD))), BF16)
    bias = jnp.asarray(_dil_bias(TQ))
    bf_pad = jnp.zeros((1, FPAD), F32).at[:, :nha].set(b_fgate)

    qkv, fa = _mod_mm("qkv_proj", x, sc_a, sh_a, (w_qkv, w_f), (BF16, F32), rope=rope, rope_secs=(3, 4))
    fcum = _cumsum_seq("fgate_fwd", fa, lambda f, b: _log_sigmoid(f + b), lambda cum, f, b: cum,
                       reverse=False, extra=(bf_pad,))[0]
    eq, ek = _fold_operands(fcum, nha)
    oa, lse_a = _attn_fwd("fox_fwd", qkv, (0, 1, 2), True, eq=eq, ek=ek)
    ob, lse_b = _attn_fwd("dil_fwd", qkv, (3, 4, 5), False, bias=bias)

    def mix_fn(i, oav, obv, xv, gav, gn_ref, g_ref, wo_ref, l1g_ref, l1b_ref):
        o = jnp.concatenate([oav, obv], axis=1)
        rs = lax.rsqrt(_head_mean(o * o, g_ref) + RMS_EPS)
        merged = (o * rs * gn_ref[...]).astype(BF16)
        mix = _dot(merged, wo_ref[...])
        x1, _, _ = _ln_fwd(ALPHA * xv + gav * mix, l1g_ref[...], l1b_ref[...])
        return merged, mix, x1

    merged, mix, x1 = _rowwise("mix_out", mix_fn, T, 256, tiles=(oa, ob, x), seqvecs=(g_a,),
                               consts=(gn, gseg, w_out, ln1_g, ln1_b),
                               outs=((2 * WA, BF16), (D, F32), (D, F32)))
    u = _mod_mm("ffn_up", x1, sc_f, sh_f, (w_up,), (F32,))[0]

    def conv_y(i, uv, prev, cw_ref, cb_ref, tm):
        first = (i * tm) % S == 0
        s1, s2 = _conv_taps(uv, prev, first)
        y = cb_ref[...] + cw_ref[0:1, :] * s2 + cw_ref[1:2, :] * s1 + cw_ref[2:3, :] * uv
        return y, s1, s2

    tmc = 128

    def gate_fn(i, uv, prev, cw_ref, cb_ref):
        y, _, _ = conv_y(i, uv, prev, cw_ref, cb_ref, tmc)
        a, g = y[:, :DFF], y[:, DFF:]
        return g * _sigmoid(g) * a

    act = _rowwise("conv_gate", gate_fn, T, tmc, tiles=(u,), halos=((u, -1),), consts=(conv_w8, conv_b),
                   outs=((DFF, BF16),))[0]

    def down_fn(i, actv, x1v, tgt, gfv, wd_ref, g2_ref, b2_ref):
        ffn = _dot(actv, wd_ref[...])
        y, n2, rstd = _ln_fwd(ALPHA * x1v + gfv * ffn, g2_ref[...], b2_ref[...])
        err = y - tgt
        dy = err * (1.0 / D)
        dr2 = _ln_bwd(dy, n2, rstd, g2_ref[...])
        return (dr2, gfv * dr2, _rsum8(err * err), _rsum8(dy * n2), _rsum8(dy), _rsum8(dr2 * ffn))

    dr2, dffn, loss_acc, d_ln2g, d_ln2b, d_gf = _rowwise(
        "ffn_down_loss", down_fn, T, 256, tiles=(act, x1, loss_target), seqvecs=(g_f,),
        consts=(w_down, ln2_g, ln2_b), outs=((D, F32), (D, BF16)), accs=(D, D, D), seqaccs=(D,))

    dact = _mm_nt("dact", dffn, w_down)

    def gate_bwd_fn(i, uv, dav, prev, cw_ref, cb_ref):
        y, s1, s2 = conv_y(i, uv, prev, cw_ref, cb_ref, tmc)
        a, g = y[:, :DFF], y[:, DFF:]
        sg = _sigmoid(g)
        dyc = jnp.concatenate([dav * (g * sg), dav * a * (sg * (1.0 + g * (1.0 - sg)))], axis=1)
        return dyc, _rsum8(dyc), _rsum8(dyc * s2), _rsum8(dyc * s1), _rsum8(dyc * uv)

    dyc, d_cb, d_cw0, d_cw1, d_cw2 = _rowwise(
        "gate_bwd", gate_bwd_fn, T, tmc, tiles=(u, dact), halos=((u, -1),), consts=(conv_w8, conv_b),
        outs=((2 * DFF, F32),), accs=(2 * DFF,) * 4)

    def conv_bwd_fn(i, dv, nxt, cw_ref):
        last = ((i + 1) * tmc) % S == 0
        u1, u2 = _conv_taps_up(dv, nxt, last)
        return cw_ref[2:3, :] * dv + cw_ref[1:2, :] * u1 + cw_ref[0:1, :] * u2

    du = _rowwise("conv_bwd", conv_bwd_fn, T, tmc, tiles=(dyc,), halos=((dyc, 1),), consts=(conv_w8,),
                  outs=((2 * DFF, BF16),))[0]
    dh2 = _mm_nt("dh2", du, w_up)
    g_w_down = _mm_tn("dw_down", act, dffn)
    g_w_up = _mm_tn("dw_up", x1, du, mod=(sc_f, sh_f))

    def ln1_bwd_fn(i, dr2v, dh2v, xv, mixv, x1v, scfv, gav, l1g_ref):
        dx1 = ALPHA * dr2v + dh2v * (1.0 + scfv)
        _, n1, rstd = _ln_fwd(ALPHA * xv + gav * mixv, l1g_ref[...], 0.0)
        dr1 = _ln_bwd(dx1, n1, rstd, l1g_ref[...])
        return (dr1, gav * dr1, _rsum8(dx1 * n1), _rsum8(dx1),
                _rsum8(dh2v * x1v), _rsum8(dh2v), _rsum8(dr1 * mixv))

    dr1, dmix, d_ln1g, d_ln1b, d_scf, d_shf, d_ga = _rowwise(
        "ln1_bwd", ln1_bwd_fn, T, 256, tiles=(dr2, dh2, x, mix, x1), seqvecs=(sc_f, g_a), consts=(ln1_g,),
        outs=((D, F32), (D, BF16)), accs=(D, D), seqaccs=(D, D, D))

    dmerged = _mm_nt("dmerged", dmix, w_out)
    g_w_out = _mm_tn("dw_out", merged, dmix)

    def hn_bwd_fn(i, dmv, oav, obv, gn_ref, g_ref):
        o = jnp.concatenate([oav, obv], axis=1)
        rs = lax.rsqrt(_head_mean(o * o, g_ref) + RMS_EPS)
        nrm = o * rs
        dn = dmv * gn_ref[...]
        do = rs * (dn - nrm * _head_mean(dn * nrm, g_ref))
        return do, _rsum8(dmv * nrm)

    do, d_gn = _rowwise("headnorm_bwd", hn_bwd_fn, T, 256, tiles=(dmerged, oa, ob), consts=(gn, gseg),
                        outs=((2 * WA, F32),), accs=(2 * WA,))
    dqa, dka, dva, dqe, dek = _attn_bwd("fox_bwd", qkv, (0, 1, 2), oa, do, 0, lse_a, True, eq=eq, ek=ek)
    dqb, dkb, dvb = _attn_bwd("dil_bwd", qkv, (3, 4, 5), ob, do, 1, lse_b, False, bias=bias)
    dftok = jnp.stack([dqe[:, (h // 2) * LANES + 8 * (h % 2)] - dek[:, (h // 2) * LANES + 3 + 8 * (h % 2)]
                       for h in range(nha)], axis=1)
    dftok = jnp.pad(dftok, ((0, 0), (0, FPAD - nha)))
    dfa, d_bf = _cumsum_seq("fgate_bwd", dftok, lambda d, f, b: d,
                            lambda cum, d, f, b: cum * _sigmoid(-(f + b)),
                            reverse=True, extra=(fa, bf_pad), n_acc=1)

    def dz_fn(i, a0, a1, a2, b0, b1, b2, fv, cv, s1v, s2v):
        ct, s1t, s2t = (_tile_lanes(t, WA) for t in (cv, s1v, s2v))
        return jnp.concatenate([a0, a1, a2, _rope(b0, ct, s1t, s2t), _rope(b1, ct, s1t, s2t), b2, fv], axis=1)

    dz = _rowwise("dz_pack", dz_fn, T, 256, tiles=(dqa, dka, dva, dqb, dkb, dvb, dfa) + neg_rope,
                  outs=((6 * WA + FPAD, BF16),))[0]
    w_cat = jnp.concatenate([w_qkv, w_f], axis=1)
    dh1 = _mm_nt("dh1", dz, w_cat)
    g_w_cat = _mm_tn("dw_in", x, dz, mod=(sc_a, sh_a))

    def dx_fn(i, dr1v, dh1v, xv, scav):
        return ALPHA * dr1v + dh1v * (1.0 + scav), _rsum8(dh1v * xv), _rsum8(dh1v)

    grad_x, d_sca, d_sha = _rowwise("dx_out", dx_fn, T, 256, tiles=(dr1, dh1, x), seqvecs=(sc_a,),
                                    outs=((D, F32),), seqaccs=(D, D))

    row0 = lambda a: a[..., 0, :]
    d_ada = jnp.stack([row0(d_sha), row0(d_sca), row0(d_ga), row0(d_shf), row0(d_scf), row0(d_gf)], axis=1)
    d_cw = jnp.stack([row0(d_cw0), row0(d_cw1), row0(d_cw2)], axis=0)
    loss_part = (0.5 / D) * jnp.sum(loss_acc[0])
    small = dict(b_fgate=row0(d_bf)[:nha], gn=row0(d_gn), ln1_g=row0(d_ln1g), ln1_b=row0(d_ln1b),
                 conv_b=row0(d_cb), ln2_g=row0(d_ln2g), ln2_b=row0(d_ln2b))
    big = dict(w_cat=g_w_cat, w_out=g_w_out, w_up=g_w_up, conv_w=d_cw, w_down=g_w_down)
    return loss_part, grad_x, d_ada, small, big


def _rows_of(n, w=None):
    return -(-n // (D if w is None else w))


def _as_rows(v):
    w = D
    k = v.shape[0]
    flat = v.reshape(k, -1)
    rows = _rows_of(_rows_of(flat.shape[1], w), SUBLANES) * SUBLANES
    flat = jnp.pad(flat, ((0, 0), (0, rows * w - flat.shape[1])))
    return flat.reshape(k, rows, w)


def kernel(x, c, positions, w_ada, b_ada, w_in, b_fgate, gn_a, gn_b, w_out, ln1_g, ln1_b, w_up, conv_w, conv_b, w_down, ln2_g, ln2_b, loss_target, m_w_ada, m_b_ada, m_w_in, m_b_fgate, m_gn_a, m_gn_b, m_w_out, m_ln1_g, m_ln1_b, m_w_up, m_conv_w, m_conv_b, m_w_down, m_ln2_g, m_ln2_b, v_w_ada, v_b_ada, v_w_in, v_b_fgate, v_gn_a, v_gn_b, v_w_out, v_ln1_g, v_ln1_b, v_w_up, v_conv_w, v_conv_b, v_w_down, v_ln2_g, v_ln2_b):
    mx, my, mc = lax.axis_index("x"), lax.axis_index("y"), lax.axis_index("c")
    dev = _dev_index(mx, my, mc)
    chip = _chip_index(mx, my, mc)
    nbl = x.shape[0]
    T = nbl * S
    nha = WA // HD
    d_in = w_in.shape[2] * NCHIP
    n_ada = w_ada.shape[2]

    c_pad = jnp.zeros((SUBLANES, D), F32).at[:nbl].set(c)
    c_all = _all_gather8("gather_c", c_pad)[:, :nbl].reshape(NDEV * nbl, D)
    ada_part = _ada_fwd(c_all, w_ada[0])
    n_cw = conv_w.shape[2]
    cw_rows = jnp.pad(conv_w[0], ((0, SUBLANES - conv_w.shape[1]), (0, n_ada - n_cw)))
    ada_blocks = _all_gather8("gather_ada", jnp.concatenate([ada_part, cw_rows], axis=0))
    n_c = NDEV * nbl
    ada_all = jnp.concatenate([ada_blocks[2 * k, :n_c] for k in range(NCHIP)], axis=1) + b_ada
    conv_w8 = jnp.concatenate([ada_blocks[2 * k, n_c:, :n_cw] for k in range(NCHIP)], axis=1)
    ada = lax.dynamic_slice_in_dim(ada_all, dev * nbl, nbl, axis=0).reshape(nbl, 6, D)

    w_in_sh = jnp.pad(w_in[0].astype(BF16), ((0, 0), (0, _rows_of(w_in.shape[2], LANES) * LANES - w_in.shape[2])))
    shards = [w_in_sh, w_out[0].astype(BF16), w_up[0].astype(BF16), w_down[0].astype(BF16)]
    halves = [t.reshape(2, t.shape[0] // 2, t.shape[1]) for t in shards]
    gathered_w = _gather_halves("gather_w", halves)
    g_in, g_out, g_up, g_down = (_by_chip(h, g, chip).reshape((NCHIP,) + t.shape)
                                 for g, h, t in zip(gathered_w, halves, shards))
    w_in_full = jnp.concatenate([g_in[k][:, :w_in.shape[2]] for k in range(NCHIP)], axis=1)
    w_qkv = jnp.concatenate([w_in_full[:, :3 * WA], w_in_full[:, 3 * WA + nha:]], axis=1)
    w_f = jnp.pad(w_in_full[:, 3 * WA:3 * WA + nha], ((0, 0), (0, FPAD - nha)))
    w_out_full = g_out.reshape(NCHIP * w_out.shape[1], D)
    w_up_full = jnp.concatenate([g_up[k] for k in range(NCHIP)], axis=1)
    w_down_full = g_down.reshape(NCHIP * w_down.shape[1], D)

    gn = jnp.concatenate([gn_a, gn_b], axis=1)
    loss_part, grad_x, d_ada, small, big = _local_step(
        x.reshape(T, D), loss_target.reshape(T, D), positions, ada, w_qkv, w_f, w_out_full, w_up_full, conv_w8,
        w_down_full, b_fgate, gn, ln1_g, ln1_b, conv_b, ln2_g, ln2_b)

    def row_pad(v, rows):
        flat = v.reshape(-1)
        return jnp.pad(flat, (0, rows * D - flat.shape[0]))

    n_cb = _rows_of(2 * DFF)
    small_flat = jnp.concatenate([
        row_pad(small["b_fgate"], 1), row_pad(small["gn"], 1), row_pad(small["ln1_g"], 1),
        row_pad(small["ln1_b"], 1), row_pad(small["ln2_g"], 1), row_pad(small["ln2_b"], 1),
        row_pad(jnp.full((1,), loss_part, F32), 1), row_pad(small["conv_b"], n_cb)])
    n_small = _rows_of(small_flat.shape[0], SUBLANES * D) * SUBLANES
    small_rows = jnp.pad(small_flat, (0, n_small * D - small_flat.shape[0])).reshape(n_small, D)
    ada_rows = jnp.pad(d_ada.reshape(nbl, 6, D), ((0, 0), (0, SUBLANES - 6), (0, 0))).reshape(nbl * SUBLANES, D)
    gathered = _all_gather8("gather_small", jnp.concatenate([small_rows, ada_rows], axis=0))
    red = _sum_leading("sum_small", gathered, tm=SUBLANES)
    g_b_fgate = red[0:1, :nha]
    g_gn = red[1:2, :2 * WA]
    g_ln1_g, g_ln1_b, g_ln2_g, g_ln2_b = red[2:3], red[3:4], red[4:5], red[5:6]
    loss = red[6, 0]
    g_conv_b = red[7:7 + n_cb].reshape(1, -1)[:, :2 * DFF]
    g_b_ada = _add2("sum_b_ada", red[n_small:n_small + SUBLANES], red[n_small + SUBLANES:n_small + 2 * SUBLANES],
                    tm=SUBLANES)[:6].reshape(1, 6 * D)
    dada_all = gathered[:, n_small:].reshape(NDEV, nbl, SUBLANES, D)[:, :, :6].reshape(NDEV * nbl, 6 * D)
    g_w_ada = _ada_bwd(c_all, lax.dynamic_slice_in_dim(dada_all, chip * n_ada, n_ada, axis=1))

    g_cat = big["w_cat"]
    g_w_in_full = jnp.concatenate([g_cat[:, :3 * WA], g_cat[:, 6 * WA:6 * WA + nha], g_cat[:, 3 * WA:6 * WA]], axis=1)
    sh_in = g_w_in_full.reshape(D, NCHIP, -1).transpose(1, 0, 2)
    sh_out = big["w_out"].reshape(NCHIP, -1, D)
    sh_up = big["w_up"].reshape(D, NCHIP, -1).transpose(1, 0, 2)
    sh_cw = big["conv_w"].reshape(conv_w.shape[1], NCHIP, -1).transpose(1, 0, 2)
    sh_down = big["w_down"].reshape(NCHIP, -1, D)
    parts = [_as_rows(t) for t in (sh_in, sh_out, sh_up, sh_cw, sh_down)]
    part_rows = [p.shape[1] for p in parts]
    packed = jnp.concatenate(parts, axis=1)
    n_rows = _rows_of(packed.shape[1], 2 * LANES) * 2 * LANES
    half = n_rows // 2
    packed = jnp.pad(packed, ((0, 0), (0, n_rows - packed.shape[1]), (0, 0)))
    halves = packed.reshape(NCHIP, 2, half, D)
    mine = lax.dynamic_index_in_dim(halves, mc, axis=1, keepdims=False).reshape(NCHIP * half, D)
    theirs = lax.dynamic_index_in_dim(halves, 1 - mc, axis=1, keepdims=False).reshape(NCHIP * half, D)
    from_sib = _to_sibling("pair_swap", theirs)
    pair_sum = _add2("pair_sum", mine, from_sib).reshape(NCHIP, half, D)
    by_chip = _scatter_chips("scatter_grads", pair_sum)
    own = lax.dynamic_index_in_dim(pair_sum, chip, axis=0, keepdims=False)
    my_half = _sum_leading("chip_sum", _by_chip(own, by_chip, chip))
    sib_half = _to_sibling("pair_share", my_half)
    pair = jnp.stack([my_half, sib_half])
    shard = jnp.concatenate([lax.dynamic_index_in_dim(pair, mc, axis=0, keepdims=False),
                             lax.dynamic_index_in_dim(pair, 1 - mc, axis=0, keepdims=False)], axis=0)

    def unpack(k, shape):
        start = sum(part_rows[:k])
        n = int(np.prod(shape))
        return shard[start:start + part_rows[k]].reshape(-1)[:n].reshape(shape)

    g_w_in = unpack(0, w_in.shape[1:])
    g_w_out = unpack(1, w_out.shape[1:])
    g_w_up = unpack(2, w_up.shape[1:])
    g_conv_w = unpack(3, conv_w.shape[1:])
    g_w_down = unpack(4, w_down.shape[1:])

    grads = dict(w_ada=g_w_ada, b_ada=g_b_ada, w_in=g_w_in, b_fgate=g_b_fgate, gn_a=g_gn[:, :WA], gn_b=g_gn[:, WA:],
                 w_out=g_w_out, ln1_g=g_ln1_g, ln1_b=g_ln1_b, w_up=g_w_up, conv_w=g_conv_w, conv_b=g_conv_b,
                 w_down=g_w_down, ln2_g=g_ln2_g, ln2_b=g_ln2_b)
    weights = dict(w_ada=w_ada, b_ada=b_ada, w_in=w_in, b_fgate=b_fgate, gn_a=gn_a, gn_b=gn_b, w_out=w_out,
                   ln1_g=ln1_g, ln1_b=ln1_b, w_up=w_up, conv_w=conv_w, conv_b=conv_b, w_down=w_down,
                   ln2_g=ln2_g, ln2_b=ln2_b)
    ms = dict(w_ada=m_w_ada, b_ada=m_b_ada, w_in=m_w_in, b_fgate=m_b_fgate, gn_a=m_gn_a, gn_b=m_gn_b,
              w_out=m_w_out, ln1_g=m_ln1_g, ln1_b=m_ln1_b, w_up=m_w_up, conv_w=m_conv_w, conv_b=m_conv_b,
              w_down=m_w_down, ln2_g=m_ln2_g, ln2_b=m_ln2_b)
    vs = dict(w_ada=v_w_ada, b_ada=v_b_ada, w_in=v_w_in, b_fgate=v_b_fgate, gn_a=v_gn_a, gn_b=v_gn_b,
              w_out=v_w_out, ln1_g=v_ln1_g, ln1_b=v_ln1_b, w_up=v_w_up, conv_w=v_conv_w, conv_b=v_conv_b,
              w_down=v_w_down, ln2_g=v_ln2_g, ln2_b=v_ln2_b)
    names = list(weights)
    big_names = ("w_ada", "w_in", "w_out", "w_up", "w_down")
    delta, new_m, new_v = {}, {}, {}
    for n in big_names:
        shp = weights[n].shape
        d, m2, v2 = _adamw("adamw_" + n, weights[n][0], grads[n].reshape(shp[1:]), ms[n][0], vs[n][0])
        delta[n], new_m[n], new_v[n] = d.reshape(shp), m2.reshape(shp), v2.reshape(shp)
    small_names = [n for n in names if n not in big_names]

    def pack_small(src):
        flats = []
        for n in small_names:
            flat = src[n].reshape(-1)
            flats.append(jnp.pad(flat, (0, _rows_of(flat.shape[0]) * D - flat.shape[0])))
        allf = jnp.concatenate(flats)
        rows = _rows_of(allf.shape[0], SUBLANES * D) * SUBLANES
        return jnp.pad(allf, (0, rows * D - allf.shape[0])).reshape(rows, D)

    sd, sm, sv_ = _adamw("adamw_small", pack_small(weights), pack_small(grads), pack_small(ms), pack_small(vs))
    off = 0
    for n in small_names:
        shp = weights[n].shape
        cnt = int(np.prod(shp))
        r = _rows_of(cnt)
        for dst, src in ((delta, sd), (new_m, sm), (new_v, sv_)):
            dst[n] = src[off:off + r].reshape(-1)[:cnt].reshape(shp)
        off += r

    out_g = {n: grads[n].reshape(weights[n].shape) for n in names}
    return (loss, grad_x.reshape(x.shape), *[out_g[n] for n in names], *[delta[n] for n in names],
            *[new_m[n] for n in names], *[new_v[n] for n in names])
```

```python
import functools
import math

import numpy as np
import jax
import jax.numpy as jnp
from jax import lax
from jax.experimental import pallas as pl
from jax.experimental.pallas import tpu as pltpu

F32 = jnp.float32
BF16 = jnp.bfloat16

D = 1024
S = 4096
HD = 64
WA = 512
DFF = 2816
NCHIP = 4
NDEV = 8
PATTERNS = ((128, 1), (512, 4), (2048, 16))
ROPE_THETA = 500000.0
ROPE_DIMS = HD // 4
ALPHA = (2.0 * 1) ** 0.25
LN_EPS = 1e-5
RMS_EPS = 1e-6
ADAM_LR = 0.001
ADAM_B1 = 0.9
ADAM_B2 = 0.999
ADAM_EPS = 1e-08
ADAM_WD = 0.01
ADAM_STEP = 10

LANES = 128
SUBLANES = 8
TQ = 256
FPAD = LANES
NEG = -1e30
VMEM_LIMIT = 56 * 1024 * 1024
MESH = pl.DeviceIdType.MESH


def _cparams(sem):
    return pltpu.CompilerParams(dimension_semantics=sem, vmem_limit_bytes=VMEM_LIMIT)


def _pick(n, cands):
    for c in cands:
        if n % c == 0:
            return c
    return n


def _rsum8(v):
    tm, w = v.shape
    return jnp.sum(v.reshape(tm // SUBLANES, SUBLANES, w), axis=0)


def _sigmoid(x):
    return 1.0 / (1.0 + jnp.exp(-x))


def _dot(a, b):
    return jnp.dot(a, b, preferred_element_type=F32)


def _dot_nt(a, b):
    return lax.dot_general(a, b, (((1,), (1,)), ((), ())), preferred_element_type=F32)


def _dot_tn(a, b):
    return lax.dot_general(a, b, (((0,), (0,)), ((), ())), preferred_element_type=F32)


def _rowwise(name, fn, T, tm, *, tiles=(), halos=(), seqvecs=(), consts=(), outs=(), accs=(), seqaccs=(),
             seq_len=None):
    seq_len = S if seq_len is None else seq_len
    nb = T // tm
    spb = max(seq_len // tm, 1)
    nseq = max(T // seq_len, 1)
    n8 = T // SUBLANES
    r8 = tm // SUBLANES
    in_specs, args = [], []
    for a in tiles:
        in_specs.append(pl.BlockSpec((tm, a.shape[1]), lambda i: (i, 0)))
        args.append(a)
    for a, direction in halos:
        if direction < 0:
            idx = lambda i: (jnp.maximum(i * r8 - 1, 0), 0)
        else:
            idx = lambda i: (jnp.minimum((i + 1) * r8, n8 - 1), 0)
        in_specs.append(pl.BlockSpec((SUBLANES, a.shape[1]), idx))
        args.append(a)
    for a in seqvecs:
        in_specs.append(pl.BlockSpec((1, 1, a.shape[2]), lambda i: (i // spb, 0, 0)))
        args.append(a)
    for a in consts:
        in_specs.append(pl.BlockSpec(a.shape, lambda i, nd=a.ndim: (0,) * nd))
        args.append(a)
    out_shape, out_specs = [], []
    for w, dt in outs:
        out_shape.append(jax.ShapeDtypeStruct((T, w), dt))
        out_specs.append(pl.BlockSpec((tm, w), lambda i: (i, 0)))
    for w in accs:
        out_shape.append(jax.ShapeDtypeStruct((SUBLANES, w), F32))
        out_specs.append(pl.BlockSpec((SUBLANES, w), lambda i: (0, 0)))
    for w in seqaccs:
        out_shape.append(jax.ShapeDtypeStruct((nseq, SUBLANES, w), F32))
        out_specs.append(pl.BlockSpec((1, SUBLANES, w), lambda i: (i // spb, 0, 0)))
    n_t, n_h, n_s, n_c = len(tiles), len(halos), len(seqvecs), len(consts)
    n_o, n_a, n_sa = len(outs), len(accs), len(seqaccs)

    def body(*refs):
        i = pl.program_id(0)
        ins = refs[:n_t + n_h + n_s + n_c]
        orefs = refs[n_t + n_h + n_s + n_c:]
        vals = [r[...] for r in ins[:n_t + n_h]]
        vals += [r[0] for r in ins[n_t + n_h:n_t + n_h + n_s]]
        vals += list(ins[n_t + n_h + n_s:])
        res = fn(i, *vals)
        if not isinstance(res, (tuple, list)):
            res = (res,)
        for k in range(n_o):
            orefs[k][...] = res[k].astype(orefs[k].dtype)
        for k in range(n_a):
            r = orefs[n_o + k]

            @pl.when(i == 0)
            def _():
                r[...] = jnp.zeros_like(r)

            r[...] += res[n_o + k]

            @pl.when(i == nb - 1)
            def _():
                r[...] = jnp.broadcast_to(jnp.sum(r[...], axis=0, keepdims=True), r.shape)
        for k in range(n_sa):
            r = orefs[n_o + n_a + k]

            @pl.when(i % spb == 0)
            def _():
                r[...] = jnp.zeros_like(r)

            r[0] += res[n_o + n_a + k]

            @pl.when(i % spb == spb - 1)
            def _():
                r[0] = jnp.broadcast_to(jnp.sum(r[0], axis=0, keepdims=True), r.shape[1:])

    sem = ("arbitrary",) if (n_a or n_sa) else ("parallel",)
    res = pl.pallas_call(
        body, name=name, grid=(nb,), in_specs=in_specs, out_specs=out_specs, out_shape=out_shape,
        compiler_params=_cparams(sem),
    )(*args)
    return res


def _ln_fwd(r, g, b):
    mu = jnp.mean(r, axis=-1, keepdims=True)
    xc = r - mu
    var = jnp.mean(xc * xc, axis=-1, keepdims=True)
    rstd = lax.rsqrt(var + LN_EPS)
    n = xc * rstd
    return n * g + b, n, rstd


def _ln_bwd(dy, n, rstd, g):
    dn = dy * g
    return rstd * (dn - jnp.mean(dn, axis=-1, keepdims=True) - n * jnp.mean(dn * n, axis=-1, keepdims=True))


def _head_mean(t, g_ref):
    gw = g_ref.shape[0]
    hi = t.astype(BF16)
    lo = (t - hi.astype(F32)).astype(BF16)
    g = g_ref[...]
    parts = []
    for c in range(t.shape[1] // gw):
        sl = slice(c * gw, (c + 1) * gw)
        parts.append(_dot(hi[:, sl], g) + _dot(lo[:, sl], g))
    out = parts[0] if len(parts) == 1 else jnp.concatenate(parts, axis=1)
    return out * (1.0 / HD)


def _rope(z, c, s1, s2):
    w = z.shape[1]
    half = ROPE_DIMS // 2
    return z * c + pltpu.roll(z, half, 1) * s1 + pltpu.roll(z, w - half, 1) * s2


def _tile_lanes(t, w):
    reps = w // t.shape[1]
    return t if reps == 1 else jnp.concatenate([t] * reps, axis=1)


def _conv_taps(ext, prev, first):
    tm = ext.shape[0]
    prev = jnp.where(first, jnp.zeros_like(prev), prev)
    r8 = lax.broadcasted_iota(jnp.int32, (SUBLANES, 1), 0)
    top = ext[0:SUBLANES]
    s1_top = jnp.where(r8 < 1, pltpu.roll(prev, 1, 0), pltpu.roll(top, 1, 0))
    s2_top = jnp.where(r8 < 2, pltpu.roll(prev, 2, 0), pltpu.roll(top, 2, 0))
    s1 = jnp.concatenate([s1_top, pltpu.roll(ext, 1, 0)[SUBLANES:]], axis=0)
    s2 = jnp.concatenate([s2_top, pltpu.roll(ext, 2, 0)[SUBLANES:]], axis=0)
    return s1, s2


def _conv_taps_up(ext, nxt, last):
    tm = ext.shape[0]
    nxt = jnp.where(last, jnp.zeros_like(nxt), nxt)
    r8 = lax.broadcasted_iota(jnp.int32, (SUBLANES, 1), 0)
    bot = ext[tm - SUBLANES:tm]
    u1_bot = jnp.where(r8 >= 7, pltpu.roll(nxt, 7, 0), pltpu.roll(bot, 7, 0))
    u2_bot = jnp.where(r8 >= 6, pltpu.roll(nxt, 6, 0), pltpu.roll(bot, 6, 0))
    u1 = jnp.concatenate([pltpu.roll(ext, tm - 1, 0)[:tm - SUBLANES], u1_bot], axis=0)
    u2 = jnp.concatenate([pltpu.roll(ext, tm - 2, 0)[:tm - SUBLANES], u2_bot], axis=0)
    return u1, u2


def _mm_nt(name, a, w, tm=256):
    T = a.shape[0]
    n = w.shape[0]
    ch = _pick(n, (512, 256, 128))

    def fn(i, av, w_ref):
        ab = av.astype(BF16)
        parts = [_dot_nt(ab, w_ref[c * ch:(c + 1) * ch, :]) for c in range(n // ch)]
        return parts[0] if len(parts) == 1 else jnp.concatenate(parts, axis=1)

    return _rowwise(name, fn, T, tm, tiles=(a,), consts=(w,), outs=((n, F32),))[0]


def _mm_tn(name, a, b, *, mod=None, tt=512):
    T, k1 = a.shape
    k2 = b.shape[1]
    t1 = k1 if k1 <= 1536 else _pick(k1, (1408, 1024, 512, 256, 128))
    t2 = k2 if k2 <= 1536 else _pick(k2, (1408, 1024, 640, 512, 256, 128))
    tt = min(tt, S)
    spb = S // tt

    def body(*refs):
        if mod is not None:
            a_ref, sc_ref, sh_ref, b_ref, o_ref = refs
        else:
            a_ref, b_ref, o_ref = refs
        t = pl.program_id(2)

        @pl.when(t == 0)
        def _():
            o_ref[...] = jnp.zeros_like(o_ref)

        av = a_ref[...]
        if mod is not None:
            av = av * (1.0 + sc_ref[0]) + sh_ref[0]
        o_ref[...] += _dot_tn(av.astype(BF16), b_ref[...].astype(BF16))

    in_specs = [pl.BlockSpec((tt, t1), lambda p, q, t: (t, p))]
    args = [a]
    if mod is not None:
        for v in mod:
            in_specs.append(pl.BlockSpec((1, 1, t1), lambda p, q, t: (t // spb, 0, p)))
            args.append(v)
    in_specs.append(pl.BlockSpec((tt, t2), lambda p, q, t: (t, q)))
    args.append(b)
    return pl.pallas_call(
        body, name=name, grid=(k1 // t1, k2 // t2, T // tt), in_specs=in_specs,
        out_specs=pl.BlockSpec((t1, t2), lambda p, q, t: (p, q)),
        out_shape=jax.ShapeDtypeStruct((k1, k2), F32),
        compiler_params=_cparams(("parallel", "parallel", "arbitrary")),
    )(*args)


def _mod_mm(name, x, sc, sh, ws, out_dtypes, rope=None, rope_secs=(), tm=256):
    T = x.shape[0]
    nw = len(ws)

    def fn(i, xv, *rest):
        if rope is not None:
            cv, s1v, s2v = rest[:3]
            rest = rest[3:]
        scv, shv = rest[:2]
        w_refs = rest[2:]
        h = (xv * (1.0 + scv) + shv).astype(BF16)
        res = []
        for k, w_ref in enumerate(w_refs):
            n = w_ref.shape[1]
            ch = WA if (k == 0 and rope is not None) else _pick(n, (512, 256, 128))
            parts = []
            for c in range(n // ch):
                z = _dot(h, w_ref[:, c * ch:(c + 1) * ch])
                if k == 0 and c in rope_secs:
                    z = _rope(z, _tile_lanes(cv, ch), _tile_lanes(s1v, ch), _tile_lanes(s2v, ch))
                parts.append(z.astype(out_dtypes[k]))
            res.append(parts[0] if len(parts) == 1 else jnp.concatenate(parts, axis=1))
        return tuple(res)

    tiles = (x,) + (tuple(rope) if rope is not None else ())
    outs = tuple((w.shape[1], dt) for w, dt in zip(ws, out_dtypes))
    return _rowwise(name, fn, T, tm, tiles=tiles, seqvecs=(sc, sh), consts=tuple(ws), outs=outs)


def _tri(tb, lower):
    r = lax.broadcasted_iota(jnp.int32, (tb, tb), 0)
    c = lax.broadcasted_iota(jnp.int32, (tb, tb), 1)
    return jnp.where((r >= c) if lower else (r <= c), 1.0, 0.0).astype(F32)


def _cumsum_seq(name, ins, consts, fn_in, fn_out, outs, reverse, n_acc=0, tb=512):
    T = ins[0].shape[0]
    tb = min(tb, S)
    nbs = S // tb
    nseq = T // S
    n_i, n_c, n_o = len(ins), len(consts), len(outs)

    def blk(b, j):
        return (b * nbs + (nbs - 1 - j if reverse else j), 0)

    def body(*refs):
        i_refs, c_refs = refs[:n_i], refs[n_i:n_i + n_c]
        o_refs = refs[n_i + n_c:n_i + n_c + n_o]
        acc_refs = refs[n_i + n_c + n_o:n_i + n_c + n_o + n_acc]
        carry = refs[-1]
        b, j = pl.program_id(0), pl.program_id(1)

        @pl.when(j == 0)
        def _():
            carry[...] = jnp.zeros_like(carry)

        iv = [r[...] for r in i_refs]
        xin = fn_in(*iv, *c_refs)
        cum = jnp.dot(_tri(tb, not reverse), xin, preferred_element_type=F32,
                      precision=lax.Precision.HIGHEST) + carry[0:1, :]
        carry[...] = carry[...] + jnp.sum(xin, axis=0, keepdims=True)
        res = fn_out(cum, *iv, *c_refs)
        for o, r in zip(o_refs, res):
            o[...] = r.astype(o.dtype)
        for a in acc_refs:
            @pl.when((b == 0) & (j == 0))
            def _():
                a[...] = jnp.zeros_like(a)

            a[...] += _rsum8(res[0])

            @pl.when((b == nseq - 1) & (j == nbs - 1))
            def _():
                a[...] = jnp.broadcast_to(jnp.sum(a[...], axis=0, keepdims=True), a.shape)

    in_specs = [pl.BlockSpec((tb, a.shape[1]), blk) for a in ins]
    in_specs += [pl.BlockSpec(c.shape, lambda b, j, nd=c.ndim: (0,) * nd) for c in consts]
    out_shape = [jax.ShapeDtypeStruct((T, w), dt) for w, dt in outs]
    out_shape += [jax.ShapeDtypeStruct((SUBLANES, outs[0][0]), F32)] * n_acc
    out_specs = [pl.BlockSpec((tb, w), blk) for w, _ in outs]
    out_specs += [pl.BlockSpec((SUBLANES, outs[0][0]), lambda b, j: (0, 0))] * n_acc
    return pl.pallas_call(
        body, name=name, grid=(nseq, nbs), in_specs=in_specs, out_specs=out_specs, out_shape=out_shape,
        scratch_shapes=[pltpu.VMEM((SUBLANES, FPAD), F32)],
        compiler_params=_cparams(("arbitrary", "arbitrary")),
    )(*ins, *consts)


def _log_sigmoid(x):
    return jnp.minimum(x, 0.0) - jnp.log(1.0 + jnp.exp(-jnp.abs(x)))


def _dil_bias(tq):
    max_win = max(w for w, _ in PATTERNS)
    nd = (max_win + tq - 1) // tq + 1
    qi = np.arange(tq)[:, None]
    kj = np.arange(tq)[None, :]
    tabs = []
    for dlt in range(nd):
        dist = dlt * tq + qi - kj
        mult = np.zeros((tq, tq), np.float64)
        for win, dil in PATTERNS:
            mult += (dist >= 0) & (dist % dil == 0) & (dist // dil <= win // dil)
        tabs.append(np.where(mult > 0, np.log(np.maximum(mult, 1.0)), NEG))
    return np.stack(tabs).astype(np.float32)


def _fold_tables(nha):
    hp_n = nha // 2
    pq = np.zeros((3, FPAD, hp_n * 2 * LANES), np.float32)
    pk = np.zeros((3, FPAD, hp_n * LANES), np.float32)
    oq = np.zeros((1, hp_n * 2 * LANES), np.float32)
    ok = np.zeros((1, hp_n * LANES), np.float32)
    sq = np.zeros((hp_n * LANES, FPAD), np.float32)
    sk = np.zeros((hp_n * LANES, FPAD), np.float32)
    for h in range(nha):
        hp, odd = divmod(h, 2)
        qb = hp * 2 * LANES + odd * (LANES + 8)
        kb = hp * LANES + odd * 8
        for i in range(3):
            pq[i, h, qb + i] = 1
            oq[0, qb + 3 + i] = 1
            ok[0, kb + i] = 1
            pk[i, h, kb + 3 + i] = 1
        sq[kb, h] = 1
        sk[kb + 3, h] = 1
    return pq, pk, oq, ok, sq, sk


def _stack_heads(x2, h0, extra=None):
    z = jnp.zeros_like(x2)
    a, b = jnp.where(h0, x2, z), jnp.where(h0, z, x2)
    if extra is not None:
        a = jnp.concatenate([a, extra[:, :LANES]], axis=1)
        b = jnp.concatenate([b, extra[:, LANES:]], axis=1)
    return jnp.concatenate([a, b], axis=0)


def _attn_fwd(name, qkv, secs, fox, eq=None, ek=None, bias=None):
    T = qkv.shape[0]
    nq = S // TQ
    nbl = T // S
    hp_n = WA // LANES
    sq, sk, sv = (s * hp_n for s in secs)
    scale = HD ** -0.5
    nd = None if fox else bias.shape[0]

    def body(*refs):
        if fox:
            q_ref, k_ref, v_ref, eq_ref, ek_ref, o_ref, lse_ref = refs
        else:
            q_ref, k_ref, v_ref, b_ref, o_ref, lse_ref = refs
        i = pl.program_id(2)
        lane = lax.broadcasted_iota(jnp.int32, (1, LANES), 1)
        h0 = lane < HD
        q2 = (q_ref[...].astype(F32) * scale).astype(BF16)
        qs = _stack_heads(q2, h0, eq_ref[...] if fox else None)

        def scores(t, diag):
            off = pl.multiple_of((i - t) * TQ, TQ)
            kk = k_ref[pl.ds(off, TQ), :]
            if fox:
                kk = jnp.concatenate([kk, ek_ref[pl.ds(off, TQ), :]], axis=1)
            s = jnp.concatenate([_dot_nt(qs[:TQ], kk), _dot_nt(qs[TQ:], kk)], axis=0)
            if not fox:
                s = (s.reshape(2, TQ, TQ) + b_ref[t]).reshape(2 * TQ, TQ)
            elif diag:
                rows = lax.broadcasted_iota(jnp.int32, (2, TQ, TQ), 1).reshape(2 * TQ, TQ)
                cols = lax.broadcasted_iota(jnp.int32, (2 * TQ, TQ), 1)
                s = jnp.where(cols <= rows, s, NEG)
            return s

        def update(t, s, m, l, acc):
            off = pl.multiple_of((i - t) * TQ, TQ)
            v2 = v_ref[pl.ds(off, TQ), :]
            m_new = jnp.maximum(m, jnp.max(s, axis=1, keepdims=True))
            p = jnp.exp(s - m_new)
            a = jnp.exp(m - m_new)
            l = a * l + jnp.sum(p, axis=1, keepdims=True)
            pb = p.astype(BF16)
            acc = a * acc + jnp.concatenate([_dot(pb[:TQ], v2), _dot(pb[TQ:], v2)], axis=0)
            return m_new, l, acc

        def step(t, carry):
            s, m, l, acc = carry
            return (scores(t + 1, False),) + update(t, s, m, l, acc)

        init = (jnp.full((2 * TQ, 1), NEG, F32), jnp.zeros((2 * TQ, 1), F32), jnp.zeros((2 * TQ, LANES), F32))
        n = i + 1 if fox else jnp.minimum(i + 1, nd)
        s, m, l, acc = lax.fori_loop(0, n - 1, step, (scores(0, True),) + init)
        m, l, acc = update(n - 1, s, m, l, acc)
        on = acc / l
        o_ref[...] = jnp.where(h0, on[:TQ], on[TQ:])
        lse = jnp.broadcast_to(m + jnp.log(l), (2 * TQ, LANES))
        lse_ref[...] = jnp.concatenate([lse[:TQ], lse[TQ:]], axis=1)

    in_specs = [
        pl.BlockSpec((TQ, LANES), lambda b, hp, i: (b * nq + i, sq + hp)),
        pl.BlockSpec((S, LANES), lambda b, hp, i: (b, sk + hp)),
        pl.BlockSpec((S, LANES), lambda b, hp, i: (b, sv + hp)),
    ]
    args = [qkv, qkv, qkv]
    if fox:
        in_specs += [pl.BlockSpec((TQ, 2 * LANES), lambda b, hp, i: (b * nq + i, hp)),
                     pl.BlockSpec((S, LANES), lambda b, hp, i: (b, hp))]
        args += [eq, ek]
    else:
        in_specs.append(pl.BlockSpec(bias.shape, lambda b, hp, i: (0, 0, 0)))
        args.append(bias)
    return pl.pallas_call(
        body, name=name, grid=(nbl, hp_n, nq), in_specs=in_specs,
        out_specs=[pl.BlockSpec((TQ, LANES), lambda b, hp, i: (b * nq + i, hp)),
                   pl.BlockSpec((TQ, 2 * LANES), lambda b, hp, i: (b * nq + i, hp))],
        out_shape=[jax.ShapeDtypeStruct((T, WA), F32), jax.ShapeDtypeStruct((T, 2 * WA), F32)],
        compiler_params=_cparams(("parallel", "parallel", "arbitrary")),
    )(*args)


def _attn_bwd(name, qkv, secs, o, do, do_sec, lse, fox, eq=None, ek=None, bias=None):
    T = qkv.shape[0]
    nq = S // TQ
    nbl = T // S
    hp_n = WA // LANES
    sq, sk, sv = (s * hp_n for s in secs)
    dsec = do_sec * hp_n
    scale = HD ** -0.5
    nd = None if fox else bias.shape[0]
    kc = 2 * LANES if fox else LANES

    def body(*refs):
        if fox:
            (q_ref, k_ref, v_ref, o_ref, do_ref, lse_ref, eq_ref, ek_ref,
             dq_ref, dk_ref, dv_ref, dqe_ref, dek_ref, dl_ref) = refs
        else:
            q_ref, k_ref, v_ref, o_ref, do_ref, lse_ref, b_ref, dq_ref, dk_ref, dv_ref, dl_ref = refs
        j = pl.program_id(2)
        lane = lax.broadcasted_iota(jnp.int32, (1, LANES), 1)
        h0 = lane < HD

        @pl.when(j == 0)
        def _():
            dq_ref[...] = jnp.zeros_like(dq_ref)
            if fox:
                dqe_ref[...] = jnp.zeros_like(dqe_ref)

            def dl_step(r, c):
                off = pl.multiple_of(r * TQ, TQ)
                d2 = do_ref[pl.ds(off, TQ), :] * o_ref[pl.ds(off, TQ), :]
                z2 = jnp.zeros_like(d2)
                dl0 = jnp.sum(jnp.where(h0, d2, z2), axis=1, keepdims=True)
                dl1 = jnp.sum(jnp.where(h0, z2, d2), axis=1, keepdims=True)
                dl_ref[pl.ds(off, TQ), :] = jnp.concatenate(
                    [jnp.broadcast_to(dl0, (TQ, LANES)), jnp.broadcast_to(dl1, (TQ, LANES))], axis=1)
                return c

            lax.fori_loop(0, nq, dl_step, 0)

        kk = k_ref[...]
        if fox:
            kk = jnp.concatenate([kk, ek_ref[...]], axis=1)
        v2 = v_ref[...]

        def wide(x2):
            st = jnp.concatenate([x2[:, :LANES], x2[:, LANES:]], axis=0)
            return st if TQ == LANES else jnp.concatenate([st] * (TQ // LANES), axis=1)

        def step(t, carry, diag):
            dkk, dv2 = carry
            off = pl.multiple_of((j + t) * TQ, TQ)
            q2 = (q_ref[pl.ds(off, TQ), :].astype(F32) * scale).astype(BF16)
            qs = _stack_heads(q2, h0, eq_ref[pl.ds(off, TQ), :] if fox else None)
            dos = _stack_heads(do_ref[pl.ds(off, TQ), :].astype(BF16), h0)
            s = jnp.concatenate([_dot_nt(qs[:TQ], kk), _dot_nt(qs[TQ:], kk)], axis=0)
            if not fox:
                s = (s.reshape(2, TQ, TQ) + b_ref[t]).reshape(2 * TQ, TQ)
            elif diag:
                rows = lax.broadcasted_iota(jnp.int32, (2, TQ, TQ), 1).reshape(2 * TQ, TQ)
                cols = lax.broadcasted_iota(jnp.int32, (2 * TQ, TQ), 1)
                s = jnp.where(cols <= rows, s, NEG)
            p = jnp.exp(s - wide(lse_ref[pl.ds(off, TQ), :]))
            dp = jnp.concatenate([_dot_nt(dos[:TQ], v2), _dot_nt(dos[TQ:], v2)], axis=0)
            dsb = (p * (dp - wide(dl_ref[pl.ds(off, TQ), :]))).astype(BF16)
            dv2 = dv2 + _dot_tn(p.astype(BF16), dos)
            dkk = dkk + _dot_tn(dsb, qs)
            dqq = jnp.concatenate([_dot(dsb[:TQ], kk), _dot(dsb[TQ:], kk)], axis=0)
            dq_ref[pl.ds(off, TQ), :] += jnp.where(h0, dqq[:TQ, :LANES], dqq[TQ:, :LANES])
            if fox:
                dqe_ref[pl.ds(off, TQ), :] += jnp.where(lane < SUBLANES, dqq[:TQ, LANES:], dqq[TQ:, LANES:])
            return dkk, dv2

        zero = (jnp.zeros((TQ, kc), F32), jnp.zeros((TQ, LANES), F32))
        if fox:
            dkk, dv2 = lax.fori_loop(1, nq - j, lambda t, c: step(t, c, False), step(0, zero, True))
        else:
            dkk, dv2 = lax.fori_loop(0, jnp.minimum(nq - j, nd), lambda t, c: step(t, c, False), zero)
        dk_ref[...] = dkk[:, :LANES]
        dv_ref[...] = dv2
        if fox:
            dek_ref[...] = dkk[:, LANES:]

        @pl.when(j == nq - 1)
        def _():
            dq_ref[...] = dq_ref[...] * scale

    seq = lambda c, w=LANES: pl.BlockSpec((S, w), lambda b, hp, j: (b, c + hp))
    blk = lambda c: pl.BlockSpec((TQ, LANES), lambda b, hp, j: (b * nq + j, c + hp))
    in_specs = [seq(sq), blk(sk), blk(sv), seq(0), seq(dsec), seq(0, 2 * LANES)]
    args = [qkv, qkv, qkv, o, do, lse]
    if fox:
        in_specs += [seq(0, 2 * LANES), blk(0)]
        args += [eq, ek]
    else:
        in_specs.append(pl.BlockSpec(bias.shape, lambda b, hp, j: (0, 0, 0)))
        args.append(bias)
    out_specs = [seq(0), blk(0), blk(0)]
    out_shape = [jax.ShapeDtypeStruct((T, WA), F32)] * 3
    if fox:
        out_specs += [seq(0), blk(0)]
        out_shape += [jax.ShapeDtypeStruct((T, WA), F32)] * 2
    return pl.pallas_call(
        body, name=name, grid=(nbl, hp_n, nq), in_specs=in_specs, out_specs=out_specs, out_shape=out_shape,
        scratch_shapes=[pltpu.VMEM((S, 2 * LANES), F32)],
        compiler_params=_cparams(("parallel", "parallel", "arbitrary")),
    )(*args)


def _exchange(name, ins, out_shapes, remote, local):
    n_in, n_out = len(ins), len(out_shapes)
    nr, nl = len(remote), len(local)

    def body(*refs):
        in_refs = refs[:n_in]
        out_refs = refs[n_in:n_in + n_out]
        send_sems, recv_sems, loc_sems = refs[n_in + n_out:]
        me = (lax.axis_index("x"), lax.axis_index("y"), lax.axis_index("c"))

        def peer_of(flip):
            return tuple(1 - v if f else v for v, f in zip(me, flip))

        def at(ref, idx):
            return ref if idx is None else ref.at[idx]

        def rcopy(k, who):
            flip, a, sfn, b, dfn = remote[k]
            return pltpu.make_async_remote_copy(
                src_ref=at(in_refs[a], sfn(*who)), dst_ref=at(out_refs[b], dfn(*who)),
                send_sem=send_sems.at[k], recv_sem=recv_sems.at[k],
                device_id=peer_of(flip), device_id_type=MESH)

        locs = [pltpu.make_async_copy(at(in_refs[a], sfn(*me)), at(out_refs[b], dfn(*me)), loc_sems.at[k])
                for k, (a, sfn, b, dfn) in enumerate(local)]
        for cp in locs:
            cp.start()
        sends = [rcopy(k, me) for k in range(nr)]
        for cp in sends:
            cp.start()
        for k in range(nr):
            rcopy(k, peer_of(remote[k][0])).wait_recv()
        for cp in sends:
            cp.wait_send()
        for cp in locs:
            cp.wait()

    any_spec = pl.BlockSpec(memory_space=pl.ANY)
    return pl.pallas_call(
        body, name=name, in_specs=[any_spec] * n_in, out_specs=[any_spec] * n_out, out_shape=list(out_shapes),
        scratch_shapes=[pltpu.SemaphoreType.DMA((max(nr, 1),)), pltpu.SemaphoreType.DMA((max(nr, 1),)),
                        pltpu.SemaphoreType.DMA((max(nl, 1),))],
    )(*ins)


_FLIPS7 = [(0, 0, 1), (0, 1, 0), (0, 1, 1), (1, 0, 0), (1, 0, 1), (1, 1, 0), (1, 1, 1)]
_CHIP_FLIPS = [(1, 0, 0), (0, 1, 0), (1, 1, 0)]


def _dev_index(x, y, c):
    return 4 * x + 2 * y + c


def _chip_index(x, y, c):
    return 2 * x + y


def _all_gather8(name, v):
    remote = [(f, 0, lambda x, y, c: None, 0, _dev_index) for f in _FLIPS7]
    local = [(0, lambda x, y, c: None, 0, _dev_index)]
    return _exchange(name, [v], [jax.ShapeDtypeStruct((NDEV,) + v.shape, v.dtype)], remote, local)[0]


def _gather_halves(name, vs):
    n_v = len(vs)

    def body(*refs):
        in_refs, out_refs = refs[:n_v], refs[n_v:2 * n_v]
        send_sems, recv_sems = refs[2 * n_v:]
        x, y, c = lax.axis_index("x"), lax.axis_index("y"), lax.axis_index("c")
        sibling = (x, y, 1 - c)
        chips = [(1 - x, y), (x, 1 - y), (1 - x, 1 - y)]

        def copy(k, n, src, blk, half, to):
            return pltpu.make_async_remote_copy(
                src_ref=src, dst_ref=out_refs[n].at[blk, half], send_sem=send_sems.at[k], recv_sem=recv_sems.at[k],
                device_id=to, device_id_type=MESH)

        first = [copy(6 * n + j, n, in_refs[n].at[c], j, c, (*chip, c))
                 for n in range(n_v) for j, chip in enumerate(chips)]
        for cp in first:
            cp.start()
        passed = []
        for n in range(n_v):
            for j, chip in enumerate(chips):
                copy(6 * n + j, n, in_refs[n].at[c], j, c, (*chip, c)).wait_recv()
                fw = copy(6 * n + 3 + j, n, out_refs[n].at[j, c], j, c, sibling)
                fw.start()
                passed.append(fw)
        for n in range(n_v):
            for j in range(len(chips)):
                copy(6 * n + 3 + j, n, out_refs[n].at[j, 1 - c], j, 1 - c, sibling).wait_recv()
        for cp in first + passed:
            cp.wait_send()

    any_spec = pl.BlockSpec(memory_space=pl.ANY)
    return pl.pallas_call(
        body, name=name, in_specs=[any_spec] * n_v, out_specs=[any_spec] * n_v,
        out_shape=[jax.ShapeDtypeStruct((NCHIP - 1,) + v.shape, v.dtype) for v in vs],
        scratch_shapes=[pltpu.SemaphoreType.DMA((6 * n_v,)), pltpu.SemaphoreType.DMA((6 * n_v,))],
    )(*vs)


def _to_sibling(name, v):
    remote = [((0, 0, 1), 0, lambda x, y, c: None, 0, lambda x, y, c: None)]
    return _exchange(name, [v], [jax.ShapeDtypeStruct(v.shape, v.dtype)], remote, [])[0]


def _scatter_chips(name, v):
    remote = []
    for j, f in enumerate(_CHIP_FLIPS):
        src = lambda x, y, c, f=f: _chip_index(1 - x if f[0] else x, 1 - y if f[1] else y, c)
        remote.append((f, 0, src, 0, lambda x, y, c, j=j: j))
    return _exchange(name, [v], [jax.ShapeDtypeStruct((NCHIP - 1,) + v.shape[1:], v.dtype)], remote, [])[0]


def _by_chip(own, others, chip):
    stacked = jnp.concatenate([own[None], others], axis=0)
    blocks = []
    for k in range(NCHIP):
        d = k ^ chip
        place = jnp.where(d == 0, 0, jnp.where(d == 2, 1, jnp.where(d == 1, 2, 3)))
        blocks.append(lax.dynamic_index_in_dim(stacked, place, axis=0, keepdims=False))
    return jnp.stack(blocks)


def _sum_leading(name, v, tm=None):
    n, r, w = v.shape
    tm = _pick(r, (256, 128, 64, 32, 16, 8)) if tm is None else tm

    def body(v_ref, o_ref):
        acc = v_ref[0].astype(F32)
        for k in range(1, n):
            acc = acc + v_ref[k].astype(F32)
        o_ref[...] = acc

    return pl.pallas_call(
        body, name=name, grid=(r // tm,), in_specs=[pl.BlockSpec((n, tm, w), lambda i: (0, i, 0))],
        out_specs=pl.BlockSpec((tm, w), lambda i: (i, 0)), out_shape=jax.ShapeDtypeStruct((r, w), F32),
        compiler_params=_cparams(("parallel",)),
    )(v)


def _add2(name, a, b, tm=None, out_dtype=F32):
    r, w = a.shape
    tm = _pick(r, (256, 128, 64, 32, 16, 8)) if tm is None else tm

    def body(a_ref, b_ref, o_ref):
        o_ref[...] = (a_ref[...] + b_ref[...]).astype(out_dtype)

    spec = pl.BlockSpec((tm, w), lambda i: (i, 0))
    return pl.pallas_call(
        body, name=name, grid=(r // tm,), in_specs=[spec, spec], out_specs=spec,
        out_shape=jax.ShapeDtypeStruct((r, w), out_dtype), compiler_params=_cparams(("parallel",)),
    )(a, b)


def _ada_fwd(call_all, w_shard):
    def body(c_ref, w_ref, o_ref):
        cv = c_ref[...]
        o_ref[...] = jnp.dot(cv * _sigmoid(cv), w_ref[...], preferred_element_type=F32,
                             precision=lax.Precision.HIGHEST)

    n = w_shard.shape[1]
    return pl.pallas_call(
        body, name="ada_fwd", out_shape=jax.ShapeDtypeStruct((call_all.shape[0], n), F32),
        compiler_params=pltpu.CompilerParams(vmem_limit_bytes=VMEM_LIMIT),
    )(call_all, w_shard)


def _ada_bwd(call_all, dada):
    def body(c_ref, d_ref, o_ref):
        cv = c_ref[...]
        o_ref[...] = lax.dot_general(cv * _sigmoid(cv), d_ref[...], (((0,), (0,)), ((), ())),
                                     preferred_element_type=F32, precision=lax.Precision.HIGHEST)

    return pl.pallas_call(
        body, name="ada_bwd", out_shape=jax.ShapeDtypeStruct((call_all.shape[1], dada.shape[1]), F32),
        compiler_params=pltpu.CompilerParams(vmem_limit_bytes=VMEM_LIMIT),
    )(call_all, dada)


def _adamw(name, w, g, m, v):
    r, wd = w.shape
    tm = _pick(r, (256, 128, 64, 32, 16, 8))
    bc1 = 1.0 - ADAM_B1 ** ADAM_STEP
    bc2 = 1.0 - ADAM_B2 ** ADAM_STEP

    def body(w_ref, g_ref, m_ref, v_ref, d_ref, mo_ref, vo_ref):
        gv = g_ref[...]
        mn = ADAM_B1 * m_ref[...] + (1.0 - ADAM_B1) * gv
        vn = ADAM_B2 * v_ref[...] + (1.0 - ADAM_B2) * (gv * gv)
        d_ref[...] = -ADAM_LR * ((mn / bc1) / (jnp.sqrt(vn / bc2) + ADAM_EPS) + ADAM_WD * w_ref[...])
        mo_ref[...] = mn
        vo_ref[...] = vn

    spec = pl.BlockSpec((tm, wd), lambda i: (i, 0))
    return pl.pallas_call(
        body, name=name, grid=(r // tm,), in_specs=[spec] * 4, out_specs=[spec] * 3,
        out_shape=[jax.ShapeDtypeStruct((r, wd), F32)] * 3, compiler_params=_cparams(("parallel",)),
    )(w, g, m, v)


def _rope_tables(positions):
    half = ROPE_DIMS // 2
    freqs = ROPE_THETA ** (-jnp.arange(0, ROPE_DIMS, 2, dtype=F32) / ROPE_DIMS)
    ang = positions.astype(F32).reshape(-1, 1) * freqs
    cos, sin = jnp.cos(ang), jnp.sin(ang)
    T = ang.shape[0]
    one = jnp.ones((T, HD - ROPE_DIMS), F32)
    zero = jnp.zeros((T, HD - ROPE_DIMS), F32)
    zh = jnp.zeros((T, half), F32)
    c64 = jnp.concatenate([cos, cos, one], axis=1)
    s1 = jnp.concatenate([zh, sin, zero], axis=1)
    s2 = jnp.concatenate([-sin, zh, zero], axis=1)
    rep = lambda t: jnp.concatenate([t] * (LANES // HD), axis=1)
    return rep(c64), rep(s1), rep(s2)


def _local_step(x, loss_target, positions, ada, w_qkv, w_f, w_out, w_up, conv_w8, w_down,
                b_fgate, gn, ln1_g, ln1_b, conv_b, ln2_g, ln2_b):
    T = x.shape[0]
    nbl = T // S
    nha = WA // HD
    sv = lambda k: ada[:, k:k + 1, :]
    sh_a, sc_a, g_a, sh_f, sc_f, g_f = (sv(k) for k in range(6))
    rope = _rope_tables(positions)
    neg_rope = (rope[0], -rope[1], -rope[2])
    gseg = jnp.asarray(np.kron(np.eye(min(256, 2 * WA) // HD), np.ones((HD, HD))), BF16)
    bias = jnp.asarray(_dil_bias(TQ))
    bf_pad = jnp.zeros((1, FPAD), F32).at[:, :nha].set(b_fgate)

    qkv, fa = _mod_mm("qkv_proj", x, sc_a, sh_a, (w_qkv, w_f), (BF16, F32), rope=rope, rope_secs=(3, 4))
    pq, pk, oq, ok, sq, sk = _fold_tables(nha)

    def fold_out(cum, f, b_ref, pq_ref, pk_ref, oq_ref, ok_ref):
        hi = cum.astype(BF16)
        r1 = cum - hi.astype(F32)
        mid = r1.astype(BF16)
        lo = (r1 - mid.astype(F32)).astype(BF16)
        eqv = _dot(hi, pq_ref[0]) + _dot(mid, pq_ref[1]) + _dot(lo, pq_ref[2]) + oq_ref[...]
        ekv = ok_ref[...] - (_dot(hi, pk_ref[0]) + _dot(mid, pk_ref[1]) + _dot(lo, pk_ref[2]))
        return eqv, ekv

    eq, ek = _cumsum_seq(
        "fgate_fwd", [fa], [bf_pad, jnp.asarray(pq, BF16), jnp.asarray(pk, BF16), jnp.asarray(oq), jnp.asarray(ok)],
        lambda f, b_ref, *_: _log_sigmoid(f + b_ref[...]), fold_out, ((2 * WA, BF16), (WA, BF16)), reverse=False)
    oa, lse_a = _attn_fwd("fox_fwd", qkv, (0, 1, 2), True, eq=eq, ek=ek)
    ob, lse_b = _attn_fwd("dil_fwd", qkv, (3, 4, 5), False, bias=bias)

    def mix_fn(i, oav, obv, xv, gav, gn_ref, g_ref, wo_ref, l1g_ref, l1b_ref):
        o = jnp.concatenate([oav, obv], axis=1)
        rs = lax.rsqrt(_head_mean(o * o, g_ref) + RMS_EPS)
        merged = (o * rs * gn_ref[...]).astype(BF16)
        mix = _dot(merged, wo_ref[...])
        x1, _, _ = _ln_fwd(ALPHA * xv + gav * mix, l1g_ref[...], l1b_ref[...])
        return merged, mix, x1

    merged, mix, x1 = _rowwise("mix_out", mix_fn, T, 256, tiles=(oa, ob, x), seqvecs=(g_a,),
                               consts=(gn, gseg, w_out, ln1_g, ln1_b),
                               outs=((2 * WA, BF16), (D, F32), (D, F32)))
    u = _mod_mm("ffn_up", x1, sc_f, sh_f, (w_up,), (F32,))[0]

    def conv_y(i, uv, prev, cw_ref, cb_ref, tm):
        first = (i * tm) % S == 0
        s1, s2 = _conv_taps(uv, prev, first)
        y = cb_ref[...] + cw_ref[0:1, :] * s2 + cw_ref[1:2, :] * s1 + cw_ref[2:3, :] * uv
        return y, s1, s2

    tmc = 128

    def gate_fn(i, uv, prev, cw_ref, cb_ref):
        y, _, _ = conv_y(i, uv, prev, cw_ref, cb_ref, tmc)
        a, g = y[:, :DFF], y[:, DFF:]
        return g * _sigmoid(g) * a

    act = _rowwise("conv_gate", gate_fn, T, tmc, tiles=(u,), halos=((u, -1),), consts=(conv_w8, conv_b),
                   outs=((DFF, BF16),))[0]

    def down_fn(i, actv, x1v, tgt, gfv, wd_ref, g2_ref, b2_ref):
        ffn = _dot(actv, wd_ref[...])
        y, n2, rstd = _ln_fwd(ALPHA * x1v + gfv * ffn, g2_ref[...], b2_ref[...])
        err = y - tgt
        dy = err * (1.0 / D)
        dr2 = _ln_bwd(dy, n2, rstd, g2_ref[...])
        return (dr2, gfv * dr2, _rsum8(err * err), _rsum8(dy * n2), _rsum8(dy), _rsum8(dr2 * ffn))

    dr2, dffn, loss_acc, d_ln2g, d_ln2b, d_gf = _rowwise(
        "ffn_down_loss", down_fn, T, 256, tiles=(act, x1, loss_target), seqvecs=(g_f,),
        consts=(w_down, ln2_g, ln2_b), outs=((D, F32), (D, BF16)), accs=(D, D, D), seqaccs=(D,))

    dact = _mm_nt("dact", dffn, w_down)

    def gate_bwd_fn(i, uv, dav, prev, cw_ref, cb_ref):
        y, s1, s2 = conv_y(i, uv, prev, cw_ref, cb_ref, tmc)
        a, g = y[:, :DFF], y[:, DFF:]
        sg = _sigmoid(g)
        dyc = jnp.concatenate([dav * (g * sg), dav * a * (sg * (1.0 + g * (1.0 - sg)))], axis=1)
        return dyc, _rsum8(dyc), _rsum8(dyc * s2), _rsum8(dyc * s1), _rsum8(dyc * uv)

    dyc, d_cb, d_cw0, d_cw1, d_cw2 = _rowwise(
        "gate_bwd", gate_bwd_fn, T, tmc, tiles=(u, dact), halos=((u, -1),), consts=(conv_w8, conv_b),
        outs=((2 * DFF, F32),), accs=(2 * DFF,) * 4)

    def conv_bwd_fn(i, dv, nxt, cw_ref):
        last = ((i + 1) * tmc) % S == 0
        u1, u2 = _conv_taps_up(dv, nxt, last)
        return cw_ref[2:3, :] * dv + cw_ref[1:2, :] * u1 + cw_ref[0:1, :] * u2

    du = _rowwise("conv_bwd", conv_bwd_fn, T, tmc, tiles=(dyc,), halos=((dyc, 1),), consts=(conv_w8,),
                  outs=((2 * DFF, BF16),))[0]
    dh2 = _mm_nt("dh2", du, w_up)
    g_w_down = _mm_tn("dw_down", act, dffn)
    g_w_up = _mm_tn("dw_up", x1, du, mod=(sc_f, sh_f))

    def ln1_bwd_fn(i, dr2v, dh2v, xv, mixv, x1v, scfv, gav, l1g_ref):
        dx1 = ALPHA * dr2v + dh2v * (1.0 + scfv)
        _, n1, rstd = _ln_fwd(ALPHA * xv + gav * mixv, l1g_ref[...], 0.0)
        dr1 = _ln_bwd(dx1, n1, rstd, l1g_ref[...])
        return (dr1, gav * dr1, _rsum8(dx1 * n1), _rsum8(dx1),
                _rsum8(dh2v * x1v), _rsum8(dh2v), _rsum8(dr1 * mixv))

    dr1, dmix, d_ln1g, d_ln1b, d_scf, d_shf, d_ga = _rowwise(
        "ln1_bwd", ln1_bwd_fn, T, 256, tiles=(dr2, dh2, x, mix, x1), seqvecs=(sc_f, g_a), consts=(ln1_g,),
        outs=((D, F32), (D, BF16)), accs=(D, D), seqaccs=(D, D, D))

    dmerged = _mm_nt("dmerged", dmix, w_out)
    g_w_out = _mm_tn("dw_out", merged, dmix)

    def hn_bwd_fn(i, dmv, oav, obv, gn_ref, g_ref):
        o = jnp.concatenate([oav, obv], axis=1)
        rs = lax.rsqrt(_head_mean(o * o, g_ref) + RMS_EPS)
        nrm = o * rs
        dn = dmv * gn_ref[...]
        do = rs * (dn - nrm * _head_mean(dn * nrm, g_ref))
        return do, _rsum8(dmv * nrm)

    do, d_gn = _rowwise("headnorm_bwd", hn_bwd_fn, T, 256, tiles=(dmerged, oa, ob), consts=(gn, gseg),
                        outs=((2 * WA, F32),), accs=(2 * WA,))
    dqa, dka, dva, dqe, dek = _attn_bwd("fox_bwd", qkv, (0, 1, 2), oa, do, 0, lse_a, True, eq=eq, ek=ek)
    dqb, dkb, dvb = _attn_bwd("dil_bwd", qkv, (3, 4, 5), ob, do, 1, lse_b, False, bias=bias)
    hdot = lambda a, m_ref: jnp.dot(a, m_ref[...], preferred_element_type=F32, precision=lax.Precision.HIGHEST)
    dfa, d_bf = _cumsum_seq(
        "fgate_bwd", [dqe, dek, fa], [bf_pad, jnp.asarray(sq), jnp.asarray(sk)],
        lambda dq_, dk_, f, b_ref, sq_ref, sk_ref: hdot(dq_, sq_ref) - hdot(dk_, sk_ref),
        lambda cum, dq_, dk_, f, b_ref, sq_ref, sk_ref: (cum * _sigmoid(-(f + b_ref[...])),),
        ((FPAD, F32),), reverse=True, n_acc=1)

    def dz_fn(i, a0, a1, a2, b0, b1, b2, fv, cv, s1v, s2v):
        ct, s1t, s2t = (_tile_lanes(t, WA) for t in (cv, s1v, s2v))
        return jnp.concatenate([a0, a1, a2, _rope(b0, ct, s1t, s2t), _rope(b1, ct, s1t, s2t), b2, fv], axis=1)

    dz = _rowwise("dz_pack", dz_fn, T, 256, tiles=(dqa, dka, dva, dqb, dkb, dvb, dfa) + neg_rope,
                  outs=((6 * WA + FPAD, BF16),))[0]
    w_cat = jnp.concatenate([w_qkv, w_f], axis=1)
    dh1 = _mm_nt("dh1", dz, w_cat)
    g_w_cat = _mm_tn("dw_in", x, dz, mod=(sc_a, sh_a))

    def dx_fn(i, dr1v, dh1v, xv, scav):
        return ALPHA * dr1v + dh1v * (1.0 + scav), _rsum8(dh1v * xv), _rsum8(dh1v)

    grad_x, d_sca, d_sha = _rowwise("dx_out", dx_fn, T, 256, tiles=(dr1, dh1, x), seqvecs=(sc_a,),
                                    outs=((D, F32),), seqaccs=(D, D))

    row0 = lambda a: a[..., 0, :]
    d_ada = jnp.stack([row0(d_sha), row0(d_sca), row0(d_ga), row0(d_shf), row0(d_scf), row0(d_gf)], axis=1)
    d_cw = jnp.stack([row0(d_cw0), row0(d_cw1), row0(d_cw2)], axis=0)
    loss_part = (0.5 / D) * jnp.sum(loss_acc[0])
    small = dict(b_fgate=row0(d_bf)[:nha], gn=row0(d_gn), ln1_g=row0(d_ln1g), ln1_b=row0(d_ln1b),
                 conv_b=row0(d_cb), ln2_g=row0(d_ln2g), ln2_b=row0(d_ln2b))
    big = dict(w_cat=g_w_cat, w_out=g_w_out, w_up=g_w_up, conv_w=d_cw, w_down=g_w_down)
    return loss_part, grad_x, d_ada, small, big


def _rows_of(n, w=None):
    return -(-n // (D if w is None else w))


def _as_rows(v):
    w = D
    k = v.shape[0]
    flat = v.reshape(k, -1)
    rows = _rows_of(_rows_of(flat.shape[1], w), SUBLANES) * SUBLANES
    flat = jnp.pad(flat, ((0, 0), (0, rows * w - flat.shape[1])))
    return flat.reshape(k, rows, w)


def kernel(x, c, positions, w_ada, b_ada, w_in, b_fgate, gn_a, gn_b, w_out, ln1_g, ln1_b, w_up, conv_w, conv_b, w_down, ln2_g, ln2_b, loss_target, m_w_ada, m_b_ada, m_w_in, m_b_fgate, m_gn_a, m_gn_b, m_w_out, m_ln1_g, m_ln1_b, m_w_up, m_conv_w, m_conv_b, m_w_down, m_ln2_g, m_ln2_b, v_w_ada, v_b_ada, v_w_in, v_b_fgate, v_gn_a, v_gn_b, v_w_out, v_ln1_g, v_ln1_b, v_w_up, v_conv_w, v_conv_b, v_w_down, v_ln2_g, v_ln2_b):
    mx, my, mc = lax.axis_index("x"), lax.axis_index("y"), lax.axis_index("c")
    dev = _dev_index(mx, my, mc)
    chip = _chip_index(mx, my, mc)
    nbl = x.shape[0]
    T = nbl * S
    nha = WA // HD
    d_in = w_in.shape[2] * NCHIP
    n_ada = w_ada.shape[2]

    c_pad = jnp.zeros((SUBLANES, D), F32).at[:nbl].set(c)
    c_all = _all_gather8("gather_c", c_pad)[:, :nbl].reshape(NDEV * nbl, D)
    ada_part = _ada_fwd(c_all, w_ada[0])
    n_cw = conv_w.shape[2]
    cw_rows = jnp.pad(conv_w[0], ((0, SUBLANES - conv_w.shape[1]), (0, n_ada - n_cw)))
    ada_blocks = _all_gather8("gather_ada", jnp.concatenate([ada_part, cw_rows], axis=0))
    n_c = NDEV * nbl
    ada_all = jnp.concatenate([ada_blocks[2 * k, :n_c] for k in range(NCHIP)], axis=1) + b_ada
    conv_w8 = jnp.concatenate([ada_blocks[2 * k, n_c:, :n_cw] for k in range(NCHIP)], axis=1)
    ada = lax.dynamic_slice_in_dim(ada_all, dev * nbl, nbl, axis=0).reshape(nbl, 6, D)

    w_in_sh = jnp.pad(w_in[0].astype(BF16), ((0, 0), (0, _rows_of(w_in.shape[2], LANES) * LANES - w_in.shape[2])))
    shards = [w_in_sh, w_out[0].astype(BF16), w_up[0].astype(BF16), w_down[0].astype(BF16)]
    halves = [t.reshape(2, t.shape[0] // 2, t.shape[1]) for t in shards]
    gathered_w = _gather_halves("gather_w", halves)
    g_in, g_out, g_up, g_down = (_by_chip(h, g, chip).reshape((NCHIP,) + t.shape)
                                 for g, h, t in zip(gathered_w, halves, shards))
    w_in_full = jnp.concatenate([g_in[k][:, :w_in.shape[2]] for k in range(NCHIP)], axis=1)
    w_qkv = jnp.concatenate([w_in_full[:, :3 * WA], w_in_full[:, 3 * WA + nha:]], axis=1)
    w_f = jnp.pad(w_in_full[:, 3 * WA:3 * WA + nha], ((0, 0), (0, FPAD - nha)))
    w_out_full = g_out.reshape(NCHIP * w_out.shape[1], D)
    w_up_full = jnp.concatenate([g_up[k] for k in range(NCHIP)], axis=1)
    w_down_full = g_down.reshape(NCHIP * w_down.shape[1], D)

    gn = jnp.concatenate([gn_a, gn_b], axis=1)
    loss_part, grad_x, d_ada, small, big = _local_step(
        x.reshape(T, D), loss_target.reshape(T, D), positions, ada, w_qkv, w_f, w_out_full, w_up_full, conv_w8,
        w_down_full, b_fgate, gn, ln1_g, ln1_b, conv_b, ln2_g, ln2_b)

    def row_pad(v, rows):
        flat = v.reshape(-1)
        return jnp.pad(flat, (0, rows * D - flat.shape[0]))

    n_cb = _rows_of(2 * DFF)
    small_flat = jnp.concatenate([
        row_pad(small["b_fgate"], 1), row_pad(small["gn"], 1), row_pad(small["ln1_g"], 1),
        row_pad(small["ln1_b"], 1), row_pad(small["ln2_g"], 1), row_pad(small["ln2_b"], 1),
        row_pad(jnp.full((1,), loss_part, F32), 1), row_pad(small["conv_b"], n_cb)])
    n_small = _rows_of(small_flat.shape[0], SUBLANES * D) * SUBLANES
    small_rows = jnp.pad(small_flat, (0, n_small * D - small_flat.shape[0])).reshape(n_small, D)
    ada_rows = jnp.pad(d_ada.reshape(nbl, 6, D), ((0, 0), (0, SUBLANES - 6), (0, 0))).reshape(nbl * SUBLANES, D)
    gathered = _all_gather8("gather_small", jnp.concatenate([small_rows, ada_rows], axis=0))
    red = _sum_leading("sum_small", gathered, tm=SUBLANES)
    g_b_fgate = red[0:1, :nha]
    g_gn = red[1:2, :2 * WA]
    g_ln1_g, g_ln1_b, g_ln2_g, g_ln2_b = red[2:3], red[3:4], red[4:5], red[5:6]
    loss = red[6, 0]
    g_conv_b = red[7:7 + n_cb].reshape(1, -1)[:, :2 * DFF]
    g_b_ada = _add2("sum_b_ada", red[n_small:n_small + SUBLANES], red[n_small + SUBLANES:n_small + 2 * SUBLANES],
                    tm=SUBLANES)[:6].reshape(1, 6 * D)
    dada_all = gathered[:, n_small:].reshape(NDEV, nbl, SUBLANES, D)[:, :, :6].reshape(NDEV * nbl, 6 * D)
    g_w_ada = _ada_bwd(c_all, lax.dynamic_slice_in_dim(dada_all, chip * n_ada, n_ada, axis=1))

    g_cat = big["w_cat"]
    g_w_in_full = jnp.concatenate([g_cat[:, :3 * WA], g_cat[:, 6 * WA:6 * WA + nha], g_cat[:, 3 * WA:6 * WA]], axis=1)
    sh_in = g_w_in_full.reshape(D, NCHIP, -1).transpose(1, 0, 2)
    sh_out = big["w_out"].reshape(NCHIP, -1, D)
    sh_up = big["w_up"].reshape(D, NCHIP, -1).transpose(1, 0, 2)
    sh_cw = big["conv_w"].reshape(conv_w.shape[1], NCHIP, -1).transpose(1, 0, 2)
    sh_down = big["w_down"].reshape(NCHIP, -1, D)
    parts = [_as_rows(t) for t in (sh_in, sh_out, sh_up, sh_cw, sh_down)]
    part_rows = [p.shape[1] for p in parts]
    packed = jnp.concatenate(parts, axis=1)
    n_rows = _rows_of(packed.shape[1], 2 * LANES) * 2 * LANES
    half = n_rows // 2
    packed = jnp.pad(packed, ((0, 0), (0, n_rows - packed.shape[1]), (0, 0)))
    halves = packed.reshape(NCHIP, 2, half, D)
    mine = lax.dynamic_index_in_dim(halves, mc, axis=1, keepdims=False).reshape(NCHIP * half, D)
    theirs = lax.dynamic_index_in_dim(halves, 1 - mc, axis=1, keepdims=False).reshape(NCHIP * half, D)
    from_sib = _to_sibling("pair_swap", theirs)
    pair_sum = _add2("pair_sum", mine, from_sib, out_dtype=BF16).reshape(NCHIP, half, D)
    by_chip = _scatter_chips("scatter_grads", pair_sum)
    own = lax.dynamic_index_in_dim(pair_sum, chip, axis=0, keepdims=False)
    my_half = _sum_leading("chip_sum", _by_chip(own, by_chip, chip))
    sib_half = _to_sibling("pair_share", my_half)
    pair = jnp.stack([my_half, sib_half])
    shard = jnp.concatenate([lax.dynamic_index_in_dim(pair, mc, axis=0, keepdims=False),
                             lax.dynamic_index_in_dim(pair, 1 - mc, axis=0, keepdims=False)], axis=0)

    def unpack(k, shape):
        start = sum(part_rows[:k])
        n = int(np.prod(shape))
        return shard[start:start + part_rows[k]].reshape(-1)[:n].reshape(shape)

    g_w_in = unpack(0, w_in.shape[1:])
    g_w_out = unpack(1, w_out.shape[1:])
    g_w_up = unpack(2, w_up.shape[1:])
    g_conv_w = unpack(3, conv_w.shape[1:])
    g_w_down = unpack(4, w_down.shape[1:])

    grads = dict(w_ada=g_w_ada, b_ada=g_b_ada, w_in=g_w_in, b_fgate=g_b_fgate, gn_a=g_gn[:, :WA], gn_b=g_gn[:, WA:],
                 w_out=g_w_out, ln1_g=g_ln1_g, ln1_b=g_ln1_b, w_up=g_w_up, conv_w=g_conv_w, conv_b=g_conv_b,
                 w_down=g_w_down, ln2_g=g_ln2_g, ln2_b=g_ln2_b)
    weights = dict(w_ada=w_ada, b_ada=b_ada, w_in=w_in, b_fgate=b_fgate, gn_a=gn_a, gn_b=gn_b, w_out=w_out,
                   ln1_g=ln1_g, ln1_b=ln1_b, w_up=w_up, conv_w=conv_w, conv_b=conv_b, w_down=w_down,
                   ln2_g=ln2_g, ln2_b=ln2_b)
    ms = dict(w_ada=m_w_ada, b_ada=m_b_ada, w_in=m_w_in, b_fgate=m_b_fgate, gn_a=m_gn_a, gn_b=m_gn_b,
              w_out=m_w_out, ln1_g=m_ln1_g, ln1_b=m_ln1_b, w_up=m_w_up, conv_w=m_conv_w, conv_b=m_conv_b,
              w_down=m_w_down, ln2_g=m_ln2_g, ln2_b=m_ln2_b)
    vs = dict(w_ada=v_w_ada, b_ada=v_b_ada, w_in=v_w_in, b_fgate=v_b_fgate, gn_a=v_gn_a, gn_b=v_gn_b,
              w_out=v_w_out, ln1_g=v_ln1_g, ln1_b=v_ln1_b, w_up=v_w_up, conv_w=v_conv_w, conv_b=v_conv_b,
              w_down=v_w_down, ln2_g=v_ln2_g, ln2_b=v_ln2_b)
    names = list(weights)
    big_names = ("w_ada", "w_in", "w_out", "w_up", "w_down")
    delta, new_m, new_v = {}, {}, {}
    for n in big_names:
        shp = weights[n].shape
        d, m2, v2 = _adamw("adamw_" + n, weights[n][0], grads[n].reshape(shp[1:]), ms[n][0], vs[n][0])
        delta[n], new_m[n], new_v[n] = d.reshape(shp), m2.reshape(shp), v2.reshape(shp)
    small_names = [n for n in names if n not in big_names]

    def pack_small(src):
        flats = []
        for n in small_names:
            flat = src[n].reshape(-1)
            flats.append(jnp.pad(flat, (0, _rows_of(flat.shape[0]) * D - flat.shape[0])))
        allf = jnp.concatenate(flats)
        rows = _rows_of(allf.shape[0], SUBLANES * D) * SUBLANES
        return jnp.pad(allf, (0, rows * D - allf.shape[0])).reshape(rows, D)

    sd, sm, sv_ = _adamw("adamw_small", pack_small(weights), pack_small(grads), pack_small(ms), pack_small(vs))
    off = 0
    for n in small_names:
        shp = weights[n].shape
        cnt = int(np.prod(shp))
        r = _rows_of(cnt)
        for dst, src in ((delta, sd), (new_m, sm), (new_v, sv_)):
            dst[n] = src[off:off + r].reshape(-1)[:cnt].reshape(shp)
        off += r

    out_g = {n: grads[n].reshape(weights[n].shape) for n in names}
    return (loss, grad_x.reshape(x.shape), *[out_g[n] for n in names], *[delta[n] for n in names],
            *[new_m[n] for n in names], *[new_v[n] for n in names])
```

```python
import functools
import math

import numpy as np
import jax
import jax.numpy as jnp
from jax import lax
from jax.experimental import pallas as pl
from jax.experimental.pallas import tpu as pltpu

F32 = jnp.float32
BF16 = jnp.bfloat16

D = 1024
S = 4096
HD = 64
WA = 512
DFF = 2816
NCHIP = 4
NDEV = 8
PATTERNS = ((128, 1), (512, 4), (2048, 16))
ROPE_THETA = 500000.0
ROPE_DIMS = HD // 4
ALPHA = (2.0 * 1) ** 0.25
LN_EPS = 1e-5
RMS_EPS = 1e-6
ADAM_LR = 0.001
ADAM_B1 = 0.9
ADAM_B2 = 0.999
ADAM_EPS = 1e-08
ADAM_WD = 0.01
ADAM_STEP = 10

LANES = 128
SUBLANES = 8
TQ = 512
FPAD = LANES
NEG = -1e30
VMEM_LIMIT = 56 * 1024 * 1024
MESH = pl.DeviceIdType.MESH


def _cparams(sem):
    return pltpu.CompilerParams(dimension_semantics=sem, vmem_limit_bytes=VMEM_LIMIT)


def _pick(n, cands):
    for c in cands:
        if n % c == 0:
            return c
    return n


def _rsum8(v):
    tm, w = v.shape
    return jnp.sum(v.reshape(tm // SUBLANES, SUBLANES, w), axis=0)


def _sigmoid(x):
    return 1.0 / (1.0 + jnp.exp(-x))


def _dot(a, b):
    return jnp.dot(a, b, preferred_element_type=F32)


def _dot_nt(a, b):
    return lax.dot_general(a, b, (((1,), (1,)), ((), ())), preferred_element_type=F32)


def _dot_tn(a, b):
    return lax.dot_general(a, b, (((0,), (0,)), ((), ())), preferred_element_type=F32)


def _rowwise(name, fn, T, tm, *, tiles=(), halos=(), seqvecs=(), consts=(), outs=(), accs=(), seqaccs=(),
             seq_len=None):
    seq_len = S if seq_len is None else seq_len
    nb = T // tm
    spb = max(seq_len // tm, 1)
    nseq = max(T // seq_len, 1)
    n8 = T // SUBLANES
    r8 = tm // SUBLANES
    in_specs, args = [], []
    for a in tiles:
        in_specs.append(pl.BlockSpec((tm, a.shape[1]), lambda i: (i, 0)))
        args.append(a)
    for a, direction in halos:
        if direction < 0:
            idx = lambda i: (jnp.maximum(i * r8 - 1, 0), 0)
        else:
            idx = lambda i: (jnp.minimum((i + 1) * r8, n8 - 1), 0)
        in_specs.append(pl.BlockSpec((SUBLANES, a.shape[1]), idx))
        args.append(a)
    for a in seqvecs:
        in_specs.append(pl.BlockSpec((1, 1, a.shape[2]), lambda i: (i // spb, 0, 0)))
        args.append(a)
    for a in consts:
        in_specs.append(pl.BlockSpec(a.shape, lambda i, nd=a.ndim: (0,) * nd))
        args.append(a)
    out_shape, out_specs = [], []
    for w, dt in outs:
        out_shape.append(jax.ShapeDtypeStruct((T, w), dt))
        out_specs.append(pl.BlockSpec((tm, w), lambda i: (i, 0)))
    for w in accs:
        out_shape.append(jax.ShapeDtypeStruct((SUBLANES, w), F32))
        out_specs.append(pl.BlockSpec((SUBLANES, w), lambda i: (0, 0)))
    for w in seqaccs:
        out_shape.append(jax.ShapeDtypeStruct((nseq, SUBLANES, w), F32))
        out_specs.append(pl.BlockSpec((1, SUBLANES, w), lambda i: (i // spb, 0, 0)))
    n_t, n_h, n_s, n_c = len(tiles), len(halos), len(seqvecs), len(consts)
    n_o, n_a, n_sa = len(outs), len(accs), len(seqaccs)

    def body(*refs):
        i = pl.program_id(0)
        ins = refs[:n_t + n_h + n_s + n_c]
        orefs = refs[n_t + n_h + n_s + n_c:]
        vals = [r[...] for r in ins[:n_t + n_h]]
        vals += [r[0] for r in ins[n_t + n_h:n_t + n_h + n_s]]
        vals += list(ins[n_t + n_h + n_s:])
        res = fn(i, *vals)
        if not isinstance(res, (tuple, list)):
            res = (res,)
        for k in range(n_o):
            orefs[k][...] = res[k].astype(orefs[k].dtype)
        for k in range(n_a):
            r = orefs[n_o + k]

            @pl.when(i == 0)
            def _():
                r[...] = jnp.zeros_like(r)

            r[...] += res[n_o + k]

            @pl.when(i == nb - 1)
            def _():
                r[...] = jnp.broadcast_to(jnp.sum(r[...], axis=0, keepdims=True), r.shape)
        for k in range(n_sa):
            r = orefs[n_o + n_a + k]

            @pl.when(i % spb == 0)
            def _():
                r[...] = jnp.zeros_like(r)

            r[0] += res[n_o + n_a + k]

            @pl.when(i % spb == spb - 1)
            def _():
                r[0] = jnp.broadcast_to(jnp.sum(r[0], axis=0, keepdims=True), r.shape[1:])

    sem = ("arbitrary",) if (n_a or n_sa) else ("parallel",)
    res = pl.pallas_call(
        body, name=name, grid=(nb,), in_specs=in_specs, out_specs=out_specs, out_shape=out_shape,
        compiler_params=_cparams(sem),
    )(*args)
    return res


def _ln_fwd(r, g, b):
    mu = jnp.mean(r, axis=-1, keepdims=True)
    xc = r - mu
    var = jnp.mean(xc * xc, axis=-1, keepdims=True)
    rstd = lax.rsqrt(var + LN_EPS)
    n = xc * rstd
    return n * g + b, n, rstd


def _ln_bwd(dy, n, rstd, g):
    dn = dy * g
    return rstd * (dn - jnp.mean(dn, axis=-1, keepdims=True) - n * jnp.mean(dn * n, axis=-1, keepdims=True))


def _head_mean(t, g_ref):
    gw = g_ref.shape[0]
    hi = t.astype(BF16)
    lo = (t - hi.astype(F32)).astype(BF16)
    g = g_ref[...]
    parts = []
    for c in range(t.shape[1] // gw):
        sl = slice(c * gw, (c + 1) * gw)
        parts.append(_dot(hi[:, sl], g) + _dot(lo[:, sl], g))
    out = parts[0] if len(parts) == 1 else jnp.concatenate(parts, axis=1)
    return out * (1.0 / HD)


def _rope(z, c, s1, s2):
    w = z.shape[1]
    half = ROPE_DIMS // 2
    return z * c + pltpu.roll(z, half, 1) * s1 + pltpu.roll(z, w - half, 1) * s2


def _tile_lanes(t, w):
    reps = w // t.shape[1]
    return t if reps == 1 else jnp.concatenate([t] * reps, axis=1)


def _conv_taps(ext, prev, first):
    tm = ext.shape[0]
    prev = jnp.where(first, jnp.zeros_like(prev), prev)
    r8 = lax.broadcasted_iota(jnp.int32, (SUBLANES, 1), 0)
    top = ext[0:SUBLANES]
    s1_top = jnp.where(r8 < 1, pltpu.roll(prev, 1, 0), pltpu.roll(top, 1, 0))
    s2_top = jnp.where(r8 < 2, pltpu.roll(prev, 2, 0), pltpu.roll(top, 2, 0))
    s1 = jnp.concatenate([s1_top, pltpu.roll(ext, 1, 0)[SUBLANES:]], axis=0)
    s2 = jnp.concatenate([s2_top, pltpu.roll(ext, 2, 0)[SUBLANES:]], axis=0)
    return s1, s2


def _conv_taps_up(ext, nxt, last):
    tm = ext.shape[0]
    nxt = jnp.where(last, jnp.zeros_like(nxt), nxt)
    r8 = lax.broadcasted_iota(jnp.int32, (SUBLANES, 1), 0)
    bot = ext[tm - SUBLANES:tm]
    u1_bot = jnp.where(r8 >= 7, pltpu.roll(nxt, 7, 0), pltpu.roll(bot, 7, 0))
    u2_bot = jnp.where(r8 >= 6, pltpu.roll(nxt, 6, 0), pltpu.roll(bot, 6, 0))
    u1 = jnp.concatenate([pltpu.roll(ext, tm - 1, 0)[:tm - SUBLANES], u1_bot], axis=0)
    u2 = jnp.concatenate([pltpu.roll(ext, tm - 2, 0)[:tm - SUBLANES], u2_bot], axis=0)
    return u1, u2


def _mm_nt(name, a, w, tm=256):
    T = a.shape[0]
    n = w.shape[0]
    ch = _pick(n, (512, 256, 128))

    def fn(i, av, w_ref):
        ab = av.astype(BF16)
        parts = [_dot_nt(ab, w_ref[c * ch:(c + 1) * ch, :]) for c in range(n // ch)]
        return parts[0] if len(parts) == 1 else jnp.concatenate(parts, axis=1)

    return _rowwise(name, fn, T, tm, tiles=(a,), consts=(w,), outs=((n, F32),))[0]


def _mm_tn(name, a, b, *, mod=None, tt=512):
    T, k1 = a.shape
    k2 = b.shape[1]
    t1 = k1 if k1 <= 1536 else _pick(k1, (1408, 1024, 512, 256, 128))
    t2 = k2 if k2 <= 1536 else _pick(k2, (1408, 1024, 640, 512, 256, 128))
    tt = min(tt, S)
    spb = S // tt

    def body(*refs):
        if mod is not None:
            a_ref, sc_ref, sh_ref, b_ref, o_ref = refs
        else:
            a_ref, b_ref, o_ref = refs
        t = pl.program_id(2)

        @pl.when(t == 0)
        def _():
            o_ref[...] = jnp.zeros_like(o_ref)

        av = a_ref[...]
        if mod is not None:
            av = av * (1.0 + sc_ref[0]) + sh_ref[0]
        o_ref[...] += _dot_tn(av.astype(BF16), b_ref[...].astype(BF16))

    in_specs = [pl.BlockSpec((tt, t1), lambda p, q, t: (t, p))]
    args = [a]
    if mod is not None:
        for v in mod:
            in_specs.append(pl.BlockSpec((1, 1, t1), lambda p, q, t: (t // spb, 0, p)))
            args.append(v)
    in_specs.append(pl.BlockSpec((tt, t2), lambda p, q, t: (t, q)))
    args.append(b)
    return pl.pallas_call(
        body, name=name, grid=(k1 // t1, k2 // t2, T // tt), in_specs=in_specs,
        out_specs=pl.BlockSpec((t1, t2), lambda p, q, t: (p, q)),
        out_shape=jax.ShapeDtypeStruct((k1, k2), F32),
        compiler_params=_cparams(("parallel", "parallel", "arbitrary")),
    )(*args)


def _mod_mm(name, x, sc, sh, ws, out_dtypes, rope=None, rope_secs=(), tm=256):
    T = x.shape[0]
    nw = len(ws)

    def fn(i, xv, *rest):
        if rope is not None:
            cv, s1v, s2v = rest[:3]
            rest = rest[3:]
        scv, shv = rest[:2]
        w_refs = rest[2:]
        h = (xv * (1.0 + scv) + shv).astype(BF16)
        res = []
        for k, w_ref in enumerate(w_refs):
            n = w_ref.shape[1]
            ch = WA if (k == 0 and rope is not None) else _pick(n, (512, 256, 128))
            parts = []
            for c in range(n // ch):
                z = _dot(h, w_ref[:, c * ch:(c + 1) * ch])
                if k == 0 and c in rope_secs:
                    z = _rope(z, _tile_lanes(cv, ch), _tile_lanes(s1v, ch), _tile_lanes(s2v, ch))
                parts.append(z.astype(out_dtypes[k]))
            res.append(parts[0] if len(parts) == 1 else jnp.concatenate(parts, axis=1))
        return tuple(res)

    tiles = (x,) + (tuple(rope) if rope is not None else ())
    outs = tuple((w.shape[1], dt) for w, dt in zip(ws, out_dtypes))
    return _rowwise(name, fn, T, tm, tiles=tiles, seqvecs=(sc, sh), consts=tuple(ws), outs=outs)


def _tri(tb, lower):
    r = lax.broadcasted_iota(jnp.int32, (tb, tb), 0)
    c = lax.broadcasted_iota(jnp.int32, (tb, tb), 1)
    return jnp.where((r >= c) if lower else (r <= c), 1.0, 0.0).astype(F32)


def _cumsum_seq(name, ins, consts, fn_in, fn_out, outs, reverse, n_acc=0, tb=512):
    T = ins[0].shape[0]
    tb = min(tb, S)
    nbs = S // tb
    nseq = T // S
    n_i, n_c, n_o = len(ins), len(consts), len(outs)

    def blk(b, j):
        return (b * nbs + (nbs - 1 - j if reverse else j), 0)

    def body(*refs):
        i_refs, c_refs = refs[:n_i], refs[n_i:n_i + n_c]
        o_refs = refs[n_i + n_c:n_i + n_c + n_o]
        acc_refs = refs[n_i + n_c + n_o:n_i + n_c + n_o + n_acc]
        carry = refs[-1]
        b, j = pl.program_id(0), pl.program_id(1)

        @pl.when(j == 0)
        def _():
            carry[...] = jnp.zeros_like(carry)

        iv = [r[...] for r in i_refs]
        xin = fn_in(*iv, *c_refs)
        cum = jnp.dot(_tri(tb, not reverse), xin, preferred_element_type=F32,
                      precision=lax.Precision.HIGHEST) + carry[0:1, :]
        carry[...] = carry[...] + jnp.sum(xin, axis=0, keepdims=True)
        res = fn_out(cum, *iv, *c_refs)
        for o, r in zip(o_refs, res):
            o[...] = r.astype(o.dtype)
        for a in acc_refs:
            @pl.when((b == 0) & (j == 0))
            def _():
                a[...] = jnp.zeros_like(a)

            a[...] += _rsum8(res[0])

            @pl.when((b == nseq - 1) & (j == nbs - 1))
            def _():
                a[...] = jnp.broadcast_to(jnp.sum(a[...], axis=0, keepdims=True), a.shape)

    in_specs = [pl.BlockSpec((tb, a.shape[1]), blk) for a in ins]
    in_specs += [pl.BlockSpec(c.shape, lambda b, j, nd=c.ndim: (0,) * nd) for c in consts]
    out_shape = [jax.ShapeDtypeStruct((T, w), dt) for w, dt in outs]
    out_shape += [jax.ShapeDtypeStruct((SUBLANES, outs[0][0]), F32)] * n_acc
    out_specs = [pl.BlockSpec((tb, w), blk) for w, _ in outs]
    out_specs += [pl.BlockSpec((SUBLANES, outs[0][0]), lambda b, j: (0, 0))] * n_acc
    return pl.pallas_call(
        body, name=name, grid=(nseq, nbs), in_specs=in_specs, out_specs=out_specs, out_shape=out_shape,
        scratch_shapes=[pltpu.VMEM((SUBLANES, FPAD), F32)],
        compiler_params=_cparams(("arbitrary", "arbitrary")),
    )(*ins, *consts)


def _log_sigmoid(x):
    return jnp.minimum(x, 0.0) - jnp.log(1.0 + jnp.exp(-jnp.abs(x)))


def _dil_bias(tq):
    max_win = max(w for w, _ in PATTERNS)
    nd = (max_win + tq - 1) // tq + 1
    qi = np.arange(tq)[:, None]
    kj = np.arange(tq)[None, :]
    tabs = []
    for dlt in range(nd):
        dist = dlt * tq + qi - kj
        mult = np.zeros((tq, tq), np.float64)
        for win, dil in PATTERNS:
            mult += (dist >= 0) & (dist % dil == 0) & (dist // dil <= win // dil)
        tabs.append(np.where(mult > 0, np.log(np.maximum(mult, 1.0)), NEG))
    return np.stack(tabs).astype(np.float32)


def _fold_tables(nha):
    hp_n = nha // 2
    pq = np.zeros((3, FPAD, hp_n * 2 * LANES), np.float32)
    pk = np.zeros((3, FPAD, hp_n * LANES), np.float32)
    oq = np.zeros((1, hp_n * 2 * LANES), np.float32)
    ok = np.zeros((1, hp_n * LANES), np.float32)
    sq = np.zeros((hp_n * LANES, FPAD), np.float32)
    sk = np.zeros((hp_n * LANES, FPAD), np.float32)
    for h in range(nha):
        hp, odd = divmod(h, 2)
        qb = hp * 2 * LANES + odd * (LANES + 8)
        kb = hp * LANES + odd * 8
        for i in range(3):
            pq[i, h, qb + i] = 1
            oq[0, qb + 3 + i] = 1
            ok[0, kb + i] = 1
            pk[i, h, kb + 3 + i] = 1
        sq[kb, h] = 1
        sk[kb + 3, h] = 1
    return pq, pk, oq, ok, sq, sk


def _stack_heads(x2, h0, extra=None):
    z = jnp.zeros_like(x2)
    a, b = jnp.where(h0, x2, z), jnp.where(h0, z, x2)
    if extra is not None:
        a = jnp.concatenate([a, extra[:, :LANES]], axis=1)
        b = jnp.concatenate([b, extra[:, LANES:]], axis=1)
    return jnp.concatenate([a, b], axis=0)


def _attn_fwd(name, qkv, secs, fox, eq=None, ek=None, bias=None):
    T = qkv.shape[0]
    nq = S // TQ
    nbl = T // S
    hp_n = WA // LANES
    sq, sk, sv = (s * hp_n for s in secs)
    scale = HD ** -0.5
    nd = None if fox else bias.shape[0]

    def body(*refs):
        if fox:
            q_ref, k_ref, v_ref, eq_ref, ek_ref, o_ref, lse_ref = refs
        else:
            q_ref, k_ref, v_ref, b_ref, o_ref, lse_ref = refs
        i = pl.program_id(2)
        lane = lax.broadcasted_iota(jnp.int32, (1, LANES), 1)
        h0 = lane < HD
        q2 = (q_ref[...].astype(F32) * scale).astype(BF16)
        qs = _stack_heads(q2, h0, eq_ref[...] if fox else None)

        def scores(t, diag):
            off = pl.multiple_of((i - t) * TQ, TQ)
            kk = k_ref[pl.ds(off, TQ), :]
            if fox:
                kk = jnp.concatenate([kk, ek_ref[pl.ds(off, TQ), :]], axis=1)
            s = jnp.concatenate([_dot_nt(qs[:TQ], kk), _dot_nt(qs[TQ:], kk)], axis=0)
            if not fox:
                s = (s.reshape(2, TQ, TQ) + b_ref[t]).reshape(2 * TQ, TQ)
            elif diag:
                rows = lax.broadcasted_iota(jnp.int32, (2, TQ, TQ), 1).reshape(2 * TQ, TQ)
                cols = lax.broadcasted_iota(jnp.int32, (2 * TQ, TQ), 1)
                s = jnp.where(cols <= rows, s, NEG)
            return s

        def update(t, s, m, l, acc):
            off = pl.multiple_of((i - t) * TQ, TQ)
            v2 = v_ref[pl.ds(off, TQ), :]
            m_new = jnp.maximum(m, jnp.max(s, axis=1, keepdims=True))
            p = jnp.exp(s - m_new)
            a = jnp.exp(m - m_new)
            l = a * l + jnp.sum(p, axis=1, keepdims=True)
            pb = p.astype(BF16)
            acc = a * acc + jnp.concatenate([_dot(pb[:TQ], v2), _dot(pb[TQ:], v2)], axis=0)
            return m_new, l, acc

        def step(t, carry):
            s, m, l, acc = carry
            return (scores(t + 1, False),) + update(t, s, m, l, acc)

        init = (jnp.full((2 * TQ, 1), NEG, F32), jnp.zeros((2 * TQ, 1), F32), jnp.zeros((2 * TQ, LANES), F32))
        n = i + 1 if fox else jnp.minimum(i + 1, nd)
        s, m, l, acc = lax.fori_loop(0, n - 1, step, (scores(0, True),) + init)
        m, l, acc = update(n - 1, s, m, l, acc)
        on = acc / l
        o_ref[...] = jnp.where(h0, on[:TQ], on[TQ:])
        lse = jnp.broadcast_to(m + jnp.log(l), (2 * TQ, LANES))
        lse_ref[...] = jnp.concatenate([lse[:TQ], lse[TQ:]], axis=1)

    in_specs = [
        pl.BlockSpec((TQ, LANES), lambda b, hp, i: (b * nq + i, sq + hp)),
        pl.BlockSpec((S, LANES), lambda b, hp, i: (b, sk + hp)),
        pl.BlockSpec((S, LANES), lambda b, hp, i: (b, sv + hp)),
    ]
    args = [qkv, qkv, qkv]
    if fox:
        in_specs += [pl.BlockSpec((TQ, 2 * LANES), lambda b, hp, i: (b * nq + i, hp)),
                     pl.BlockSpec((S, LANES), lambda b, hp, i: (b, hp))]
        args += [eq, ek]
    else:
        in_specs.append(pl.BlockSpec(bias.shape, lambda b, hp, i: (0, 0, 0)))
        args.append(bias)
    return pl.pallas_call(
        body, name=name, grid=(nbl, hp_n, nq), in_specs=in_specs,
        out_specs=[pl.BlockSpec((TQ, LANES), lambda b, hp, i: (b * nq + i, hp)),
                   pl.BlockSpec((TQ, 2 * LANES), lambda b, hp, i: (b * nq + i, hp))],
        out_shape=[jax.ShapeDtypeStruct((T, WA), F32), jax.ShapeDtypeStruct((T, 2 * WA), F32)],
        compiler_params=_cparams(("parallel", "parallel", "arbitrary")),
    )(*args)


def _attn_bwd(name, qkv, secs, o, do, do_sec, lse, fox, eq=None, ek=None, bias=None):
    T = qkv.shape[0]
    nq = S // TQ
    nbl = T // S
    hp_n = WA // LANES
    sq, sk, sv = (s * hp_n for s in secs)
    dsec = do_sec * hp_n
    scale = HD ** -0.5
    nd = None if fox else bias.shape[0]
    kc = 2 * LANES if fox else LANES

    def body(*refs):
        if fox:
            (q_ref, k_ref, v_ref, o_ref, do_ref, lse_ref, eq_ref, ek_ref,
             dq_ref, dk_ref, dv_ref, dqe_ref, dek_ref, dl_ref) = refs
        else:
            q_ref, k_ref, v_ref, o_ref, do_ref, lse_ref, b_ref, dq_ref, dk_ref, dv_ref, dl_ref = refs
        j = pl.program_id(2)
        lane = lax.broadcasted_iota(jnp.int32, (1, LANES), 1)
        h0 = lane < HD

        @pl.when(j == 0)
        def _():
            dq_ref[...] = jnp.zeros_like(dq_ref)
            if fox:
                dqe_ref[...] = jnp.zeros_like(dqe_ref)

            def dl_step(r, c):
                off = pl.multiple_of(r * TQ, TQ)
                d2 = do_ref[pl.ds(off, TQ), :] * o_ref[pl.ds(off, TQ), :]
                z2 = jnp.zeros_like(d2)
                dl0 = jnp.sum(jnp.where(h0, d2, z2), axis=1, keepdims=True)
                dl1 = jnp.sum(jnp.where(h0, z2, d2), axis=1, keepdims=True)
                dl_ref[pl.ds(off, TQ), :] = jnp.concatenate(
                    [jnp.broadcast_to(dl0, (TQ, LANES)), jnp.broadcast_to(dl1, (TQ, LANES))], axis=1)
                return c

            lax.fori_loop(0, nq, dl_step, 0)

        kk = k_ref[...]
        if fox:
            kk = jnp.concatenate([kk, ek_ref[...]], axis=1)
        v2 = v_ref[...]

        def wide(x2):
            st = jnp.concatenate([x2[:, :LANES], x2[:, LANES:]], axis=0)
            return st if TQ == LANES else jnp.concatenate([st] * (TQ // LANES), axis=1)

        def step(t, carry, diag):
            dkk, dv2 = carry
            off = pl.multiple_of((j + t) * TQ, TQ)
            q2 = (q_ref[pl.ds(off, TQ), :].astype(F32) * scale).astype(BF16)
            qs = _stack_heads(q2, h0, eq_ref[pl.ds(off, TQ), :] if fox else None)
            dos = _stack_heads(do_ref[pl.ds(off, TQ), :].astype(BF16), h0)
            s = jnp.concatenate([_dot_nt(qs[:TQ], kk), _dot_nt(qs[TQ:], kk)], axis=0)
            if not fox:
                s = (s.reshape(2, TQ, TQ) + b_ref[t]).reshape(2 * TQ, TQ)
            elif diag:
                rows = lax.broadcasted_iota(jnp.int32, (2, TQ, TQ), 1).reshape(2 * TQ, TQ)
                cols = lax.broadcasted_iota(jnp.int32, (2 * TQ, TQ), 1)
                s = jnp.where(cols <= rows, s, NEG)
            p = jnp.exp(s - wide(lse_ref[pl.ds(off, TQ), :]))
            dp = jnp.concatenate([_dot_nt(dos[:TQ], v2), _dot_nt(dos[TQ:], v2)], axis=0)
            dsb = (p * (dp - wide(dl_ref[pl.ds(off, TQ), :]))).astype(BF16)
            dv2 = dv2 + _dot_tn(p.astype(BF16), dos)
            dkk = dkk + _dot_tn(dsb, qs)
            dqq = jnp.concatenate([_dot(dsb[:TQ], kk), _dot(dsb[TQ:], kk)], axis=0)
            dq_ref[pl.ds(off, TQ), :] += jnp.where(h0, dqq[:TQ, :LANES], dqq[TQ:, :LANES])
            if fox:
                dqe_ref[pl.ds(off, TQ), :] += jnp.where(lane < SUBLANES, dqq[:TQ, LANES:], dqq[TQ:, LANES:])
            return dkk, dv2

        zero = (jnp.zeros((TQ, kc), F32), jnp.zeros((TQ, LANES), F32))
        if fox:
            dkk, dv2 = lax.fori_loop(1, nq - j, lambda t, c: step(t, c, False), step(0, zero, True))
        else:
            dkk, dv2 = lax.fori_loop(0, jnp.minimum(nq - j, nd), lambda t, c: step(t, c, False), zero)
        dk_ref[...] = dkk[:, :LANES]
        dv_ref[...] = dv2
        if fox:
            dek_ref[...] = dkk[:, LANES:]

        @pl.when(j == nq - 1)
        def _():
            dq_ref[...] = dq_ref[...] * scale

    seq = lambda c, w=LANES: pl.BlockSpec((S, w), lambda b, hp, j: (b, c + hp))
    blk = lambda c: pl.BlockSpec((TQ, LANES), lambda b, hp, j: (b * nq + j, c + hp))
    in_specs = [seq(sq), blk(sk), blk(sv), seq(0), seq(dsec), seq(0, 2 * LANES)]
    args = [qkv, qkv, qkv, o, do, lse]
    if fox:
        in_specs += [seq(0, 2 * LANES), blk(0)]
        args += [eq, ek]
    else:
        in_specs.append(pl.BlockSpec(bias.shape, lambda b, hp, j: (0, 0, 0)))
        args.append(bias)
    out_specs = [seq(0), blk(0), blk(0)]
    out_shape = [jax.ShapeDtypeStruct((T, WA), F32)] * 3
    if fox:
        out_specs += [seq(0), blk(0)]
        out_shape += [jax.ShapeDtypeStruct((T, WA), F32)] * 2
    return pl.pallas_call(
        body, name=name, grid=(nbl, hp_n, nq), in_specs=in_specs, out_specs=out_specs, out_shape=out_shape,
        scratch_shapes=[pltpu.VMEM((S, 2 * LANES), F32)],
        compiler_params=_cparams(("parallel", "parallel", "arbitrary")),
    )(*args)


def _exchange(name, ins, out_shapes, remote, local):
    n_in, n_out = len(ins), len(out_shapes)
    nr, nl = len(remote), len(local)

    def body(*refs):
        in_refs = refs[:n_in]
        out_refs = refs[n_in:n_in + n_out]
        send_sems, recv_sems, loc_sems = refs[n_in + n_out:]
        me = (lax.axis_index("x"), lax.axis_index("y"), lax.axis_index("c"))

        def peer_of(flip):
            return tuple(1 - v if f else v for v, f in zip(me, flip))

        def at(ref, idx):
            return ref if idx is None else ref.at[idx]

        def rcopy(k, who):
            flip, a, sfn, b, dfn = remote[k]
            return pltpu.make_async_remote_copy(
                src_ref=at(in_refs[a], sfn(*who)), dst_ref=at(out_refs[b], dfn(*who)),
                send_sem=send_sems.at[k], recv_sem=recv_sems.at[k],
                device_id=peer_of(flip), device_id_type=MESH)

        locs = [pltpu.make_async_copy(at(in_refs[a], sfn(*me)), at(out_refs[b], dfn(*me)), loc_sems.at[k])
                for k, (a, sfn, b, dfn) in enumerate(local)]
        for cp in locs:
            cp.start()
        sends = [rcopy(k, me) for k in range(nr)]
        for cp in sends:
            cp.start()
        for k in range(nr):
            rcopy(k, peer_of(remote[k][0])).wait_recv()
        for cp in sends:
            cp.wait_send()
        for cp in locs:
            cp.wait()

    any_spec = pl.BlockSpec(memory_space=pl.ANY)
    return pl.pallas_call(
        body, name=name, in_specs=[any_spec] * n_in, out_specs=[any_spec] * n_out, out_shape=list(out_shapes),
        scratch_shapes=[pltpu.SemaphoreType.DMA((max(nr, 1),)), pltpu.SemaphoreType.DMA((max(nr, 1),)),
                        pltpu.SemaphoreType.DMA((max(nl, 1),))],
    )(*ins)


_FLIPS7 = [(0, 0, 1), (0, 1, 0), (0, 1, 1), (1, 0, 0), (1, 0, 1), (1, 1, 0), (1, 1, 1)]
_CHIP_FLIPS = [(1, 0, 0), (0, 1, 0), (1, 1, 0)]


def _dev_index(x, y, c):
    return 4 * x + 2 * y + c


def _chip_index(x, y, c):
    return 2 * x + y


def _all_gather8(name, v):
    remote = [(f, 0, lambda x, y, c: None, 0, _dev_index) for f in _FLIPS7]
    local = [(0, lambda x, y, c: None, 0, _dev_index)]
    return _exchange(name, [v], [jax.ShapeDtypeStruct((NDEV,) + v.shape, v.dtype)], remote, local)[0]


def _gather_halves(name, vs):
    n_v = len(vs)

    def body(*refs):
        in_refs, out_refs = refs[:n_v], refs[n_v:2 * n_v]
        send_sems, recv_sems = refs[2 * n_v:]
        x, y, c = lax.axis_index("x"), lax.axis_index("y"), lax.axis_index("c")
        sibling = (x, y, 1 - c)
        chips = [(1 - x, y), (x, 1 - y), (1 - x, 1 - y)]

        def copy(k, n, src, blk, half, to):
            return pltpu.make_async_remote_copy(
                src_ref=src, dst_ref=out_refs[n].at[blk, half], send_sem=send_sems.at[k], recv_sem=recv_sems.at[k],
                device_id=to, device_id_type=MESH)

        first = [copy(6 * n + j, n, in_refs[n].at[c], j, c, (*chip, c))
                 for n in range(n_v) for j, chip in enumerate(chips)]
        for cp in first:
            cp.start()
        passed = []
        for n in range(n_v):
            for j, chip in enumerate(chips):
                copy(6 * n + j, n, in_refs[n].at[c], j, c, (*chip, c)).wait_recv()
                fw = copy(6 * n + 3 + j, n, out_refs[n].at[j, c], j, c, sibling)
                fw.start()
                passed.append(fw)
        for n in range(n_v):
            for j in range(len(chips)):
                copy(6 * n + 3 + j, n, out_refs[n].at[j, 1 - c], j, 1 - c, sibling).wait_recv()
        for cp in first + passed:
            cp.wait_send()

    any_spec = pl.BlockSpec(memory_space=pl.ANY)
    return pl.pallas_call(
        body, name=name, in_specs=[any_spec] * n_v, out_specs=[any_spec] * n_v,
        out_shape=[jax.ShapeDtypeStruct((NCHIP - 1,) + v.shape, v.dtype) for v in vs],
        scratch_shapes=[pltpu.SemaphoreType.DMA((6 * n_v,)), pltpu.SemaphoreType.DMA((6 * n_v,))],
    )(*vs)


def _to_sibling(name, v):
    remote = [((0, 0, 1), 0, lambda x, y, c: None, 0, lambda x, y, c: None)]
    return _exchange(name, [v], [jax.ShapeDtypeStruct(v.shape, v.dtype)], remote, [])[0]


def _scatter_chips(name, v):
    remote = []
    for j, f in enumerate(_CHIP_FLIPS):
        src = lambda x, y, c, f=f: _chip_index(1 - x if f[0] else x, 1 - y if f[1] else y, c)
        remote.append((f, 0, src, 0, lambda x, y, c, j=j: j))
    return _exchange(name, [v], [jax.ShapeDtypeStruct((NCHIP - 1,) + v.shape[1:], v.dtype)], remote, [])[0]


def _by_chip(own, others, chip):
    stacked = jnp.concatenate([own[None], others], axis=0)
    blocks = []
    for k in range(NCHIP):
        d = k ^ chip
        place = jnp.where(d == 0, 0, jnp.where(d == 2, 1, jnp.where(d == 1, 2, 3)))
        blocks.append(lax.dynamic_index_in_dim(stacked, place, axis=0, keepdims=False))
    return jnp.stack(blocks)


def _sum_leading(name, v, tm=None):
    n, r, w = v.shape
    tm = _pick(r, (256, 128, 64, 32, 16, 8)) if tm is None else tm

    def body(v_ref, o_ref):
        acc = v_ref[0].astype(F32)
        for k in range(1, n):
            acc = acc + v_ref[k].astype(F32)
        o_ref[...] = acc

    return pl.pallas_call(
        body, name=name, grid=(r // tm,), in_specs=[pl.BlockSpec((n, tm, w), lambda i: (0, i, 0))],
        out_specs=pl.BlockSpec((tm, w), lambda i: (i, 0)), out_shape=jax.ShapeDtypeStruct((r, w), F32),
        compiler_params=_cparams(("parallel",)),
    )(v)


def _add2(name, a, b, tm=None, out_dtype=F32):
    r, w = a.shape
    tm = _pick(r, (256, 128, 64, 32, 16, 8)) if tm is None else tm

    def body(a_ref, b_ref, o_ref):
        o_ref[...] = (a_ref[...] + b_ref[...]).astype(out_dtype)

    spec = pl.BlockSpec((tm, w), lambda i: (i, 0))
    return pl.pallas_call(
        body, name=name, grid=(r // tm,), in_specs=[spec, spec], out_specs=spec,
        out_shape=jax.ShapeDtypeStruct((r, w), out_dtype), compiler_params=_cparams(("parallel",)),
    )(a, b)


def _ada_fwd(call_all, w_shard):
    def body(c_ref, w_ref, o_ref):
        cv = c_ref[...]
        o_ref[...] = jnp.dot(cv * _sigmoid(cv), w_ref[...], preferred_element_type=F32,
                             precision=lax.Precision.HIGHEST)

    n = w_shard.shape[1]
    return pl.pallas_call(
        body, name="ada_fwd", out_shape=jax.ShapeDtypeStruct((call_all.shape[0], n), F32),
        compiler_params=pltpu.CompilerParams(vmem_limit_bytes=VMEM_LIMIT),
    )(call_all, w_shard)


def _ada_bwd(call_all, dada):
    def body(c_ref, d_ref, o_ref):
        cv = c_ref[...]
        o_ref[...] = lax.dot_general(cv * _sigmoid(cv), d_ref[...], (((0,), (0,)), ((), ())),
                                     preferred_element_type=F32, precision=lax.Precision.HIGHEST)

    return pl.pallas_call(
        body, name="ada_bwd", out_shape=jax.ShapeDtypeStruct((call_all.shape[1], dada.shape[1]), F32),
        compiler_params=pltpu.CompilerParams(vmem_limit_bytes=VMEM_LIMIT),
    )(call_all, dada)


def _adamw(name, w, g, m, v):
    r, wd = w.shape
    tm = _pick(r, (256, 128, 64, 32, 16, 8))
    bc1 = 1.0 - ADAM_B1 ** ADAM_STEP
    bc2 = 1.0 - ADAM_B2 ** ADAM_STEP

    def body(w_ref, g_ref, m_ref, v_ref, d_ref, mo_ref, vo_ref):
        gv = g_ref[...]
        mn = ADAM_B1 * m_ref[...] + (1.0 - ADAM_B1) * gv
        vn = ADAM_B2 * v_ref[...] + (1.0 - ADAM_B2) * (gv * gv)
        d_ref[...] = -ADAM_LR * ((mn / bc1) / (jnp.sqrt(vn / bc2) + ADAM_EPS) + ADAM_WD * w_ref[...])
        mo_ref[...] = mn
        vo_ref[...] = vn

    spec = pl.BlockSpec((tm, wd), lambda i: (i, 0))
    return pl.pallas_call(
        body, name=name, grid=(r // tm,), in_specs=[spec] * 4, out_specs=[spec] * 3,
        out_shape=[jax.ShapeDtypeStruct((r, wd), F32)] * 3, compiler_params=_cparams(("parallel",)),
    )(w, g, m, v)


def _rope_tables(positions):
    half = ROPE_DIMS // 2
    freqs = ROPE_THETA ** (-jnp.arange(0, ROPE_DIMS, 2, dtype=F32) / ROPE_DIMS)
    ang = positions.astype(F32).reshape(-1, 1) * freqs
    cos, sin = jnp.cos(ang), jnp.sin(ang)
    T = ang.shape[0]
    one = jnp.ones((T, HD - ROPE_DIMS), F32)
    zero = jnp.zeros((T, HD - ROPE_DIMS), F32)
    zh = jnp.zeros((T, half), F32)
    c64 = jnp.concatenate([cos, cos, one], axis=1)
    s1 = jnp.concatenate([zh, sin, zero], axis=1)
    s2 = jnp.concatenate([-sin, zh, zero], axis=1)
    rep = lambda t: jnp.concatenate([t] * (LANES // HD), axis=1)
    return rep(c64), rep(s1), rep(s2)


def _local_step(x, loss_target, positions, ada, w_qkv, w_f, w_out, w_up, conv_w8, w_down,
                b_fgate, gn, ln1_g, ln1_b, conv_b, ln2_g, ln2_b):
    T = x.shape[0]
    nbl = T // S
    nha = WA // HD
    sv = lambda k: ada[:, k:k + 1, :]
    sh_a, sc_a, g_a, sh_f, sc_f, g_f = (sv(k) for k in range(6))
    rope = _rope_tables(positions)
    neg_rope = (rope[0], -rope[1], -rope[2])
    gseg = jnp.asarray(np.kron(np.eye(min(256, 2 * WA) // HD), np.ones((HD, HD))), BF16)
    bias = jnp.asarray(_dil_bias(TQ))
    bf_pad = jnp.zeros((1, FPAD), F32).at[:, :nha].set(b_fgate)

    qkv, fa = _mod_mm("qkv_proj", x, sc_a, sh_a, (w_qkv, w_f), (BF16, F32), rope=rope, rope_secs=(3, 4))
    pq, pk, oq, ok, sq, sk = _fold_tables(nha)

    def fold_out(cum, f, b_ref, pq_ref, pk_ref, oq_ref, ok_ref):
        hi = cum.astype(BF16)
        r1 = cum - hi.astype(F32)
        mid = r1.astype(BF16)
        lo = (r1 - mid.astype(F32)).astype(BF16)
        eqv = _dot(hi, pq_ref[0]) + _dot(mid, pq_ref[1]) + _dot(lo, pq_ref[2]) + oq_ref[...]
        ekv = ok_ref[...] - (_dot(hi, pk_ref[0]) + _dot(mid, pk_ref[1]) + _dot(lo, pk_ref[2]))
        return eqv, ekv

    eq, ek = _cumsum_seq(
        "fgate_fwd", [fa], [bf_pad, jnp.asarray(pq, BF16), jnp.asarray(pk, BF16), jnp.asarray(oq), jnp.asarray(ok)],
        lambda f, b_ref, *_: _log_sigmoid(f + b_ref[...]), fold_out, ((2 * WA, BF16), (WA, BF16)), reverse=False)
    oa, lse_a = _attn_fwd("fox_fwd", qkv, (0, 1, 2), True, eq=eq, ek=ek)
    ob, lse_b = _attn_fwd("dil_fwd", qkv, (3, 4, 5), False, bias=bias)

    def mix_fn(i, oav, obv, xv, gav, gn_ref, g_ref, wo_ref, l1g_ref, l1b_ref):
        o = jnp.concatenate([oav, obv], axis=1)
        rs = lax.rsqrt(_head_mean(o * o, g_ref) + RMS_EPS)
        merged = (o * rs * gn_ref[...]).astype(BF16)
        mix = _dot(merged, wo_ref[...])
        x1, _, _ = _ln_fwd(ALPHA * xv + gav * mix, l1g_ref[...], l1b_ref[...])
        return merged, mix, x1

    merged, mix, x1 = _rowwise("mix_out", mix_fn, T, 256, tiles=(oa, ob, x), seqvecs=(g_a,),
                               consts=(gn, gseg, w_out, ln1_g, ln1_b),
                               outs=((2 * WA, BF16), (D, F32), (D, F32)))
    u = _mod_mm("ffn_up", x1, sc_f, sh_f, (w_up,), (F32,))[0]

    def conv_y(i, uv, prev, cw_ref, cb_ref, tm):
        first = (i * tm) % S == 0
        s1, s2 = _conv_taps(uv, prev, first)
        y = cb_ref[...] + cw_ref[0:1, :] * s2 + cw_ref[1:2, :] * s1 + cw_ref[2:3, :] * uv
        return y, s1, s2

    tmc = 128

    def gate_fn(i, uv, prev, cw_ref, cb_ref):
        y, _, _ = conv_y(i, uv, prev, cw_ref, cb_ref, tmc)
        a, g = y[:, :DFF], y[:, DFF:]
        return g * _sigmoid(g) * a

    act = _rowwise("conv_gate", gate_fn, T, tmc, tiles=(u,), halos=((u, -1),), consts=(conv_w8, conv_b),
                   outs=((DFF, BF16),))[0]

    def down_fn(i, actv, x1v, tgt, gfv, wd_ref, g2_ref, b2_ref):
        ffn = _dot(actv, wd_ref[...])
        y, n2, rstd = _ln_fwd(ALPHA * x1v + gfv * ffn, g2_ref[...], b2_ref[...])
        err = y - tgt
        dy = err * (1.0 / D)
        dr2 = _ln_bwd(dy, n2, rstd, g2_ref[...])
        return (dr2, gfv * dr2, _rsum8(err * err), _rsum8(dy * n2), _rsum8(dy), _rsum8(dr2 * ffn))

    dr2, dffn, loss_acc, d_ln2g, d_ln2b, d_gf = _rowwise(
        "ffn_down_loss", down_fn, T, 256, tiles=(act, x1, loss_target), seqvecs=(g_f,),
        consts=(w_down, ln2_g, ln2_b), outs=((D, F32), (D, BF16)), accs=(D, D, D), seqaccs=(D,))

    dact = _mm_nt("dact", dffn, w_down)

    def gate_bwd_fn(i, uv, dav, prev, cw_ref, cb_ref):
        y, s1, s2 = conv_y(i, uv, prev, cw_ref, cb_ref, tmc)
        a, g = y[:, :DFF], y[:, DFF:]
        sg = _sigmoid(g)
        dyc = jnp.concatenate([dav * (g * sg), dav * a * (sg * (1.0 + g * (1.0 - sg)))], axis=1)
        return dyc, _rsum8(dyc), _rsum8(dyc * s2), _rsum8(dyc * s1), _rsum8(dyc * uv)

    dyc, d_cb, d_cw0, d_cw1, d_cw2 = _rowwise(
        "gate_bwd", gate_bwd_fn, T, tmc, tiles=(u, dact), halos=((u, -1),), consts=(conv_w8, conv_b),
        outs=((2 * DFF, F32),), accs=(2 * DFF,) * 4)

    def conv_bwd_fn(i, dv, nxt, cw_ref):
        last = ((i + 1) * tmc) % S == 0
        u1, u2 = _conv_taps_up(dv, nxt, last)
        return cw_ref[2:3, :] * dv + cw_ref[1:2, :] * u1 + cw_ref[0:1, :] * u2

    du = _rowwise("conv_bwd", conv_bwd_fn, T, tmc, tiles=(dyc,), halos=((dyc, 1),), consts=(conv_w8,),
                  outs=((2 * DFF, BF16),))[0]
    dh2 = _mm_nt("dh2", du, w_up)
    g_w_down = _mm_tn("dw_down", act, dffn)
    g_w_up = _mm_tn("dw_up", x1, du, mod=(sc_f, sh_f))

    def ln1_bwd_fn(i, dr2v, dh2v, xv, mixv, x1v, scfv, gav, l1g_ref):
        dx1 = ALPHA * dr2v + dh2v * (1.0 + scfv)
        _, n1, rstd = _ln_fwd(ALPHA * xv + gav * mixv, l1g_ref[...], 0.0)
        dr1 = _ln_bwd(dx1, n1, rstd, l1g_ref[...])
        return (dr1, gav * dr1, _rsum8(dx1 * n1), _rsum8(dx1),
                _rsum8(dh2v * x1v), _rsum8(dh2v), _rsum8(dr1 * mixv))

    dr1, dmix, d_ln1g, d_ln1b, d_scf, d_shf, d_ga = _rowwise(
        "ln1_bwd", ln1_bwd_fn, T, 256, tiles=(dr2, dh2, x, mix, x1), seqvecs=(sc_f, g_a), consts=(ln1_g,),
        outs=((D, F32), (D, BF16)), accs=(D, D), seqaccs=(D, D, D))

    dmerged = _mm_nt("dmerged", dmix, w_out)
    g_w_out = _mm_tn("dw_out", merged, dmix)

    def hn_bwd_fn(i, dmv, oav, obv, gn_ref, g_ref):
        o = jnp.concatenate([oav, obv], axis=1)
        rs = lax.rsqrt(_head_mean(o * o, g_ref) + RMS_EPS)
        nrm = o * rs
        dn = dmv * gn_ref[...]
        do = rs * (dn - nrm * _head_mean(dn * nrm, g_ref))
        return do, _rsum8(dmv * nrm)

    do, d_gn = _rowwise("headnorm_bwd", hn_bwd_fn, T, 256, tiles=(dmerged, oa, ob), consts=(gn, gseg),
                        outs=((2 * WA, F32),), accs=(2 * WA,))
    dqa, dka, dva, dqe, dek = _attn_bwd("fox_bwd", qkv, (0, 1, 2), oa, do, 0, lse_a, True, eq=eq, ek=ek)
    dqb, dkb, dvb = _attn_bwd("dil_bwd", qkv, (3, 4, 5), ob, do, 1, lse_b, False, bias=bias)
    hdot = lambda a, m_ref: jnp.dot(a, m_ref[...], preferred_element_type=F32, precision=lax.Precision.HIGHEST)
    dfa, d_bf = _cumsum_seq(
        "fgate_bwd", [dqe, dek, fa], [bf_pad, jnp.asarray(sq), jnp.asarray(sk)],
        lambda dq_, dk_, f, b_ref, sq_ref, sk_ref: hdot(dq_, sq_ref) - hdot(dk_, sk_ref),
        lambda cum, dq_, dk_, f, b_ref, sq_ref, sk_ref: (cum * _sigmoid(-(f + b_ref[...])),),
        ((FPAD, F32),), reverse=True, n_acc=1)

    def dz_fn(i, a0, a1, a2, b0, b1, b2, fv, cv, s1v, s2v):
        ct, s1t, s2t = (_tile_lanes(t, WA) for t in (cv, s1v, s2v))
        return jnp.concatenate([a0, a1, a2, _rope(b0, ct, s1t, s2t), _rope(b1, ct, s1t, s2t), b2, fv], axis=1)

    dz = _rowwise("dz_pack", dz_fn, T, 256, tiles=(dqa, dka, dva, dqb, dkb, dvb, dfa) + neg_rope,
                  outs=((6 * WA + FPAD, BF16),))[0]
    w_cat = jnp.concatenate([w_qkv, w_f], axis=1)
    dh1 = _mm_nt("dh1", dz, w_cat)
    g_w_cat = _mm_tn("dw_in", x, dz, mod=(sc_a, sh_a))

    def dx_fn(i, dr1v, dh1v, xv, scav):
        return ALPHA * dr1v + dh1v * (1.0 + scav), _rsum8(dh1v * xv), _rsum8(dh1v)

    grad_x, d_sca, d_sha = _rowwise("dx_out", dx_fn, T, 256, tiles=(dr1, dh1, x), seqvecs=(sc_a,),
                                    outs=((D, F32),), seqaccs=(D, D))

    row0 = lambda a: a[..., 0, :]
    d_ada = jnp.stack([row0(d_sha), row0(d_sca), row0(d_ga), row0(d_shf), row0(d_scf), row0(d_gf)], axis=1)
    d_cw = jnp.stack([row0(d_cw0), row0(d_cw1), row0(d_cw2)], axis=0)
    loss_part = (0.5 / D) * jnp.sum(loss_acc[0])
    small = dict(b_fgate=row0(d_bf)[:nha], gn=row0(d_gn), ln1_g=row0(d_ln1g), ln1_b=row0(d_ln1b),
                 conv_b=row0(d_cb), ln2_g=row0(d_ln2g), ln2_b=row0(d_ln2b))
    big = dict(w_cat=g_w_cat, w_out=g_w_out, w_up=g_w_up, conv_w=d_cw, w_down=g_w_down)
    return loss_part, grad_x, d_ada, small, big


def _rows_of(n, w=None):
    return -(-n // (D if w is None else w))


def _as_rows(v):
    w = D
    k = v.shape[0]
    flat = v.reshape(k, -1)
    rows = _rows_of(_rows_of(flat.shape[1], w), SUBLANES) * SUBLANES
    flat = jnp.pad(flat, ((0, 0), (0, rows * w - flat.shape[1])))
    return flat.reshape(k, rows, w)


def kernel(x, c, positions, w_ada, b_ada, w_in, b_fgate, gn_a, gn_b, w_out, ln1_g, ln1_b, w_up, conv_w, conv_b, w_down, ln2_g, ln2_b, loss_target, m_w_ada, m_b_ada, m_w_in, m_b_fgate, m_gn_a, m_gn_b, m_w_out, m_ln1_g, m_ln1_b, m_w_up, m_conv_w, m_conv_b, m_w_down, m_ln2_g, m_ln2_b, v_w_ada, v_b_ada, v_w_in, v_b_fgate, v_gn_a, v_gn_b, v_w_out, v_ln1_g, v_ln1_b, v_w_up, v_conv_w, v_conv_b, v_w_down, v_ln2_g, v_ln2_b):
    mx, my, mc = lax.axis_index("x"), lax.axis_index("y"), lax.axis_index("c")
    dev = _dev_index(mx, my, mc)
    chip = _chip_index(mx, my, mc)
    nbl = x.shape[0]
    T = nbl * S
    nha = WA // HD
    d_in = w_in.shape[2] * NCHIP
    n_ada = w_ada.shape[2]

    c_pad = jnp.zeros((SUBLANES, D), F32).at[:nbl].set(c)
    c_all = _all_gather8("gather_c", c_pad)[:, :nbl].reshape(NDEV * nbl, D)
    ada_part = _ada_fwd(c_all, w_ada[0])
    n_cw = conv_w.shape[2]
    cw_rows = jnp.pad(conv_w[0], ((0, SUBLANES - conv_w.shape[1]), (0, n_ada - n_cw)))
    ada_blocks = _all_gather8("gather_ada", jnp.concatenate([ada_part, cw_rows], axis=0))
    n_c = NDEV * nbl
    ada_all = jnp.concatenate([ada_blocks[2 * k, :n_c] for k in range(NCHIP)], axis=1) + b_ada
    conv_w8 = jnp.concatenate([ada_blocks[2 * k, n_c:, :n_cw] for k in range(NCHIP)], axis=1)
    ada = lax.dynamic_slice_in_dim(ada_all, dev * nbl, nbl, axis=0).reshape(nbl, 6, D)

    w_in_sh = jnp.pad(w_in[0].astype(BF16), ((0, 0), (0, _rows_of(w_in.shape[2], LANES) * LANES - w_in.shape[2])))
    shards = [w_in_sh, w_out[0].astype(BF16), w_up[0].astype(BF16), w_down[0].astype(BF16)]
    halves = [t.reshape(2, t.shape[0] // 2, t.shape[1]) for t in shards]
    gathered_w = _gather_halves("gather_w", halves)
    g_in, g_out, g_up, g_down = (_by_chip(h, g, chip).reshape((NCHIP,) + t.shape)
                                 for g, h, t in zip(gathered_w, halves, shards))
    w_in_full = jnp.concatenate([g_in[k][:, :w_in.shape[2]] for k in range(NCHIP)], axis=1)
    w_qkv = jnp.concatenate([w_in_full[:, :3 * WA], w_in_full[:, 3 * WA + nha:]], axis=1)
    w_f = jnp.pad(w_in_full[:, 3 * WA:3 * WA + nha], ((0, 0), (0, FPAD - nha)))
    w_out_full = g_out.reshape(NCHIP * w_out.shape[1], D)
    w_up_full = jnp.concatenate([g_up[k] for k in range(NCHIP)], axis=1)
    w_down_full = g_down.reshape(NCHIP * w_down.shape[1], D)

    gn = jnp.concatenate([gn_a, gn_b], axis=1)
    loss_part, grad_x, d_ada, small, big = _local_step(
        x.reshape(T, D), loss_target.reshape(T, D), positions, ada, w_qkv, w_f, w_out_full, w_up_full, conv_w8,
        w_down_full, b_fgate, gn, ln1_g, ln1_b, conv_b, ln2_g, ln2_b)

    def row_pad(v, rows):
        flat = v.reshape(-1)
        return jnp.pad(flat, (0, rows * D - flat.shape[0]))

    n_cb = _rows_of(2 * DFF)
    small_flat = jnp.concatenate([
        row_pad(small["b_fgate"], 1), row_pad(small["gn"], 1), row_pad(small["ln1_g"], 1),
        row_pad(small["ln1_b"], 1), row_pad(small["ln2_g"], 1), row_pad(small["ln2_b"], 1),
        row_pad(jnp.full((1,), loss_part, F32), 1), row_pad(small["conv_b"], n_cb)])
    n_small = _rows_of(small_flat.shape[0], SUBLANES * D) * SUBLANES
    small_rows = jnp.pad(small_flat, (0, n_small * D - small_flat.shape[0])).reshape(n_small, D)
    ada_rows = jnp.pad(d_ada.reshape(nbl, 6, D), ((0, 0), (0, SUBLANES - 6), (0, 0))).reshape(nbl * SUBLANES, D)
    gathered = _all_gather8("gather_small", jnp.concatenate([small_rows, ada_rows], axis=0))
    red = _sum_leading("sum_small", gathered, tm=SUBLANES)
    g_b_fgate = red[0:1, :nha]
    g_gn = red[1:2, :2 * WA]
    g_ln1_g, g_ln1_b, g_ln2_g, g_ln2_b = red[2:3], red[3:4], red[4:5], red[5:6]
    loss = red[6, 0]
    g_conv_b = red[7:7 + n_cb].reshape(1, -1)[:, :2 * DFF]
    g_b_ada = _add2("sum_b_ada", red[n_small:n_small + SUBLANES], red[n_small + SUBLANES:n_small + 2 * SUBLANES],
                    tm=SUBLANES)[:6].reshape(1, 6 * D)
    dada_all = gathered[:, n_small:].reshape(NDEV, nbl, SUBLANES, D)[:, :, :6].reshape(NDEV * nbl, 6 * D)
    g_w_ada = _ada_bwd(c_all, lax.dynamic_slice_in_dim(dada_all, chip * n_ada, n_ada, axis=1))

    g_cat = big["w_cat"]
    g_w_in_full = jnp.concatenate([g_cat[:, :3 * WA], g_cat[:, 6 * WA:6 * WA + nha], g_cat[:, 3 * WA:6 * WA]], axis=1)
    sh_in = g_w_in_full.reshape(D, NCHIP, -1).transpose(1, 0, 2)
    sh_out = big["w_out"].reshape(NCHIP, -1, D)
    sh_up = big["w_up"].reshape(D, NCHIP, -1).transpose(1, 0, 2)
    sh_cw = big["conv_w"].reshape(conv_w.shape[1], NCHIP, -1).transpose(1, 0, 2)
    sh_down = big["w_down"].reshape(NCHIP, -1, D)
    parts = [_as_rows(t) for t in (sh_in, sh_out, sh_up, sh_cw, sh_down)]
    part_rows = [p.shape[1] for p in parts]
    packed = jnp.concatenate(parts, axis=1)
    n_rows = _rows_of(packed.shape[1], 2 * LANES) * 2 * LANES
    half = n_rows // 2
    packed = jnp.pad(packed, ((0, 0), (0, n_rows - packed.shape[1]), (0, 0)))
    halves = packed.reshape(NCHIP, 2, half, D)
    mine = lax.dynamic_index_in_dim(halves, mc, axis=1, keepdims=False).reshape(NCHIP * half, D)
    theirs = lax.dynamic_index_in_dim(halves, 1 - mc, axis=1, keepdims=False).reshape(NCHIP * half, D)
    from_sib = _to_sibling("pair_swap", theirs)
    pair_sum = _add2("pair_sum", mine, from_sib, out_dtype=BF16).reshape(NCHIP, half, D)
    by_chip = _scatter_chips("scatter_grads", pair_sum)
    own = lax.dynamic_index_in_dim(pair_sum, chip, axis=0, keepdims=False)
    my_half = _sum_leading("chip_sum", _by_chip(own, by_chip, chip))
    sib_half = _to_sibling("pair_share", my_half)
    pair = jnp.stack([my_half, sib_half])
    shard = jnp.concatenate([lax.dynamic_index_in_dim(pair, mc, axis=0, keepdims=False),
                             lax.dynamic_index_in_dim(pair, 1 - mc, axis=0, keepdims=False)], axis=0)

    def unpack(k, shape):
        start = sum(part_rows[:k])
        n = int(np.prod(shape))
        return shard[start:start + part_rows[k]].reshape(-1)[:n].reshape(shape)

    g_w_in = unpack(0, w_in.shape[1:])
    g_w_out = unpack(1, w_out.shape[1:])
    g_w_up = unpack(2, w_up.shape[1:])
    g_conv_w = unpack(3, conv_w.shape[1:])
    g_w_down = unpack(4, w_down.shape[1:])

    grads = dict(w_ada=g_w_ada, b_ada=g_b_ada, w_in=g_w_in, b_fgate=g_b_fgate, gn_a=g_gn[:, :WA], gn_b=g_gn[:, WA:],
                 w_out=g_w_out, ln1_g=g_ln1_g, ln1_b=g_ln1_b, w_up=g_w_up, conv_w=g_conv_w, conv_b=g_conv_b,
                 w_down=g_w_down, ln2_g=g_ln2_g, ln2_b=g_ln2_b)
    weights = dict(w_ada=w_ada, b_ada=b_ada, w_in=w_in, b_fgate=b_fgate, gn_a=gn_a, gn_b=gn_b, w_out=w_out,
                   ln1_g=ln1_g, ln1_b=ln1_b, w_up=w_up, conv_w=conv_w, conv_b=conv_b, w_down=w_down,
                   ln2_g=ln2_g, ln2_b=ln2_b)
    ms = dict(w_ada=m_w_ada, b_ada=m_b_ada, w_in=m_w_in, b_fgate=m_b_fgate, gn_a=m_gn_a, gn_b=m_gn_b,
              w_out=m_w_out, ln1_g=m_ln1_g, ln1_b=m_ln1_b, w_up=m_w_up, conv_w=m_conv_w, conv_b=m_conv_b,
              w_down=m_w_down, ln2_g=m_ln2_g, ln2_b=m_ln2_b)
    vs = dict(w_ada=v_w_ada, b_ada=v_b_ada, w_in=v_w_in, b_fgate=v_b_fgate, gn_a=v_gn_a, gn_b=v_gn_b,
              w_out=v_w_out, ln1_g=v_ln1_g, ln1_b=v_ln1_b, w_up=v_w_up, conv_w=v_conv_w, conv_b=v_conv_b,
              w_down=v_w_down, ln2_g=v_ln2_g, ln2_b=v_ln2_b)
    names = list(weights)
    big_names = ("w_ada", "w_in", "w_out", "w_up", "w_down")
    delta, new_m, new_v = {}, {}, {}
    for n in big_names:
        shp = weights[n].shape
        d, m2, v2 = _adamw("adamw_" + n, weights[n][0], grads[n].reshape(shp[1:]), ms[n][0], vs[n][0])
        delta[n], new_m[n], new_v[n] = d.reshape(shp), m2.reshape(shp), v2.reshape(shp)
    small_names = [n for n in names if n not in big_names]

    def pack_small(src):
        flats = []
        for n in small_names:
            flat = src[n].reshape(-1)
            flats.append(jnp.pad(flat, (0, _rows_of(flat.shape[0]) * D - flat.shape[0])))
        allf = jnp.concatenate(flats)
        rows = _rows_of(allf.shape[0], SUBLANES * D) * SUBLANES
        return jnp.pad(allf, (0, rows * D - allf.shape[0])).reshape(rows, D)

    sd, sm, sv_ = _adamw("adamw_small", pack_small(weights), pack_small(grads), pack_small(ms), pack_small(vs))
    off = 0
    for n in small_names:
        shp = weights[n].shape
        cnt = int(np.prod(shp))
        r = _rows_of(cnt)
        for dst, src in ((delta, sd), (new_m, sm), (new_v, sv_)):
            dst[n] = src[off:off + r].reshape(-1)[:cnt].reshape(shp)
        off += r

    out_g = {n: grads[n].reshape(weights[n].shape) for n in names}
    return (loss, grad_x.reshape(x.shape), *[out_g[n] for n in names], *[delta[n] for n in names],
            *[new_m[n] for n in names], *[new_v[n] for n in names])
```

```python
import functools
import math

import numpy as np
import jax
import jax.numpy as jnp
from jax import lax
from jax.experimental import pallas as pl
from jax.experimental.pallas import tpu as pltpu

F32 = jnp.float32
BF16 = jnp.bfloat16

D = 1024
S = 4096
HD = 64
WA = 512
DFF = 2816
NCHIP = 4
NDEV = 8
PATTERNS = ((128, 1), (512, 4), (2048, 16))
ROPE_THETA = 500000.0
ROPE_DIMS = HD // 4
ALPHA = (2.0 * 1) ** 0.25
LN_EPS = 1e-5
RMS_EPS = 1e-6
ADAM_LR = 0.001
ADAM_B1 = 0.9
ADAM_B2 = 0.999
ADAM_EPS = 1e-08
ADAM_WD = 0.01
ADAM_STEP = 10

LANES = 128
SUBLANES = 8
TQ = 512
FPAD = LANES
NEG = -1e30
VMEM_LIMIT = 56 * 1024 * 1024
MESH = pl.DeviceIdType.MESH


def _cparams(sem):
    return pltpu.CompilerParams(dimension_semantics=sem, vmem_limit_bytes=VMEM_LIMIT)


def _pick(n, cands):
    for c in cands:
        if n % c == 0:
            return c
    return n


def _rsum8(v):
    tm, w = v.shape
    return jnp.sum(v.reshape(tm // SUBLANES, SUBLANES, w), axis=0)


def _sigmoid(x):
    return 1.0 / (1.0 + jnp.exp(-x))


def _dot(a, b):
    return jnp.dot(a, b, preferred_element_type=F32)


def _dot_nt(a, b):
    return lax.dot_general(a, b, (((1,), (1,)), ((), ())), preferred_element_type=F32)


def _dot_tn(a, b):
    return lax.dot_general(a, b, (((0,), (0,)), ((), ())), preferred_element_type=F32)


def _rowwise(name, fn, T, tm, *, tiles=(), halos=(), seqvecs=(), consts=(), outs=(), accs=(), seqaccs=(),
             seq_len=None):
    seq_len = S if seq_len is None else seq_len
    nb = T // tm
    spb = max(seq_len // tm, 1)
    nseq = max(T // seq_len, 1)
    n8 = T // SUBLANES
    r8 = tm // SUBLANES
    in_specs, args = [], []
    for a in tiles:
        in_specs.append(pl.BlockSpec((tm, a.shape[1]), lambda i: (i, 0)))
        args.append(a)
    for a, direction in halos:
        if direction < 0:
            idx = lambda i: (jnp.maximum(i * r8 - 1, 0), 0)
        else:
            idx = lambda i: (jnp.minimum((i + 1) * r8, n8 - 1), 0)
        in_specs.append(pl.BlockSpec((SUBLANES, a.shape[1]), idx))
        args.append(a)
    for a in seqvecs:
        in_specs.append(pl.BlockSpec((1, 1, a.shape[2]), lambda i: (i // spb, 0, 0)))
        args.append(a)
    for a in consts:
        in_specs.append(pl.BlockSpec(a.shape, lambda i, nd=a.ndim: (0,) * nd))
        args.append(a)
    out_shape, out_specs = [], []
    for w, dt in outs:
        out_shape.append(jax.ShapeDtypeStruct((T, w), dt))
        out_specs.append(pl.BlockSpec((tm, w), lambda i: (i, 0)))
    for w in accs:
        out_shape.append(jax.ShapeDtypeStruct((SUBLANES, w), F32))
        out_specs.append(pl.BlockSpec((SUBLANES, w), lambda i: (0, 0)))
    for w in seqaccs:
        out_shape.append(jax.ShapeDtypeStruct((nseq, SUBLANES, w), F32))
        out_specs.append(pl.BlockSpec((1, SUBLANES, w), lambda i: (i // spb, 0, 0)))
    n_t, n_h, n_s, n_c = len(tiles), len(halos), len(seqvecs), len(consts)
    n_o, n_a, n_sa = len(outs), len(accs), len(seqaccs)

    def body(*refs):
        i = pl.program_id(0)
        ins = refs[:n_t + n_h + n_s + n_c]
        orefs = refs[n_t + n_h + n_s + n_c:]
        vals = [r[...] for r in ins[:n_t + n_h]]
        vals += [r[0] for r in ins[n_t + n_h:n_t + n_h + n_s]]
        vals += list(ins[n_t + n_h + n_s:])
        res = fn(i, *vals)
        if not isinstance(res, (tuple, list)):
            res = (res,)
        for k in range(n_o):
            orefs[k][...] = res[k].astype(orefs[k].dtype)
        for k in range(n_a):
            r = orefs[n_o + k]

            @pl.when(i == 0)
            def _():
                r[...] = jnp.zeros_like(r)

            r[...] += res[n_o + k]

            @pl.when(i == nb - 1)
            def _():
                r[...] = jnp.broadcast_to(jnp.sum(r[...], axis=0, keepdims=True), r.shape)
        for k in range(n_sa):
            r = orefs[n_o + n_a + k]

            @pl.when(i % spb == 0)
            def _():
                r[...] = jnp.zeros_like(r)

            r[0] += res[n_o + n_a + k]

            @pl.when(i % spb == spb - 1)
            def _():
                r[0] = jnp.broadcast_to(jnp.sum(r[0], axis=0, keepdims=True), r.shape[1:])

    sem = ("arbitrary",) if (n_a or n_sa) else ("parallel",)
    res = pl.pallas_call(
        body, name=name, grid=(nb,), in_specs=in_specs, out_specs=out_specs, out_shape=out_shape,
        compiler_params=_cparams(sem),
    )(*args)
    return res


def _ln_fwd(r, g, b):
    mu = jnp.mean(r, axis=-1, keepdims=True)
    xc = r - mu
    var = jnp.mean(xc * xc, axis=-1, keepdims=True)
    rstd = lax.rsqrt(var + LN_EPS)
    n = xc * rstd
    return n * g + b, n, rstd


def _ln_bwd(dy, n, rstd, g):
    dn = dy * g
    return rstd * (dn - jnp.mean(dn, axis=-1, keepdims=True) - n * jnp.mean(dn * n, axis=-1, keepdims=True))


def _head_mean(t, g_ref):
    gw = g_ref.shape[0]
    hi = t.astype(BF16)
    lo = (t - hi.astype(F32)).astype(BF16)
    g = g_ref[...]
    parts = []
    for c in range(t.shape[1] // gw):
        sl = slice(c * gw, (c + 1) * gw)
        parts.append(_dot(hi[:, sl], g) + _dot(lo[:, sl], g))
    out = parts[0] if len(parts) == 1 else jnp.concatenate(parts, axis=1)
    return out * (1.0 / HD)


def _rope(z, c, s1, s2):
    w = z.shape[1]
    half = ROPE_DIMS // 2
    return z * c + pltpu.roll(z, half, 1) * s1 + pltpu.roll(z, w - half, 1) * s2


def _tile_lanes(t, w):
    reps = w // t.shape[1]
    return t if reps == 1 else jnp.concatenate([t] * reps, axis=1)


def _conv_taps(ext, prev, first):
    tm = ext.shape[0]
    prev = jnp.where(first, jnp.zeros_like(prev), prev)
    r8 = lax.broadcasted_iota(jnp.int32, (SUBLANES, 1), 0)
    top = ext[0:SUBLANES]
    s1_top = jnp.where(r8 < 1, pltpu.roll(prev, 1, 0), pltpu.roll(top, 1, 0))
    s2_top = jnp.where(r8 < 2, pltpu.roll(prev, 2, 0), pltpu.roll(top, 2, 0))
    s1 = jnp.concatenate([s1_top, pltpu.roll(ext, 1, 0)[SUBLANES:]], axis=0)
    s2 = jnp.concatenate([s2_top, pltpu.roll(ext, 2, 0)[SUBLANES:]], axis=0)
    return s1, s2


def _conv_taps_up(ext, nxt, last):
    tm = ext.shape[0]
    nxt = jnp.where(last, jnp.zeros_like(nxt), nxt)
    r8 = lax.broadcasted_iota(jnp.int32, (SUBLANES, 1), 0)
    bot = ext[tm - SUBLANES:tm]
    u1_bot = jnp.where(r8 >= 7, pltpu.roll(nxt, 7, 0), pltpu.roll(bot, 7, 0))
    u2_bot = jnp.where(r8 >= 6, pltpu.roll(nxt, 6, 0), pltpu.roll(bot, 6, 0))
    u1 = jnp.concatenate([pltpu.roll(ext, tm - 1, 0)[:tm - SUBLANES], u1_bot], axis=0)
    u2 = jnp.concatenate([pltpu.roll(ext, tm - 2, 0)[:tm - SUBLANES], u2_bot], axis=0)
    return u1, u2


def _mm_nt(name, a, w, tm=256):
    T = a.shape[0]
    n = w.shape[0]
    ch = _pick(n, (512, 256, 128))

    def fn(i, av, w_ref):
        ab = av.astype(BF16)
        parts = [_dot_nt(ab, w_ref[c * ch:(c + 1) * ch, :]) for c in range(n // ch)]
        return parts[0] if len(parts) == 1 else jnp.concatenate(parts, axis=1)

    return _rowwise(name, fn, T, tm, tiles=(a,), consts=(w,), outs=((n, F32),))[0]


def _mm_tn(name, a, b, *, mod=None, tt=512, by_chip=False):
    T, k1 = a.shape
    k2 = b.shape[1]
    t1 = k1 if k1 <= 1536 else _pick(k1, (1408, 1024, 512, 256, 128))
    t2 = k2 if k2 <= 1536 else _pick(k2, (1408, 1024, 640, 512, 256, 128))
    if by_chip:
        t2 = k2 // NCHIP
    tt = min(tt, S)
    spb = S // tt

    def body(*refs):
        if mod is not None:
            a_ref, sc_ref, sh_ref, b_ref, o_ref = refs
        else:
            a_ref, b_ref, o_ref = refs
        t = pl.program_id(2)

        @pl.when(t == 0)
        def _():
            o_ref[...] = jnp.zeros_like(o_ref)

        av = a_ref[...]
        if mod is not None:
            av = av * (1.0 + sc_ref[0]) + sh_ref[0]
        o_ref[...] += _dot_tn(av.astype(BF16), b_ref[...].astype(BF16))

    in_specs = [pl.BlockSpec((tt, t1), lambda p, q, t: (t, p))]
    args = [a]
    if mod is not None:
        for v in mod:
            in_specs.append(pl.BlockSpec((1, 1, t1), lambda p, q, t: (t // spb, 0, p)))
            args.append(v)
    in_specs.append(pl.BlockSpec((tt, t2), lambda p, q, t: (t, q)))
    args.append(b)
    if by_chip:
        out_specs = pl.BlockSpec((None, t1, t2), lambda p, q, t: (q, p, 0))
        out_shape = jax.ShapeDtypeStruct((NCHIP, k1, t2), F32)
    else:
        out_specs = pl.BlockSpec((t1, t2), lambda p, q, t: (p, q))
        out_shape = jax.ShapeDtypeStruct((k1, k2), F32)
    return pl.pallas_call(
        body, name=name, grid=(k1 // t1, k2 // t2, T // tt), in_specs=in_specs, out_specs=out_specs,
        out_shape=out_shape, compiler_params=_cparams(("parallel", "parallel", "arbitrary")),
    )(*args)


def _mod_mm(name, x, sc, sh, ws, out_dtypes, rope=None, rope_secs=(), tm=256):
    T = x.shape[0]
    nw = len(ws)

    def fn(i, xv, *rest):
        if rope is not None:
            cv, s1v, s2v = rest[:3]
            rest = rest[3:]
        scv, shv = rest[:2]
        w_refs = rest[2:]
        h = (xv * (1.0 + scv) + shv).astype(BF16)
        res = []
        for k, w_ref in enumerate(w_refs):
            n = w_ref.shape[1]
            ch = WA if (k == 0 and rope is not None) else _pick(n, (512, 256, 128))
            parts = []
            for c in range(n // ch):
                z = _dot(h, w_ref[:, c * ch:(c + 1) * ch])
                if k == 0 and c in rope_secs:
                    z = _rope(z, _tile_lanes(cv, ch), _tile_lanes(s1v, ch), _tile_lanes(s2v, ch))
                parts.append(z.astype(out_dtypes[k]))
            res.append(parts[0] if len(parts) == 1 else jnp.concatenate(parts, axis=1))
        return tuple(res)

    tiles = (x,) + (tuple(rope) if rope is not None else ())
    outs = tuple((w.shape[1], dt) for w, dt in zip(ws, out_dtypes))
    return _rowwise(name, fn, T, tm, tiles=tiles, seqvecs=(sc, sh), consts=tuple(ws), outs=outs)


def _tri(tb, lower):
    r = lax.broadcasted_iota(jnp.int32, (tb, tb), 0)
    c = lax.broadcasted_iota(jnp.int32, (tb, tb), 1)
    return jnp.where((r >= c) if lower else (r <= c), 1.0, 0.0).astype(F32)


def _cumsum_seq(name, ins, consts, fn_in, fn_out, outs, reverse, n_acc=0, tb=512):
    T = ins[0].shape[0]
    tb = min(tb, S)
    nbs = S // tb
    nseq = T // S
    n_i, n_c, n_o = len(ins), len(consts), len(outs)

    def blk(b, j):
        return (b * nbs + (nbs - 1 - j if reverse else j), 0)

    def body(*refs):
        i_refs, c_refs = refs[:n_i], refs[n_i:n_i + n_c]
        o_refs = refs[n_i + n_c:n_i + n_c + n_o]
        acc_refs = refs[n_i + n_c + n_o:n_i + n_c + n_o + n_acc]
        carry = refs[-1]
        b, j = pl.program_id(0), pl.program_id(1)

        @pl.when(j == 0)
        def _():
            carry[...] = jnp.zeros_like(carry)

        iv = [r[...] for r in i_refs]
        xin = fn_in(*iv, *c_refs)
        cum = jnp.dot(_tri(tb, not reverse), xin, preferred_element_type=F32,
                      precision=lax.Precision.HIGHEST) + carry[0:1, :]
        carry[...] = carry[...] + jnp.sum(xin, axis=0, keepdims=True)
        res = fn_out(cum, *iv, *c_refs)
        for o, r in zip(o_refs, res):
            o[...] = r.astype(o.dtype)
        for a in acc_refs:
            @pl.when((b == 0) & (j == 0))
            def _():
                a[...] = jnp.zeros_like(a)

            a[...] += _rsum8(res[0])

            @pl.when((b == nseq - 1) & (j == nbs - 1))
            def _():
                a[...] = jnp.broadcast_to(jnp.sum(a[...], axis=0, keepdims=True), a.shape)

    in_specs = [pl.BlockSpec((tb, a.shape[1]), blk) for a in ins]
    in_specs += [pl.BlockSpec(c.shape, lambda b, j, nd=c.ndim: (0,) * nd) for c in consts]
    out_shape = [jax.ShapeDtypeStruct((T, w), dt) for w, dt in outs]
    out_shape += [jax.ShapeDtypeStruct((SUBLANES, outs[0][0]), F32)] * n_acc
    out_specs = [pl.BlockSpec((tb, w), blk) for w, _ in outs]
    out_specs += [pl.BlockSpec((SUBLANES, outs[0][0]), lambda b, j: (0, 0))] * n_acc
    return pl.pallas_call(
        body, name=name, grid=(nseq, nbs), in_specs=in_specs, out_specs=out_specs, out_shape=out_shape,
        scratch_shapes=[pltpu.VMEM((SUBLANES, FPAD), F32)],
        compiler_params=_cparams(("arbitrary", "arbitrary")),
    )(*ins, *consts)


def _log_sigmoid(x):
    return jnp.minimum(x, 0.0) - jnp.log(1.0 + jnp.exp(-jnp.abs(x)))


def _dil_bias(tq):
    max_win = max(w for w, _ in PATTERNS)
    nd = (max_win + tq - 1) // tq + 1
    qi = np.arange(tq)[:, None]
    kj = np.arange(tq)[None, :]
    tabs = []
    for dlt in range(nd):
        dist = dlt * tq + qi - kj
        mult = np.zeros((tq, tq), np.float64)
        for win, dil in PATTERNS:
            mult += (dist >= 0) & (dist % dil == 0) & (dist // dil <= win // dil)
        tabs.append(np.where(mult > 0, np.log(np.maximum(mult, 1.0)), NEG))
    return np.stack(tabs).astype(np.float32)


def _fold_tables(nha):
    hp_n = nha // 2
    pq = np.zeros((3, FPAD, hp_n * 2 * LANES), np.float32)
    pk = np.zeros((3, FPAD, hp_n * LANES), np.float32)
    oq = np.zeros((1, hp_n * 2 * LANES), np.float32)
    ok = np.zeros((1, hp_n * LANES), np.float32)
    sq = np.zeros((hp_n * LANES, FPAD), np.float32)
    sk = np.zeros((hp_n * LANES, FPAD), np.float32)
    for h in range(nha):
        hp, odd = divmod(h, 2)
        qb = hp * 2 * LANES + odd * (LANES + 8)
        kb = hp * LANES + odd * 8
        for i in range(3):
            pq[i, h, qb + i] = 1
            oq[0, qb + 3 + i] = 1
            ok[0, kb + i] = 1
            pk[i, h, kb + 3 + i] = 1
        sq[kb, h] = 1
        sk[kb + 3, h] = 1
    return pq, pk, oq, ok, sq, sk


def _stack_heads(x2, h0, extra=None):
    z = jnp.zeros_like(x2)
    a, b = jnp.where(h0, x2, z), jnp.where(h0, z, x2)
    if extra is not None:
        a = jnp.concatenate([a, extra[:, :LANES]], axis=1)
        b = jnp.concatenate([b, extra[:, LANES:]], axis=1)
    return jnp.concatenate([a, b], axis=0)


def _attn_fwd(name, qkv, secs, fox, eq=None, ek=None, bias=None):
    T = qkv.shape[0]
    nq = S // TQ
    nbl = T // S
    hp_n = WA // LANES
    sq, sk, sv = (s * hp_n for s in secs)
    scale = HD ** -0.5
    nd = None if fox else bias.shape[0]

    def body(*refs):
        if fox:
            q_ref, k_ref, v_ref, eq_ref, ek_ref, o_ref, lse_ref = refs
        else:
            q_ref, k_ref, v_ref, b_ref, o_ref, lse_ref = refs
        i = pl.program_id(2)
        lane = lax.broadcasted_iota(jnp.int32, (1, LANES), 1)
        h0 = lane < HD
        q2 = (q_ref[...].astype(F32) * scale).astype(BF16)
        qs = _stack_heads(q2, h0, eq_ref[...] if fox else None)

        def scores(t, diag):
            off = pl.multiple_of((i - t) * TQ, TQ)
            kk = k_ref[pl.ds(off, TQ), :]
            if fox:
                kk = jnp.concatenate([kk, ek_ref[pl.ds(off, TQ), :]], axis=1)
            s = jnp.concatenate([_dot_nt(qs[:TQ], kk), _dot_nt(qs[TQ:], kk)], axis=0)
            if not fox:
                s = (s.reshape(2, TQ, TQ) + b_ref[t]).reshape(2 * TQ, TQ)
            elif diag:
                rows = lax.broadcasted_iota(jnp.int32, (2, TQ, TQ), 1).reshape(2 * TQ, TQ)
                cols = lax.broadcasted_iota(jnp.int32, (2 * TQ, TQ), 1)
                s = jnp.where(cols <= rows, s, NEG)
            return s

        def update(t, s, m, l, acc):
            off = pl.multiple_of((i - t) * TQ, TQ)
            v2 = v_ref[pl.ds(off, TQ), :]
            m_new = jnp.maximum(m, jnp.max(s, axis=1, keepdims=True))
            p = jnp.exp(s - m_new)
            a = jnp.exp(m - m_new)
            l = a * l + jnp.sum(p, axis=1, keepdims=True)
            pb = p.astype(BF16)
            acc = a * acc + jnp.concatenate([_dot(pb[:TQ], v2), _dot(pb[TQ:], v2)], axis=0)
            return m_new, l, acc

        def step(t, carry):
            s, m, l, acc = carry
            return (scores(t + 1, False),) + update(t, s, m, l, acc)

        init = (jnp.full((2 * TQ, 1), NEG, F32), jnp.zeros((2 * TQ, 1), F32), jnp.zeros((2 * TQ, LANES), F32))
        n = i + 1 if fox else jnp.minimum(i + 1, nd)
        m, l, acc = update(0, scores(0, True), *init)
        m, l, acc = lax.fori_loop(1, n, lambda t, c: update(t, scores(t, False), *c), (m, l, acc))
        on = acc / l
        o_ref[...] = jnp.where(h0, on[:TQ], on[TQ:])
        lse = jnp.broadcast_to(m + jnp.log(l), (2 * TQ, LANES))
        lse_ref[...] = jnp.concatenate([lse[:TQ], lse[TQ:]], axis=1)

    in_specs = [
        pl.BlockSpec((TQ, LANES), lambda b, hp, i: (b * nq + i, sq + hp)),
        pl.BlockSpec((S, LANES), lambda b, hp, i: (b, sk + hp)),
        pl.BlockSpec((S, LANES), lambda b, hp, i: (b, sv + hp)),
    ]
    args = [qkv, qkv, qkv]
    if fox:
        in_specs += [pl.BlockSpec((TQ, 2 * LANES), lambda b, hp, i: (b * nq + i, hp)),
                     pl.BlockSpec((S, LANES), lambda b, hp, i: (b, hp))]
        args += [eq, ek]
    else:
        in_specs.append(pl.BlockSpec(bias.shape, lambda b, hp, i: (0, 0, 0)))
        args.append(bias)
    return pl.pallas_call(
        body, name=name, grid=(nbl, hp_n, nq), in_specs=in_specs,
        out_specs=[pl.BlockSpec((TQ, LANES), lambda b, hp, i: (b * nq + i, hp)),
                   pl.BlockSpec((TQ, 2 * LANES), lambda b, hp, i: (b * nq + i, hp))],
        out_shape=[jax.ShapeDtypeStruct((T, WA), F32), jax.ShapeDtypeStruct((T, 2 * WA), F32)],
        compiler_params=_cparams(("parallel", "parallel", "arbitrary")),
    )(*args)


def _attn_bwd(name, qkv, secs, o, do, do_sec, lse, fox, eq=None, ek=None, bias=None):
    T = qkv.shape[0]
    nq = S // TQ
    nbl = T // S
    hp_n = WA // LANES
    sq, sk, sv = (s * hp_n for s in secs)
    dsec = do_sec * hp_n
    scale = HD ** -0.5
    nd = None if fox else bias.shape[0]
    kc = 2 * LANES if fox else LANES

    def body(*refs):
        if fox:
            (q_ref, k_ref, v_ref, o_ref, do_ref, lse_ref, eq_ref, ek_ref,
             dq_ref, dk_ref, dv_ref, dqe_ref, dek_ref, dl_ref) = refs
        else:
            q_ref, k_ref, v_ref, o_ref, do_ref, lse_ref, b_ref, dq_ref, dk_ref, dv_ref, dl_ref = refs
        j = pl.program_id(2)
        lane = lax.broadcasted_iota(jnp.int32, (1, LANES), 1)
        h0 = lane < HD

        @pl.when(j == 0)
        def _():
            dq_ref[...] = jnp.zeros_like(dq_ref)
            if fox:
                dqe_ref[...] = jnp.zeros_like(dqe_ref)

            def dl_step(r, c):
                off = pl.multiple_of(r * TQ, TQ)
                d2 = do_ref[pl.ds(off, TQ), :] * o_ref[pl.ds(off, TQ), :]
                z2 = jnp.zeros_like(d2)
                dl0 = jnp.sum(jnp.where(h0, d2, z2), axis=1, keepdims=True)
                dl1 = jnp.sum(jnp.where(h0, z2, d2), axis=1, keepdims=True)
                dl_ref[pl.ds(off, TQ), :] = jnp.concatenate(
                    [jnp.broadcast_to(dl0, (TQ, LANES)), jnp.broadcast_to(dl1, (TQ, LANES))], axis=1)
                return c

            lax.fori_loop(0, nq, dl_step, 0)

        kk = k_ref[...]
        if fox:
            kk = jnp.concatenate([kk, ek_ref[...]], axis=1)
        v2 = v_ref[...]

        def wide(x2):
            st = jnp.concatenate([x2[:, :LANES], x2[:, LANES:]], axis=0)
            return st if TQ == LANES else jnp.concatenate([st] * (TQ // LANES), axis=1)

        def step(t, carry, diag):
            dkk, dv2 = carry
            off = pl.multiple_of((j + t) * TQ, TQ)
            q2 = (q_ref[pl.ds(off, TQ), :].astype(F32) * scale).astype(BF16)
            qs = _stack_heads(q2, h0, eq_ref[pl.ds(off, TQ), :] if fox else None)
            dos = _stack_heads(do_ref[pl.ds(off, TQ), :].astype(BF16), h0)
            s = jnp.concatenate([_dot_nt(qs[:TQ], kk), _dot_nt(qs[TQ:], kk)], axis=0)
            if not fox:
                s = (s.reshape(2, TQ, TQ) + b_ref[t]).reshape(2 * TQ, TQ)
            elif diag:
                rows = lax.broadcasted_iota(jnp.int32, (2, TQ, TQ), 1).reshape(2 * TQ, TQ)
                cols = lax.broadcasted_iota(jnp.int32, (2 * TQ, TQ), 1)
                s = jnp.where(cols <= rows, s, NEG)
            p = jnp.exp(s - wide(lse_ref[pl.ds(off, TQ), :]))
            dp = jnp.concatenate([_dot_nt(dos[:TQ], v2), _dot_nt(dos[TQ:], v2)], axis=0)
            dsb = (p * (dp - wide(dl_ref[pl.ds(off, TQ), :]))).astype(BF16)
            dv2 = dv2 + _dot_tn(p.astype(BF16), dos)
            dkk = dkk + _dot_tn(dsb, qs)
            dqq = jnp.concatenate([_dot(dsb[:TQ], kk), _dot(dsb[TQ:], kk)], axis=0)
            dq_ref[pl.ds(off, TQ), :] += jnp.where(h0, dqq[:TQ, :LANES], dqq[TQ:, :LANES])
            if fox:
                dqe_ref[pl.ds(off, TQ), :] += jnp.where(lane < SUBLANES, dqq[:TQ, LANES:], dqq[TQ:, LANES:])
            return dkk, dv2

        zero = (jnp.zeros((TQ, kc), F32), jnp.zeros((TQ, LANES), F32))
        if fox:
            dkk, dv2 = lax.fori_loop(1, nq - j, lambda t, c: step(t, c, False), step(0, zero, True))
        else:
            dkk, dv2 = lax.fori_loop(0, jnp.minimum(nq - j, nd), lambda t, c: step(t, c, False), zero)
        dk_ref[...] = dkk[:, :LANES]
        dv_ref[...] = dv2
        if fox:
            dek_ref[...] = dkk[:, LANES:]

        @pl.when(j == nq - 1)
        def _():
            dq_ref[...] = dq_ref[...] * scale

    seq = lambda c, w=LANES: pl.BlockSpec((S, w), lambda b, hp, j: (b, c + hp))
    blk = lambda c: pl.BlockSpec((TQ, LANES), lambda b, hp, j: (b * nq + j, c + hp))
    in_specs = [seq(sq), blk(sk), blk(sv), seq(0), seq(dsec), seq(0, 2 * LANES)]
    args = [qkv, qkv, qkv, o, do, lse]
    if fox:
        in_specs += [seq(0, 2 * LANES), blk(0)]
        args += [eq, ek]
    else:
        in_specs.append(pl.BlockSpec(bias.shape, lambda b, hp, j: (0, 0, 0)))
        args.append(bias)
    out_specs = [seq(0), blk(0), blk(0)]
    out_shape = [jax.ShapeDtypeStruct((T, WA), F32)] * 3
    if fox:
        out_specs += [seq(0), blk(0)]
        out_shape += [jax.ShapeDtypeStruct((T, WA), F32)] * 2
    return pl.pallas_call(
        body, name=name, grid=(nbl, hp_n, nq), in_specs=in_specs, out_specs=out_specs, out_shape=out_shape,
        scratch_shapes=[pltpu.VMEM((S, 2 * LANES), F32)],
        compiler_params=_cparams(("parallel", "parallel", "arbitrary")),
    )(*args)


def _exchange(name, ins, out_shapes, remote, local):
    n_in, n_out = len(ins), len(out_shapes)
    nr, nl = len(remote), len(local)

    def body(*refs):
        in_refs = refs[:n_in]
        out_refs = refs[n_in:n_in + n_out]
        send_sems, recv_sems, loc_sems = refs[n_in + n_out:]
        me = (lax.axis_index("x"), lax.axis_index("y"), lax.axis_index("c"))

        def peer_of(flip):
            return tuple(1 - v if f else v for v, f in zip(me, flip))

        def at(ref, idx):
            return ref if idx is None else ref.at[idx]

        def rcopy(k, who):
            flip, a, sfn, b, dfn = remote[k]
            return pltpu.make_async_remote_copy(
                src_ref=at(in_refs[a], sfn(*who)), dst_ref=at(out_refs[b], dfn(*who)),
                send_sem=send_sems.at[k], recv_sem=recv_sems.at[k],
                device_id=peer_of(flip), device_id_type=MESH)

        locs = [pltpu.make_async_copy(at(in_refs[a], sfn(*me)), at(out_refs[b], dfn(*me)), loc_sems.at[k])
                for k, (a, sfn, b, dfn) in enumerate(local)]
        for cp in locs:
            cp.start()
        sends = [rcopy(k, me) for k in range(nr)]
        for cp in sends:
            cp.start()
        for k in range(nr):
            rcopy(k, peer_of(remote[k][0])).wait_recv()
        for cp in sends:
            cp.wait_send()
        for cp in locs:
            cp.wait()

    any_spec = pl.BlockSpec(memory_space=pl.ANY)
    return pl.pallas_call(
        body, name=name, in_specs=[any_spec] * n_in, out_specs=[any_spec] * n_out, out_shape=list(out_shapes),
        scratch_shapes=[pltpu.SemaphoreType.DMA((max(nr, 1),)), pltpu.SemaphoreType.DMA((max(nr, 1),)),
                        pltpu.SemaphoreType.DMA((max(nl, 1),))],
    )(*ins)


_FLIPS7 = [(0, 0, 1), (0, 1, 0), (0, 1, 1), (1, 0, 0), (1, 0, 1), (1, 1, 0), (1, 1, 1)]
_CHIP_FLIPS = [(1, 0, 0), (0, 1, 0), (1, 1, 0)]


def _dev_index(x, y, c):
    return 4 * x + 2 * y + c


def _chip_index(x, y, c):
    return 2 * x + y


def _all_gather8(name, v):
    remote = [(f, 0, lambda x, y, c: None, 0, _dev_index) for f in _FLIPS7]
    local = [(0, lambda x, y, c: None, 0, _dev_index)]
    return _exchange(name, [v], [jax.ShapeDtypeStruct((NDEV,) + v.shape, v.dtype)], remote, local)[0]


def _gather_halves(name, vs):
    n_v = len(vs)

    def body(*refs):
        in_refs, out_refs = refs[:n_v], refs[n_v:2 * n_v]
        send_sems, recv_sems = refs[2 * n_v:]
        x, y, c = lax.axis_index("x"), lax.axis_index("y"), lax.axis_index("c")
        sibling = (x, y, 1 - c)
        chips = [(1 - x, y), (x, 1 - y), (1 - x, 1 - y)]

        def copy(k, n, src, blk, half, to):
            return pltpu.make_async_remote_copy(
                src_ref=src, dst_ref=out_refs[n].at[blk, half], send_sem=send_sems.at[k], recv_sem=recv_sems.at[k],
                device_id=to, device_id_type=MESH)

        first = [copy(6 * n + j, n, in_refs[n].at[c], j, c, (*chip, c))
                 for n in range(n_v) for j, chip in enumerate(chips)]
        for cp in first:
            cp.start()
        passed = []
        for n in range(n_v):
            for j, chip in enumerate(chips):
                copy(6 * n + j, n, in_refs[n].at[c], j, c, (*chip, c)).wait_recv()
                fw = copy(6 * n + 3 + j, n, out_refs[n].at[j, c], j, c, sibling)
                fw.start()
                passed.append(fw)
        for n in range(n_v):
            for j in range(len(chips)):
                copy(6 * n + 3 + j, n, out_refs[n].at[j, 1 - c], j, 1 - c, sibling).wait_recv()
        for cp in first + passed:
            cp.wait_send()

    any_spec = pl.BlockSpec(memory_space=pl.ANY)
    return pl.pallas_call(
        body, name=name, in_specs=[any_spec] * n_v, out_specs=[any_spec] * n_v,
        out_shape=[jax.ShapeDtypeStruct((NCHIP - 1,) + v.shape, v.dtype) for v in vs],
        scratch_shapes=[pltpu.SemaphoreType.DMA((6 * n_v,)), pltpu.SemaphoreType.DMA((6 * n_v,))],
    )(*vs)


def _to_sibling(name, v):
    remote = [((0, 0, 1), 0, lambda x, y, c: None, 0, lambda x, y, c: None)]
    return _exchange(name, [v], [jax.ShapeDtypeStruct(v.shape, v.dtype)], remote, [])[0]


def _scatter_chips(name, v):
    remote = []
    for j, f in enumerate(_CHIP_FLIPS):
        src = lambda x, y, c, f=f: _chip_index(1 - x if f[0] else x, 1 - y if f[1] else y, c)
        remote.append((f, 0, src, 0, lambda x, y, c, j=j: j))
    return _exchange(name, [v], [jax.ShapeDtypeStruct((NCHIP - 1,) + v.shape[1:], v.dtype)], remote, [])[0]


def _by_chip(own, others, chip):
    stacked = jnp.concatenate([own[None], others], axis=0)
    blocks = []
    for k in range(NCHIP):
        d = k ^ chip
        place = jnp.where(d == 0, 0, jnp.where(d == 2, 1, jnp.where(d == 1, 2, 3)))
        blocks.append(lax.dynamic_index_in_dim(stacked, place, axis=0, keepdims=False))
    return jnp.stack(blocks)


def _sum_leading(name, v, tm=None):
    n, r, w = v.shape
    tm = _pick(r, (256, 128, 64, 32, 16, 8)) if tm is None else tm

    def body(v_ref, o_ref):
        acc = v_ref[0].astype(F32)
        for k in range(1, n):
            acc = acc + v_ref[k].astype(F32)
        o_ref[...] = acc

    return pl.pallas_call(
        body, name=name, grid=(r // tm,), in_specs=[pl.BlockSpec((n, tm, w), lambda i: (0, i, 0))],
        out_specs=pl.BlockSpec((tm, w), lambda i: (i, 0)), out_shape=jax.ShapeDtypeStruct((r, w), F32),
        compiler_params=_cparams(("parallel",)),
    )(v)


def _add2(name, a, b, tm=None, out_dtype=F32):
    r, w = a.shape
    tm = _pick(r, (256, 128, 64, 32, 16, 8)) if tm is None else tm

    def body(a_ref, b_ref, o_ref):
        o_ref[...] = (a_ref[...] + b_ref[...]).astype(out_dtype)

    spec = pl.BlockSpec((tm, w), lambda i: (i, 0))
    return pl.pallas_call(
        body, name=name, grid=(r // tm,), in_specs=[spec, spec], out_specs=spec,
        out_shape=jax.ShapeDtypeStruct((r, w), out_dtype), compiler_params=_cparams(("parallel",)),
    )(a, b)


def _ada_fwd(call_all, w_shard):
    def body(c_ref, w_ref, o_ref):
        cv = c_ref[...]
        o_ref[...] = jnp.dot(cv * _sigmoid(cv), w_ref[...], preferred_element_type=F32,
                             precision=lax.Precision.HIGHEST)

    n = w_shard.shape[1]
    return pl.pallas_call(
        body, name="ada_fwd", out_shape=jax.ShapeDtypeStruct((call_all.shape[0], n), F32),
        compiler_params=pltpu.CompilerParams(vmem_limit_bytes=VMEM_LIMIT),
    )(call_all, w_shard)


def _ada_bwd(call_all, dada):
    def body(c_ref, d_ref, o_ref):
        cv = c_ref[...]
        o_ref[...] = lax.dot_general(cv * _sigmoid(cv), d_ref[...], (((0,), (0,)), ((), ())),
                                     preferred_element_type=F32, precision=lax.Precision.HIGHEST)

    return pl.pallas_call(
        body, name="ada_bwd", out_shape=jax.ShapeDtypeStruct((call_all.shape[1], dada.shape[1]), F32),
        compiler_params=pltpu.CompilerParams(vmem_limit_bytes=VMEM_LIMIT),
    )(call_all, dada)


def _adamw(name, w, g, m, v):
    r, wd = w.shape
    tm = _pick(r, (256, 128, 64, 32, 16, 8))
    bc1 = 1.0 - ADAM_B1 ** ADAM_STEP
    bc2 = 1.0 - ADAM_B2 ** ADAM_STEP

    def body(w_ref, g_ref, m_ref, v_ref, d_ref, mo_ref, vo_ref):
        gv = g_ref[...]
        mn = ADAM_B1 * m_ref[...] + (1.0 - ADAM_B1) * gv
        vn = ADAM_B2 * v_ref[...] + (1.0 - ADAM_B2) * (gv * gv)
        d_ref[...] = -ADAM_LR * ((mn / bc1) / (jnp.sqrt(vn / bc2) + ADAM_EPS) + ADAM_WD * w_ref[...])
        mo_ref[...] = mn
        vo_ref[...] = vn

    spec = pl.BlockSpec((tm, wd), lambda i: (i, 0))
    return pl.pallas_call(
        body, name=name, grid=(r // tm,), in_specs=[spec] * 4, out_specs=[spec] * 3,
        out_shape=[jax.ShapeDtypeStruct((r, wd), F32)] * 3, compiler_params=_cparams(("parallel",)),
    )(w, g, m, v)


def _rope_tables(positions):
    half = ROPE_DIMS // 2
    freqs = ROPE_THETA ** (-jnp.arange(0, ROPE_DIMS, 2, dtype=F32) / ROPE_DIMS)
    ang = positions.astype(F32).reshape(-1, 1) * freqs
    cos, sin = jnp.cos(ang), jnp.sin(ang)
    T = ang.shape[0]
    one = jnp.ones((T, HD - ROPE_DIMS), F32)
    zero = jnp.zeros((T, HD - ROPE_DIMS), F32)
    zh = jnp.zeros((T, half), F32)
    c64 = jnp.concatenate([cos, cos, one], axis=1)
    s1 = jnp.concatenate([zh, sin, zero], axis=1)
    s2 = jnp.concatenate([-sin, zh, zero], axis=1)
    rep = lambda t: jnp.concatenate([t] * (LANES // HD), axis=1)
    return rep(c64), rep(s1), rep(s2)


def _local_step(x, loss_target, positions, ada, w_qkv, w_f, w_out, w_up, conv_w8, w_down,
                b_fgate, gn, ln1_g, ln1_b, conv_b, ln2_g, ln2_b):
    T = x.shape[0]
    nbl = T // S
    nha = WA // HD
    sv = lambda k: ada[:, k:k + 1, :]
    sh_a, sc_a, g_a, sh_f, sc_f, g_f = (sv(k) for k in range(6))
    rope = _rope_tables(positions)
    neg_rope = (rope[0], -rope[1], -rope[2])
    gseg = jnp.asarray(np.kron(np.eye(min(256, 2 * WA) // HD), np.ones((HD, HD))), BF16)
    bias = jnp.asarray(_dil_bias(TQ))
    bf_pad = jnp.zeros((1, FPAD), F32).at[:, :nha].set(b_fgate)

    qkv, fa = _mod_mm("qkv_proj", x, sc_a, sh_a, (w_qkv, w_f), (BF16, F32), rope=rope, rope_secs=(3, 4))
    pq, pk, oq, ok, sq, sk = _fold_tables(nha)

    def fold_out(cum, f, b_ref, pq_ref, pk_ref, oq_ref, ok_ref):
        hi = cum.astype(BF16)
        r1 = cum - hi.astype(F32)
        mid = r1.astype(BF16)
        lo = (r1 - mid.astype(F32)).astype(BF16)
        eqv = _dot(hi, pq_ref[0]) + _dot(mid, pq_ref[1]) + _dot(lo, pq_ref[2]) + oq_ref[...]
        ekv = ok_ref[...] - (_dot(hi, pk_ref[0]) + _dot(mid, pk_ref[1]) + _dot(lo, pk_ref[2]))
        return eqv, ekv

    eq, ek = _cumsum_seq(
        "fgate_fwd", [fa], [bf_pad, jnp.asarray(pq, BF16), jnp.asarray(pk, BF16), jnp.asarray(oq), jnp.asarray(ok)],
        lambda f, b_ref, *_: _log_sigmoid(f + b_ref[...]), fold_out, ((2 * WA, BF16), (WA, BF16)), reverse=False)
    oa, lse_a = _attn_fwd("fox_fwd", qkv, (0, 1, 2), True, eq=eq, ek=ek)
    ob, lse_b = _attn_fwd("dil_fwd", qkv, (3, 4, 5), False, bias=bias)

    def mix_fn(i, oav, obv, xv, gav, gn_ref, g_ref, wo_ref, l1g_ref, l1b_ref):
        o = jnp.concatenate([oav, obv], axis=1)
        rs = lax.rsqrt(_head_mean(o * o, g_ref) + RMS_EPS)
        merged = (o * rs * gn_ref[...]).astype(BF16)
        mix = _dot(merged, wo_ref[...])
        x1, _, _ = _ln_fwd(ALPHA * xv + gav * mix, l1g_ref[...], l1b_ref[...])
        return merged, mix, x1

    merged, mix, x1 = _rowwise("mix_out", mix_fn, T, 256, tiles=(oa, ob, x), seqvecs=(g_a,),
                               consts=(gn, gseg, w_out, ln1_g, ln1_b),
                               outs=((2 * WA, BF16), (D, F32), (D, F32)))
    u = _mod_mm("ffn_up", x1, sc_f, sh_f, (w_up,), (F32,))[0]

    def conv_y(i, uv, prev, cw_ref, cb_ref, tm):
        first = (i * tm) % S == 0
        s1, s2 = _conv_taps(uv, prev, first)
        y = cb_ref[...] + cw_ref[0:1, :] * s2 + cw_ref[1:2, :] * s1 + cw_ref[2:3, :] * uv
        return y, s1, s2

    tmc = 128

    def gate_fn(i, uv, prev, cw_ref, cb_ref):
        y, _, _ = conv_y(i, uv, prev, cw_ref, cb_ref, tmc)
        a, g = y[:, :DFF], y[:, DFF:]
        return g * _sigmoid(g) * a

    act = _rowwise("conv_gate", gate_fn, T, tmc, tiles=(u,), halos=((u, -1),), consts=(conv_w8, conv_b),
                   outs=((DFF, BF16),))[0]

    def down_fn(i, actv, x1v, tgt, gfv, wd_ref, g2_ref, b2_ref):
        ffn = _dot(actv, wd_ref[...])
        y, n2, rstd = _ln_fwd(ALPHA * x1v + gfv * ffn, g2_ref[...], b2_ref[...])
        err = y - tgt
        dy = err * (1.0 / D)
        dr2 = _ln_bwd(dy, n2, rstd, g2_ref[...])
        return (dr2, gfv * dr2, _rsum8(err * err), _rsum8(dy * n2), _rsum8(dy), _rsum8(dr2 * ffn))

    dr2, dffn, loss_acc, d_ln2g, d_ln2b, d_gf = _rowwise(
        "ffn_down_loss", down_fn, T, 256, tiles=(act, x1, loss_target), seqvecs=(g_f,),
        consts=(w_down, ln2_g, ln2_b), outs=((D, F32), (D, BF16)), accs=(D, D, D), seqaccs=(D,))

    dact = _mm_nt("dact", dffn, w_down)

    def gate_bwd_fn(i, uv, dav, prev, cw_ref, cb_ref):
        y, s1, s2 = conv_y(i, uv, prev, cw_ref, cb_ref, tmc)
        a, g = y[:, :DFF], y[:, DFF:]
        sg = _sigmoid(g)
        dyc = jnp.concatenate([dav * (g * sg), dav * a * (sg * (1.0 + g * (1.0 - sg)))], axis=1)
        return dyc, _rsum8(dyc), _rsum8(dyc * s2), _rsum8(dyc * s1), _rsum8(dyc * uv)

    dyc, d_cb, d_cw0, d_cw1, d_cw2 = _rowwise(
        "gate_bwd", gate_bwd_fn, T, tmc, tiles=(u, dact), halos=((u, -1),), consts=(conv_w8, conv_b),
        outs=((2 * DFF, F32),), accs=(2 * DFF,) * 4)

    def conv_bwd_fn(i, dv, nxt, cw_ref):
        last = ((i + 1) * tmc) % S == 0
        u1, u2 = _conv_taps_up(dv, nxt, last)
        return cw_ref[2:3, :] * dv + cw_ref[1:2, :] * u1 + cw_ref[0:1, :] * u2

    du = _rowwise("conv_bwd", conv_bwd_fn, T, tmc, tiles=(dyc,), halos=((dyc, 1),), consts=(conv_w8,),
                  outs=((2 * DFF, BF16),))[0]
    dh2 = _mm_nt("dh2", du, w_up)
    g_w_down = _mm_tn("dw_down", act, dffn)
    g_w_up = _mm_tn("dw_up", x1, du, mod=(sc_f, sh_f), by_chip=True)

    def ln1_bwd_fn(i, dr2v, dh2v, xv, mixv, x1v, scfv, gav, l1g_ref):
        dx1 = ALPHA * dr2v + dh2v * (1.0 + scfv)
        _, n1, rstd = _ln_fwd(ALPHA * xv + gav * mixv, l1g_ref[...], 0.0)
        dr1 = _ln_bwd(dx1, n1, rstd, l1g_ref[...])
        return (dr1, gav * dr1, _rsum8(dx1 * n1), _rsum8(dx1),
                _rsum8(dh2v * x1v), _rsum8(dh2v), _rsum8(dr1 * mixv))

    dr1, dmix, d_ln1g, d_ln1b, d_scf, d_shf, d_ga = _rowwise(
        "ln1_bwd", ln1_bwd_fn, T, 256, tiles=(dr2, dh2, x, mix, x1), seqvecs=(sc_f, g_a), consts=(ln1_g,),
        outs=((D, F32), (D, BF16)), accs=(D, D), seqaccs=(D, D, D))

    dmerged = _mm_nt("dmerged", dmix, w_out)
    g_w_out = _mm_tn("dw_out", merged, dmix)

    def hn_bwd_fn(i, dmv, oav, obv, gn_ref, g_ref):
        o = jnp.concatenate([oav, obv], axis=1)
        rs = lax.rsqrt(_head_mean(o * o, g_ref) + RMS_EPS)
        nrm = o * rs
        dn = dmv * gn_ref[...]
        do = rs * (dn - nrm * _head_mean(dn * nrm, g_ref))
        return do, _rsum8(dmv * nrm)

    do, d_gn = _rowwise("headnorm_bwd", hn_bwd_fn, T, 256, tiles=(dmerged, oa, ob), consts=(gn, gseg),
                        outs=((2 * WA, F32),), accs=(2 * WA,))
    dqa, dka, dva, dqe, dek = _attn_bwd("fox_bwd", qkv, (0, 1, 2), oa, do, 0, lse_a, True, eq=eq, ek=ek)
    dqb, dkb, dvb = _attn_bwd("dil_bwd", qkv, (3, 4, 5), ob, do, 1, lse_b, False, bias=bias)
    hdot = lambda a, m_ref: jnp.dot(a, m_ref[...], preferred_element_type=F32, precision=lax.Precision.HIGHEST)
    dfa, d_bf = _cumsum_seq(
        "fgate_bwd", [dqe, dek, fa], [bf_pad, jnp.asarray(sq), jnp.asarray(sk)],
        lambda dq_, dk_, f, b_ref, sq_ref, sk_ref: hdot(dq_, sq_ref) - hdot(dk_, sk_ref),
        lambda cum, dq_, dk_, f, b_ref, sq_ref, sk_ref: (cum * _sigmoid(-(f + b_ref[...])),),
        ((FPAD, F32),), reverse=True, n_acc=1)

    def dz_fn(i, a0, a1, a2, b0, b1, b2, fv, cv, s1v, s2v):
        ct, s1t, s2t = (_tile_lanes(t, WA) for t in (cv, s1v, s2v))
        return jnp.concatenate([a0, a1, a2, _rope(b0, ct, s1t, s2t), _rope(b1, ct, s1t, s2t), b2, fv], axis=1)

    dz = _rowwise("dz_pack", dz_fn, T, 256, tiles=(dqa, dka, dva, dqb, dkb, dvb, dfa) + neg_rope,
                  outs=((6 * WA + FPAD, BF16),))[0]
    w_cat = jnp.concatenate([w_qkv, w_f], axis=1)
    dh1 = _mm_nt("dh1", dz, w_cat)
    g_w_cat = _mm_tn("dw_in", x, dz, mod=(sc_a, sh_a))

    def dx_fn(i, dr1v, dh1v, xv, scav):
        return ALPHA * dr1v + dh1v * (1.0 + scav), _rsum8(dh1v * xv), _rsum8(dh1v)

    grad_x, d_sca, d_sha = _rowwise("dx_out", dx_fn, T, 256, tiles=(dr1, dh1, x), seqvecs=(sc_a,),
                                    outs=((D, F32),), seqaccs=(D, D))

    row0 = lambda a: a[..., 0, :]
    d_ada = jnp.stack([row0(d_sha), row0(d_sca), row0(d_ga), row0(d_shf), row0(d_scf), row0(d_gf)], axis=1)
    d_cw = jnp.stack([row0(d_cw0), row0(d_cw1), row0(d_cw2)], axis=0)
    loss_part = (0.5 / D) * jnp.sum(loss_acc[0])
    small = dict(b_fgate=row0(d_bf)[:nha], gn=row0(d_gn), ln1_g=row0(d_ln1g), ln1_b=row0(d_ln1b),
                 conv_b=row0(d_cb), ln2_g=row0(d_ln2g), ln2_b=row0(d_ln2b))
    big = dict(w_cat=g_w_cat, w_out=g_w_out, w_up=g_w_up, conv_w=d_cw, w_down=g_w_down)
    return loss_part, grad_x, d_ada, small, big


def _rows_of(n, w=None):
    return -(-n // (D if w is None else w))


def _as_rows(v):
    w = D
    k = v.shape[0]
    flat = v.reshape(k, -1)
    rows = _rows_of(_rows_of(flat.shape[1], w), SUBLANES) * SUBLANES
    flat = jnp.pad(flat, ((0, 0), (0, rows * w - flat.shape[1])))
    return flat.reshape(k, rows, w)


def kernel(x, c, positions, w_ada, b_ada, w_in, b_fgate, gn_a, gn_b, w_out, ln1_g, ln1_b, w_up, conv_w, conv_b, w_down, ln2_g, ln2_b, loss_target, m_w_ada, m_b_ada, m_w_in, m_b_fgate, m_gn_a, m_gn_b, m_w_out, m_ln1_g, m_ln1_b, m_w_up, m_conv_w, m_conv_b, m_w_down, m_ln2_g, m_ln2_b, v_w_ada, v_b_ada, v_w_in, v_b_fgate, v_gn_a, v_gn_b, v_w_out, v_ln1_g, v_ln1_b, v_w_up, v_conv_w, v_conv_b, v_w_down, v_ln2_g, v_ln2_b):
    mx, my, mc = lax.axis_index("x"), lax.axis_index("y"), lax.axis_index("c")
    dev = _dev_index(mx, my, mc)
    chip = _chip_index(mx, my, mc)
    nbl = x.shape[0]
    T = nbl * S
    nha = WA // HD
    d_in = w_in.shape[2] * NCHIP
    n_ada = w_ada.shape[2]

    c_pad = jnp.zeros((SUBLANES, D), F32).at[:nbl].set(c)
    c_all = _all_gather8("gather_c", c_pad)[:, :nbl].reshape(NDEV * nbl, D)
    ada_part = _ada_fwd(c_all, w_ada[0])
    n_cw = conv_w.shape[2]
    cw_rows = jnp.pad(conv_w[0], ((0, SUBLANES - conv_w.shape[1]), (0, n_ada - n_cw)))
    ada_blocks = _all_gather8("gather_ada", jnp.concatenate([ada_part, cw_rows], axis=0))
    n_c = NDEV * nbl
    ada_all = jnp.concatenate([ada_blocks[2 * k, :n_c] for k in range(NCHIP)], axis=1) + b_ada
    conv_w8 = jnp.concatenate([ada_blocks[2 * k, n_c:, :n_cw] for k in range(NCHIP)], axis=1)
    ada = lax.dynamic_slice_in_dim(ada_all, dev * nbl, nbl, axis=0).reshape(nbl, 6, D)

    w_in_sh = jnp.pad(w_in[0].astype(BF16), ((0, 0), (0, _rows_of(w_in.shape[2], LANES) * LANES - w_in.shape[2])))
    shards = [w_in_sh, w_out[0].astype(BF16), w_up[0].astype(BF16), w_down[0].astype(BF16)]
    halves = [t.reshape(2, t.shape[0] // 2, t.shape[1]) for t in shards]
    gathered_w = _gather_halves("gather_w", halves)
    g_in, g_out, g_up, g_down = (_by_chip(h, g, chip).reshape((NCHIP,) + t.shape)
                                 for g, h, t in zip(gathered_w, halves, shards))
    w_in_full = jnp.concatenate([g_in[k][:, :w_in.shape[2]] for k in range(NCHIP)], axis=1)
    w_qkv = jnp.concatenate([w_in_full[:, :3 * WA], w_in_full[:, 3 * WA + nha:]], axis=1)
    w_f = jnp.pad(w_in_full[:, 3 * WA:3 * WA + nha], ((0, 0), (0, FPAD - nha)))
    w_out_full = g_out.reshape(NCHIP * w_out.shape[1], D)
    w_up_full = jnp.concatenate([g_up[k] for k in range(NCHIP)], axis=1)
    w_down_full = g_down.reshape(NCHIP * w_down.shape[1], D)

    gn = jnp.concatenate([gn_a, gn_b], axis=1)
    loss_part, grad_x, d_ada, small, big = _local_step(
        x.reshape(T, D), loss_target.reshape(T, D), positions, ada, w_qkv, w_f, w_out_full, w_up_full, conv_w8,
        w_down_full, b_fgate, gn, ln1_g, ln1_b, conv_b, ln2_g, ln2_b)

    def row_pad(v, rows):
        flat = v.reshape(-1)
        return jnp.pad(flat, (0, rows * D - flat.shape[0]))

    n_cb = _rows_of(2 * DFF)
    small_flat = jnp.concatenate([
        row_pad(small["b_fgate"], 1), row_pad(small["gn"], 1), row_pad(small["ln1_g"], 1),
        row_pad(small["ln1_b"], 1), row_pad(small["ln2_g"], 1), row_pad(small["ln2_b"], 1),
        row_pad(jnp.full((1,), loss_part, F32), 1), row_pad(small["conv_b"], n_cb)])
    n_small = _rows_of(small_flat.shape[0], SUBLANES * D) * SUBLANES
    small_rows = jnp.pad(small_flat, (0, n_small * D - small_flat.shape[0])).reshape(n_small, D)
    ada_rows = jnp.pad(d_ada.reshape(nbl, 6, D), ((0, 0), (0, SUBLANES - 6), (0, 0))).reshape(nbl * SUBLANES, D)
    gathered = _all_gather8("gather_small", jnp.concatenate([small_rows, ada_rows], axis=0))
    red = _sum_leading("sum_small", gathered, tm=SUBLANES)
    g_b_fgate = red[0:1, :nha]
    g_gn = red[1:2, :2 * WA]
    g_ln1_g, g_ln1_b, g_ln2_g, g_ln2_b = red[2:3], red[3:4], red[4:5], red[5:6]
    loss = red[6, 0]
    g_conv_b = red[7:7 + n_cb].reshape(1, -1)[:, :2 * DFF]
    g_b_ada = _add2("sum_b_ada", red[n_small:n_small + SUBLANES], red[n_small + SUBLANES:n_small + 2 * SUBLANES],
                    tm=SUBLANES)[:6].reshape(1, 6 * D)
    dada_all = gathered[:, n_small:].reshape(NDEV, nbl, SUBLANES, D)[:, :, :6].reshape(NDEV * nbl, 6 * D)
    g_w_ada = _ada_bwd(c_all, lax.dynamic_slice_in_dim(dada_all, chip * n_ada, n_ada, axis=1))

    g_cat = big["w_cat"]
    g_w_in_full = jnp.concatenate([g_cat[:, :3 * WA], g_cat[:, 6 * WA:6 * WA + nha], g_cat[:, 3 * WA:6 * WA]], axis=1)
    sh_in = g_w_in_full.reshape(D, NCHIP, -1).transpose(1, 0, 2)
    sh_out = big["w_out"].reshape(NCHIP, -1, D)
    sh_up = big["w_up"]
    sh_cw = big["conv_w"].reshape(conv_w.shape[1], NCHIP, -1).transpose(1, 0, 2)
    sh_down = big["w_down"].reshape(NCHIP, -1, D)
    parts = [_as_rows(t) for t in (sh_in, sh_out, sh_up, sh_cw, sh_down)]
    part_rows = [p.shape[1] for p in parts]
    n_rows = _rows_of(sum(part_rows), 2 * LANES) * 2 * LANES
    half = n_rows // 2
    if n_rows > sum(part_rows):
        parts.append(jnp.zeros((NCHIP, n_rows - sum(part_rows), D), F32))
    halves = jnp.concatenate(parts, axis=1).reshape(NCHIP, 2, half, D)
    mine = lax.dynamic_index_in_dim(halves, mc, axis=1, keepdims=False).reshape(NCHIP * half, D)
    theirs = lax.dynamic_index_in_dim(halves, 1 - mc, axis=1, keepdims=False).reshape(NCHIP * half, D)
    from_sib = _to_sibling("pair_swap", theirs)
    pair_sum = _add2("pair_sum", mine, from_sib, out_dtype=BF16).reshape(NCHIP, half, D)
    by_chip = _scatter_chips("scatter_grads", pair_sum)
    own = lax.dynamic_index_in_dim(pair_sum, chip, axis=0, keepdims=False)
    my_half = _sum_leading("chip_sum", _by_chip(own, by_chip, chip))
    sib_half = _to_sibling("pair_share", my_half)
    pair = jnp.stack([my_half, sib_half])
    shard = jnp.concatenate([lax.dynamic_index_in_dim(pair, mc, axis=0, keepdims=False),
                             lax.dynamic_index_in_dim(pair, 1 - mc, axis=0, keepdims=False)], axis=0)

    def unpack(k, shape):
        start = sum(part_rows[:k])
        n = int(np.prod(shape))
        return shard[start:start + part_rows[k]].reshape(-1)[:n].reshape(shape)

    g_w_in = unpack(0, w_in.shape[1:])
    g_w_out = unpack(1, w_out.shape[1:])
    g_w_up = unpack(2, w_up.shape[1:])
    g_conv_w = unpack(3, conv_w.shape[1:])
    g_w_down = unpack(4, w_down.shape[1:])

    grads = dict(w_ada=g_w_ada, b_ada=g_b_ada, w_in=g_w_in, b_fgate=g_b_fgate, gn_a=g_gn[:, :WA], gn_b=g_gn[:, WA:],
                 w_out=g_w_out, ln1_g=g_ln1_g, ln1_b=g_ln1_b, w_up=g_w_up, conv_w=g_conv_w, conv_b=g_conv_b,
                 w_down=g_w_down, ln2_g=g_ln2_g, ln2_b=g_ln2_b)
    weights = dict(w_ada=w_ada, b_ada=b_ada, w_in=w_in, b_fgate=b_fgate, gn_a=gn_a, gn_b=gn_b, w_out=w_out,
                   ln1_g=ln1_g, ln1_b=ln1_b, w_up=w_up, conv_w=conv_w, conv_b=conv_b, w_down=w_down,
                   ln2_g=ln2_g, ln2_b=ln2_b)
    ms = dict(w_ada=m_w_ada, b_ada=m_b_ada, w_in=m_w_in, b_fgate=m_b_fgate, gn_a=m_gn_a, gn_b=m_gn_b,
              w_out=m_w_out, ln1_g=m_ln1_g, ln1_b=m_ln1_b, w_up=m_w_up, conv_w=m_conv_w, conv_b=m_conv_b,
              w_down=m_w_down, ln2_g=m_ln2_g, ln2_b=m_ln2_b)
    vs = dict(w_ada=v_w_ada, b_ada=v_b_ada, w_in=v_w_in, b_fgate=v_b_fgate, gn_a=v_gn_a, gn_b=v_gn_b,
              w_out=v_w_out, ln1_g=v_ln1_g, ln1_b=v_ln1_b, w_up=v_w_up, conv_w=v_conv_w, conv_b=v_conv_b,
              w_down=v_w_down, ln2_g=v_ln2_g, ln2_b=v_ln2_b)
    names = list(weights)
    big_names = ("w_ada", "w_in", "w_out", "w_up", "w_down")
    delta, new_m, new_v = {}, {}, {}
    for n in big_names:
        shp = weights[n].shape
        d, m2, v2 = _adamw("adamw_" + n, weights[n][0], grads[n].reshape(shp[1:]), ms[n][0], vs[n][0])
        delta[n], new_m[n], new_v[n] = d.reshape(shp), m2.reshape(shp), v2.reshape(shp)
    small_names = [n for n in names if n not in big_names]

    def pack_small(src):
        flats = []
        for n in small_names:
            flat = src[n].reshape(-1)
            flats.append(jnp.pad(flat, (0, _rows_of(flat.shape[0]) * D - flat.shape[0])))
        allf = jnp.concatenate(flats)
        rows = _rows_of(allf.shape[0], SUBLANES * D) * SUBLANES
        return jnp.pad(allf, (0, rows * D - allf.shape[0])).reshape(rows, D)

    sd, sm, sv_ = _adamw("adamw_small", pack_small(weights), pack_small(grads), pack_small(ms), pack_small(vs))
    off = 0
    for n in small_names:
        shp = weights[n].shape
        cnt = int(np.prod(shp))
        r = _rows_of(cnt)
        for dst, src in ((delta, sd), (new_m, sm), (new_v, sv_)):
            dst[n] = src[off:off + r].reshape(-1)[:cnt].reshape(shp)
        off += r

    out_g = {n: grads[n].reshape(weights[n].shape) for n in names}
    return (loss, grad_x.reshape(x.shape), *[out_g[n] for n in names], *[delta[n] for n in names],
            *[new_m[n] for n in names], *[new_v[n] for n in names])
```

```python
import functools
import math

import numpy as np
import jax
import jax.numpy as jnp
from jax import lax
from jax.experimental import pallas as pl
from jax.experimental.pallas import tpu as pltpu

F32 = jnp.float32
BF16 = jnp.bfloat16

D = 1024
S = 4096
HD = 64
WA = 512
DFF = 2816
NCHIP = 4
NDEV = 8
PATTERNS = ((128, 1), (512, 4), (2048, 16))
ROPE_THETA = 500000.0
ROPE_DIMS = HD // 4
ALPHA = (2.0 * 1) ** 0.25
LN_EPS = 1e-5
RMS_EPS = 1e-6
ADAM_LR = 0.001
ADAM_B1 = 0.9
ADAM_B2 = 0.999
ADAM_EPS = 1e-08
ADAM_WD = 0.01
ADAM_STEP = 10

LANES = 128
SUBLANES = 8
TQ = 512
FPAD = LANES
NEG = -1e30
VMEM_LIMIT = 56 * 1024 * 1024
MESH = pl.DeviceIdType.MESH


def _cparams(sem):
    return pltpu.CompilerParams(dimension_semantics=sem, vmem_limit_bytes=VMEM_LIMIT)


def _pick(n, cands):
    for c in cands:
        if n % c == 0:
            return c
    return n


def _rsum8(v):
    tm, w = v.shape
    return jnp.sum(v.reshape(tm // SUBLANES, SUBLANES, w), axis=0)


def _sigmoid(x):
    return 1.0 / (1.0 + jnp.exp(-x))


def _dot(a, b):
    return jnp.dot(a, b, preferred_element_type=F32)


def _dot_nt(a, b):
    return lax.dot_general(a, b, (((1,), (1,)), ((), ())), preferred_element_type=F32)


def _dot_tn(a, b):
    return lax.dot_general(a, b, (((0,), (0,)), ((), ())), preferred_element_type=F32)


def _rowwise(name, fn, T, tm, *, tiles=(), halos=(), seqvecs=(), consts=(), outs=(), accs=(), seqaccs=(),
             seq_len=None):
    seq_len = S if seq_len is None else seq_len
    nb = T // tm
    spb = max(seq_len // tm, 1)
    nseq = max(T // seq_len, 1)
    n8 = T // SUBLANES
    r8 = tm // SUBLANES
    in_specs, args = [], []
    for a in tiles:
        in_specs.append(pl.BlockSpec((tm, a.shape[1]), lambda i: (i, 0)))
        args.append(a)
    for a, direction in halos:
        if direction < 0:
            idx = lambda i: (jnp.maximum(i * r8 - 1, 0), 0)
        else:
            idx = lambda i: (jnp.minimum((i + 1) * r8, n8 - 1), 0)
        in_specs.append(pl.BlockSpec((SUBLANES, a.shape[1]), idx))
        args.append(a)
    for a in seqvecs:
        in_specs.append(pl.BlockSpec((1, 1, a.shape[2]), lambda i: (i // spb, 0, 0)))
        args.append(a)
    for a in consts:
        in_specs.append(pl.BlockSpec(a.shape, lambda i, nd=a.ndim: (0,) * nd))
        args.append(a)
    out_shape, out_specs = [], []
    for w, dt in outs:
        out_shape.append(jax.ShapeDtypeStruct((T, w), dt))
        out_specs.append(pl.BlockSpec((tm, w), lambda i: (i, 0)))
    for w in accs:
        out_shape.append(jax.ShapeDtypeStruct((SUBLANES, w), F32))
        out_specs.append(pl.BlockSpec((SUBLANES, w), lambda i: (0, 0)))
    for w in seqaccs:
        out_shape.append(jax.ShapeDtypeStruct((nseq, SUBLANES, w), F32))
        out_specs.append(pl.BlockSpec((1, SUBLANES, w), lambda i: (i // spb, 0, 0)))
    n_t, n_h, n_s, n_c = len(tiles), len(halos), len(seqvecs), len(consts)
    n_o, n_a, n_sa = len(outs), len(accs), len(seqaccs)

    def body(*refs):
        i = pl.program_id(0)
        ins = refs[:n_t + n_h + n_s + n_c]
        orefs = refs[n_t + n_h + n_s + n_c:]
        vals = [r[...] for r in ins[:n_t + n_h]]
        vals += [r[0] for r in ins[n_t + n_h:n_t + n_h + n_s]]
        vals += list(ins[n_t + n_h + n_s:])
        res = fn(i, *vals)
        if not isinstance(res, (tuple, list)):
            res = (res,)
        for k in range(n_o):
            orefs[k][...] = res[k].astype(orefs[k].dtype)
        for k in range(n_a):
            r = orefs[n_o + k]

            @pl.when(i == 0)
            def _():
                r[...] = jnp.zeros_like(r)

            r[...] += res[n_o + k]

            @pl.when(i == nb - 1)
            def _():
                r[...] = jnp.broadcast_to(jnp.sum(r[...], axis=0, keepdims=True), r.shape)
        for k in range(n_sa):
            r = orefs[n_o + n_a + k]

            @pl.when(i % spb == 0)
            def _():
                r[...] = jnp.zeros_like(r)

            r[0] += res[n_o + n_a + k]

            @pl.when(i % spb == spb - 1)
            def _():
                r[0] = jnp.broadcast_to(jnp.sum(r[0], axis=0, keepdims=True), r.shape[1:])

    sem = ("arbitrary",) if (n_a or n_sa) else ("parallel",)
    res = pl.pallas_call(
        body, name=name, grid=(nb,), in_specs=in_specs, out_specs=out_specs, out_shape=out_shape,
        compiler_params=_cparams(sem),
    )(*args)
    return res


def _ln_fwd(r, g, b):
    mu = jnp.mean(r, axis=-1, keepdims=True)
    xc = r - mu
    var = jnp.mean(xc * xc, axis=-1, keepdims=True)
    rstd = lax.rsqrt(var + LN_EPS)
    n = xc * rstd
    return n * g + b, n, rstd


def _ln_bwd(dy, n, rstd, g):
    dn = dy * g
    return rstd * (dn - jnp.mean(dn, axis=-1, keepdims=True) - n * jnp.mean(dn * n, axis=-1, keepdims=True))


def _head_mean(t, g_ref):
    gw = g_ref.shape[0]
    hi = t.astype(BF16)
    lo = (t - hi.astype(F32)).astype(BF16)
    g = g_ref[...]
    parts = []
    for c in range(t.shape[1] // gw):
        sl = slice(c * gw, (c + 1) * gw)
        parts.append(_dot(hi[:, sl], g) + _dot(lo[:, sl], g))
    out = parts[0] if len(parts) == 1 else jnp.concatenate(parts, axis=1)
    return out * (1.0 / HD)


def _rope(z, c, s1, s2):
    w = z.shape[1]
    half = ROPE_DIMS // 2
    return z * c + pltpu.roll(z, half, 1) * s1 + pltpu.roll(z, w - half, 1) * s2


def _tile_lanes(t, w):
    reps = w // t.shape[1]
    return t if reps == 1 else jnp.concatenate([t] * reps, axis=1)


def _conv_taps(ext, prev, first):
    tm = ext.shape[0]
    prev = jnp.where(first, jnp.zeros_like(prev), prev)
    r8 = lax.broadcasted_iota(jnp.int32, (SUBLANES, 1), 0)
    top = ext[0:SUBLANES]
    s1_top = jnp.where(r8 < 1, pltpu.roll(prev, 1, 0), pltpu.roll(top, 1, 0))
    s2_top = jnp.where(r8 < 2, pltpu.roll(prev, 2, 0), pltpu.roll(top, 2, 0))
    s1 = jnp.concatenate([s1_top, pltpu.roll(ext, 1, 0)[SUBLANES:]], axis=0)
    s2 = jnp.concatenate([s2_top, pltpu.roll(ext, 2, 0)[SUBLANES:]], axis=0)
    return s1, s2


def _conv_taps_up(ext, nxt, last):
    tm = ext.shape[0]
    nxt = jnp.where(last, jnp.zeros_like(nxt), nxt)
    r8 = lax.broadcasted_iota(jnp.int32, (SUBLANES, 1), 0)
    bot = ext[tm - SUBLANES:tm]
    u1_bot = jnp.where(r8 >= 7, pltpu.roll(nxt, 7, 0), pltpu.roll(bot, 7, 0))
    u2_bot = jnp.where(r8 >= 6, pltpu.roll(nxt, 6, 0), pltpu.roll(bot, 6, 0))
    u1 = jnp.concatenate([pltpu.roll(ext, tm - 1, 0)[:tm - SUBLANES], u1_bot], axis=0)
    u2 = jnp.concatenate([pltpu.roll(ext, tm - 2, 0)[:tm - SUBLANES], u2_bot], axis=0)
    return u1, u2


def _mm_nt(name, a, w, tm=256):
    T = a.shape[0]
    n = w.shape[0]
    ch = _pick(n, (512, 256, 128))

    def fn(i, av, w_ref):
        ab = av.astype(BF16)
        parts = [_dot_nt(ab, w_ref[c * ch:(c + 1) * ch, :]) for c in range(n // ch)]
        return parts[0] if len(parts) == 1 else jnp.concatenate(parts, axis=1)

    return _rowwise(name, fn, T, tm, tiles=(a,), consts=(w,), outs=((n, F32),))[0]


def _mm_tn(name, a, b, *, mod=None, tt=512, by_chip=False):
    T, k1 = a.shape
    k2 = b.shape[1]
    t1 = k1 if k1 <= 1536 else _pick(k1, (1408, 1024, 512, 256, 128))
    t2 = k2 if k2 <= 1536 else _pick(k2, (1408, 1024, 640, 512, 256, 128))
    if by_chip:
        t2 = k2 // NCHIP
    tt = min(tt, S)
    spb = S // tt

    def body(*refs):
        if mod is not None:
            a_ref, sc_ref, sh_ref, b_ref, o_ref = refs
        else:
            a_ref, b_ref, o_ref = refs
        t = pl.program_id(2)

        @pl.when(t == 0)
        def _():
            o_ref[...] = jnp.zeros_like(o_ref)

        av = a_ref[...]
        if mod is not None:
            av = av * (1.0 + sc_ref[0]) + sh_ref[0]
        o_ref[...] += _dot_tn(av.astype(BF16), b_ref[...].astype(BF16))

    in_specs = [pl.BlockSpec((tt, t1), lambda p, q, t: (t, p))]
    args = [a]
    if mod is not None:
        for v in mod:
            in_specs.append(pl.BlockSpec((1, 1, t1), lambda p, q, t: (t // spb, 0, p)))
            args.append(v)
    in_specs.append(pl.BlockSpec((tt, t2), lambda p, q, t: (t, q)))
    args.append(b)
    if by_chip:
        out_specs = pl.BlockSpec((None, t1, t2), lambda p, q, t: (q, p, 0))
        out_shape = jax.ShapeDtypeStruct((NCHIP, k1, t2), F32)
    else:
        out_specs = pl.BlockSpec((t1, t2), lambda p, q, t: (p, q))
        out_shape = jax.ShapeDtypeStruct((k1, k2), F32)
    return pl.pallas_call(
        body, name=name, grid=(k1 // t1, k2 // t2, T // tt), in_specs=in_specs, out_specs=out_specs,
        out_shape=out_shape, compiler_params=_cparams(("parallel", "parallel", "arbitrary")),
    )(*args)


def _mod_mm(name, x, sc, sh, ws, out_dtypes, rope=None, rope_secs=(), tm=256):
    T = x.shape[0]
    nw = len(ws)

    def fn(i, xv, *rest):
        if rope is not None:
            cv, s1v, s2v = rest[:3]
            rest = rest[3:]
        scv, shv = rest[:2]
        w_refs = rest[2:]
        h = (xv * (1.0 + scv) + shv).astype(BF16)
        res = []
        for k, w_ref in enumerate(w_refs):
            n = w_ref.shape[1]
            ch = WA if (k == 0 and rope is not None) else _pick(n, (512, 256, 128))
            parts = []
            for c in range(n // ch):
                z = _dot(h, w_ref[:, c * ch:(c + 1) * ch])
                if k == 0 and c in rope_secs:
                    z = _rope(z, _tile_lanes(cv, ch), _tile_lanes(s1v, ch), _tile_lanes(s2v, ch))
                parts.append(z.astype(out_dtypes[k]))
            res.append(parts[0] if len(parts) == 1 else jnp.concatenate(parts, axis=1))
        return tuple(res)

    tiles = (x,) + (tuple(rope) if rope is not None else ())
    outs = tuple((w.shape[1], dt) for w, dt in zip(ws, out_dtypes))
    return _rowwise(name, fn, T, tm, tiles=tiles, seqvecs=(sc, sh), consts=tuple(ws), outs=outs)


def _tri(tb, lower):
    r = lax.broadcasted_iota(jnp.int32, (tb, tb), 0)
    c = lax.broadcasted_iota(jnp.int32, (tb, tb), 1)
    return jnp.where((r >= c) if lower else (r <= c), 1.0, 0.0).astype(BF16)


def _split3(x):
    hi = x.astype(BF16)
    r = x - hi.astype(F32)
    mid = r.astype(BF16)
    return hi, mid, (r - mid.astype(F32)).astype(BF16)


def _cumsum_seq(name, ins, consts, fn_in, fn_out, outs, reverse, n_acc=0, tb=256):
    T = ins[0].shape[0]
    tb = min(tb, S)
    nbs = S // tb
    nseq = T // S
    n_i, n_c, n_o = len(ins), len(consts), len(outs)

    def blk(b, j):
        return (b * nbs + (nbs - 1 - j if reverse else j), 0)

    def body(*refs):
        i_refs, c_refs = refs[:n_i], refs[n_i:n_i + n_c]
        o_refs = refs[n_i + n_c:n_i + n_c + n_o]
        acc_refs = refs[n_i + n_c + n_o:n_i + n_c + n_o + n_acc]
        carry = refs[-1]
        b, j = pl.program_id(0), pl.program_id(1)

        @pl.when(j == 0)
        def _():
            carry[...] = jnp.zeros_like(carry)

        iv = [r[...] for r in i_refs]
        xin = fn_in(*iv, *c_refs)
        tri = _tri(tb, not reverse)
        cum = sum(_dot(tri, piece) for piece in _split3(xin)) + carry[0:1, :]
        carry[...] = carry[...] + jnp.sum(xin, axis=0, keepdims=True)
        res = fn_out(cum, *iv, *c_refs)
        for o, r in zip(o_refs, res):
            o[...] = r.astype(o.dtype)
        for a in acc_refs:
            @pl.when((b == 0) & (j == 0))
            def _():
                a[...] = jnp.zeros_like(a)

            a[...] += _rsum8(res[0])

            @pl.when((b == nseq - 1) & (j == nbs - 1))
            def _():
                a[...] = jnp.broadcast_to(jnp.sum(a[...], axis=0, keepdims=True), a.shape)

    in_specs = [pl.BlockSpec((tb, a.shape[1]), blk) for a in ins]
    in_specs += [pl.BlockSpec(c.shape, lambda b, j, nd=c.ndim: (0,) * nd) for c in consts]
    out_shape = [jax.ShapeDtypeStruct((T, w), dt) for w, dt in outs]
    out_shape += [jax.ShapeDtypeStruct((SUBLANES, outs[0][0]), F32)] * n_acc
    out_specs = [pl.BlockSpec((tb, w), blk) for w, _ in outs]
    out_specs += [pl.BlockSpec((SUBLANES, outs[0][0]), lambda b, j: (0, 0))] * n_acc
    return pl.pallas_call(
        body, name=name, grid=(nseq, nbs), in_specs=in_specs, out_specs=out_specs, out_shape=out_shape,
        scratch_shapes=[pltpu.VMEM((SUBLANES, FPAD), F32)],
        compiler_params=_cparams(("arbitrary", "arbitrary")),
    )(*ins, *consts)


def _log_sigmoid(x):
    return jnp.minimum(x, 0.0) - jnp.log(1.0 + jnp.exp(-jnp.abs(x)))


def _dil_bias(tq):
    max_win = max(w for w, _ in PATTERNS)
    nd = (max_win + tq - 1) // tq + 1
    qi = np.arange(tq)[:, None]
    kj = np.arange(tq)[None, :]
    tabs = []
    for dlt in range(nd):
        dist = dlt * tq + qi - kj
        mult = np.zeros((tq, tq), np.float64)
        for win, dil in PATTERNS:
            mult += (dist >= 0) & (dist % dil == 0) & (dist // dil <= win // dil)
        tabs.append(np.where(mult > 0, np.log(np.maximum(mult, 1.0)), NEG))
    return np.stack(tabs).astype(np.float32)


def _fold_tables(nha):
    hp_n = nha // 2
    pq = np.zeros((3, FPAD, hp_n * 2 * LANES), np.float32)
    pk = np.zeros((3, FPAD, hp_n * LANES), np.float32)
    oq = np.zeros((1, hp_n * 2 * LANES), np.float32)
    ok = np.zeros((1, hp_n * LANES), np.float32)
    sq = np.zeros((hp_n * LANES, FPAD), np.float32)
    sk = np.zeros((hp_n * LANES, FPAD), np.float32)
    for h in range(nha):
        hp, odd = divmod(h, 2)
        qb = hp * 2 * LANES + odd * (LANES + 8)
        kb = hp * LANES + odd * 8
        for i in range(3):
            pq[i, h, qb + i] = 1
            oq[0, qb + 3 + i] = 1
            ok[0, kb + i] = 1
            pk[i, h, kb + 3 + i] = 1
        sq[kb, h] = 1
        sk[kb + 3, h] = 1
    return pq, pk, oq, ok, sq, sk


def _stack_heads(x2, h0, extra=None):
    z = jnp.zeros_like(x2)
    a, b = jnp.where(h0, x2, z), jnp.where(h0, z, x2)
    if extra is not None:
        a = jnp.concatenate([a, extra[:, :LANES]], axis=1)
        b = jnp.concatenate([b, extra[:, LANES:]], axis=1)
    return jnp.concatenate([a, b], axis=0)


def _attn_fwd(name, qkv, secs, fox, eq=None, ek=None, bias=None):
    T = qkv.shape[0]
    nq = S // TQ
    nbl = T // S
    hp_n = WA // LANES
    sq, sk, sv = (s * hp_n for s in secs)
    scale = HD ** -0.5
    nd = None if fox else bias.shape[0]

    def body(*refs):
        if fox:
            q_ref, k_ref, v_ref, eq_ref, ek_ref, o_ref, lse_ref = refs
        else:
            q_ref, k_ref, v_ref, b_ref, o_ref, lse_ref = refs
        i = pl.program_id(2)
        lane = lax.broadcasted_iota(jnp.int32, (1, LANES), 1)
        h0 = lane < HD
        q2 = (q_ref[...].astype(F32) * scale).astype(BF16)
        qs = _stack_heads(q2, h0, eq_ref[...] if fox else None)

        def scores(t, diag):
            off = pl.multiple_of((i - t) * TQ, TQ)
            kk = k_ref[pl.ds(off, TQ), :]
            if fox:
                kk = jnp.concatenate([kk, ek_ref[pl.ds(off, TQ), :]], axis=1)
            s = jnp.concatenate([_dot_nt(qs[:TQ], kk), _dot_nt(qs[TQ:], kk)], axis=0)
            if not fox:
                s = (s.reshape(2, TQ, TQ) + b_ref[t]).reshape(2 * TQ, TQ)
            elif diag:
                rows = lax.broadcasted_iota(jnp.int32, (2, TQ, TQ), 1).reshape(2 * TQ, TQ)
                cols = lax.broadcasted_iota(jnp.int32, (2 * TQ, TQ), 1)
                s = jnp.where(cols <= rows, s, NEG)
            return s

        def update(t, s, m, l, acc):
            off = pl.multiple_of((i - t) * TQ, TQ)
            v2 = v_ref[pl.ds(off, TQ), :]
            m_new = jnp.maximum(m, jnp.max(s, axis=1, keepdims=True))
            p = jnp.exp(s - m_new)
            a = jnp.exp(m - m_new)
            l = a * l + jnp.sum(p, axis=1, keepdims=True)
            pb = p.astype(BF16)
            acc = a * acc + jnp.concatenate([_dot(pb[:TQ], v2), _dot(pb[TQ:], v2)], axis=0)
            return m_new, l, acc

        def step(t, carry):
            s, m, l, acc = carry
            return (scores(t + 1, False),) + update(t, s, m, l, acc)

        init = (jnp.full((2 * TQ, 1), NEG, F32), jnp.zeros((2 * TQ, 1), F32), jnp.zeros((2 * TQ, LANES), F32))
        n = i + 1 if fox else jnp.minimum(i + 1, nd)
        m, l, acc = update(0, scores(0, True), *init)
        m, l, acc = lax.fori_loop(1, n, lambda t, c: update(t, scores(t, False), *c), (m, l, acc))
        on = acc / l
        o_ref[...] = jnp.where(h0, on[:TQ], on[TQ:])
        lse = jnp.broadcast_to(m + jnp.log(l), (2 * TQ, LANES))
        lse_ref[...] = jnp.concatenate([lse[:TQ], lse[TQ:]], axis=1)

    in_specs = [
        pl.BlockSpec((TQ, LANES), lambda b, hp, i: (b * nq + i, sq + hp)),
        pl.BlockSpec((S, LANES), lambda b, hp, i: (b, sk + hp)),
        pl.BlockSpec((S, LANES), lambda b, hp, i: (b, sv + hp)),
    ]
    args = [qkv, qkv, qkv]
    if fox:
        in_specs += [pl.BlockSpec((TQ, 2 * LANES), lambda b, hp, i: (b * nq + i, hp)),
                     pl.BlockSpec((S, LANES), lambda b, hp, i: (b, hp))]
        args += [eq, ek]
    else:
        in_specs.append(pl.BlockSpec(bias.shape, lambda b, hp, i: (0, 0, 0)))
        args.append(bias)
    return pl.pallas_call(
        body, name=name, grid=(nbl, hp_n, nq), in_specs=in_specs,
        out_specs=[pl.BlockSpec((TQ, LANES), lambda b, hp, i: (b * nq + i, hp)),
                   pl.BlockSpec((TQ, 2 * LANES), lambda b, hp, i: (b * nq + i, hp))],
        out_shape=[jax.ShapeDtypeStruct((T, WA), F32), jax.ShapeDtypeStruct((T, 2 * WA), F32)],
        compiler_params=_cparams(("parallel", "parallel", "arbitrary")),
    )(*args)


def _attn_bwd(name, qkv, secs, o, do, do_sec, lse, fox, eq=None, ek=None, bias=None):
    T = qkv.shape[0]
    nq = S // TQ
    nbl = T // S
    hp_n = WA // LANES
    sq, sk, sv = (s * hp_n for s in secs)
    dsec = do_sec * hp_n
    scale = HD ** -0.5
    nd = None if fox else bias.shape[0]
    kc = 2 * LANES if fox else LANES

    def body(*refs):
        if fox:
            (q_ref, k_ref, v_ref, o_ref, do_ref, lse_ref, eq_ref, ek_ref,
             dq_ref, dk_ref, dv_ref, dqe_ref, dek_ref, dl_ref) = refs
        else:
            q_ref, k_ref, v_ref, o_ref, do_ref, lse_ref, b_ref, dq_ref, dk_ref, dv_ref, dl_ref = refs
        j = pl.program_id(2)
        lane = lax.broadcasted_iota(jnp.int32, (1, LANES), 1)
        h0 = lane < HD

        @pl.when(j == 0)
        def _():
            dq_ref[...] = jnp.zeros_like(dq_ref)
            if fox:
                dqe_ref[...] = jnp.zeros_like(dqe_ref)

            def dl_step(r, c):
                off = pl.multiple_of(r * TQ, TQ)
                d2 = do_ref[pl.ds(off, TQ), :] * o_ref[pl.ds(off, TQ), :]
                z2 = jnp.zeros_like(d2)
                dl0 = jnp.sum(jnp.where(h0, d2, z2), axis=1, keepdims=True)
                dl1 = jnp.sum(jnp.where(h0, z2, d2), axis=1, keepdims=True)
                dl_ref[pl.ds(off, TQ), :] = jnp.concatenate(
                    [jnp.broadcast_to(dl0, (TQ, LANES)), jnp.broadcast_to(dl1, (TQ, LANES))], axis=1)
                return c

            lax.fori_loop(0, nq, dl_step, 0)

        kk = k_ref[...]
        if fox:
            kk = jnp.concatenate([kk, ek_ref[...]], axis=1)
        v2 = v_ref[...]

        def wide(x2):
            st = jnp.concatenate([x2[:, :LANES], x2[:, LANES:]], axis=0)
            return st if TQ == LANES else jnp.concatenate([st] * (TQ // LANES), axis=1)

        def step(t, carry, diag):
            dkk, dv2 = carry
            off = pl.multiple_of((j + t) * TQ, TQ)
            q2 = (q_ref[pl.ds(off, TQ), :].astype(F32) * scale).astype(BF16)
            qs = _stack_heads(q2, h0, eq_ref[pl.ds(off, TQ), :] if fox else None)
            dos = _stack_heads(do_ref[pl.ds(off, TQ), :].astype(BF16), h0)
            s = jnp.concatenate([_dot_nt(qs[:TQ], kk), _dot_nt(qs[TQ:], kk)], axis=0)
            if not fox:
                s = (s.reshape(2, TQ, TQ) + b_ref[t]).reshape(2 * TQ, TQ)
            elif diag:
                rows = lax.broadcasted_iota(jnp.int32, (2, TQ, TQ), 1).reshape(2 * TQ, TQ)
                cols = lax.broadcasted_iota(jnp.int32, (2 * TQ, TQ), 1)
                s = jnp.where(cols <= rows, s, NEG)
            p = jnp.exp(s - wide(lse_ref[pl.ds(off, TQ), :]))
            dp = jnp.concatenate([_dot_nt(dos[:TQ], v2), _dot_nt(dos[TQ:], v2)], axis=0)
            dsb = (p * (dp - wide(dl_ref[pl.ds(off, TQ), :]))).astype(BF16)
            dv2 = dv2 + _dot_tn(p.astype(BF16), dos)
            dkk = dkk + _dot_tn(dsb, qs)
            dqq = jnp.concatenate([_dot(dsb[:TQ], kk), _dot(dsb[TQ:], kk)], axis=0)
            dq_ref[pl.ds(off, TQ), :] += jnp.where(h0, dqq[:TQ, :LANES], dqq[TQ:, :LANES])
            if fox:
                dqe_ref[pl.ds(off, TQ), :] += jnp.where(lane < SUBLANES, dqq[:TQ, LANES:], dqq[TQ:, LANES:])
            return dkk, dv2

        zero = (jnp.zeros((TQ, kc), F32), jnp.zeros((TQ, LANES), F32))
        if fox:
            dkk, dv2 = lax.fori_loop(1, nq - j, lambda t, c: step(t, c, False), step(0, zero, True))
        else:
            dkk, dv2 = lax.fori_loop(0, jnp.minimum(nq - j, nd), lambda t, c: step(t, c, False), zero)
        dk_ref[...] = dkk[:, :LANES]
        dv_ref[...] = dv2
        if fox:
            dek_ref[...] = dkk[:, LANES:]

        @pl.when(j == nq - 1)
        def _():
            dq_ref[...] = dq_ref[...] * scale

    seq = lambda c, w=LANES: pl.BlockSpec((S, w), lambda b, hp, j: (b, c + hp))
    blk = lambda c: pl.BlockSpec((TQ, LANES), lambda b, hp, j: (b * nq + j, c + hp))
    in_specs = [seq(sq), blk(sk), blk(sv), seq(0), seq(dsec), seq(0, 2 * LANES)]
    args = [qkv, qkv, qkv, o, do, lse]
    if fox:
        in_specs += [seq(0, 2 * LANES), blk(0)]
        args += [eq, ek]
    else:
        in_specs.append(pl.BlockSpec(bias.shape, lambda b, hp, j: (0, 0, 0)))
        args.append(bias)
    out_specs = [seq(0), blk(0), blk(0)]
    out_shape = [jax.ShapeDtypeStruct((T, WA), F32)] * 3
    if fox:
        out_specs += [seq(0), blk(0)]
        out_shape += [jax.ShapeDtypeStruct((T, WA), F32)] * 2
    return pl.pallas_call(
        body, name=name, grid=(nbl, hp_n, nq), in_specs=in_specs, out_specs=out_specs, out_shape=out_shape,
        scratch_shapes=[pltpu.VMEM((S, 2 * LANES), F32)],
        compiler_params=_cparams(("parallel", "parallel", "arbitrary")),
    )(*args)


def _exchange(name, ins, out_shapes, remote, local):
    n_in, n_out = len(ins), len(out_shapes)
    nr, nl = len(remote), len(local)

    def body(*refs):
        in_refs = refs[:n_in]
        out_refs = refs[n_in:n_in + n_out]
        send_sems, recv_sems, loc_sems = refs[n_in + n_out:]
        me = (lax.axis_index("x"), lax.axis_index("y"), lax.axis_index("c"))

        def peer_of(flip):
            return tuple(1 - v if f else v for v, f in zip(me, flip))

        def at(ref, idx):
            return ref if idx is None else ref.at[idx]

        def rcopy(k, who):
            flip, a, sfn, b, dfn = remote[k]
            return pltpu.make_async_remote_copy(
                src_ref=at(in_refs[a], sfn(*who)), dst_ref=at(out_refs[b], dfn(*who)),
                send_sem=send_sems.at[k], recv_sem=recv_sems.at[k],
                device_id=peer_of(flip), device_id_type=MESH)

        locs = [pltpu.make_async_copy(at(in_refs[a], sfn(*me)), at(out_refs[b], dfn(*me)), loc_sems.at[k])
                for k, (a, sfn, b, dfn) in enumerate(local)]
        for cp in locs:
            cp.start()
        sends = [rcopy(k, me) for k in range(nr)]
        for cp in sends:
            cp.start()
        for k in range(nr):
            rcopy(k, peer_of(remote[k][0])).wait_recv()
        for cp in sends:
            cp.wait_send()
        for cp in locs:
            cp.wait()

    any_spec = pl.BlockSpec(memory_space=pl.ANY)
    return pl.pallas_call(
        body, name=name, in_specs=[any_spec] * n_in, out_specs=[any_spec] * n_out, out_shape=list(out_shapes),
        scratch_shapes=[pltpu.SemaphoreType.DMA((max(nr, 1),)), pltpu.SemaphoreType.DMA((max(nr, 1),)),
                        pltpu.SemaphoreType.DMA((max(nl, 1),))],
    )(*ins)


_FLIPS7 = [(0, 0, 1), (0, 1, 0), (0, 1, 1), (1, 0, 0), (1, 0, 1), (1, 1, 0), (1, 1, 1)]
_CHIP_FLIPS = [(1, 0, 0), (0, 1, 0), (1, 1, 0)]


def _dev_index(x, y, c):
    return 4 * x + 2 * y + c


def _chip_index(x, y, c):
    return 2 * x + y


def _all_gather8(name, v):
    remote = [(f, 0, lambda x, y, c: None, 0, _dev_index) for f in _FLIPS7]
    local = [(0, lambda x, y, c: None, 0, _dev_index)]
    return _exchange(name, [v], [jax.ShapeDtypeStruct((NDEV,) + v.shape, v.dtype)], remote, local)[0]


def _gather_halves(name, vs):
    n_v = len(vs)

    def body(*refs):
        in_refs, out_refs = refs[:n_v], refs[n_v:2 * n_v]
        send_sems, recv_sems = refs[2 * n_v:]
        x, y, c = lax.axis_index("x"), lax.axis_index("y"), lax.axis_index("c")
        sibling = (x, y, 1 - c)
        chips = [(1 - x, y), (x, 1 - y), (1 - x, 1 - y)]

        def copy(k, n, src, blk, half, to):
            return pltpu.make_async_remote_copy(
                src_ref=src, dst_ref=out_refs[n].at[blk, half], send_sem=send_sems.at[k], recv_sem=recv_sems.at[k],
                device_id=to, device_id_type=MESH)

        first = [copy(6 * n + j, n, in_refs[n].at[c], j, c, (*chip, c))
                 for n in range(n_v) for j, chip in enumerate(chips)]
        for cp in first:
            cp.start()
        passed = []
        for n in range(n_v):
            for j, chip in enumerate(chips):
                copy(6 * n + j, n, in_refs[n].at[c], j, c, (*chip, c)).wait_recv()
                fw = copy(6 * n + 3 + j, n, out_refs[n].at[j, c], j, c, sibling)
                fw.start()
                passed.append(fw)
        for n in range(n_v):
            for j in range(len(chips)):
                copy(6 * n + 3 + j, n, out_refs[n].at[j, 1 - c], j, 1 - c, sibling).wait_recv()
        for cp in first + passed:
            cp.wait_send()

    any_spec = pl.BlockSpec(memory_space=pl.ANY)
    return pl.pallas_call(
        body, name=name, in_specs=[any_spec] * n_v, out_specs=[any_spec] * n_v,
        out_shape=[jax.ShapeDtypeStruct((NCHIP - 1,) + v.shape, v.dtype) for v in vs],
        scratch_shapes=[pltpu.SemaphoreType.DMA((6 * n_v,)), pltpu.SemaphoreType.DMA((6 * n_v,))],
    )(*vs)


def _to_sibling(name, v):
    remote = [((0, 0, 1), 0, lambda x, y, c: None, 0, lambda x, y, c: None)]
    return _exchange(name, [v], [jax.ShapeDtypeStruct(v.shape, v.dtype)], remote, [])[0]


def _scatter_chips(name, v):
    remote = []
    for j, f in enumerate(_CHIP_FLIPS):
        src = lambda x, y, c, f=f: _chip_index(1 - x if f[0] else x, 1 - y if f[1] else y, c)
        remote.append((f, 0, src, 0, lambda x, y, c, j=j: j))
    return _exchange(name, [v], [jax.ShapeDtypeStruct((NCHIP - 1,) + v.shape[1:], v.dtype)], remote, [])[0]


def _by_chip(own, others, chip):
    stacked = jnp.concatenate([own[None], others], axis=0)
    blocks = []
    for k in range(NCHIP):
        d = k ^ chip
        place = jnp.where(d == 0, 0, jnp.where(d == 2, 1, jnp.where(d == 1, 2, 3)))
        blocks.append(lax.dynamic_index_in_dim(stacked, place, axis=0, keepdims=False))
    return jnp.stack(blocks)


def _sum_leading(name, v, tm=None):
    n, r, w = v.shape
    tm = _pick(r, (256, 128, 64, 32, 16, 8)) if tm is None else tm

    def body(v_ref, o_ref):
        acc = v_ref[0].astype(F32)
        for k in range(1, n):
            acc = acc + v_ref[k].astype(F32)
        o_ref[...] = acc

    return pl.pallas_call(
        body, name=name, grid=(r // tm,), in_specs=[pl.BlockSpec((n, tm, w), lambda i: (0, i, 0))],
        out_specs=pl.BlockSpec((tm, w), lambda i: (i, 0)), out_shape=jax.ShapeDtypeStruct((r, w), F32),
        compiler_params=_cparams(("parallel",)),
    )(v)


def _add2(name, a, b, tm=None, out_dtype=F32):
    r, w = a.shape
    tm = _pick(r, (256, 128, 64, 32, 16, 8)) if tm is None else tm

    def body(a_ref, b_ref, o_ref):
        o_ref[...] = (a_ref[...] + b_ref[...]).astype(out_dtype)

    spec = pl.BlockSpec((tm, w), lambda i: (i, 0))
    return pl.pallas_call(
        body, name=name, grid=(r // tm,), in_specs=[spec, spec], out_specs=spec,
        out_shape=jax.ShapeDtypeStruct((r, w), out_dtype), compiler_params=_cparams(("parallel",)),
    )(a, b)


def _ada_fwd(call_all, w_shard):
    def body(c_ref, w_ref, o_ref):
        cv = c_ref[...]
        o_ref[...] = jnp.dot(cv * _sigmoid(cv), w_ref[...], preferred_element_type=F32,
                             precision=lax.Precision.HIGHEST)

    n = w_shard.shape[1]
    return pl.pallas_call(
        body, name="ada_fwd", out_shape=jax.ShapeDtypeStruct((call_all.shape[0], n), F32),
        compiler_params=pltpu.CompilerParams(vmem_limit_bytes=VMEM_LIMIT),
    )(call_all, w_shard)


def _ada_bwd(call_all, dada):
    def body(c_ref, d_ref, o_ref):
        cv = c_ref[...]
        o_ref[...] = lax.dot_general(cv * _sigmoid(cv), d_ref[...], (((0,), (0,)), ((), ())),
                                     preferred_element_type=F32, precision=lax.Precision.HIGHEST)

    return pl.pallas_call(
        body, name="ada_bwd", out_shape=jax.ShapeDtypeStruct((call_all.shape[1], dada.shape[1]), F32),
        compiler_params=pltpu.CompilerParams(vmem_limit_bytes=VMEM_LIMIT),
    )(call_all, dada)


def _adamw(name, w, g, m, v):
    r, wd = w.shape
    tm = _pick(r, (256, 128, 64, 32, 16, 8))
    bc1 = 1.0 - ADAM_B1 ** ADAM_STEP
    bc2 = 1.0 - ADAM_B2 ** ADAM_STEP

    def body(w_ref, g_ref, m_ref, v_ref, d_ref, mo_ref, vo_ref):
        gv = g_ref[...]
        mn = ADAM_B1 * m_ref[...] + (1.0 - ADAM_B1) * gv
        vn = ADAM_B2 * v_ref[...] + (1.0 - ADAM_B2) * (gv * gv)
        d_ref[...] = -ADAM_LR * ((mn / bc1) / (jnp.sqrt(vn / bc2) + ADAM_EPS) + ADAM_WD * w_ref[...])
        mo_ref[...] = mn
        vo_ref[...] = vn

    spec = pl.BlockSpec((tm, wd), lambda i: (i, 0))
    return pl.pallas_call(
        body, name=name, grid=(r // tm,), in_specs=[spec] * 4, out_specs=[spec] * 3,
        out_shape=[jax.ShapeDtypeStruct((r, wd), F32)] * 3, compiler_params=_cparams(("parallel",)),
    )(w, g, m, v)


def _rope_tables(positions):
    half = ROPE_DIMS // 2
    freqs = ROPE_THETA ** (-jnp.arange(0, ROPE_DIMS, 2, dtype=F32) / ROPE_DIMS)
    ang = positions.astype(F32).reshape(-1, 1) * freqs
    cos, sin = jnp.cos(ang), jnp.sin(ang)
    T = ang.shape[0]
    one = jnp.ones((T, HD - ROPE_DIMS), F32)
    zero = jnp.zeros((T, HD - ROPE_DIMS), F32)
    zh = jnp.zeros((T, half), F32)
    c64 = jnp.concatenate([cos, cos, one], axis=1)
    s1 = jnp.concatenate([zh, sin, zero], axis=1)
    s2 = jnp.concatenate([-sin, zh, zero], axis=1)
    rep = lambda t: jnp.concatenate([t] * (LANES // HD), axis=1)
    return rep(c64), rep(s1), rep(s2)


def _local_step(x, loss_target, positions, ada, w_qkv, w_f, w_out, w_up, conv_w8, w_down,
                b_fgate, gn, ln1_g, ln1_b, conv_b, ln2_g, ln2_b):
    T = x.shape[0]
    nbl = T // S
    nha = WA // HD
    sv = lambda k: ada[:, k:k + 1, :]
    sh_a, sc_a, g_a, sh_f, sc_f, g_f = (sv(k) for k in range(6))
    rope = _rope_tables(positions)
    neg_rope = (rope[0], -rope[1], -rope[2])
    gseg = jnp.asarray(np.kron(np.eye(min(256, 2 * WA) // HD), np.ones((HD, HD))), BF16)
    bias = jnp.asarray(_dil_bias(TQ))
    bf_pad = jnp.zeros((1, FPAD), F32).at[:, :nha].set(b_fgate)

    qkv, fa = _mod_mm("qkv_proj", x, sc_a, sh_a, (w_qkv, w_f), (BF16, F32), rope=rope, rope_secs=(3, 4))
    pq, pk, oq, ok, sq, sk = _fold_tables(nha)

    def fold_out(cum, f, b_ref, pq_ref, pk_ref, oq_ref, ok_ref):
        hi, mid, lo = _split3(cum)
        eqv = _dot(hi, pq_ref[0]) + _dot(mid, pq_ref[1]) + _dot(lo, pq_ref[2]) + oq_ref[...]
        ekv = ok_ref[...] - (_dot(hi, pk_ref[0]) + _dot(mid, pk_ref[1]) + _dot(lo, pk_ref[2]))
        return eqv, ekv

    eq, ek = _cumsum_seq(
        "fgate_fwd", [fa], [bf_pad, jnp.asarray(pq, BF16), jnp.asarray(pk, BF16), jnp.asarray(oq), jnp.asarray(ok)],
        lambda f, b_ref, *_: _log_sigmoid(f + b_ref[...]), fold_out, ((2 * WA, BF16), (WA, BF16)), reverse=False)
    oa, lse_a = _attn_fwd("fox_fwd", qkv, (0, 1, 2), True, eq=eq, ek=ek)
    ob, lse_b = _attn_fwd("dil_fwd", qkv, (3, 4, 5), False, bias=bias)

    def mix_fn(i, oav, obv, xv, gav, gn_ref, g_ref, wo_ref, l1g_ref, l1b_ref):
        o = jnp.concatenate([oav, obv], axis=1)
        rs = lax.rsqrt(_head_mean(o * o, g_ref) + RMS_EPS)
        merged = (o * rs * gn_ref[...]).astype(BF16)
        mix = _dot(merged, wo_ref[...])
        x1, _, _ = _ln_fwd(ALPHA * xv + gav * mix, l1g_ref[...], l1b_ref[...])
        return merged, mix, x1

    merged, mix, x1 = _rowwise("mix_out", mix_fn, T, 256, tiles=(oa, ob, x), seqvecs=(g_a,),
                               consts=(gn, gseg, w_out, ln1_g, ln1_b),
                               outs=((2 * WA, BF16), (D, F32), (D, F32)))
    u = _mod_mm("ffn_up", x1, sc_f, sh_f, (w_up,), (F32,))[0]

    def conv_y(i, uv, prev, cw_ref, cb_ref, tm):
        first = (i * tm) % S == 0
        s1, s2 = _conv_taps(uv, prev, first)
        y = cb_ref[...] + cw_ref[0:1, :] * s2 + cw_ref[1:2, :] * s1 + cw_ref[2:3, :] * uv
        return y, s1, s2

    tmc = 128

    def gate_fn(i, uv, prev, cw_ref, cb_ref):
        y, _, _ = conv_y(i, uv, prev, cw_ref, cb_ref, tmc)
        a, g = y[:, :DFF], y[:, DFF:]
        return g * _sigmoid(g) * a

    act = _rowwise("conv_gate", gate_fn, T, tmc, tiles=(u,), halos=((u, -1),), consts=(conv_w8, conv_b),
                   outs=((DFF, BF16),))[0]

    def down_fn(i, actv, x1v, tgt, gfv, wd_ref, g2_ref, b2_ref):
        ffn = _dot(actv, wd_ref[...])
        y, n2, rstd = _ln_fwd(ALPHA * x1v + gfv * ffn, g2_ref[...], b2_ref[...])
        err = y - tgt
        dy = err * (1.0 / D)
        dr2 = _ln_bwd(dy, n2, rstd, g2_ref[...])
        return (dr2, gfv * dr2, _rsum8(err * err), _rsum8(dy * n2), _rsum8(dy), _rsum8(dr2 * ffn))

    dr2, dffn, loss_acc, d_ln2g, d_ln2b, d_gf = _rowwise(
        "ffn_down_loss", down_fn, T, 256, tiles=(act, x1, loss_target), seqvecs=(g_f,),
        consts=(w_down, ln2_g, ln2_b), outs=((D, F32), (D, F32)), accs=(D, D, D), seqaccs=(D,))

    def gate_conv_bwd_fn(i, uv, dfv, prev, u_nxt, df_nxt, cw_ref, cb_ref, wd_ref):
        last = ((i + 1) * tmc) % S == 0
        u_ext = jnp.concatenate([uv, u_nxt], axis=0)
        df_ext = jnp.concatenate([dfv, df_nxt], axis=0).astype(BF16)
        ch = _pick(DFF, (256, 128))
        dav = jnp.concatenate([_dot_nt(df_ext, wd_ref[c * ch:(c + 1) * ch, :]) for c in range(DFF // ch)], axis=1)
        y, s1, s2 = conv_y(i, u_ext, prev, cw_ref, cb_ref, tmc)
        a, g = y[:, :DFF], y[:, DFF:]
        sg = _sigmoid(g)
        dyc_ext = jnp.concatenate([dav * (g * sg), dav * a * (sg * (1.0 + g * (1.0 - sg)))], axis=1)
        dyc = dyc_ext[:tmc]
        u1, u2 = _conv_taps_up(dyc, dyc_ext[tmc:], last)
        du_ = cw_ref[2:3, :] * dyc + cw_ref[1:2, :] * u1 + cw_ref[0:1, :] * u2
        return du_, _rsum8(dyc), _rsum8(dyc * s2[:tmc]), _rsum8(dyc * s1[:tmc]), _rsum8(dyc * uv)

    du, d_cb, d_cw0, d_cw1, d_cw2 = _rowwise(
        "gate_conv_bwd", gate_conv_bwd_fn, T, tmc, tiles=(u, dffn), halos=((u, -1), (u, 1), (dffn, 1)),
        consts=(conv_w8, conv_b, w_down), outs=((2 * DFF, BF16),), accs=(2 * DFF,) * 4)
    dh2 = _mm_nt("dh2", du, w_up)
    g_w_down = _mm_tn("dw_down", act, dffn)
    g_w_up = _mm_tn("dw_up", x1, du, mod=(sc_f, sh_f), by_chip=True)

    def ln1_bwd_fn(i, dr2v, dh2v, xv, mixv, x1v, scfv, gav, l1g_ref):
        dx1 = ALPHA * dr2v + dh2v * (1.0 + scfv)
        _, n1, rstd = _ln_fwd(ALPHA * xv + gav * mixv, l1g_ref[...], 0.0)
        dr1 = _ln_bwd(dx1, n1, rstd, l1g_ref[...])
        return (dr1, gav * dr1, _rsum8(dx1 * n1), _rsum8(dx1),
                _rsum8(dh2v * x1v), _rsum8(dh2v), _rsum8(dr1 * mixv))

    dr1, dmix, d_ln1g, d_ln1b, d_scf, d_shf, d_ga = _rowwise(
        "ln1_bwd", ln1_bwd_fn, T, 256, tiles=(dr2, dh2, x, mix, x1), seqvecs=(sc_f, g_a), consts=(ln1_g,),
        outs=((D, F32), (D, BF16)), accs=(D, D), seqaccs=(D, D, D))

    dmerged = _mm_nt("dmerged", dmix, w_out)
    g_w_out = _mm_tn("dw_out", merged, dmix)

    def hn_bwd_fn(i, dmv, oav, obv, gn_ref, g_ref):
        o = jnp.concatenate([oav, obv], axis=1)
        rs = lax.rsqrt(_head_mean(o * o, g_ref) + RMS_EPS)
        nrm = o * rs
        dn = dmv * gn_ref[...]
        do = rs * (dn - nrm * _head_mean(dn * nrm, g_ref))
        return do, _rsum8(dmv * nrm)

    do, d_gn = _rowwise("headnorm_bwd", hn_bwd_fn, T, 256, tiles=(dmerged, oa, ob), consts=(gn, gseg),
                        outs=((2 * WA, F32),), accs=(2 * WA,))
    dqa, dka, dva, dqe, dek = _attn_bwd("fox_bwd", qkv, (0, 1, 2), oa, do, 0, lse_a, True, eq=eq, ek=ek)
    dqb, dkb, dvb = _attn_bwd("dil_bwd", qkv, (3, 4, 5), ob, do, 1, lse_b, False, bias=bias)
    hdot = lambda a, m_ref: sum(_dot(piece, m_ref[...]) for piece in _split3(a))
    dfa, d_bf = _cumsum_seq(
        "fgate_bwd", [dqe, dek, fa], [bf_pad, jnp.asarray(sq, BF16), jnp.asarray(sk, BF16)],
        lambda dq_, dk_, f, b_ref, sq_ref, sk_ref: hdot(dq_, sq_ref) - hdot(dk_, sk_ref),
        lambda cum, dq_, dk_, f, b_ref, sq_ref, sk_ref: (cum * _sigmoid(-(f + b_ref[...])),),
        ((FPAD, F32),), reverse=True, n_acc=1)

    def dz_fn(i, a0, a1, a2, b0, b1, b2, fv, cv, s1v, s2v):
        ct, s1t, s2t = (_tile_lanes(t, WA) for t in (cv, s1v, s2v))
        return jnp.concatenate([a0, a1, a2, _rope(b0, ct, s1t, s2t), _rope(b1, ct, s1t, s2t), b2, fv], axis=1)

    dz = _rowwise("dz_pack", dz_fn, T, 256, tiles=(dqa, dka, dva, dqb, dkb, dvb, dfa) + neg_rope,
                  outs=((6 * WA + FPAD, BF16),))[0]
    w_cat = jnp.concatenate([w_qkv, w_f], axis=1)
    dh1 = _mm_nt("dh1", dz, w_cat)
    g_w_cat = _mm_tn("dw_in", x, dz, mod=(sc_a, sh_a))

    def dx_fn(i, dr1v, dh1v, xv, scav):
        return ALPHA * dr1v + dh1v * (1.0 + scav), _rsum8(dh1v * xv), _rsum8(dh1v)

    grad_x, d_sca, d_sha = _rowwise("dx_out", dx_fn, T, 256, tiles=(dr1, dh1, x), seqvecs=(sc_a,),
                                    outs=((D, F32),), seqaccs=(D, D))

    row0 = lambda a: a[..., 0, :]
    d_ada = jnp.stack([row0(d_sha), row0(d_sca), row0(d_ga), row0(d_shf), row0(d_scf), row0(d_gf)], axis=1)
    d_cw = jnp.stack([row0(d_cw0), row0(d_cw1), row0(d_cw2)], axis=0)
    loss_part = (0.5 / D) * jnp.sum(loss_acc[0])
    small = dict(b_fgate=row0(d_bf)[:nha], gn=row0(d_gn), ln1_g=row0(d_ln1g), ln1_b=row0(d_ln1b),
                 conv_b=row0(d_cb), ln2_g=row0(d_ln2g), ln2_b=row0(d_ln2b))
    big = dict(w_cat=g_w_cat, w_out=g_w_out, w_up=g_w_up, conv_w=d_cw, w_down=g_w_down)
    return loss_part, grad_x, d_ada, small, big


def _rows_of(n, w=None):
    return -(-n // (D if w is None else w))


def _as_rows(v):
    w = D
    k = v.shape[0]
    flat = v.reshape(k, -1)
    rows = _rows_of(_rows_of(flat.shape[1], w), SUBLANES) * SUBLANES
    flat = jnp.pad(flat, ((0, 0), (0, rows * w - flat.shape[1])))
    return flat.reshape(k, rows, w)


def kernel(x, c, positions, w_ada, b_ada, w_in, b_fgate, gn_a, gn_b, w_out, ln1_g, ln1_b, w_up, conv_w, conv_b, w_down, ln2_g, ln2_b, loss_target, m_w_ada, m_b_ada, m_w_in, m_b_fgate, m_gn_a, m_gn_b, m_w_out, m_ln1_g, m_ln1_b, m_w_up, m_conv_w, m_conv_b, m_w_down, m_ln2_g, m_ln2_b, v_w_ada, v_b_ada, v_w_in, v_b_fgate, v_gn_a, v_gn_b, v_w_out, v_ln1_g, v_ln1_b, v_w_up, v_conv_w, v_conv_b, v_w_down, v_ln2_g, v_ln2_b):
    mx, my, mc = lax.axis_index("x"), lax.axis_index("y"), lax.axis_index("c")
    dev = _dev_index(mx, my, mc)
    chip = _chip_index(mx, my, mc)
    nbl = x.shape[0]
    T = nbl * S
    nha = WA // HD
    d_in = w_in.shape[2] * NCHIP
    n_ada = w_ada.shape[2]

    c_pad = jnp.zeros((SUBLANES, D), F32).at[:nbl].set(c)
    c_all = _all_gather8("gather_c", c_pad)[:, :nbl].reshape(NDEV * nbl, D)
    ada_part = _ada_fwd(c_all, w_ada[0])
    n_cw = conv_w.shape[2]
    cw_rows = jnp.pad(conv_w[0], ((0, SUBLANES - conv_w.shape[1]), (0, n_ada - n_cw)))
    ada_blocks = _all_gather8("gather_ada", jnp.concatenate([ada_part, cw_rows], axis=0))
    n_c = NDEV * nbl
    ada_all = jnp.concatenate([ada_blocks[2 * k, :n_c] for k in range(NCHIP)], axis=1) + b_ada
    conv_w8 = jnp.concatenate([ada_blocks[2 * k, n_c:, :n_cw] for k in range(NCHIP)], axis=1)
    ada = lax.dynamic_slice_in_dim(ada_all, dev * nbl, nbl, axis=0).reshape(nbl, 6, D)

    w_in_sh = jnp.pad(w_in[0].astype(BF16), ((0, 0), (0, _rows_of(w_in.shape[2], LANES) * LANES - w_in.shape[2])))
    shards = [w_in_sh, w_out[0].astype(BF16), w_up[0].astype(BF16), w_down[0].astype(BF16)]
    halves = [t.reshape(2, t.shape[0] // 2, t.shape[1]) for t in shards]
    gathered_w = _gather_halves("gather_w", halves)
    g_in, g_out, g_up, g_down = (_by_chip(h, g, chip).reshape((NCHIP,) + t.shape)
                                 for g, h, t in zip(gathered_w, halves, shards))
    w_in_full = jnp.concatenate([g_in[k][:, :w_in.shape[2]] for k in range(NCHIP)], axis=1)
    w_qkv = jnp.concatenate([w_in_full[:, :3 * WA], w_in_full[:, 3 * WA + nha:]], axis=1)
    w_f = jnp.pad(w_in_full[:, 3 * WA:3 * WA + nha], ((0, 0), (0, FPAD - nha)))
    w_out_full = g_out.reshape(NCHIP * w_out.shape[1], D)
    w_up_full = jnp.concatenate([g_up[k] for k in range(NCHIP)], axis=1)
    w_down_full = g_down.reshape(NCHIP * w_down.shape[1], D)

    gn = jnp.concatenate([gn_a, gn_b], axis=1)
    loss_part, grad_x, d_ada, small, big = _local_step(
        x.reshape(T, D), loss_target.reshape(T, D), positions, ada, w_qkv, w_f, w_out_full, w_up_full, conv_w8,
        w_down_full, b_fgate, gn, ln1_g, ln1_b, conv_b, ln2_g, ln2_b)

    def row_pad(v, rows):
        flat = v.reshape(-1)
        return jnp.pad(flat, (0, rows * D - flat.shape[0]))

    n_cb = _rows_of(2 * DFF)
    small_flat = jnp.concatenate([
        row_pad(small["b_fgate"], 1), row_pad(small["gn"], 1), row_pad(small["ln1_g"], 1),
        row_pad(small["ln1_b"], 1), row_pad(small["ln2_g"], 1), row_pad(small["ln2_b"], 1),
        row_pad(jnp.full((1,), loss_part, F32), 1), row_pad(small["conv_b"], n_cb)])
    n_small = _rows_of(small_flat.shape[0], SUBLANES * D) * SUBLANES
    small_rows = jnp.pad(small_flat, (0, n_small * D - small_flat.shape[0])).reshape(n_small, D)
    ada_rows = jnp.pad(d_ada.reshape(nbl, 6, D), ((0, 0), (0, SUBLANES - 6), (0, 0))).reshape(nbl * SUBLANES, D)
    gathered = _all_gather8("gather_small", jnp.concatenate([small_rows, ada_rows], axis=0))
    red = _sum_leading("sum_small", gathered, tm=SUBLANES)
    g_b_fgate = red[0:1, :nha]
    g_gn = red[1:2, :2 * WA]
    g_ln1_g, g_ln1_b, g_ln2_g, g_ln2_b = red[2:3], red[3:4], red[4:5], red[5:6]
    loss = red[6, 0]
    g_conv_b = red[7:7 + n_cb].reshape(1, -1)[:, :2 * DFF]
    g_b_ada = _add2("sum_b_ada", red[n_small:n_small + SUBLANES], red[n_small + SUBLANES:n_small + 2 * SUBLANES],
                    tm=SUBLANES)[:6].reshape(1, 6 * D)
    dada_all = gathered[:, n_small:].reshape(NDEV, nbl, SUBLANES, D)[:, :, :6].reshape(NDEV * nbl, 6 * D)
    g_w_ada = _ada_bwd(c_all, lax.dynamic_slice_in_dim(dada_all, chip * n_ada, n_ada, axis=1))

    g_cat = big["w_cat"]
    g_w_in_full = jnp.concatenate([g_cat[:, :3 * WA], g_cat[:, 6 * WA:6 * WA + nha], g_cat[:, 3 * WA:6 * WA]], axis=1)
    sh_in = g_w_in_full.reshape(D, NCHIP, -1).transpose(1, 0, 2)
    sh_out = big["w_out"].reshape(NCHIP, -1, D)
    sh_up = big["w_up"]
    sh_cw = big["conv_w"].reshape(conv_w.shape[1], NCHIP, -1).transpose(1, 0, 2)
    sh_down = big["w_down"].reshape(NCHIP, -1, D)
    parts = [_as_rows(t) for t in (sh_in, sh_out, sh_up, sh_cw, sh_down)]
    part_rows = [p.shape[1] for p in parts]
    n_rows = _rows_of(sum(part_rows), 2 * LANES) * 2 * LANES
    half = n_rows // 2
    if n_rows > sum(part_rows):
        parts.append(jnp.zeros((NCHIP, n_rows - sum(part_rows), D), F32))
    halves = jnp.concatenate(parts, axis=1).reshape(NCHIP, 2, half, D)
    mine = lax.dynamic_index_in_dim(halves, mc, axis=1, keepdims=False).reshape(NCHIP * half, D)
    theirs = lax.dynamic_index_in_dim(halves, 1 - mc, axis=1, keepdims=False).reshape(NCHIP * half, D)
    from_sib = _to_sibling("pair_swap", theirs)
    pair_sum = _add2("pair_sum", mine, from_sib, out_dtype=BF16).reshape(NCHIP, half, D)
    by_chip = _scatter_chips("scatter_grads", pair_sum)
    own = lax.dynamic_index_in_dim(pair_sum, chip, axis=0, keepdims=False)
    my_half = _sum_leading("chip_sum", _by_chip(own, by_chip, chip))
    sib_half = _to_sibling("pair_share", my_half)
    pair = jnp.stack([my_half, sib_half])
    shard = jnp.concatenate([lax.dynamic_index_in_dim(pair, mc, axis=0, keepdims=False),
                             lax.dynamic_index_in_dim(pair, 1 - mc, axis=0, keepdims=False)], axis=0)

    def unpack(k, shape):
        start = sum(part_rows[:k])
        n = int(np.prod(shape))
        return shard[start:start + part_rows[k]].reshape(-1)[:n].reshape(shape)

    g_w_in = unpack(0, w_in.shape[1:])
    g_w_out = unpack(1, w_out.shape[1:])
    g_w_up = unpack(2, w_up.shape[1:])
    g_conv_w = unpack(3, conv_w.shape[1:])
    g_w_down = unpack(4, w_down.shape[1:])

    grads = dict(w_ada=g_w_ada, b_ada=g_b_ada, w_in=g_w_in, b_fgate=g_b_fgate, gn_a=g_gn[:, :WA], gn_b=g_gn[:, WA:],
                 w_out=g_w_out, ln1_g=g_ln1_g, ln1_b=g_ln1_b, w_up=g_w_up, conv_w=g_conv_w, conv_b=g_conv_b,
                 w_down=g_w_down, ln2_g=g_ln2_g, ln2_b=g_ln2_b)
    weights = dict(w_ada=w_ada, b_ada=b_ada, w_in=w_in, b_fgate=b_fgate, gn_a=gn_a, gn_b=gn_b, w_out=w_out,
                   ln1_g=ln1_g, ln1_b=ln1_b, w_up=w_up, conv_w=conv_w, conv_b=conv_b, w_down=w_down,
                   ln2_g=ln2_g, ln2_b=ln2_b)
    ms = dict(w_ada=m_w_ada, b_ada=m_b_ada, w_in=m_w_in, b_fgate=m_b_fgate, gn_a=m_gn_a, gn_b=m_gn_b,
              w_out=m_w_out, ln1_g=m_ln1_g, ln1_b=m_ln1_b, w_up=m_w_up, conv_w=m_conv_w, conv_b=m_conv_b,
              w_down=m_w_down, ln2_g=m_ln2_g, ln2_b=m_ln2_b)
    vs = dict(w_ada=v_w_ada, b_ada=v_b_ada, w_in=v_w_in, b_fgate=v_b_fgate, gn_a=v_gn_a, gn_b=v_gn_b,
              w_out=v_w_out, ln1_g=v_ln1_g, ln1_b=v_ln1_b, w_up=v_w_up, conv_w=v_conv_w, conv_b=v_conv_b,
              w_down=v_w_down, ln2_g=v_ln2_g, ln2_b=v_ln2_b)
    names = list(weights)
    big_names = ("w_ada", "w_in", "w_out", "w_up", "w_down")
    delta, new_m, new_v = {}, {}, {}
    for n in big_names:
        shp = weights[n].shape
        d, m2, v2 = _adamw("adamw_" + n, weights[n][0], grads[n].reshape(shp[1:]), ms[n][0], vs[n][0])
        delta[n], new_m[n], new_v[n] = d.reshape(shp), m2.reshape(shp), v2.reshape(shp)
    small_names = [n for n in names if n not in big_names]

    def pack_small(src):
        flats = []
        for n in small_names:
            flat = src[n].reshape(-1)
            flats.append(jnp.pad(flat, (0, _rows_of(flat.shape[0]) * D - flat.shape[0])))
        allf = jnp.concatenate(flats)
        rows = _rows_of(allf.shape[0], SUBLANES * D) * SUBLANES
        return jnp.pad(allf, (0, rows * D - allf.shape[0])).reshape(rows, D)

    sd, sm, sv_ = _adamw("adamw_small", pack_small(weights), pack_small(grads), pack_small(ms), pack_small(vs))
    off = 0
    for n in small_names:
        shp = weights[n].shape
        cnt = int(np.prod(shp))
        r = _rows_of(cnt)
        for dst, src in ((delta, sd), (new_m, sm), (new_v, sv_)):
            dst[n] = src[off:off + r].reshape(-1)[:cnt].reshape(shp)
        off += r

    out_g = {n: grads[n].reshape(weights[n].shape) for n in names}
    return (loss, grad_x.reshape(x.shape), *[out_g[n] for n in names], *[delta[n] for n in names],
            *[new_m[n] for n in names], *[new_v[n] for n in names])
```

```python
import functools
import math

import numpy as np
import jax
import jax.numpy as jnp
from jax import lax
from jax.experimental import pallas as pl
from jax.experimental.pallas import tpu as pltpu

F32 = jnp.float32
BF16 = jnp.bfloat16

D = 1024
S = 4096
HD = 64
WA = 512
DFF = 2816
NCHIP = 4
NDEV = 8
PATTERNS = ((128, 1), (512, 4), (2048, 16))
ROPE_THETA = 500000.0
ROPE_DIMS = HD // 4
ALPHA = (2.0 * 1) ** 0.25
LN_EPS = 1e-5
RMS_EPS = 1e-6
ADAM_LR = 0.001
ADAM_B1 = 0.9
ADAM_B2 = 0.999
ADAM_EPS = 1e-08
ADAM_WD = 0.01
ADAM_STEP = 10

LANES = 128
SUBLANES = 8
TQ = 512
FPAD = LANES
NEG = -1e30
VMEM_LIMIT = 56 * 1024 * 1024
MESH = pl.DeviceIdType.MESH


def _cparams(sem):
    return pltpu.CompilerParams(dimension_semantics=sem, vmem_limit_bytes=VMEM_LIMIT)


def _pick(n, cands):
    for c in cands:
        if n % c == 0:
            return c
    return n


def _rsum8(v):
    tm, w = v.shape
    return jnp.sum(v.reshape(tm // SUBLANES, SUBLANES, w), axis=0)


def _sigmoid(x):
    return 1.0 / (1.0 + jnp.exp(-x))


def _dot(a, b):
    return jnp.dot(a, b, preferred_element_type=F32)


def _dot_nt(a, b):
    return lax.dot_general(a, b, (((1,), (1,)), ((), ())), preferred_element_type=F32)


def _dot_tn(a, b):
    return lax.dot_general(a, b, (((0,), (0,)), ((), ())), preferred_element_type=F32)


def _rowwise(name, fn, T, tm, *, tiles=(), halos=(), seqvecs=(), consts=(), outs=(), accs=(), seqaccs=(),
             seq_len=None):
    seq_len = S if seq_len is None else seq_len
    nb = T // tm
    spb = max(seq_len // tm, 1)
    nseq = max(T // seq_len, 1)
    n8 = T // SUBLANES
    r8 = tm // SUBLANES
    in_specs, args = [], []
    for a in tiles:
        in_specs.append(pl.BlockSpec((tm, a.shape[1]), lambda i: (i, 0)))
        args.append(a)
    for a, direction in halos:
        if direction < 0:
            idx = lambda i: (jnp.maximum(i * r8 - 1, 0), 0)
        else:
            idx = lambda i: (jnp.minimum((i + 1) * r8, n8 - 1), 0)
        in_specs.append(pl.BlockSpec((SUBLANES, a.shape[1]), idx))
        args.append(a)
    for a in seqvecs:
        in_specs.append(pl.BlockSpec((1, 1, a.shape[2]), lambda i: (i // spb, 0, 0)))
        args.append(a)
    for a in consts:
        in_specs.append(pl.BlockSpec(a.shape, lambda i, nd=a.ndim: (0,) * nd))
        args.append(a)
    out_shape, out_specs = [], []
    for w, dt in outs:
        out_shape.append(jax.ShapeDtypeStruct((T, w), dt))
        out_specs.append(pl.BlockSpec((tm, w), lambda i: (i, 0)))
    for w in accs:
        out_shape.append(jax.ShapeDtypeStruct((SUBLANES, w), F32))
        out_specs.append(pl.BlockSpec((SUBLANES, w), lambda i: (0, 0)))
    for w in seqaccs:
        out_shape.append(jax.ShapeDtypeStruct((nseq, SUBLANES, w), F32))
        out_specs.append(pl.BlockSpec((1, SUBLANES, w), lambda i: (i // spb, 0, 0)))
    n_t, n_h, n_s, n_c = len(tiles), len(halos), len(seqvecs), len(consts)
    n_o, n_a, n_sa = len(outs), len(accs), len(seqaccs)

    def body(*refs):
        i = pl.program_id(0)
        ins = refs[:n_t + n_h + n_s + n_c]
        orefs = refs[n_t + n_h + n_s + n_c:]
        vals = [r[...] for r in ins[:n_t + n_h]]
        vals += [r[0] for r in ins[n_t + n_h:n_t + n_h + n_s]]
        vals += list(ins[n_t + n_h + n_s:])
        res = fn(i, *vals)
        if not isinstance(res, (tuple, list)):
            res = (res,)
        for k in range(n_o):
            orefs[k][...] = res[k].astype(orefs[k].dtype)
        for k in range(n_a):
            r = orefs[n_o + k]

            @pl.when(i == 0)
            def _():
                r[...] = jnp.zeros_like(r)

            r[...] += res[n_o + k]

            @pl.when(i == nb - 1)
            def _():
                r[...] = jnp.broadcast_to(jnp.sum(r[...], axis=0, keepdims=True), r.shape)
        for k in range(n_sa):
            r = orefs[n_o + n_a + k]

            @pl.when(i % spb == 0)
            def _():
                r[...] = jnp.zeros_like(r)

            r[0] += res[n_o + n_a + k]

            @pl.when(i % spb == spb - 1)
            def _():
                r[0] = jnp.broadcast_to(jnp.sum(r[0], axis=0, keepdims=True), r.shape[1:])

    sem = ("arbitrary",) if (n_a or n_sa) else ("parallel",)
    res = pl.pallas_call(
        body, name=name, grid=(nb,), in_specs=in_specs, out_specs=out_specs, out_shape=out_shape,
        compiler_params=_cparams(sem),
    )(*args)
    return res


def _ln_fwd(r, g, b):
    mu = jnp.mean(r, axis=-1, keepdims=True)
    xc = r - mu
    var = jnp.mean(xc * xc, axis=-1, keepdims=True)
    rstd = lax.rsqrt(var + LN_EPS)
    n = xc * rstd
    return n * g + b, n, rstd


def _ln_bwd(dy, n, rstd, g):
    dn = dy * g
    return rstd * (dn - jnp.mean(dn, axis=-1, keepdims=True) - n * jnp.mean(dn * n, axis=-1, keepdims=True))


def _head_mean(t, g_ref):
    gw = g_ref.shape[0]
    hi = t.astype(BF16)
    lo = (t - hi.astype(F32)).astype(BF16)
    g = g_ref[...]
    parts = []
    for c in range(t.shape[1] // gw):
        sl = slice(c * gw, (c + 1) * gw)
        parts.append(_dot(hi[:, sl], g) + _dot(lo[:, sl], g))
    out = parts[0] if len(parts) == 1 else jnp.concatenate(parts, axis=1)
    return out * (1.0 / HD)


def _rope(z, c, s1, s2):
    w = z.shape[1]
    half = ROPE_DIMS // 2
    return z * c + pltpu.roll(z, half, 1) * s1 + pltpu.roll(z, w - half, 1) * s2


def _tile_lanes(t, w):
    reps = w // t.shape[1]
    return t if reps == 1 else jnp.concatenate([t] * reps, axis=1)


def _conv_taps(ext, prev, first):
    tm = ext.shape[0]
    prev = jnp.where(first, jnp.zeros_like(prev), prev)
    r8 = lax.broadcasted_iota(jnp.int32, (SUBLANES, 1), 0)
    top = ext[0:SUBLANES]
    s1_top = jnp.where(r8 < 1, pltpu.roll(prev, 1, 0), pltpu.roll(top, 1, 0))
    s2_top = jnp.where(r8 < 2, pltpu.roll(prev, 2, 0), pltpu.roll(top, 2, 0))
    s1 = jnp.concatenate([s1_top, pltpu.roll(ext, 1, 0)[SUBLANES:]], axis=0)
    s2 = jnp.concatenate([s2_top, pltpu.roll(ext, 2, 0)[SUBLANES:]], axis=0)
    return s1, s2


def _conv_taps_up(ext, nxt, last):
    tm = ext.shape[0]
    nxt = jnp.where(last, jnp.zeros_like(nxt), nxt)
    r8 = lax.broadcasted_iota(jnp.int32, (SUBLANES, 1), 0)
    bot = ext[tm - SUBLANES:tm]
    u1_bot = jnp.where(r8 >= 7, pltpu.roll(nxt, 7, 0), pltpu.roll(bot, 7, 0))
    u2_bot = jnp.where(r8 >= 6, pltpu.roll(nxt, 6, 0), pltpu.roll(bot, 6, 0))
    u1 = jnp.concatenate([pltpu.roll(ext, tm - 1, 0)[:tm - SUBLANES], u1_bot], axis=0)
    u2 = jnp.concatenate([pltpu.roll(ext, tm - 2, 0)[:tm - SUBLANES], u2_bot], axis=0)
    return u1, u2


def _mm_nt(name, a, w, tm=256):
    T = a.shape[0]
    n = w.shape[0]
    ch = _pick(n, (512, 256, 128))

    def fn(i, av, w_ref):
        ab = av.astype(BF16)
        parts = [_dot_nt(ab, w_ref[c * ch:(c + 1) * ch, :]) for c in range(n // ch)]
        return parts[0] if len(parts) == 1 else jnp.concatenate(parts, axis=1)

    return _rowwise(name, fn, T, tm, tiles=(a,), consts=(w,), outs=((n, F32),))[0]


def _mm_tn(name, a, b, *, mod=None, tt=512, t2=None, by_chip=False):
    T, k1 = a.shape
    k2 = b.shape[1]
    t1 = k1 if k1 <= 1536 else _pick(k1, (1408, 1024, 512, 256, 128))
    if t2 is None:
        t2 = k2 if k2 <= 1536 else _pick(k2, (1408, 1024, 640, 512, 256, 128))
    wc = k2 // NCHIP
    if by_chip:
        t2 = 2 * wc
    tt = min(tt, S)
    spb = S // tt

    def body(*refs):
        if mod is not None:
            a_ref, sc_ref, sh_ref, b_ref, o_ref = refs
        else:
            a_ref, b_ref, o_ref = refs
        t = pl.program_id(2)

        @pl.when(t == 0)
        def _():
            o_ref[...] = jnp.zeros_like(o_ref)

        av = a_ref[...]
        if mod is not None:
            av = av * (1.0 + sc_ref[0]) + sh_ref[0]
        res = _dot_tn(av.astype(BF16), b_ref[...].astype(BF16))
        if by_chip:
            o_ref[0] += res[:, :wc]
            o_ref[1] += res[:, wc:]
        else:
            o_ref[...] += res

    in_specs = [pl.BlockSpec((tt, t1), lambda p, q, t: (t, p))]
    args = [a]
    if mod is not None:
        for v in mod:
            in_specs.append(pl.BlockSpec((1, 1, t1), lambda p, q, t: (t // spb, 0, p)))
            args.append(v)
    in_specs.append(pl.BlockSpec((tt, t2), lambda p, q, t: (t, q)))
    args.append(b)
    if by_chip:
        out_specs = pl.BlockSpec((2, t1, wc), lambda p, q, t: (q, p, 0))
        out_shape = jax.ShapeDtypeStruct((NCHIP, k1, wc), F32)
    else:
        out_specs = pl.BlockSpec((t1, t2), lambda p, q, t: (p, q))
        out_shape = jax.ShapeDtypeStruct((k1, k2), F32)
    return pl.pallas_call(
        body, name=name, grid=(k1 // t1, k2 // t2, T // tt), in_specs=in_specs, out_specs=out_specs,
        out_shape=out_shape, compiler_params=_cparams(("parallel", "parallel", "arbitrary")),
    )(*args)


def _mod_mm(name, x, sc, sh, ws, out_dtypes, rope=None, rope_secs=(), tm=256):
    T = x.shape[0]
    nw = len(ws)

    def fn(i, xv, *rest):
        if rope is not None:
            cv, s1v, s2v = rest[:3]
            rest = rest[3:]
        scv, shv = rest[:2]
        w_refs = rest[2:]
        h = (xv * (1.0 + scv) + shv).astype(BF16)
        res = []
        for k, w_ref in enumerate(w_refs):
            n = w_ref.shape[1]
            ch = WA if (k == 0 and rope is not None) else _pick(n, (512, 256, 128))
            parts = []
            for c in range(n // ch):
                z = _dot(h, w_ref[:, c * ch:(c + 1) * ch])
                if k == 0 and c in rope_secs:
                    z = _rope(z, _tile_lanes(cv, ch), _tile_lanes(s1v, ch), _tile_lanes(s2v, ch))
                parts.append(z.astype(out_dtypes[k]))
            res.append(parts[0] if len(parts) == 1 else jnp.concatenate(parts, axis=1))
        return tuple(res)

    tiles = (x,) + (tuple(rope) if rope is not None else ())
    outs = tuple((w.shape[1], dt) for w, dt in zip(ws, out_dtypes))
    return _rowwise(name, fn, T, tm, tiles=tiles, seqvecs=(sc, sh), consts=tuple(ws), outs=outs)


def _tri(tb, lower):
    r = lax.broadcasted_iota(jnp.int32, (tb, tb), 0)
    c = lax.broadcasted_iota(jnp.int32, (tb, tb), 1)
    return jnp.where((r >= c) if lower else (r <= c), 1.0, 0.0).astype(BF16)


def _split3(x):
    hi = x.astype(BF16)
    r = x - hi.astype(F32)
    mid = r.astype(BF16)
    return hi, mid, (r - mid.astype(F32)).astype(BF16)


def _cumsum_seq(name, ins, consts, fn_in, fn_out, outs, reverse, n_acc=0, tb=256):
    T = ins[0].shape[0]
    tb = min(tb, S)
    nbs = S // tb
    nseq = T // S
    n_i, n_c, n_o = len(ins), len(consts), len(outs)

    def blk(b, j):
        return (b * nbs + (nbs - 1 - j if reverse else j), 0)

    def body(*refs):
        i_refs, c_refs = refs[:n_i], refs[n_i:n_i + n_c]
        o_refs = refs[n_i + n_c:n_i + n_c + n_o]
        acc_refs = refs[n_i + n_c + n_o:n_i + n_c + n_o + n_acc]
        carry = refs[-1]
        b, j = pl.program_id(0), pl.program_id(1)

        @pl.when(j == 0)
        def _():
            carry[...] = jnp.zeros_like(carry)

        iv = [r[...] for r in i_refs]
        xin = fn_in(*iv, *c_refs)
        tri = _tri(tb, not reverse)
        cum = sum(_dot(tri, piece) for piece in _split3(xin)) + carry[0:1, :]
        carry[...] = carry[...] + jnp.sum(xin, axis=0, keepdims=True)
        res = fn_out(cum, *iv, *c_refs)
        for o, r in zip(o_refs, res):
            o[...] = r.astype(o.dtype)
        for a in acc_refs:
            @pl.when((b == 0) & (j == 0))
            def _():
                a[...] = jnp.zeros_like(a)

            a[...] += _rsum8(res[0])

            @pl.when((b == nseq - 1) & (j == nbs - 1))
            def _():
                a[...] = jnp.broadcast_to(jnp.sum(a[...], axis=0, keepdims=True), a.shape)

    in_specs = [pl.BlockSpec((tb, a.shape[1]), blk) for a in ins]
    in_specs += [pl.BlockSpec(c.shape, lambda b, j, nd=c.ndim: (0,) * nd) for c in consts]
    out_shape = [jax.ShapeDtypeStruct((T, w), dt) for w, dt in outs]
    out_shape += [jax.ShapeDtypeStruct((SUBLANES, outs[0][0]), F32)] * n_acc
    out_specs = [pl.BlockSpec((tb, w), blk) for w, _ in outs]
    out_specs += [pl.BlockSpec((SUBLANES, outs[0][0]), lambda b, j: (0, 0))] * n_acc
    return pl.pallas_call(
        body, name=name, grid=(nseq, nbs), in_specs=in_specs, out_specs=out_specs, out_shape=out_shape,
        scratch_shapes=[pltpu.VMEM((SUBLANES, FPAD), F32)],
        compiler_params=_cparams(("arbitrary", "arbitrary")),
    )(*ins, *consts)


def _log_sigmoid(x):
    return jnp.minimum(x, 0.0) - jnp.log(1.0 + jnp.exp(-jnp.abs(x)))


def _dil_bias(tq):
    max_win = max(w for w, _ in PATTERNS)
    nd = (max_win + tq - 1) // tq + 1
    qi = np.arange(tq)[:, None]
    kj = np.arange(tq)[None, :]
    tabs = []
    for dlt in range(nd):
        dist = dlt * tq + qi - kj
        mult = np.zeros((tq, tq), np.float64)
        for win, dil in PATTERNS:
            mult += (dist >= 0) & (dist % dil == 0) & (dist // dil <= win // dil)
        tabs.append(np.where(mult > 0, np.log(np.maximum(mult, 1.0)), NEG))
    return np.stack(tabs).astype(np.float32)


def _fold_tables(nha):
    hp_n = nha // 2
    pq = np.zeros((3, FPAD, hp_n * 2 * LANES), np.float32)
    pk = np.zeros((3, FPAD, hp_n * LANES), np.float32)
    oq = np.zeros((1, hp_n * 2 * LANES), np.float32)
    ok = np.zeros((1, hp_n * LANES), np.float32)
    sq = np.zeros((hp_n * LANES, FPAD), np.float32)
    sk = np.zeros((hp_n * LANES, FPAD), np.float32)
    for h in range(nha):
        hp, odd = divmod(h, 2)
        qb = hp * 2 * LANES + odd * (LANES + 8)
        kb = hp * LANES + odd * 8
        for i in range(3):
            pq[i, h, qb + i] = 1
            oq[0, qb + 3 + i] = 1
            ok[0, kb + i] = 1
            pk[i, h, kb + 3 + i] = 1
        sq[kb, h] = 1
        sk[kb + 3, h] = 1
    return pq, pk, oq, ok, sq, sk


def _stack_heads(x2, h0, extra=None):
    z = jnp.zeros_like(x2)
    a, b = jnp.where(h0, x2, z), jnp.where(h0, z, x2)
    if extra is not None:
        a = jnp.concatenate([a, extra[:, :LANES]], axis=1)
        b = jnp.concatenate([b, extra[:, LANES:]], axis=1)
    return jnp.concatenate([a, b], axis=0)


def _attn_fwd(name, qkv, secs, fox, eq=None, ek=None, bias=None):
    T = qkv.shape[0]
    nq = S // TQ
    nbl = T // S
    hp_n = WA // LANES
    sq, sk, sv = (s * hp_n for s in secs)
    scale = HD ** -0.5
    nd = None if fox else bias.shape[0]

    def body(*refs):
        if fox:
            q_ref, k_ref, v_ref, eq_ref, ek_ref, o_ref, lse_ref = refs
        else:
            q_ref, k_ref, v_ref, b_ref, o_ref, lse_ref = refs
        i = pl.program_id(2)
        lane = lax.broadcasted_iota(jnp.int32, (1, LANES), 1)
        h0 = lane < HD
        q2 = (q_ref[...].astype(F32) * scale).astype(BF16)
        qs = _stack_heads(q2, h0, eq_ref[...] if fox else None)

        def scores(t, diag):
            off = pl.multiple_of((i - t) * TQ, TQ)
            kk = k_ref[pl.ds(off, TQ), :]
            if fox:
                kk = jnp.concatenate([kk, ek_ref[pl.ds(off, TQ), :]], axis=1)
            s = jnp.concatenate([_dot_nt(qs[:TQ], kk), _dot_nt(qs[TQ:], kk)], axis=0)
            if not fox:
                s = (s.reshape(2, TQ, TQ) + b_ref[t]).reshape(2 * TQ, TQ)
            elif diag:
                rows = lax.broadcasted_iota(jnp.int32, (2, TQ, TQ), 1).reshape(2 * TQ, TQ)
                cols = lax.broadcasted_iota(jnp.int32, (2 * TQ, TQ), 1)
                s = jnp.where(cols <= rows, s, NEG)
            return s

        def update(t, s, m, l, acc):
            off = pl.multiple_of((i - t) * TQ, TQ)
            v2 = v_ref[pl.ds(off, TQ), :]
            m_new = jnp.maximum(m, jnp.max(s, axis=1, keepdims=True))
            p = jnp.exp(s - m_new)
            a = jnp.exp(m - m_new)
            l = a * l + jnp.sum(p, axis=1, keepdims=True)
            pb = p.astype(BF16)
            acc = a * acc + jnp.concatenate([_dot(pb[:TQ], v2), _dot(pb[TQ:], v2)], axis=0)
            return m_new, l, acc

        def step(t, carry):
            s, m, l, acc = carry
            return (scores(t + 1, False),) + update(t, s, m, l, acc)

        init = (jnp.full((2 * TQ, 1), NEG, F32), jnp.zeros((2 * TQ, 1), F32), jnp.zeros((2 * TQ, LANES), F32))
        n = i + 1 if fox else jnp.minimum(i + 1, nd)
        m, l, acc = update(0, scores(0, True), *init)
        m, l, acc = lax.fori_loop(1, n, lambda t, c: update(t, scores(t, False), *c), (m, l, acc))
        on = acc / l
        o_ref[...] = jnp.where(h0, on[:TQ], on[TQ:])
        lse = jnp.broadcast_to(m + jnp.log(l), (2 * TQ, LANES))
        lse_ref[...] = jnp.concatenate([lse[:TQ], lse[TQ:]], axis=1)

    in_specs = [
        pl.BlockSpec((TQ, LANES), lambda b, hp, i: (b * nq + i, sq + hp)),
        pl.BlockSpec((S, LANES), lambda b, hp, i: (b, sk + hp)),
        pl.BlockSpec((S, LANES), lambda b, hp, i: (b, sv + hp)),
    ]
    args = [qkv, qkv, qkv]
    if fox:
        in_specs += [pl.BlockSpec((TQ, 2 * LANES), lambda b, hp, i: (b * nq + i, hp)),
                     pl.BlockSpec((S, LANES), lambda b, hp, i: (b, hp))]
        args += [eq, ek]
    else:
        in_specs.append(pl.BlockSpec(bias.shape, lambda b, hp, i: (0, 0, 0)))
        args.append(bias)
    return pl.pallas_call(
        body, name=name, grid=(nbl, hp_n, nq), in_specs=in_specs,
        out_specs=[pl.BlockSpec((TQ, LANES), lambda b, hp, i: (b * nq + i, hp)),
                   pl.BlockSpec((TQ, 2 * LANES), lambda b, hp, i: (b * nq + i, hp))],
        out_shape=[jax.ShapeDtypeStruct((T, WA), F32), jax.ShapeDtypeStruct((T, 2 * WA), F32)],
        compiler_params=_cparams(("parallel", "parallel", "arbitrary")),
    )(*args)


def _attn_bwd(name, qkv, secs, o, do, do_sec, lse, fox, eq=None, ek=None, bias=None):
    T = qkv.shape[0]
    nq = S // TQ
    nbl = T // S
    hp_n = WA // LANES
    sq, sk, sv = (s * hp_n for s in secs)
    dsec = do_sec * hp_n
    scale = HD ** -0.5
    nd = None if fox else bias.shape[0]
    kc = 2 * LANES if fox else LANES

    def body(*refs):
        if fox:
            (q_ref, k_ref, v_ref, o_ref, do_ref, lse_ref, eq_ref, ek_ref,
             dq_ref, dk_ref, dv_ref, dqe_ref, dek_ref, dl_ref) = refs
        else:
            q_ref, k_ref, v_ref, o_ref, do_ref, lse_ref, b_ref, dq_ref, dk_ref, dv_ref, dl_ref = refs
        j = pl.program_id(2)
        lane = lax.broadcasted_iota(jnp.int32, (1, LANES), 1)
        h0 = lane < HD

        @pl.when(j == 0)
        def _():
            dq_ref[...] = jnp.zeros_like(dq_ref)
            if fox:
                dqe_ref[...] = jnp.zeros_like(dqe_ref)

            def dl_step(r, c):
                off = pl.multiple_of(r * TQ, TQ)
                d2 = do_ref[pl.ds(off, TQ), :] * o_ref[pl.ds(off, TQ), :]
                z2 = jnp.zeros_like(d2)
                dl0 = jnp.sum(jnp.where(h0, d2, z2), axis=1, keepdims=True)
                dl1 = jnp.sum(jnp.where(h0, z2, d2), axis=1, keepdims=True)
                dl_ref[pl.ds(off, TQ), :] = jnp.concatenate(
                    [jnp.broadcast_to(dl0, (TQ, LANES)), jnp.broadcast_to(dl1, (TQ, LANES))], axis=1)
                return c

            lax.fori_loop(0, nq, dl_step, 0)

        kk = k_ref[...]
        if fox:
            kk = jnp.concatenate([kk, ek_ref[...]], axis=1)
        v2 = v_ref[...]

        def wide(x2):
            st = jnp.concatenate([x2[:, :LANES], x2[:, LANES:]], axis=0)
            return st if TQ == LANES else jnp.concatenate([st] * (TQ // LANES), axis=1)

        def step(t, carry, diag):
            dkk, dv2 = carry
            off = pl.multiple_of((j + t) * TQ, TQ)
            q2 = (q_ref[pl.ds(off, TQ), :].astype(F32) * scale).astype(BF16)
            qs = _stack_heads(q2, h0, eq_ref[pl.ds(off, TQ), :] if fox else None)
            dos = _stack_heads(do_ref[pl.ds(off, TQ), :].astype(BF16), h0)
            s = jnp.concatenate([_dot_nt(qs[:TQ], kk), _dot_nt(qs[TQ:], kk)], axis=0)
            if not fox:
                s = (s.reshape(2, TQ, TQ) + b_ref[t]).reshape(2 * TQ, TQ)
            elif diag:
                rows = lax.broadcasted_iota(jnp.int32, (2, TQ, TQ), 1).reshape(2 * TQ, TQ)
                cols = lax.broadcasted_iota(jnp.int32, (2 * TQ, TQ), 1)
                s = jnp.where(cols <= rows, s, NEG)
            p = jnp.exp(s - wide(lse_ref[pl.ds(off, TQ), :]))
            dp = jnp.concatenate([_dot_nt(dos[:TQ], v2), _dot_nt(dos[TQ:], v2)], axis=0)
            dsb = (p * (dp - wide(dl_ref[pl.ds(off, TQ), :]))).astype(BF16)
            dv2 = dv2 + _dot_tn(p.astype(BF16), dos)
            dkk = dkk + _dot_tn(dsb, qs)
            dqq = jnp.concatenate([_dot(dsb[:TQ], kk), _dot(dsb[TQ:], kk)], axis=0)
            dq_ref[pl.ds(off, TQ), :] += jnp.where(h0, dqq[:TQ, :LANES], dqq[TQ:, :LANES])
            if fox:
                dqe_ref[pl.ds(off, TQ), :] += jnp.where(lane < SUBLANES, dqq[:TQ, LANES:], dqq[TQ:, LANES:])
            return dkk, dv2

        zero = (jnp.zeros((TQ, kc), F32), jnp.zeros((TQ, LANES), F32))
        if fox:
            dkk, dv2 = lax.fori_loop(1, nq - j, lambda t, c: step(t, c, False), step(0, zero, True))
        else:
            dkk, dv2 = lax.fori_loop(0, jnp.minimum(nq - j, nd), lambda t, c: step(t, c, False), zero)
        dk_ref[...] = dkk[:, :LANES]
        dv_ref[...] = dv2
        if fox:
            dek_ref[...] = dkk[:, LANES:]

        @pl.when(j == nq - 1)
        def _():
            dq_ref[...] = dq_ref[...] * scale

    seq = lambda c, w=LANES: pl.BlockSpec((S, w), lambda b, hp, j: (b, c + hp))
    blk = lambda c: pl.BlockSpec((TQ, LANES), lambda b, hp, j: (b * nq + j, c + hp))
    in_specs = [seq(sq), blk(sk), blk(sv), seq(0), seq(dsec), seq(0, 2 * LANES)]
    args = [qkv, qkv, qkv, o, do, lse]
    if fox:
        in_specs += [seq(0, 2 * LANES), blk(0)]
        args += [eq, ek]
    else:
        in_specs.append(pl.BlockSpec(bias.shape, lambda b, hp, j: (0, 0, 0)))
        args.append(bias)
    out_specs = [seq(0), blk(0), blk(0)]
    out_shape = [jax.ShapeDtypeStruct((T, WA), F32)] * 3
    if fox:
        out_specs += [seq(0), blk(0)]
        out_shape += [jax.ShapeDtypeStruct((T, WA), F32)] * 2
    return pl.pallas_call(
        body, name=name, grid=(nbl, hp_n, nq), in_specs=in_specs, out_specs=out_specs, out_shape=out_shape,
        scratch_shapes=[pltpu.VMEM((S, 2 * LANES), F32)],
        compiler_params=_cparams(("parallel", "parallel", "arbitrary")),
    )(*args)


def _exchange(name, ins, out_shapes, remote, local):
    n_in, n_out = len(ins), len(out_shapes)
    nr, nl = len(remote), len(local)

    def body(*refs):
        in_refs = refs[:n_in]
        out_refs = refs[n_in:n_in + n_out]
        send_sems, recv_sems, loc_sems = refs[n_in + n_out:]
        me = (lax.axis_index("x"), lax.axis_index("y"), lax.axis_index("c"))

        def peer_of(flip):
            return tuple(1 - v if f else v for v, f in zip(me, flip))

        def at(ref, idx):
            return ref if idx is None else ref.at[idx]

        def rcopy(k, who):
            flip, a, sfn, b, dfn = remote[k]
            return pltpu.make_async_remote_copy(
                src_ref=at(in_refs[a], sfn(*who)), dst_ref=at(out_refs[b], dfn(*who)),
                send_sem=send_sems.at[k], recv_sem=recv_sems.at[k],
                device_id=peer_of(flip), device_id_type=MESH)

        locs = [pltpu.make_async_copy(at(in_refs[a], sfn(*me)), at(out_refs[b], dfn(*me)), loc_sems.at[k])
                for k, (a, sfn, b, dfn) in enumerate(local)]
        for cp in locs:
            cp.start()
        sends = [rcopy(k, me) for k in range(nr)]
        for cp in sends:
            cp.start()
        for k in range(nr):
            rcopy(k, peer_of(remote[k][0])).wait_recv()
        for cp in sends:
            cp.wait_send()
        for cp in locs:
            cp.wait()

    any_spec = pl.BlockSpec(memory_space=pl.ANY)
    return pl.pallas_call(
        body, name=name, in_specs=[any_spec] * n_in, out_specs=[any_spec] * n_out, out_shape=list(out_shapes),
        scratch_shapes=[pltpu.SemaphoreType.DMA((max(nr, 1),)), pltpu.SemaphoreType.DMA((max(nr, 1),)),
                        pltpu.SemaphoreType.DMA((max(nl, 1),))],
    )(*ins)


_FLIPS7 = [(0, 0, 1), (0, 1, 0), (0, 1, 1), (1, 0, 0), (1, 0, 1), (1, 1, 0), (1, 1, 1)]
_CHIP_FLIPS = [(1, 0, 0), (0, 1, 0), (1, 1, 0)]


def _dev_index(x, y, c):
    return 4 * x + 2 * y + c


def _chip_index(x, y, c):
    return 2 * x + y


def _all_gather8(name, v):
    remote = [(f, 0, lambda x, y, c: None, 0, _dev_index) for f in _FLIPS7]
    local = [(0, lambda x, y, c: None, 0, _dev_index)]
    return _exchange(name, [v], [jax.ShapeDtypeStruct((NDEV,) + v.shape, v.dtype)], remote, local)[0]


def _gather_halves(name, vs):
    n_v = len(vs)

    def body(*refs):
        in_refs, out_refs = refs[:n_v], refs[n_v:2 * n_v]
        send_sems, recv_sems = refs[2 * n_v:]
        x, y, c = lax.axis_index("x"), lax.axis_index("y"), lax.axis_index("c")
        sibling = (x, y, 1 - c)
        chips = [(1 - x, y), (x, 1 - y), (1 - x, 1 - y)]

        def copy(k, n, src, blk, half, to):
            return pltpu.make_async_remote_copy(
                src_ref=src, dst_ref=out_refs[n].at[blk, half], send_sem=send_sems.at[k], recv_sem=recv_sems.at[k],
                device_id=to, device_id_type=MESH)

        first = [copy(6 * n + j, n, in_refs[n].at[c], j, c, (*chip, c))
                 for n in range(n_v) for j, chip in enumerate(chips)]
        for cp in first:
            cp.start()
        passed = []
        for n in range(n_v):
            for j, chip in enumerate(chips):
                copy(6 * n + j, n, in_refs[n].at[c], j, c, (*chip, c)).wait_recv()
                fw = copy(6 * n + 3 + j, n, out_refs[n].at[j, c], j, c, sibling)
                fw.start()
                passed.append(fw)
        for n in range(n_v):
            for j in range(len(chips)):
                copy(6 * n + 3 + j, n, out_refs[n].at[j, 1 - c], j, 1 - c, sibling).wait_recv()
        for cp in first + passed:
            cp.wait_send()

    any_spec = pl.BlockSpec(memory_space=pl.ANY)
    return pl.pallas_call(
        body, name=name, in_specs=[any_spec] * n_v, out_specs=[any_spec] * n_v,
        out_shape=[jax.ShapeDtypeStruct((NCHIP - 1,) + v.shape, v.dtype) for v in vs],
        scratch_shapes=[pltpu.SemaphoreType.DMA((6 * n_v,)), pltpu.SemaphoreType.DMA((6 * n_v,))],
    )(*vs)


def _to_sibling(name, v):
    remote = [((0, 0, 1), 0, lambda x, y, c: None, 0, lambda x, y, c: None)]
    return _exchange(name, [v], [jax.ShapeDtypeStruct(v.shape, v.dtype)], remote, [])[0]


def _scatter_chips(name, v):
    remote = []
    for j, f in enumerate(_CHIP_FLIPS):
        src = lambda x, y, c, f=f: _chip_index(1 - x if f[0] else x, 1 - y if f[1] else y, c)
        remote.append((f, 0, src, 0, lambda x, y, c, j=j: j))
    return _exchange(name, [v], [jax.ShapeDtypeStruct((NCHIP - 1,) + v.shape[1:], v.dtype)], remote, [])[0]


def _by_chip(own, others, chip):
    stacked = jnp.concatenate([own[None], others], axis=0)
    blocks = []
    for k in range(NCHIP):
        d = k ^ chip
        place = jnp.where(d == 0, 0, jnp.where(d == 2, 1, jnp.where(d == 1, 2, 3)))
        blocks.append(lax.dynamic_index_in_dim(stacked, place, axis=0, keepdims=False))
    return jnp.stack(blocks)


def _sum_leading(name, v, tm=None):
    n, r, w = v.shape
    tm = _pick(r, (256, 128, 64, 32, 16, 8)) if tm is None else tm

    def body(v_ref, o_ref):
        acc = v_ref[0].astype(F32)
        for k in range(1, n):
            acc = acc + v_ref[k].astype(F32)
        o_ref[...] = acc

    return pl.pallas_call(
        body, name=name, grid=(r // tm,), in_specs=[pl.BlockSpec((n, tm, w), lambda i: (0, i, 0))],
        out_specs=pl.BlockSpec((tm, w), lambda i: (i, 0)), out_shape=jax.ShapeDtypeStruct((r, w), F32),
        compiler_params=_cparams(("parallel",)),
    )(v)


def _add2(name, a, b, tm=None, out_dtype=F32):
    r, w = a.shape
    tm = _pick(r, (256, 128, 64, 32, 16, 8)) if tm is None else tm

    def body(a_ref, b_ref, o_ref):
        o_ref[...] = (a_ref[...] + b_ref[...]).astype(out_dtype)

    spec = pl.BlockSpec((tm, w), lambda i: (i, 0))
    return pl.pallas_call(
        body, name=name, grid=(r // tm,), in_specs=[spec, spec], out_specs=spec,
        out_shape=jax.ShapeDtypeStruct((r, w), out_dtype), compiler_params=_cparams(("parallel",)),
    )(a, b)


def _ada_fwd(call_all, w_shard):
    def body(c_ref, w_ref, o_ref):
        cv = c_ref[...]
        o_ref[...] = jnp.dot(cv * _sigmoid(cv), w_ref[...], preferred_element_type=F32,
                             precision=lax.Precision.HIGHEST)

    n = w_shard.shape[1]
    return pl.pallas_call(
        body, name="ada_fwd", out_shape=jax.ShapeDtypeStruct((call_all.shape[0], n), F32),
        compiler_params=pltpu.CompilerParams(vmem_limit_bytes=VMEM_LIMIT),
    )(call_all, w_shard)


def _ada_bwd(call_all, dada):
    def body(c_ref, d_ref, o_ref):
        cv = c_ref[...]
        o_ref[...] = lax.dot_general(cv * _sigmoid(cv), d_ref[...], (((0,), (0,)), ((), ())),
                                     preferred_element_type=F32, precision=lax.Precision.HIGHEST)

    return pl.pallas_call(
        body, name="ada_bwd", out_shape=jax.ShapeDtypeStruct((call_all.shape[1], dada.shape[1]), F32),
        compiler_params=pltpu.CompilerParams(vmem_limit_bytes=VMEM_LIMIT),
    )(call_all, dada)


def _adamw(name, w, g, m, v):
    r, wd = w.shape
    tm = _pick(r, (256, 128, 64, 32, 16, 8))
    bc1 = 1.0 - ADAM_B1 ** ADAM_STEP
    bc2 = 1.0 - ADAM_B2 ** ADAM_STEP

    def body(w_ref, g_ref, m_ref, v_ref, d_ref, mo_ref, vo_ref):
        gv = g_ref[...]
        mn = ADAM_B1 * m_ref[...] + (1.0 - ADAM_B1) * gv
        vn = ADAM_B2 * v_ref[...] + (1.0 - ADAM_B2) * (gv * gv)
        d_ref[...] = -ADAM_LR * ((mn / bc1) / (jnp.sqrt(vn / bc2) + ADAM_EPS) + ADAM_WD * w_ref[...])
        mo_ref[...] = mn
        vo_ref[...] = vn

    spec = pl.BlockSpec((tm, wd), lambda i: (i, 0))
    return pl.pallas_call(
        body, name=name, grid=(r // tm,), in_specs=[spec] * 4, out_specs=[spec] * 3,
        out_shape=[jax.ShapeDtypeStruct((r, wd), F32)] * 3, compiler_params=_cparams(("parallel",)),
    )(w, g, m, v)


def _rope_tables(positions):
    half = ROPE_DIMS // 2
    freqs = ROPE_THETA ** (-jnp.arange(0, ROPE_DIMS, 2, dtype=F32) / ROPE_DIMS)
    ang = positions.astype(F32).reshape(-1, 1) * freqs
    cos, sin = jnp.cos(ang), jnp.sin(ang)
    T = ang.shape[0]
    one = jnp.ones((T, HD - ROPE_DIMS), F32)
    zero = jnp.zeros((T, HD - ROPE_DIMS), F32)
    zh = jnp.zeros((T, half), F32)
    c64 = jnp.concatenate([cos, cos, one], axis=1)
    s1 = jnp.concatenate([zh, sin, zero], axis=1)
    s2 = jnp.concatenate([-sin, zh, zero], axis=1)
    rep = lambda t: jnp.concatenate([t] * (LANES // HD), axis=1)
    return rep(c64), rep(s1), rep(s2)


def _local_step(x, loss_target, positions, ada, w_qkv, w_f, w_out, w_up, conv_w8, w_down,
                b_fgate, gn, ln1_g, ln1_b, conv_b, ln2_g, ln2_b):
    T = x.shape[0]
    nbl = T // S
    nha = WA // HD
    sv = lambda k: ada[:, k:k + 1, :]
    sh_a, sc_a, g_a, sh_f, sc_f, g_f = (sv(k) for k in range(6))
    rope = _rope_tables(positions)
    neg_rope = (rope[0], -rope[1], -rope[2])
    gseg = jnp.asarray(np.kron(np.eye(min(256, 2 * WA) // HD), np.ones((HD, HD))), BF16)
    bias = jnp.asarray(_dil_bias(TQ))
    bf_pad = jnp.zeros((1, FPAD), F32).at[:, :nha].set(b_fgate)

    qkv, fa = _mod_mm("qkv_proj", x, sc_a, sh_a, (w_qkv, w_f), (BF16, F32), rope=rope, rope_secs=(3, 4))
    pq, pk, oq, ok, sq, sk = _fold_tables(nha)

    def fold_out(cum, f, b_ref, pq_ref, pk_ref, oq_ref, ok_ref):
        hi, mid, lo = _split3(cum)
        eqv = _dot(hi, pq_ref[0]) + _dot(mid, pq_ref[1]) + _dot(lo, pq_ref[2]) + oq_ref[...]
        ekv = ok_ref[...] - (_dot(hi, pk_ref[0]) + _dot(mid, pk_ref[1]) + _dot(lo, pk_ref[2]))
        return eqv, ekv

    eq, ek = _cumsum_seq(
        "fgate_fwd", [fa], [bf_pad, jnp.asarray(pq, BF16), jnp.asarray(pk, BF16), jnp.asarray(oq), jnp.asarray(ok)],
        lambda f, b_ref, *_: _log_sigmoid(f + b_ref[...]), fold_out, ((2 * WA, BF16), (WA, BF16)), reverse=False)
    oa, lse_a = _attn_fwd("fox_fwd", qkv, (0, 1, 2), True, eq=eq, ek=ek)
    ob, lse_b = _attn_fwd("dil_fwd", qkv, (3, 4, 5), False, bias=bias)

    def mix_fn(i, oav, obv, xv, gav, gn_ref, g_ref, wo_ref, l1g_ref, l1b_ref):
        o = jnp.concatenate([oav, obv], axis=1)
        rs = lax.rsqrt(_head_mean(o * o, g_ref) + RMS_EPS)
        merged = (o * rs * gn_ref[...]).astype(BF16)
        mix = _dot(merged, wo_ref[...])
        x1, _, _ = _ln_fwd(ALPHA * xv + gav * mix, l1g_ref[...], l1b_ref[...])
        return merged, mix, x1

    merged, mix, x1 = _rowwise("mix_out", mix_fn, T, 256, tiles=(oa, ob, x), seqvecs=(g_a,),
                               consts=(gn, gseg, w_out, ln1_g, ln1_b),
                               outs=((2 * WA, BF16), (D, F32), (D, F32)))
    u = _mod_mm("ffn_up", x1, sc_f, sh_f, (w_up,), (F32,))[0]

    def conv_y(i, uv, prev, cw_ref, cb_ref, tm):
        first = (i * tm) % S == 0
        s1, s2 = _conv_taps(uv, prev, first)
        y = cb_ref[...] + cw_ref[0:1, :] * s2 + cw_ref[1:2, :] * s1 + cw_ref[2:3, :] * uv
        return y, s1, s2

    tmc = 128

    def gate_fn(i, uv, prev, cw_ref, cb_ref):
        y, _, _ = conv_y(i, uv, prev, cw_ref, cb_ref, tmc)
        a, g = y[:, :DFF], y[:, DFF:]
        return g * _sigmoid(g) * a, y

    act, yconv = _rowwise("conv_gate", gate_fn, T, tmc, tiles=(u,), halos=((u, -1),), consts=(conv_w8, conv_b),
                          outs=((DFF, BF16), (2 * DFF, F32)))

    def down_fn(i, actv, x1v, tgt, gfv, wd_ref, g2_ref, b2_ref):
        ffn = _dot(actv, wd_ref[...])
        y, n2, rstd = _ln_fwd(ALPHA * x1v + gfv * ffn, g2_ref[...], b2_ref[...])
        err = y - tgt
        dy = err * (1.0 / D)
        dr2 = _ln_bwd(dy, n2, rstd, g2_ref[...])
        return (dr2, gfv * dr2, _rsum8(err * err), _rsum8(dy * n2), _rsum8(dy), _rsum8(dr2 * ffn))

    dr2, dffn, loss_acc, d_ln2g, d_ln2b, d_gf = _rowwise(
        "ffn_down_loss", down_fn, T, 256, tiles=(act, x1, loss_target), seqvecs=(g_f,),
        consts=(w_down, ln2_g, ln2_b), outs=((D, F32), (D, F32)), accs=(D, D, D), seqaccs=(D,))

    def gate_conv_bwd_fn(i, uv, yv, dfv, y_nxt, df_nxt, cw_ref, wd_ref):
        last = ((i + 1) * tmc) % S == 0
        y = jnp.concatenate([yv, y_nxt], axis=0)
        df_ext = jnp.concatenate([dfv, df_nxt], axis=0).astype(BF16)
        ch = _pick(DFF, (256, 128))
        dav = jnp.concatenate([_dot_nt(df_ext, wd_ref[c * ch:(c + 1) * ch, :]) for c in range(DFF // ch)], axis=1)
        a, g = y[:, :DFF], y[:, DFF:]
        sg = _sigmoid(g)
        dyc_ext = jnp.concatenate([dav * (g * sg), dav * a * (sg * (1.0 + g * (1.0 - sg)))], axis=1)
        dyc = dyc_ext[:tmc]
        u1, u2 = _conv_taps_up(dyc, dyc_ext[tmc:], last)
        du_ = cw_ref[2:3, :] * dyc + cw_ref[1:2, :] * u1 + cw_ref[0:1, :] * u2
        return du_, _rsum8(dyc), _rsum8(uv * u2), _rsum8(uv * u1), _rsum8(uv * dyc)

    du, d_cb, d_cw0, d_cw1, d_cw2 = _rowwise(
        "gate_conv_bwd", gate_conv_bwd_fn, T, tmc, tiles=(u, yconv, dffn), halos=((yconv, 1), (dffn, 1)),
        consts=(conv_w8, w_down), outs=((2 * DFF, BF16),), accs=(2 * DFF,) * 4)
    dh2 = _mm_nt("dh2", du, w_up)
    g_w_down = _mm_tn("dw_down", act, dffn)
    g_w_up = _mm_tn("dw_up", x1, du, mod=(sc_f, sh_f), by_chip=True)

    def ln1_bwd_fn(i, dr2v, dh2v, xv, mixv, x1v, scfv, gav, l1g_ref):
        dx1 = ALPHA * dr2v + dh2v * (1.0 + scfv)
        _, n1, rstd = _ln_fwd(ALPHA * xv + gav * mixv, l1g_ref[...], 0.0)
        dr1 = _ln_bwd(dx1, n1, rstd, l1g_ref[...])
        return (dr1, gav * dr1, _rsum8(dx1 * n1), _rsum8(dx1),
                _rsum8(dh2v * x1v), _rsum8(dh2v), _rsum8(dr1 * mixv))

    dr1, dmix, d_ln1g, d_ln1b, d_scf, d_shf, d_ga = _rowwise(
        "ln1_bwd", ln1_bwd_fn, T, 256, tiles=(dr2, dh2, x, mix, x1), seqvecs=(sc_f, g_a), consts=(ln1_g,),
        outs=((D, F32), (D, BF16)), accs=(D, D), seqaccs=(D, D, D))

    dmerged = _mm_nt("dmerged", dmix, w_out)
    g_w_out = _mm_tn("dw_out", merged, dmix)

    def hn_bwd_fn(i, dmv, oav, obv, gn_ref, g_ref):
        o = jnp.concatenate([oav, obv], axis=1)
        rs = lax.rsqrt(_head_mean(o * o, g_ref) + RMS_EPS)
        nrm = o * rs
        dn = dmv * gn_ref[...]
        do = rs * (dn - nrm * _head_mean(dn * nrm, g_ref))
        return do, _rsum8(dmv * nrm)

    do, d_gn = _rowwise("headnorm_bwd", hn_bwd_fn, T, 256, tiles=(dmerged, oa, ob), consts=(gn, gseg),
                        outs=((2 * WA, F32),), accs=(2 * WA,))
    dqa, dka, dva, dqe, dek = _attn_bwd("fox_bwd", qkv, (0, 1, 2), oa, do, 0, lse_a, True, eq=eq, ek=ek)
    dqb, dkb, dvb = _attn_bwd("dil_bwd", qkv, (3, 4, 5), ob, do, 1, lse_b, False, bias=bias)
    hdot = lambda a, m_ref: sum(_dot(piece, m_ref[...]) for piece in _split3(a))
    dfa, d_bf = _cumsum_seq(
        "fgate_bwd", [dqe, dek, fa], [bf_pad, jnp.asarray(sq, BF16), jnp.asarray(sk, BF16)],
        lambda dq_, dk_, f, b_ref, sq_ref, sk_ref: hdot(dq_, sq_ref) - hdot(dk_, sk_ref),
        lambda cum, dq_, dk_, f, b_ref, sq_ref, sk_ref: (cum * _sigmoid(-(f + b_ref[...])),),
        ((FPAD, F32),), reverse=True, n_acc=1)

    def dz_fn(i, a0, a1, a2, b0, b1, b2, fv, cv, s1v, s2v):
        ct, s1t, s2t = (_tile_lanes(t, WA) for t in (cv, s1v, s2v))
        return jnp.concatenate([a0, a1, a2, _rope(b0, ct, s1t, s2t), _rope(b1, ct, s1t, s2t), b2, fv], axis=1)

    dz = _rowwise("dz_pack", dz_fn, T, 256, tiles=(dqa, dka, dva, dqb, dkb, dvb, dfa) + neg_rope,
                  outs=((6 * WA + FPAD, BF16),))[0]
    w_cat = jnp.concatenate([w_qkv, w_f], axis=1)
    dh1 = _mm_nt("dh1", dz, w_cat)
    g_w_cat = _mm_tn("dw_in", x, dz, mod=(sc_a, sh_a), tt=256, t2=dz.shape[1])

    def dx_fn(i, dr1v, dh1v, xv, scav):
        return ALPHA * dr1v + dh1v * (1.0 + scav), _rsum8(dh1v * xv), _rsum8(dh1v)

    grad_x, d_sca, d_sha = _rowwise("dx_out", dx_fn, T, 256, tiles=(dr1, dh1, x), seqvecs=(sc_a,),
                                    outs=((D, F32),), seqaccs=(D, D))

    row0 = lambda a: a[..., 0, :]
    d_ada = jnp.stack([row0(d_sha), row0(d_sca), row0(d_ga), row0(d_shf), row0(d_scf), row0(d_gf)], axis=1)
    d_cw = jnp.stack([row0(d_cw0), row0(d_cw1), row0(d_cw2)], axis=0)
    loss_part = (0.5 / D) * jnp.sum(loss_acc[0])
    small = dict(b_fgate=row0(d_bf)[:nha], gn=row0(d_gn), ln1_g=row0(d_ln1g), ln1_b=row0(d_ln1b),
                 conv_b=row0(d_cb), ln2_g=row0(d_ln2g), ln2_b=row0(d_ln2b))
    big = dict(w_cat=g_w_cat, w_out=g_w_out, w_up=g_w_up, conv_w=d_cw, w_down=g_w_down)
    return loss_part, grad_x, d_ada, small, big


def _rows_of(n, w=None):
    return -(-n // (D if w is None else w))


def _as_rows(v):
    w = D
    k = v.shape[0]
    flat = v.reshape(k, -1)
    rows = _rows_of(_rows_of(flat.shape[1], w), SUBLANES) * SUBLANES
    flat = jnp.pad(flat, ((0, 0), (0, rows * w - flat.shape[1])))
    return flat.reshape(k, rows, w)


def kernel(x, c, positions, w_ada, b_ada, w_in, b_fgate, gn_a, gn_b, w_out, ln1_g, ln1_b, w_up, conv_w, conv_b, w_down, ln2_g, ln2_b, loss_target, m_w_ada, m_b_ada, m_w_in, m_b_fgate, m_gn_a, m_gn_b, m_w_out, m_ln1_g, m_ln1_b, m_w_up, m_conv_w, m_conv_b, m_w_down, m_ln2_g, m_ln2_b, v_w_ada, v_b_ada, v_w_in, v_b_fgate, v_gn_a, v_gn_b, v_w_out, v_ln1_g, v_ln1_b, v_w_up, v_conv_w, v_conv_b, v_w_down, v_ln2_g, v_ln2_b):
    mx, my, mc = lax.axis_index("x"), lax.axis_index("y"), lax.axis_index("c")
    dev = _dev_index(mx, my, mc)
    chip = _chip_index(mx, my, mc)
    nbl = x.shape[0]
    T = nbl * S
    nha = WA // HD
    d_in = w_in.shape[2] * NCHIP
    n_ada = w_ada.shape[2]

    c_pad = jnp.zeros((SUBLANES, D), F32).at[:nbl].set(c)
    c_all = _all_gather8("gather_c", c_pad)[:, :nbl].reshape(NDEV * nbl, D)
    ada_part = _ada_fwd(c_all, w_ada[0])
    n_cw = conv_w.shape[2]
    cw_rows = jnp.pad(conv_w[0], ((0, SUBLANES - conv_w.shape[1]), (0, n_ada - n_cw)))
    ada_blocks = _all_gather8("gather_ada", jnp.concatenate([ada_part, cw_rows], axis=0))
    n_c = NDEV * nbl
    ada_all = jnp.concatenate([ada_blocks[2 * k, :n_c] for k in range(NCHIP)], axis=1) + b_ada
    conv_w8 = jnp.concatenate([ada_blocks[2 * k, n_c:, :n_cw] for k in range(NCHIP)], axis=1)
    ada = lax.dynamic_slice_in_dim(ada_all, dev * nbl, nbl, axis=0).reshape(nbl, 6, D)

    w_in_sh = jnp.pad(w_in[0].astype(BF16), ((0, 0), (0, _rows_of(w_in.shape[2], LANES) * LANES - w_in.shape[2])))
    shards = [w_in_sh, w_out[0].astype(BF16), w_up[0].astype(BF16), w_down[0].astype(BF16)]
    halves = [t.reshape(2, t.shape[0] // 2, t.shape[1]) for t in shards]
    gathered_w = _gather_halves("gather_w", halves)
    g_in, g_out, g_up, g_down = (_by_chip(h, g, chip).reshape((NCHIP,) + t.shape)
                                 for g, h, t in zip(gathered_w, halves, shards))
    w_in_full = jnp.concatenate([g_in[k][:, :w_in.shape[2]] for k in range(NCHIP)], axis=1)
    w_qkv = jnp.concatenate([w_in_full[:, :3 * WA], w_in_full[:, 3 * WA + nha:]], axis=1)
    w_f = jnp.pad(w_in_full[:, 3 * WA:3 * WA + nha], ((0, 0), (0, FPAD - nha)))
    w_out_full = g_out.reshape(NCHIP * w_out.shape[1], D)
    w_up_full = jnp.concatenate([g_up[k] for k in range(NCHIP)], axis=1)
    w_down_full = g_down.reshape(NCHIP * w_down.shape[1], D)

    gn = jnp.concatenate([gn_a, gn_b], axis=1)
    loss_part, grad_x, d_ada, small, big = _local_step(
        x.reshape(T, D), loss_target.reshape(T, D), positions, ada, w_qkv, w_f, w_out_full, w_up_full, conv_w8,
        w_down_full, b_fgate, gn, ln1_g, ln1_b, conv_b, ln2_g, ln2_b)

    def row_pad(v, rows):
        flat = v.reshape(-1)
        return jnp.pad(flat, (0, rows * D - flat.shape[0]))

    n_cb = _rows_of(2 * DFF)
    small_flat = jnp.concatenate([
        row_pad(small["b_fgate"], 1), row_pad(small["gn"], 1), row_pad(small["ln1_g"], 1),
        row_pad(small["ln1_b"], 1), row_pad(small["ln2_g"], 1), row_pad(small["ln2_b"], 1),
        row_pad(jnp.full((1,), loss_part, F32), 1), row_pad(small["conv_b"], n_cb)])
    n_small = _rows_of(small_flat.shape[0], SUBLANES * D) * SUBLANES
    small_rows = jnp.pad(small_flat, (0, n_small * D - small_flat.shape[0])).reshape(n_small, D)
    ada_rows = jnp.pad(d_ada.reshape(nbl, 6, D), ((0, 0), (0, SUBLANES - 6), (0, 0))).reshape(nbl * SUBLANES, D)
    gathered = _all_gather8("gather_small", jnp.concatenate([small_rows, ada_rows], axis=0))
    red = _sum_leading("sum_small", gathered, tm=SUBLANES)
    g_b_fgate = red[0:1, :nha]
    g_gn = red[1:2, :2 * WA]
    g_ln1_g, g_ln1_b, g_ln2_g, g_ln2_b = red[2:3], red[3:4], red[4:5], red[5:6]
    loss = red[6, 0]
    g_conv_b = red[7:7 + n_cb].reshape(1, -1)[:, :2 * DFF]
    g_b_ada = _add2("sum_b_ada", red[n_small:n_small + SUBLANES], red[n_small + SUBLANES:n_small + 2 * SUBLANES],
                    tm=SUBLANES)[:6].reshape(1, 6 * D)
    dada_all = gathered[:, n_small:].reshape(NDEV, nbl, SUBLANES, D)[:, :, :6].reshape(NDEV * nbl, 6 * D)
    g_w_ada = _ada_bwd(c_all, lax.dynamic_slice_in_dim(dada_all, chip * n_ada, n_ada, axis=1))

    g_cat = big["w_cat"]
    g_w_in_full = jnp.concatenate([g_cat[:, :3 * WA], g_cat[:, 6 * WA:6 * WA + nha], g_cat[:, 3 * WA:6 * WA]], axis=1)
    sh_in = g_w_in_full.reshape(D, NCHIP, -1).transpose(1, 0, 2)
    sh_out = big["w_out"].reshape(NCHIP, -1, D)
    sh_up = big["w_up"]
    sh_cw = big["conv_w"].reshape(conv_w.shape[1], NCHIP, -1).transpose(1, 0, 2)
    sh_down = big["w_down"].reshape(NCHIP, -1, D)
    parts = [_as_rows(t) for t in (sh_in, sh_out, sh_up, sh_cw, sh_down)]
    part_rows = [p.shape[1] for p in parts]
    n_rows = _rows_of(sum(part_rows), 2 * LANES) * 2 * LANES
    half = n_rows // 2
    if n_rows > sum(part_rows):
        parts.append(jnp.zeros((NCHIP, n_rows - sum(part_rows), D), F32))
    halves = jnp.concatenate(parts, axis=1).reshape(NCHIP, 2, half, D)
    mine = lax.dynamic_index_in_dim(halves, mc, axis=1, keepdims=False).reshape(NCHIP * half, D)
    theirs = lax.dynamic_index_in_dim(halves, 1 - mc, axis=1, keepdims=False).reshape(NCHIP * half, D)
    from_sib = _to_sibling("pair_swap", theirs)
    pair_sum = _add2("pair_sum", mine, from_sib, out_dtype=BF16).reshape(NCHIP, half, D)
    by_chip = _scatter_chips("scatter_grads", pair_sum)
    own = lax.dynamic_index_in_dim(pair_sum, chip, axis=0, keepdims=False)
    my_half = _sum_leading("chip_sum", _by_chip(own, by_chip, chip))
    sib_half = _to_sibling("pair_share", my_half)
    pair = jnp.stack([my_half, sib_half])
    shard = jnp.concatenate([lax.dynamic_index_in_dim(pair, mc, axis=0, keepdims=False),
                             lax.dynamic_index_in_dim(pair, 1 - mc, axis=0, keepdims=False)], axis=0)

    def unpack(k, shape):
        start = sum(part_rows[:k])
        n = int(np.prod(shape))
        return shard[start:start + part_rows[k]].reshape(-1)[:n].reshape(shape)

    g_w_in = unpack(0, w_in.shape[1:])
    g_w_out = unpack(1, w_out.shape[1:])
    g_w_up = unpack(2, w_up.shape[1:])
    g_conv_w = unpack(3, conv_w.shape[1:])
    g_w_down = unpack(4, w_down.shape[1:])

    grads = dict(w_ada=g_w_ada, b_ada=g_b_ada, w_in=g_w_in, b_fgate=g_b_fgate, gn_a=g_gn[:, :WA], gn_b=g_gn[:, WA:],
                 w_out=g_w_out, ln1_g=g_ln1_g, ln1_b=g_ln1_b, w_up=g_w_up, conv_w=g_conv_w, conv_b=g_conv_b,
                 w_down=g_w_down, ln2_g=g_ln2_g, ln2_b=g_ln2_b)
    weights = dict(w_ada=w_ada, b_ada=b_ada, w_in=w_in, b_fgate=b_fgate, gn_a=gn_a, gn_b=gn_b, w_out=w_out,
                   ln1_g=ln1_g, ln1_b=ln1_b, w_up=w_up, conv_w=conv_w, conv_b=conv_b, w_down=w_down,
                   ln2_g=ln2_g, ln2_b=ln2_b)
    ms = dict(w_ada=m_w_ada, b_ada=m_b_ada, w_in=m_w_in, b_fgate=m_b_fgate, gn_a=m_gn_a, gn_b=m_gn_b,
              w_out=m_w_out, ln1_g=m_ln1_g, ln1_b=m_ln1_b, w_up=m_w_up, conv_w=m_conv_w, conv_b=m_conv_b,
              w_down=m_w_down, ln2_g=m_ln2_g, ln2_b=m_ln2_b)
    vs = dict(w_ada=v_w_ada, b_ada=v_b_ada, w_in=v_w_in, b_fgate=v_b_fgate, gn_a=v_gn_a, gn_b=v_gn_b,
              w_out=v_w_out, ln1_g=v_ln1_g, ln1_b=v_ln1_b, w_up=v_w_up, conv_w=v_conv_w, conv_b=v_conv_b,
              w_down=v_w_down, ln2_g=v_ln2_g, ln2_b=v_ln2_b)
    names = list(weights)
    big_names = ("w_ada", "w_in", "w_out", "w_up", "w_down")
    delta, new_m, new_v = {}, {}, {}
    for n in big_names:
        shp = weights[n].shape
        d, m2, v2 = _adamw("adamw_" + n, weights[n][0], grads[n].reshape(shp[1:]), ms[n][0], vs[n][0])
        delta[n], new_m[n], new_v[n] = d.reshape(shp), m2.reshape(shp), v2.reshape(shp)
    small_names = [n for n in names if n not in big_names]

    def pack_small(src):
        flats = []
        for n in small_names:
            flat = src[n].reshape(-1)
            flats.append(jnp.pad(flat, (0, _rows_of(flat.shape[0]) * D - flat.shape[0])))
        allf = jnp.concatenate(flats)
        rows = _rows_of(allf.shape[0], SUBLANES * D) * SUBLANES
        return jnp.pad(allf, (0, rows * D - allf.shape[0])).reshape(rows, D)

    sd, sm, sv_ = _adamw("adamw_small", pack_small(weights), pack_small(grads), pack_small(ms), pack_small(vs))
    off = 0
    for n in small_names:
        shp = weights[n].shape
        cnt = int(np.prod(shp))
        r = _rows_of(cnt)
        for dst, src in ((delta, sd), (new_m, sm), (new_v, sv_)):
            dst[n] = src[off:off + r].reshape(-1)[:cnt].reshape(shp)
        off += r

    out_g = {n: grads[n].reshape(weights[n].shape) for n in names}
    return (loss, grad_x.reshape(x.shape), *[out_g[n] for n in names], *[delta[n] for n in names],
            *[new_m[n] for n in names], *[new_v[n] for n in names])
```

```python
import functools
import math

import numpy as np
import jax
import jax.numpy as jnp
from jax import lax
from jax.experimental import pallas as pl
from jax.experimental.pallas import tpu as pltpu

F32 = jnp.float32
BF16 = jnp.bfloat16

D = 1024
S = 4096
HD = 64
WA = 512
DFF = 2816
NCHIP = 4
NDEV = 8
PATTERNS = ((128, 1), (512, 4), (2048, 16))
ROPE_THETA = 500000.0
ROPE_DIMS = HD // 4
ALPHA = (2.0 * 1) ** 0.25
LN_EPS = 1e-5
RMS_EPS = 1e-6
ADAM_LR = 0.001
ADAM_B1 = 0.9
ADAM_B2 = 0.999
ADAM_EPS = 1e-08
ADAM_WD = 0.01
ADAM_STEP = 10

LANES = 128
SUBLANES = 8
TQ = 512
FPAD = LANES
NEG = -1e30
VMEM_LIMIT = 56 * 1024 * 1024
MESH = pl.DeviceIdType.MESH


def _cparams(sem):
    return pltpu.CompilerParams(dimension_semantics=sem, vmem_limit_bytes=VMEM_LIMIT)


def _pick(n, cands):
    for c in cands:
        if n % c == 0:
            return c
    return n


def _rsum8(v):
    tm, w = v.shape
    return jnp.sum(v.reshape(tm // SUBLANES, SUBLANES, w), axis=0)


def _sigmoid(x):
    return 1.0 / (1.0 + jnp.exp(-x))


def _dot(a, b):
    return jnp.dot(a, b, preferred_element_type=F32)


def _dot_nt(a, b):
    return lax.dot_general(a, b, (((1,), (1,)), ((), ())), preferred_element_type=F32)


def _dot_tn(a, b):
    return lax.dot_general(a, b, (((0,), (0,)), ((), ())), preferred_element_type=F32)


def _rowwise(name, fn, T, tm, *, tiles=(), halos=(), seqvecs=(), consts=(), outs=(), accs=(), seqaccs=(),
             seq_len=None):
    seq_len = S if seq_len is None else seq_len
    nb = T // tm
    spb = max(seq_len // tm, 1)
    nseq = max(T // seq_len, 1)
    n8 = T // SUBLANES
    r8 = tm // SUBLANES
    in_specs, args = [], []
    for a in tiles:
        in_specs.append(pl.BlockSpec((tm, a.shape[1]), lambda i: (i, 0)))
        args.append(a)
    for a, direction in halos:
        if direction < 0:
            idx = lambda i: (jnp.maximum(i * r8 - 1, 0), 0)
        else:
            idx = lambda i: (jnp.minimum((i + 1) * r8, n8 - 1), 0)
        in_specs.append(pl.BlockSpec((SUBLANES, a.shape[1]), idx))
        args.append(a)
    for a in seqvecs:
        in_specs.append(pl.BlockSpec((1, 1, a.shape[2]), lambda i: (i // spb, 0, 0)))
        args.append(a)
    for a in consts:
        in_specs.append(pl.BlockSpec(a.shape, lambda i, nd=a.ndim: (0,) * nd))
        args.append(a)
    out_shape, out_specs = [], []
    for w, dt in outs:
        out_shape.append(jax.ShapeDtypeStruct((T, w), dt))
        out_specs.append(pl.BlockSpec((tm, w), lambda i: (i, 0)))
    for w in accs:
        out_shape.append(jax.ShapeDtypeStruct((SUBLANES, w), F32))
        out_specs.append(pl.BlockSpec((SUBLANES, w), lambda i: (0, 0)))
    for w in seqaccs:
        out_shape.append(jax.ShapeDtypeStruct((nseq, SUBLANES, w), F32))
        out_specs.append(pl.BlockSpec((1, SUBLANES, w), lambda i: (i // spb, 0, 0)))
    n_t, n_h, n_s, n_c = len(tiles), len(halos), len(seqvecs), len(consts)
    n_o, n_a, n_sa = len(outs), len(accs), len(seqaccs)

    def body(*refs):
        i = pl.program_id(0)
        ins = refs[:n_t + n_h + n_s + n_c]
        orefs = refs[n_t + n_h + n_s + n_c:]
        vals = [r[...] for r in ins[:n_t + n_h]]
        vals += [r[0] for r in ins[n_t + n_h:n_t + n_h + n_s]]
        vals += list(ins[n_t + n_h + n_s:])
        res = fn(i, *vals)
        if not isinstance(res, (tuple, list)):
            res = (res,)
        for k in range(n_o):
            orefs[k][...] = res[k].astype(orefs[k].dtype)
        for k in range(n_a):
            r = orefs[n_o + k]

            @pl.when(i == 0)
            def _():
                r[...] = jnp.zeros_like(r)

            r[...] += res[n_o + k]

            @pl.when(i == nb - 1)
            def _():
                r[...] = jnp.broadcast_to(jnp.sum(r[...], axis=0, keepdims=True), r.shape)
        for k in range(n_sa):
            r = orefs[n_o + n_a + k]

            @pl.when(i % spb == 0)
            def _():
                r[...] = jnp.zeros_like(r)

            r[0] += res[n_o + n_a + k]

            @pl.when(i % spb == spb - 1)
            def _():
                r[0] = jnp.broadcast_to(jnp.sum(r[0], axis=0, keepdims=True), r.shape[1:])

    sem = ("arbitrary",) if (n_a or n_sa) else ("parallel",)
    res = pl.pallas_call(
        body, name=name, grid=(nb,), in_specs=in_specs, out_specs=out_specs, out_shape=out_shape,
        compiler_params=_cparams(sem),
    )(*args)
    return res


def _ln_fwd(r, g, b):
    mu = jnp.mean(r, axis=-1, keepdims=True)
    xc = r - mu
    var = jnp.mean(xc * xc, axis=-1, keepdims=True)
    rstd = lax.rsqrt(var + LN_EPS)
    n = xc * rstd
    return n * g + b, n, rstd


def _ln_bwd(dy, n, rstd, g):
    dn = dy * g
    return rstd * (dn - jnp.mean(dn, axis=-1, keepdims=True) - n * jnp.mean(dn * n, axis=-1, keepdims=True))


def _head_mean(t, g_ref):
    gw = g_ref.shape[0]
    hi = t.astype(BF16)
    lo = (t - hi.astype(F32)).astype(BF16)
    g = g_ref[...]
    parts = []
    for c in range(t.shape[1] // gw):
        sl = slice(c * gw, (c + 1) * gw)
        parts.append(_dot(hi[:, sl], g) + _dot(lo[:, sl], g))
    out = parts[0] if len(parts) == 1 else jnp.concatenate(parts, axis=1)
    return out * (1.0 / HD)


def _rope(z, c, s1, s2):
    w = z.shape[1]
    half = ROPE_DIMS // 2
    return z * c + pltpu.roll(z, half, 1) * s1 + pltpu.roll(z, w - half, 1) * s2


def _tile_lanes(t, w):
    reps = w // t.shape[1]
    return t if reps == 1 else jnp.concatenate([t] * reps, axis=1)


def _conv_taps(ext, prev, first):
    tm = ext.shape[0]
    prev = jnp.where(first, jnp.zeros_like(prev), prev)
    r8 = lax.broadcasted_iota(jnp.int32, (SUBLANES, 1), 0)
    top = ext[0:SUBLANES]
    s1_top = jnp.where(r8 < 1, pltpu.roll(prev, 1, 0), pltpu.roll(top, 1, 0))
    s2_top = jnp.where(r8 < 2, pltpu.roll(prev, 2, 0), pltpu.roll(top, 2, 0))
    s1 = jnp.concatenate([s1_top, pltpu.roll(ext, 1, 0)[SUBLANES:]], axis=0)
    s2 = jnp.concatenate([s2_top, pltpu.roll(ext, 2, 0)[SUBLANES:]], axis=0)
    return s1, s2


def _conv_taps_up(ext, nxt, last):
    tm = ext.shape[0]
    nxt = jnp.where(last, jnp.zeros_like(nxt), nxt)
    r8 = lax.broadcasted_iota(jnp.int32, (SUBLANES, 1), 0)
    bot = ext[tm - SUBLANES:tm]
    u1_bot = jnp.where(r8 >= 7, pltpu.roll(nxt, 7, 0), pltpu.roll(bot, 7, 0))
    u2_bot = jnp.where(r8 >= 6, pltpu.roll(nxt, 6, 0), pltpu.roll(bot, 6, 0))
    u1 = jnp.concatenate([pltpu.roll(ext, tm - 1, 0)[:tm - SUBLANES], u1_bot], axis=0)
    u2 = jnp.concatenate([pltpu.roll(ext, tm - 2, 0)[:tm - SUBLANES], u2_bot], axis=0)
    return u1, u2


def _mm_nt(name, a, w, tm=256):
    T = a.shape[0]
    n = w.shape[0]
    ch = _pick(n, (512, 256, 128))

    def fn(i, av, w_ref):
        ab = av.astype(BF16)
        parts = [_dot_nt(ab, w_ref[c * ch:(c + 1) * ch, :]) for c in range(n // ch)]
        return parts[0] if len(parts) == 1 else jnp.concatenate(parts, axis=1)

    return _rowwise(name, fn, T, tm, tiles=(a,), consts=(w,), outs=((n, F32),))[0]


def _mm_tn(name, a, b, *, mod=None, tt=512, t2=None, by_chip=False):
    T, k1 = a.shape
    k2 = b.shape[1]
    t1 = k1 if k1 <= 1536 else _pick(k1, (1408, 1024, 512, 256, 128))
    if t2 is None:
        t2 = k2 if k2 <= 1536 else _pick(k2, (1408, 1024, 640, 512, 256, 128))
    wc = k2 // NCHIP
    if by_chip:
        t2 = 2 * wc
    tt = min(tt, S)
    spb = S // tt

    def body(*refs):
        if mod is not None:
            a_ref, sc_ref, sh_ref, b_ref, o_ref = refs
        else:
            a_ref, b_ref, o_ref = refs
        t = pl.program_id(2)

        @pl.when(t == 0)
        def _():
            o_ref[...] = jnp.zeros_like(o_ref)

        av = a_ref[...]
        if mod is not None:
            av = av * (1.0 + sc_ref[0]) + sh_ref[0]
        res = _dot_tn(av.astype(BF16), b_ref[...].astype(BF16))
        if by_chip:
            o_ref[0] += res[:, :wc]
            o_ref[1] += res[:, wc:]
        else:
            o_ref[...] += res

    in_specs = [pl.BlockSpec((tt, t1), lambda p, q, t: (t, p))]
    args = [a]
    if mod is not None:
        for v in mod:
            in_specs.append(pl.BlockSpec((1, 1, t1), lambda p, q, t: (t // spb, 0, p)))
            args.append(v)
    in_specs.append(pl.BlockSpec((tt, t2), lambda p, q, t: (t, q)))
    args.append(b)
    if by_chip:
        out_specs = pl.BlockSpec((2, t1, wc), lambda p, q, t: (q, p, 0))
        out_shape = jax.ShapeDtypeStruct((NCHIP, k1, wc), F32)
    else:
        out_specs = pl.BlockSpec((t1, t2), lambda p, q, t: (p, q))
        out_shape = jax.ShapeDtypeStruct((k1, k2), F32)
    return pl.pallas_call(
        body, name=name, grid=(k1 // t1, k2 // t2, T // tt), in_specs=in_specs, out_specs=out_specs,
        out_shape=out_shape, compiler_params=_cparams(("parallel", "parallel", "arbitrary")),
    )(*args)


def _mod_mm(name, x, sc, sh, ws, out_dtypes, rope=None, rope_secs=(), tm=256):
    T = x.shape[0]
    nw = len(ws)

    def fn(i, xv, *rest):
        if rope is not None:
            cv, s1v, s2v = rest[:3]
            rest = rest[3:]
        scv, shv = rest[:2]
        w_refs = rest[2:]
        h = (xv * (1.0 + scv) + shv).astype(BF16)
        res = []
        for k, w_ref in enumerate(w_refs):
            n = w_ref.shape[1]
            ch = WA if (k == 0 and rope is not None) else _pick(n, (512, 256, 128))
            parts = []
            for c in range(n // ch):
                z = _dot(h, w_ref[:, c * ch:(c + 1) * ch])
                if k == 0 and c in rope_secs:
                    z = _rope(z, _tile_lanes(cv, ch), _tile_lanes(s1v, ch), _tile_lanes(s2v, ch))
                parts.append(z.astype(out_dtypes[k]))
            res.append(parts[0] if len(parts) == 1 else jnp.concatenate(parts, axis=1))
        return tuple(res)

    tiles = (x,) + (tuple(rope) if rope is not None else ())
    outs = tuple((w.shape[1], dt) for w, dt in zip(ws, out_dtypes))
    return _rowwise(name, fn, T, tm, tiles=tiles, seqvecs=(sc, sh), consts=tuple(ws), outs=outs)


def _tri(tb, lower):
    r = lax.broadcasted_iota(jnp.int32, (tb, tb), 0)
    c = lax.broadcasted_iota(jnp.int32, (tb, tb), 1)
    return jnp.where((r >= c) if lower else (r <= c), 1.0, 0.0).astype(BF16)


def _split3(x):
    hi = x.astype(BF16)
    r = x - hi.astype(F32)
    mid = r.astype(BF16)
    return hi, mid, (r - mid.astype(F32)).astype(BF16)


def _cumsum_seq(name, ins, consts, fn_in, fn_out, outs, reverse, n_acc=0, tb=256):
    T = ins[0].shape[0]
    tb = min(tb, S)
    nbs = S // tb
    nseq = T // S
    n_i, n_c, n_o = len(ins), len(consts), len(outs)

    def blk(b, j):
        return (b * nbs + (nbs - 1 - j if reverse else j), 0)

    def body(*refs):
        i_refs, c_refs = refs[:n_i], refs[n_i:n_i + n_c]
        o_refs = refs[n_i + n_c:n_i + n_c + n_o]
        acc_refs = refs[n_i + n_c + n_o:n_i + n_c + n_o + n_acc]
        carry = refs[-1]
        b, j = pl.program_id(0), pl.program_id(1)

        @pl.when(j == 0)
        def _():
            carry[...] = jnp.zeros_like(carry)

        iv = [r[...] for r in i_refs]
        xin = fn_in(*iv, *c_refs)
        tri = _tri(tb, not reverse)
        cum = sum(_dot(tri, piece) for piece in _split3(xin)) + carry[0:1, :]
        carry[...] = carry[...] + jnp.sum(xin, axis=0, keepdims=True)
        res = fn_out(cum, *iv, *c_refs)
        for o, r in zip(o_refs, res):
            o[...] = r.astype(o.dtype)
        for a in acc_refs:
            @pl.when((b == 0) & (j == 0))
            def _():
                a[...] = jnp.zeros_like(a)

            a[...] += _rsum8(res[0])

            @pl.when((b == nseq - 1) & (j == nbs - 1))
            def _():
                a[...] = jnp.broadcast_to(jnp.sum(a[...], axis=0, keepdims=True), a.shape)

    in_specs = [pl.BlockSpec((tb, a.shape[1]), blk) for a in ins]
    in_specs += [pl.BlockSpec(c.shape, lambda b, j, nd=c.ndim: (0,) * nd) for c in consts]
    out_shape = [jax.ShapeDtypeStruct((T, w), dt) for w, dt in outs]
    out_shape += [jax.ShapeDtypeStruct((SUBLANES, outs[0][0]), F32)] * n_acc
    out_specs = [pl.BlockSpec((tb, w), blk) for w, _ in outs]
    out_specs += [pl.BlockSpec((SUBLANES, outs[0][0]), lambda b, j: (0, 0))] * n_acc
    return pl.pallas_call(
        body, name=name, grid=(nseq, nbs), in_specs=in_specs, out_specs=out_specs, out_shape=out_shape,
        scratch_shapes=[pltpu.VMEM((SUBLANES, FPAD), F32)],
        compiler_params=_cparams(("arbitrary", "arbitrary")),
    )(*ins, *consts)


def _log_sigmoid(x):
    return jnp.minimum(x, 0.0) - jnp.log(1.0 + jnp.exp(-jnp.abs(x)))


def _dil_bias(tq):
    max_win = max(w for w, _ in PATTERNS)
    nd = (max_win + tq - 1) // tq + 1
    qi = np.arange(tq)[:, None]
    kj = np.arange(tq)[None, :]
    tabs = []
    for dlt in range(nd):
        dist = dlt * tq + qi - kj
        mult = np.zeros((tq, tq), np.float64)
        for win, dil in PATTERNS:
            mult += (dist >= 0) & (dist % dil == 0) & (dist // dil <= win // dil)
        tabs.append(np.where(mult > 0, np.log(np.maximum(mult, 1.0)), NEG))
    return np.stack(tabs).astype(np.float32)


def _fold_tables(nha):
    hp_n = nha // 2
    pq = np.zeros((3, FPAD, hp_n * 2 * LANES), np.float32)
    pk = np.zeros((3, FPAD, hp_n * LANES), np.float32)
    oq = np.zeros((1, hp_n * 2 * LANES), np.float32)
    ok = np.zeros((1, hp_n * LANES), np.float32)
    sq = np.zeros((hp_n * LANES, FPAD), np.float32)
    sk = np.zeros((hp_n * LANES, FPAD), np.float32)
    for h in range(nha):
        hp, odd = divmod(h, 2)
        qb = hp * 2 * LANES + odd * (LANES + 8)
        kb = hp * LANES + odd * 8
        for i in range(3):
            pq[i, h, qb + i] = 1
            oq[0, qb + 3 + i] = 1
            ok[0, kb + i] = 1
            pk[i, h, kb + 3 + i] = 1
        sq[kb, h] = 1
        sk[kb + 3, h] = 1
    return pq, pk, oq, ok, sq, sk


def _stack_heads(x2, h0, extra=None):
    z = jnp.zeros_like(x2)
    a, b = jnp.where(h0, x2, z), jnp.where(h0, z, x2)
    if extra is not None:
        a = jnp.concatenate([a, extra[:, :LANES]], axis=1)
        b = jnp.concatenate([b, extra[:, LANES:]], axis=1)
    return jnp.concatenate([a, b], axis=0)


def _attn_fwd(name, qkv, secs, fox, eq=None, ek=None, bias=None, ride=()):
    T = qkv.shape[0]
    nq = S // TQ
    nbl = T // S
    hp_n = WA // LANES
    sq, sk, sv = (s * hp_n for s in secs)
    scale = HD ** -0.5
    nd = None if fox else bias.shape[0]

    n_r = len(ride)
    n_in = (5 if fox else 4) + n_r

    def body(*refs):
        if fox:
            q_ref, k_ref, v_ref, eq_ref, ek_ref = refs[:5]
        else:
            q_ref, k_ref, v_ref, b_ref = refs[:4]
        o_ref, lse_ref = refs[n_in:n_in + 2]
        i = pl.program_id(2)
        if n_r:
            ride_start, ride_finish = _gather_halves_steps(
                refs[n_in - n_r:n_in], refs[n_in + 2:n_in + 2 + n_r], *refs[n_in + 2 + n_r:])
            at = lambda b, hp, q: (pl.program_id(0) == b) & (pl.program_id(1) == hp) & (i == q)
            pl.when(at(0, 0, 0))(ride_start)
        lane = lax.broadcasted_iota(jnp.int32, (1, LANES), 1)
        h0 = lane < HD
        q2 = (q_ref[...].astype(F32) * scale).astype(BF16)
        qs = _stack_heads(q2, h0, eq_ref[...] if fox else None)

        def scores(t, diag):
            off = pl.multiple_of((i - t) * TQ, TQ)
            kk = k_ref[pl.ds(off, TQ), :]
            if fox:
                kk = jnp.concatenate([kk, ek_ref[pl.ds(off, TQ), :]], axis=1)
            s = jnp.concatenate([_dot_nt(qs[:TQ], kk), _dot_nt(qs[TQ:], kk)], axis=0)
            if not fox:
                s = (s.reshape(2, TQ, TQ) + b_ref[t]).reshape(2 * TQ, TQ)
            elif diag:
                rows = lax.broadcasted_iota(jnp.int32, (2, TQ, TQ), 1).reshape(2 * TQ, TQ)
                cols = lax.broadcasted_iota(jnp.int32, (2 * TQ, TQ), 1)
                s = jnp.where(cols <= rows, s, NEG)
            return s

        def update(t, s, m, l, acc):
            off = pl.multiple_of((i - t) * TQ, TQ)
            v2 = v_ref[pl.ds(off, TQ), :]
            m_new = jnp.maximum(m, jnp.max(s, axis=1, keepdims=True))
            p = jnp.exp(s - m_new)
            a = jnp.exp(m - m_new)
            l = a * l + jnp.sum(p, axis=1, keepdims=True)
            pb = p.astype(BF16)
            acc = a * acc + jnp.concatenate([_dot(pb[:TQ], v2), _dot(pb[TQ:], v2)], axis=0)
            return m_new, l, acc

        init =(jnp.full((2 * TQ, 1), NEG, F32), jnp.zeros((2 * TQ, 1), F32), jnp.zeros((2 * TQ, LANES), F32))
        n = i + 1 if fox else jnp.minimum(i + 1, nd)
        m, l, acc = update(0, scores(0, True), *init)
        m, l, acc = lax.fori_loop(1, n, lambda t, c: update(t, scores(t, False), *c), (m, l, acc))
        on = acc / l
        o_ref[...] = jnp.where(h0, on[:TQ], on[TQ:])
        lse = jnp.broadcast_to(m + jnp.log(l), (2 * TQ, LANES))
        lse_ref[...] = jnp.concatenate([lse[:TQ], lse[TQ:]], axis=1)
        if n_r:
            pl.when(at(nbl - 1, hp_n - 1, nq - 1))(ride_finish)

    in_specs = [
        pl.BlockSpec((TQ, LANES), lambda b, hp, i: (b * nq + i, sq + hp)),
        pl.BlockSpec((S, LANES), lambda b, hp, i: (b, sk + hp)),
        pl.BlockSpec((S, LANES), lambda b, hp, i: (b, sv + hp)),
    ]
    args = [qkv, qkv, qkv]
    if fox:
        in_specs += [pl.BlockSpec((TQ, 2 * LANES), lambda b, hp, i: (b * nq + i, hp)),
                     pl.BlockSpec((S, LANES), lambda b, hp, i: (b, hp))]
        args += [eq, ek]
    else:
        in_specs.append(pl.BlockSpec(bias.shape, lambda b, hp, i: (0, 0, 0)))
        args.append(bias)
    any_spec = pl.BlockSpec(memory_space=pl.ANY)
    out_specs = [pl.BlockSpec((TQ, LANES), lambda b, hp, i: (b * nq + i, hp)),
                 pl.BlockSpec((TQ, 2 * LANES), lambda b, hp, i: (b * nq + i, hp))] + [any_spec] * n_r
    out_shape = [jax.ShapeDtypeStruct((T, WA), F32), jax.ShapeDtypeStruct((T, 2 * WA), F32)]
    out_shape += [jax.ShapeDtypeStruct((NCHIP - 1,) + v.shape, v.dtype) for v in ride]
    sems = [pltpu.SemaphoreType.DMA((6 * n_r,)), pltpu.SemaphoreType.DMA((6 * n_r,))] if n_r else []
    sem = ("arbitrary",) * 3 if n_r else ("parallel", "parallel", "arbitrary")
    return pl.pallas_call(
        body, name=name, grid=(nbl, hp_n, nq), in_specs=in_specs + [any_spec] * n_r, out_specs=out_specs,
        out_shape=out_shape, scratch_shapes=sems, compiler_params=_cparams(sem),
    )(*args, *ride)


def _attn_bwd(name, qkv, secs, o, do, do_sec, lse, fox, eq=None, ek=None, bias=None):
    T = qkv.shape[0]
    nq = S // TQ
    nbl = T // S
    hp_n = WA // LANES
    sq, sk, sv = (s * hp_n for s in secs)
    dsec = do_sec * hp_n
    scale = HD ** -0.5
    nd = None if fox else bias.shape[0]
    kc = 2 * LANES if fox else LANES

    def body(*refs):
        if fox:
            (q_ref, k_ref, v_ref, o_ref, do_ref, lse_ref, eq_ref, ek_ref,
             dq_ref, dk_ref, dv_ref, dqe_ref, dek_ref, dl_ref) = refs
        else:
            q_ref, k_ref, v_ref, o_ref, do_ref, lse_ref, b_ref, dq_ref, dk_ref, dv_ref, dl_ref = refs
        j = pl.program_id(2)
        lane = lax.broadcasted_iota(jnp.int32, (1, LANES), 1)
        h0 = lane < HD

        @pl.when(j == 0)
        def _():
            dq_ref[...] = jnp.zeros_like(dq_ref)
            if fox:
                dqe_ref[...] = jnp.zeros_like(dqe_ref)

            def dl_step(r, c):
                off = pl.multiple_of(r * TQ, TQ)
                d2 = do_ref[pl.ds(off, TQ), :] * o_ref[pl.ds(off, TQ), :]
                z2 = jnp.zeros_like(d2)
                dl0 = jnp.sum(jnp.where(h0, d2, z2), axis=1, keepdims=True)
                dl1 = jnp.sum(jnp.where(h0, z2, d2), axis=1, keepdims=True)
                dl_ref[pl.ds(off, TQ), :] = jnp.concatenate(
                    [jnp.broadcast_to(dl0, (TQ, LANES)), jnp.broadcast_to(dl1, (TQ, LANES))], axis=1)
                return c

            lax.fori_loop(0, nq, dl_step, 0)

        kk = k_ref[...]
        if fox:
            kk = jnp.concatenate([kk, ek_ref[...]], axis=1)
        v2 = v_ref[...]

        def wide(x2):
            st = jnp.concatenate([x2[:, :LANES], x2[:, LANES:]], axis=0)
            return st if TQ == LANES else jnp.concatenate([st] * (TQ // LANES), axis=1)

        def step(t, carry, diag):
            dkk, dv2 = carry
            off = pl.multiple_of((j + t) * TQ, TQ)
            q2 = (q_ref[pl.ds(off, TQ), :].astype(F32) * scale).astype(BF16)
            qs = _stack_heads(q2, h0, eq_ref[pl.ds(off, TQ), :] if fox else None)
            dos = _stack_heads(do_ref[pl.ds(off, TQ), :].astype(BF16), h0)
            s = jnp.concatenate([_dot_nt(qs[:TQ], kk), _dot_nt(qs[TQ:], kk)], axis=0)
            if not fox:
                s = (s.reshape(2, TQ, TQ) + b_ref[t]).reshape(2 * TQ, TQ)
            elif diag:
                rows = lax.broadcasted_iota(jnp.int32, (2, TQ, TQ), 1).reshape(2 * TQ, TQ)
                cols = lax.broadcasted_iota(jnp.int32, (2 * TQ, TQ), 1)
                s = jnp.where(cols <= rows, s, NEG)
            p = jnp.exp(s - wide(lse_ref[pl.ds(off, TQ), :]))
            dp = jnp.concatenate([_dot_nt(dos[:TQ], v2), _dot_nt(dos[TQ:], v2)], axis=0)
            dsb = (p * (dp - wide(dl_ref[pl.ds(off, TQ), :]))).astype(BF16)
            dv2 = dv2 + _dot_tn(p.astype(BF16), dos)
            dkk = dkk + _dot_tn(dsb, qs)
            dqq = jnp.concatenate([_dot(dsb[:TQ], kk), _dot(dsb[TQ:], kk)], axis=0)
            dq_ref[pl.ds(off, TQ), :] += jnp.where(h0, dqq[:TQ, :LANES], dqq[TQ:, :LANES])
            if fox:
                dqe_ref[pl.ds(off, TQ), :] += jnp.where(lane < SUBLANES, dqq[:TQ, LANES:], dqq[TQ:, LANES:])
            return dkk, dv2

        zero = (jnp.zeros((TQ, kc), F32), jnp.zeros((TQ, LANES), F32))
        if fox:
            dkk, dv2 = lax.fori_loop(1, nq - j, lambda t, c: step(t, c, False), step(0, zero, True))
        else:
            dkk, dv2 = lax.fori_loop(0, jnp.minimum(nq - j, nd), lambda t, c: step(t, c, False), zero)
        dk_ref[...] = dkk[:, :LANES]
        dv_ref[...] = dv2
        if fox:
            dek_ref[...] = dkk[:, LANES:]

        @pl.when(j == nq - 1)
        def _():
            dq_ref[...] = dq_ref[...] * scale

    seq = lambda c, w=LANES: pl.BlockSpec((S, w), lambda b, hp, j: (b, c + hp))
    blk = lambda c: pl.BlockSpec((TQ, LANES), lambda b, hp, j: (b * nq + j, c + hp))
    in_specs = [seq(sq), blk(sk), blk(sv), seq(0), seq(dsec), seq(0, 2 * LANES)]
    args = [qkv, qkv, qkv, o, do, lse]
    if fox:
        in_specs += [seq(0, 2 * LANES), blk(0)]
        args += [eq, ek]
    else:
        in_specs.append(pl.BlockSpec(bias.shape, lambda b, hp, j: (0, 0, 0)))
        args.append(bias)
    out_specs = [seq(0), blk(0), blk(0)]
    out_shape = [jax.ShapeDtypeStruct((T, WA), F32)] * 3
    if fox:
        out_specs += [seq(0), blk(0)]
        out_shape += [jax.ShapeDtypeStruct((T, WA), F32)] * 2
    return pl.pallas_call(
        body, name=name, grid=(nbl, hp_n, nq), in_specs=in_specs, out_specs=out_specs, out_shape=out_shape,
        scratch_shapes=[pltpu.VMEM((S, 2 * LANES), F32)],
        compiler_params=_cparams(("parallel", "parallel", "arbitrary")),
    )(*args)


def _exchange(name, ins, out_shapes, remote, local):
    n_in, n_out = len(ins), len(out_shapes)
    nr, nl = len(remote), len(local)

    def body(*refs):
        in_refs = refs[:n_in]
        out_refs = refs[n_in:n_in + n_out]
        send_sems, recv_sems, loc_sems = refs[n_in + n_out:]
        me = (lax.axis_index("x"), lax.axis_index("y"), lax.axis_index("c"))

        def peer_of(flip):
            return tuple(1 - v if f else v for v, f in zip(me, flip))

        def at(ref, idx):
            return ref if idx is None else ref.at[idx]

        def rcopy(k, who):
            flip, a, sfn, b, dfn = remote[k]
            return pltpu.make_async_remote_copy(
                src_ref=at(in_refs[a], sfn(*who)), dst_ref=at(out_refs[b], dfn(*who)),
                send_sem=send_sems.at[k], recv_sem=recv_sems.at[k],
                device_id=peer_of(flip), device_id_type=MESH)

        locs = [pltpu.make_async_copy(at(in_refs[a], sfn(*me)), at(out_refs[b], dfn(*me)), loc_sems.at[k])
                for k, (a, sfn, b, dfn) in enumerate(local)]
        for cp in locs:
            cp.start()
        sends = [rcopy(k, me) for k in range(nr)]
        for cp in sends:
            cp.start()
        for k in range(nr):
            rcopy(k, peer_of(remote[k][0])).wait_recv()
        for cp in sends:
            cp.wait_send()
        for cp in locs:
            cp.wait()

    any_spec = pl.BlockSpec(memory_space=pl.ANY)
    return pl.pallas_call(
        body, name=name, in_specs=[any_spec] * n_in, out_specs=[any_spec] * n_out, out_shape=list(out_shapes),
        scratch_shapes=[pltpu.SemaphoreType.DMA((max(nr, 1),)), pltpu.SemaphoreType.DMA((max(nr, 1),)),
                        pltpu.SemaphoreType.DMA((max(nl, 1),))],
    )(*ins)


_FLIPS7 = [(0, 0, 1), (0, 1, 0), (0, 1, 1), (1, 0, 0), (1, 0, 1), (1, 1, 0), (1, 1, 1)]
_CHIP_FLIPS = [(1, 0, 0), (0, 1, 0), (1, 1, 0)]


def _dev_index(x, y, c):
    return 4 * x + 2 * y + c


def _chip_index(x, y, c):
    return 2 * x + y


def _all_gather8(name, v):
    remote = [(f, 0, lambda x, y, c: None, 0, _dev_index) for f in _FLIPS7]
    local = [(0, lambda x, y, c: None, 0, _dev_index)]
    return _exchange(name, [v], [jax.ShapeDtypeStruct((NDEV,) + v.shape, v.dtype)], remote, local)[0]


def _gather_halves_steps(in_refs, out_refs, send_sems, recv_sems):
    n_v = len(in_refs)
    x, y, c = lax.axis_index("x"), lax.axis_index("y"), lax.axis_index("c")
    sibling = (x, y, 1 - c)
    chips = [(1 - x, y), (x, 1 - y), (1 - x, 1 - y)]

    def copy(k, n, src, blk, half, to):
        return pltpu.make_async_remote_copy(
            src_ref=src, dst_ref=out_refs[n].at[blk, half], send_sem=send_sems.at[k], recv_sem=recv_sems.at[k],
            device_id=to, device_id_type=MESH)

    def first():
        return [copy(6 * n + j, n, in_refs[n].at[c], j, c, (*chip, c))
                for n in range(n_v) for j, chip in enumerate(chips)]

    def start():
        for cp in first():
            cp.start()

    def finish():
        passed = []
        for n in range(n_v):
            for j, chip in enumerate(chips):
                copy(6 * n + j, n, in_refs[n].at[c], j, c, (*chip, c)).wait_recv()
                fw = copy(6 * n + 3 + j, n, out_refs[n].at[j, c], j, c, sibling)
                fw.start()
                passed.append(fw)
        for n in range(n_v):
            for j in range(len(chips)):
                copy(6 * n + 3 + j, n, out_refs[n].at[j, 1 - c], j, 1 - c, sibling).wait_recv()
        for cp in first() + passed:
            cp.wait_send()

    return start, finish


def _gather_halves(name, vs):
    n_v = len(vs)

    def body(*refs):
        start, finish = _gather_halves_steps(refs[:n_v], refs[n_v:2 * n_v], *refs[2 * n_v:])
        start()
        finish()

    any_spec = pl.BlockSpec(memory_space=pl.ANY)
    return pl.pallas_call(
        body, name=name, in_specs=[any_spec] * n_v, out_specs=[any_spec] * n_v,
        out_shape=[jax.ShapeDtypeStruct((NCHIP - 1,) + v.shape, v.dtype) for v in vs],
        scratch_shapes=[pltpu.SemaphoreType.DMA((6 * n_v,)), pltpu.SemaphoreType.DMA((6 * n_v,))],
    )(*vs)


def _to_sibling(name, v):
    remote = [((0, 0, 1), 0, lambda x, y, c: None, 0, lambda x, y, c: None)]
    return _exchange(name, [v], [jax.ShapeDtypeStruct(v.shape, v.dtype)], remote, [])[0]


def _scatter_chips(name, v):
    remote = []
    for j, f in enumerate(_CHIP_FLIPS):
        src = lambda x, y, c, f=f: _chip_index(1 - x if f[0] else x, 1 - y if f[1] else y, c)
        remote.append((f, 0, src, 0, lambda x, y, c, j=j: j))
    return _exchange(name, [v], [jax.ShapeDtypeStruct((NCHIP - 1,) + v.shape[1:], v.dtype)], remote, [])[0]


def _by_chip(own, others, chip):
    stacked = jnp.concatenate([own[None], others], axis=0)
    blocks = []
    for k in range(NCHIP):
        d = k ^ chip
        place = jnp.where(d == 0, 0, jnp.where(d == 2, 1, jnp.where(d == 1, 2, 3)))
        blocks.append(lax.dynamic_index_in_dim(stacked, place, axis=0, keepdims=False))
    return jnp.stack(blocks)


def _sum_leading(name, v, tm=None):
    n, r, w = v.shape
    tm = _pick(r, (256, 128, 64, 32, 16, 8)) if tm is None else tm

    def body(v_ref, o_ref):
        acc = v_ref[0].astype(F32)
        for k in range(1, n):
            acc = acc + v_ref[k].astype(F32)
        o_ref[...] = acc

    return pl.pallas_call(
        body, name=name, grid=(r // tm,), in_specs=[pl.BlockSpec((n, tm, w), lambda i: (0, i, 0))],
        out_specs=pl.BlockSpec((tm, w), lambda i: (i, 0)), out_shape=jax.ShapeDtypeStruct((r, w), F32),
        compiler_params=_cparams(("parallel",)),
    )(v)


def _add2(name, a, b, tm=None, out_dtype=F32):
    r, w = a.shape
    tm = _pick(r, (256, 128, 64, 32, 16, 8)) if tm is None else tm

    def body(a_ref, b_ref, o_ref):
        o_ref[...] = (a_ref[...] + b_ref[...]).astype(out_dtype)

    spec = pl.BlockSpec((tm, w), lambda i: (i, 0))
    return pl.pallas_call(
        body, name=name, grid=(r // tm,), in_specs=[spec, spec], out_specs=spec,
        out_shape=jax.ShapeDtypeStruct((r, w), out_dtype), compiler_params=_cparams(("parallel",)),
    )(a, b)


def _ada_fwd(call_all, w_shard):
    def body(c_ref, w_ref, o_ref):
        cv = c_ref[...]
        o_ref[...] = jnp.dot(cv * _sigmoid(cv), w_ref[...], preferred_element_type=F32,
                             precision=lax.Precision.HIGHEST)

    n = w_shard.shape[1]
    return pl.pallas_call(
        body, name="ada_fwd", out_shape=jax.ShapeDtypeStruct((call_all.shape[0], n), F32),
        compiler_params=pltpu.CompilerParams(vmem_limit_bytes=VMEM_LIMIT),
    )(call_all, w_shard)


def _ada_bwd(call_all, dada):
    def body(c_ref, d_ref, o_ref):
        cv = c_ref[...]
        o_ref[...] = lax.dot_general(cv * _sigmoid(cv), d_ref[...], (((0,), (0,)), ((), ())),
                                     preferred_element_type=F32, precision=lax.Precision.HIGHEST)

    return pl.pallas_call(
        body, name="ada_bwd", out_shape=jax.ShapeDtypeStruct((call_all.shape[1], dada.shape[1]), F32),
        compiler_params=pltpu.CompilerParams(vmem_limit_bytes=VMEM_LIMIT),
    )(call_all, dada)


def _adamw(name, w, g, m, v):
    r, wd = w.shape
    tm = _pick(r, (256, 128, 64, 32, 16, 8))
    bc1 = 1.0 - ADAM_B1 ** ADAM_STEP
    bc2 = 1.0 - ADAM_B2 ** ADAM_STEP

    def body(w_ref, g_ref, m_ref, v_ref, d_ref, mo_ref, vo_ref):
        gv = g_ref[...]
        mn = ADAM_B1 * m_ref[...] + (1.0 - ADAM_B1) * gv
        vn = ADAM_B2 * v_ref[...] + (1.0 - ADAM_B2) * (gv * gv)
        d_ref[...] = -ADAM_LR * ((mn / bc1) / (jnp.sqrt(vn / bc2) + ADAM_EPS) + ADAM_WD * w_ref[...])
        mo_ref[...] = mn
        vo_ref[...] = vn

    spec = pl.BlockSpec((tm, wd), lambda i: (i, 0))
    return pl.pallas_call(
        body, name=name, grid=(r // tm,), in_specs=[spec] * 4, out_specs=[spec] * 3,
        out_shape=[jax.ShapeDtypeStruct((r, wd), F32)] * 3, compiler_params=_cparams(("parallel",)),
    )(w, g, m, v)


def _rope_tables(positions):
    half = ROPE_DIMS // 2
    freqs = ROPE_THETA ** (-jnp.arange(0, ROPE_DIMS, 2, dtype=F32) / ROPE_DIMS)
    ang = positions.astype(F32).reshape(-1, 1) * freqs
    cos, sin = jnp.cos(ang), jnp.sin(ang)
    T = ang.shape[0]
    one = jnp.ones((T, HD - ROPE_DIMS), F32)
    zero = jnp.zeros((T, HD - ROPE_DIMS), F32)
    zh = jnp.zeros((T, half), F32)
    c64 = jnp.concatenate([cos, cos, one], axis=1)
    s1 = jnp.concatenate([zh, sin, zero], axis=1)
    s2 = jnp.concatenate([-sin, zh, zero], axis=1)
    rep = lambda t: jnp.concatenate([t] * (LANES // HD), axis=1)
    return rep(c64), rep(s1), rep(s2)


def _local_step(x, loss_target, positions, ada, w_qkv, w_f, w_out, w_up, conv_w8, w_down,
                b_fgate, gn, ln1_g, ln1_b, conv_b, ln2_g, ln2_b, late=None):
    T = x.shape[0]
    nbl = T // S
    nha = WA // HD
    sv = lambda k: ada[:, k:k + 1, :]
    sh_a, sc_a, g_a, sh_f, sc_f, g_f = (sv(k) for k in range(6))
    rope = _rope_tables(positions)
    neg_rope = (rope[0], -rope[1], -rope[2])
    gseg = jnp.asarray(np.kron(np.eye(min(256, 2 * WA) // HD), np.ones((HD, HD))), BF16)
    bias = jnp.asarray(_dil_bias(TQ))
    bf_pad = jnp.zeros((1, FPAD), F32).at[:, :nha].set(b_fgate)

    qkv, fa = _mod_mm("qkv_proj", x, sc_a, sh_a, (w_qkv, w_f), (BF16, F32), rope=rope, rope_secs=(3, 4))
    pq, pk, oq, ok, sq, sk = _fold_tables(nha)

    def fold_out(cum, f, b_ref, pq_ref, pk_ref, oq_ref, ok_ref):
        hi, mid, lo = _split3(cum)
        eqv = _dot(hi, pq_ref[0]) + _dot(mid, pq_ref[1]) + _dot(lo, pq_ref[2]) + oq_ref[...]
        ekv = ok_ref[...] - (_dot(hi, pk_ref[0]) + _dot(mid, pk_ref[1]) + _dot(lo, pk_ref[2]))
        return eqv, ekv

    eq, ek = _cumsum_seq(
        "fgate_fwd", [fa], [bf_pad, jnp.asarray(pq, BF16), jnp.asarray(pk, BF16), jnp.asarray(oq), jnp.asarray(ok)],
        lambda f, b_ref, *_: _log_sigmoid(f + b_ref[...]), fold_out, ((2 * WA, BF16), (WA, BF16)), reverse=False)
    oa, lse_a, *got = _attn_fwd("fox_fwd", qkv, (0, 1, 2), True, eq=eq, ek=ek, ride=late[0] if late else ())
    if late:
        w_up, w_down = late[1](got)
    ob, lse_b = _attn_fwd("dil_fwd", qkv, (3, 4, 5), False, bias=bias)

    def mix_fn(i, oav, obv, xv, gav, gn_ref, g_ref, wo_ref, l1g_ref, l1b_ref):
        o = jnp.concatenate([oav, obv], axis=1)
        rs = lax.rsqrt(_head_mean(o * o, g_ref) + RMS_EPS)
        merged = (o * rs * gn_ref[...]).astype(BF16)
        mix = _dot(merged, wo_ref[...])
        x1, _, _ = _ln_fwd(ALPHA * xv + gav * mix, l1g_ref[...], l1b_ref[...])
        return merged, mix, x1

    merged, mix, x1 = _rowwise("mix_out", mix_fn, T, 256, tiles=(oa, ob, x), seqvecs=(g_a,),
                               consts=(gn, gseg, w_out, ln1_g, ln1_b),
                               outs=((2 * WA, BF16), (D, F32), (D, F32)))
    u = _mod_mm("ffn_up", x1, sc_f, sh_f, (w_up,), (F32,))[0]

    def conv_y(i, uv, prev, cw_ref, cb_ref, tm):
        first = (i * tm) % S == 0
        s1, s2 = _conv_taps(uv, prev, first)
        y = cb_ref[...] + cw_ref[0:1, :] * s2 + cw_ref[1:2, :] * s1 + cw_ref[2:3, :] * uv
        return y, s1, s2

    tmc = 128

    def gate_fn(i, uv, prev, cw_ref, cb_ref):
        y, _, _ = conv_y(i, uv, prev, cw_ref, cb_ref, tmc)
        a, g = y[:, :DFF], y[:, DFF:]
        return g * _sigmoid(g) * a, y

    act, yconv = _rowwise("conv_gate", gate_fn, T, tmc, tiles=(u,), halos=((u, -1),), consts=(conv_w8, conv_b),
                          outs=((DFF, BF16), (2 * DFF, F32)))

    def down_fn(i, actv, x1v, tgt, gfv, wd_ref, g2_ref, b2_ref):
        ffn = _dot(actv, wd_ref[...])
        y, n2, rstd = _ln_fwd(ALPHA * x1v + gfv * ffn, g2_ref[...], b2_ref[...])
        err = y - tgt
        dy = err * (1.0 / D)
        dr2 = _ln_bwd(dy, n2, rstd, g2_ref[...])
        return (dr2, gfv * dr2, _rsum8(err * err), _rsum8(dy * n2), _rsum8(dy), _rsum8(dr2 * ffn))

    dr2, dffn, loss_acc, d_ln2g, d_ln2b, d_gf = _rowwise(
        "ffn_down_loss", down_fn, T, 256, tiles=(act, x1, loss_target), seqvecs=(g_f,),
        consts=(w_down, ln2_g, ln2_b), outs=((D, F32), (D, F32)), accs=(D, D, D), seqaccs=(D,))

    def gate_conv_bwd_fn(i, uv, yv, dfv, y_nxt, df_nxt, cw_ref, wd_ref):
        last = ((i + 1) * tmc) % S == 0
        y = jnp.concatenate([yv, y_nxt], axis=0)
        df_ext = jnp.concatenate([dfv, df_nxt], axis=0).astype(BF16)
        ch = _pick(DFF, (256, 128))
        dav = jnp.concatenate([_dot_nt(df_ext, wd_ref[c * ch:(c + 1) * ch, :]) for c in range(DFF // ch)], axis=1)
        a, g = y[:, :DFF], y[:, DFF:]
        sg = _sigmoid(g)
        dyc_ext = jnp.concatenate([dav * (g * sg), dav * a * (sg * (1.0 + g * (1.0 - sg)))], axis=1)
        dyc = dyc_ext[:tmc]
        u1, u2 = _conv_taps_up(dyc, dyc_ext[tmc:], last)
        du_ = cw_ref[2:3, :] * dyc + cw_ref[1:2, :] * u1 + cw_ref[0:1, :] * u2
        return du_, _rsum8(dyc), _rsum8(uv * u2), _rsum8(uv * u1), _rsum8(uv * dyc)

    du, d_cb, d_cw0, d_cw1, d_cw2 = _rowwise(
        "gate_conv_bwd", gate_conv_bwd_fn, T, tmc, tiles=(u, yconv, dffn), halos=((yconv, 1), (dffn, 1)),
        consts=(conv_w8, w_down), outs=((2 * DFF, BF16),), accs=(2 * DFF,) * 4)
    dh2 = _mm_nt("dh2", du, w_up)
    g_w_down = _mm_tn("dw_down", act, dffn)
    g_w_up = _mm_tn("dw_up", x1, du, mod=(sc_f, sh_f), by_chip=True)

    def ln1_bwd_fn(i, dr2v, dh2v, xv, mixv, x1v, scfv, gav, l1g_ref):
        dx1 = ALPHA * dr2v + dh2v * (1.0 + scfv)
        _, n1, rstd = _ln_fwd(ALPHA * xv + gav * mixv, l1g_ref[...], 0.0)
        dr1 = _ln_bwd(dx1, n1, rstd, l1g_ref[...])
        return (dr1, gav * dr1, _rsum8(dx1 * n1), _rsum8(dx1),
                _rsum8(dh2v * x1v), _rsum8(dh2v), _rsum8(dr1 * mixv))

    dr1, dmix, d_ln1g, d_ln1b, d_scf, d_shf, d_ga = _rowwise(
        "ln1_bwd", ln1_bwd_fn, T, 256, tiles=(dr2, dh2, x, mix, x1), seqvecs=(sc_f, g_a), consts=(ln1_g,),
        outs=((D, F32), (D, BF16)), accs=(D, D), seqaccs=(D, D, D))

    dmerged = _mm_nt("dmerged", dmix, w_out)
    g_w_out = _mm_tn("dw_out", merged, dmix)

    def hn_bwd_fn(i, dmv, oav, obv, gn_ref, g_ref):
        o = jnp.concatenate([oav, obv], axis=1)
        rs = lax.rsqrt(_head_mean(o * o, g_ref) + RMS_EPS)
        nrm = o * rs
        dn = dmv * gn_ref[...]
        do = rs * (dn - nrm * _head_mean(dn * nrm, g_ref))
        return do, _rsum8(dmv * nrm)

    do, d_gn = _rowwise("headnorm_bwd", hn_bwd_fn, T, 256, tiles=(dmerged, oa, ob), consts=(gn, gseg),
                        outs=((2 * WA, F32),), accs=(2 * WA,))
    dqa, dka, dva, dqe, dek = _attn_bwd("fox_bwd", qkv, (0, 1, 2), oa, do, 0, lse_a, True, eq=eq, ek=ek)
    dqb, dkb, dvb = _attn_bwd("dil_bwd", qkv, (3, 4, 5), ob, do, 1, lse_b, False, bias=bias)
    hdot = lambda a, m_ref: sum(_dot(piece, m_ref[...]) for piece in _split3(a))
    dfa, d_bf = _cumsum_seq(
        "fgate_bwd", [dqe, dek, fa], [bf_pad, jnp.asarray(sq, BF16), jnp.asarray(sk, BF16)],
        lambda dq_, dk_, f, b_ref, sq_ref, sk_ref: hdot(dq_, sq_ref) - hdot(dk_, sk_ref),
        lambda cum, dq_, dk_, f, b_ref, sq_ref, sk_ref: (cum * _sigmoid(-(f + b_ref[...])),),
        ((FPAD, F32),), reverse=True, n_acc=1)

    def dz_fn(i, a0, a1, a2, b0, b1, b2, fv, cv, s1v, s2v):
        ct, s1t, s2t = (_tile_lanes(t, WA) for t in (cv, s1v, s2v))
        return jnp.concatenate([a0, a1, a2, _rope(b0, ct, s1t, s2t), _rope(b1, ct, s1t, s2t), b2, fv], axis=1)

    dz = _rowwise("dz_pack", dz_fn, T, 256, tiles=(dqa, dka, dva, dqb, dkb, dvb, dfa) + neg_rope,
                  outs=((6 * WA + FPAD, BF16),))[0]
    w_cat = jnp.concatenate([w_qkv, w_f], axis=1)
    dh1 = _mm_nt("dh1", dz, w_cat)
    g_w_cat = _mm_tn("dw_in", x, dz, mod=(sc_a, sh_a), tt=256, t2=dz.shape[1])

    def dx_fn(i, dr1v, dh1v, xv, scav):
        return ALPHA * dr1v + dh1v * (1.0 + scav), _rsum8(dh1v * xv), _rsum8(dh1v)

    grad_x, d_sca, d_sha = _rowwise("dx_out", dx_fn, T, 256, tiles=(dr1, dh1, x), seqvecs=(sc_a,),
                                    outs=((D, F32),), seqaccs=(D, D))

    row0 = lambda a: a[..., 0, :]
    d_ada = jnp.stack([row0(d_sha), row0(d_sca), row0(d_ga), row0(d_shf), row0(d_scf), row0(d_gf)], axis=1)
    d_cw = jnp.stack([row0(d_cw0), row0(d_cw1), row0(d_cw2)], axis=0)
    loss_part = (0.5 / D) * jnp.sum(loss_acc[0])
    small = dict(b_fgate=row0(d_bf)[:nha], gn=row0(d_gn), ln1_g=row0(d_ln1g), ln1_b=row0(d_ln1b),
                 conv_b=row0(d_cb), ln2_g=row0(d_ln2g), ln2_b=row0(d_ln2b))
    big = dict(w_cat=g_w_cat, w_out=g_w_out, w_up=g_w_up, conv_w=d_cw, w_down=g_w_down)
    return loss_part, grad_x, d_ada, small, big


def _rows_of(n, w=None):
    return -(-n // (D if w is None else w))


def _as_rows(v):
    w = D
    k = v.shape[0]
    flat = v.reshape(k, -1)
    rows = _rows_of(_rows_of(flat.shape[1], w), SUBLANES) * SUBLANES
    flat = jnp.pad(flat, ((0, 0), (0, rows * w - flat.shape[1])))
    return flat.reshape(k, rows, w)


def kernel(x, c, positions, w_ada, b_ada, w_in, b_fgate, gn_a, gn_b, w_out, ln1_g, ln1_b, w_up, conv_w, conv_b, w_down, ln2_g, ln2_b, loss_target, m_w_ada, m_b_ada, m_w_in, m_b_fgate, m_gn_a, m_gn_b, m_w_out, m_ln1_g, m_ln1_b, m_w_up, m_conv_w, m_conv_b, m_w_down, m_ln2_g, m_ln2_b, v_w_ada, v_b_ada, v_w_in, v_b_fgate, v_gn_a, v_gn_b, v_w_out, v_ln1_g, v_ln1_b, v_w_up, v_conv_w, v_conv_b, v_w_down, v_ln2_g, v_ln2_b):
    mx, my, mc = lax.axis_index("x"), lax.axis_index("y"), lax.axis_index("c")
    dev = _dev_index(mx, my, mc)
    chip = _chip_index(mx, my, mc)
    nbl = x.shape[0]
    T = nbl * S
    nha = WA // HD
    d_in = w_in.shape[2] * NCHIP
    n_ada = w_ada.shape[2]

    c_pad = jnp.zeros((SUBLANES, D), F32).at[:nbl].set(c)
    c_all = _all_gather8("gather_c", c_pad)[:, :nbl].reshape(NDEV * nbl, D)
    ada_part = _ada_fwd(c_all, w_ada[0])
    n_cw = conv_w.shape[2]
    cw_rows = jnp.pad(conv_w[0], ((0, SUBLANES - conv_w.shape[1]), (0, n_ada - n_cw)))
    ada_blocks = _all_gather8("gather_ada", jnp.concatenate([ada_part, cw_rows], axis=0))
    n_c = NDEV * nbl
    ada_all = jnp.concatenate([ada_blocks[2 * k, :n_c] for k in range(NCHIP)], axis=1) + b_ada
    conv_w8 = jnp.concatenate([ada_blocks[2 * k, n_c:, :n_cw] for k in range(NCHIP)], axis=1)
    ada = lax.dynamic_slice_in_dim(ada_all, dev * nbl, nbl, axis=0).reshape(nbl, 6, D)

    w_in_sh = jnp.pad(w_in[0].astype(BF16), ((0, 0), (0, _rows_of(w_in.shape[2], LANES) * LANES - w_in.shape[2])))
    halve = lambda t: t.reshape(2, t.shape[0] // 2, t.shape[1])
    whole = lambda g, t: _by_chip(halve(t), g, chip).reshape((NCHIP,) + t.shape)
    w_out_sh, w_up_sh, w_down_sh = w_out[0].astype(BF16), w_up[0].astype(BF16), w_down[0].astype(BF16)
    g_in, g_out = _gather_halves("gather_w", [halve(w_in_sh), halve(w_out_sh)])
    g_in, g_out = whole(g_in, w_in_sh), whole(g_out, w_out_sh)
    w_in_full = jnp.concatenate([g_in[k][:, :w_in.shape[2]] for k in range(NCHIP)], axis=1)
    w_qkv = jnp.concatenate([w_in_full[:, :3 * WA], w_in_full[:, 3 * WA + nha:]], axis=1)
    w_f = jnp.pad(w_in_full[:, 3 * WA:3 * WA + nha], ((0, 0), (0, FPAD - nha)))
    w_out_full = g_out.reshape(NCHIP * w_out.shape[1], D)

    def late_weights(got):
        g_up, g_down = whole(got[0], w_up_sh), whole(got[1], w_down_sh)
        return (jnp.concatenate([g_up[k] for k in range(NCHIP)], axis=1),
                g_down.reshape(NCHIP * w_down.shape[1], D))

    gn = jnp.concatenate([gn_a, gn_b], axis=1)
    loss_part, grad_x, d_ada, small, big = _local_step(
        x.reshape(T, D), loss_target.reshape(T, D), positions, ada, w_qkv, w_f, w_out_full, None, conv_w8,
        None, b_fgate, gn, ln1_g, ln1_b, conv_b, ln2_g, ln2_b, late=([halve(w_up_sh), halve(w_down_sh)], late_weights))

    def row_pad(v, rows):
        flat = v.reshape(-1)
        return jnp.pad(flat, (0, rows * D - flat.shape[0]))

    n_cb = _rows_of(2 * DFF)
    small_flat = jnp.concatenate([
        row_pad(small["b_fgate"], 1), row_pad(small["gn"], 1), row_pad(small["ln1_g"], 1),
        row_pad(small["ln1_b"], 1), row_pad(small["ln2_g"], 1), row_pad(small["ln2_b"], 1),
        row_pad(jnp.full((1,), loss_part, F32), 1), row_pad(small["conv_b"], n_cb)])
    n_small = _rows_of(small_flat.shape[0], SUBLANES * D) * SUBLANES
    small_rows = jnp.pad(small_flat, (0, n_small * D - small_flat.shape[0])).reshape(n_small, D)
    ada_rows = jnp.pad(d_ada.reshape(nbl, 6, D), ((0, 0), (0, SUBLANES - 6), (0, 0))).reshape(nbl * SUBLANES, D)
    gathered = _all_gather8("gather_small", jnp.concatenate([small_rows, ada_rows], axis=0))
    red = _sum_leading("sum_small", gathered, tm=SUBLANES)
    g_b_fgate = red[0:1, :nha]
    g_gn = red[1:2, :2 * WA]
    g_ln1_g, g_ln1_b, g_ln2_g, g_ln2_b = red[2:3], red[3:4], red[4:5], red[5:6]
    loss = red[6, 0]
    g_conv_b = red[7:7 + n_cb].reshape(1, -1)[:, :2 * DFF]
    g_b_ada = _add2("sum_b_ada", red[n_small:n_small + SUBLANES], red[n_small + SUBLANES:n_small + 2 * SUBLANES],
                    tm=SUBLANES)[:6].reshape(1, 6 * D)
    dada_all = gathered[:, n_small:].reshape(NDEV, nbl, SUBLANES, D)[:, :, :6].reshape(NDEV * nbl, 6 * D)
    g_w_ada = _ada_bwd(c_all, lax.dynamic_slice_in_dim(dada_all, chip * n_ada, n_ada, axis=1))

    g_cat = big["w_cat"]
    g_w_in_full = jnp.concatenate([g_cat[:, :3 * WA], g_cat[:, 6 * WA:6 * WA + nha], g_cat[:, 3 * WA:6 * WA]], axis=1)
    sh_in = g_w_in_full.reshape(D, NCHIP, -1).transpose(1, 0, 2)
    sh_out = big["w_out"].reshape(NCHIP, -1, D)
    sh_up = big["w_up"]
    sh_cw = big["conv_w"].reshape(conv_w.shape[1], NCHIP, -1).transpose(1, 0, 2)
    sh_down = big["w_down"].reshape(NCHIP, -1, D)
    parts = [_as_rows(t) for t in (sh_in, sh_out, sh_up, sh_cw, sh_down)]
    part_rows = [p.shape[1] for p in parts]
    n_rows = _rows_of(sum(part_rows), 2 * LANES) * 2 * LANES
    half = n_rows // 2
    if n_rows > sum(part_rows):
        parts.append(jnp.zeros((NCHIP, n_rows - sum(part_rows), D), F32))
    halves = jnp.concatenate(parts, axis=1).reshape(NCHIP, 2, half, D)
    mine = lax.dynamic_index_in_dim(halves, mc, axis=1, keepdims=False).reshape(NCHIP * half, D)
    theirs = lax.dynamic_index_in_dim(halves, 1 - mc, axis=1, keepdims=False).reshape(NCHIP * half, D)
    from_sib = _to_sibling("pair_swap", theirs)
    pair_sum = _add2("pair_sum", mine, from_sib, out_dtype=BF16).reshape(NCHIP, half, D)
    by_chip = _scatter_chips("scatter_grads", pair_sum)
    own = lax.dynamic_index_in_dim(pair_sum, chip, axis=0, keepdims=False)
    my_half = _sum_leading("chip_sum", _by_chip(own, by_chip, chip))
    sib_half = _to_sibling("pair_share", my_half)
    pair = jnp.stack([my_half, sib_half])
    shard = jnp.concatenate([lax.dynamic_index_in_dim(pair, mc, axis=0, keepdims=False),
                             lax.dynamic_index_in_dim(pair, 1 - mc, axis=0, keepdims=False)], axis=0)

    def unpack(k, shape):
        start = sum(part_rows[:k])
        n = int(np.prod(shape))
        return shard[start:start + part_rows[k]].reshape(-1)[:n].reshape(shape)

    g_w_in = unpack(0, w_in.shape[1:])
    g_w_out = unpack(1, w_out.shape[1:])
    g_w_up = unpack(2, w_up.shape[1:])
    g_conv_w = unpack(3, conv_w.shape[1:])
    g_w_down = unpack(4, w_down.shape[1:])

    grads = dict(w_ada=g_w_ada, b_ada=g_b_ada, w_in=g_w_in, b_fgate=g_b_fgate, gn_a=g_gn[:, :WA], gn_b=g_gn[:, WA:],
                 w_out=g_w_out, ln1_g=g_ln1_g, ln1_b=g_ln1_b, w_up=g_w_up, conv_w=g_conv_w, conv_b=g_conv_b,
                 w_down=g_w_down, ln2_g=g_ln2_g, ln2_b=g_ln2_b)
    weights = dict(w_ada=w_ada, b_ada=b_ada, w_in=w_in, b_fgate=b_fgate, gn_a=gn_a, gn_b=gn_b, w_out=w_out,
                   ln1_g=ln1_g, ln1_b=ln1_b, w_up=w_up, conv_w=conv_w, conv_b=conv_b, w_down=w_down,
                   ln2_g=ln2_g, ln2_b=ln2_b)
    ms = dict(w_ada=m_w_ada, b_ada=m_b_ada, w_in=m_w_in, b_fgate=m_b_fgate, gn_a=m_gn_a, gn_b=m_gn_b,
              w_out=m_w_out, ln1_g=m_ln1_g, ln1_b=m_ln1_b, w_up=m_w_up, conv_w=m_conv_w, conv_b=m_conv_b,
              w_down=m_w_down, ln2_g=m_ln2_g, ln2_b=m_ln2_b)
    vs = dict(w_ada=v_w_ada, b_ada=v_b_ada, w_in=v_w_in, b_fgate=v_b_fgate, gn_a=v_gn_a, gn_b=v_gn_b,
              w_out=v_w_out, ln1_g=v_ln1_g, ln1_b=v_ln1_b, w_up=v_w_up, conv_w=v_conv_w, conv_b=v_conv_b,
              w_down=v_w_down, ln2_g=v_ln2_g, ln2_b=v_ln2_b)
    names = list(weights)
    big_names = ("w_ada", "w_in", "w_out", "w_up", "w_down")
    delta, new_m, new_v = {}, {}, {}
    for n in big_names:
        shp = weights[n].shape
        d, m2, v2 = _adamw("adamw_" + n, weights[n][0], grads[n].reshape(shp[1:]), ms[n][0], vs[n][0])
        delta[n], new_m[n], new_v[n] = d.reshape(shp), m2.reshape(shp), v2.reshape(shp)
    small_names = [n for n in names if n not in big_names]

    def pack_small(src):
        flats = []
        for n in small_names:
            flat = src[n].reshape(-1)
            flats.append(jnp.pad(flat, (0, _rows_of(flat.shape[0]) * D - flat.shape[0])))
        allf = jnp.concatenate(flats)
        rows = _rows_of(allf.shape[0], SUBLANES * D) * SUBLANES
        return jnp.pad(allf, (0, rows * D - allf.shape[0])).reshape(rows, D)

    sd, sm, sv_ = _adamw("adamw_small", pack_small(weights), pack_small(grads), pack_small(ms), pack_small(vs))
    off = 0
    for n in small_names:
        shp = weights[n].shape
        cnt = int(np.prod(shp))
        r = _rows_of(cnt)
        for dst, src in ((delta, sd), (new_m, sm), (new_v, sv_)):
            dst[n] = src[off:off + r].reshape(-1)[:cnt].reshape(shp)
        off += r

    out_g = {n: grads[n].reshape(weights[n].shape) for n in names}
    return (loss, grad_x.reshape(x.shape), *[out_g[n] for n in names], *[delta[n] for n in names],
            *[new_m[n] for n in names], *[new_v[n] for n in names])
```

```python
import functools
import math

import numpy as np
import jax
import jax.numpy as jnp
from jax import lax
from jax.experimental import pallas as pl
from jax.experimental.pallas import tpu as pltpu

F32 = jnp.float32
BF16 = jnp.bfloat16

D = 1024
S = 4096
HD = 64
WA = 512
DFF = 2816
NCHIP = 4
NDEV = 8
PATTERNS = ((128, 1), (512, 4), (2048, 16))
ROPE_THETA = 500000.0
ROPE_DIMS = HD // 4
ALPHA = (2.0 * 1) ** 0.25
LN_EPS = 1e-5
RMS_EPS = 1e-6
ADAM_LR = 0.001
ADAM_B1 = 0.9
ADAM_B2 = 0.999
ADAM_EPS = 1e-08
ADAM_WD = 0.01
ADAM_STEP = 10

LANES = 128
SUBLANES = 8
TQ = 512
FPAD = LANES
NEG = -1e30
VMEM_LIMIT = 56 * 1024 * 1024
MESH = pl.DeviceIdType.MESH


def _cparams(sem):
    return pltpu.CompilerParams(dimension_semantics=sem, vmem_limit_bytes=VMEM_LIMIT)


def _pick(n, cands):
    for c in cands:
        if n % c == 0:
            return c
    return n


def _rsum8(v):
    tm, w = v.shape
    return jnp.sum(v.reshape(tm // SUBLANES, SUBLANES, w), axis=0)


def _sigmoid(x):
    return 1.0 / (1.0 + jnp.exp(-x))


def _dot(a, b):
    return jnp.dot(a, b, preferred_element_type=F32)


def _dot_nt(a, b):
    return lax.dot_general(a, b, (((1,), (1,)), ((), ())), preferred_element_type=F32)


def _dot_tn(a, b):
    return lax.dot_general(a, b, (((0,), (0,)), ((), ())), preferred_element_type=F32)


def _rowwise(name, fn, T, tm, *, tiles=(), halos=(), seqvecs=(), consts=(), outs=(), accs=(), seqaccs=(),
             seq_len=None):
    seq_len = S if seq_len is None else seq_len
    nb = T // tm
    spb = max(seq_len // tm, 1)
    nseq = max(T // seq_len, 1)
    n8 = T // SUBLANES
    r8 = tm // SUBLANES
    in_specs, args = [], []
    for a in tiles:
        in_specs.append(pl.BlockSpec((tm, a.shape[1]), lambda i: (i, 0)))
        args.append(a)
    for a, direction in halos:
        if direction < 0:
            idx = lambda i: (jnp.maximum(i * r8 - 1, 0), 0)
        else:
            idx = lambda i: (jnp.minimum((i + 1) * r8, n8 - 1), 0)
        in_specs.append(pl.BlockSpec((SUBLANES, a.shape[1]), idx))
        args.append(a)
    for a in seqvecs:
        in_specs.append(pl.BlockSpec((1, 1, a.shape[2]), lambda i: (i // spb, 0, 0)))
        args.append(a)
    for a in consts:
        in_specs.append(pl.BlockSpec(a.shape, lambda i, nd=a.ndim: (0,) * nd))
        args.append(a)
    out_shape, out_specs = [], []
    for w, dt in outs:
        out_shape.append(jax.ShapeDtypeStruct((T, w), dt))
        out_specs.append(pl.BlockSpec((tm, w), lambda i: (i, 0)))
    for w in accs:
        out_shape.append(jax.ShapeDtypeStruct((SUBLANES, w), F32))
        out_specs.append(pl.BlockSpec((SUBLANES, w), lambda i: (0, 0)))
    for w in seqaccs:
        out_shape.append(jax.ShapeDtypeStruct((nseq, SUBLANES, w), F32))
        out_specs.append(pl.BlockSpec((1, SUBLANES, w), lambda i: (i // spb, 0, 0)))
    n_t, n_h, n_s, n_c = len(tiles), len(halos), len(seqvecs), len(consts)
    n_o, n_a, n_sa = len(outs), len(accs), len(seqaccs)

    def body(*refs):
        i = pl.program_id(0)
        ins = refs[:n_t + n_h + n_s + n_c]
        orefs = refs[n_t + n_h + n_s + n_c:]
        vals = [r[...] for r in ins[:n_t + n_h]]
        vals += [r[0] for r in ins[n_t + n_h:n_t + n_h + n_s]]
        vals += list(ins[n_t + n_h + n_s:])
        res = fn(i, *vals)
        if not isinstance(res, (tuple, list)):
            res = (res,)
        for k in range(n_o):
            orefs[k][...] = res[k].astype(orefs[k].dtype)
        for k in range(n_a):
            r = orefs[n_o + k]

            @pl.when(i == 0)
            def _():
                r[...] = jnp.zeros_like(r)

            r[...] += res[n_o + k]

            @pl.when(i == nb - 1)
            def _():
                r[...] = jnp.broadcast_to(jnp.sum(r[...], axis=0, keepdims=True), r.shape)
        for k in range(n_sa):
            r = orefs[n_o + n_a + k]

            @pl.when(i % spb == 0)
            def _():
                r[...] = jnp.zeros_like(r)

            r[0] += res[n_o + n_a + k]

            @pl.when(i % spb == spb - 1)
            def _():
                r[0] = jnp.broadcast_to(jnp.sum(r[0], axis=0, keepdims=True), r.shape[1:])

    sem = ("arbitrary",) if (n_a or n_sa) else ("parallel",)
    res = pl.pallas_call(
        body, name=name, grid=(nb,), in_specs=in_specs, out_specs=out_specs, out_shape=out_shape,
        compiler_params=_cparams(sem),
    )(*args)
    return res


def _ln_fwd(r, g, b):
    mu = jnp.mean(r, axis=-1, keepdims=True)
    xc = r - mu
    var = jnp.mean(xc * xc, axis=-1, keepdims=True)
    rstd = lax.rsqrt(var + LN_EPS)
    n = xc * rstd
    return n * g + b, n, rstd


def _ln_bwd(dy, n, rstd, g):
    dn = dy * g
    return rstd * (dn - jnp.mean(dn, axis=-1, keepdims=True) - n * jnp.mean(dn * n, axis=-1, keepdims=True))


def _head_mean(t, g_ref):
    gw = g_ref.shape[0]
    hi = t.astype(BF16)
    lo = (t - hi.astype(F32)).astype(BF16)
    g = g_ref[...]
    parts = []
    for c in range(t.shape[1] // gw):
        sl = slice(c * gw, (c + 1) * gw)
        parts.append(_dot(hi[:, sl], g) + _dot(lo[:, sl], g))
    out = parts[0] if len(parts) == 1 else jnp.concatenate(parts, axis=1)
    return out * (1.0 / HD)


def _rope(z, c, s1, s2):
    w = z.shape[1]
    half = ROPE_DIMS // 2
    return z * c + pltpu.roll(z, half, 1) * s1 + pltpu.roll(z, w - half, 1) * s2


def _tile_lanes(t, w):
    reps = w // t.shape[1]
    return t if reps == 1 else jnp.concatenate([t] * reps, axis=1)


def _conv_taps(ext, prev, first):
    tm = ext.shape[0]
    prev = jnp.where(first, jnp.zeros_like(prev), prev)
    r8 = lax.broadcasted_iota(jnp.int32, (SUBLANES, 1), 0)
    top = ext[0:SUBLANES]
    s1_top = jnp.where(r8 < 1, pltpu.roll(prev, 1, 0), pltpu.roll(top, 1, 0))
    s2_top = jnp.where(r8 < 2, pltpu.roll(prev, 2, 0), pltpu.roll(top, 2, 0))
    s1 = jnp.concatenate([s1_top, pltpu.roll(ext, 1, 0)[SUBLANES:]], axis=0)
    s2 = jnp.concatenate([s2_top, pltpu.roll(ext, 2, 0)[SUBLANES:]], axis=0)
    return s1, s2


def _conv_taps_up(ext, nxt, last):
    tm = ext.shape[0]
    nxt = jnp.where(last, jnp.zeros_like(nxt), nxt)
    r8 = lax.broadcasted_iota(jnp.int32, (SUBLANES, 1), 0)
    bot = ext[tm - SUBLANES:tm]
    u1_bot = jnp.where(r8 >= 7, pltpu.roll(nxt, 7, 0), pltpu.roll(bot, 7, 0))
    u2_bot = jnp.where(r8 >= 6, pltpu.roll(nxt, 6, 0), pltpu.roll(bot, 6, 0))
    u1 = jnp.concatenate([pltpu.roll(ext, tm - 1, 0)[:tm - SUBLANES], u1_bot], axis=0)
    u2 = jnp.concatenate([pltpu.roll(ext, tm - 2, 0)[:tm - SUBLANES], u2_bot], axis=0)
    return u1, u2


def _mm_nt(name, a, w, tm=256):
    T = a.shape[0]
    n = w.shape[0]
    ch = _pick(n, (512, 256, 128))

    def fn(i, av, w_ref):
        ab = av.astype(BF16)
        parts = [_dot_nt(ab, w_ref[c * ch:(c + 1) * ch, :]) for c in range(n // ch)]
        return parts[0] if len(parts) == 1 else jnp.concatenate(parts, axis=1)

    return _rowwise(name, fn, T, tm, tiles=(a,), consts=(w,), outs=((n, F32),))[0]


def _mm_tn(name, a, b, *, mod=None, tt=512, t2=None, by_chip=False):
    T, k1 = a.shape
    k2 = b.shape[1]
    t1 = k1 if k1 <= 1536 else _pick(k1, (1408, 1024, 512, 256, 128))
    if t2 is None:
        t2 = k2 if k2 <= 1536 else _pick(k2, (1408, 1024, 640, 512, 256, 128))
    wc = k2 // NCHIP
    if by_chip:
        t2 = 2 * wc
    tt = min(tt, S)
    spb = S // tt

    def body(*refs):
        if mod is not None:
            a_ref, sc_ref, sh_ref, b_ref, o_ref = refs
        else:
            a_ref, b_ref, o_ref = refs
        t = pl.program_id(2)

        @pl.when(t == 0)
        def _():
            o_ref[...] = jnp.zeros_like(o_ref)

        av = a_ref[...]
        if mod is not None:
            av = av * (1.0 + sc_ref[0]) + sh_ref[0]
        res = _dot_tn(av.astype(BF16), b_ref[...].astype(BF16))
        if by_chip:
            o_ref[0] += res[:, :wc]
            o_ref[1] += res[:, wc:]
        else:
            o_ref[...] += res

    in_specs = [pl.BlockSpec((tt, t1), lambda p, q, t: (t, p))]
    args = [a]
    if mod is not None:
        for v in mod:
            in_specs.append(pl.BlockSpec((1, 1, t1), lambda p, q, t: (t // spb, 0, p)))
            args.append(v)
    in_specs.append(pl.BlockSpec((tt, t2), lambda p, q, t: (t, q)))
    args.append(b)
    if by_chip:
        out_specs = pl.BlockSpec((2, t1, wc), lambda p, q, t: (q, p, 0))
        out_shape = jax.ShapeDtypeStruct((NCHIP, k1, wc), F32)
    else:
        out_specs = pl.BlockSpec((t1, t2), lambda p, q, t: (p, q))
        out_shape = jax.ShapeDtypeStruct((k1, k2), F32)
    return pl.pallas_call(
        body, name=name, grid=(k1 // t1, k2 // t2, T // tt), in_specs=in_specs, out_specs=out_specs,
        out_shape=out_shape, compiler_params=_cparams(("parallel", "parallel", "arbitrary")),
    )(*args)


def _mod_mm(name, x, sc, sh, ws, out_dtypes, rope=None, rope_secs=(), tm=256):
    T = x.shape[0]
    nw = len(ws)

    def fn(i, xv, *rest):
        if rope is not None:
            cv, s1v, s2v = rest[:3]
            rest = rest[3:]
        scv, shv = rest[:2]
        w_refs = rest[2:]
        h = (xv * (1.0 + scv) + shv).astype(BF16)
        res = []
        for k, w_ref in enumerate(w_refs):
            n = w_ref.shape[1]
            ch = WA if (k == 0 and rope is not None) else _pick(n, (512, 256, 128))
            parts = []
            for c in range(n // ch):
                z = _dot(h, w_ref[:, c * ch:(c + 1) * ch])
                if k == 0 and c in rope_secs:
                    z = _rope(z, _tile_lanes(cv, ch), _tile_lanes(s1v, ch), _tile_lanes(s2v, ch))
                parts.append(z.astype(out_dtypes[k]))
            res.append(parts[0] if len(parts) == 1 else jnp.concatenate(parts, axis=1))
        return tuple(res)

    tiles = (x,) + (tuple(rope) if rope is not None else ())
    outs = tuple((w.shape[1], dt) for w, dt in zip(ws, out_dtypes))
    return _rowwise(name, fn, T, tm, tiles=tiles, seqvecs=(sc, sh), consts=tuple(ws), outs=outs)


def _tri(tb, lower):
    r = lax.broadcasted_iota(jnp.int32, (tb, tb), 0)
    c = lax.broadcasted_iota(jnp.int32, (tb, tb), 1)
    return jnp.where((r >= c) if lower else (r <= c), 1.0, 0.0).astype(BF16)


def _split3(x):
    hi = x.astype(BF16)
    r = x - hi.astype(F32)
    mid = r.astype(BF16)
    return hi, mid, (r - mid.astype(F32)).astype(BF16)


def _cumsum_seq(name, ins, consts, fn_in, fn_out, outs, reverse, n_acc=0, tb=256):
    T = ins[0].shape[0]
    tb = min(tb, S)
    nbs = S // tb
    nseq = T // S
    n_i, n_c, n_o = len(ins), len(consts), len(outs)

    def blk(b, j):
        return (b * nbs + (nbs - 1 - j if reverse else j), 0)

    def body(*refs):
        i_refs, c_refs = refs[:n_i], refs[n_i:n_i + n_c]
        o_refs = refs[n_i + n_c:n_i + n_c + n_o]
        acc_refs = refs[n_i + n_c + n_o:n_i + n_c + n_o + n_acc]
        carry = refs[-1]
        b, j = pl.program_id(0), pl.program_id(1)

        @pl.when(j == 0)
        def _():
            carry[...] = jnp.zeros_like(carry)

        iv = [r[...] for r in i_refs]
        xin = fn_in(*iv, *c_refs)
        tri = _tri(tb, not reverse)
        cum = sum(_dot(tri, piece) for piece in _split3(xin)) + carry[0:1, :]
        carry[...] = carry[...] + jnp.sum(xin, axis=0, keepdims=True)
        res = fn_out(cum, *iv, *c_refs)
        for o, r in zip(o_refs, res):
            o[...] = r.astype(o.dtype)
        for a in acc_refs:
            @pl.when((b == 0) & (j == 0))
            def _():
                a[...] = jnp.zeros_like(a)

            a[...] += _rsum8(res[0])

            @pl.when((b == nseq - 1) & (j == nbs - 1))
            def _():
                a[...] = jnp.broadcast_to(jnp.sum(a[...], axis=0, keepdims=True), a.shape)

    in_specs = [pl.BlockSpec((tb, a.shape[1]), blk) for a in ins]
    in_specs += [pl.BlockSpec(c.shape, lambda b, j, nd=c.ndim: (0,) * nd) for c in consts]
    out_shape = [jax.ShapeDtypeStruct((T, w), dt) for w, dt in outs]
    out_shape += [jax.ShapeDtypeStruct((SUBLANES, outs[0][0]), F32)] * n_acc
    out_specs = [pl.BlockSpec((tb, w), blk) for w, _ in outs]
    out_specs += [pl.BlockSpec((SUBLANES, outs[0][0]), lambda b, j: (0, 0))] * n_acc
    return pl.pallas_call(
        body, name=name, grid=(nseq, nbs), in_specs=in_specs, out_specs=out_specs, out_shape=out_shape,
        scratch_shapes=[pltpu.VMEM((SUBLANES, FPAD), F32)],
        compiler_params=_cparams(("arbitrary", "arbitrary")),
    )(*ins, *consts)


def _log_sigmoid(x):
    return jnp.minimum(x, 0.0) - jnp.log(1.0 + jnp.exp(-jnp.abs(x)))


def _dil_bias(tq):
    max_win = max(w for w, _ in PATTERNS)
    nd = (max_win + tq - 1) // tq + 1
    qi = np.arange(tq)[:, None]
    kj = np.arange(tq)[None, :]
    tabs = []
    for dlt in range(nd):
        dist = dlt * tq + qi - kj
        mult = np.zeros((tq, tq), np.float64)
        for win, dil in PATTERNS:
            mult += (dist >= 0) & (dist % dil == 0) & (dist // dil <= win // dil)
        tabs.append(np.where(mult > 0, np.log(np.maximum(mult, 1.0)), NEG))
    return np.stack(tabs).astype(np.float32)


def _fold_tables(nha):
    hp_n = nha // 2
    pq = np.zeros((3, FPAD, hp_n * 2 * LANES), np.float32)
    pk = np.zeros((3, FPAD, hp_n * LANES), np.float32)
    oq = np.zeros((1, hp_n * 2 * LANES), np.float32)
    ok = np.zeros((1, hp_n * LANES), np.float32)
    sq = np.zeros((hp_n * LANES, FPAD), np.float32)
    sk = np.zeros((hp_n * LANES, FPAD), np.float32)
    for h in range(nha):
        hp, odd = divmod(h, 2)
        qb = hp * 2 * LANES + odd * (LANES + 8)
        kb = hp * LANES + odd * 8
        for i in range(3):
            pq[i, h, qb + i] = 1
            oq[0, qb + 3 + i] = 1
            ok[0, kb + i] = 1
            pk[i, h, kb + 3 + i] = 1
        sq[kb, h] = 1
        sk[kb + 3, h] = 1
    return pq, pk, oq, ok, sq, sk


def _stack_heads(x2, h0, extra=None):
    z = jnp.zeros_like(x2)
    a, b = jnp.where(h0, x2, z), jnp.where(h0, z, x2)
    if extra is not None:
        a = jnp.concatenate([a, extra[:, :LANES]], axis=1)
        b = jnp.concatenate([b, extra[:, LANES:]], axis=1)
    return jnp.concatenate([a, b], axis=0)


def _attn_fwd(name, qkv, secs, fox, eq=None, ek=None, bias=None, ride=()):
    T = qkv.shape[0]
    nq = S // TQ
    nbl = T // S
    hp_n = WA // LANES
    sq, sk, sv = (s * hp_n for s in secs)
    scale = HD ** -0.5
    nd = None if fox else bias.shape[0]

    n_r = len(ride)
    n_in = (5 if fox else 4) + n_r

    def body(*refs):
        if fox:
            q_ref, k_ref, v_ref, eq_ref, ek_ref = refs[:5]
        else:
            q_ref, k_ref, v_ref, b_ref = refs[:4]
        o_ref, lse_ref = refs[n_in:n_in + 2]
        i = pl.program_id(2)
        if n_r:
            ride_start, ride_finish = _gather_halves_steps(
                refs[n_in - n_r:n_in], refs[n_in + 2:n_in + 2 + n_r], *refs[n_in + 2 + n_r:])
            at = lambda b, hp, q: (pl.program_id(0) == b) & (pl.program_id(1) == hp) & (i == q)
            pl.when(at(0, 0, 0))(ride_start)
        lane = lax.broadcasted_iota(jnp.int32, (1, LANES), 1)
        h0 = lane < HD
        q2 = (q_ref[...].astype(F32) * scale).astype(BF16)
        qs = _stack_heads(q2, h0, eq_ref[...] if fox else None)

        def scores(t, diag):
            off = pl.multiple_of((i - t) * TQ, TQ)
            kk = k_ref[pl.ds(off, TQ), :]
            if fox:
                kk = jnp.concatenate([kk, ek_ref[pl.ds(off, TQ), :]], axis=1)
            s = jnp.concatenate([_dot_nt(qs[:TQ], kk), _dot_nt(qs[TQ:], kk)], axis=0)
            if not fox:
                s = (s.reshape(2, TQ, TQ) + b_ref[t]).reshape(2 * TQ, TQ)
            elif diag:
                rows = lax.broadcasted_iota(jnp.int32, (2, TQ, TQ), 1).reshape(2 * TQ, TQ)
                cols = lax.broadcasted_iota(jnp.int32, (2 * TQ, TQ), 1)
                s = jnp.where(cols <= rows, s, NEG)
            return s

        def update(t, s, m, l, acc):
            off = pl.multiple_of((i - t) * TQ, TQ)
            v2 = v_ref[pl.ds(off, TQ), :]
            m_new = jnp.maximum(m, jnp.max(s, axis=1, keepdims=True))
            p = jnp.exp(s - m_new)
            a = jnp.exp(m - m_new)
            l = a * l + jnp.sum(p, axis=1, keepdims=True)
            pb = p.astype(BF16)
            acc = a * acc + jnp.concatenate([_dot(pb[:TQ], v2), _dot(pb[TQ:], v2)], axis=0)
            return m_new, l, acc

        init = (jnp.full((2 * TQ, 1), NEG, F32), jnp.zeros((2 * TQ, 1), F32), jnp.zeros((2 * TQ, LANES), F32))
        n = i + 1 if fox else jnp.minimum(i + 1, nd)
        m, l, acc = update(0, scores(0, True), *init)
        m, l, acc = lax.fori_loop(1, n, lambda t, c: update(t, scores(t, False), *c), (m, l, acc))
        on = acc / l
        o_ref[...] = jnp.where(h0, on[:TQ], on[TQ:])
        lse = jnp.broadcast_to(m + jnp.log(l), (2 * TQ, LANES))
        lse_ref[...] = jnp.concatenate([lse[:TQ], lse[TQ:]], axis=1)
        if n_r:
            pl.when(at(nbl - 1, hp_n - 1, nq - 1))(ride_finish)

    in_specs = [
        pl.BlockSpec((TQ, LANES), lambda b, hp, i: (b * nq + i, sq + hp)),
        pl.BlockSpec((S, LANES), lambda b, hp, i: (b, sk + hp)),
        pl.BlockSpec((S, LANES), lambda b, hp, i: (b, sv + hp)),
    ]
    args = [qkv, qkv, qkv]
    if fox:
        in_specs += [pl.BlockSpec((TQ, 2 * LANES), lambda b, hp, i: (b * nq + i, hp)),
                     pl.BlockSpec((S, LANES), lambda b, hp, i: (b, hp))]
        args += [eq, ek]
    else:
        in_specs.append(pl.BlockSpec(bias.shape, lambda b, hp, i: (0, 0, 0)))
        args.append(bias)
    any_spec = pl.BlockSpec(memory_space=pl.ANY)
    out_specs = [pl.BlockSpec((TQ, LANES), lambda b, hp, i: (b * nq + i, hp)),
                 pl.BlockSpec((TQ, 2 * LANES), lambda b, hp, i: (b * nq + i, hp))] + [any_spec] * n_r
    out_shape = [jax.ShapeDtypeStruct((T, WA), F32), jax.ShapeDtypeStruct((T, 2 * WA), F32)]
    out_shape += [jax.ShapeDtypeStruct((NCHIP - 1,) + v.shape, v.dtype) for v in ride]
    sems = [pltpu.SemaphoreType.DMA((6 * n_r,)), pltpu.SemaphoreType.DMA((6 * n_r,))] if n_r else []
    sem = ("arbitrary",) * 3 if n_r else ("parallel", "parallel", "arbitrary")
    return pl.pallas_call(
        body, name=name, grid=(nbl, hp_n, nq), in_specs=in_specs + [any_spec] * n_r, out_specs=out_specs,
        out_shape=out_shape, scratch_shapes=sems, compiler_params=_cparams(sem),
    )(*args, *ride)


def _attn_bwd(name, qkv, secs, o, do, do_sec, lse, fox, eq=None, ek=None, bias=None):
    T = qkv.shape[0]
    nq = S // TQ
    nbl = T // S
    hp_n = WA // LANES
    sq, sk, sv = (s * hp_n for s in secs)
    dsec = do_sec * hp_n
    scale = HD ** -0.5
    nd = None if fox else bias.shape[0]
    kc = 2 * LANES if fox else LANES

    def body(*refs):
        if fox:
            (q_ref, k_ref, v_ref, o_ref, do_ref, lse_ref, eq_ref, ek_ref,
             dq_ref, dk_ref, dv_ref, dqe_ref, dek_ref, dl_ref) = refs
        else:
            q_ref, k_ref, v_ref, o_ref, do_ref, lse_ref, b_ref, dq_ref, dk_ref, dv_ref, dl_ref = refs
        j = pl.program_id(2)
        lane = lax.broadcasted_iota(jnp.int32, (1, LANES), 1)
        h0 = lane < HD

        @pl.when(j == 0)
        def _():
            dq_ref[...] = jnp.zeros_like(dq_ref)
            if fox:
                dqe_ref[...] = jnp.zeros_like(dqe_ref)

            def dl_step(r, c):
                off = pl.multiple_of(r * TQ, TQ)
                d2 = do_ref[pl.ds(off, TQ), :] * o_ref[pl.ds(off, TQ), :]
                z2 = jnp.zeros_like(d2)
                dl0 = jnp.sum(jnp.where(h0, d2, z2), axis=1, keepdims=True)
                dl1 = jnp.sum(jnp.where(h0, z2, d2), axis=1, keepdims=True)
                dl_ref[pl.ds(off, TQ), :] = jnp.concatenate(
                    [jnp.broadcast_to(dl0, (TQ, LANES)), jnp.broadcast_to(dl1, (TQ, LANES))], axis=1)
                return c

            lax.fori_loop(0, nq, dl_step, 0)

        kk = k_ref[...]
        if fox:
            kk = jnp.concatenate([kk, ek_ref[...]], axis=1)
        v2 = v_ref[...]

        def wide(x2):
            st = jnp.concatenate([x2[:, :LANES], x2[:, LANES:]], axis=0)
            return st if TQ == LANES else jnp.concatenate([st] * (TQ // LANES), axis=1)

        def step(t, carry, diag):
            dkk, dv2 = carry
            off = pl.multiple_of((j + t) * TQ, TQ)
            q2 = (q_ref[pl.ds(off, TQ), :].astype(F32) * scale).astype(BF16)
            qs = _stack_heads(q2, h0, eq_ref[pl.ds(off, TQ), :] if fox else None)
            dos = _stack_heads(do_ref[pl.ds(off, TQ), :].astype(BF16), h0)
            s = jnp.concatenate([_dot_nt(qs[:TQ], kk), _dot_nt(qs[TQ:], kk)], axis=0)
            if not fox:
                s = (s.reshape(2, TQ, TQ) + b_ref[t]).reshape(2 * TQ, TQ)
            elif diag:
                rows = lax.broadcasted_iota(jnp.int32, (2, TQ, TQ), 1).reshape(2 * TQ, TQ)
                cols = lax.broadcasted_iota(jnp.int32, (2 * TQ, TQ), 1)
                s = jnp.where(cols <= rows, s, NEG)
            p = jnp.exp(s - wide(lse_ref[pl.ds(off, TQ), :]))
            dp = jnp.concatenate([_dot_nt(dos[:TQ], v2), _dot_nt(dos[TQ:], v2)], axis=0)
            dsb = (p * (dp - wide(dl_ref[pl.ds(off, TQ), :]))).astype(BF16)
            dv2 = dv2 + _dot_tn(p.astype(BF16), dos)
            dkk = dkk + _dot_tn(dsb, qs)
            dqq = jnp.concatenate([_dot(dsb[:TQ], kk), _dot(dsb[TQ:], kk)], axis=0)
            dq_ref[pl.ds(off, TQ), :] += jnp.where(h0, dqq[:TQ, :LANES], dqq[TQ:, :LANES])
            if fox:
                dqe_ref[pl.ds(off, TQ), :] += jnp.where(lane < SUBLANES, dqq[:TQ, LANES:], dqq[TQ:, LANES:])
            return dkk, dv2

        zero = (jnp.zeros((TQ, kc), F32), jnp.zeros((TQ, LANES), F32))
        if fox:
            dkk, dv2 = lax.fori_loop(1, nq - j, lambda t, c: step(t, c, False), step(0, zero, True))
        else:
            dkk, dv2 = lax.fori_loop(0, jnp.minimum(nq - j, nd), lambda t, c: step(t, c, False), zero)
        dk_ref[...] = dkk[:, :LANES]
        dv_ref[...] = dv2
        if fox:
            dek_ref[...] = dkk[:, LANES:]

        @pl.when(j == nq - 1)
        def _():
            dq_ref[...] = dq_ref[...] * scale

    seq = lambda c, w=LANES: pl.BlockSpec((S, w), lambda b, hp, j: (b, c + hp))
    blk = lambda c: pl.BlockSpec((TQ, LANES), lambda b, hp, j: (b * nq + j, c + hp))
    in_specs = [seq(sq), blk(sk), blk(sv), seq(0), seq(dsec), seq(0, 2 * LANES)]
    args = [qkv, qkv, qkv, o, do, lse]
    if fox:
        in_specs += [seq(0, 2 * LANES), blk(0)]
        args += [eq, ek]
    else:
        in_specs.append(pl.BlockSpec(bias.shape, lambda b, hp, j: (0, 0, 0)))
        args.append(bias)
    out_specs = [seq(0), blk(0), blk(0)]
    out_shape = [jax.ShapeDtypeStruct((T, WA), F32)] * 3
    if fox:
        out_specs += [seq(0), blk(0)]
        out_shape += [jax.ShapeDtypeStruct((T, WA), F32)] * 2
    return pl.pallas_call(
        body, name=name, grid=(nbl, hp_n, nq), in_specs=in_specs, out_specs=out_specs, out_shape=out_shape,
        scratch_shapes=[pltpu.VMEM((S, 2 * LANES), F32)],
        compiler_params=_cparams(("parallel", "parallel", "arbitrary")),
    )(*args)


def _exchange(name, ins, out_shapes, remote, local):
    n_in, n_out = len(ins), len(out_shapes)
    nr, nl = len(remote), len(local)

    def body(*refs):
        in_refs = refs[:n_in]
        out_refs = refs[n_in:n_in + n_out]
        send_sems, recv_sems, loc_sems = refs[n_in + n_out:]
        me = (lax.axis_index("x"), lax.axis_index("y"), lax.axis_index("c"))

        def peer_of(flip):
            return tuple(1 - v if f else v for v, f in zip(me, flip))

        def at(ref, idx):
            return ref if idx is None else ref.at[idx]

        def rcopy(k, who):
            flip, a, sfn, b, dfn = remote[k]
            return pltpu.make_async_remote_copy(
                src_ref=at(in_refs[a], sfn(*who)), dst_ref=at(out_refs[b], dfn(*who)),
                send_sem=send_sems.at[k], recv_sem=recv_sems.at[k],
                device_id=peer_of(flip), device_id_type=MESH)

        locs = [pltpu.make_async_copy(at(in_refs[a], sfn(*me)), at(out_refs[b], dfn(*me)), loc_sems.at[k])
                for k, (a, sfn, b, dfn) in enumerate(local)]
        for cp in locs:
            cp.start()
        sends = [rcopy(k, me) for k in range(nr)]
        for cp in sends:
            cp.start()
        for k in range(nr):
            rcopy(k, peer_of(remote[k][0])).wait_recv()
        for cp in sends:
            cp.wait_send()
        for cp in locs:
            cp.wait()

    any_spec = pl.BlockSpec(memory_space=pl.ANY)
    return pl.pallas_call(
        body, name=name, in_specs=[any_spec] * n_in, out_specs=[any_spec] * n_out, out_shape=list(out_shapes),
        scratch_shapes=[pltpu.SemaphoreType.DMA((max(nr, 1),)), pltpu.SemaphoreType.DMA((max(nr, 1),)),
                        pltpu.SemaphoreType.DMA((max(nl, 1),))],
    )(*ins)


_FLIPS7 = [(0, 0, 1), (0, 1, 0), (0, 1, 1), (1, 0, 0), (1, 0, 1), (1, 1, 0), (1, 1, 1)]
_CHIP_FLIPS = [(1, 0, 0), (0, 1, 0), (1, 1, 0)]


def _dev_index(x, y, c):
    return 4 * x + 2 * y + c


def _chip_index(x, y, c):
    return 2 * x + y


def _all_gather8(name, v):
    remote = [(f, 0, lambda x, y, c: None, 0, _dev_index) for f in _FLIPS7]
    local = [(0, lambda x, y, c: None, 0, _dev_index)]
    return _exchange(name, [v], [jax.ShapeDtypeStruct((NDEV,) + v.shape, v.dtype)], remote, local)[0]


def _gather_halves_steps(in_refs, out_refs, send_sems, recv_sems):
    n_v = len(in_refs)
    x, y, c = lax.axis_index("x"), lax.axis_index("y"), lax.axis_index("c")
    sibling = (x, y, 1 - c)
    chips = [(1 - x, y), (x, 1 - y), (1 - x, 1 - y)]

    def copy(k, n, src, blk, half, to):
        return pltpu.make_async_remote_copy(
            src_ref=src, dst_ref=out_refs[n].at[blk, half], send_sem=send_sems.at[k], recv_sem=recv_sems.at[k],
            device_id=to, device_id_type=MESH)

    def first():
        return [copy(6 * n + j, n, in_refs[n].at[c], j, c, (*chip, c))
                for n in range(n_v) for j, chip in enumerate(chips)]

    def start():
        for cp in first():
            cp.start()

    def finish():
        passed = []
        for n in range(n_v):
            for j, chip in enumerate(chips):
                copy(6 * n + j, n, in_refs[n].at[c], j, c, (*chip, c)).wait_recv()
                fw = copy(6 * n + 3 + j, n, out_refs[n].at[j, c], j, c, sibling)
                fw.start()
                passed.append(fw)
        for n in range(n_v):
            for j in range(len(chips)):
                copy(6 * n + 3 + j, n, out_refs[n].at[j, 1 - c], j, 1 - c, sibling).wait_recv()
        for cp in first() + passed:
            cp.wait_send()

    return start, finish


def _gather_halves(name, vs):
    n_v = len(vs)

    def body(*refs):
        start, finish = _gather_halves_steps(refs[:n_v], refs[n_v:2 * n_v], *refs[2 * n_v:])
        start()
        finish()

    any_spec = pl.BlockSpec(memory_space=pl.ANY)
    return pl.pallas_call(
        body, name=name, in_specs=[any_spec] * n_v, out_specs=[any_spec] * n_v,
        out_shape=[jax.ShapeDtypeStruct((NCHIP - 1,) + v.shape, v.dtype) for v in vs],
        scratch_shapes=[pltpu.SemaphoreType.DMA((6 * n_v,)), pltpu.SemaphoreType.DMA((6 * n_v,))],
    )(*vs)


def _by_chip(own, others, chip):
    stacked = jnp.concatenate([own[None], others], axis=0)
    blocks = []
    for k in range(NCHIP):
        d = k ^ chip
        place = jnp.where(d == 0, 0, jnp.where(d == 2, 1, jnp.where(d == 1, 2, 3)))
        blocks.append(lax.dynamic_index_in_dim(stacked, place, axis=0, keepdims=False))
    return jnp.stack(blocks)


def _sum_leading(name, v, tm=None):
    n, r, w = v.shape
    tm = _pick(r, (256, 128, 64, 32, 16, 8)) if tm is None else tm

    def body(v_ref, o_ref):
        acc = v_ref[0].astype(F32)
        for k in range(1, n):
            acc = acc + v_ref[k].astype(F32)
        o_ref[...] = acc

    return pl.pallas_call(
        body, name=name, grid=(r // tm,), in_specs=[pl.BlockSpec((n, tm, w), lambda i: (0, i, 0))],
        out_specs=pl.BlockSpec((tm, w), lambda i: (i, 0)), out_shape=jax.ShapeDtypeStruct((r, w), F32),
        compiler_params=_cparams(("parallel",)),
    )(v)


def _add2(name, a, b, tm=None, out_dtype=F32):
    r, w = a.shape
    tm = _pick(r, (256, 128, 64, 32, 16, 8)) if tm is None else tm

    def body(a_ref, b_ref, o_ref):
        o_ref[...] = (a_ref[...] + b_ref[...]).astype(out_dtype)

    spec = pl.BlockSpec((tm, w), lambda i: (i, 0))
    return pl.pallas_call(
        body, name=name, grid=(r // tm,), in_specs=[spec, spec], out_specs=spec,
        out_shape=jax.ShapeDtypeStruct((r, w), out_dtype), compiler_params=_cparams(("parallel",)),
    )(a, b)


def _ada_fwd(call_all, w_shard):
    def body(c_ref, w_ref, o_ref):
        cv = c_ref[...]
        o_ref[...] = jnp.dot(cv * _sigmoid(cv), w_ref[...], preferred_element_type=F32,
                             precision=lax.Precision.HIGHEST)

    n = w_shard.shape[1]
    return pl.pallas_call(
        body, name="ada_fwd", out_shape=jax.ShapeDtypeStruct((call_all.shape[0], n), F32),
        compiler_params=pltpu.CompilerParams(vmem_limit_bytes=VMEM_LIMIT),
    )(call_all, w_shard)


def _ada_bwd(call_all, dada):
    def body(c_ref, d_ref, o_ref):
        cv = c_ref[...]
        o_ref[...] = lax.dot_general(cv * _sigmoid(cv), d_ref[...], (((0,), (0,)), ((), ())),
                                     preferred_element_type=F32, precision=lax.Precision.HIGHEST)

    return pl.pallas_call(
        body, name="ada_bwd", out_shape=jax.ShapeDtypeStruct((call_all.shape[1], dada.shape[1]), F32),
        compiler_params=pltpu.CompilerParams(vmem_limit_bytes=VMEM_LIMIT),
    )(call_all, dada)


def _adamw(name, w, g, m, v):
    r, wd = w.shape
    tm = _pick(r, (256, 128, 64, 32, 16, 8))
    bc1 = 1.0 - ADAM_B1 ** ADAM_STEP
    bc2 = 1.0 - ADAM_B2 ** ADAM_STEP

    def body(w_ref, g_ref, m_ref, v_ref, d_ref, mo_ref, vo_ref):
        gv = g_ref[...]
        mn = ADAM_B1 * m_ref[...] + (1.0 - ADAM_B1) * gv
        vn = ADAM_B2 * v_ref[...] + (1.0 - ADAM_B2) * (gv * gv)
        d_ref[...] = -ADAM_LR * ((mn / bc1) / (jnp.sqrt(vn / bc2) + ADAM_EPS) + ADAM_WD * w_ref[...])
        mo_ref[...] = mn
        vo_ref[...] = vn

    spec = pl.BlockSpec((tm, wd), lambda i: (i, 0))
    return pl.pallas_call(
        body, name=name, grid=(r // tm,), in_specs=[spec] * 4, out_specs=[spec] * 3,
        out_shape=[jax.ShapeDtypeStruct((r, wd), F32)] * 3, compiler_params=_cparams(("parallel",)),
    )(w, g, m, v)


def _rope_tables(positions):
    half = ROPE_DIMS // 2
    freqs = ROPE_THETA ** (-jnp.arange(0, ROPE_DIMS, 2, dtype=F32) / ROPE_DIMS)
    ang = positions.astype(F32).reshape(-1, 1) * freqs
    cos, sin = jnp.cos(ang), jnp.sin(ang)
    T = ang.shape[0]
    one = jnp.ones((T, HD - ROPE_DIMS), F32)
    zero = jnp.zeros((T, HD - ROPE_DIMS), F32)
    zh = jnp.zeros((T, half), F32)
    c64 = jnp.concatenate([cos, cos, one], axis=1)
    s1 = jnp.concatenate([zh, sin, zero], axis=1)
    s2 = jnp.concatenate([-sin, zh, zero], axis=1)
    rep = lambda t: jnp.concatenate([t] * (LANES // HD), axis=1)
    return rep(c64), rep(s1), rep(s2)


def _local_step(x, loss_target, positions, ada, w_qkv, w_f, w_out, w_up, conv_w8, w_down,
                b_fgate, gn, ln1_g, ln1_b, conv_b, ln2_g, ln2_b, late=None):
    T = x.shape[0]
    nbl = T // S
    nha = WA // HD
    sv = lambda k: ada[:, k:k + 1, :]
    sh_a, sc_a, g_a, sh_f, sc_f, g_f = (sv(k) for k in range(6))
    rope = _rope_tables(positions)
    neg_rope = (rope[0], -rope[1], -rope[2])
    gseg = jnp.asarray(np.kron(np.eye(min(256, 2 * WA) // HD), np.ones((HD, HD))), BF16)
    bias = jnp.asarray(_dil_bias(TQ))
    bf_pad = jnp.zeros((1, FPAD), F32).at[:, :nha].set(b_fgate)

    qkv, fa = _mod_mm("qkv_proj", x, sc_a, sh_a, (w_qkv, w_f), (BF16, F32), rope=rope, rope_secs=(3, 4))
    pq, pk, oq, ok, sq, sk = _fold_tables(nha)

    def fold_out(cum, f, b_ref, pq_ref, pk_ref, oq_ref, ok_ref):
        hi, mid, lo = _split3(cum)
        eqv = _dot(hi, pq_ref[0]) + _dot(mid, pq_ref[1]) + _dot(lo, pq_ref[2]) + oq_ref[...]
        ekv = ok_ref[...] - (_dot(hi, pk_ref[0]) + _dot(mid, pk_ref[1]) + _dot(lo, pk_ref[2]))
        return eqv, ekv

    eq, ek = _cumsum_seq(
        "fgate_fwd", [fa], [bf_pad, jnp.asarray(pq, BF16), jnp.asarray(pk, BF16), jnp.asarray(oq), jnp.asarray(ok)],
        lambda f, b_ref, *_: _log_sigmoid(f + b_ref[...]), fold_out, ((2 * WA, BF16), (WA, BF16)), reverse=False)
    oa, lse_a, *got = _attn_fwd("fox_fwd", qkv, (0, 1, 2), True, eq=eq, ek=ek, ride=late[0] if late else ())
    if late:
        w_up, w_down = late[1](got)
    ob, lse_b = _attn_fwd("dil_fwd", qkv, (3, 4, 5), False, bias=bias)

    def mix_fn(i, oav, obv, xv, gav, gn_ref, g_ref, wo_ref, l1g_ref, l1b_ref):
        o = jnp.concatenate([oav, obv], axis=1)
        rs = lax.rsqrt(_head_mean(o * o, g_ref) + RMS_EPS)
        merged = (o * rs * gn_ref[...]).astype(BF16)
        mix = _dot(merged, wo_ref[...])
        x1, _, _ = _ln_fwd(ALPHA * xv + gav * mix, l1g_ref[...], l1b_ref[...])
        return merged, mix, x1

    merged, mix, x1 = _rowwise("mix_out", mix_fn, T, 256, tiles=(oa, ob, x), seqvecs=(g_a,),
                               consts=(gn, gseg, w_out, ln1_g, ln1_b),
                               outs=((2 * WA, BF16), (D, F32), (D, F32)))
    u = _mod_mm("ffn_up", x1, sc_f, sh_f, (w_up,), (F32,))[0]

    def conv_y(i, uv, prev, cw_ref, cb_ref, tm):
        first = (i * tm) % S == 0
        s1, s2 = _conv_taps(uv, prev, first)
        y = cb_ref[...] + cw_ref[0:1, :] * s2 + cw_ref[1:2, :] * s1 + cw_ref[2:3, :] * uv
        return y, s1, s2

    tmc = 128

    def gate_fn(i, uv, prev, cw_ref, cb_ref):
        y, _, _ = conv_y(i, uv, prev, cw_ref, cb_ref, tmc)
        a, g = y[:, :DFF], y[:, DFF:]
        return g * _sigmoid(g) * a, y

    act, yconv = _rowwise("conv_gate", gate_fn, T, tmc, tiles=(u,), halos=((u, -1),), consts=(conv_w8, conv_b),
                          outs=((DFF, BF16), (2 * DFF, F32)))

    def down_fn(i, actv, x1v, tgt, gfv, wd_ref, g2_ref, b2_ref):
        ffn = _dot(actv, wd_ref[...])
        y, n2, rstd = _ln_fwd(ALPHA * x1v + gfv * ffn, g2_ref[...], b2_ref[...])
        err = y - tgt
        dy = err * (1.0 / D)
        dr2 = _ln_bwd(dy, n2, rstd, g2_ref[...])
        return (dr2, gfv * dr2, _rsum8(err * err), _rsum8(dy * n2), _rsum8(dy), _rsum8(dr2 * ffn))

    dr2, dffn, loss_acc, d_ln2g, d_ln2b, d_gf = _rowwise(
        "ffn_down_loss", down_fn, T, 256, tiles=(act, x1, loss_target), seqvecs=(g_f,),
        consts=(w_down, ln2_g, ln2_b), outs=((D, F32), (D, F32)), accs=(D, D, D), seqaccs=(D,))

    def gate_conv_bwd_fn(i, uv, yv, dfv, y_nxt, df_nxt, cw_ref, wd_ref):
        last = ((i + 1) * tmc) % S == 0
        y = jnp.concatenate([yv, y_nxt], axis=0)
        df_ext = jnp.concatenate([dfv, df_nxt], axis=0).astype(BF16)
        ch = _pick(DFF, (256, 128))
        dav = jnp.concatenate([_dot_nt(df_ext, wd_ref[c * ch:(c + 1) * ch, :]) for c in range(DFF // ch)], axis=1)
        a, g = y[:, :DFF], y[:, DFF:]
        sg = _sigmoid(g)
        dyc_ext = jnp.concatenate([dav * (g * sg), dav * a * (sg * (1.0 + g * (1.0 - sg)))], axis=1)
        dyc = dyc_ext[:tmc]
        u1, u2 = _conv_taps_up(dyc, dyc_ext[tmc:], last)
        du_ = cw_ref[2:3, :] * dyc + cw_ref[1:2, :] * u1 + cw_ref[0:1, :] * u2
        return du_, _rsum8(dyc), _rsum8(uv * u2), _rsum8(uv * u1), _rsum8(uv * dyc)

    du, d_cb, d_cw0, d_cw1, d_cw2 = _rowwise(
        "gate_conv_bwd", gate_conv_bwd_fn, T, tmc, tiles=(u, yconv, dffn), halos=((yconv, 1), (dffn, 1)),
        consts=(conv_w8, w_down), outs=((2 * DFF, BF16),), accs=(2 * DFF,) * 4)
    dh2 = _mm_nt("dh2", du, w_up)
    g_w_down = _mm_tn("dw_down", act, dffn)
    g_w_up = _mm_tn("dw_up", x1, du, mod=(sc_f, sh_f), by_chip=True)

    def ln1_bwd_fn(i, dr2v, dh2v, xv, mixv, x1v, scfv, gav, l1g_ref):
        dx1 = ALPHA * dr2v + dh2v * (1.0 + scfv)
        _, n1, rstd = _ln_fwd(ALPHA * xv + gav * mixv, l1g_ref[...], 0.0)
        dr1 = _ln_bwd(dx1, n1, rstd, l1g_ref[...])
        return (dr1, gav * dr1, _rsum8(dx1 * n1), _rsum8(dx1),
                _rsum8(dh2v * x1v), _rsum8(dh2v), _rsum8(dr1 * mixv))

    dr1, dmix, d_ln1g, d_ln1b, d_scf, d_shf, d_ga = _rowwise(
        "ln1_bwd", ln1_bwd_fn, T, 256, tiles=(dr2, dh2, x, mix, x1), seqvecs=(sc_f, g_a), consts=(ln1_g,),
        outs=((D, F32), (D, BF16)), accs=(D, D), seqaccs=(D, D, D))

    dmerged = _mm_nt("dmerged", dmix, w_out)
    g_w_out = _mm_tn("dw_out", merged, dmix)

    def hn_bwd_fn(i, dmv, oav, obv, gn_ref, g_ref):
        o = jnp.concatenate([oav, obv], axis=1)
        rs = lax.rsqrt(_head_mean(o * o, g_ref) + RMS_EPS)
        nrm = o * rs
        dn = dmv * gn_ref[...]
        do = rs * (dn - nrm * _head_mean(dn * nrm, g_ref))
        return do, _rsum8(dmv * nrm)

    do, d_gn = _rowwise("headnorm_bwd", hn_bwd_fn, T, 256, tiles=(dmerged, oa, ob), consts=(gn, gseg),
                        outs=((2 * WA, F32),), accs=(2 * WA,))
    dqa, dka, dva, dqe, dek = _attn_bwd("fox_bwd", qkv, (0, 1, 2), oa, do, 0, lse_a, True, eq=eq, ek=ek)
    dqb, dkb, dvb = _attn_bwd("dil_bwd", qkv, (3, 4, 5), ob, do, 1, lse_b, False, bias=bias)
    hdot = lambda a, m_ref: sum(_dot(piece, m_ref[...]) for piece in _split3(a))
    dfa, d_bf = _cumsum_seq(
        "fgate_bwd", [dqe, dek, fa], [bf_pad, jnp.asarray(sq, BF16), jnp.asarray(sk, BF16)],
        lambda dq_, dk_, f, b_ref, sq_ref, sk_ref: hdot(dq_, sq_ref) - hdot(dk_, sk_ref),
        lambda cum, dq_, dk_, f, b_ref, sq_ref, sk_ref: (cum * _sigmoid(-(f + b_ref[...])),),
        ((FPAD, F32),), reverse=True, n_acc=1)

    def dz_fn(i, a0, a1, a2, b0, b1, b2, fv, cv, s1v, s2v):
        ct, s1t, s2t = (_tile_lanes(t, WA) for t in (cv, s1v, s2v))
        return jnp.concatenate([a0, a1, a2, _rope(b0, ct, s1t, s2t), _rope(b1, ct, s1t, s2t), b2, fv], axis=1)

    dz = _rowwise("dz_pack", dz_fn, T, 256, tiles=(dqa, dka, dva, dqb, dkb, dvb, dfa) + neg_rope,
                  outs=((6 * WA + FPAD, BF16),))[0]
    w_cat = jnp.concatenate([w_qkv, w_f], axis=1)
    dh1 = _mm_nt("dh1", dz, w_cat)
    g_w_cat = _mm_tn("dw_in", x, dz, mod=(sc_a, sh_a), tt=256, t2=dz.shape[1])

    def dx_fn(i, dr1v, dh1v, xv, scav):
        return ALPHA * dr1v + dh1v * (1.0 + scav), _rsum8(dh1v * xv), _rsum8(dh1v)

    grad_x, d_sca, d_sha = _rowwise("dx_out", dx_fn, T, 256, tiles=(dr1, dh1, x), seqvecs=(sc_a,),
                                    outs=((D, F32),), seqaccs=(D, D))

    row0 = lambda a: a[..., 0, :]
    d_ada = jnp.stack([row0(d_sha), row0(d_sca), row0(d_ga), row0(d_shf), row0(d_scf), row0(d_gf)], axis=1)
    d_cw = jnp.stack([row0(d_cw0), row0(d_cw1), row0(d_cw2)], axis=0)
    loss_part = (0.5 / D) * jnp.sum(loss_acc[0])
    small = dict(b_fgate=row0(d_bf)[:nha], gn=row0(d_gn), ln1_g=row0(d_ln1g), ln1_b=row0(d_ln1b),
                 conv_b=row0(d_cb), ln2_g=row0(d_ln2g), ln2_b=row0(d_ln2b))
    big = dict(w_cat=g_w_cat, w_out=g_w_out, w_up=g_w_up, conv_w=d_cw, w_down=g_w_down)
    return loss_part, grad_x, d_ada, small, big


def _rows_of(n, w=None):
    return -(-n // (D if w is None else w))


def _as_rows(v):
    w = D
    k = v.shape[0]
    flat = v.reshape(k, -1)
    rows = _rows_of(_rows_of(flat.shape[1], w), SUBLANES) * SUBLANES
    flat = jnp.pad(flat, ((0, 0), (0, rows * w - flat.shape[1])))
    return flat.reshape(k, rows, w)


def kernel(x, c, positions, w_ada, b_ada, w_in, b_fgate, gn_a, gn_b, w_out, ln1_g, ln1_b, w_up, conv_w, conv_b, w_down, ln2_g, ln2_b, loss_target, m_w_ada, m_b_ada, m_w_in, m_b_fgate, m_gn_a, m_gn_b, m_w_out, m_ln1_g, m_ln1_b, m_w_up, m_conv_w, m_conv_b, m_w_down, m_ln2_g, m_ln2_b, v_w_ada, v_b_ada, v_w_in, v_b_fgate, v_gn_a, v_gn_b, v_w_out, v_ln1_g, v_ln1_b, v_w_up, v_conv_w, v_conv_b, v_w_down, v_ln2_g, v_ln2_b):
    mx, my, mc = lax.axis_index("x"), lax.axis_index("y"), lax.axis_index("c")
    dev = _dev_index(mx, my, mc)
    chip = _chip_index(mx, my, mc)
    nbl = x.shape[0]
    T = nbl * S
    nha = WA // HD
    d_in = w_in.shape[2] * NCHIP
    n_ada = w_ada.shape[2]

    c_pad = jnp.zeros((SUBLANES, D), F32).at[:nbl].set(c)
    c_all = _all_gather8("gather_c", c_pad)[:, :nbl].reshape(NDEV * nbl, D)
    ada_part = _ada_fwd(c_all, w_ada[0])
    n_cw = conv_w.shape[2]
    cw_rows = jnp.pad(conv_w[0], ((0, SUBLANES - conv_w.shape[1]), (0, n_ada - n_cw)))
    ada_blocks = _all_gather8("gather_ada", jnp.concatenate([ada_part, cw_rows], axis=0))
    n_c = NDEV * nbl
    ada_all = jnp.concatenate([ada_blocks[2 * k, :n_c] for k in range(NCHIP)], axis=1) + b_ada
    conv_w8 = jnp.concatenate([ada_blocks[2 * k, n_c:, :n_cw] for k in range(NCHIP)], axis=1)
    ada = lax.dynamic_slice_in_dim(ada_all, dev * nbl, nbl, axis=0).reshape(nbl, 6, D)

    w_in_sh = jnp.pad(w_in[0].astype(BF16), ((0, 0), (0, _rows_of(w_in.shape[2], LANES) * LANES - w_in.shape[2])))
    halve = lambda t: t.reshape(2, t.shape[0] // 2, t.shape[1])
    whole = lambda g, t: _by_chip(halve(t), g, chip).reshape((NCHIP,) + t.shape)
    w_out_sh, w_up_sh, w_down_sh = w_out[0].astype(BF16), w_up[0].astype(BF16), w_down[0].astype(BF16)
    g_in, g_out = _gather_halves("gather_w", [halve(w_in_sh), halve(w_out_sh)])
    g_in, g_out = whole(g_in, w_in_sh), whole(g_out, w_out_sh)
    w_in_full = jnp.concatenate([g_in[k][:, :w_in.shape[2]] for k in range(NCHIP)], axis=1)
    w_qkv = jnp.concatenate([w_in_full[:, :3 * WA], w_in_full[:, 3 * WA + nha:]], axis=1)
    w_f = jnp.pad(w_in_full[:, 3 * WA:3 * WA + nha], ((0, 0), (0, FPAD - nha)))
    w_out_full = g_out.reshape(NCHIP * w_out.shape[1], D)

    def late_weights(got):
        g_up, g_down = whole(got[0], w_up_sh), whole(got[1], w_down_sh)
        return (jnp.concatenate([g_up[k] for k in range(NCHIP)], axis=1),
                g_down.reshape(NCHIP * w_down.shape[1], D))

    gn = jnp.concatenate([gn_a, gn_b], axis=1)
    loss_part, grad_x, d_ada, small, big = _local_step(
        x.reshape(T, D), loss_target.reshape(T, D), positions, ada, w_qkv, w_f, w_out_full, None, conv_w8,
        None, b_fgate, gn, ln1_g, ln1_b, conv_b, ln2_g, ln2_b, late=([halve(w_up_sh), halve(w_down_sh)], late_weights))

    def row_pad(v, rows):
        flat = v.reshape(-1)
        return jnp.pad(flat, (0, rows * D - flat.shape[0]))

    n_cb = _rows_of(2 * DFF)
    small_flat = jnp.concatenate([
        row_pad(small["b_fgate"], 1), row_pad(small["gn"], 1), row_pad(small["ln1_g"], 1),
        row_pad(small["ln1_b"], 1), row_pad(small["ln2_g"], 1), row_pad(small["ln2_b"], 1),
        row_pad(jnp.full((1,), loss_part, F32), 1), row_pad(small["conv_b"], n_cb)])
    n_small = _rows_of(small_flat.shape[0], SUBLANES * D) * SUBLANES
    small_rows = jnp.pad(small_flat, (0, n_small * D - small_flat.shape[0])).reshape(n_small, D)
    ada_rows = jnp.pad(d_ada.reshape(nbl, 6, D), ((0, 0), (0, SUBLANES - 6), (0, 0))).reshape(nbl * SUBLANES, D)
    gathered = _all_gather8("gather_small", jnp.concatenate([small_rows, ada_rows], axis=0))
    red = _sum_leading("sum_small", gathered, tm=SUBLANES)
    g_b_fgate = red[0:1, :nha]
    g_gn = red[1:2, :2 * WA]
    g_ln1_g, g_ln1_b, g_ln2_g, g_ln2_b = red[2:3], red[3:4], red[4:5], red[5:6]
    loss = red[6, 0]
    g_conv_b = red[7:7 + n_cb].reshape(1, -1)[:, :2 * DFF]
    g_b_ada = _add2("sum_b_ada", red[n_small:n_small + SUBLANES], red[n_small + SUBLANES:n_small + 2 * SUBLANES],
                    tm=SUBLANES)[:6].reshape(1, 6 * D)
    dada_all = gathered[:, n_small:].reshape(NDEV, nbl, SUBLANES, D)[:, :, :6].reshape(NDEV * nbl, 6 * D)
    g_w_ada = _ada_bwd(c_all, lax.dynamic_slice_in_dim(dada_all, chip * n_ada, n_ada, axis=1))

    g_cat = big["w_cat"]
    g_w_in_full = jnp.concatenate([g_cat[:, :3 * WA], g_cat[:, 6 * WA:6 * WA + nha], g_cat[:, 3 * WA:6 * WA]], axis=1)
    n_in = w_in.shape[2]
    sh_in = jnp.pad(g_w_in_full.reshape(D, NCHIP, n_in).transpose(1, 0, 2),
                    ((0, 0), (0, 0), (0, _rows_of(n_in, LANES) * LANES - n_in)))
    sh_cw = _as_rows(big["conv_w"].reshape(conv_w.shape[1], NCHIP, -1).transpose(1, 0, 2))
    rows_a = [w_out.shape[1], w_down.shape[1], sh_cw.shape[1]]
    pad_a = _rows_of(sum(rows_a), 4 * SUBLANES) * 4 * SUBLANES - sum(rows_a)
    sh_a = jnp.concatenate([big["w_out"].reshape(NCHIP, -1, D), big["w_down"].reshape(NCHIP, -1, D), sh_cw,
                            jnp.zeros((NCHIP, pad_a, D), F32)], axis=1)
    blocks = [sh_a, big["w_up"], sh_in]
    tags = "abc"
    cut = [t.reshape(2 * NCHIP, t.shape[1] // 2, t.shape[2]) for t in blocks]
    to_sib = lambda x, y, c: None
    from_sib = _exchange(
        "pair_swap", cut, [jax.ShapeDtypeStruct((NCHIP,) + t.shape[1:], F32) for t in cut],
        [((0, 0, 1), n, lambda x, y, c, k=k: 2 * k + 1 - c, n, lambda x, y, c, k=k: k)
         for n in range(len(cut)) for k in range(NCHIP)], [])
    pair_sums = []
    for t, fs, tag in zip(cut, from_sib, tags):
        mine = lax.dynamic_index_in_dim(t.reshape((NCHIP, 2) + t.shape[1:]), mc, axis=1, keepdims=False)
        flat = lambda v: v.reshape(-1, v.shape[-1])
        pair_sums.append(_add2("pair_sum_" + tag, flat(mine), flat(fs), out_dtype=BF16).reshape(fs.shape))
    scatter = []
    for n in range(len(cut)):
        for j, f in enumerate(_CHIP_FLIPS):
            src = lambda x, y, c, f=f: _chip_index(1 - x if f[0] else x, 1 - y if f[1] else y, c)
            scatter.append((f, n, src, n, lambda x, y, c, j=j: j))
    arrived = _exchange("scatter_grads", pair_sums,
                        [jax.ShapeDtypeStruct((NCHIP - 1,) + t.shape[1:], BF16) for t in pair_sums], scatter, [])
    my_halves = []
    for ps, got, tag in zip(pair_sums, arrived, tags):
        own = lax.dynamic_index_in_dim(ps, chip, axis=0, keepdims=True)
        my_halves.append(_sum_leading("chip_sum_" + tag, jnp.concatenate([own, got], axis=0)))
    sib_halves = _exchange("pair_share", my_halves, [jax.ShapeDtypeStruct(t.shape, F32) for t in my_halves],
                           [((0, 0, 1), n, to_sib, n, to_sib) for n in range(len(my_halves))], [])
    shards = []
    for mh, shf in zip(my_halves, sib_halves):
        pair = jnp.stack([mh, shf])
        shards.append(jnp.concatenate([lax.dynamic_index_in_dim(pair, mc, axis=0, keepdims=False),
                                       lax.dynamic_index_in_dim(pair, 1 - mc, axis=0, keepdims=False)], axis=0))
    r0, r1 = rows_a[0], rows_a[0] + rows_a[1]
    g_w_out, g_w_down = shards[0][:r0], shards[0][r0:r1]
    g_conv_w = shards[0][r1:r1 + rows_a[2]].reshape(-1)[:int(np.prod(conv_w.shape[1:]))].reshape(conv_w.shape[1:])
    g_w_up = shards[1]
    g_w_in = shards[2][:, :n_in]

    grads = dict(w_ada=g_w_ada, b_ada=g_b_ada, w_in=g_w_in, b_fgate=g_b_fgate, gn_a=g_gn[:, :WA], gn_b=g_gn[:, WA:],
                 w_out=g_w_out, ln1_g=g_ln1_g, ln1_b=g_ln1_b, w_up=g_w_up, conv_w=g_conv_w, conv_b=g_conv_b,
                 w_down=g_w_down, ln2_g=g_ln2_g, ln2_b=g_ln2_b)
    weights = dict(w_ada=w_ada, b_ada=b_ada, w_in=w_in, b_fgate=b_fgate, gn_a=gn_a, gn_b=gn_b, w_out=w_out,
                   ln1_g=ln1_g, ln1_b=ln1_b, w_up=w_up, conv_w=conv_w, conv_b=conv_b, w_down=w_down,
                   ln2_g=ln2_g, ln2_b=ln2_b)
    ms = dict(w_ada=m_w_ada, b_ada=m_b_ada, w_in=m_w_in, b_fgate=m_b_fgate, gn_a=m_gn_a, gn_b=m_gn_b,
              w_out=m_w_out, ln1_g=m_ln1_g, ln1_b=m_ln1_b, w_up=m_w_up, conv_w=m_conv_w, conv_b=m_conv_b,
              w_down=m_w_down, ln2_g=m_ln2_g, ln2_b=m_ln2_b)
    vs = dict(w_ada=v_w_ada, b_ada=v_b_ada, w_in=v_w_in, b_fgate=v_b_fgate, gn_a=v_gn_a, gn_b=v_gn_b,
              w_out=v_w_out, ln1_g=v_ln1_g, ln1_b=v_ln1_b, w_up=v_w_up, conv_w=v_conv_w, conv_b=v_conv_b,
              w_down=v_w_down, ln2_g=v_ln2_g, ln2_b=v_ln2_b)
    names = list(weights)
    big_names = ("w_ada", "w_in", "w_out", "w_up", "w_down")
    delta, new_m, new_v = {}, {}, {}
    for n in big_names:
        shp = weights[n].shape
        d, m2, v2 = _adamw("adamw_" + n, weights[n][0], grads[n].reshape(shp[1:]), ms[n][0], vs[n][0])
        delta[n], new_m[n], new_v[n] = d.reshape(shp), m2.reshape(shp), v2.reshape(shp)
    small_names = [n for n in names if n not in big_names]

    def pack_small(src):
        flats = []
        for n in small_names:
            flat = src[n].reshape(-1)
            flats.append(jnp.pad(flat, (0, _rows_of(flat.shape[0]) * D - flat.shape[0])))
        allf = jnp.concatenate(flats)
        rows = _rows_of(allf.shape[0], SUBLANES * D) * SUBLANES
        return jnp.pad(allf, (0, rows * D - allf.shape[0])).reshape(rows, D)

    sd, sm, sv_ = _adamw("adamw_small", pack_small(weights), pack_small(grads), pack_small(ms), pack_small(vs))
    off = 0
    for n in small_names:
        shp = weights[n].shape
        cnt = int(np.prod(shp))
        r = _rows_of(cnt)
        for dst, src in ((delta, sd), (new_m, sm), (new_v, sv_)):
            dst[n] = src[off:off + r].reshape(-1)[:cnt].reshape(shp)
        off += r

    out_g = {n: grads[n].reshape(weights[n].shape) for n in names}
    return (loss, grad_x.reshape(x.shape), *[out_g[n] for n in names], *[delta[n] for n in names],
            *[new_m[n] for n in names], *[new_v[n] for n in names])
```

```python
import functools
import math

import numpy as np
import jax
import jax.numpy as jnp
from jax import lax
from jax.experimental import pallas as pl
from jax.experimental.pallas import tpu as pltpu

F32 = jnp.float32
BF16 = jnp.bfloat16

D = 1024
S = 4096
HD = 64
WA = 512
DFF = 2816
NCHIP = 4
NDEV = 8
PATTERNS = ((128, 1), (512, 4), (2048, 16))
ROPE_THETA = 500000.0
ROPE_DIMS = HD // 4
ALPHA = (2.0 * 1) ** 0.25
LN_EPS = 1e-5
RMS_EPS = 1e-6
ADAM_LR = 0.001
ADAM_B1 = 0.9
ADAM_B2 = 0.999
ADAM_EPS = 1e-08
ADAM_WD = 0.01
ADAM_STEP = 10

LANES = 128
SUBLANES = 8
TQ = 512
FPAD = LANES
NEG = -1e30
VMEM_LIMIT = 56 * 1024 * 1024
MESH = pl.DeviceIdType.MESH


def _cparams(sem):
    return pltpu.CompilerParams(dimension_semantics=sem, vmem_limit_bytes=VMEM_LIMIT)


def _pick(n, cands):
    for c in cands:
        if n % c == 0:
            return c
    return n


def _rsum8(v):
    tm, w = v.shape
    return jnp.sum(v.reshape(tm // SUBLANES, SUBLANES, w), axis=0)


def _sigmoid(x):
    return 1.0 / (1.0 + jnp.exp(-x))


def _dot(a, b):
    return jnp.dot(a, b, preferred_element_type=F32)


def _dot_nt(a, b):
    return lax.dot_general(a, b, (((1,), (1,)), ((), ())), preferred_element_type=F32)


def _dot_tn(a, b):
    return lax.dot_general(a, b, (((0,), (0,)), ((), ())), preferred_element_type=F32)


def _rowwise(name, fn, T, tm, *, tiles=(), halos=(), seqvecs=(), consts=(), outs=(), accs=(), seqaccs=(),
             seq_len=None):
    seq_len = S if seq_len is None else seq_len
    nb = T // tm
    spb = max(seq_len // tm, 1)
    nseq = max(T // seq_len, 1)
    n8 = T // SUBLANES
    r8 = tm // SUBLANES
    in_specs, args = [], []
    for a in tiles:
        in_specs.append(pl.BlockSpec((tm, a.shape[1]), lambda i: (i, 0)))
        args.append(a)
    for a, direction in halos:
        if direction < 0:
            idx = lambda i: (jnp.maximum(i * r8 - 1, 0), 0)
        else:
            idx = lambda i: (jnp.minimum((i + 1) * r8, n8 - 1), 0)
        in_specs.append(pl.BlockSpec((SUBLANES, a.shape[1]), idx))
        args.append(a)
    for a in seqvecs:
        in_specs.append(pl.BlockSpec((1, 1, a.shape[2]), lambda i: (i // spb, 0, 0)))
        args.append(a)
    for a in consts:
        in_specs.append(pl.BlockSpec(a.shape, lambda i, nd=a.ndim: (0,) * nd))
        args.append(a)
    out_shape, out_specs = [], []
    for w, dt in outs:
        out_shape.append(jax.ShapeDtypeStruct((T, w), dt))
        out_specs.append(pl.BlockSpec((tm, w), lambda i: (i, 0)))
    for w in accs:
        out_shape.append(jax.ShapeDtypeStruct((SUBLANES, w), F32))
        out_specs.append(pl.BlockSpec((SUBLANES, w), lambda i: (0, 0)))
    for w in seqaccs:
        out_shape.append(jax.ShapeDtypeStruct((nseq, SUBLANES, w), F32))
        out_specs.append(pl.BlockSpec((1, SUBLANES, w), lambda i: (i // spb, 0, 0)))
    n_t, n_h, n_s, n_c = len(tiles), len(halos), len(seqvecs), len(consts)
    n_o, n_a, n_sa = len(outs), len(accs), len(seqaccs)

    def body(*refs):
        i = pl.program_id(0)
        ins = refs[:n_t + n_h + n_s + n_c]
        orefs = refs[n_t + n_h + n_s + n_c:]
        vals = [r[...] for r in ins[:n_t + n_h]]
        vals += [r[0] for r in ins[n_t + n_h:n_t + n_h + n_s]]
        vals += list(ins[n_t + n_h + n_s:])
        res = fn(i, *vals)
        if not isinstance(res, (tuple, list)):
            res = (res,)
        for k in range(n_o):
            orefs[k][...] = res[k].astype(orefs[k].dtype)
        for k in range(n_a):
            r = orefs[n_o + k]

            @pl.when(i == 0)
            def _():
                r[...] = jnp.zeros_like(r)

            r[...] += res[n_o + k]

            @pl.when(i == nb - 1)
            def _():
                r[...] = jnp.broadcast_to(jnp.sum(r[...], axis=0, keepdims=True), r.shape)
        for k in range(n_sa):
            r = orefs[n_o + n_a + k]

            @pl.when(i % spb == 0)
            def _():
                r[...] = jnp.zeros_like(r)

            r[0] += res[n_o + n_a + k]

            @pl.when(i % spb == spb - 1)
            def _():
                r[0] = jnp.broadcast_to(jnp.sum(r[0], axis=0, keepdims=True), r.shape[1:])

    sem = ("arbitrary",) if (n_a or n_sa) else ("parallel",)
    res = pl.pallas_call(
        body, name=name, grid=(nb,), in_specs=in_specs, out_specs=out_specs, out_shape=out_shape,
        compiler_params=_cparams(sem),
    )(*args)
    return res


def _ln_fwd(r, g, b):
    mu = jnp.mean(r, axis=-1, keepdims=True)
    xc = r - mu
    var = jnp.mean(xc * xc, axis=-1, keepdims=True)
    rstd = lax.rsqrt(var + LN_EPS)
    n = xc * rstd
    return n * g + b, n, rstd


def _ln_bwd(dy, n, rstd, g):
    dn = dy * g
    return rstd * (dn - jnp.mean(dn, axis=-1, keepdims=True) - n * jnp.mean(dn * n, axis=-1, keepdims=True))


def _head_mean(t, g_ref):
    gw = g_ref.shape[0]
    hi = t.astype(BF16)
    lo = (t - hi.astype(F32)).astype(BF16)
    g = g_ref[...]
    parts = []
    for c in range(t.shape[1] // gw):
        sl = slice(c * gw, (c + 1) * gw)
        parts.append(_dot(hi[:, sl], g) + _dot(lo[:, sl], g))
    out = parts[0] if len(parts) == 1 else jnp.concatenate(parts, axis=1)
    return out * (1.0 / HD)


def _rope(z, c, s1, s2):
    w = z.shape[1]
    half = ROPE_DIMS // 2
    return z * c + pltpu.roll(z, half, 1) * s1 + pltpu.roll(z, w - half, 1) * s2


def _tile_lanes(t, w):
    reps = w // t.shape[1]
    return t if reps == 1 else jnp.concatenate([t] * reps, axis=1)


def _conv_taps(ext, prev, first):
    tm = ext.shape[0]
    prev = jnp.where(first, jnp.zeros_like(prev), prev)
    r8 = lax.broadcasted_iota(jnp.int32, (SUBLANES, 1), 0)
    top = ext[0:SUBLANES]
    s1_top = jnp.where(r8 < 1, pltpu.roll(prev, 1, 0), pltpu.roll(top, 1, 0))
    s2_top = jnp.where(r8 < 2, pltpu.roll(prev, 2, 0), pltpu.roll(top, 2, 0))
    s1 = jnp.concatenate([s1_top, pltpu.roll(ext, 1, 0)[SUBLANES:]], axis=0)
    s2 = jnp.concatenate([s2_top, pltpu.roll(ext, 2, 0)[SUBLANES:]], axis=0)
    return s1, s2


def _conv_taps_up(ext, nxt, last):
    tm = ext.shape[0]
    nxt = jnp.where(last, jnp.zeros_like(nxt), nxt)
    r8 = lax.broadcasted_iota(jnp.int32, (SUBLANES, 1), 0)
    bot = ext[tm - SUBLANES:tm]
    u1_bot = jnp.where(r8 >= 7, pltpu.roll(nxt, 7, 0), pltpu.roll(bot, 7, 0))
    u2_bot = jnp.where(r8 >= 6, pltpu.roll(nxt, 6, 0), pltpu.roll(bot, 6, 0))
    u1 = jnp.concatenate([pltpu.roll(ext, tm - 1, 0)[:tm - SUBLANES], u1_bot], axis=0)
    u2 = jnp.concatenate([pltpu.roll(ext, tm - 2, 0)[:tm - SUBLANES], u2_bot], axis=0)
    return u1, u2


def _nt_rows(av, w_ref):
    n = w_ref.shape[0]
    ch = _pick(n, (512, 256, 128))
    ab = av.astype(BF16)
    parts = [_dot_nt(ab, w_ref[c * ch:(c + 1) * ch, :]) for c in range(n // ch)]
    return parts[0] if len(parts) == 1 else jnp.concatenate(parts, axis=1)


def _mm_tn(name, a, b, *, mod=None, tt=512, t2=None, by_chip=False):
    T, k1 = a.shape
    k2 = b.shape[1]
    t1 = k1 if k1 <= 1536 else _pick(k1, (1408, 1024, 512, 256, 128))
    if t2 is None:
        t2 = k2 if k2 <= 1536 else _pick(k2, (1408, 1024, 640, 512, 256, 128))
    wc = k2 // NCHIP
    if by_chip:
        t2 = 2 * wc
    tt = min(tt, S)
    spb = S // tt

    def body(*refs):
        if mod is not None:
            a_ref, sc_ref, sh_ref, b_ref, o_ref = refs
        else:
            a_ref, b_ref, o_ref = refs
        t = pl.program_id(2)

        @pl.when(t == 0)
        def _():
            o_ref[...] = jnp.zeros_like(o_ref)

        av = a_ref[...]
        if mod is not None:
            av = av * (1.0 + sc_ref[0]) + sh_ref[0]
        res = _dot_tn(av.astype(BF16), b_ref[...].astype(BF16))
        if by_chip:
            o_ref[0] += res[:, :wc]
            o_ref[1] += res[:, wc:]
        else:
            o_ref[...] += res

    in_specs = [pl.BlockSpec((tt, t1), lambda p, q, t: (t, p))]
    args = [a]
    if mod is not None:
        for v in mod:
            in_specs.append(pl.BlockSpec((1, 1, t1), lambda p, q, t: (t // spb, 0, p)))
            args.append(v)
    in_specs.append(pl.BlockSpec((tt, t2), lambda p, q, t: (t, q)))
    args.append(b)
    if by_chip:
        out_specs = pl.BlockSpec((2, t1, wc), lambda p, q, t: (q, p, 0))
        out_shape = jax.ShapeDtypeStruct((NCHIP, k1, wc), F32)
    else:
        out_specs = pl.BlockSpec((t1, t2), lambda p, q, t: (p, q))
        out_shape = jax.ShapeDtypeStruct((k1, k2), F32)
    return pl.pallas_call(
        body, name=name, grid=(k1 // t1, k2 // t2, T // tt), in_specs=in_specs, out_specs=out_specs,
        out_shape=out_shape, compiler_params=_cparams(("parallel", "parallel", "arbitrary")),
    )(*args)


def _mod_mm(name, x, sc, sh, ws, out_dtypes, rope=None, rope_secs=(), tm=256):
    T = x.shape[0]
    nw = len(ws)

    def fn(i, xv, *rest):
        if rope is not None:
            cv, s1v, s2v = rest[:3]
            rest = rest[3:]
        scv, shv = rest[:2]
        w_refs = rest[2:]
        h = (xv * (1.0 + scv) + shv).astype(BF16)
        res = []
        for k, w_ref in enumerate(w_refs):
            n = w_ref.shape[1]
            ch = WA if (k == 0 and rope is not None) else _pick(n, (512, 256, 128))
            parts = []
            for c in range(n // ch):
                z = _dot(h, w_ref[:, c * ch:(c + 1) * ch])
                if k == 0 and c in rope_secs:
                    z = _rope(z, _tile_lanes(cv, ch), _tile_lanes(s1v, ch), _tile_lanes(s2v, ch))
                parts.append(z.astype(out_dtypes[k]))
            res.append(parts[0] if len(parts) == 1 else jnp.concatenate(parts, axis=1))
        return tuple(res)

    tiles = (x,) + (tuple(rope) if rope is not None else ())
    outs = tuple((w.shape[1], dt) for w, dt in zip(ws, out_dtypes))
    return _rowwise(name, fn, T, tm, tiles=tiles, seqvecs=(sc, sh), consts=tuple(ws), outs=outs)


def _tri(tb, lower):
    r = lax.broadcasted_iota(jnp.int32, (tb, tb), 0)
    c = lax.broadcasted_iota(jnp.int32, (tb, tb), 1)
    return jnp.where((r >= c) if lower else (r <= c), 1.0, 0.0).astype(BF16)


def _split3(x):
    hi = x.astype(BF16)
    r = x - hi.astype(F32)
    mid = r.astype(BF16)
    return hi, mid, (r - mid.astype(F32)).astype(BF16)


def _cumsum_seq(name, ins, consts, fn_in, fn_out, outs, reverse, n_acc=0, tb=256):
    T = ins[0].shape[0]
    tb = min(tb, S)
    nbs = S // tb
    nseq = T // S
    n_i, n_c, n_o = len(ins), len(consts), len(outs)

    def blk(b, j):
        return (b * nbs + (nbs - 1 - j if reverse else j), 0)

    def body(*refs):
        i_refs, c_refs = refs[:n_i], refs[n_i:n_i + n_c]
        o_refs = refs[n_i + n_c:n_i + n_c + n_o]
        acc_refs = refs[n_i + n_c + n_o:n_i + n_c + n_o + n_acc]
        carry = refs[-1]
        b, j = pl.program_id(0), pl.program_id(1)

        @pl.when(j == 0)
        def _():
            carry[...] = jnp.zeros_like(carry)

        iv = [r[...] for r in i_refs]
        xin = fn_in(*iv, *c_refs)
        tri = _tri(tb, not reverse)
        cum = sum(_dot(tri, piece) for piece in _split3(xin)) + carry[0:1, :]
        carry[...] = carry[...] + jnp.sum(xin, axis=0, keepdims=True)
        res = fn_out(cum, *iv, *c_refs)
        for o, r in zip(o_refs, res):
            o[...] = r.astype(o.dtype)
        for a in acc_refs:
            @pl.when((b == 0) & (j == 0))
            def _():
                a[...] = jnp.zeros_like(a)

            a[...] += _rsum8(res[0])

            @pl.when((b == nseq - 1) & (j == nbs - 1))
            def _():
                a[...] = jnp.broadcast_to(jnp.sum(a[...], axis=0, keepdims=True), a.shape)

    in_specs = [pl.BlockSpec((tb, a.shape[1]), blk) for a in ins]
    in_specs += [pl.BlockSpec(c.shape, lambda b, j, nd=c.ndim: (0,) * nd) for c in consts]
    out_shape = [jax.ShapeDtypeStruct((T, w), dt) for w, dt in outs]
    out_shape += [jax.ShapeDtypeStruct((SUBLANES, outs[0][0]), F32)] * n_acc
    out_specs = [pl.BlockSpec((tb, w), blk) for w, _ in outs]
    out_specs += [pl.BlockSpec((SUBLANES, outs[0][0]), lambda b, j: (0, 0))] * n_acc
    return pl.pallas_call(
        body, name=name, grid=(nseq, nbs), in_specs=in_specs, out_specs=out_specs, out_shape=out_shape,
        scratch_shapes=[pltpu.VMEM((SUBLANES, FPAD), F32)],
        compiler_params=_cparams(("arbitrary", "arbitrary")),
    )(*ins, *consts)


def _log_sigmoid(x):
    return jnp.minimum(x, 0.0) - jnp.log(1.0 + jnp.exp(-jnp.abs(x)))


def _dil_bias(tq):
    max_win = max(w for w, _ in PATTERNS)
    nd = (max_win + tq - 1) // tq + 1
    qi = np.arange(tq)[:, None]
    kj = np.arange(tq)[None, :]
    tabs = []
    for dlt in range(nd):
        dist = dlt * tq + qi - kj
        mult = np.zeros((tq, tq), np.float64)
        for win, dil in PATTERNS:
            mult += (dist >= 0) & (dist % dil == 0) & (dist // dil <= win // dil)
        tabs.append(np.where(mult > 0, np.log(np.maximum(mult, 1.0)), NEG))
    return np.stack(tabs).astype(np.float32)


def _fold_tables(nha):
    hp_n = nha // 2
    pq = np.zeros((3, FPAD, hp_n * 2 * LANES), np.float32)
    pk = np.zeros((3, FPAD, hp_n * LANES), np.float32)
    oq = np.zeros((1, hp_n * 2 * LANES), np.float32)
    ok = np.zeros((1, hp_n * LANES), np.float32)
    sq = np.zeros((hp_n * LANES, FPAD), np.float32)
    sk = np.zeros((hp_n * LANES, FPAD), np.float32)
    for h in range(nha):
        hp, odd = divmod(h, 2)
        qb = hp * 2 * LANES + odd * (LANES + 8)
        kb = hp * LANES + odd * 8
        for i in range(3):
            pq[i, h, qb + i] = 1
            oq[0, qb + 3 + i] = 1
            ok[0, kb + i] = 1
            pk[i, h, kb + 3 + i] = 1
        sq[kb, h] = 1
        sk[kb + 3, h] = 1
    return pq, pk, oq, ok, sq, sk


def _stack_heads(x2, h0, extra=None):
    z = jnp.zeros_like(x2)
    a, b = jnp.where(h0, x2, z), jnp.where(h0, z, x2)
    if extra is not None:
        a = jnp.concatenate([a, extra[:, :LANES]], axis=1)
        b = jnp.concatenate([b, extra[:, LANES:]], axis=1)
    return jnp.concatenate([a, b], axis=0)


def _attn_fwd(name, qkv, secs, fox, eq=None, ek=None, bias=None, ride=()):
    T = qkv.shape[0]
    nq = S // TQ
    nbl = T // S
    hp_n = WA // LANES
    sq, sk, sv = (s * hp_n for s in secs)
    scale = HD ** -0.5
    nd = None if fox else bias.shape[0]

    n_r = len(ride)
    n_in = (5 if fox else 4) + n_r

    def body(*refs):
        if fox:
            q_ref, k_ref, v_ref, eq_ref, ek_ref = refs[:5]
        else:
            q_ref, k_ref, v_ref, b_ref = refs[:4]
        o_ref, lse_ref = refs[n_in:n_in + 2]
        i = pl.program_id(2)
        if n_r:
            ride_start, ride_finish = _gather_halves_steps(
                refs[n_in - n_r:n_in], refs[n_in + 2:n_in + 2 + n_r], *refs[n_in + 2 + n_r:])
            at = lambda b, hp, q: (pl.program_id(0) == b) & (pl.program_id(1) == hp) & (i == q)
            pl.when(at(0, 0, 0))(ride_start)
        lane = lax.broadcasted_iota(jnp.int32, (1, LANES), 1)
        h0 = lane < HD
        q2 = (q_ref[...].astype(F32) * scale).astype(BF16)
        qs = _stack_heads(q2, h0, eq_ref[...] if fox else None)

        def scores(t, diag):
            off = pl.multiple_of((i - t) * TQ, TQ)
            kk = k_ref[pl.ds(off, TQ), :]
            if fox:
                kk = jnp.concatenate([kk, ek_ref[pl.ds(off, TQ), :]], axis=1)
            s = jnp.concatenate([_dot_nt(qs[:TQ], kk), _dot_nt(qs[TQ:], kk)], axis=0)
            if not fox:
                s = (s.reshape(2, TQ, TQ) + b_ref[t]).reshape(2 * TQ, TQ)
            elif diag:
                rows = lax.broadcasted_iota(jnp.int32, (2, TQ, TQ), 1).reshape(2 * TQ, TQ)
                cols = lax.broadcasted_iota(jnp.int32, (2 * TQ, TQ), 1)
                s = jnp.where(cols <= rows, s, NEG)
            return s

        def update(t, s, m, l, acc):
            off = pl.multiple_of((i - t) * TQ, TQ)
            v2 = v_ref[pl.ds(off, TQ), :]
            m_new = jnp.maximum(m, jnp.max(s, axis=1, keepdims=True))
            p = jnp.exp(s - m_new)
            a = jnp.exp(m - m_new)
            l = a * l + jnp.sum(p, axis=1, keepdims=True)
            pb = p.astype(BF16)
            acc = a * acc + jnp.concatenate([_dot(pb[:TQ], v2), _dot(pb[TQ:], v2)], axis=0)
            return m_new, l, acc

        init = (jnp.full((2 * TQ, 1), NEG, F32), jnp.zeros((2 * TQ, 1), F32), jnp.zeros((2 * TQ, LANES), F32))
        n = i + 1 if fox else jnp.minimum(i + 1, nd)
        m, l, acc = update(0, scores(0, True), *init)
        m, l, acc = lax.fori_loop(1, n, lambda t, c: update(t, scores(t, False), *c), (m, l, acc))
        on = acc / l
        o_ref[...] = jnp.where(h0, on[:TQ], on[TQ:])
        lse = jnp.broadcast_to(m + jnp.log(l), (2 * TQ, LANES))
        lse_ref[...] = jnp.concatenate([lse[:TQ], lse[TQ:]], axis=1)
        if n_r:
            pl.when(at(nbl - 1, hp_n - 1, nq - 1))(ride_finish)

    in_specs = [
        pl.BlockSpec((TQ, LANES), lambda b, hp, i: (b * nq + i, sq + hp)),
        pl.BlockSpec((S, LANES), lambda b, hp, i: (b, sk + hp)),
        pl.BlockSpec((S, LANES), lambda b, hp, i: (b, sv + hp)),
    ]
    args = [qkv, qkv, qkv]
    if fox:
        in_specs += [pl.BlockSpec((TQ, 2 * LANES), lambda b, hp, i: (b * nq + i, hp)),
                     pl.BlockSpec((S, LANES), lambda b, hp, i: (b, hp))]
        args += [eq, ek]
    else:
        in_specs.append(pl.BlockSpec(bias.shape, lambda b, hp, i: (0, 0, 0)))
        args.append(bias)
    any_spec = pl.BlockSpec(memory_space=pl.ANY)
    out_specs = [pl.BlockSpec((TQ, LANES), lambda b, hp, i: (b * nq + i, hp)),
                 pl.BlockSpec((TQ, 2 * LANES), lambda b, hp, i: (b * nq + i, hp))] + [any_spec] * n_r
    out_shape = [jax.ShapeDtypeStruct((T, WA), F32), jax.ShapeDtypeStruct((T, 2 * WA), F32)]
    out_shape += [jax.ShapeDtypeStruct((NCHIP - 1,) + v.shape, v.dtype) for v in ride]
    sems = [pltpu.SemaphoreType.DMA((6 * n_r,)), pltpu.SemaphoreType.DMA((6 * n_r,))] if n_r else []
    sem = ("arbitrary",) * 3 if n_r else ("parallel", "parallel", "arbitrary")
    return pl.pallas_call(
        body, name=name, grid=(nbl, hp_n, nq), in_specs=in_specs + [any_spec] * n_r, out_specs=out_specs,
        out_shape=out_shape, scratch_shapes=sems, compiler_params=_cparams(sem),
    )(*args, *ride)


def _attn_bwd(name, qkv, secs, o, do, do_sec, lse, fox, eq=None, ek=None, bias=None):
    T = qkv.shape[0]
    nq = S // TQ
    nbl = T // S
    hp_n = WA // LANES
    sq, sk, sv = (s * hp_n for s in secs)
    dsec = do_sec * hp_n
    scale = HD ** -0.5
    nd = None if fox else bias.shape[0]
    kc = 2 * LANES if fox else LANES

    def body(*refs):
        if fox:
            (q_ref, k_ref, v_ref, o_ref, do_ref, lse_ref, eq_ref, ek_ref,
             dq_ref, dk_ref, dv_ref, dqe_ref, dek_ref, dl_ref) = refs
        else:
            q_ref, k_ref, v_ref, o_ref, do_ref, lse_ref, b_ref, dq_ref, dk_ref, dv_ref, dl_ref = refs
        j = pl.program_id(2)
        lane = lax.broadcasted_iota(jnp.int32, (1, LANES), 1)
        h0 = lane < HD

        @pl.when(j == 0)
        def _():
            dq_ref[...] = jnp.zeros_like(dq_ref)
            if fox:
                dqe_ref[...] = jnp.zeros_like(dqe_ref)

            def dl_step(r, c):
                off = pl.multiple_of(r * TQ, TQ)
                d2 = do_ref[pl.ds(off, TQ), :] * o_ref[pl.ds(off, TQ), :]
                z2 = jnp.zeros_like(d2)
                dl0 = jnp.sum(jnp.where(h0, d2, z2), axis=1, keepdims=True)
                dl1 = jnp.sum(jnp.where(h0, z2, d2), axis=1, keepdims=True)
                dl_ref[pl.ds(off, TQ), :] = jnp.concatenate(
                    [jnp.broadcast_to(dl0, (TQ, LANES)), jnp.broadcast_to(dl1, (TQ, LANES))], axis=1)
                return c

            lax.fori_loop(0, nq, dl_step, 0)

        kk = k_ref[...]
        if fox:
            kk = jnp.concatenate([kk, ek_ref[...]], axis=1)
        v2 = v_ref[...]

        def wide(x2):
            st = jnp.concatenate([x2[:, :LANES], x2[:, LANES:]], axis=0)
            return st if TQ == LANES else jnp.concatenate([st] * (TQ // LANES), axis=1)

        def step(t, carry, diag):
            dkk, dv2 = carry
            off = pl.multiple_of((j + t) * TQ, TQ)
            q2 = (q_ref[pl.ds(off, TQ), :].astype(F32) * scale).astype(BF16)
            qs = _stack_heads(q2, h0, eq_ref[pl.ds(off, TQ), :] if fox else None)
            dos = _stack_heads(do_ref[pl.ds(off, TQ), :].astype(BF16), h0)
            s = jnp.concatenate([_dot_nt(qs[:TQ], kk), _dot_nt(qs[TQ:], kk)], axis=0)
            if not fox:
                s = (s.reshape(2, TQ, TQ) + b_ref[t]).reshape(2 * TQ, TQ)
            elif diag:
                rows = lax.broadcasted_iota(jnp.int32, (2, TQ, TQ), 1).reshape(2 * TQ, TQ)
                cols = lax.broadcasted_iota(jnp.int32, (2 * TQ, TQ), 1)
                s = jnp.where(cols <= rows, s, NEG)
            p = jnp.exp(s - wide(lse_ref[pl.ds(off, TQ), :]))
            dp = jnp.concatenate([_dot_nt(dos[:TQ], v2), _dot_nt(dos[TQ:], v2)], axis=0)
            dsb = (p * (dp - wide(dl_ref[pl.ds(off, TQ), :]))).astype(BF16)
            dv2 = dv2 + _dot_tn(p.astype(BF16), dos)
            dkk = dkk + _dot_tn(dsb, qs)
            dqq = jnp.concatenate([_dot(dsb[:TQ], kk), _dot(dsb[TQ:], kk)], axis=0)
            dq_ref[pl.ds(off, TQ), :] += jnp.where(h0, dqq[:TQ, :LANES], dqq[TQ:, :LANES])
            if fox:
                dqe_ref[pl.ds(off, TQ), :] += jnp.where(lane < SUBLANES, dqq[:TQ, LANES:], dqq[TQ:, LANES:])
            return dkk, dv2

        zero = (jnp.zeros((TQ, kc), F32), jnp.zeros((TQ, LANES), F32))
        if fox:
            dkk, dv2 = lax.fori_loop(1, nq - j, lambda t, c: step(t, c, False), step(0, zero, True))
        else:
            dkk, dv2 = lax.fori_loop(0, jnp.minimum(nq - j, nd), lambda t, c: step(t, c, False), zero)
        dk_ref[...] = dkk[:, :LANES]
        dv_ref[...] = dv2
        if fox:
            dek_ref[...] = dkk[:, LANES:]

        @pl.when(j == nq - 1)
        def _():
            dq_ref[...] = dq_ref[...] * scale

    seq = lambda c, w=LANES: pl.BlockSpec((S, w), lambda b, hp, j: (b, c + hp))
    blk = lambda c: pl.BlockSpec((TQ, LANES), lambda b, hp, j: (b * nq + j, c + hp))
    in_specs = [seq(sq), blk(sk), blk(sv), seq(0), seq(dsec), seq(0, 2 * LANES)]
    args = [qkv, qkv, qkv, o, do, lse]
    if fox:
        in_specs += [seq(0, 2 * LANES), blk(0)]
        args += [eq, ek]
    else:
        in_specs.append(pl.BlockSpec(bias.shape, lambda b, hp, j: (0, 0, 0)))
        args.append(bias)
    out_specs = [seq(0), blk(0), blk(0)]
    out_shape = [jax.ShapeDtypeStruct((T, WA), F32)] * 3
    if fox:
        out_specs += [seq(0), blk(0)]
        out_shape += [jax.ShapeDtypeStruct((T, WA), F32)] * 2
    return pl.pallas_call(
        body, name=name, grid=(nbl, hp_n, nq), in_specs=in_specs, out_specs=out_specs, out_shape=out_shape,
        scratch_shapes=[pltpu.VMEM((S, 2 * LANES), F32)],
        compiler_params=_cparams(("parallel", "parallel", "arbitrary")),
    )(*args)


def _exchange(name, ins, out_shapes, remote, local):
    n_in, n_out = len(ins), len(out_shapes)
    nr, nl = len(remote), len(local)

    def body(*refs):
        in_refs = refs[:n_in]
        out_refs = refs[n_in:n_in + n_out]
        send_sems, recv_sems, loc_sems = refs[n_in + n_out:]
        me = (lax.axis_index("x"), lax.axis_index("y"), lax.axis_index("c"))

        def peer_of(flip):
            return tuple(1 - v if f else v for v, f in zip(me, flip))

        def at(ref, idx):
            return ref if idx is None else ref.at[idx]

        def rcopy(k, who):
            flip, a, sfn, b, dfn = remote[k]
            return pltpu.make_async_remote_copy(
                src_ref=at(in_refs[a], sfn(*who)), dst_ref=at(out_refs[b], dfn(*who)),
                send_sem=send_sems.at[k], recv_sem=recv_sems.at[k],
                device_id=peer_of(flip), device_id_type=MESH)

        locs = [pltpu.make_async_copy(at(in_refs[a], sfn(*me)), at(out_refs[b], dfn(*me)), loc_sems.at[k])
                for k, (a, sfn, b, dfn) in enumerate(local)]
        for cp in locs:
            cp.start()
        sends = [rcopy(k, me) for k in range(nr)]
        for cp in sends:
            cp.start()
        for k in range(nr):
            rcopy(k, peer_of(remote[k][0])).wait_recv()
        for cp in sends:
            cp.wait_send()
        for cp in locs:
            cp.wait()

    any_spec = pl.BlockSpec(memory_space=pl.ANY)
    return pl.pallas_call(
        body, name=name, in_specs=[any_spec] * n_in, out_specs=[any_spec] * n_out, out_shape=list(out_shapes),
        scratch_shapes=[pltpu.SemaphoreType.DMA((max(nr, 1),)), pltpu.SemaphoreType.DMA((max(nr, 1),)),
                        pltpu.SemaphoreType.DMA((max(nl, 1),))],
    )(*ins)


_FLIPS7 = [(0, 0, 1), (0, 1, 0), (0, 1, 1), (1, 0, 0), (1, 0, 1), (1, 1, 0), (1, 1, 1)]
_CHIP_FLIPS = [(1, 0, 0), (0, 1, 0), (1, 1, 0)]


def _dev_index(x, y, c):
    return 4 * x + 2 * y + c


def _chip_index(x, y, c):
    return 2 * x + y


def _all_gather8(name, v):
    remote = [(f, 0, lambda x, y, c: None, 0, _dev_index) for f in _FLIPS7]
    local = [(0, lambda x, y, c: None, 0, _dev_index)]
    return _exchange(name, [v], [jax.ShapeDtypeStruct((NDEV,) + v.shape, v.dtype)], remote, local)[0]


def _gather_halves_steps(in_refs, out_refs, send_sems, recv_sems):
    n_v = len(in_refs)
    x, y, c = lax.axis_index("x"), lax.axis_index("y"), lax.axis_index("c")
    sibling = (x, y, 1 - c)
    chips = [(1 - x, y), (x, 1 - y), (1 - x, 1 - y)]

    def copy(k, n, src, blk, half, to):
        return pltpu.make_async_remote_copy(
            src_ref=src, dst_ref=out_refs[n].at[blk, half], send_sem=send_sems.at[k], recv_sem=recv_sems.at[k],
            device_id=to, device_id_type=MESH)

    def first():
        return [copy(6 * n + j, n, in_refs[n].at[c], j, c, (*chip, c))
                for n in range(n_v) for j, chip in enumerate(chips)]

    def start():
        for cp in first():
            cp.start()

    def finish():
        passed = []
        for n in range(n_v):
            for j, chip in enumerate(chips):
                copy(6 * n + j, n, in_refs[n].at[c], j, c, (*chip, c)).wait_recv()
                fw = copy(6 * n + 3 + j, n, out_refs[n].at[j, c], j, c, sibling)
                fw.start()
                passed.append(fw)
        for n in range(n_v):
            for j in range(len(chips)):
                copy(6 * n + 3 + j, n, out_refs[n].at[j, 1 - c], j, 1 - c, sibling).wait_recv()
        for cp in first() + passed:
            cp.wait_send()

    return start, finish


def _gather_halves(name, vs):
    n_v = len(vs)

    def body(*refs):
        start, finish = _gather_halves_steps(refs[:n_v], refs[n_v:2 * n_v], *refs[2 * n_v:])
        start()
        finish()

    any_spec = pl.BlockSpec(memory_space=pl.ANY)
    return pl.pallas_call(
        body, name=name, in_specs=[any_spec] * n_v, out_specs=[any_spec] * n_v,
        out_shape=[jax.ShapeDtypeStruct((NCHIP - 1,) + v.shape, v.dtype) for v in vs],
        scratch_shapes=[pltpu.SemaphoreType.DMA((6 * n_v,)), pltpu.SemaphoreType.DMA((6 * n_v,))],
    )(*vs)


def _by_chip(own, others, chip):
    stacked = jnp.concatenate([own[None], others], axis=0)
    blocks = []
    for k in range(NCHIP):
        d = k ^ chip
        place = jnp.where(d == 0, 0, jnp.where(d == 2, 1, jnp.where(d == 1, 2, 3)))
        blocks.append(lax.dynamic_index_in_dim(stacked, place, axis=0, keepdims=False))
    return jnp.stack(blocks)


def _sum_leading(name, v, tm=None):
    n, r, w = v.shape
    tm = _pick(r, (256, 128, 64, 32, 16, 8)) if tm is None else tm

    def body(v_ref, o_ref):
        acc = v_ref[0].astype(F32)
        for k in range(1, n):
            acc = acc + v_ref[k].astype(F32)
        o_ref[...] = acc

    return pl.pallas_call(
        body, name=name, grid=(r // tm,), in_specs=[pl.BlockSpec((n, tm, w), lambda i: (0, i, 0))],
        out_specs=pl.BlockSpec((tm, w), lambda i: (i, 0)), out_shape=jax.ShapeDtypeStruct((r, w), F32),
        compiler_params=_cparams(("parallel",)),
    )(v)


def _add2(name, a, b, tm=None, out_dtype=F32):
    r, w = a.shape
    tm = _pick(r, (256, 128, 64, 32, 16, 8)) if tm is None else tm

    def body(a_ref, b_ref, o_ref):
        o_ref[...] = (a_ref[...] + b_ref[...]).astype(out_dtype)

    spec = pl.BlockSpec((tm, w), lambda i: (i, 0))
    return pl.pallas_call(
        body, name=name, grid=(r // tm,), in_specs=[spec, spec], out_specs=spec,
        out_shape=jax.ShapeDtypeStruct((r, w), out_dtype), compiler_params=_cparams(("parallel",)),
    )(a, b)


def _ada_fwd(call_all, w_shard):
    def body(c_ref, w_ref, o_ref):
        cv = c_ref[...]
        o_ref[...] = jnp.dot(cv * _sigmoid(cv), w_ref[...], preferred_element_type=F32,
                             precision=lax.Precision.HIGHEST)

    n = w_shard.shape[1]
    return pl.pallas_call(
        body, name="ada_fwd", out_shape=jax.ShapeDtypeStruct((call_all.shape[0], n), F32),
        compiler_params=pltpu.CompilerParams(vmem_limit_bytes=VMEM_LIMIT),
    )(call_all, w_shard)


def _ada_bwd(call_all, dada):
    def body(c_ref, d_ref, o_ref):
        cv = c_ref[...]
        o_ref[...] = lax.dot_general(cv * _sigmoid(cv), d_ref[...], (((0,), (0,)), ((), ())),
                                     preferred_element_type=F32, precision=lax.Precision.HIGHEST)

    return pl.pallas_call(
        body, name="ada_bwd", out_shape=jax.ShapeDtypeStruct((call_all.shape[1], dada.shape[1]), F32),
        compiler_params=pltpu.CompilerParams(vmem_limit_bytes=VMEM_LIMIT),
    )(call_all, dada)


def _adamw(name, w, g, m, v):
    r, wd = w.shape
    tm = _pick(r, (256, 128, 64, 32, 16, 8))
    bc1 = 1.0 - ADAM_B1 ** ADAM_STEP
    bc2 = 1.0 - ADAM_B2 ** ADAM_STEP

    def body(w_ref, g_ref, m_ref, v_ref, d_ref, mo_ref, vo_ref):
        gv = g_ref[...]
        mn = ADAM_B1 * m_ref[...] + (1.0 - ADAM_B1) * gv
        vn = ADAM_B2 * v_ref[...] + (1.0 - ADAM_B2) * (gv * gv)
        d_ref[...] = -ADAM_LR * ((mn / bc1) / (jnp.sqrt(vn / bc2) + ADAM_EPS) + ADAM_WD * w_ref[...])
        mo_ref[...] = mn
        vo_ref[...] = vn

    spec = pl.BlockSpec((tm, wd), lambda i: (i, 0))
    return pl.pallas_call(
        body, name=name, grid=(r // tm,), in_specs=[spec] * 4, out_specs=[spec] * 3,
        out_shape=[jax.ShapeDtypeStruct((r, wd), F32)] * 3, compiler_params=_cparams(("parallel",)),
    )(w, g, m, v)


def _rope_tables(positions):
    half = ROPE_DIMS // 2
    freqs = ROPE_THETA ** (-jnp.arange(0, ROPE_DIMS, 2, dtype=F32) / ROPE_DIMS)
    ang = positions.astype(F32).reshape(-1, 1) * freqs
    cos, sin = jnp.cos(ang), jnp.sin(ang)
    T = ang.shape[0]
    one = jnp.ones((T, HD - ROPE_DIMS), F32)
    zero = jnp.zeros((T, HD - ROPE_DIMS), F32)
    zh = jnp.zeros((T, half), F32)
    c64 = jnp.concatenate([cos, cos, one], axis=1)
    s1 = jnp.concatenate([zh, sin, zero], axis=1)
    s2 = jnp.concatenate([-sin, zh, zero], axis=1)
    rep = lambda t: jnp.concatenate([t] * (LANES // HD), axis=1)
    return rep(c64), rep(s1), rep(s2)


def _local_step(x, loss_target, positions, ada, w_qkv, w_f, w_out, w_up, conv_w8, w_down,
                b_fgate, gn, ln1_g, ln1_b, conv_b, ln2_g, ln2_b, late=None):
    T = x.shape[0]
    nbl = T // S
    nha = WA // HD
    sv = lambda k: ada[:, k:k + 1, :]
    sh_a, sc_a, g_a, sh_f, sc_f, g_f = (sv(k) for k in range(6))
    rope = _rope_tables(positions)
    neg_rope = (rope[0], -rope[1], -rope[2])
    gseg = jnp.asarray(np.kron(np.eye(min(256, 2 * WA) // HD), np.ones((HD, HD))), BF16)
    bias = jnp.asarray(_dil_bias(TQ))
    bf_pad = jnp.zeros((1, FPAD), F32).at[:, :nha].set(b_fgate)

    qkv, fa = _mod_mm("qkv_proj", x, sc_a, sh_a, (w_qkv, w_f), (BF16, F32), rope=rope, rope_secs=(3, 4))
    pq, pk, oq, ok, sq, sk = _fold_tables(nha)

    def fold_out(cum, f, b_ref, pq_ref, pk_ref, oq_ref, ok_ref):
        hi, mid, lo = _split3(cum)
        eqv = _dot(hi, pq_ref[0]) + _dot(mid, pq_ref[1]) + _dot(lo, pq_ref[2]) + oq_ref[...]
        ekv = ok_ref[...] - (_dot(hi, pk_ref[0]) + _dot(mid, pk_ref[1]) + _dot(lo, pk_ref[2]))
        return eqv, ekv

    eq, ek = _cumsum_seq(
        "fgate_fwd", [fa], [bf_pad, jnp.asarray(pq, BF16), jnp.asarray(pk, BF16), jnp.asarray(oq), jnp.asarray(ok)],
        lambda f, b_ref, *_: _log_sigmoid(f + b_ref[...]), fold_out, ((2 * WA, BF16), (WA, BF16)), reverse=False)
    oa, lse_a, *got = _attn_fwd("fox_fwd", qkv, (0, 1, 2), True, eq=eq, ek=ek, ride=late[0] if late else ())
    if late:
        w_up, w_down = late[1](got)
    ob, lse_b = _attn_fwd("dil_fwd", qkv, (3, 4, 5), False, bias=bias)

    def mix_fn(i, oav, obv, xv, gav, gn_ref, g_ref, wo_ref, l1g_ref, l1b_ref):
        o = jnp.concatenate([oav, obv], axis=1)
        rs = lax.rsqrt(_head_mean(o * o, g_ref) + RMS_EPS)
        merged = (o * rs * gn_ref[...]).astype(BF16)
        mix = _dot(merged, wo_ref[...])
        x1, _, _ = _ln_fwd(ALPHA * xv + gav * mix, l1g_ref[...], l1b_ref[...])
        return merged, mix, x1

    merged, mix, x1 = _rowwise("mix_out", mix_fn, T, 256, tiles=(oa, ob, x), seqvecs=(g_a,),
                               consts=(gn, gseg, w_out, ln1_g, ln1_b),
                               outs=((2 * WA, BF16), (D, F32), (D, F32)))
    u = _mod_mm("ffn_up", x1, sc_f, sh_f, (w_up,), (F32,))[0]

    def conv_y(i, uv, prev, cw_ref, cb_ref, tm):
        first = (i * tm) % S == 0
        s1, s2 = _conv_taps(uv, prev, first)
        y = cb_ref[...] + cw_ref[0:1, :] * s2 + cw_ref[1:2, :] * s1 + cw_ref[2:3, :] * uv
        return y, s1, s2

    tmc = 128

    def gate_fn(i, uv, prev, cw_ref, cb_ref):
        y, _, _ = conv_y(i, uv, prev, cw_ref, cb_ref, tmc)
        a, g = y[:, :DFF], y[:, DFF:]
        return g * _sigmoid(g) * a, y

    act, yconv = _rowwise("conv_gate", gate_fn, T, tmc, tiles=(u,), halos=((u, -1),), consts=(conv_w8, conv_b),
                          outs=((DFF, BF16), (2 * DFF, F32)))

    def down_fn(i, actv, x1v, tgt, gfv, wd_ref, g2_ref, b2_ref):
        ffn = _dot(actv, wd_ref[...])
        y, n2, rstd = _ln_fwd(ALPHA * x1v + gfv * ffn, g2_ref[...], b2_ref[...])
        err = y - tgt
        dy = err * (1.0 / D)
        dr2 = _ln_bwd(dy, n2, rstd, g2_ref[...])
        return (dr2, gfv * dr2, _rsum8(err * err), _rsum8(dy * n2), _rsum8(dy), _rsum8(dr2 * ffn))

    dr2, dffn, loss_acc, d_ln2g, d_ln2b, d_gf = _rowwise(
        "ffn_down_loss", down_fn, T, 256, tiles=(act, x1, loss_target), seqvecs=(g_f,),
        consts=(w_down, ln2_g, ln2_b), outs=((D, F32), (D, F32)), accs=(D, D, D), seqaccs=(D,))

    def gate_conv_bwd_fn(i, uv, yv, dfv, y_nxt, df_nxt, cw_ref, wd_ref):
        last = ((i + 1) * tmc) % S == 0
        y = jnp.concatenate([yv, y_nxt], axis=0)
        df_ext = jnp.concatenate([dfv, df_nxt], axis=0).astype(BF16)
        ch = _pick(DFF, (256, 128))
        dav = jnp.concatenate([_dot_nt(df_ext, wd_ref[c * ch:(c + 1) * ch, :]) for c in range(DFF // ch)], axis=1)
        a, g = y[:, :DFF], y[:, DFF:]
        sg = _sigmoid(g)
        dyc_ext = jnp.concatenate([dav * (g * sg), dav * a * (sg * (1.0 + g * (1.0 - sg)))], axis=1)
        dyc = dyc_ext[:tmc]
        u1, u2 = _conv_taps_up(dyc, dyc_ext[tmc:], last)
        du_ = cw_ref[2:3, :] * dyc + cw_ref[1:2, :] * u1 + cw_ref[0:1, :] * u2
        return du_, _rsum8(dyc), _rsum8(uv * u2), _rsum8(uv * u1), _rsum8(uv * dyc)

    du, d_cb, d_cw0, d_cw1, d_cw2 = _rowwise(
        "gate_conv_bwd", gate_conv_bwd_fn, T, tmc, tiles=(u, yconv, dffn), halos=((yconv, 1), (dffn, 1)),
        consts=(conv_w8, w_down), outs=((2 * DFF, BF16),), accs=(2 * DFF,) * 4)
    g_w_down = _mm_tn("dw_down", act, dffn)
    g_w_up = _mm_tn("dw_up", x1, du, mod=(sc_f, sh_f), by_chip=True)

    def ln1_bwd_fn(i, dr2v, duv, xv, mixv, x1v, scfv, gav, l1g_ref, wu_ref):
        dh2v = _nt_rows(duv, wu_ref)
        dx1 = ALPHA * dr2v + dh2v * (1.0 + scfv)
        _, n1, rstd = _ln_fwd(ALPHA * xv + gav * mixv, l1g_ref[...], 0.0)
        dr1 = _ln_bwd(dx1, n1, rstd, l1g_ref[...])
        return (dr1, gav * dr1, _rsum8(dx1 * n1), _rsum8(dx1),
                _rsum8(dh2v * x1v), _rsum8(dh2v), _rsum8(dr1 * mixv))

    dr1, dmix, d_ln1g, d_ln1b, d_scf, d_shf, d_ga = _rowwise(
        "ln1_bwd", ln1_bwd_fn, T, 256, tiles=(dr2, du, x, mix, x1), seqvecs=(sc_f, g_a), consts=(ln1_g, w_up),
        outs=((D, F32), (D, BF16)), accs=(D, D), seqaccs=(D, D, D))

    g_w_out = _mm_tn("dw_out", merged, dmix)

    def hn_bwd_fn(i, dmixv, oav, obv, gn_ref, g_ref, wo_ref):
        dmv = _nt_rows(dmixv, wo_ref)
        o = jnp.concatenate([oav, obv], axis=1)
        rs = lax.rsqrt(_head_mean(o * o, g_ref) + RMS_EPS)
        nrm = o * rs
        dn = dmv * gn_ref[...]
        do = rs * (dn - nrm * _head_mean(dn * nrm, g_ref))
        return do, _rsum8(dmv * nrm)

    do, d_gn = _rowwise("headnorm_bwd", hn_bwd_fn, T, 256, tiles=(dmix, oa, ob), consts=(gn, gseg, w_out),
                        outs=((2 * WA, F32),), accs=(2 * WA,))
    dqa, dka, dva, dqe, dek = _attn_bwd("fox_bwd", qkv, (0, 1, 2), oa, do, 0, lse_a, True, eq=eq, ek=ek)
    dqb, dkb, dvb = _attn_bwd("dil_bwd", qkv, (3, 4, 5), ob, do, 1, lse_b, False, bias=bias)
    hdot = lambda a, m_ref: sum(_dot(piece, m_ref[...]) for piece in _split3(a))
    dfa, d_bf = _cumsum_seq(
        "fgate_bwd", [dqe, dek, fa], [bf_pad, jnp.asarray(sq, BF16), jnp.asarray(sk, BF16)],
        lambda dq_, dk_, f, b_ref, sq_ref, sk_ref: hdot(dq_, sq_ref) - hdot(dk_, sk_ref),
        lambda cum, dq_, dk_, f, b_ref, sq_ref, sk_ref: (cum * _sigmoid(-(f + b_ref[...])),),
        ((FPAD, F32),), reverse=True, n_acc=1)

    def dz_fn(i, a0, a1, a2, b0, b1, b2, fv, cv, s1v, s2v):
        ct, s1t, s2t = (_tile_lanes(t, WA) for t in (cv, s1v, s2v))
        return jnp.concatenate([a0, a1, a2, _rope(b0, ct, s1t, s2t), _rope(b1, ct, s1t, s2t), b2, fv], axis=1)

    dz = _rowwise("dz_pack", dz_fn, T, 256, tiles=(dqa, dka, dva, dqb, dkb, dvb, dfa) + neg_rope,
                  outs=((6 * WA + FPAD, BF16),))[0]
    w_cat = jnp.concatenate([w_qkv, w_f], axis=1)
    g_w_cat = _mm_tn("dw_in", x, dz, mod=(sc_a, sh_a), tt=256, t2=dz.shape[1])

    def dx_fn(i, dr1v, dzv, xv, scav, wc_ref):
        dh1v = _nt_rows(dzv, wc_ref)
        return ALPHA * dr1v + dh1v * (1.0 + scav), _rsum8(dh1v * xv), _rsum8(dh1v)

    grad_x, d_sca, d_sha = _rowwise("dx_out", dx_fn, T, 256, tiles=(dr1, dz, x), seqvecs=(sc_a,), consts=(w_cat,),
                                    outs=((D, F32),), seqaccs=(D, D))

    row0 = lambda a: a[..., 0, :]
    d_ada = jnp.stack([row0(d_sha), row0(d_sca), row0(d_ga), row0(d_shf), row0(d_scf), row0(d_gf)], axis=1)
    d_cw = jnp.stack([row0(d_cw0), row0(d_cw1), row0(d_cw2)], axis=0)
    loss_part = (0.5 / D) * jnp.sum(loss_acc[0])
    small = dict(b_fgate=row0(d_bf)[:nha], gn=row0(d_gn), ln1_g=row0(d_ln1g), ln1_b=row0(d_ln1b),
                 conv_b=row0(d_cb), ln2_g=row0(d_ln2g), ln2_b=row0(d_ln2b))
    big = dict(w_cat=g_w_cat, w_out=g_w_out, w_up=g_w_up, conv_w=d_cw, w_down=g_w_down)
    return loss_part, grad_x, d_ada, small, big


def _rows_of(n, w=None):
    return -(-n // (D if w is None else w))


def _as_rows(v):
    w = D
    k = v.shape[0]
    flat = v.reshape(k, -1)
    rows = _rows_of(_rows_of(flat.shape[1], w), SUBLANES) * SUBLANES
    flat = jnp.pad(flat, ((0, 0), (0, rows * w - flat.shape[1])))
    return flat.reshape(k, rows, w)


def kernel(x, c, positions, w_ada, b_ada, w_in, b_fgate, gn_a, gn_b, w_out, ln1_g, ln1_b, w_up, conv_w, conv_b, w_down, ln2_g, ln2_b, loss_target, m_w_ada, m_b_ada, m_w_in, m_b_fgate, m_gn_a, m_gn_b, m_w_out, m_ln1_g, m_ln1_b, m_w_up, m_conv_w, m_conv_b, m_w_down, m_ln2_g, m_ln2_b, v_w_ada, v_b_ada, v_w_in, v_b_fgate, v_gn_a, v_gn_b, v_w_out, v_ln1_g, v_ln1_b, v_w_up, v_conv_w, v_conv_b, v_w_down, v_ln2_g, v_ln2_b):
    mx, my, mc = lax.axis_index("x"), lax.axis_index("y"), lax.axis_index("c")
    dev = _dev_index(mx, my, mc)
    chip = _chip_index(mx, my, mc)
    nbl = x.shape[0]
    T = nbl * S
    nha = WA // HD
    d_in = w_in.shape[2] * NCHIP
    n_ada = w_ada.shape[2]

    c_pad = jnp.zeros((SUBLANES, D), F32).at[:nbl].set(c)
    c_all = _all_gather8("gather_c", c_pad)[:, :nbl].reshape(NDEV * nbl, D)
    ada_part = _ada_fwd(c_all, w_ada[0])
    n_cw = conv_w.shape[2]
    cw_rows = jnp.pad(conv_w[0], ((0, SUBLANES - conv_w.shape[1]), (0, n_ada - n_cw)))
    ada_blocks = _all_gather8("gather_ada", jnp.concatenate([ada_part, cw_rows], axis=0))
    n_c = NDEV * nbl
    ada_all = jnp.concatenate([ada_blocks[2 * k, :n_c] for k in range(NCHIP)], axis=1) + b_ada
    conv_w8 = jnp.concatenate([ada_blocks[2 * k, n_c:, :n_cw] for k in range(NCHIP)], axis=1)
    ada = lax.dynamic_slice_in_dim(ada_all, dev * nbl, nbl, axis=0).reshape(nbl, 6, D)

    w_in_sh = jnp.pad(w_in[0].astype(BF16), ((0, 0), (0, _rows_of(w_in.shape[2], LANES) * LANES - w_in.shape[2])))
    halve = lambda t: t.reshape(2, t.shape[0] // 2, t.shape[1])
    whole = lambda g, t: _by_chip(halve(t), g, chip).reshape((NCHIP,) + t.shape)
    w_out_sh, w_up_sh, w_down_sh = w_out[0].astype(BF16), w_up[0].astype(BF16), w_down[0].astype(BF16)
    g_in, g_out = _gather_halves("gather_w", [halve(w_in_sh), halve(w_out_sh)])
    g_in, g_out = whole(g_in, w_in_sh), whole(g_out, w_out_sh)
    w_in_full = jnp.concatenate([g_in[k][:, :w_in.shape[2]] for k in range(NCHIP)], axis=1)
    w_qkv = jnp.concatenate([w_in_full[:, :3 * WA], w_in_full[:, 3 * WA + nha:]], axis=1)
    w_f = jnp.pad(w_in_full[:, 3 * WA:3 * WA + nha], ((0, 0), (0, FPAD - nha)))
    w_out_full = g_out.reshape(NCHIP * w_out.shape[1], D)

    def late_weights(got):
        g_up, g_down = whole(got[0], w_up_sh), whole(got[1], w_down_sh)
        return (jnp.concatenate([g_up[k] for k in range(NCHIP)], axis=1),
                g_down.reshape(NCHIP * w_down.shape[1], D))

    gn = jnp.concatenate([gn_a, gn_b], axis=1)
    loss_part, grad_x, d_ada, small, big = _local_step(
        x.reshape(T, D), loss_target.reshape(T, D), positions, ada, w_qkv, w_f, w_out_full, None, conv_w8,
        None, b_fgate, gn, ln1_g, ln1_b, conv_b, ln2_g, ln2_b, late=([halve(w_up_sh), halve(w_down_sh)], late_weights))

    def row_pad(v, rows):
        flat = v.reshape(-1)
        return jnp.pad(flat, (0, rows * D - flat.shape[0]))

    n_cb = _rows_of(2 * DFF)
    small_flat = jnp.concatenate([
        row_pad(small["b_fgate"], 1), row_pad(small["gn"], 1), row_pad(small["ln1_g"], 1),
        row_pad(small["ln1_b"], 1), row_pad(small["ln2_g"], 1), row_pad(small["ln2_b"], 1),
        row_pad(jnp.full((1,), loss_part, F32), 1), row_pad(small["conv_b"], n_cb)])
    n_small = _rows_of(small_flat.shape[0], SUBLANES * D) * SUBLANES
    small_rows = jnp.pad(small_flat, (0, n_small * D - small_flat.shape[0])).reshape(n_small, D)
    ada_rows = jnp.pad(d_ada.reshape(nbl, 6, D), ((0, 0), (0, SUBLANES - 6), (0, 0))).reshape(nbl * SUBLANES, D)
    gathered = _all_gather8("gather_small", jnp.concatenate([small_rows, ada_rows], axis=0))
    red = _sum_leading("sum_small", gathered, tm=SUBLANES)
    g_b_fgate = red[0:1, :nha]
    g_gn = red[1:2, :2 * WA]
    g_ln1_g, g_ln1_b, g_ln2_g, g_ln2_b = red[2:3], red[3:4], red[4:5], red[5:6]
    loss = red[6, 0]
    g_conv_b = red[7:7 + n_cb].reshape(1, -1)[:, :2 * DFF]
    g_b_ada = _add2("sum_b_ada", red[n_small:n_small + SUBLANES], red[n_small + SUBLANES:n_small + 2 * SUBLANES],
                    tm=SUBLANES)[:6].reshape(1, 6 * D)
    dada_all = gathered[:, n_small:].reshape(NDEV, nbl, SUBLANES, D)[:, :, :6].reshape(NDEV * nbl, 6 * D)
    g_w_ada = _ada_bwd(c_all, lax.dynamic_slice_in_dim(dada_all, chip * n_ada, n_ada, axis=1))

    g_cat = big["w_cat"]
    g_w_in_full = jnp.concatenate([g_cat[:, :3 * WA], g_cat[:, 6 * WA:6 * WA + nha], g_cat[:, 3 * WA:6 * WA]], axis=1)
    n_in = w_in.shape[2]
    sh_in = jnp.pad(g_w_in_full.reshape(D, NCHIP, n_in).transpose(1, 0, 2),
                    ((0, 0), (0, 0), (0, _rows_of(n_in, LANES) * LANES - n_in)))
    sh_cw = _as_rows(big["conv_w"].reshape(conv_w.shape[1], NCHIP, -1).transpose(1, 0, 2))
    rows_a = [w_out.shape[1], w_down.shape[1], sh_cw.shape[1]]
    pad_a = _rows_of(sum(rows_a), 2 * LANES) * 2 * LANES - sum(rows_a)
    sh_a = jnp.concatenate([big["w_out"].reshape(NCHIP, -1, D), big["w_down"].reshape(NCHIP, -1, D), sh_cw,
                            jnp.zeros((NCHIP, pad_a, D), F32)], axis=1)
    blocks = [sh_a, big["w_up"], sh_in]
    tags = "abc"
    cut = [t.reshape(2 * NCHIP, t.shape[1] // 2, t.shape[2]) for t in blocks]
    to_sib = lambda x, y, c: None
    from_sib = _exchange(
        "pair_swap", cut, [jax.ShapeDtypeStruct((NCHIP,) + t.shape[1:], F32) for t in cut],
        [((0, 0, 1), n, lambda x, y, c, k=k: 2 * k + 1 - c, n, lambda x, y, c, k=k: k)
         for n in range(len(cut)) for k in range(NCHIP)], [])
    pair_sums = []
    for t, fs, tag in zip(cut, from_sib, tags):
        mine = lax.dynamic_index_in_dim(t.reshape((NCHIP, 2) + t.shape[1:]), mc, axis=1, keepdims=False)
        flat = lambda v: v.reshape(-1, v.shape[-1])
        pair_sums.append(_add2("pair_sum_" + tag, flat(mine), flat(fs), out_dtype=BF16).reshape(fs.shape))
    scatter = []
    for n in range(len(cut)):
        for j, f in enumerate(_CHIP_FLIPS):
            src = lambda x, y, c, f=f: _chip_index(1 - x if f[0] else x, 1 - y if f[1] else y, c)
            scatter.append((f, n, src, n, lambda x, y, c, j=j: j))
    arrived = _exchange("scatter_grads", pair_sums,
                        [jax.ShapeDtypeStruct((NCHIP - 1,) + t.shape[1:], BF16) for t in pair_sums], scatter, [])
    my_halves = []
    for ps, got, tag in zip(pair_sums, arrived, tags):
        own = lax.dynamic_index_in_dim(ps, chip, axis=0, keepdims=True)
        my_halves.append(_sum_leading("chip_sum_" + tag, jnp.concatenate([own, got], axis=0)))
    sib_halves = _exchange("pair_share", my_halves, [jax.ShapeDtypeStruct(t.shape, F32) for t in my_halves],
                           [((0, 0, 1), n, to_sib, n, to_sib) for n in range(len(my_halves))], [])
    shards = []
    for mh, shf in zip(my_halves, sib_halves):
        pair = jnp.stack([mh, shf])
        shards.append(jnp.concatenate([lax.dynamic_index_in_dim(pair, mc, axis=0, keepdims=False),
                                       lax.dynamic_index_in_dim(pair, 1 - mc, axis=0, keepdims=False)], axis=0))
    r0, r1 = rows_a[0], rows_a[0] + rows_a[1]
    g_w_out, g_w_down = shards[0][:r0], shards[0][r0:r1]
    g_conv_w = shards[0][r1:r1 + rows_a[2]].reshape(-1)[:int(np.prod(conv_w.shape[1:]))].reshape(conv_w.shape[1:])
    g_w_up = shards[1]
    g_w_in = shards[2][:, :n_in]

    grads = dict(w_ada=g_w_ada, b_ada=g_b_ada, w_in=g_w_in, b_fgate=g_b_fgate, gn_a=g_gn[:, :WA], gn_b=g_gn[:, WA:],
                 w_out=g_w_out, ln1_g=g_ln1_g, ln1_b=g_ln1_b, w_up=g_w_up, conv_w=g_conv_w, conv_b=g_conv_b,
                 w_down=g_w_down, ln2_g=g_ln2_g, ln2_b=g_ln2_b)
    weights = dict(w_ada=w_ada, b_ada=b_ada, w_in=w_in, b_fgate=b_fgate, gn_a=gn_a, gn_b=gn_b, w_out=w_out,
                   ln1_g=ln1_g, ln1_b=ln1_b, w_up=w_up, conv_w=conv_w, conv_b=conv_b, w_down=w_down,
                   ln2_g=ln2_g, ln2_b=ln2_b)
    ms = dict(w_ada=m_w_ada, b_ada=m_b_ada, w_in=m_w_in, b_fgate=m_b_fgate, gn_a=m_gn_a, gn_b=m_gn_b,
              w_out=m_w_out, ln1_g=m_ln1_g, ln1_b=m_ln1_b, w_up=m_w_up, conv_w=m_conv_w, conv_b=m_conv_b,
              w_down=m_w_down, ln2_g=m_ln2_g, ln2_b=m_ln2_b)
    vs = dict(w_ada=v_w_ada, b_ada=v_b_ada, w_in=v_w_in, b_fgate=v_b_fgate, gn_a=v_gn_a, gn_b=v_gn_b,
              w_out=v_w_out, ln1_g=v_ln1_g, ln1_b=v_ln1_b, w_up=v_w_up, conv_w=v_conv_w, conv_b=v_conv_b,
              w_down=v_w_down, ln2_g=v_ln2_g, ln2_b=v_ln2_b)
    names = list(weights)
    big_names = ("w_ada", "w_in", "w_out", "w_up", "w_down")
    delta, new_m, new_v = {}, {}, {}
    for n in big_names:
        shp = weights[n].shape
        d, m2, v2 = _adamw("adamw_" + n, weights[n][0], grads[n].reshape(shp[1:]), ms[n][0], vs[n][0])
        delta[n], new_m[n], new_v[n] = d.reshape(shp), m2.reshape(shp), v2.reshape(shp)
    small_names = [n for n in names if n not in big_names]

    def pack_small(src):
        flats = []
        for n in small_names:
            flat = src[n].reshape(-1)
            flats.append(jnp.pad(flat, (0, _rows_of(flat.shape[0]) * D - flat.shape[0])))
        allf = jnp.concatenate(flats)
        rows = _rows_of(allf.shape[0], SUBLANES * D) * SUBLANES
        return jnp.pad(allf, (0, rows * D - allf.shape[0])).reshape(rows, D)

    sd, sm, sv_ = _adamw("adamw_small", pack_small(weights), pack_small(grads), pack_small(ms), pack_small(vs))
    off = 0
    for n in small_names:
        shp = weights[n].shape
        cnt = int(np.prod(shp))
        r = _rows_of(cnt)
        for dst, src in ((delta, sd), (new_m, sm), (new_v, sv_)):
            dst[n] = src[off:off + r].reshape(-1)[:cnt].reshape(shp)
        off += r

    out_g = {n: grads[n].reshape(weights[n].shape) for n in names}
    return (loss, grad_x.reshape(x.shape), *[out_g[n] for n in names], *[delta[n] for n in names],
            *[new_m[n] for n in names], *[new_v[n] for n in names])
```

```python
import functools
import math

import numpy as np
import jax
import jax.numpy as jnp
from jax import lax
from jax.experimental import pallas as pl
from jax.experimental.pallas import tpu as pltpu

F32 = jnp.float32
BF16 = jnp.bfloat16

D = 1024
S = 4096
HD = 64
WA = 512
DFF = 2816
NCHIP = 4
NDEV = 8
PATTERNS = ((128, 1), (512, 4), (2048, 16))
ROPE_THETA = 500000.0
ROPE_DIMS = HD // 4
ALPHA = (2.0 * 1) ** 0.25
LN_EPS = 1e-5
RMS_EPS = 1e-6
ADAM_LR = 0.001
ADAM_B1 = 0.9
ADAM_B2 = 0.999
ADAM_EPS = 1e-08
ADAM_WD = 0.01
ADAM_STEP = 10

LANES = 128
SUBLANES = 8
TQ = 512
FPAD = LANES
NEG = -1e30
VMEM_LIMIT = 56 * 1024 * 1024
MESH = pl.DeviceIdType.MESH


def _cparams(sem):
    return pltpu.CompilerParams(dimension_semantics=sem, vmem_limit_bytes=VMEM_LIMIT)


def _pick(n, cands):
    for c in cands:
        if n % c == 0:
            return c
    return n


def _rsum8(v):
    tm, w = v.shape
    return jnp.sum(v.reshape(tm // SUBLANES, SUBLANES, w), axis=0)


def _sigmoid(x):
    return 1.0 / (1.0 + jnp.exp(-x))


def _dot(a, b):
    return jnp.dot(a, b, preferred_element_type=F32)


def _dot_nt(a, b):
    return lax.dot_general(a, b, (((1,), (1,)), ((), ())), preferred_element_type=F32)


def _dot_tn(a, b):
    return lax.dot_general(a, b, (((0,), (0,)), ((), ())), preferred_element_type=F32)


def _rowwise(name, fn, T, tm, *, tiles=(), halos=(), seqvecs=(), consts=(), outs=(), accs=(), seqaccs=(),
             seq_len=None):
    seq_len = S if seq_len is None else seq_len
    nb = T // tm
    spb = max(seq_len // tm, 1)
    nseq = max(T // seq_len, 1)
    n8 = T // SUBLANES
    r8 = tm // SUBLANES
    in_specs, args = [], []
    for a in tiles:
        in_specs.append(pl.BlockSpec((tm, a.shape[1]), lambda i: (i, 0)))
        args.append(a)
    for a, direction in halos:
        if direction < 0:
            idx = lambda i: (jnp.maximum(i * r8 - 1, 0), 0)
        else:
            idx = lambda i: (jnp.minimum((i + 1) * r8, n8 - 1), 0)
        in_specs.append(pl.BlockSpec((SUBLANES, a.shape[1]), idx))
        args.append(a)
    for a in seqvecs:
        in_specs.append(pl.BlockSpec((1, 1, a.shape[2]), lambda i: (i // spb, 0, 0)))
        args.append(a)
    for a in consts:
        in_specs.append(pl.BlockSpec(a.shape, lambda i, nd=a.ndim: (0,) * nd))
        args.append(a)
    out_shape, out_specs = [], []
    for w, dt in outs:
        out_shape.append(jax.ShapeDtypeStruct((T, w), dt))
        out_specs.append(pl.BlockSpec((tm, w), lambda i: (i, 0)))
    for w in accs:
        out_shape.append(jax.ShapeDtypeStruct((SUBLANES, w), F32))
        out_specs.append(pl.BlockSpec((SUBLANES, w), lambda i: (0, 0)))
    for w in seqaccs:
        out_shape.append(jax.ShapeDtypeStruct((nseq, SUBLANES, w), F32))
        out_specs.append(pl.BlockSpec((1, SUBLANES, w), lambda i: (i // spb, 0, 0)))
    n_t, n_h, n_s, n_c = len(tiles), len(halos), len(seqvecs), len(consts)
    n_o, n_a, n_sa = len(outs), len(accs), len(seqaccs)

    def body(*refs):
        i = pl.program_id(0)
        ins = refs[:n_t + n_h + n_s + n_c]
        orefs = refs[n_t + n_h + n_s + n_c:]
        vals = [r[...] for r in ins[:n_t + n_h]]
        vals += [r[0] for r in ins[n_t + n_h:n_t + n_h + n_s]]
        vals += list(ins[n_t + n_h + n_s:])
        res = fn(i, *vals)
        if not isinstance(res, (tuple, list)):
            res = (res,)
        for k in range(n_o):
            orefs[k][...] = res[k].astype(orefs[k].dtype)
        for k in range(n_a):
            r = orefs[n_o + k]

            @pl.when(i == 0)
            def _():
                r[...] = jnp.zeros_like(r)

            r[...] += res[n_o + k]

            @pl.when(i == nb - 1)
            def _():
                r[...] = jnp.broadcast_to(jnp.sum(r[...], axis=0, keepdims=True), r.shape)
        for k in range(n_sa):
            r = orefs[n_o + n_a + k]

            @pl.when(i % spb == 0)
            def _():
                r[...] = jnp.zeros_like(r)

            r[0] += res[n_o + n_a + k]

            @pl.when(i % spb == spb - 1)
            def _():
                r[0] = jnp.broadcast_to(jnp.sum(r[0], axis=0, keepdims=True), r.shape[1:])

    sem = ("arbitrary",) if (n_a or n_sa) else ("parallel",)
    res = pl.pallas_call(
        body, name=name, grid=(nb,), in_specs=in_specs, out_specs=out_specs, out_shape=out_shape,
        compiler_params=_cparams(sem),
    )(*args)
    return res


def _ln_fwd(r, g, b):
    mu = jnp.mean(r, axis=-1, keepdims=True)
    xc = r - mu
    var = jnp.mean(xc * xc, axis=-1, keepdims=True)
    rstd = lax.rsqrt(var + LN_EPS)
    n = xc * rstd
    return n * g + b, n, rstd


def _ln_bwd(dy, n, rstd, g):
    dn = dy * g
    return rstd * (dn - jnp.mean(dn, axis=-1, keepdims=True) - n * jnp.mean(dn * n, axis=-1, keepdims=True))


def _head_mean(t, g_ref):
    gw = g_ref.shape[0]
    hi = t.astype(BF16)
    lo = (t - hi.astype(F32)).astype(BF16)
    g = g_ref[...]
    parts = []
    for c in range(t.shape[1] // gw):
        sl = slice(c * gw, (c + 1) * gw)
        parts.append(_dot(hi[:, sl], g) + _dot(lo[:, sl], g))
    out = parts[0] if len(parts) == 1 else jnp.concatenate(parts, axis=1)
    return out * (1.0 / HD)


def _rope(z, c, s1, s2):
    w = z.shape[1]
    half = ROPE_DIMS // 2
    return z * c + pltpu.roll(z, half, 1) * s1 + pltpu.roll(z, w - half, 1) * s2


def _tile_lanes(t, w):
    reps = w // t.shape[1]
    return t if reps == 1 else jnp.concatenate([t] * reps, axis=1)


def _conv_taps(ext, prev, first):
    tm = ext.shape[0]
    prev = jnp.where(first, jnp.zeros_like(prev), prev)
    r8 = lax.broadcasted_iota(jnp.int32, (SUBLANES, 1), 0)
    top = ext[0:SUBLANES]
    s1_top = jnp.where(r8 < 1, pltpu.roll(prev, 1, 0), pltpu.roll(top, 1, 0))
    s2_top = jnp.where(r8 < 2, pltpu.roll(prev, 2, 0), pltpu.roll(top, 2, 0))
    s1 = jnp.concatenate([s1_top, pltpu.roll(ext, 1, 0)[SUBLANES:]], axis=0)
    s2 = jnp.concatenate([s2_top, pltpu.roll(ext, 2, 0)[SUBLANES:]], axis=0)
    return s1, s2


def _conv_taps_up(ext, nxt, last):
    tm = ext.shape[0]
    nxt = jnp.where(last, jnp.zeros_like(nxt), nxt)
    r8 = lax.broadcasted_iota(jnp.int32, (SUBLANES, 1), 0)
    bot = ext[tm - SUBLANES:tm]
    u1_bot = jnp.where(r8 >= 7, pltpu.roll(nxt, 7, 0), pltpu.roll(bot, 7, 0))
    u2_bot = jnp.where(r8 >= 6, pltpu.roll(nxt, 6, 0), pltpu.roll(bot, 6, 0))
    u1 = jnp.concatenate([pltpu.roll(ext, tm - 1, 0)[:tm - SUBLANES], u1_bot], axis=0)
    u2 = jnp.concatenate([pltpu.roll(ext, tm - 2, 0)[:tm - SUBLANES], u2_bot], axis=0)
    return u1, u2


def _nt_rows(av, w_ref):
    n = w_ref.shape[0]
    ch = _pick(n, (512, 256, 128))
    ab = av.astype(BF16)
    parts = [_dot_nt(ab, w_ref[c * ch:(c + 1) * ch, :]) for c in range(n // ch)]
    return parts[0] if len(parts) == 1 else jnp.concatenate(parts, axis=1)


def _mm_tn(name, a, b, *, mod=None, tt=512, t2=None, by_chip=False):
    T, k1 = a.shape
    k2 = b.shape[1]
    t1 = k1 if k1 <= 1536 else _pick(k1, (1408, 1024, 512, 256, 128))
    if t2 is None:
        t2 = k2 if k2 <= 1536 else _pick(k2, (1408, 1024, 640, 512, 256, 128))
    wc = k2 // NCHIP
    if by_chip:
        t2 = 2 * wc
    tt = min(tt, S)
    spb = S // tt

    def body(*refs):
        if mod is not None:
            a_ref, sc_ref, sh_ref, b_ref, o_ref = refs
        else:
            a_ref, b_ref, o_ref = refs
        t = pl.program_id(2)

        @pl.when(t == 0)
        def _():
            o_ref[...] = jnp.zeros_like(o_ref)

        av = a_ref[...]
        if mod is not None:
            av = av * (1.0 + sc_ref[0]) + sh_ref[0]
        res = _dot_tn(av.astype(BF16), b_ref[...].astype(BF16))
        if by_chip:
            o_ref[0] += res[:, :wc]
            o_ref[1] += res[:, wc:]
        else:
            o_ref[...] += res

    in_specs = [pl.BlockSpec((tt, t1), lambda p, q, t: (t, p))]
    args = [a]
    if mod is not None:
        for v in mod:
            in_specs.append(pl.BlockSpec((1, 1, t1), lambda p, q, t: (t // spb, 0, p)))
            args.append(v)
    in_specs.append(pl.BlockSpec((tt, t2), lambda p, q, t: (t, q)))
    args.append(b)
    if by_chip:
        out_specs = pl.BlockSpec((2, t1, wc), lambda p, q, t: (q, p, 0))
        out_shape = jax.ShapeDtypeStruct((NCHIP, k1, wc), F32)
    else:
        out_specs = pl.BlockSpec((t1, t2), lambda p, q, t: (p, q))
        out_shape = jax.ShapeDtypeStruct((k1, k2), F32)
    return pl.pallas_call(
        body, name=name, grid=(k1 // t1, k2 // t2, T // tt), in_specs=in_specs, out_specs=out_specs,
        out_shape=out_shape, compiler_params=_cparams(("parallel", "parallel", "arbitrary")),
    )(*args)


def _mod_mm(name, x, sc, sh, ws, out_dtypes, rope=None, rope_secs=(), tm=256):
    T = x.shape[0]
    nw = len(ws)

    def fn(i, xv, *rest):
        if rope is not None:
            cv, s1v, s2v = rest[:3]
            rest = rest[3:]
        scv, shv = rest[:2]
        w_refs = rest[2:]
        h = (xv * (1.0 + scv) + shv).astype(BF16)
        res = []
        for k, w_ref in enumerate(w_refs):
            n = w_ref.shape[1]
            ch = WA if (k == 0 and rope is not None) else _pick(n, (512, 256, 128))
            parts = []
            for c in range(n // ch):
                z = _dot(h, w_ref[:, c * ch:(c + 1) * ch])
                if k == 0 and c in rope_secs:
                    z = _rope(z, _tile_lanes(cv, ch), _tile_lanes(s1v, ch), _tile_lanes(s2v, ch))
                parts.append(z.astype(out_dtypes[k]))
            res.append(parts[0] if len(parts) == 1 else jnp.concatenate(parts, axis=1))
        return tuple(res)

    tiles = (x,) + (tuple(rope) if rope is not None else ())
    outs = tuple((w.shape[1], dt) for w, dt in zip(ws, out_dtypes))
    return _rowwise(name, fn, T, tm, tiles=tiles, seqvecs=(sc, sh), consts=tuple(ws), outs=outs)


def _tri(tb, lower):
    r = lax.broadcasted_iota(jnp.int32, (tb, tb), 0)
    c = lax.broadcasted_iota(jnp.int32, (tb, tb), 1)
    return jnp.where((r >= c) if lower else (r <= c), 1.0, 0.0).astype(BF16)


def _split3(x):
    hi = x.astype(BF16)
    r = x - hi.astype(F32)
    mid = r.astype(BF16)
    return hi, mid, (r - mid.astype(F32)).astype(BF16)


def _cumsum_seq(name, ins, consts, fn_in, fn_out, outs, reverse, n_acc=0, tb=256):
    T = ins[0].shape[0]
    tb = min(tb, S)
    nbs = S // tb
    nseq = T // S
    n_i, n_c, n_o = len(ins), len(consts), len(outs)

    def blk(b, j):
        return (b * nbs + (nbs - 1 - j if reverse else j), 0)

    def body(*refs):
        i_refs, c_refs = refs[:n_i], refs[n_i:n_i + n_c]
        o_refs = refs[n_i + n_c:n_i + n_c + n_o]
        acc_refs = refs[n_i + n_c + n_o:n_i + n_c + n_o + n_acc]
        carry = refs[-1]
        b, j = pl.program_id(0), pl.program_id(1)

        @pl.when(j == 0)
        def _():
            carry[...] = jnp.zeros_like(carry)

        iv = [r[...] for r in i_refs]
        xin = fn_in(*iv, *c_refs)
        tri = _tri(tb, not reverse)
        cum = sum(_dot(tri, piece) for piece in _split3(xin)) + carry[0:1, :]
        carry[...] = carry[...] + jnp.sum(xin, axis=0, keepdims=True)
        res = fn_out(cum, *iv, *c_refs)
        for o, r in zip(o_refs, res):
            o[...] = r.astype(o.dtype)
        for a in acc_refs:
            @pl.when((b == 0) & (j == 0))
            def _():
                a[...] = jnp.zeros_like(a)

            a[...] += _rsum8(res[0])

            @pl.when((b == nseq - 1) & (j == nbs - 1))
            def _():
                a[...] = jnp.broadcast_to(jnp.sum(a[...], axis=0, keepdims=True), a.shape)

    in_specs = [pl.BlockSpec((tb, a.shape[1]), blk) for a in ins]
    in_specs += [pl.BlockSpec(c.shape, lambda b, j, nd=c.ndim: (0,) * nd) for c in consts]
    out_shape = [jax.ShapeDtypeStruct((T, w), dt) for w, dt in outs]
    out_shape += [jax.ShapeDtypeStruct((SUBLANES, outs[0][0]), F32)] * n_acc
    out_specs = [pl.BlockSpec((tb, w), blk) for w, _ in outs]
    out_specs += [pl.BlockSpec((SUBLANES, outs[0][0]), lambda b, j: (0, 0))] * n_acc
    return pl.pallas_call(
        body, name=name, grid=(nseq, nbs), in_specs=in_specs, out_specs=out_specs, out_shape=out_shape,
        scratch_shapes=[pltpu.VMEM((SUBLANES, FPAD), F32)],
        compiler_params=_cparams(("arbitrary", "arbitrary")),
    )(*ins, *consts)


def _log_sigmoid(x):
    return jnp.minimum(x, 0.0) - jnp.log(1.0 + jnp.exp(-jnp.abs(x)))


def _dil_bias(tq):
    max_win = max(w for w, _ in PATTERNS)
    nd = (max_win + tq - 1) // tq + 1
    qi = np.arange(tq)[:, None]
    kj = np.arange(tq)[None, :]
    tabs = []
    for dlt in range(nd):
        dist = dlt * tq + qi - kj
        mult = np.zeros((tq, tq), np.float64)
        for win, dil in PATTERNS:
            mult += (dist >= 0) & (dist % dil == 0) & (dist // dil <= win // dil)
        tabs.append(np.where(mult > 0, np.log(np.maximum(mult, 1.0)), NEG))
    return np.stack(tabs).astype(np.float32)


def _fold_tables(nha):
    hp_n = nha // 2
    pq = np.zeros((3, FPAD, hp_n * 2 * LANES), np.float32)
    pk = np.zeros((3, FPAD, hp_n * LANES), np.float32)
    oq = np.zeros((1, hp_n * 2 * LANES), np.float32)
    ok = np.zeros((1, hp_n * LANES), np.float32)
    sq = np.zeros((hp_n * LANES, FPAD), np.float32)
    sk = np.zeros((hp_n * LANES, FPAD), np.float32)
    for h in range(nha):
        hp, odd = divmod(h, 2)
        qb = hp * 2 * LANES + odd * (LANES + 8)
        kb = hp * LANES + odd * 8
        for i in range(3):
            pq[i, h, qb + i] = 1
            oq[0, qb + 3 + i] = 1
            ok[0, kb + i] = 1
            pk[i, h, kb + 3 + i] = 1
        sq[kb, h] = 1
        sk[kb + 3, h] = 1
    return pq, pk, oq, ok, sq, sk


def _stack_heads(x2, h0, extra=None):
    z = jnp.zeros_like(x2)
    a, b = jnp.where(h0, x2, z), jnp.where(h0, z, x2)
    if extra is not None:
        a = jnp.concatenate([a, extra[:, :LANES]], axis=1)
        b = jnp.concatenate([b, extra[:, LANES:]], axis=1)
    return jnp.concatenate([a, b], axis=0)


def _attn_fwd(name, qkv, secs, fox, eq=None, ek=None, bias=None, ride=()):
    T = qkv.shape[0]
    nq = S // TQ
    nbl = T // S
    hp_n = WA // LANES
    sq, sk, sv = (s * hp_n for s in secs)
    scale = HD ** -0.5
    nd = None if fox else bias.shape[0]

    n_r = len(ride)
    n_in = (5 if fox else 4) + n_r

    def body(*refs):
        if fox:
            q_ref, k_ref, v_ref, eq_ref, ek_ref = refs[:5]
        else:
            q_ref, k_ref, v_ref, b_ref = refs[:4]
        o_ref, lse_ref = refs[n_in:n_in + 2]
        i = pl.program_id(2)
        if n_r:
            ride_start, ride_finish = _gather_halves_steps(
                refs[n_in - n_r:n_in], refs[n_in + 2:n_in + 2 + n_r], *refs[n_in + 2 + n_r:])
            at = lambda b, hp, q: (pl.program_id(0) == b) & (pl.program_id(1) == hp) & (i == q)
            pl.when(at(0, 0, 0))(ride_start)
        lane = lax.broadcasted_iota(jnp.int32, (1, LANES), 1)
        h0 = lane < HD
        q2 = (q_ref[...].astype(F32) * scale).astype(BF16)
        qs = _stack_heads(q2, h0, eq_ref[...] if fox else None)

        def scores(t, diag):
            off = pl.multiple_of((i - t) * TQ, TQ)
            kk = k_ref[pl.ds(off, TQ), :]
            if fox:
                kk = jnp.concatenate([kk, ek_ref[pl.ds(off, TQ), :]], axis=1)
            s = jnp.concatenate([_dot_nt(qs[:TQ], kk), _dot_nt(qs[TQ:], kk)], axis=0)
            if not fox:
                s = (s.reshape(2, TQ, TQ) + b_ref[t]).reshape(2 * TQ, TQ)
            elif diag:
                rows = lax.broadcasted_iota(jnp.int32, (2, TQ, TQ), 1).reshape(2 * TQ, TQ)
                cols = lax.broadcasted_iota(jnp.int32, (2 * TQ, TQ), 1)
                s = jnp.where(cols <= rows, s, NEG)
            return s

        def update(t, s, m, l, acc):
            off = pl.multiple_of((i - t) * TQ, TQ)
            v2 = v_ref[pl.ds(off, TQ), :]
            m_new = jnp.maximum(m, jnp.max(s, axis=1, keepdims=True))
            p = jnp.exp(s - m_new)
            a = jnp.exp(m - m_new)
            l = a * l + jnp.sum(p, axis=1, keepdims=True)
            pb = p.astype(BF16)
            acc = a * acc + jnp.concatenate([_dot(pb[:TQ], v2), _dot(pb[TQ:], v2)], axis=0)
            return m_new, l, acc

        init = (jnp.full((2 * TQ, 1), NEG, F32), jnp.zeros((2 * TQ, 1), F32), jnp.zeros((2 * TQ, LANES), F32))
        n = i + 1 if fox else jnp.minimum(i + 1, nd)
        m, l, acc = update(0, scores(0, True), *init)
        m, l, acc = lax.fori_loop(1, n, lambda t, c: update(t, scores(t, False), *c), (m, l, acc))
        on = acc / l
        o_ref[...] = jnp.where(h0, on[:TQ], on[TQ:])
        lse = jnp.broadcast_to(m + jnp.log(l), (2 * TQ, LANES))
        lse_ref[...] = jnp.concatenate([lse[:TQ], lse[TQ:]], axis=1)
        if n_r:
            pl.when(at(nbl - 1, hp_n - 1, nq - 1))(ride_finish)

    in_specs = [
        pl.BlockSpec((TQ, LANES), lambda b, hp, i: (b * nq + i, sq + hp)),
        pl.BlockSpec((S, LANES), lambda b, hp, i: (b, sk + hp)),
        pl.BlockSpec((S, LANES), lambda b, hp, i: (b, sv + hp)),
    ]
    args = [qkv, qkv, qkv]
    if fox:
        in_specs += [pl.BlockSpec((TQ, 2 * LANES), lambda b, hp, i: (b * nq + i, hp)),
                     pl.BlockSpec((S, LANES), lambda b, hp, i: (b, hp))]
        args += [eq, ek]
    else:
        in_specs.append(pl.BlockSpec(bias.shape, lambda b, hp, i: (0, 0, 0)))
        args.append(bias)
    any_spec = pl.BlockSpec(memory_space=pl.ANY)
    out_specs = [pl.BlockSpec((TQ, LANES), lambda b, hp, i: (b * nq + i, hp)),
                 pl.BlockSpec((TQ, 2 * LANES), lambda b, hp, i: (b * nq + i, hp))] + [any_spec] * n_r
    out_shape = [jax.ShapeDtypeStruct((T, WA), F32), jax.ShapeDtypeStruct((T, 2 * WA), F32)]
    out_shape += [jax.ShapeDtypeStruct((NCHIP - 1,) + v.shape, v.dtype) for v in ride]
    sems = [pltpu.SemaphoreType.DMA((6 * n_r,)), pltpu.SemaphoreType.DMA((6 * n_r,))] if n_r else []
    sem = ("arbitrary",) * 3 if n_r else ("parallel", "parallel", "arbitrary")
    return pl.pallas_call(
        body, name=name, grid=(nbl, hp_n, nq), in_specs=in_specs + [any_spec] * n_r, out_specs=out_specs,
        out_shape=out_shape, scratch_shapes=sems, compiler_params=_cparams(sem),
    )(*args, *ride)


def _attn_bwd(name, qkv, secs, o, do, do_sec, lse, fox, eq=None, ek=None, bias=None, ride=None):
    T = qkv.shape[0]
    nq = S // TQ
    nbl = T // S
    hp_n = WA // LANES
    sq, sk, sv = (s * hp_n for s in secs)
    dsec = do_sec * hp_n
    scale = HD ** -0.5
    nd = None if fox else bias.shape[0]
    kc = 2 * LANES if fox else LANES

    r_in, r_out, r_copies = ride if ride else ((), (), ())
    n_bi, n_bo = (8, 5) if fox else (7, 3)
    n_i = n_bi + len(r_in)

    def body(*refs):
        if fox:
            q_ref, k_ref, v_ref, o_ref, do_ref, lse_ref, eq_ref, ek_ref = refs[:n_bi]
            dq_ref, dk_ref, dv_ref, dqe_ref, dek_ref = refs[n_i:n_i + n_bo]
        else:
            q_ref, k_ref, v_ref, o_ref, do_ref, lse_ref, b_ref = refs[:n_bi]
            dq_ref, dk_ref, dv_ref = refs[n_i:n_i + n_bo]
        dl_ref = refs[n_i + n_bo + len(r_out)]
        j = pl.program_id(2)
        if ride:
            ride_start, ride_finish = _remote_steps(
                refs[n_bi:n_i], refs[n_i + n_bo:n_i + n_bo + len(r_out)], *refs[n_i + n_bo + len(r_out) + 1:], r_copies)
            at = lambda b, hp, q: (pl.program_id(0) == b) & (pl.program_id(1) == hp) & (j == q)
            pl.when(at(0, 0, 0))(ride_start)
        lane = lax.broadcasted_iota(jnp.int32, (1, LANES), 1)
        h0 = lane < HD

        @pl.when(j == 0)
        def _():
            dq_ref[...] = jnp.zeros_like(dq_ref)
            if fox:
                dqe_ref[...] = jnp.zeros_like(dqe_ref)

            def dl_step(r, c):
                off = pl.multiple_of(r * TQ, TQ)
                d2 = do_ref[pl.ds(off, TQ), :] * o_ref[pl.ds(off, TQ), :]
                z2 = jnp.zeros_like(d2)
                dl0 = jnp.sum(jnp.where(h0, d2, z2), axis=1, keepdims=True)
                dl1 = jnp.sum(jnp.where(h0, z2, d2), axis=1, keepdims=True)
                dl_ref[pl.ds(off, TQ), :] = jnp.concatenate(
                    [jnp.broadcast_to(dl0, (TQ, LANES)), jnp.broadcast_to(dl1, (TQ, LANES))], axis=1)
                return c

            lax.fori_loop(0, nq, dl_step, 0)

        kk = k_ref[...]
        if fox:
            kk = jnp.concatenate([kk, ek_ref[...]], axis=1)
        v2 = v_ref[...]

        def wide(x2):
            st = jnp.concatenate([x2[:, :LANES], x2[:, LANES:]], axis=0)
            return st if TQ == LANES else jnp.concatenate([st] * (TQ // LANES), axis=1)

        def step(t, carry, diag):
            dkk, dv2 = carry
            off = pl.multiple_of((j + t) * TQ, TQ)
            q2 = (q_ref[pl.ds(off, TQ), :].astype(F32) * scale).astype(BF16)
            qs = _stack_heads(q2, h0, eq_ref[pl.ds(off, TQ), :] if fox else None)
            dos = _stack_heads(do_ref[pl.ds(off, TQ), :].astype(BF16), h0)
            s = jnp.concatenate([_dot_nt(qs[:TQ], kk), _dot_nt(qs[TQ:], kk)], axis=0)
            if not fox:
                s = (s.reshape(2, TQ, TQ) + b_ref[t]).reshape(2 * TQ, TQ)
            elif diag:
                rows = lax.broadcasted_iota(jnp.int32, (2, TQ, TQ), 1).reshape(2 * TQ, TQ)
                cols = lax.broadcasted_iota(jnp.int32, (2 * TQ, TQ), 1)
                s = jnp.where(cols <= rows, s, NEG)
            p = jnp.exp(s - wide(lse_ref[pl.ds(off, TQ), :]))
            dp = jnp.concatenate([_dot_nt(dos[:TQ], v2), _dot_nt(dos[TQ:], v2)], axis=0)
            dsb = (p * (dp - wide(dl_ref[pl.ds(off, TQ), :]))).astype(BF16)
            dv2 = dv2 + _dot_tn(p.astype(BF16), dos)
            dkk = dkk + _dot_tn(dsb, qs)
            dqq = jnp.concatenate([_dot(dsb[:TQ], kk), _dot(dsb[TQ:], kk)], axis=0)
            dq_ref[pl.ds(off, TQ), :] += jnp.where(h0, dqq[:TQ, :LANES], dqq[TQ:, :LANES])
            if fox:
                dqe_ref[pl.ds(off, TQ), :] += jnp.where(lane < SUBLANES, dqq[:TQ, LANES:], dqq[TQ:, LANES:])
            return dkk, dv2

        zero = (jnp.zeros((TQ, kc), F32), jnp.zeros((TQ, LANES), F32))
        if fox:
            dkk, dv2 = lax.fori_loop(1, nq - j, lambda t, c: step(t, c, False), step(0, zero, True))
        else:
            dkk, dv2 = lax.fori_loop(0, jnp.minimum(nq - j, nd), lambda t, c: step(t, c, False), zero)
        dk_ref[...] = dkk[:, :LANES]
        dv_ref[...] = dv2
        if fox:
            dek_ref[...] = dkk[:, LANES:]

        @pl.when(j == nq - 1)
        def _():
            dq_ref[...] = dq_ref[...] * scale

        if ride:
            pl.when(at(nbl - 1, hp_n - 1, nq - 1))(ride_finish)

    seq = lambda c, w=LANES: pl.BlockSpec((S, w), lambda b, hp, j: (b, c + hp))
    blk = lambda c: pl.BlockSpec((TQ, LANES), lambda b, hp, j: (b * nq + j, c + hp))
    in_specs = [seq(sq), blk(sk), blk(sv), seq(0), seq(dsec), seq(0, 2 * LANES)]
    args = [qkv, qkv, qkv, o, do, lse]
    if fox:
        in_specs += [seq(0, 2 * LANES), blk(0)]
        args += [eq, ek]
    else:
        in_specs.append(pl.BlockSpec(bias.shape, lambda b, hp, j: (0, 0, 0)))
        args.append(bias)
    out_specs = [seq(0), blk(0), blk(0)]
    out_shape = [jax.ShapeDtypeStruct((T, WA), F32)] * 3
    if fox:
        out_specs += [seq(0), blk(0)]
        out_shape += [jax.ShapeDtypeStruct((T, WA), F32)] * 2
    any_spec = pl.BlockSpec(memory_space=pl.ANY)
    scratch = [pltpu.VMEM((S, 2 * LANES), F32)]
    if ride:
        scratch += [pltpu.SemaphoreType.DMA((len(r_copies),)), pltpu.SemaphoreType.DMA((len(r_copies),))]
    sem = ("arbitrary",) * 3 if ride else ("parallel", "parallel", "arbitrary")
    return pl.pallas_call(
        body, name=name, grid=(nbl, hp_n, nq), in_specs=in_specs + [any_spec] * len(r_in),
        out_specs=out_specs + [any_spec] * len(r_out), out_shape=out_shape + list(r_out),
        scratch_shapes=scratch, compiler_params=_cparams(sem),
    )(*args, *r_in)


def _remote_steps(in_refs, out_refs, send_sems, recv_sems, remote):
    me = (lax.axis_index("x"), lax.axis_index("y"), lax.axis_index("c"))

    def peer_of(flip):
        return tuple(1 - v if f else v for v, f in zip(me, flip))

    def at(ref, idx):
        return ref if idx is None else ref.at[idx]

    def rcopy(k, who):
        flip, a, sfn, b, dfn = remote[k]
        return pltpu.make_async_remote_copy(
            src_ref=at(in_refs[a], sfn(*who)), dst_ref=at(out_refs[b], dfn(*who)),
            send_sem=send_sems.at[k], recv_sem=recv_sems.at[k], device_id=peer_of(flip), device_id_type=MESH)

    def start():
        for k in range(len(remote)):
            rcopy(k, me).start()

    def finish():
        for k in range(len(remote)):
            rcopy(k, peer_of(remote[k][0])).wait_recv()
        for k in range(len(remote)):
            rcopy(k, me).wait_send()

    return start, finish


def _exchange(name, ins, out_shapes, remote, local):
    n_in, n_out = len(ins), len(out_shapes)
    nr, nl = len(remote), len(local)

    def body(*refs):
        in_refs = refs[:n_in]
        out_refs = refs[n_in:n_in + n_out]
        send_sems, recv_sems, loc_sems = refs[n_in + n_out:]
        me = (lax.axis_index("x"), lax.axis_index("y"), lax.axis_index("c"))
        at = lambda ref, idx: ref if idx is None else ref.at[idx]
        locs = [pltpu.make_async_copy(at(in_refs[a], sfn(*me)), at(out_refs[b], dfn(*me)), loc_sems.at[k])
                for k, (a, sfn, b, dfn) in enumerate(local)]
        for cp in locs:
            cp.start()
        start, finish = _remote_steps(in_refs, out_refs, send_sems, recv_sems, remote)
        start()
        finish()
        for cp in locs:
            cp.wait()

    any_spec = pl.BlockSpec(memory_space=pl.ANY)
    return pl.pallas_call(
        body, name=name, in_specs=[any_spec] * n_in, out_specs=[any_spec] * n_out, out_shape=list(out_shapes),
        scratch_shapes=[pltpu.SemaphoreType.DMA((max(nr, 1),)), pltpu.SemaphoreType.DMA((max(nr, 1),)),
                        pltpu.SemaphoreType.DMA((max(nl, 1),))],
    )(*ins)


_FLIPS7 = [(0, 0, 1), (0, 1, 0), (0, 1, 1), (1, 0, 0), (1, 0, 1), (1, 1, 0), (1, 1, 1)]
_CHIP_FLIPS = [(1, 0, 0), (0, 1, 0), (1, 1, 0)]


def _dev_index(x, y, c):
    return 4 * x + 2 * y + c


def _chip_index(x, y, c):
    return 2 * x + y


def _all_gather8(name, v):
    remote = [(f, 0, lambda x, y, c: None, 0, _dev_index) for f in _FLIPS7]
    local = [(0, lambda x, y, c: None, 0, _dev_index)]
    return _exchange(name, [v], [jax.ShapeDtypeStruct((NDEV,) + v.shape, v.dtype)], remote, local)[0]


def _gather_halves_steps(in_refs, out_refs, send_sems, recv_sems):
    n_v = len(in_refs)
    x, y, c = lax.axis_index("x"), lax.axis_index("y"), lax.axis_index("c")
    sibling = (x, y, 1 - c)
    chips = [(1 - x, y), (x, 1 - y), (1 - x, 1 - y)]

    def copy(k, n, src, blk, half, to):
        return pltpu.make_async_remote_copy(
            src_ref=src, dst_ref=out_refs[n].at[blk, half], send_sem=send_sems.at[k], recv_sem=recv_sems.at[k],
            device_id=to, device_id_type=MESH)

    def first():
        return [copy(6 * n + j, n, in_refs[n].at[c], j, c, (*chip, c))
                for n in range(n_v) for j, chip in enumerate(chips)]

    def start():
        for cp in first():
            cp.start()

    def finish():
        passed = []
        for n in range(n_v):
            for j, chip in enumerate(chips):
                copy(6 * n + j, n, in_refs[n].at[c], j, c, (*chip, c)).wait_recv()
                fw = copy(6 * n + 3 + j, n, out_refs[n].at[j, c], j, c, sibling)
                fw.start()
                passed.append(fw)
        for n in range(n_v):
            for j in range(len(chips)):
                copy(6 * n + 3 + j, n, out_refs[n].at[j, 1 - c], j, 1 - c, sibling).wait_recv()
        for cp in first() + passed:
            cp.wait_send()

    return start, finish


def _gather_halves(name, vs):
    n_v = len(vs)

    def body(*refs):
        start, finish = _gather_halves_steps(refs[:n_v], refs[n_v:2 * n_v], *refs[2 * n_v:])
        start()
        finish()

    any_spec = pl.BlockSpec(memory_space=pl.ANY)
    return pl.pallas_call(
        body, name=name, in_specs=[any_spec] * n_v, out_specs=[any_spec] * n_v,
        out_shape=[jax.ShapeDtypeStruct((NCHIP - 1,) + v.shape, v.dtype) for v in vs],
        scratch_shapes=[pltpu.SemaphoreType.DMA((6 * n_v,)), pltpu.SemaphoreType.DMA((6 * n_v,))],
    )(*vs)


def _by_chip(own, others, chip):
    stacked = jnp.concatenate([own[None], others], axis=0)
    blocks = []
    for k in range(NCHIP):
        d = k ^ chip
        place = jnp.where(d == 0, 0, jnp.where(d == 2, 1, jnp.where(d == 1, 2, 3)))
        blocks.append(lax.dynamic_index_in_dim(stacked, place, axis=0, keepdims=False))
    return jnp.stack(blocks)


def _sum_leading(name, v, tm=None):
    n, r, w = v.shape
    tm = _pick(r, (256, 128, 64, 32, 16, 8)) if tm is None else tm

    def body(v_ref, o_ref):
        acc = v_ref[0].astype(F32)
        for k in range(1, n):
            acc = acc + v_ref[k].astype(F32)
        o_ref[...] = acc

    return pl.pallas_call(
        body, name=name, grid=(r // tm,), in_specs=[pl.BlockSpec((n, tm, w), lambda i: (0, i, 0))],
        out_specs=pl.BlockSpec((tm, w), lambda i: (i, 0)), out_shape=jax.ShapeDtypeStruct((r, w), F32),
        compiler_params=_cparams(("parallel",)),
    )(v)


def _add2(name, a, b, tm=None, out_dtype=F32):
    r, w = a.shape
    tm = _pick(r, (256, 128, 64, 32, 16, 8)) if tm is None else tm

    def body(a_ref, b_ref, o_ref):
        o_ref[...] = (a_ref[...] + b_ref[...]).astype(out_dtype)

    spec = pl.BlockSpec((tm, w), lambda i: (i, 0))
    return pl.pallas_call(
        body, name=name, grid=(r // tm,), in_specs=[spec, spec], out_specs=spec,
        out_shape=jax.ShapeDtypeStruct((r, w), out_dtype), compiler_params=_cparams(("parallel",)),
    )(a, b)


def _ada_fwd(call_all, w_shard):
    def body(c_ref, w_ref, o_ref):
        cv = c_ref[...]
        o_ref[...] = jnp.dot(cv * _sigmoid(cv), w_ref[...], preferred_element_type=F32,
                             precision=lax.Precision.HIGHEST)

    n = w_shard.shape[1]
    return pl.pallas_call(
        body, name="ada_fwd", out_shape=jax.ShapeDtypeStruct((call_all.shape[0], n), F32),
        compiler_params=pltpu.CompilerParams(vmem_limit_bytes=VMEM_LIMIT),
    )(call_all, w_shard)


def _ada_bwd(call_all, dada):
    def body(c_ref, d_ref, o_ref):
        cv = c_ref[...]
        o_ref[...] = lax.dot_general(cv * _sigmoid(cv), d_ref[...], (((0,), (0,)), ((), ())),
                                     preferred_element_type=F32, precision=lax.Precision.HIGHEST)

    return pl.pallas_call(
        body, name="ada_bwd", out_shape=jax.ShapeDtypeStruct((call_all.shape[1], dada.shape[1]), F32),
        compiler_params=pltpu.CompilerParams(vmem_limit_bytes=VMEM_LIMIT),
    )(call_all, dada)


def _adamw(name, w, g, m, v):
    r, wd = w.shape
    tm = _pick(r, (256, 128, 64, 32, 16, 8))
    bc1 = 1.0 - ADAM_B1 ** ADAM_STEP
    bc2 = 1.0 - ADAM_B2 ** ADAM_STEP

    def body(w_ref, g_ref, m_ref, v_ref, d_ref, mo_ref, vo_ref):
        gv = g_ref[...]
        mn = ADAM_B1 * m_ref[...] + (1.0 - ADAM_B1) * gv
        vn = ADAM_B2 * v_ref[...] + (1.0 - ADAM_B2) * (gv * gv)
        d_ref[...] = -ADAM_LR * ((mn / bc1) / (jnp.sqrt(vn / bc2) + ADAM_EPS) + ADAM_WD * w_ref[...])
        mo_ref[...] = mn
        vo_ref[...] = vn

    spec = pl.BlockSpec((tm, wd), lambda i: (i, 0))
    return pl.pallas_call(
        body, name=name, grid=(r // tm,), in_specs=[spec] * 4, out_specs=[spec] * 3,
        out_shape=[jax.ShapeDtypeStruct((r, wd), F32)] * 3, compiler_params=_cparams(("parallel",)),
    )(w, g, m, v)


def _rope_tables(positions):
    half = ROPE_DIMS // 2
    freqs = ROPE_THETA ** (-jnp.arange(0, ROPE_DIMS, 2, dtype=F32) / ROPE_DIMS)
    ang = positions.astype(F32).reshape(-1, 1) * freqs
    cos, sin = jnp.cos(ang), jnp.sin(ang)
    T = ang.shape[0]
    one = jnp.ones((T, HD - ROPE_DIMS), F32)
    zero = jnp.zeros((T, HD - ROPE_DIMS), F32)
    zh = jnp.zeros((T, half), F32)
    c64 = jnp.concatenate([cos, cos, one], axis=1)
    s1 = jnp.concatenate([zh, sin, zero], axis=1)
    s2 = jnp.concatenate([-sin, zh, zero], axis=1)
    rep = lambda t: jnp.concatenate([t] * (LANES // HD), axis=1)
    return rep(c64), rep(s1), rep(s2)


def _local_step(x, loss_target, positions, ada, w_qkv, w_f, w_out, w_up, conv_w8, w_down,
                b_fgate, gn, ln1_g, ln1_b, conv_b, ln2_g, ln2_b, late=None, early=None):
    T = x.shape[0]
    nbl = T // S
    nha = WA // HD
    sv = lambda k: ada[:, k:k + 1, :]
    sh_a, sc_a, g_a, sh_f, sc_f, g_f = (sv(k) for k in range(6))
    rope = _rope_tables(positions)
    neg_rope = (rope[0], -rope[1], -rope[2])
    gseg = jnp.asarray(np.kron(np.eye(min(256, 2 * WA) // HD), np.ones((HD, HD))), BF16)
    bias = jnp.asarray(_dil_bias(TQ))
    bf_pad = jnp.zeros((1, FPAD), F32).at[:, :nha].set(b_fgate)

    qkv, fa = _mod_mm("qkv_proj", x, sc_a, sh_a, (w_qkv, w_f), (BF16, F32), rope=rope, rope_secs=(3, 4))
    pq, pk, oq, ok, sq, sk = _fold_tables(nha)

    def fold_out(cum, f, b_ref, pq_ref, pk_ref, oq_ref, ok_ref):
        hi, mid, lo = _split3(cum)
        eqv = _dot(hi, pq_ref[0]) + _dot(mid, pq_ref[1]) + _dot(lo, pq_ref[2]) + oq_ref[...]
        ekv = ok_ref[...] - (_dot(hi, pk_ref[0]) + _dot(mid, pk_ref[1]) + _dot(lo, pk_ref[2]))
        return eqv, ekv

    eq, ek = _cumsum_seq(
        "fgate_fwd", [fa], [bf_pad, jnp.asarray(pq, BF16), jnp.asarray(pk, BF16), jnp.asarray(oq), jnp.asarray(ok)],
        lambda f, b_ref, *_: _log_sigmoid(f + b_ref[...]), fold_out, ((2 * WA, BF16), (WA, BF16)), reverse=False)
    oa, lse_a, *got = _attn_fwd("fox_fwd", qkv, (0, 1, 2), True, eq=eq, ek=ek, ride=late[0] if late else ())
    if late:
        w_up, w_down = late[1](got)
    ob, lse_b = _attn_fwd("dil_fwd", qkv, (3, 4, 5), False, bias=bias)

    def mix_fn(i, oav, obv, xv, gav, gn_ref, g_ref, wo_ref, l1g_ref, l1b_ref):
        o = jnp.concatenate([oav, obv], axis=1)
        rs = lax.rsqrt(_head_mean(o * o, g_ref) + RMS_EPS)
        merged = (o * rs * gn_ref[...]).astype(BF16)
        mix = _dot(merged, wo_ref[...])
        x1, _, _ = _ln_fwd(ALPHA * xv + gav * mix, l1g_ref[...], l1b_ref[...])
        return merged, mix, x1

    merged, mix, x1 = _rowwise("mix_out", mix_fn, T, 256, tiles=(oa, ob, x), seqvecs=(g_a,),
                               consts=(gn, gseg, w_out, ln1_g, ln1_b),
                               outs=((2 * WA, BF16), (D, F32), (D, F32)))
    u = _mod_mm("ffn_up", x1, sc_f, sh_f, (w_up,), (F32,))[0]

    def conv_y(i, uv, prev, cw_ref, cb_ref, tm):
        first = (i * tm) % S == 0
        s1, s2 = _conv_taps(uv, prev, first)
        y = cb_ref[...] + cw_ref[0:1, :] * s2 + cw_ref[1:2, :] * s1 + cw_ref[2:3, :] * uv
        return y, s1, s2

    tmc = 128

    def gate_fn(i, uv, prev, cw_ref, cb_ref):
        y, _, _ = conv_y(i, uv, prev, cw_ref, cb_ref, tmc)
        a, g = y[:, :DFF], y[:, DFF:]
        return g * _sigmoid(g) * a, y

    act, yconv = _rowwise("conv_gate", gate_fn, T, tmc, tiles=(u,), halos=((u, -1),), consts=(conv_w8, conv_b),
                          outs=((DFF, BF16), (2 * DFF, F32)))

    def down_fn(i, actv, x1v, tgt, gfv, wd_ref, g2_ref, b2_ref):
        ffn = _dot(actv, wd_ref[...])
        y, n2, rstd = _ln_fwd(ALPHA * x1v + gfv * ffn, g2_ref[...], b2_ref[...])
        err = y - tgt
        dy = err * (1.0 / D)
        dr2 = _ln_bwd(dy, n2, rstd, g2_ref[...])
        return (dr2, gfv * dr2, _rsum8(err * err), _rsum8(dy * n2), _rsum8(dy), _rsum8(dr2 * ffn))

    dr2, dffn, loss_acc, d_ln2g, d_ln2b, d_gf = _rowwise(
        "ffn_down_loss", down_fn, T, 256, tiles=(act, x1, loss_target), seqvecs=(g_f,),
        consts=(w_down, ln2_g, ln2_b), outs=((D, F32), (D, F32)), accs=(D, D, D), seqaccs=(D,))

    def gate_conv_bwd_fn(i, uv, yv, dfv, y_nxt, df_nxt, cw_ref, wd_ref):
        last = ((i + 1) * tmc) % S == 0
        y = jnp.concatenate([yv, y_nxt], axis=0)
        df_ext = jnp.concatenate([dfv, df_nxt], axis=0).astype(BF16)
        ch = _pick(DFF, (256, 128))
        dav = jnp.concatenate([_dot_nt(df_ext, wd_ref[c * ch:(c + 1) * ch, :]) for c in range(DFF // ch)], axis=1)
        a, g = y[:, :DFF], y[:, DFF:]
        sg = _sigmoid(g)
        dyc_ext = jnp.concatenate([dav * (g * sg), dav * a * (sg * (1.0 + g * (1.0 - sg)))], axis=1)
        dyc = dyc_ext[:tmc]
        u1, u2 = _conv_taps_up(dyc, dyc_ext[tmc:], last)
        du_ = cw_ref[2:3, :] * dyc + cw_ref[1:2, :] * u1 + cw_ref[0:1, :] * u2
        return du_, _rsum8(dyc), _rsum8(uv * u2), _rsum8(uv * u1), _rsum8(uv * dyc)

    du, d_cb, d_cw0, d_cw1, d_cw2 = _rowwise(
        "gate_conv_bwd", gate_conv_bwd_fn, T, tmc, tiles=(u, yconv, dffn), halos=((yconv, 1), (dffn, 1)),
        consts=(conv_w8, w_down), outs=((2 * DFF, BF16),), accs=(2 * DFF,) * 4)
    g_w_down = _mm_tn("dw_down", act, dffn)
    g_w_up = _mm_tn("dw_up", x1, du, mod=(sc_f, sh_f), by_chip=True)

    def ln1_bwd_fn(i, dr2v, duv, xv, mixv, x1v, scfv, gav, l1g_ref, wu_ref):
        dh2v = _nt_rows(duv, wu_ref)
        dx1 = ALPHA * dr2v + dh2v * (1.0 + scfv)
        _, n1, rstd = _ln_fwd(ALPHA * xv + gav * mixv, l1g_ref[...], 0.0)
        dr1 = _ln_bwd(dx1, n1, rstd, l1g_ref[...])
        return (dr1, gav * dr1, _rsum8(dx1 * n1), _rsum8(dx1),
                _rsum8(dh2v * x1v), _rsum8(dh2v), _rsum8(dr1 * mixv))

    dr1, dmix, d_ln1g, d_ln1b, d_scf, d_shf, d_ga = _rowwise(
        "ln1_bwd", ln1_bwd_fn, T, 256, tiles=(dr2, du, x, mix, x1), seqvecs=(sc_f, g_a), consts=(ln1_g, w_up),
        outs=((D, F32), (D, BF16)), accs=(D, D), seqaccs=(D, D, D))

    g_w_out = _mm_tn("dw_out", merged, dmix)

    def hn_bwd_fn(i, dmixv, oav, obv, gn_ref, g_ref, wo_ref):
        dmv = _nt_rows(dmixv, wo_ref)
        o = jnp.concatenate([oav, obv], axis=1)
        rs = lax.rsqrt(_head_mean(o * o, g_ref) + RMS_EPS)
        nrm = o * rs
        dn = dmv * gn_ref[...]
        do = rs * (dn - nrm * _head_mean(dn * nrm, g_ref))
        return do, _rsum8(dmv * nrm)

    do, d_gn = _rowwise("headnorm_bwd", hn_bwd_fn, T, 256, tiles=(dmix, oa, ob), consts=(gn, gseg, w_out),
                        outs=((2 * WA, F32),), accs=(2 * WA,))
    d_cw = jnp.stack([d_cw0[0], d_cw1[0], d_cw2[0]], axis=0)
    dqa, dka, dva, dqe, dek, *arrived = _attn_bwd(
        "fox_bwd", qkv, (0, 1, 2), oa, do, 0, lse_a, True, eq=eq, ek=ek,
        ride=early(g_w_up, g_w_down, d_cw) if early else None)
    dqb, dkb, dvb = _attn_bwd("dil_bwd", qkv, (3, 4, 5), ob, do, 1, lse_b, False, bias=bias)
    hdot = lambda a, m_ref: sum(_dot(piece, m_ref[...]) for piece in _split3(a))
    dfa, d_bf = _cumsum_seq(
        "fgate_bwd", [dqe, dek, fa], [bf_pad, jnp.asarray(sq, BF16), jnp.asarray(sk, BF16)],
        lambda dq_, dk_, f, b_ref, sq_ref, sk_ref: hdot(dq_, sq_ref) - hdot(dk_, sk_ref),
        lambda cum, dq_, dk_, f, b_ref, sq_ref, sk_ref: (cum * _sigmoid(-(f + b_ref[...])),),
        ((FPAD, F32),), reverse=True, n_acc=1)

    def dz_fn(i, a0, a1, a2, b0, b1, b2, fv, cv, s1v, s2v):
        ct, s1t, s2t = (_tile_lanes(t, WA) for t in (cv, s1v, s2v))
        return jnp.concatenate([a0, a1, a2, _rope(b0, ct, s1t, s2t), _rope(b1, ct, s1t, s2t), b2, fv], axis=1)

    dz = _rowwise("dz_pack", dz_fn, T, 256, tiles=(dqa, dka, dva, dqb, dkb, dvb, dfa) + neg_rope,
                  outs=((6 * WA + FPAD, BF16),))[0]
    w_cat = jnp.concatenate([w_qkv, w_f], axis=1)
    g_w_cat = _mm_tn("dw_in", x, dz, mod=(sc_a, sh_a), tt=256, t2=dz.shape[1])

    def dx_fn(i, dr1v, dzv, xv, scav, wc_ref):
        dh1v = _nt_rows(dzv, wc_ref)
        return ALPHA * dr1v + dh1v * (1.0 + scav), _rsum8(dh1v * xv), _rsum8(dh1v)

    grad_x, d_sca, d_sha = _rowwise("dx_out", dx_fn, T, 256, tiles=(dr1, dz, x), seqvecs=(sc_a,), consts=(w_cat,),
                                    outs=((D, F32),), seqaccs=(D, D))

    row0 = lambda a: a[..., 0, :]
    d_ada = jnp.stack([row0(d_sha), row0(d_sca), row0(d_ga), row0(d_shf), row0(d_scf), row0(d_gf)], axis=1)
    loss_part = (0.5 / D) * jnp.sum(loss_acc[0])
    small = dict(b_fgate=row0(d_bf)[:nha], gn=row0(d_gn), ln1_g=row0(d_ln1g), ln1_b=row0(d_ln1b),
                 conv_b=row0(d_cb), ln2_g=row0(d_ln2g), ln2_b=row0(d_ln2b))
    big = dict(w_cat=g_w_cat, w_out=g_w_out, w_up=g_w_up, conv_w=d_cw, w_down=g_w_down, early=arrived)
    return loss_part, grad_x, d_ada, small, big


def _rows_of(n, w=None):
    return -(-n // (D if w is None else w))


def _as_rows(v):
    w = D
    k = v.shape[0]
    flat = v.reshape(k, -1)
    rows = _rows_of(_rows_of(flat.shape[1], w), SUBLANES) * SUBLANES
    flat = jnp.pad(flat, ((0, 0), (0, rows * w - flat.shape[1])))
    return flat.reshape(k, rows, w)


def kernel(x, c, positions, w_ada, b_ada, w_in, b_fgate, gn_a, gn_b, w_out, ln1_g, ln1_b, w_up, conv_w, conv_b, w_down, ln2_g, ln2_b, loss_target, m_w_ada, m_b_ada, m_w_in, m_b_fgate, m_gn_a, m_gn_b, m_w_out, m_ln1_g, m_ln1_b, m_w_up, m_conv_w, m_conv_b, m_w_down, m_ln2_g, m_ln2_b, v_w_ada, v_b_ada, v_w_in, v_b_fgate, v_gn_a, v_gn_b, v_w_out, v_ln1_g, v_ln1_b, v_w_up, v_conv_w, v_conv_b, v_w_down, v_ln2_g, v_ln2_b):
    mx, my, mc = lax.axis_index("x"), lax.axis_index("y"), lax.axis_index("c")
    dev = _dev_index(mx, my, mc)
    chip = _chip_index(mx, my, mc)
    nbl = x.shape[0]
    T = nbl * S
    nha = WA // HD
    d_in = w_in.shape[2] * NCHIP
    n_ada = w_ada.shape[2]

    c_pad = jnp.zeros((SUBLANES, D), F32).at[:nbl].set(c)
    c_all = _all_gather8("gather_c", c_pad)[:, :nbl].reshape(NDEV * nbl, D)
    ada_part = _ada_fwd(c_all, w_ada[0])
    n_cw = conv_w.shape[2]
    cw_rows = jnp.pad(conv_w[0], ((0, SUBLANES - conv_w.shape[1]), (0, n_ada - n_cw)))
    ada_blocks = _all_gather8("gather_ada", jnp.concatenate([ada_part, cw_rows], axis=0))
    n_c = NDEV * nbl
    ada_all = jnp.concatenate([ada_blocks[2 * k, :n_c] for k in range(NCHIP)], axis=1) + b_ada
    conv_w8 = jnp.concatenate([ada_blocks[2 * k, n_c:, :n_cw] for k in range(NCHIP)], axis=1)
    ada = lax.dynamic_slice_in_dim(ada_all, dev * nbl, nbl, axis=0).reshape(nbl, 6, D)

    w_in_sh = jnp.pad(w_in[0].astype(BF16), ((0, 0), (0, _rows_of(w_in.shape[2], LANES) * LANES - w_in.shape[2])))
    halve = lambda t: t.reshape(2, t.shape[0] // 2, t.shape[1])
    whole = lambda g, t: _by_chip(halve(t), g, chip).reshape((NCHIP,) + t.shape)
    w_out_sh, w_up_sh, w_down_sh = w_out[0].astype(BF16), w_up[0].astype(BF16), w_down[0].astype(BF16)
    g_in, g_out = _gather_halves("gather_w", [halve(w_in_sh), halve(w_out_sh)])
    g_in, g_out = whole(g_in, w_in_sh), whole(g_out, w_out_sh)
    w_in_full = jnp.concatenate([g_in[k][:, :w_in.shape[2]] for k in range(NCHIP)], axis=1)
    w_qkv = jnp.concatenate([w_in_full[:, :3 * WA], w_in_full[:, 3 * WA + nha:]], axis=1)
    w_f = jnp.pad(w_in_full[:, 3 * WA:3 * WA + nha], ((0, 0), (0, FPAD - nha)))
    w_out_full = g_out.reshape(NCHIP * w_out.shape[1], D)

    def late_weights(got):
        g_up, g_down = whole(got[0], w_up_sh), whole(got[1], w_down_sh)
        return (jnp.concatenate([g_up[k] for k in range(NCHIP)], axis=1),
                g_down.reshape(NCHIP * w_down.shape[1], D))

    n_in = w_in.shape[2]
    to_sib = lambda x, y, c: None

    def reduce_front(blocks, tag):
        cut = [t.reshape(2 * NCHIP, t.shape[1] // 2, t.shape[2]) for t in blocks]
        from_sib = _exchange(
            "pair_swap_" + tag, cut, [jax.ShapeDtypeStruct((NCHIP,) + t.shape[1:], F32) for t in cut],
            [((0, 0, 1), n, lambda x, y, c, k=k: 2 * k + 1 - c, n, lambda x, y, c, k=k: k)
             for n in range(len(cut)) for k in range(NCHIP)], [])
        flat = lambda v: v.reshape(-1, v.shape[-1])
        pair_sums = []
        for n, (t, fs) in enumerate(zip(cut, from_sib)):
            mine = lax.dynamic_index_in_dim(t.reshape((NCHIP, 2) + t.shape[1:]), mc, axis=1, keepdims=False)
            pair_sums.append(_add2("pair_sum_%s%d" % (tag, n), flat(mine), flat(fs), out_dtype=BF16).reshape(fs.shape))
        copies = []
        for n in range(len(cut)):
            for j, f in enumerate(_CHIP_FLIPS):
                src = lambda x, y, c, f=f: _chip_index(1 - x if f[0] else x, 1 - y if f[1] else y, c)
                copies.append((f, n, src, n, lambda x, y, c, j=j: j))
        shapes = [jax.ShapeDtypeStruct((NCHIP - 1,) + t.shape[1:], BF16) for t in pair_sums]
        return pair_sums, (pair_sums, shapes, copies)

    def reduce_back(pair_sums, arrived, tag):
        my_halves = []
        for n, (ps, got) in enumerate(zip(pair_sums, arrived)):
            own = lax.dynamic_index_in_dim(ps, chip, axis=0, keepdims=True)
            my_halves.append(_sum_leading("chip_sum_%s%d" % (tag, n), jnp.concatenate([own, got], axis=0)))
        sib_halves = _exchange("pair_share_" + tag, my_halves, [jax.ShapeDtypeStruct(t.shape, F32) for t in my_halves],
                               [((0, 0, 1), n, to_sib, n, to_sib) for n in range(len(my_halves))], [])
        whole_blocks = []
        for mh, shf in zip(my_halves, sib_halves):
            pair = jnp.stack([mh, shf])
            whole_blocks.append(jnp.concatenate([lax.dynamic_index_in_dim(pair, mc, axis=0, keepdims=False),
                                                 lax.dynamic_index_in_dim(pair, 1 - mc, axis=0, keepdims=False)], axis=0))
        return whole_blocks

    stash = {}

    def early_reduce(g_up, g_down, g_cw):
        sh_cw = _as_rows(g_cw.reshape(conv_w.shape[1], NCHIP, -1).transpose(1, 0, 2))
        rows = w_down.shape[1] + sh_cw.shape[1]
        pad = _rows_of(rows, 2 * LANES) * 2 * LANES - rows
        sh_a = jnp.concatenate([g_down.reshape(NCHIP, -1, D), sh_cw, jnp.zeros((NCHIP, pad, D), F32)], axis=1)
        stash["ps"], ride = reduce_front([sh_a, g_up], "e")
        return ride

    gn = jnp.concatenate([gn_a, gn_b], axis=1)
    loss_part, grad_x, d_ada, small, big = _local_step(
        x.reshape(T, D), loss_target.reshape(T, D), positions, ada, w_qkv, w_f, w_out_full, None, conv_w8,
        None, b_fgate, gn, ln1_g, ln1_b, conv_b, ln2_g, ln2_b, late=([halve(w_up_sh), halve(w_down_sh)], late_weights), early=early_reduce)

    def row_pad(v, rows):
        flat = v.reshape(-1)
        return jnp.pad(flat, (0, rows * D - flat.shape[0]))

    n_cb = _rows_of(2 * DFF)
    small_flat = jnp.concatenate([
        row_pad(small["b_fgate"], 1), row_pad(small["gn"], 1), row_pad(small["ln1_g"], 1),
        row_pad(small["ln1_b"], 1), row_pad(small["ln2_g"], 1), row_pad(small["ln2_b"], 1),
        row_pad(jnp.full((1,), loss_part, F32), 1), row_pad(small["conv_b"], n_cb)])
    n_small = _rows_of(small_flat.shape[0], SUBLANES * D) * SUBLANES
    small_rows = jnp.pad(small_flat, (0, n_small * D - small_flat.shape[0])).reshape(n_small, D)
    ada_rows = jnp.pad(d_ada.reshape(nbl, 6, D), ((0, 0), (0, SUBLANES - 6), (0, 0))).reshape(nbl * SUBLANES, D)
    gathered = _all_gather8("gather_small", jnp.concatenate([small_rows, ada_rows], axis=0))
    red = _sum_leading("sum_small", gathered, tm=SUBLANES)
    g_b_fgate = red[0:1, :nha]
    g_gn = red[1:2, :2 * WA]
    g_ln1_g, g_ln1_b, g_ln2_g, g_ln2_b = red[2:3], red[3:4], red[4:5], red[5:6]
    loss = red[6, 0]
    g_conv_b = red[7:7 + n_cb].reshape(1, -1)[:, :2 * DFF]
    g_b_ada = _add2("sum_b_ada", red[n_small:n_small + SUBLANES], red[n_small + SUBLANES:n_small + 2 * SUBLANES],
                    tm=SUBLANES)[:6].reshape(1, 6 * D)
    dada_all = gathered[:, n_small:].reshape(NDEV, nbl, SUBLANES, D)[:, :, :6].reshape(NDEV * nbl, 6 * D)
    g_w_ada = _ada_bwd(c_all, lax.dynamic_slice_in_dim(dada_all, chip * n_ada, n_ada, axis=1))

    g_cat = big["w_cat"]
    g_w_in_full = jnp.concatenate([g_cat[:, :3 * WA], g_cat[:, 6 * WA:6 * WA + nha], g_cat[:, 3 * WA:6 * WA]], axis=1)
    sh_in = jnp.pad(g_w_in_full.reshape(D, NCHIP, n_in).transpose(1, 0, 2),
                    ((0, 0), (0, 0), (0, _rows_of(n_in, LANES) * LANES - n_in)))
    ps_l, scatter_l = reduce_front([big["w_out"].reshape(NCHIP, -1, D), sh_in], "l")
    arrived_l = _exchange("scatter_l", ps_l, scatter_l[1], scatter_l[2], [])
    shards_e = reduce_back(stash["ps"], big["early"], "e")
    shards_l = reduce_back(ps_l, arrived_l, "l")
    r0 = w_down.shape[1]
    g_w_down = shards_e[0][:r0]
    g_conv_w = shards_e[0][r0:r0 + SUBLANES].reshape(-1)[:int(np.prod(conv_w.shape[1:]))].reshape(conv_w.shape[1:])
    g_w_up = shards_e[1]
    g_w_out = shards_l[0]
    g_w_in = shards_l[1][:, :n_in]

    grads = dict(w_ada=g_w_ada, b_ada=g_b_ada, w_in=g_w_in, b_fgate=g_b_fgate, gn_a=g_gn[:, :WA], gn_b=g_gn[:, WA:],
                 w_out=g_w_out, ln1_g=g_ln1_g, ln1_b=g_ln1_b, w_up=g_w_up, conv_w=g_conv_w, conv_b=g_conv_b,
                 w_down=g_w_down, ln2_g=g_ln2_g, ln2_b=g_ln2_b)
    weights = dict(w_ada=w_ada, b_ada=b_ada, w_in=w_in, b_fgate=b_fgate, gn_a=gn_a, gn_b=gn_b, w_out=w_out,
                   ln1_g=ln1_g, ln1_b=ln1_b, w_up=w_up, conv_w=conv_w, conv_b=conv_b, w_down=w_down,
                   ln2_g=ln2_g, ln2_b=ln2_b)
    ms = dict(w_ada=m_w_ada, b_ada=m_b_ada, w_in=m_w_in, b_fgate=m_b_fgate, gn_a=m_gn_a, gn_b=m_gn_b,
              w_out=m_w_out, ln1_g=m_ln1_g, ln1_b=m_ln1_b, w_up=m_w_up, conv_w=m_conv_w, conv_b=m_conv_b,
              w_down=m_w_down, ln2_g=m_ln2_g, ln2_b=m_ln2_b)
    vs = dict(w_ada=v_w_ada, b_ada=v_b_ada, w_in=v_w_in, b_fgate=v_b_fgate, gn_a=v_gn_a, gn_b=v_gn_b,
              w_out=v_w_out, ln1_g=v_ln1_g, ln1_b=v_ln1_b, w_up=v_w_up, conv_w=v_conv_w, conv_b=v_conv_b,
              w_down=v_w_down, ln2_g=v_ln2_g, ln2_b=v_ln2_b)
    names = list(weights)
    big_names = ("w_ada", "w_in", "w_out", "w_up", "w_down")
    delta, new_m, new_v = {}, {}, {}
    for n in big_names:
        shp = weights[n].shape
        d, m2, v2 = _adamw("adamw_" + n, weights[n][0], grads[n].reshape(shp[1:]), ms[n][0], vs[n][0])
        delta[n], new_m[n], new_v[n] = d.reshape(shp), m2.reshape(shp), v2.reshape(shp)
    small_names = [n for n in names if n not in big_names]

    def pack_small(src):
        flats = []
        for n in small_names:
            flat = src[n].reshape(-1)
            flats.append(jnp.pad(flat, (0, _rows_of(flat.shape[0]) * D - flat.shape[0])))
        allf = jnp.concatenate(flats)
        rows = _rows_of(allf.shape[0], SUBLANES * D) * SUBLANES
        return jnp.pad(allf, (0, rows * D - allf.shape[0])).reshape(rows, D)

    sd, sm, sv_ = _adamw("adamw_small", pack_small(weights), pack_small(grads), pack_small(ms), pack_small(vs))
    off = 0
    for n in small_names:
        shp = weights[n].shape
        cnt = int(np.prod(shp))
        r = _rows_of(cnt)
        for dst, src in ((delta, sd), (new_m, sm), (new_v, sv_)):
            dst[n] = src[off:off + r].reshape(-1)[:cnt].reshape(shp)
        off += r

    out_g = {n: grads[n].reshape(weights[n].shape) for n in names}
    return (loss, grad_x.reshape(x.shape), *[out_g[n] for n in names], *[delta[n] for n in names],
            *[new_m[n] for n in names], *[new_v[n] for n in names])
```

```python
import numpy as np
import jax
import jax.numpy as jnp
from jax import lax
from jax.experimental import pallas as pl
from jax.experimental.pallas import tpu as pltpu

F32 = jnp.float32
BF16 = jnp.bfloat16

D = 1024
S = 4096
HD = 64
WA = 512
DFF = 2816
NCHIP = 4
NDEV = 8
PATTERNS = ((128, 1), (512, 4), (2048, 16))
ROPE_THETA = 500000.0
ROPE_DIMS = HD // 4
ALPHA = (2.0 * 1) ** 0.25
LN_EPS = 1e-5
RMS_EPS = 1e-6
ADAM_LR = 0.001
ADAM_B1 = 0.9
ADAM_B2 = 0.999
ADAM_EPS = 1e-08
ADAM_WD = 0.01
ADAM_STEP = 10

LANES = 128
SUBLANES = 8
TQ = 512
FPAD = LANES
NEG = -1e30
VMEM_LIMIT = 56 * 1024 * 1024
MESH = pl.DeviceIdType.MESH


def _cparams(sem):
    return pltpu.CompilerParams(dimension_semantics=sem, vmem_limit_bytes=VMEM_LIMIT)


def _pick(n, cands):
    for c in cands:
        if n % c == 0:
            return c
    return n


def _rsum8(v):
    tm, w = v.shape
    return jnp.sum(v.reshape(tm // SUBLANES, SUBLANES, w), axis=0)


def _sigmoid(x):
    return 1.0 / (1.0 + jnp.exp(-x))


def _dot(a, b):
    return jnp.dot(a, b, preferred_element_type=F32)


def _dot_nt(a, b):
    return lax.dot_general(a, b, (((1,), (1,)), ((), ())), preferred_element_type=F32)


def _dot_tn(a, b):
    return lax.dot_general(a, b, (((0,), (0,)), ((), ())), preferred_element_type=F32)


def _rowwise(name, fn, T, tm, *, tiles=(), halos=(), seqvecs=(), consts=(), outs=(), accs=(), seqaccs=(),
             seq_len=None):
    seq_len = S if seq_len is None else seq_len
    nb = T // tm
    spb = max(seq_len // tm, 1)
    nseq = max(T // seq_len, 1)
    n8 = T // SUBLANES
    r8 = tm // SUBLANES
    in_specs, args = [], []
    for a in tiles:
        in_specs.append(pl.BlockSpec((tm, a.shape[1]), lambda i: (i, 0)))
        args.append(a)
    for a, direction in halos:
        if direction < 0:
            idx = lambda i: (jnp.maximum(i * r8 - 1, 0), 0)
        else:
            idx = lambda i: (jnp.minimum((i + 1) * r8, n8 - 1), 0)
        in_specs.append(pl.BlockSpec((SUBLANES, a.shape[1]), idx))
        args.append(a)
    for a in seqvecs:
        in_specs.append(pl.BlockSpec((1, 1, a.shape[2]), lambda i: (i // spb, 0, 0)))
        args.append(a)
    for a in consts:
        in_specs.append(pl.BlockSpec(a.shape, lambda i, nd=a.ndim: (0,) * nd))
        args.append(a)
    out_shape, out_specs = [], []
    for w, dt in outs:
        out_shape.append(jax.ShapeDtypeStruct((T, w), dt))
        out_specs.append(pl.BlockSpec((tm, w), lambda i: (i, 0)))
    for w in accs:
        out_shape.append(jax.ShapeDtypeStruct((SUBLANES, w), F32))
        out_specs.append(pl.BlockSpec((SUBLANES, w), lambda i: (0, 0)))
    for w in seqaccs:
        out_shape.append(jax.ShapeDtypeStruct((nseq, SUBLANES, w), F32))
        out_specs.append(pl.BlockSpec((1, SUBLANES, w), lambda i: (i // spb, 0, 0)))
    n_t, n_h, n_s, n_c = len(tiles), len(halos), len(seqvecs), len(consts)
    n_o, n_a, n_sa = len(outs), len(accs), len(seqaccs)

    def body(*refs):
        i = pl.program_id(0)
        ins = refs[:n_t + n_h + n_s + n_c]
        orefs = refs[n_t + n_h + n_s + n_c:]
        vals = [r[...] for r in ins[:n_t + n_h]]
        vals += [r[0] for r in ins[n_t + n_h:n_t + n_h + n_s]]
        vals += list(ins[n_t + n_h + n_s:])
        res = fn(i, *vals)
        if not isinstance(res, (tuple, list)):
            res = (res,)
        for k in range(n_o):
            orefs[k][...] = res[k].astype(orefs[k].dtype)
        for k in range(n_a):
            r = orefs[n_o + k]

            @pl.when(i == 0)
            def _():
                r[...] = jnp.zeros_like(r)

            r[...] += res[n_o + k]

            @pl.when(i == nb - 1)
            def _():
                r[...] = jnp.broadcast_to(jnp.sum(r[...], axis=0, keepdims=True), r.shape)
        for k in range(n_sa):
            r = orefs[n_o + n_a + k]

            @pl.when(i % spb == 0)
            def _():
                r[...] = jnp.zeros_like(r)

            r[0] += res[n_o + n_a + k]

            @pl.when(i % spb == spb - 1)
            def _():
                r[0] = jnp.broadcast_to(jnp.sum(r[0], axis=0, keepdims=True), r.shape[1:])

    sem = ("arbitrary",) if (n_a or n_sa) else ("parallel",)
    res = pl.pallas_call(
        body, name=name, grid=(nb,), in_specs=in_specs, out_specs=out_specs, out_shape=out_shape,
        compiler_params=_cparams(sem),
    )(*args)
    return res


def _ln_fwd(r, g, b):
    mu = jnp.mean(r, axis=-1, keepdims=True)
    xc = r - mu
    var = jnp.mean(xc * xc, axis=-1, keepdims=True)
    rstd = lax.rsqrt(var + LN_EPS)
    n = xc * rstd
    return n * g + b, n, rstd


def _ln_bwd(dy, n, rstd, g):
    dn = dy * g
    return rstd * (dn - jnp.mean(dn, axis=-1, keepdims=True) - n * jnp.mean(dn * n, axis=-1, keepdims=True))


def _head_mean(t, g_ref):
    gw = g_ref.shape[0]
    hi = t.astype(BF16)
    lo = (t - hi.astype(F32)).astype(BF16)
    g = g_ref[...]
    parts = []
    for c in range(t.shape[1] // gw):
        sl = slice(c * gw, (c + 1) * gw)
        parts.append(_dot(hi[:, sl], g) + _dot(lo[:, sl], g))
    out = parts[0] if len(parts) == 1 else jnp.concatenate(parts, axis=1)
    return out * (1.0 / HD)


def _rope(z, c, s1, s2):
    w = z.shape[1]
    half = ROPE_DIMS // 2
    return z * c + pltpu.roll(z, half, 1) * s1 + pltpu.roll(z, w - half, 1) * s2


def _tile_lanes(t, w):
    reps = w // t.shape[1]
    return t if reps == 1 else jnp.concatenate([t] * reps, axis=1)


def _conv_taps(ext, prev, first):
    prev = jnp.where(first, jnp.zeros_like(prev), prev)
    r8 = lax.broadcasted_iota(jnp.int32, (SUBLANES, 1), 0)
    top = ext[0:SUBLANES]
    s1_top = jnp.where(r8 < 1, pltpu.roll(prev, 1, 0), pltpu.roll(top, 1, 0))
    s2_top = jnp.where(r8 < 2, pltpu.roll(prev, 2, 0), pltpu.roll(top, 2, 0))
    s1 = jnp.concatenate([s1_top, pltpu.roll(ext, 1, 0)[SUBLANES:]], axis=0)
    s2 = jnp.concatenate([s2_top, pltpu.roll(ext, 2, 0)[SUBLANES:]], axis=0)
    return s1, s2


def _conv_taps_up(ext, nxt, last):
    tm = ext.shape[0]
    nxt = jnp.where(last, jnp.zeros_like(nxt), nxt)
    r8 = lax.broadcasted_iota(jnp.int32, (SUBLANES, 1), 0)
    bot = ext[tm - SUBLANES:tm]
    u1_bot = jnp.where(r8 >= 7, pltpu.roll(nxt, 7, 0), pltpu.roll(bot, 7, 0))
    u2_bot = jnp.where(r8 >= 6, pltpu.roll(nxt, 6, 0), pltpu.roll(bot, 6, 0))
    u1 = jnp.concatenate([pltpu.roll(ext, tm - 1, 0)[:tm - SUBLANES], u1_bot], axis=0)
    u2 = jnp.concatenate([pltpu.roll(ext, tm - 2, 0)[:tm - SUBLANES], u2_bot], axis=0)
    return u1, u2


def _nt_rows(av, w_ref):
    n = w_ref.shape[0]
    ch = _pick(n, (512, 256, 128))
    ab = av.astype(BF16)
    parts = [_dot_nt(ab, w_ref[c * ch:(c + 1) * ch, :]) for c in range(n // ch)]
    return parts[0] if len(parts) == 1 else jnp.concatenate(parts, axis=1)


def _mm_tn(name, a, b, *, mod=None, tt=512, t2=None, by_chip=False):
    T, k1 = a.shape
    k2 = b.shape[1]
    t1 = k1 if k1 <= 1536 else _pick(k1, (1408, 1024, 512, 256, 128))
    if t2 is None:
        t2 = k2 if k2 <= 1536 else _pick(k2, (1408, 1024, 640, 512, 256, 128))
    wc = k2 // NCHIP
    if by_chip:
        t2 = 2 * wc
    tt = min(tt, S)
    spb = S // tt

    def body(*refs):
        if mod is not None:
            a_ref, sc_ref, sh_ref, b_ref, o_ref = refs
        else:
            a_ref, b_ref, o_ref = refs
        t = pl.program_id(2)

        @pl.when(t == 0)
        def _():
            o_ref[...] = jnp.zeros_like(o_ref)

        av = a_ref[...]
        if mod is not None:
            av = av * (1.0 + sc_ref[0]) + sh_ref[0]
        res = _dot_tn(av.astype(BF16), b_ref[...].astype(BF16))
        if by_chip:
            o_ref[0] += res[:, :wc]
            o_ref[1] += res[:, wc:]
        else:
            o_ref[...] += res

    in_specs = [pl.BlockSpec((tt, t1), lambda p, q, t: (t, p))]
    args = [a]
    if mod is not None:
        for v in mod:
            in_specs.append(pl.BlockSpec((1, 1, t1), lambda p, q, t: (t // spb, 0, p)))
            args.append(v)
    in_specs.append(pl.BlockSpec((tt, t2), lambda p, q, t: (t, q)))
    args.append(b)
    if by_chip:
        out_specs = pl.BlockSpec((2, t1, wc), lambda p, q, t: (q, p, 0))
        out_shape = jax.ShapeDtypeStruct((NCHIP, k1, wc), F32)
    else:
        out_specs = pl.BlockSpec((t1, t2), lambda p, q, t: (p, q))
        out_shape = jax.ShapeDtypeStruct((k1, k2), F32)
    return pl.pallas_call(
        body, name=name, grid=(k1 // t1, k2 // t2, T // tt), in_specs=in_specs, out_specs=out_specs,
        out_shape=out_shape, compiler_params=_cparams(("parallel", "parallel", "arbitrary")),
    )(*args)


def _mod_mm(name, x, sc, sh, ws, out_dtypes, rope=None, rope_secs=(), tm=256):
    T = x.shape[0]
    nw = len(ws)

    def fn(i, xv, *rest):
        if rope is not None:
            cv, s1v, s2v = rest[:3]
            rest = rest[3:]
        scv, shv = rest[:2]
        w_refs = rest[2:]
        h = (xv * (1.0 + scv) + shv).astype(BF16)
        res = []
        for k, w_ref in enumerate(w_refs):
            n = w_ref.shape[1]
            ch = WA if (k == 0 and rope is not None) else _pick(n, (512, 256, 128))
            parts = []
            for c in range(n // ch):
                z = _dot(h, w_ref[:, c * ch:(c + 1) * ch])
                if k == 0 and c in rope_secs:
                    z = _rope(z, _tile_lanes(cv, ch), _tile_lanes(s1v, ch), _tile_lanes(s2v, ch))
                parts.append(z.astype(out_dtypes[k]))
            res.append(parts[0] if len(parts) == 1 else jnp.concatenate(parts, axis=1))
        return tuple(res)

    tiles = (x,) + (tuple(rope) if rope is not None else ())
    outs = tuple((w.shape[1], dt) for w, dt in zip(ws, out_dtypes))
    return _rowwise(name, fn, T, tm, tiles=tiles, seqvecs=(sc, sh), consts=tuple(ws), outs=outs)


def _tri(tb, lower):
    r = lax.broadcasted_iota(jnp.int32, (tb, tb), 0)
    c = lax.broadcasted_iota(jnp.int32, (tb, tb), 1)
    return jnp.where((r >= c) if lower else (r <= c), 1.0, 0.0).astype(BF16)


def _split3(x):
    hi = x.astype(BF16)
    r = x - hi.astype(F32)
    mid = r.astype(BF16)
    return hi, mid, (r - mid.astype(F32)).astype(BF16)


def _cumsum_seq(name, ins, consts, fn_in, fn_out, outs, reverse, n_acc=0, tb=256):
    T = ins[0].shape[0]
    tb = min(tb, S)
    nbs = S // tb
    nseq = T // S
    n_i, n_c, n_o = len(ins), len(consts), len(outs)

    def blk(b, j):
        return (b * nbs + (nbs - 1 - j if reverse else j), 0)

    def body(*refs):
        i_refs, c_refs = refs[:n_i], refs[n_i:n_i + n_c]
        o_refs = refs[n_i + n_c:n_i + n_c + n_o]
        acc_refs = refs[n_i + n_c + n_o:n_i + n_c + n_o + n_acc]
        carry = refs[-1]
        b, j = pl.program_id(0), pl.program_id(1)

        @pl.when(j == 0)
        def _():
            carry[...] = jnp.zeros_like(carry)

        iv = [r[...] for r in i_refs]
        xin = fn_in(*iv, *c_refs)
        tri = _tri(tb, not reverse)
        cum = sum(_dot(tri, piece) for piece in _split3(xin)) + carry[0:1, :]
        carry[...] = carry[...] + jnp.sum(xin, axis=0, keepdims=True)
        res = fn_out(cum, *iv, *c_refs)
        for o, r in zip(o_refs, res):
            o[...] = r.astype(o.dtype)
        for a in acc_refs:
            @pl.when((b == 0) & (j == 0))
            def _():
                a[...] = jnp.zeros_like(a)

            a[...] += _rsum8(res[0])

            @pl.when((b == nseq - 1) & (j == nbs - 1))
            def _():
                a[...] = jnp.broadcast_to(jnp.sum(a[...], axis=0, keepdims=True), a.shape)

    in_specs = [pl.BlockSpec((tb, a.shape[1]), blk) for a in ins]
    in_specs += [pl.BlockSpec(c.shape, lambda b, j, nd=c.ndim: (0,) * nd) for c in consts]
    out_shape = [jax.ShapeDtypeStruct((T, w), dt) for w, dt in outs]
    out_shape += [jax.ShapeDtypeStruct((SUBLANES, outs[0][0]), F32)] * n_acc
    out_specs = [pl.BlockSpec((tb, w), blk) for w, _ in outs]
    out_specs += [pl.BlockSpec((SUBLANES, outs[0][0]), lambda b, j: (0, 0))] * n_acc
    return pl.pallas_call(
        body, name=name, grid=(nseq, nbs), in_specs=in_specs, out_specs=out_specs, out_shape=out_shape,
        scratch_shapes=[pltpu.VMEM((SUBLANES, FPAD), F32)],
        compiler_params=_cparams(("arbitrary", "arbitrary")),
    )(*ins, *consts)


def _log_sigmoid(x):
    return jnp.minimum(x, 0.0) - jnp.log(1.0 + jnp.exp(-jnp.abs(x)))


def _dil_bias(tq):
    max_win = max(w for w, _ in PATTERNS)
    nd = (max_win + tq - 1) // tq + 1
    qi = np.arange(tq)[:, None]
    kj = np.arange(tq)[None, :]
    tabs = []
    for dlt in range(nd):
        dist = dlt * tq + qi - kj
        mult = np.zeros((tq, tq), np.float64)
        for win, dil in PATTERNS:
            mult += (dist >= 0) & (dist % dil == 0) & (dist // dil <= win // dil)
        tabs.append(np.where(mult > 0, np.log(np.maximum(mult, 1.0)), NEG))
    return np.stack(tabs).astype(np.float32)


def _fold_tables(nha):
    hp_n = nha // 2
    pq = np.zeros((3, FPAD, hp_n * 2 * LANES), np.float32)
    pk = np.zeros((3, FPAD, hp_n * LANES), np.float32)
    oq = np.zeros((1, hp_n * 2 * LANES), np.float32)
    ok = np.zeros((1, hp_n * LANES), np.float32)
    sq = np.zeros((hp_n * LANES, FPAD), np.float32)
    sk = np.zeros((hp_n * LANES, FPAD), np.float32)
    for h in range(nha):
        hp, odd = divmod(h, 2)
        qb = hp * 2 * LANES + odd * (LANES + 8)
        kb = hp * LANES + odd * 8
        for i in range(3):
            pq[i, h, qb + i] = 1
            oq[0, qb + 3 + i] = 1
            ok[0, kb + i] = 1
            pk[i, h, kb + 3 + i] = 1
        sq[kb, h] = 1
        sk[kb + 3, h] = 1
    return pq, pk, oq, ok, sq, sk


def _stack_heads(x2, h0, extra=None):
    z = jnp.zeros_like(x2)
    a, b = jnp.where(h0, x2, z), jnp.where(h0, z, x2)
    if extra is not None:
        a = jnp.concatenate([a, extra[:, :LANES]], axis=1)
        b = jnp.concatenate([b, extra[:, LANES:]], axis=1)
    return jnp.concatenate([a, b], axis=0)


def _attn_fwd(name, qkv, secs, fox, eq=None, ek=None, bias=None, ride=()):
    T = qkv.shape[0]
    nq = S // TQ
    nbl = T // S
    hp_n = WA // LANES
    sq, sk, sv = (s * hp_n for s in secs)
    scale = HD ** -0.5
    nd = None if fox else bias.shape[0]

    n_r = len(ride)
    n_in = (5 if fox else 4) + n_r

    def body(*refs):
        if fox:
            q_ref, k_ref, v_ref, eq_ref, ek_ref = refs[:5]
        else:
            q_ref, k_ref, v_ref, b_ref = refs[:4]
        o_ref, lse_ref = refs[n_in:n_in + 2]
        i = pl.program_id(2)
        if n_r:
            ride_start, ride_finish = _gather_halves_steps(
                refs[n_in - n_r:n_in], refs[n_in + 2:n_in + 2 + n_r], *refs[n_in + 2 + n_r:])
            at = lambda b, hp, q: (pl.program_id(0) == b) & (pl.program_id(1) == hp) & (i == q)
            pl.when(at(0, 0, 0))(ride_start)
        lane = lax.broadcasted_iota(jnp.int32, (1, LANES), 1)
        h0 = lane < HD
        q2 = (q_ref[...].astype(F32) * scale).astype(BF16)
        qs = _stack_heads(q2, h0, eq_ref[...] if fox else None)

        def scores(t, diag):
            off = pl.multiple_of((i - t) * TQ, TQ)
            kk = k_ref[pl.ds(off, TQ), :]
            if fox:
                kk = jnp.concatenate([kk, ek_ref[pl.ds(off, TQ), :]], axis=1)
            s = jnp.concatenate([_dot_nt(qs[:TQ], kk), _dot_nt(qs[TQ:], kk)], axis=0)
            if not fox:
                s = (s.reshape(2, TQ, TQ) + b_ref[t]).reshape(2 * TQ, TQ)
            elif diag:
                rows = lax.broadcasted_iota(jnp.int32, (2, TQ, TQ), 1).reshape(2 * TQ, TQ)
                cols = lax.broadcasted_iota(jnp.int32, (2 * TQ, TQ), 1)
                s = jnp.where(cols <= rows, s, NEG)
            return s

        def update(t, s, m, l, acc):
            off = pl.multiple_of((i - t) * TQ, TQ)
            v2 = v_ref[pl.ds(off, TQ), :]
            m_new = jnp.maximum(m, jnp.max(s, axis=1, keepdims=True))
            p = jnp.exp(s - m_new)
            a = jnp.exp(m - m_new)
            l = a * l + jnp.sum(p, axis=1, keepdims=True)
            pb = p.astype(BF16)
            acc = a * acc + jnp.concatenate([_dot(pb[:TQ], v2), _dot(pb[TQ:], v2)], axis=0)
            return m_new, l, acc

        init = (jnp.full((2 * TQ, 1), NEG, F32), jnp.zeros((2 * TQ, 1), F32), jnp.zeros((2 * TQ, LANES), F32))
        n = i + 1 if fox else jnp.minimum(i + 1, nd)
        m, l, acc = update(0, scores(0, True), *init)
        m, l, acc = lax.fori_loop(1, n, lambda t, c: update(t, scores(t, False), *c), (m, l, acc))
        on = acc / l
        o_ref[...] = jnp.where(h0, on[:TQ], on[TQ:])
        lse = jnp.broadcast_to(m + jnp.log(l), (2 * TQ, LANES))
        lse_ref[...] = jnp.concatenate([lse[:TQ], lse[TQ:]], axis=1)
        if n_r:
            pl.when(at(nbl - 1, hp_n - 1, nq - 1))(ride_finish)

    in_specs = [
        pl.BlockSpec((TQ, LANES), lambda b, hp, i: (b * nq + i, sq + hp)),
        pl.BlockSpec((S, LANES), lambda b, hp, i: (b, sk + hp)),
        pl.BlockSpec((S, LANES), lambda b, hp, i: (b, sv + hp)),
    ]
    args = [qkv, qkv, qkv]
    if fox:
        in_specs += [pl.BlockSpec((TQ, 2 * LANES), lambda b, hp, i: (b * nq + i, hp)),
                     pl.BlockSpec((S, LANES), lambda b, hp, i: (b, hp))]
        args += [eq, ek]
    else:
        in_specs.append(pl.BlockSpec(bias.shape, lambda b, hp, i: (0, 0, 0)))
        args.append(bias)
    any_spec = pl.BlockSpec(memory_space=pl.ANY)
    out_specs = [pl.BlockSpec((TQ, LANES), lambda b, hp, i: (b * nq + i, hp)),
                 pl.BlockSpec((TQ, 2 * LANES), lambda b, hp, i: (b * nq + i, hp))] + [any_spec] * n_r
    out_shape = [jax.ShapeDtypeStruct((T, WA), F32), jax.ShapeDtypeStruct((T, 2 * WA), F32)]
    out_shape += [jax.ShapeDtypeStruct((NCHIP - 1,) + v.shape, v.dtype) for v in ride]
    sems = [pltpu.SemaphoreType.DMA((6 * n_r,)), pltpu.SemaphoreType.DMA((6 * n_r,))] if n_r else []
    sem = ("arbitrary",) * 3 if n_r else ("parallel", "parallel", "arbitrary")
    return pl.pallas_call(
        body, name=name, grid=(nbl, hp_n, nq), in_specs=in_specs + [any_spec] * n_r, out_specs=out_specs,
        out_shape=out_shape, scratch_shapes=sems, compiler_params=_cparams(sem),
    )(*args, *ride)


def _attn_bwd(name, qkv, secs, o, do, do_sec, lse, fox, eq=None, ek=None, bias=None, ride=None):
    T = qkv.shape[0]
    nq = S // TQ
    nbl = T // S
    hp_n = WA // LANES
    sq, sk, sv = (s * hp_n for s in secs)
    dsec = do_sec * hp_n
    scale = HD ** -0.5
    nd = None if fox else bias.shape[0]
    kc = 2 * LANES if fox else LANES

    r_in, r_out, r_copies = ride if ride else ((), (), ())
    n_bi, n_bo = (8, 5) if fox else (7, 3)
    n_i = n_bi + len(r_in)

    def body(*refs):
        if fox:
            q_ref, k_ref, v_ref, o_ref, do_ref, lse_ref, eq_ref, ek_ref = refs[:n_bi]
            dq_ref, dk_ref, dv_ref, dqe_ref, dek_ref = refs[n_i:n_i + n_bo]
        else:
            q_ref, k_ref, v_ref, o_ref, do_ref, lse_ref, b_ref = refs[:n_bi]
            dq_ref, dk_ref, dv_ref = refs[n_i:n_i + n_bo]
        dl_ref = refs[n_i + n_bo + len(r_out)]
        j = pl.program_id(2)
        if ride:
            ride_start, ride_finish = _remote_steps(
                refs[n_bi:n_i], refs[n_i + n_bo:n_i + n_bo + len(r_out)], *refs[n_i + n_bo + len(r_out) + 1:], r_copies)
            at = lambda b, hp, q: (pl.program_id(0) == b) & (pl.program_id(1) == hp) & (j == q)
            pl.when(at(0, 0, 0))(ride_start)
        lane = lax.broadcasted_iota(jnp.int32, (1, LANES), 1)
        h0 = lane < HD

        @pl.when(j == 0)
        def _():
            dq_ref[...] = jnp.zeros_like(dq_ref)
            if fox:
                dqe_ref[...] = jnp.zeros_like(dqe_ref)

            def dl_step(r, c):
                off = pl.multiple_of(r * TQ, TQ)
                d2 = do_ref[pl.ds(off, TQ), :] * o_ref[pl.ds(off, TQ), :]
                z2 = jnp.zeros_like(d2)
                dl0 = jnp.sum(jnp.where(h0, d2, z2), axis=1, keepdims=True)
                dl1 = jnp.sum(jnp.where(h0, z2, d2), axis=1, keepdims=True)
                dl_ref[pl.ds(off, TQ), :] = jnp.concatenate(
                    [jnp.broadcast_to(dl0, (TQ, LANES)), jnp.broadcast_to(dl1, (TQ, LANES))], axis=1)
                return c

            lax.fori_loop(0, nq, dl_step, 0)

        kk = k_ref[...]
        if fox:
            kk = jnp.concatenate([kk, ek_ref[...]], axis=1)
        v2 = v_ref[...]

        def wide(x2):
            st = jnp.concatenate([x2[:, :LANES], x2[:, LANES:]], axis=0)
            return st if TQ == LANES else jnp.concatenate([st] * (TQ // LANES), axis=1)

        def step(t, carry, diag):
            dkk, dv2 = carry
            off = pl.multiple_of((j + t) * TQ, TQ)
            q2 = (q_ref[pl.ds(off, TQ), :].astype(F32) * scale).astype(BF16)
            qs = _stack_heads(q2, h0, eq_ref[pl.ds(off, TQ), :] if fox else None)
            dos = _stack_heads(do_ref[pl.ds(off, TQ), :].astype(BF16), h0)
            s = jnp.concatenate([_dot_nt(qs[:TQ], kk), _dot_nt(qs[TQ:], kk)], axis=0)
            if not fox:
                s = (s.reshape(2, TQ, TQ) + b_ref[t]).reshape(2 * TQ, TQ)
            elif diag:
                rows = lax.broadcasted_iota(jnp.int32, (2, TQ, TQ), 1).reshape(2 * TQ, TQ)
                cols = lax.broadcasted_iota(jnp.int32, (2 * TQ, TQ), 1)
                s = jnp.where(cols <= rows, s, NEG)
            p = jnp.exp(s - wide(lse_ref[pl.ds(off, TQ), :]))
            dp = jnp.concatenate([_dot_nt(dos[:TQ], v2), _dot_nt(dos[TQ:], v2)], axis=0)
            dsb = (p * (dp - wide(dl_ref[pl.ds(off, TQ), :]))).astype(BF16)
            dv2 = dv2 + _dot_tn(p.astype(BF16), dos)
            dkk = dkk + _dot_tn(dsb, qs)
            dqq = jnp.concatenate([_dot(dsb[:TQ], kk), _dot(dsb[TQ:], kk)], axis=0)
            dq_ref[pl.ds(off, TQ), :] += jnp.where(h0, dqq[:TQ, :LANES], dqq[TQ:, :LANES])
            if fox:
                dqe_ref[pl.ds(off, TQ), :] += jnp.where(lane < SUBLANES, dqq[:TQ, LANES:], dqq[TQ:, LANES:])
            return dkk, dv2

        zero = (jnp.zeros((TQ, kc), F32), jnp.zeros((TQ, LANES), F32))
        if fox:
            dkk, dv2 = lax.fori_loop(1, nq - j, lambda t, c: step(t, c, False), step(0, zero, True))
        else:
            dkk, dv2 = lax.fori_loop(0, jnp.minimum(nq - j, nd), lambda t, c: step(t, c, False), zero)
        dk_ref[...] = dkk[:, :LANES]
        dv_ref[...] = dv2
        if fox:
            dek_ref[...] = dkk[:, LANES:]

        @pl.when(j == nq - 1)
        def _():
            dq_ref[...] = dq_ref[...] * scale

        if ride:
            pl.when(at(nbl - 1, hp_n - 1, nq - 1))(ride_finish)

    seq = lambda c, w=LANES: pl.BlockSpec((S, w), lambda b, hp, j: (b, c + hp))
    blk = lambda c: pl.BlockSpec((TQ, LANES), lambda b, hp, j: (b * nq + j, c + hp))
    in_specs = [seq(sq), blk(sk), blk(sv), seq(0), seq(dsec), seq(0, 2 * LANES)]
    args = [qkv, qkv, qkv, o, do, lse]
    if fox:
        in_specs += [seq(0, 2 * LANES), blk(0)]
        args += [eq, ek]
    else:
        in_specs.append(pl.BlockSpec(bias.shape, lambda b, hp, j: (0, 0, 0)))
        args.append(bias)
    out_specs = [seq(0), blk(0), blk(0)]
    out_shape = [jax.ShapeDtypeStruct((T, WA), F32)] * 3
    if fox:
        out_specs += [seq(0), blk(0)]
        out_shape += [jax.ShapeDtypeStruct((T, WA), F32)] * 2
    any_spec = pl.BlockSpec(memory_space=pl.ANY)
    scratch = [pltpu.VMEM((S, 2 * LANES), F32)]
    if ride:
        scratch += [pltpu.SemaphoreType.DMA((len(r_copies),)), pltpu.SemaphoreType.DMA((len(r_copies),))]
    sem = ("arbitrary",) * 3 if ride else ("parallel", "parallel", "arbitrary")
    return pl.pallas_call(
        body, name=name, grid=(nbl, hp_n, nq), in_specs=in_specs + [any_spec] * len(r_in),
        out_specs=out_specs + [any_spec] * len(r_out), out_shape=out_shape + list(r_out),
        scratch_shapes=scratch, compiler_params=_cparams(sem),
    )(*args, *r_in)


def _remote_steps(in_refs, out_refs, send_sems, recv_sems, remote):
    me = (lax.axis_index("x"), lax.axis_index("y"), lax.axis_index("c"))

    def peer_of(flip):
        return tuple(1 - v if f else v for v, f in zip(me, flip))

    def at(ref, idx):
        return ref if idx is None else ref.at[idx]

    def rcopy(k, who):
        flip, a, sfn, b, dfn = remote[k]
        return pltpu.make_async_remote_copy(
            src_ref=at(in_refs[a], sfn(*who)), dst_ref=at(out_refs[b], dfn(*who)),
            send_sem=send_sems.at[k], recv_sem=recv_sems.at[k], device_id=peer_of(flip), device_id_type=MESH)

    def start():
        for k in range(len(remote)):
            rcopy(k, me).start()

    def finish():
        for k in range(len(remote)):
            rcopy(k, peer_of(remote[k][0])).wait_recv()
        for k in range(len(remote)):
            rcopy(k, me).wait_send()

    return start, finish


def _exchange(name, ins, out_shapes, remote, local):
    n_in, n_out = len(ins), len(out_shapes)
    nr, nl = len(remote), len(local)

    def body(*refs):
        in_refs = refs[:n_in]
        out_refs = refs[n_in:n_in + n_out]
        send_sems, recv_sems, loc_sems = refs[n_in + n_out:]
        me = (lax.axis_index("x"), lax.axis_index("y"), lax.axis_index("c"))
        at = lambda ref, idx: ref if idx is None else ref.at[idx]
        locs = [pltpu.make_async_copy(at(in_refs[a], sfn(*me)), at(out_refs[b], dfn(*me)), loc_sems.at[k])
                for k, (a, sfn, b, dfn) in enumerate(local)]
        for cp in locs:
            cp.start()
        start, finish = _remote_steps(in_refs, out_refs, send_sems, recv_sems, remote)
        start()
        finish()
        for cp in locs:
            cp.wait()

    any_spec = pl.BlockSpec(memory_space=pl.ANY)
    return pl.pallas_call(
        body, name=name, in_specs=[any_spec] * n_in, out_specs=[any_spec] * n_out, out_shape=list(out_shapes),
        scratch_shapes=[pltpu.SemaphoreType.DMA((max(nr, 1),)), pltpu.SemaphoreType.DMA((max(nr, 1),)),
                        pltpu.SemaphoreType.DMA((max(nl, 1),))],
    )(*ins)


_FLIPS7 = [(0, 0, 1), (0, 1, 0), (0, 1, 1), (1, 0, 0), (1, 0, 1), (1, 1, 0), (1, 1, 1)]
_CHIP_FLIPS = [(1, 0, 0), (0, 1, 0), (1, 1, 0)]


def _dev_index(x, y, c):
    return 4 * x + 2 * y + c


def _chip_index(x, y, c):
    return 2 * x + y


def _all_gather8(name, v):
    remote = [(f, 0, lambda x, y, c: None, 0, _dev_index) for f in _FLIPS7]
    local = [(0, lambda x, y, c: None, 0, _dev_index)]
    return _exchange(name, [v], [jax.ShapeDtypeStruct((NDEV,) + v.shape, v.dtype)], remote, local)[0]


def _gather_halves_steps(in_refs, out_refs, send_sems, recv_sems):
    n_v = len(in_refs)
    x, y, c = lax.axis_index("x"), lax.axis_index("y"), lax.axis_index("c")
    sibling = (x, y, 1 - c)
    chips = [(1 - x, y), (x, 1 - y), (1 - x, 1 - y)]

    def copy(k, n, src, blk, half, to):
        return pltpu.make_async_remote_copy(
            src_ref=src, dst_ref=out_refs[n].at[blk, half], send_sem=send_sems.at[k], recv_sem=recv_sems.at[k],
            device_id=to, device_id_type=MESH)

    def first():
        return [copy(6 * n + j, n, in_refs[n].at[c], j, c, (*chip, c))
                for n in range(n_v) for j, chip in enumerate(chips)]

    def start():
        for cp in first():
            cp.start()

    def finish():
        passed = []
        for n in range(n_v):
            for j, chip in enumerate(chips):
                copy(6 * n + j, n, in_refs[n].at[c], j, c, (*chip, c)).wait_recv()
                fw = copy(6 * n + 3 + j, n, out_refs[n].at[j, c], j, c, sibling)
                fw.start()
                passed.append(fw)
        for n in range(n_v):
            for j in range(len(chips)):
                copy(6 * n + 3 + j, n, out_refs[n].at[j, 1 - c], j, 1 - c, sibling).wait_recv()
        for cp in first() + passed:
            cp.wait_send()

    return start, finish


def _gather_halves(name, vs):
    n_v = len(vs)

    def body(*refs):
        start, finish = _gather_halves_steps(refs[:n_v], refs[n_v:2 * n_v], *refs[2 * n_v:])
        start()
        finish()

    any_spec = pl.BlockSpec(memory_space=pl.ANY)
    return pl.pallas_call(
        body, name=name, in_specs=[any_spec] * n_v, out_specs=[any_spec] * n_v,
        out_shape=[jax.ShapeDtypeStruct((NCHIP - 1,) + v.shape, v.dtype) for v in vs],
        scratch_shapes=[pltpu.SemaphoreType.DMA((6 * n_v,)), pltpu.SemaphoreType.DMA((6 * n_v,))],
    )(*vs)


def _by_chip(own, others, chip):
    stacked = jnp.concatenate([own[None], others], axis=0)
    blocks = []
    for k in range(NCHIP):
        d = k ^ chip
        place = jnp.where(d == 0, 0, jnp.where(d == 2, 1, jnp.where(d == 1, 2, 3)))
        blocks.append(lax.dynamic_index_in_dim(stacked, place, axis=0, keepdims=False))
    return blocks


def _sum_leading(name, v, tm=None):
    n, r, w = v.shape
    tm = _pick(r, (256, 128, 64, 32, 16, 8)) if tm is None else tm

    def body(v_ref, o_ref):
        acc = v_ref[0].astype(F32)
        for k in range(1, n):
            acc = acc + v_ref[k].astype(F32)
        o_ref[...] = acc

    return pl.pallas_call(
        body, name=name, grid=(r // tm,), in_specs=[pl.BlockSpec((n, tm, w), lambda i: (0, i, 0))],
        out_specs=pl.BlockSpec((tm, w), lambda i: (i, 0)), out_shape=jax.ShapeDtypeStruct((r, w), F32),
        compiler_params=_cparams(("parallel",)),
    )(v)


def _add2(name, a, b, tm=None, out_dtype=F32):
    r, w = a.shape
    tm = _pick(r, (256, 128, 64, 32, 16, 8)) if tm is None else tm

    def body(a_ref, b_ref, o_ref):
        o_ref[...] = (a_ref[...] + b_ref[...]).astype(out_dtype)

    spec = pl.BlockSpec((tm, w), lambda i: (i, 0))
    return pl.pallas_call(
        body, name=name, grid=(r // tm,), in_specs=[spec, spec], out_specs=spec,
        out_shape=jax.ShapeDtypeStruct((r, w), out_dtype), compiler_params=_cparams(("parallel",)),
    )(a, b)


def _ada_fwd(call_all, w_shard):
    def body(c_ref, w_ref, o_ref):
        cv = c_ref[...]
        o_ref[...] = jnp.dot(cv * _sigmoid(cv), w_ref[...], preferred_element_type=F32,
                             precision=lax.Precision.HIGHEST)

    n = w_shard.shape[1]
    return pl.pallas_call(
        body, name="ada_fwd", out_shape=jax.ShapeDtypeStruct((call_all.shape[0], n), F32),
        compiler_params=pltpu.CompilerParams(vmem_limit_bytes=VMEM_LIMIT),
    )(call_all, w_shard)


def _ada_bwd(call_all, dada):
    def body(c_ref, d_ref, o_ref):
        cv = c_ref[...]
        o_ref[...] = lax.dot_general(cv * _sigmoid(cv), d_ref[...], (((0,), (0,)), ((), ())),
                                     preferred_element_type=F32, precision=lax.Precision.HIGHEST)

    return pl.pallas_call(
        body, name="ada_bwd", out_shape=jax.ShapeDtypeStruct((call_all.shape[1], dada.shape[1]), F32),
        compiler_params=pltpu.CompilerParams(vmem_limit_bytes=VMEM_LIMIT),
    )(call_all, dada)


def _adamw(name, w, g, m, v):
    r, wd = w.shape
    tm = _pick(r, (256, 128, 64, 32, 16, 8))
    bc1 = 1.0 - ADAM_B1 ** ADAM_STEP
    bc2 = 1.0 - ADAM_B2 ** ADAM_STEP

    def body(w_ref, g_ref, m_ref, v_ref, d_ref, mo_ref, vo_ref):
        gv = g_ref[...]
        mn = ADAM_B1 * m_ref[...] + (1.0 - ADAM_B1) * gv
        vn = ADAM_B2 * v_ref[...] + (1.0 - ADAM_B2) * (gv * gv)
        d_ref[...] = -ADAM_LR * ((mn / bc1) / (jnp.sqrt(vn / bc2) + ADAM_EPS) + ADAM_WD * w_ref[...])
        mo_ref[...] = mn
        vo_ref[...] = vn

    spec = pl.BlockSpec((tm, wd), lambda i: (i, 0))
    return pl.pallas_call(
        body, name=name, grid=(r // tm,), in_specs=[spec] * 4, out_specs=[spec] * 3,
        out_shape=[jax.ShapeDtypeStruct((r, wd), F32)] * 3, compiler_params=_cparams(("parallel",)),
    )(w, g, m, v)


def _rope_tables(positions):
    half = ROPE_DIMS // 2
    freqs = ROPE_THETA ** (-jnp.arange(0, ROPE_DIMS, 2, dtype=F32) / ROPE_DIMS)
    ang = positions.astype(F32).reshape(-1, 1) * freqs
    cos, sin = jnp.cos(ang), jnp.sin(ang)
    T = ang.shape[0]
    one = jnp.ones((T, HD - ROPE_DIMS), F32)
    zero = jnp.zeros((T, HD - ROPE_DIMS), F32)
    zh = jnp.zeros((T, half), F32)
    c64 = jnp.concatenate([cos, cos, one], axis=1)
    s1 = jnp.concatenate([zh, sin, zero], axis=1)
    s2 = jnp.concatenate([-sin, zh, zero], axis=1)
    rep = lambda t: jnp.concatenate([t] * (LANES // HD), axis=1)
    return rep(c64), rep(s1), rep(s2)


def _local_step(x, loss_target, positions, ada, w_qkv, w_f, w_out, w_up, conv_w8, w_down,
                b_fgate, gn, ln1_g, ln1_b, conv_b, ln2_g, ln2_b, late=None, early=None):
    T = x.shape[0]
    nbl = T // S
    nha = WA // HD
    sv = lambda k: ada[:, k:k + 1, :]
    sh_a, sc_a, g_a, sh_f, sc_f, g_f = (sv(k) for k in range(6))
    rope = _rope_tables(positions)
    neg_rope = (rope[0], -rope[1], -rope[2])
    gseg = jnp.asarray(np.kron(np.eye(min(256, 2 * WA) // HD), np.ones((HD, HD))), BF16)
    bias = jnp.asarray(_dil_bias(TQ))
    bf_pad = jnp.zeros((1, FPAD), F32).at[:, :nha].set(b_fgate)

    qkv, fa = _mod_mm("qkv_proj", x, sc_a, sh_a, (w_qkv, w_f), (BF16, F32), rope=rope, rope_secs=(3, 4))
    pq, pk, oq, ok, sq, sk = _fold_tables(nha)

    def fold_out(cum, f, b_ref, pq_ref, pk_ref, oq_ref, ok_ref):
        hi, mid, lo = _split3(cum)
        eqv = _dot(hi, pq_ref[0]) + _dot(mid, pq_ref[1]) + _dot(lo, pq_ref[2]) + oq_ref[...]
        ekv = ok_ref[...] - (_dot(hi, pk_ref[0]) + _dot(mid, pk_ref[1]) + _dot(lo, pk_ref[2]))
        return eqv, ekv

    eq, ek = _cumsum_seq(
        "fgate_fwd", [fa], [bf_pad, jnp.asarray(pq, BF16), jnp.asarray(pk, BF16), jnp.asarray(oq), jnp.asarray(ok)],
        lambda f, b_ref, *_: _log_sigmoid(f + b_ref[...]), fold_out, ((2 * WA, BF16), (WA, BF16)), reverse=False)
    oa, lse_a, *got = _attn_fwd("fox_fwd", qkv, (0, 1, 2), True, eq=eq, ek=ek, ride=late[0] if late else ())
    if late:
        w_up, w_down = late[1](got)
    ob, lse_b = _attn_fwd("dil_fwd", qkv, (3, 4, 5), False, bias=bias)

    def mix_fn(i, oav, obv, xv, gav, gn_ref, g_ref, wo_ref, l1g_ref, l1b_ref):
        o = jnp.concatenate([oav, obv], axis=1)
        rs = lax.rsqrt(_head_mean(o * o, g_ref) + RMS_EPS)
        merged = (o * rs * gn_ref[...]).astype(BF16)
        mix = _dot(merged, wo_ref[...])
        x1, _, _ = _ln_fwd(ALPHA * xv + gav * mix, l1g_ref[...], l1b_ref[...])
        return merged, mix, x1

    merged, mix, x1 = _rowwise("mix_out", mix_fn, T, 256, tiles=(oa, ob, x), seqvecs=(g_a,),
                               consts=(gn, gseg, w_out, ln1_g, ln1_b),
                               outs=((2 * WA, BF16), (D, F32), (D, F32)))
    u = _mod_mm("ffn_up", x1, sc_f, sh_f, (w_up,), (F32,))[0]

    def conv_y(i, uv, prev, cw_ref, cb_ref, tm):
        first = (i * tm) % S == 0
        s1, s2 = _conv_taps(uv, prev, first)
        y = cb_ref[...] + cw_ref[0:1, :] * s2 + cw_ref[1:2, :] * s1 + cw_ref[2:3, :] * uv
        return y, s1, s2

    tmc = 128

    def gate_fn(i, uv, prev, cw_ref, cb_ref):
        y, _, _ = conv_y(i, uv, prev, cw_ref, cb_ref, tmc)
        a, g = y[:, :DFF], y[:, DFF:]
        return g * _sigmoid(g) * a, y

    act, yconv = _rowwise("conv_gate", gate_fn, T, tmc, tiles=(u,), halos=((u, -1),), consts=(conv_w8, conv_b),
                          outs=((DFF, BF16), (2 * DFF, F32)))

    def down_fn(i, actv, x1v, tgt, gfv, wd_ref, g2_ref, b2_ref):
        ffn = _dot(actv, wd_ref[...])
        y, n2, rstd = _ln_fwd(ALPHA * x1v + gfv * ffn, g2_ref[...], b2_ref[...])
        err = y - tgt
        dy = err * (1.0 / D)
        dr2 = _ln_bwd(dy, n2, rstd, g2_ref[...])
        return (dr2, gfv * dr2, _rsum8(err * err), _rsum8(dy * n2), _rsum8(dy), _rsum8(dr2 * ffn))

    dr2, dffn, loss_acc, d_ln2g, d_ln2b, d_gf = _rowwise(
        "ffn_down_loss", down_fn, T, 256, tiles=(act, x1, loss_target), seqvecs=(g_f,),
        consts=(w_down, ln2_g, ln2_b), outs=((D, F32), (D, F32)), accs=(D, D, D), seqaccs=(D,))

    def gate_conv_bwd_fn(i, uv, yv, dfv, y_nxt, df_nxt, cw_ref, wd_ref):
        last = ((i + 1) * tmc) % S == 0
        y = jnp.concatenate([yv, y_nxt], axis=0)
        df_ext = jnp.concatenate([dfv, df_nxt], axis=0).astype(BF16)
        ch = _pick(DFF, (256, 128))
        dav = jnp.concatenate([_dot_nt(df_ext, wd_ref[c * ch:(c + 1) * ch, :]) for c in range(DFF // ch)], axis=1)
        a, g = y[:, :DFF], y[:, DFF:]
        sg = _sigmoid(g)
        dyc_ext = jnp.concatenate([dav * (g * sg), dav * a * (sg * (1.0 + g * (1.0 - sg)))], axis=1)
        dyc = dyc_ext[:tmc]
        u1, u2 = _conv_taps_up(dyc, dyc_ext[tmc:], last)
        du_ = cw_ref[2:3, :] * dyc + cw_ref[1:2, :] * u1 + cw_ref[0:1, :] * u2
        return du_, _rsum8(dyc), _rsum8(uv * u2), _rsum8(uv * u1), _rsum8(uv * dyc)

    du, d_cb, d_cw0, d_cw1, d_cw2 = _rowwise(
        "gate_conv_bwd", gate_conv_bwd_fn, T, tmc, tiles=(u, yconv, dffn), halos=((yconv, 1), (dffn, 1)),
        consts=(conv_w8, w_down), outs=((2 * DFF, BF16),), accs=(2 * DFF,) * 4)
    g_w_down = _mm_tn("dw_down", act, dffn)
    g_w_up = _mm_tn("dw_up", x1, du, mod=(sc_f, sh_f), by_chip=True)

    def ln1_bwd_fn(i, dr2v, duv, xv, mixv, x1v, scfv, gav, l1g_ref, wu_ref):
        dh2v = _nt_rows(duv, wu_ref)
        dx1 = ALPHA * dr2v + dh2v * (1.0 + scfv)
        _, n1, rstd = _ln_fwd(ALPHA * xv + gav * mixv, l1g_ref[...], 0.0)
        dr1 = _ln_bwd(dx1, n1, rstd, l1g_ref[...])
        return (dr1, gav * dr1, _rsum8(dx1 * n1), _rsum8(dx1),
                _rsum8(dh2v * x1v), _rsum8(dh2v), _rsum8(dr1 * mixv))

    dr1, dmix, d_ln1g, d_ln1b, d_scf, d_shf, d_ga = _rowwise(
        "ln1_bwd", ln1_bwd_fn, T, 256, tiles=(dr2, du, x, mix, x1), seqvecs=(sc_f, g_a), consts=(ln1_g, w_up),
        outs=((D, F32), (D, BF16)), accs=(D, D), seqaccs=(D, D, D))

    g_w_out = _mm_tn("dw_out", merged, dmix)

    def hn_bwd_fn(i, dmixv, oav, obv, gn_ref, g_ref, wo_ref):
        dmv = _nt_rows(dmixv, wo_ref)
        o = jnp.concatenate([oav, obv], axis=1)
        rs = lax.rsqrt(_head_mean(o * o, g_ref) + RMS_EPS)
        nrm = o * rs
        dn = dmv * gn_ref[...]
        do = rs * (dn - nrm * _head_mean(dn * nrm, g_ref))
        return do, _rsum8(dmv * nrm)

    do, d_gn = _rowwise("headnorm_bwd", hn_bwd_fn, T, 256, tiles=(dmix, oa, ob), consts=(gn, gseg, w_out),
                        outs=((2 * WA, F32),), accs=(2 * WA,))
    d_cw = jnp.stack([d_cw0[0], d_cw1[0], d_cw2[0]], axis=0)
    dqa, dka, dva, dqe, dek, *arrived = _attn_bwd(
        "fox_bwd", qkv, (0, 1, 2), oa, do, 0, lse_a, True, eq=eq, ek=ek,
        ride=early(g_w_up, g_w_down, d_cw) if early else None)
    dqb, dkb, dvb = _attn_bwd("dil_bwd", qkv, (3, 4, 5), ob, do, 1, lse_b, False, bias=bias)
    hdot = lambda a, m_ref: sum(_dot(piece, m_ref[...]) for piece in _split3(a))
    dfa, d_bf = _cumsum_seq(
        "fgate_bwd", [dqe, dek, fa], [bf_pad, jnp.asarray(sq, BF16), jnp.asarray(sk, BF16)],
        lambda dq_, dk_, f, b_ref, sq_ref, sk_ref: hdot(dq_, sq_ref) - hdot(dk_, sk_ref),
        lambda cum, dq_, dk_, f, b_ref, sq_ref, sk_ref: (cum * _sigmoid(-(f + b_ref[...])),),
        ((FPAD, F32),), reverse=True, n_acc=1)

    def dz_fn(i, a0, a1, a2, b0, b1, b2, fv, cv, s1v, s2v):
        ct, s1t, s2t = (_tile_lanes(t, WA) for t in (cv, s1v, s2v))
        return jnp.concatenate([a0, a1, a2, _rope(b0, ct, s1t, s2t), _rope(b1, ct, s1t, s2t), b2, fv], axis=1)

    dz = _rowwise("dz_pack", dz_fn, T, 256, tiles=(dqa, dka, dva, dqb, dkb, dvb, dfa) + neg_rope,
                  outs=((6 * WA + FPAD, BF16),))[0]
    w_cat = jnp.concatenate([w_qkv, w_f], axis=1)
    g_w_cat = _mm_tn("dw_in", x, dz, mod=(sc_a, sh_a), tt=256, t2=dz.shape[1])

    def dx_fn(i, dr1v, dzv, xv, scav, wc_ref):
        dh1v = _nt_rows(dzv, wc_ref)
        return ALPHA * dr1v + dh1v * (1.0 + scav), _rsum8(dh1v * xv), _rsum8(dh1v)

    grad_x, d_sca, d_sha = _rowwise("dx_out", dx_fn, T, 256, tiles=(dr1, dz, x), seqvecs=(sc_a,), consts=(w_cat,),
                                    outs=((D, F32),), seqaccs=(D, D))

    row0 = lambda a: a[..., 0, :]
    d_ada = jnp.stack([row0(d_sha), row0(d_sca), row0(d_ga), row0(d_shf), row0(d_scf), row0(d_gf)], axis=1)
    loss_part = (0.5 / D) * jnp.sum(loss_acc[0])
    small = dict(b_fgate=row0(d_bf)[:nha], gn=row0(d_gn), ln1_g=row0(d_ln1g), ln1_b=row0(d_ln1b),
                 conv_b=row0(d_cb), ln2_g=row0(d_ln2g), ln2_b=row0(d_ln2b))
    big = dict(w_cat=g_w_cat, w_out=g_w_out, w_up=g_w_up, conv_w=d_cw, w_down=g_w_down, early=arrived)
    return loss_part, grad_x, d_ada, small, big


def _rows_of(n, w=None):
    return -(-n // (D if w is None else w))


def _as_rows(v):
    w = D
    k = v.shape[0]
    flat = v.reshape(k, -1)
    rows = _rows_of(_rows_of(flat.shape[1], w), SUBLANES) * SUBLANES
    flat = jnp.pad(flat, ((0, 0), (0, rows * w - flat.shape[1])))
    return flat.reshape(k, rows, w)


def kernel(x, c, positions, w_ada, b_ada, w_in, b_fgate, gn_a, gn_b, w_out, ln1_g, ln1_b, w_up, conv_w, conv_b, w_down, ln2_g, ln2_b, loss_target, m_w_ada, m_b_ada, m_w_in, m_b_fgate, m_gn_a, m_gn_b, m_w_out, m_ln1_g, m_ln1_b, m_w_up, m_conv_w, m_conv_b, m_w_down, m_ln2_g, m_ln2_b, v_w_ada, v_b_ada, v_w_in, v_b_fgate, v_gn_a, v_gn_b, v_w_out, v_ln1_g, v_ln1_b, v_w_up, v_conv_w, v_conv_b, v_w_down, v_ln2_g, v_ln2_b):
    mx, my, mc = lax.axis_index("x"), lax.axis_index("y"), lax.axis_index("c")
    dev = _dev_index(mx, my, mc)
    chip = _chip_index(mx, my, mc)
    nbl = x.shape[0]
    T = nbl * S
    nha = WA // HD
    n_ada = w_ada.shape[2]

    c_pad = jnp.zeros((SUBLANES, D), F32).at[:nbl].set(c)
    c_all = _all_gather8("gather_c", c_pad)[:, :nbl].reshape(NDEV * nbl, D)
    ada_part = _ada_fwd(c_all, w_ada[0])
    n_cw = conv_w.shape[2]
    cw_rows = jnp.pad(conv_w[0], ((0, SUBLANES - conv_w.shape[1]), (0, n_ada - n_cw)))
    ada_blocks = _all_gather8("gather_ada", jnp.concatenate([ada_part, cw_rows], axis=0))
    n_c = NDEV * nbl
    ada_all = jnp.concatenate([ada_blocks[2 * k, :n_c] for k in range(NCHIP)], axis=1) + b_ada
    conv_w8 = jnp.concatenate([ada_blocks[2 * k, n_c:, :n_cw] for k in range(NCHIP)], axis=1)
    ada = lax.dynamic_slice_in_dim(ada_all, dev * nbl, nbl, axis=0).reshape(nbl, 6, D)

    w_in_sh = jnp.pad(w_in[0].astype(BF16), ((0, 0), (0, _rows_of(w_in.shape[2], LANES) * LANES - w_in.shape[2])))
    halve = lambda t: t.reshape(2, t.shape[0] // 2, t.shape[1])
    whole = lambda g, t: [b.reshape(t.shape) for b in _by_chip(halve(t), g, chip)]
    w_out_sh, w_up_sh, w_down_sh = w_out[0].astype(BF16), w_up[0].astype(BF16), w_down[0].astype(BF16)
    g_in, g_out = _gather_halves("gather_w", [halve(w_in_sh), halve(w_out_sh)])
    w_in_full = jnp.concatenate([b[:, :w_in.shape[2]] for b in whole(g_in, w_in_sh)], axis=1)
    w_qkv = jnp.concatenate([w_in_full[:, :3 * WA], w_in_full[:, 3 * WA + nha:]], axis=1)
    w_f = jnp.pad(w_in_full[:, 3 * WA:3 * WA + nha], ((0, 0), (0, FPAD - nha)))
    w_out_full = jnp.concatenate(whole(g_out, w_out_sh), axis=0)

    def late_weights(got):
        return (jnp.concatenate(whole(got[0], w_up_sh), axis=1), jnp.concatenate(whole(got[1], w_down_sh), axis=0))

    n_in = w_in.shape[2]
    to_sib = lambda x, y, c: None

    def reduce_front(blocks, tag):
        cut = [t.reshape(2 * NCHIP, t.shape[1] // 2, t.shape[2]) for t in blocks]
        from_sib = _exchange(
            "pair_swap_" + tag, cut, [jax.ShapeDtypeStruct((NCHIP,) + t.shape[1:], F32) for t in cut],
            [((0, 0, 1), n, lambda x, y, c, k=k: 2 * k + 1 - c, n, lambda x, y, c, k=k: k)
             for n in range(len(cut)) for k in range(NCHIP)], [])
        flat = lambda v: v.reshape(-1, v.shape[-1])
        pair_sums = []
        for n, (t, fs) in enumerate(zip(cut, from_sib)):
            mine = lax.dynamic_index_in_dim(t.reshape((NCHIP, 2) + t.shape[1:]), mc, axis=1, keepdims=False)
            pair_sums.append(_add2("pair_sum_%s%d" % (tag, n), flat(mine), flat(fs), out_dtype=BF16).reshape(fs.shape))
        copies = []
        for n in range(len(cut)):
            for j, f in enumerate(_CHIP_FLIPS):
                src = lambda x, y, c, f=f: _chip_index(1 - x if f[0] else x, 1 - y if f[1] else y, c)
                copies.append((f, n, src, n, lambda x, y, c, j=j: j))
        shapes = [jax.ShapeDtypeStruct((NCHIP - 1,) + t.shape[1:], BF16) for t in pair_sums]
        return pair_sums, (pair_sums, shapes, copies)

    def reduce_back(pair_sums, arrived, tag):
        my_halves = []
        for n, (ps, got) in enumerate(zip(pair_sums, arrived)):
            own = lax.dynamic_index_in_dim(ps, chip, axis=0, keepdims=True)
            my_halves.append(_sum_leading("chip_sum_%s%d" % (tag, n), jnp.concatenate([own, got], axis=0)))
        sib_halves = _exchange("pair_share_" + tag, my_halves, [jax.ShapeDtypeStruct(t.shape, F32) for t in my_halves],
                               [((0, 0, 1), n, to_sib, n, to_sib) for n in range(len(my_halves))], [])
        whole_blocks = []
        for mh, shf in zip(my_halves, sib_halves):
            pair = jnp.stack([mh, shf])
            whole_blocks.append(jnp.concatenate([lax.dynamic_index_in_dim(pair, mc, axis=0, keepdims=False),
                                                 lax.dynamic_index_in_dim(pair, 1 - mc, axis=0, keepdims=False)], axis=0))
        return whole_blocks

    stash = {}

    def early_reduce(g_up, g_down, g_cw):
        sh_cw = _as_rows(g_cw.reshape(conv_w.shape[1], NCHIP, -1).transpose(1, 0, 2))
        rows = w_down.shape[1] + sh_cw.shape[1]
        pad = _rows_of(rows, 2 * LANES) * 2 * LANES - rows
        sh_a = jnp.concatenate([g_down.reshape(NCHIP, -1, D), sh_cw, jnp.zeros((NCHIP, pad, D), F32)], axis=1)
        stash["ps"], ride = reduce_front([sh_a, g_up], "e")
        return ride

    gn = jnp.concatenate([gn_a, gn_b], axis=1)
    loss_part, grad_x, d_ada, small, big = _local_step(
        x.reshape(T, D), loss_target.reshape(T, D), positions, ada, w_qkv, w_f, w_out_full, None, conv_w8,
        None, b_fgate, gn, ln1_g, ln1_b, conv_b, ln2_g, ln2_b, late=([halve(w_up_sh), halve(w_down_sh)], late_weights), early=early_reduce)

    def row_pad(v, rows):
        flat = v.reshape(-1)
        return jnp.pad(flat, (0, rows * D - flat.shape[0]))

    n_cb = _rows_of(2 * DFF)
    small_flat = jnp.concatenate([
        row_pad(small["b_fgate"], 1), row_pad(small["gn"], 1), row_pad(small["ln1_g"], 1),
        row_pad(small["ln1_b"], 1), row_pad(small["ln2_g"], 1), row_pad(small["ln2_b"], 1),
        row_pad(jnp.full((1,), loss_part, F32), 1), row_pad(small["conv_b"], n_cb)])
    n_small = _rows_of(small_flat.shape[0], SUBLANES * D) * SUBLANES
    small_rows = jnp.pad(small_flat, (0, n_small * D - small_flat.shape[0])).reshape(n_small, D)
    ada_rows = jnp.pad(d_ada.reshape(nbl, 6, D), ((0, 0), (0, SUBLANES - 6), (0, 0))).reshape(nbl * SUBLANES, D)
    gathered = _all_gather8("gather_small", jnp.concatenate([small_rows, ada_rows], axis=0))
    red = _sum_leading("sum_small", gathered, tm=SUBLANES)
    g_b_fgate = red[0:1, :nha]
    g_gn = red[1:2, :2 * WA]
    g_ln1_g, g_ln1_b, g_ln2_g, g_ln2_b = red[2:3], red[3:4], red[4:5], red[5:6]
    loss = red[6, 0]
    g_conv_b = red[7:7 + n_cb].reshape(1, -1)[:, :2 * DFF]
    g_b_ada = _add2("sum_b_ada", red[n_small:n_small + SUBLANES], red[n_small + SUBLANES:n_small + 2 * SUBLANES],
                    tm=SUBLANES)[:6].reshape(1, 6 * D)
    dada_all = gathered[:, n_small:].reshape(NDEV, nbl, SUBLANES, D)[:, :, :6].reshape(NDEV * nbl, 6 * D)
    g_w_ada = _ada_bwd(c_all, lax.dynamic_slice_in_dim(dada_all, chip * n_ada, n_ada, axis=1))

    g_cat = big["w_cat"]
    g_w_in_full = jnp.concatenate([g_cat[:, :3 * WA], g_cat[:, 6 * WA:6 * WA + nha], g_cat[:, 3 * WA:6 * WA]], axis=1)
    sh_in = jnp.pad(g_w_in_full.reshape(D, NCHIP, n_in).transpose(1, 0, 2),
                    ((0, 0), (0, 0), (0, _rows_of(n_in, LANES) * LANES - n_in)))
    ps_l, scatter_l = reduce_front([big["w_out"].reshape(NCHIP, -1, D), sh_in], "l")
    arrived_l = _exchange("scatter_l", ps_l, scatter_l[1], scatter_l[2], [])
    shards_e = reduce_back(stash["ps"], big["early"], "e")
    shards_l = reduce_back(ps_l, arrived_l, "l")
    r0 = w_down.shape[1]
    g_w_down = shards_e[0][:r0]
    g_conv_w = shards_e[0][r0:r0 + SUBLANES].reshape(-1)[:int(np.prod(conv_w.shape[1:]))].reshape(conv_w.shape[1:])
    g_w_up = shards_e[1]
    g_w_out = shards_l[0]
    g_w_in = shards_l[1][:, :n_in]

    grads = dict(w_ada=g_w_ada, b_ada=g_b_ada, w_in=g_w_in, b_fgate=g_b_fgate, gn_a=g_gn[:, :WA], gn_b=g_gn[:, WA:],
                 w_out=g_w_out, ln1_g=g_ln1_g, ln1_b=g_ln1_b, w_up=g_w_up, conv_w=g_conv_w, conv_b=g_conv_b,
                 w_down=g_w_down, ln2_g=g_ln2_g, ln2_b=g_ln2_b)
    weights = dict(w_ada=w_ada, b_ada=b_ada, w_in=w_in, b_fgate=b_fgate, gn_a=gn_a, gn_b=gn_b, w_out=w_out,
                   ln1_g=ln1_g, ln1_b=ln1_b, w_up=w_up, conv_w=conv_w, conv_b=conv_b, w_down=w_down,
                   ln2_g=ln2_g, ln2_b=ln2_b)
    ms = dict(w_ada=m_w_ada, b_ada=m_b_ada, w_in=m_w_in, b_fgate=m_b_fgate, gn_a=m_gn_a, gn_b=m_gn_b,
              w_out=m_w_out, ln1_g=m_ln1_g, ln1_b=m_ln1_b, w_up=m_w_up, conv_w=m_conv_w, conv_b=m_conv_b,
              w_down=m_w_down, ln2_g=m_ln2_g, ln2_b=m_ln2_b)
    vs = dict(w_ada=v_w_ada, b_ada=v_b_ada, w_in=v_w_in, b_fgate=v_b_fgate, gn_a=v_gn_a, gn_b=v_gn_b,
              w_out=v_w_out, ln1_g=v_ln1_g, ln1_b=v_ln1_b, w_up=v_w_up, conv_w=v_conv_w, conv_b=v_conv_b,
              w_down=v_w_down, ln2_g=v_ln2_g, ln2_b=v_ln2_b)
    names = list(weights)
    big_names = ("w_ada", "w_in", "w_out", "w_up", "w_down")
    delta, new_m, new_v = {}, {}, {}
    for n in big_names:
        shp = weights[n].shape
        d, m2, v2 = _adamw("adamw_" + n, weights[n][0], grads[n].reshape(shp[1:]), ms[n][0], vs[n][0])
        delta[n], new_m[n], new_v[n] = d.reshape(shp), m2.reshape(shp), v2.reshape(shp)
    small_names = [n for n in names if n not in big_names]

    def pack_small(src):
        flats = []
        for n in small_names:
            flat = src[n].reshape(-1)
            flats.append(jnp.pad(flat, (0, _rows_of(flat.shape[0]) * D - flat.shape[0])))
        allf = jnp.concatenate(flats)
        rows = _rows_of(allf.shape[0], SUBLANES * D) * SUBLANES
        return jnp.pad(allf, (0, rows * D - allf.shape[0])).reshape(rows, D)

    sd, sm, sv_ = _adamw("adamw_small", pack_small(weights), pack_small(grads), pack_small(ms), pack_small(vs))
    off = 0
    for n in small_names:
        shp = weights[n].shape
        cnt = int(np.prod(shp))
        r = _rows_of(cnt)
        for dst, src in ((delta, sd), (new_m, sm), (new_v, sv_)):
            dst[n] = src[off:off + r].reshape(-1)[:cnt].reshape(shp)
        off += r

    out_g = {n: grads[n].reshape(weights[n].shape) for n in names}
    return (loss, grad_x.reshape(x.shape), *[out_g[n] for n in names], *[delta[n] for n in names],
            *[new_m[n] for n in names], *[new_v[n] for n in names])
```

```python
import numpy as np
import jax
import jax.numpy as jnp
from jax import lax
from jax.experimental import pallas as pl
from jax.experimental.pallas import tpu as pltpu

F32 = jnp.float32
BF16 = jnp.bfloat16

D = 1024
S = 4096
HD = 64
WA = 512
DFF = 2816
NCHIP = 4
NDEV = 8
PATTERNS = ((128, 1), (512, 4), (2048, 16))
ROPE_THETA = 500000.0
ROPE_DIMS = HD // 4
ALPHA = (2.0 * 1) ** 0.25
LN_EPS = 1e-5
RMS_EPS = 1e-6
ADAM_LR = 0.001
ADAM_B1 = 0.9
ADAM_B2 = 0.999
ADAM_EPS = 1e-08
ADAM_WD = 0.01
ADAM_STEP = 10

LANES = 128
SUBLANES = 8
TQ = 512
FPAD = LANES
NEG = -1e30
VMEM_LIMIT = 56 * 1024 * 1024
MESH = pl.DeviceIdType.MESH


def _cparams(sem):
    return pltpu.CompilerParams(dimension_semantics=sem, vmem_limit_bytes=VMEM_LIMIT)


def _pick(n, cands):
    for c in cands:
        if n % c == 0:
            return c
    return n


def _rsum8(v):
    tm, w = v.shape
    return jnp.sum(v.reshape(tm // SUBLANES, SUBLANES, w), axis=0)


def _sigmoid(x):
    return 1.0 / (1.0 + jnp.exp(-x))


def _dot(a, b):
    return jnp.dot(a, b, preferred_element_type=F32)


def _dot_nt(a, b):
    return lax.dot_general(a, b, (((1,), (1,)), ((), ())), preferred_element_type=F32)


def _dot_tn(a, b):
    return lax.dot_general(a, b, (((0,), (0,)), ((), ())), preferred_element_type=F32)


def _rowwise(name, fn, T, tm, *, tiles=(), halos=(), seqvecs=(), consts=(), outs=(), accs=(), seqaccs=(),
             seq_len=None, ride=None):
    seq_len = S if seq_len is None else seq_len
    nb = T // tm
    spb = max(seq_len // tm, 1)
    nseq = max(T // seq_len, 1)
    n8 = T // SUBLANES
    r8 = tm // SUBLANES
    in_specs, args = [], []
    for a in tiles:
        in_specs.append(pl.BlockSpec((tm, a.shape[1]), lambda i: (i, 0)))
        args.append(a)
    for a, direction in halos:
        if direction < 0:
            idx = lambda i: (jnp.maximum(i * r8 - 1, 0), 0)
        else:
            idx = lambda i: (jnp.minimum((i + 1) * r8, n8 - 1), 0)
        in_specs.append(pl.BlockSpec((SUBLANES, a.shape[1]), idx))
        args.append(a)
    for a in seqvecs:
        in_specs.append(pl.BlockSpec((1, 1, a.shape[2]), lambda i: (i // spb, 0, 0)))
        args.append(a)
    for a in consts:
        in_specs.append(pl.BlockSpec(a.shape, lambda i, nd=a.ndim: (0,) * nd))
        args.append(a)
    out_shape, out_specs = [], []
    for w, dt in outs:
        out_shape.append(jax.ShapeDtypeStruct((T, w), dt))
        out_specs.append(pl.BlockSpec((tm, w), lambda i: (i, 0)))
    for w in accs:
        out_shape.append(jax.ShapeDtypeStruct((SUBLANES, w), F32))
        out_specs.append(pl.BlockSpec((SUBLANES, w), lambda i: (0, 0)))
    for w in seqaccs:
        out_shape.append(jax.ShapeDtypeStruct((nseq, SUBLANES, w), F32))
        out_specs.append(pl.BlockSpec((1, SUBLANES, w), lambda i: (i // spb, 0, 0)))
    n_t, n_h, n_s, n_c = len(tiles), len(halos), len(seqvecs), len(consts)
    n_o, n_a, n_sa = len(outs), len(accs), len(seqaccs)
    r_in, r_out, r_copies = ride if ride else ((), (), ())
    n_bi, n_bo = n_t + n_h + n_s + n_c, n_o + n_a + n_sa

    def body(*refs):
        i = pl.program_id(0)
        ins = refs[:n_bi]
        orefs = refs[n_bi + len(r_in):n_bi + len(r_in) + n_bo]
        if ride:
            ride_start, ride_finish = _remote_steps(
                refs[n_bi:n_bi + len(r_in)], refs[n_bi + len(r_in) + n_bo:n_bi + len(r_in) + n_bo + len(r_out)],
                *refs[n_bi + len(r_in) + n_bo + len(r_out):], r_copies)
            pl.when(i == 0)(ride_start)
        vals = [r[...] for r in ins[:n_t + n_h]]
        vals += [r[0] for r in ins[n_t + n_h:n_t + n_h + n_s]]
        vals += list(ins[n_t + n_h + n_s:])
        res = fn(i, *vals)
        if not isinstance(res, (tuple, list)):
            res = (res,)
        for k in range(n_o):
            orefs[k][...] = res[k].astype(orefs[k].dtype)
        for k in range(n_a):
            r = orefs[n_o + k]

            @pl.when(i == 0)
            def _():
                r[...] = jnp.zeros_like(r)

            r[...] += res[n_o + k]

            @pl.when(i == nb - 1)
            def _():
                r[...] = jnp.broadcast_to(jnp.sum(r[...], axis=0, keepdims=True), r.shape)
        for k in range(n_sa):
            r = orefs[n_o + n_a + k]

            @pl.when(i % spb == 0)
            def _():
                r[...] = jnp.zeros_like(r)

            r[0] += res[n_o + n_a + k]

            @pl.when(i % spb == spb - 1)
            def _():
                r[0] = jnp.broadcast_to(jnp.sum(r[0], axis=0, keepdims=True), r.shape[1:])
        if ride:
            pl.when(i == nb - 1)(ride_finish)

    sem = ("arbitrary",) if (n_a or n_sa or ride) else ("parallel",)
    any_spec = pl.BlockSpec(memory_space=pl.ANY)
    scratch = [pltpu.SemaphoreType.DMA((len(r_copies),)), pltpu.SemaphoreType.DMA((len(r_copies),))] if ride else []
    return pl.pallas_call(
        body, name=name, grid=(nb,), in_specs=in_specs + [any_spec] * len(r_in),
        out_specs=out_specs + [any_spec] * len(r_out), out_shape=out_shape + list(r_out),
        scratch_shapes=scratch, compiler_params=_cparams(sem),
    )(*args, *r_in)


def _ln_fwd(r, g, b):
    mu = jnp.mean(r, axis=-1, keepdims=True)
    xc = r - mu
    var = jnp.mean(xc * xc, axis=-1, keepdims=True)
    rstd = lax.rsqrt(var + LN_EPS)
    n = xc * rstd
    return n * g + b, n, rstd


def _ln_bwd(dy, n, rstd, g):
    dn = dy * g
    return rstd * (dn - jnp.mean(dn, axis=-1, keepdims=True) - n * jnp.mean(dn * n, axis=-1, keepdims=True))


def _head_mean(t, g_ref):
    gw = g_ref.shape[0]
    hi = t.astype(BF16)
    lo = (t - hi.astype(F32)).astype(BF16)
    g = g_ref[...]
    parts = []
    for c in range(t.shape[1] // gw):
        sl = slice(c * gw, (c + 1) * gw)
        parts.append(_dot(hi[:, sl], g) + _dot(lo[:, sl], g))
    out = parts[0] if len(parts) == 1 else jnp.concatenate(parts, axis=1)
    return out * (1.0 / HD)


def _rope(z, c, s1, s2):
    w = z.shape[1]
    half = ROPE_DIMS // 2
    return z * c + pltpu.roll(z, half, 1) * s1 + pltpu.roll(z, w - half, 1) * s2


def _tile_lanes(t, w):
    reps = w // t.shape[1]
    return t if reps == 1 else jnp.concatenate([t] * reps, axis=1)


def _conv_taps(ext, prev, first):
    prev = jnp.where(first, jnp.zeros_like(prev), prev)
    r8 = lax.broadcasted_iota(jnp.int32, (SUBLANES, 1), 0)
    top = ext[0:SUBLANES]
    s1_top = jnp.where(r8 < 1, pltpu.roll(prev, 1, 0), pltpu.roll(top, 1, 0))
    s2_top = jnp.where(r8 < 2, pltpu.roll(prev, 2, 0), pltpu.roll(top, 2, 0))
    s1 = jnp.concatenate([s1_top, pltpu.roll(ext, 1, 0)[SUBLANES:]], axis=0)
    s2 = jnp.concatenate([s2_top, pltpu.roll(ext, 2, 0)[SUBLANES:]], axis=0)
    return s1, s2


def _conv_taps_up(ext, nxt, last):
    tm = ext.shape[0]
    nxt = jnp.where(last, jnp.zeros_like(nxt), nxt)
    r8 = lax.broadcasted_iota(jnp.int32, (SUBLANES, 1), 0)
    bot = ext[tm - SUBLANES:tm]
    u1_bot = jnp.where(r8 >= 7, pltpu.roll(nxt, 7, 0), pltpu.roll(bot, 7, 0))
    u2_bot = jnp.where(r8 >= 6, pltpu.roll(nxt, 6, 0), pltpu.roll(bot, 6, 0))
    u1 = jnp.concatenate([pltpu.roll(ext, tm - 1, 0)[:tm - SUBLANES], u1_bot], axis=0)
    u2 = jnp.concatenate([pltpu.roll(ext, tm - 2, 0)[:tm - SUBLANES], u2_bot], axis=0)
    return u1, u2


def _nt_rows(av, w_ref):
    n = w_ref.shape[0]
    ch = _pick(n, (512, 256, 128))
    ab = av.astype(BF16)
    parts = [_dot_nt(ab, w_ref[c * ch:(c + 1) * ch, :]) for c in range(n // ch)]
    return parts[0] if len(parts) == 1 else jnp.concatenate(parts, axis=1)


def _mm_tn(name, a, b, *, mod=None, tt=512, t2=None, by_chip=False):
    T, k1 = a.shape
    k2 = b.shape[1]
    t1 = k1 if k1 <= 1536 else _pick(k1, (1408, 1024, 512, 256, 128))
    if t2 is None:
        t2 = k2 if k2 <= 1536 else _pick(k2, (1408, 1024, 640, 512, 256, 128))
    wc = k2 // NCHIP
    if by_chip:
        t2 = 2 * wc
    tt = min(tt, S)
    spb = S // tt

    def body(*refs):
        if mod is not None:
            a_ref, sc_ref, sh_ref, b_ref, o_ref = refs
        else:
            a_ref, b_ref, o_ref = refs
        t = pl.program_id(2)

        @pl.when(t == 0)
        def _():
            o_ref[...] = jnp.zeros_like(o_ref)

        av = a_ref[...]
        if mod is not None:
            av = av * (1.0 + sc_ref[0]) + sh_ref[0]
        res = _dot_tn(av.astype(BF16), b_ref[...].astype(BF16))
        if by_chip:
            o_ref[0] += res[:, :wc]
            o_ref[1] += res[:, wc:]
        else:
            o_ref[...] += res

    in_specs = [pl.BlockSpec((tt, t1), lambda p, q, t: (t, p))]
    args = [a]
    if mod is not None:
        for v in mod:
            in_specs.append(pl.BlockSpec((1, 1, t1), lambda p, q, t: (t // spb, 0, p)))
            args.append(v)
    in_specs.append(pl.BlockSpec((tt, t2), lambda p, q, t: (t, q)))
    args.append(b)
    if by_chip:
        out_specs = pl.BlockSpec((2, t1, wc), lambda p, q, t: (q, p, 0))
        out_shape = jax.ShapeDtypeStruct((NCHIP, k1, wc), F32)
    else:
        out_specs = pl.BlockSpec((t1, t2), lambda p, q, t: (p, q))
        out_shape = jax.ShapeDtypeStruct((k1, k2), F32)
    return pl.pallas_call(
        body, name=name, grid=(k1 // t1, k2 // t2, T // tt), in_specs=in_specs, out_specs=out_specs,
        out_shape=out_shape, compiler_params=_cparams(("parallel", "parallel", "arbitrary")),
    )(*args)


def _mod_mm(name, x, sc, sh, ws, out_dtypes, rope=None, rope_secs=(), tm=256):
    T = x.shape[0]
    nw = len(ws)

    def fn(i, xv, *rest):
        if rope is not None:
            cv, s1v, s2v = rest[:3]
            rest = rest[3:]
        scv, shv = rest[:2]
        w_refs = rest[2:]
        h = (xv * (1.0 + scv) + shv).astype(BF16)
        res = []
        for k, w_ref in enumerate(w_refs):
            n = w_ref.shape[1]
            ch = WA if (k == 0 and rope is not None) else _pick(n, (512, 256, 128))
            parts = []
            for c in range(n // ch):
                z = _dot(h, w_ref[:, c * ch:(c + 1) * ch])
                if k == 0 and c in rope_secs:
                    z = _rope(z, _tile_lanes(cv, ch), _tile_lanes(s1v, ch), _tile_lanes(s2v, ch))
                parts.append(z.astype(out_dtypes[k]))
            res.append(parts[0] if len(parts) == 1 else jnp.concatenate(parts, axis=1))
        return tuple(res)

    tiles = (x,) + (tuple(rope) if rope is not None else ())
    outs = tuple((w.shape[1], dt) for w, dt in zip(ws, out_dtypes))
    return _rowwise(name, fn, T, tm, tiles=tiles, seqvecs=(sc, sh), consts=tuple(ws), outs=outs)


def _tri(tb, lower):
    r = lax.broadcasted_iota(jnp.int32, (tb, tb), 0)
    c = lax.broadcasted_iota(jnp.int32, (tb, tb), 1)
    return jnp.where((r >= c) if lower else (r <= c), 1.0, 0.0).astype(BF16)


def _split3(x):
    hi = x.astype(BF16)
    r = x - hi.astype(F32)
    mid = r.astype(BF16)
    return hi, mid, (r - mid.astype(F32)).astype(BF16)


def _cumsum_seq(name, ins, consts, fn_in, fn_out, outs, reverse, n_acc=0, tb=256):
    T = ins[0].shape[0]
    tb = min(tb, S)
    nbs = S // tb
    nseq = T // S
    n_i, n_c, n_o = len(ins), len(consts), len(outs)

    def blk(b, j):
        return (b * nbs + (nbs - 1 - j if reverse else j), 0)

    def body(*refs):
        i_refs, c_refs = refs[:n_i], refs[n_i:n_i + n_c]
        o_refs = refs[n_i + n_c:n_i + n_c + n_o]
        acc_refs = refs[n_i + n_c + n_o:n_i + n_c + n_o + n_acc]
        carry = refs[-1]
        b, j = pl.program_id(0), pl.program_id(1)

        @pl.when(j == 0)
        def _():
            carry[...] = jnp.zeros_like(carry)

        iv = [r[...] for r in i_refs]
        xin = fn_in(*iv, *c_refs)
        tri = _tri(tb, not reverse)
        cum = sum(_dot(tri, piece) for piece in _split3(xin)) + carry[0:1, :]
        carry[...] = carry[...] + jnp.sum(xin, axis=0, keepdims=True)
        res = fn_out(cum, *iv, *c_refs)
        for o, r in zip(o_refs, res):
            o[...] = r.astype(o.dtype)
        for a in acc_refs:
            @pl.when((b == 0) & (j == 0))
            def _():
                a[...] = jnp.zeros_like(a)

            a[...] += _rsum8(res[0])

            @pl.when((b == nseq - 1) & (j == nbs - 1))
            def _():
                a[...] = jnp.broadcast_to(jnp.sum(a[...], axis=0, keepdims=True), a.shape)

    in_specs = [pl.BlockSpec((tb, a.shape[1]), blk) for a in ins]
    in_specs += [pl.BlockSpec(c.shape, lambda b, j, nd=c.ndim: (0,) * nd) for c in consts]
    out_shape = [jax.ShapeDtypeStruct((T, w), dt) for w, dt in outs]
    out_shape += [jax.ShapeDtypeStruct((SUBLANES, outs[0][0]), F32)] * n_acc
    out_specs = [pl.BlockSpec((tb, w), blk) for w, _ in outs]
    out_specs += [pl.BlockSpec((SUBLANES, outs[0][0]), lambda b, j: (0, 0))] * n_acc
    return pl.pallas_call(
        body, name=name, grid=(nseq, nbs), in_specs=in_specs, out_specs=out_specs, out_shape=out_shape,
        scratch_shapes=[pltpu.VMEM((SUBLANES, FPAD), F32)],
        compiler_params=_cparams(("arbitrary", "arbitrary")),
    )(*ins, *consts)


def _log_sigmoid(x):
    return jnp.minimum(x, 0.0) - jnp.log(1.0 + jnp.exp(-jnp.abs(x)))


def _dil_bias(tq):
    max_win = max(w for w, _ in PATTERNS)
    nd = (max_win + tq - 1) // tq + 1
    qi = np.arange(tq)[:, None]
    kj = np.arange(tq)[None, :]
    tabs = []
    for dlt in range(nd):
        dist = dlt * tq + qi - kj
        mult = np.zeros((tq, tq), np.float64)
        for win, dil in PATTERNS:
            mult += (dist >= 0) & (dist % dil == 0) & (dist // dil <= win // dil)
        tabs.append(np.where(mult > 0, np.log(np.maximum(mult, 1.0)), NEG))
    return np.stack(tabs).astype(np.float32)


def _fold_tables(nha):
    hp_n = nha // 2
    pq = np.zeros((3, FPAD, hp_n * 2 * LANES), np.float32)
    pk = np.zeros((3, FPAD, hp_n * LANES), np.float32)
    oq = np.zeros((1, hp_n * 2 * LANES), np.float32)
    ok = np.zeros((1, hp_n * LANES), np.float32)
    sq = np.zeros((hp_n * LANES, FPAD), np.float32)
    sk = np.zeros((hp_n * LANES, FPAD), np.float32)
    for h in range(nha):
        hp, odd = divmod(h, 2)
        qb = hp * 2 * LANES + odd * (LANES + 8)
        kb = hp * LANES + odd * 8
        for i in range(3):
            pq[i, h, qb + i] = 1
            oq[0, qb + 3 + i] = 1
            ok[0, kb + i] = 1
            pk[i, h, kb + 3 + i] = 1
        sq[kb, h] = 1
        sk[kb + 3, h] = 1
    return pq, pk, oq, ok, sq, sk


def _stack_heads(x2, h0, extra=None):
    z = jnp.zeros_like(x2)
    a, b = jnp.where(h0, x2, z), jnp.where(h0, z, x2)
    if extra is not None:
        a = jnp.concatenate([a, extra[:, :LANES]], axis=1)
        b = jnp.concatenate([b, extra[:, LANES:]], axis=1)
    return jnp.concatenate([a, b], axis=0)


def _attn_fwd(name, qkv, secs, fox, eq=None, ek=None, bias=None, ride=()):
    T = qkv.shape[0]
    nq = S // TQ
    nbl = T // S
    hp_n = WA // LANES
    sq, sk, sv = (s * hp_n for s in secs)
    scale = HD ** -0.5
    nd = None if fox else bias.shape[0]

    n_r = len(ride)
    n_in = (5 if fox else 4) + n_r

    def body(*refs):
        if fox:
            q_ref, k_ref, v_ref, eq_ref, ek_ref = refs[:5]
        else:
            q_ref, k_ref, v_ref, b_ref = refs[:4]
        o_ref, lse_ref = refs[n_in:n_in + 2]
        i = pl.program_id(2)
        if n_r:
            ride_start, ride_finish = _gather_halves_steps(
                refs[n_in - n_r:n_in], refs[n_in + 2:n_in + 2 + n_r], *refs[n_in + 2 + n_r:])
            at = lambda b, hp, q: (pl.program_id(0) == b) & (pl.program_id(1) == hp) & (i == q)
            pl.when(at(0, 0, 0))(ride_start)
        lane = lax.broadcasted_iota(jnp.int32, (1, LANES), 1)
        h0 = lane < HD
        q2 = (q_ref[...].astype(F32) * scale).astype(BF16)
        qs = _stack_heads(q2, h0, eq_ref[...] if fox else None)

        def scores(t, diag):
            off = pl.multiple_of((i - t) * TQ, TQ)
            kk = k_ref[pl.ds(off, TQ), :]
            if fox:
                kk = jnp.concatenate([kk, ek_ref[pl.ds(off, TQ), :]], axis=1)
            s = jnp.concatenate([_dot_nt(qs[:TQ], kk), _dot_nt(qs[TQ:], kk)], axis=0)
            if not fox:
                s = (s.reshape(2, TQ, TQ) + b_ref[t]).reshape(2 * TQ, TQ)
            elif diag:
                rows = lax.broadcasted_iota(jnp.int32, (2, TQ, TQ), 1).reshape(2 * TQ, TQ)
                cols = lax.broadcasted_iota(jnp.int32, (2 * TQ, TQ), 1)
                s = jnp.where(cols <= rows, s, NEG)
            return s

        def update(t, s, m, l, acc):
            off = pl.multiple_of((i - t) * TQ, TQ)
            v2 = v_ref[pl.ds(off, TQ), :]
            m_new = jnp.maximum(m, jnp.max(s, axis=1, keepdims=True))
            p = jnp.exp(s - m_new)
            a = jnp.exp(m - m_new)
            l = a * l + jnp.sum(p, axis=1, keepdims=True)
            pb = p.astype(BF16)
            acc = a * acc + jnp.concatenate([_dot(pb[:TQ], v2), _dot(pb[TQ:], v2)], axis=0)
            return m_new, l, acc

        init = (jnp.full((2 * TQ, 1), NEG, F32), jnp.zeros((2 * TQ, 1), F32), jnp.zeros((2 * TQ, LANES), F32))
        n = i + 1 if fox else jnp.minimum(i + 1, nd)
        m, l, acc = update(0, scores(0, True), *init)
        m, l, acc = lax.fori_loop(1, n, lambda t, c: update(t, scores(t, False), *c), (m, l, acc))
        on = acc / l
        o_ref[...] = jnp.where(h0, on[:TQ], on[TQ:])
        lse = jnp.broadcast_to(m + jnp.log(l), (2 * TQ, LANES))
        lse_ref[...] = jnp.concatenate([lse[:TQ], lse[TQ:]], axis=1)
        if n_r:
            pl.when(at(nbl - 1, hp_n - 1, nq - 1))(ride_finish)

    in_specs = [
        pl.BlockSpec((TQ, LANES), lambda b, hp, i: (b * nq + i, sq + hp)),
        pl.BlockSpec((S, LANES), lambda b, hp, i: (b, sk + hp)),
        pl.BlockSpec((S, LANES), lambda b, hp, i: (b, sv + hp)),
    ]
    args = [qkv, qkv, qkv]
    if fox:
        in_specs += [pl.BlockSpec((TQ, 2 * LANES), lambda b, hp, i: (b * nq + i, hp)),
                     pl.BlockSpec((S, LANES), lambda b, hp, i: (b, hp))]
        args += [eq, ek]
    else:
        in_specs.append(pl.BlockSpec(bias.shape, lambda b, hp, i: (0, 0, 0)))
        args.append(bias)
    any_spec = pl.BlockSpec(memory_space=pl.ANY)
    out_specs = [pl.BlockSpec((TQ, LANES), lambda b, hp, i: (b * nq + i, hp)),
                 pl.BlockSpec((TQ, 2 * LANES), lambda b, hp, i: (b * nq + i, hp))] + [any_spec] * n_r
    out_shape = [jax.ShapeDtypeStruct((T, WA), F32), jax.ShapeDtypeStruct((T, 2 * WA), F32)]
    out_shape += [jax.ShapeDtypeStruct((NCHIP - 1,) + v.shape, v.dtype) for v in ride]
    sems = [pltpu.SemaphoreType.DMA((6 * n_r,)), pltpu.SemaphoreType.DMA((6 * n_r,))] if n_r else []
    sem = ("arbitrary",) * 3 if n_r else ("parallel", "parallel", "arbitrary")
    return pl.pallas_call(
        body, name=name, grid=(nbl, hp_n, nq), in_specs=in_specs + [any_spec] * n_r, out_specs=out_specs,
        out_shape=out_shape, scratch_shapes=sems, compiler_params=_cparams(sem),
    )(*args, *ride)


def _attn_bwd(name, qkv, secs, o, do, do_sec, lse, fox, eq=None, ek=None, bias=None, ride=None):
    T = qkv.shape[0]
    nq = S // TQ
    nbl = T // S
    hp_n = WA // LANES
    sq, sk, sv = (s * hp_n for s in secs)
    dsec = do_sec * hp_n
    scale = HD ** -0.5
    nd = None if fox else bias.shape[0]
    kc = 2 * LANES if fox else LANES

    r_in, r_out, r_copies = ride if ride else ((), (), ())
    n_bi, n_bo = (8, 5) if fox else (7, 3)
    n_i = n_bi + len(r_in)

    def body(*refs):
        if fox:
            q_ref, k_ref, v_ref, o_ref, do_ref, lse_ref, eq_ref, ek_ref = refs[:n_bi]
            dq_ref, dk_ref, dv_ref, dqe_ref, dek_ref = refs[n_i:n_i + n_bo]
        else:
            q_ref, k_ref, v_ref, o_ref, do_ref, lse_ref, b_ref = refs[:n_bi]
            dq_ref, dk_ref, dv_ref = refs[n_i:n_i + n_bo]
        dl_ref = refs[n_i + n_bo + len(r_out)]
        j = pl.program_id(2)
        if ride:
            ride_start, ride_finish = _remote_steps(
                refs[n_bi:n_i], refs[n_i + n_bo:n_i + n_bo + len(r_out)], *refs[n_i + n_bo + len(r_out) + 1:], r_copies)
            at = lambda b, hp, q: (pl.program_id(0) == b) & (pl.program_id(1) == hp) & (j == q)
            pl.when(at(0, 0, 0))(ride_start)
        lane = lax.broadcasted_iota(jnp.int32, (1, LANES), 1)
        h0 = lane < HD

        @pl.when(j == 0)
        def _():
            dq_ref[...] = jnp.zeros_like(dq_ref)
            if fox:
                dqe_ref[...] = jnp.zeros_like(dqe_ref)

            def dl_step(r, c):
                off = pl.multiple_of(r * TQ, TQ)
                d2 = do_ref[pl.ds(off, TQ), :] * o_ref[pl.ds(off, TQ), :]
                z2 = jnp.zeros_like(d2)
                dl0 = jnp.sum(jnp.where(h0, d2, z2), axis=1, keepdims=True)
                dl1 = jnp.sum(jnp.where(h0, z2, d2), axis=1, keepdims=True)
                dl_ref[pl.ds(off, TQ), :] = jnp.concatenate(
                    [jnp.broadcast_to(dl0, (TQ, LANES)), jnp.broadcast_to(dl1, (TQ, LANES))], axis=1)
                return c

            lax.fori_loop(0, nq, dl_step, 0)

        kk = k_ref[...]
        if fox:
            kk = jnp.concatenate([kk, ek_ref[...]], axis=1)
        v2 = v_ref[...]

        def wide(x2):
            st = jnp.concatenate([x2[:, :LANES], x2[:, LANES:]], axis=0)
            return st if TQ == LANES else jnp.concatenate([st] * (TQ // LANES), axis=1)

        def step(t, carry, diag):
            dkk, dv2 = carry
            off = pl.multiple_of((j + t) * TQ, TQ)
            q2 = (q_ref[pl.ds(off, TQ), :].astype(F32) * scale).astype(BF16)
            qs = _stack_heads(q2, h0, eq_ref[pl.ds(off, TQ), :] if fox else None)
            dos = _stack_heads(do_ref[pl.ds(off, TQ), :].astype(BF16), h0)
            s = jnp.concatenate([_dot_nt(qs[:TQ], kk), _dot_nt(qs[TQ:], kk)], axis=0)
            if not fox:
                s = (s.reshape(2, TQ, TQ) + b_ref[t]).reshape(2 * TQ, TQ)
            elif diag:
                rows = lax.broadcasted_iota(jnp.int32, (2, TQ, TQ), 1).reshape(2 * TQ, TQ)
                cols = lax.broadcasted_iota(jnp.int32, (2 * TQ, TQ), 1)
                s = jnp.where(cols <= rows, s, NEG)
            p = jnp.exp(s - wide(lse_ref[pl.ds(off, TQ), :]))
            dp = jnp.concatenate([_dot_nt(dos[:TQ], v2), _dot_nt(dos[TQ:], v2)], axis=0)
            dsb = (p * (dp - wide(dl_ref[pl.ds(off, TQ), :]))).astype(BF16)
            dv2 = dv2 + _dot_tn(p.astype(BF16), dos)
            dkk = dkk + _dot_tn(dsb, qs)
            dqq = jnp.concatenate([_dot(dsb[:TQ], kk), _dot(dsb[TQ:], kk)], axis=0)
            dq_ref[pl.ds(off, TQ), :] += jnp.where(h0, dqq[:TQ, :LANES], dqq[TQ:, :LANES])
            if fox:
                dqe_ref[pl.ds(off, TQ), :] += jnp.where(lane < SUBLANES, dqq[:TQ, LANES:], dqq[TQ:, LANES:])
            return dkk, dv2

        zero = (jnp.zeros((TQ, kc), F32), jnp.zeros((TQ, LANES), F32))
        if fox:
            dkk, dv2 = lax.fori_loop(1, nq - j, lambda t, c: step(t, c, False), step(0, zero, True))
        else:
            dkk, dv2 = lax.fori_loop(0, jnp.minimum(nq - j, nd), lambda t, c: step(t, c, False), zero)
        dk_ref[...] = dkk[:, :LANES]
        dv_ref[...] = dv2
        if fox:
            dek_ref[...] = dkk[:, LANES:]

        @pl.when(j == nq - 1)
        def _():
            dq_ref[...] = dq_ref[...] * scale

        if ride:
            pl.when(at(nbl - 1, hp_n - 1, nq - 1))(ride_finish)

    seq = lambda c, w=LANES: pl.BlockSpec((S, w), lambda b, hp, j: (b, c + hp))
    blk = lambda c: pl.BlockSpec((TQ, LANES), lambda b, hp, j: (b * nq + j, c + hp))
    in_specs = [seq(sq), blk(sk), blk(sv), seq(0), seq(dsec), seq(0, 2 * LANES)]
    args = [qkv, qkv, qkv, o, do, lse]
    if fox:
        in_specs += [seq(0, 2 * LANES), blk(0)]
        args += [eq, ek]
    else:
        in_specs.append(pl.BlockSpec(bias.shape, lambda b, hp, j: (0, 0, 0)))
        args.append(bias)
    out_specs = [seq(0), blk(0), blk(0)]
    out_shape = [jax.ShapeDtypeStruct((T, WA), F32)] * 3
    if fox:
        out_specs += [seq(0), blk(0)]
        out_shape += [jax.ShapeDtypeStruct((T, WA), F32)] * 2
    any_spec = pl.BlockSpec(memory_space=pl.ANY)
    scratch = [pltpu.VMEM((S, 2 * LANES), F32)]
    if ride:
        scratch += [pltpu.SemaphoreType.DMA((len(r_copies),)), pltpu.SemaphoreType.DMA((len(r_copies),))]
    sem = ("arbitrary",) * 3 if ride else ("parallel", "parallel", "arbitrary")
    return pl.pallas_call(
        body, name=name, grid=(nbl, hp_n, nq), in_specs=in_specs + [any_spec] * len(r_in),
        out_specs=out_specs + [any_spec] * len(r_out), out_shape=out_shape + list(r_out),
        scratch_shapes=scratch, compiler_params=_cparams(sem),
    )(*args, *r_in)


def _remote_steps(in_refs, out_refs, send_sems, recv_sems, remote):
    me = (lax.axis_index("x"), lax.axis_index("y"), lax.axis_index("c"))

    def peer_of(flip):
        return tuple(1 - v if f else v for v, f in zip(me, flip))

    def at(ref, idx):
        return ref if idx is None else ref.at[idx]

    def rcopy(k, who):
        flip, a, sfn, b, dfn = remote[k]
        return pltpu.make_async_remote_copy(
            src_ref=at(in_refs[a], sfn(*who)), dst_ref=at(out_refs[b], dfn(*who)),
            send_sem=send_sems.at[k], recv_sem=recv_sems.at[k], device_id=peer_of(flip), device_id_type=MESH)

    def start():
        for k in range(len(remote)):
            rcopy(k, me).start()

    def finish():
        for k in range(len(remote)):
            rcopy(k, peer_of(remote[k][0])).wait_recv()
        for k in range(len(remote)):
            rcopy(k, me).wait_send()

    return start, finish


def _exchange(name, ins, out_shapes, remote, local):
    n_in, n_out = len(ins), len(out_shapes)
    nr, nl = len(remote), len(local)

    def body(*refs):
        in_refs = refs[:n_in]
        out_refs = refs[n_in:n_in + n_out]
        send_sems, recv_sems, loc_sems = refs[n_in + n_out:]
        me = (lax.axis_index("x"), lax.axis_index("y"), lax.axis_index("c"))
        at = lambda ref, idx: ref if idx is None else ref.at[idx]
        locs = [pltpu.make_async_copy(at(in_refs[a], sfn(*me)), at(out_refs[b], dfn(*me)), loc_sems.at[k])
                for k, (a, sfn, b, dfn) in enumerate(local)]
        for cp in locs:
            cp.start()
        start, finish = _remote_steps(in_refs, out_refs, send_sems, recv_sems, remote)
        start()
        finish()
        for cp in locs:
            cp.wait()

    any_spec = pl.BlockSpec(memory_space=pl.ANY)
    return pl.pallas_call(
        body, name=name, in_specs=[any_spec] * n_in, out_specs=[any_spec] * n_out, out_shape=list(out_shapes),
        scratch_shapes=[pltpu.SemaphoreType.DMA((max(nr, 1),)), pltpu.SemaphoreType.DMA((max(nr, 1),)),
                        pltpu.SemaphoreType.DMA((max(nl, 1),))],
    )(*ins)


_FLIPS7 = [(0, 0, 1), (0, 1, 0), (0, 1, 1), (1, 0, 0), (1, 0, 1), (1, 1, 0), (1, 1, 1)]
_CHIP_FLIPS = [(1, 0, 0), (0, 1, 0), (1, 1, 0)]


def _dev_index(x, y, c):
    return 4 * x + 2 * y + c


def _chip_index(x, y, c):
    return 2 * x + y


def _all_gather8(name, v):
    remote = [(f, 0, lambda x, y, c: None, 0, _dev_index) for f in _FLIPS7]
    local = [(0, lambda x, y, c: None, 0, _dev_index)]
    return _exchange(name, [v], [jax.ShapeDtypeStruct((NDEV,) + v.shape, v.dtype)], remote, local)[0]


def _gather_halves_steps(in_refs, out_refs, send_sems, recv_sems):
    n_v = len(in_refs)
    x, y, c = lax.axis_index("x"), lax.axis_index("y"), lax.axis_index("c")
    sibling = (x, y, 1 - c)
    chips = [(1 - x, y), (x, 1 - y), (1 - x, 1 - y)]

    def copy(k, n, src, blk, half, to):
        return pltpu.make_async_remote_copy(
            src_ref=src, dst_ref=out_refs[n].at[blk, half], send_sem=send_sems.at[k], recv_sem=recv_sems.at[k],
            device_id=to, device_id_type=MESH)

    def first():
        return [copy(6 * n + j, n, in_refs[n].at[c], j, c, (*chip, c))
                for n in range(n_v) for j, chip in enumerate(chips)]

    def start():
        for cp in first():
            cp.start()

    def finish():
        passed = []
        for n in range(n_v):
            for j, chip in enumerate(chips):
                copy(6 * n + j, n, in_refs[n].at[c], j, c, (*chip, c)).wait_recv()
                fw = copy(6 * n + 3 + j, n, out_refs[n].at[j, c], j, c, sibling)
                fw.start()
                passed.append(fw)
        for n in range(n_v):
            for j in range(len(chips)):
                copy(6 * n + 3 + j, n, out_refs[n].at[j, 1 - c], j, 1 - c, sibling).wait_recv()
        for cp in first() + passed:
            cp.wait_send()

    return start, finish


def _gather_halves(name, vs):
    n_v = len(vs)

    def body(*refs):
        start, finish = _gather_halves_steps(refs[:n_v], refs[n_v:2 * n_v], *refs[2 * n_v:])
        start()
        finish()

    any_spec = pl.BlockSpec(memory_space=pl.ANY)
    return pl.pallas_call(
        body, name=name, in_specs=[any_spec] * n_v, out_specs=[any_spec] * n_v,
        out_shape=[jax.ShapeDtypeStruct((NCHIP - 1,) + v.shape, v.dtype) for v in vs],
        scratch_shapes=[pltpu.SemaphoreType.DMA((6 * n_v,)), pltpu.SemaphoreType.DMA((6 * n_v,))],
    )(*vs)


def _by_chip(own, others, chip):
    stacked = jnp.concatenate([own[None], others], axis=0)
    blocks = []
    for k in range(NCHIP):
        d = k ^ chip
        place = jnp.where(d == 0, 0, jnp.where(d == 2, 1, jnp.where(d == 1, 2, 3)))
        blocks.append(lax.dynamic_index_in_dim(stacked, place, axis=0, keepdims=False))
    return blocks


def _sum_leading(name, v, tm=None):
    n, r, w = v.shape
    tm = _pick(r, (256, 128, 64, 32, 16, 8)) if tm is None else tm

    def body(v_ref, o_ref):
        acc = v_ref[0].astype(F32)
        for k in range(1, n):
            acc = acc + v_ref[k].astype(F32)
        o_ref[...] = acc

    return pl.pallas_call(
        body, name=name, grid=(r // tm,), in_specs=[pl.BlockSpec((n, tm, w), lambda i: (0, i, 0))],
        out_specs=pl.BlockSpec((tm, w), lambda i: (i, 0)), out_shape=jax.ShapeDtypeStruct((r, w), F32),
        compiler_params=_cparams(("parallel",)),
    )(v)


def _add2(name, a, b, tm=None, out_dtype=F32):
    r, w = a.shape
    tm = _pick(r, (256, 128, 64, 32, 16, 8)) if tm is None else tm

    def body(a_ref, b_ref, o_ref):
        o_ref[...] = (a_ref[...] + b_ref[...]).astype(out_dtype)

    spec = pl.BlockSpec((tm, w), lambda i: (i, 0))
    return pl.pallas_call(
        body, name=name, grid=(r // tm,), in_specs=[spec, spec], out_specs=spec,
        out_shape=jax.ShapeDtypeStruct((r, w), out_dtype), compiler_params=_cparams(("parallel",)),
    )(a, b)


def _ada_fwd(call_all, w_shard):
    def body(c_ref, w_ref, o_ref):
        cv = c_ref[...]
        o_ref[...] = jnp.dot(cv * _sigmoid(cv), w_ref[...], preferred_element_type=F32,
                             precision=lax.Precision.HIGHEST)

    n = w_shard.shape[1]
    return pl.pallas_call(
        body, name="ada_fwd", out_shape=jax.ShapeDtypeStruct((call_all.shape[0], n), F32),
        compiler_params=pltpu.CompilerParams(vmem_limit_bytes=VMEM_LIMIT),
    )(call_all, w_shard)


def _ada_bwd(call_all, dada):
    def body(c_ref, d_ref, o_ref):
        cv = c_ref[...]
        o_ref[...] = lax.dot_general(cv * _sigmoid(cv), d_ref[...], (((0,), (0,)), ((), ())),
                                     preferred_element_type=F32, precision=lax.Precision.HIGHEST)

    return pl.pallas_call(
        body, name="ada_bwd", out_shape=jax.ShapeDtypeStruct((call_all.shape[1], dada.shape[1]), F32),
        compiler_params=pltpu.CompilerParams(vmem_limit_bytes=VMEM_LIMIT),
    )(call_all, dada)


def _adamw(name, w, g, m, v):
    r, wd = w.shape
    tm = _pick(r, (256, 128, 64, 32, 16, 8))
    bc1 = 1.0 - ADAM_B1 ** ADAM_STEP
    bc2 = 1.0 - ADAM_B2 ** ADAM_STEP

    def body(w_ref, g_ref, m_ref, v_ref, d_ref, mo_ref, vo_ref):
        gv = g_ref[...]
        mn = ADAM_B1 * m_ref[...] + (1.0 - ADAM_B1) * gv
        vn = ADAM_B2 * v_ref[...] + (1.0 - ADAM_B2) * (gv * gv)
        d_ref[...] = -ADAM_LR * ((mn / bc1) / (jnp.sqrt(vn / bc2) + ADAM_EPS) + ADAM_WD * w_ref[...])
        mo_ref[...] = mn
        vo_ref[...] = vn

    spec = pl.BlockSpec((tm, wd), lambda i: (i, 0))
    return pl.pallas_call(
        body, name=name, grid=(r // tm,), in_specs=[spec] * 4, out_specs=[spec] * 3,
        out_shape=[jax.ShapeDtypeStruct((r, wd), F32)] * 3, compiler_params=_cparams(("parallel",)),
    )(w, g, m, v)


def _rope_tables(positions):
    half = ROPE_DIMS // 2
    freqs = ROPE_THETA ** (-jnp.arange(0, ROPE_DIMS, 2, dtype=F32) / ROPE_DIMS)
    ang = positions.astype(F32).reshape(-1, 1) * freqs
    cos, sin = jnp.cos(ang), jnp.sin(ang)
    T = ang.shape[0]
    one = jnp.ones((T, HD - ROPE_DIMS), F32)
    zero = jnp.zeros((T, HD - ROPE_DIMS), F32)
    zh = jnp.zeros((T, half), F32)
    c64 = jnp.concatenate([cos, cos, one], axis=1)
    s1 = jnp.concatenate([zh, sin, zero], axis=1)
    s2 = jnp.concatenate([-sin, zh, zero], axis=1)
    rep = lambda t: jnp.concatenate([t] * (LANES // HD), axis=1)
    return rep(c64), rep(s1), rep(s2)


def _local_step(x, loss_target, positions, ada, w_qkv, w_f, w_out, w_up, conv_w8, w_down,
                b_fgate, gn, ln1_g, ln1_b, conv_b, ln2_g, ln2_b, late=None, early=None, last=None):
    T = x.shape[0]
    nbl = T // S
    nha = WA // HD
    sv = lambda k: ada[:, k:k + 1, :]
    sh_a, sc_a, g_a, sh_f, sc_f, g_f = (sv(k) for k in range(6))
    rope = _rope_tables(positions)
    neg_rope = (rope[0], -rope[1], -rope[2])
    gseg = jnp.asarray(np.kron(np.eye(min(256, 2 * WA) // HD), np.ones((HD, HD))), BF16)
    bias = jnp.asarray(_dil_bias(TQ))
    bf_pad = jnp.zeros((1, FPAD), F32).at[:, :nha].set(b_fgate)

    qkv, fa = _mod_mm("qkv_proj", x, sc_a, sh_a, (w_qkv, w_f), (BF16, F32), rope=rope, rope_secs=(3, 4))
    pq, pk, oq, ok, sq, sk = _fold_tables(nha)

    def fold_out(cum, f, b_ref, pq_ref, pk_ref, oq_ref, ok_ref):
        hi, mid, lo = _split3(cum)
        eqv = _dot(hi, pq_ref[0]) + _dot(mid, pq_ref[1]) + _dot(lo, pq_ref[2]) + oq_ref[...]
        ekv = ok_ref[...] - (_dot(hi, pk_ref[0]) + _dot(mid, pk_ref[1]) + _dot(lo, pk_ref[2]))
        return eqv, ekv

    eq, ek = _cumsum_seq(
        "fgate_fwd", [fa], [bf_pad, jnp.asarray(pq, BF16), jnp.asarray(pk, BF16), jnp.asarray(oq), jnp.asarray(ok)],
        lambda f, b_ref, *_: _log_sigmoid(f + b_ref[...]), fold_out, ((2 * WA, BF16), (WA, BF16)), reverse=False)
    oa, lse_a, *got = _attn_fwd("fox_fwd", qkv, (0, 1, 2), True, eq=eq, ek=ek, ride=late[0] if late else ())
    if late:
        w_up, w_down = late[1](got)
    ob, lse_b = _attn_fwd("dil_fwd", qkv, (3, 4, 5), False, bias=bias)

    def mix_fn(i, oav, obv, xv, gav, gn_ref, g_ref, wo_ref, l1g_ref, l1b_ref):
        o = jnp.concatenate([oav, obv], axis=1)
        rs = lax.rsqrt(_head_mean(o * o, g_ref) + RMS_EPS)
        merged = (o * rs * gn_ref[...]).astype(BF16)
        mix = _dot(merged, wo_ref[...])
        x1, _, _ = _ln_fwd(ALPHA * xv + gav * mix, l1g_ref[...], l1b_ref[...])
        return merged, mix, x1

    merged, mix, x1 = _rowwise("mix_out", mix_fn, T, 256, tiles=(oa, ob, x), seqvecs=(g_a,),
                               consts=(gn, gseg, w_out, ln1_g, ln1_b),
                               outs=((2 * WA, BF16), (D, F32), (D, F32)))
    u = _mod_mm("ffn_up", x1, sc_f, sh_f, (w_up,), (F32,))[0]

    def conv_y(i, uv, prev, cw_ref, cb_ref, tm):
        first = (i * tm) % S == 0
        s1, s2 = _conv_taps(uv, prev, first)
        y = cb_ref[...] + cw_ref[0:1, :] * s2 + cw_ref[1:2, :] * s1 + cw_ref[2:3, :] * uv
        return y, s1, s2

    tmc = 128

    def gate_fn(i, uv, prev, cw_ref, cb_ref):
        y, _, _ = conv_y(i, uv, prev, cw_ref, cb_ref, tmc)
        a, g = y[:, :DFF], y[:, DFF:]
        return g * _sigmoid(g) * a, y

    act, yconv = _rowwise("conv_gate", gate_fn, T, tmc, tiles=(u,), halos=((u, -1),), consts=(conv_w8, conv_b),
                          outs=((DFF, BF16), (2 * DFF, F32)))

    def down_fn(i, actv, x1v, tgt, gfv, wd_ref, g2_ref, b2_ref):
        ffn = _dot(actv, wd_ref[...])
        y, n2, rstd = _ln_fwd(ALPHA * x1v + gfv * ffn, g2_ref[...], b2_ref[...])
        err = y - tgt
        dy = err * (1.0 / D)
        dr2 = _ln_bwd(dy, n2, rstd, g2_ref[...])
        return (dr2, gfv * dr2, _rsum8(err * err), _rsum8(dy * n2), _rsum8(dy), _rsum8(dr2 * ffn))

    dr2, dffn, loss_acc, d_ln2g, d_ln2b, d_gf = _rowwise(
        "ffn_down_loss", down_fn, T, 256, tiles=(act, x1, loss_target), seqvecs=(g_f,),
        consts=(w_down, ln2_g, ln2_b), outs=((D, F32), (D, F32)), accs=(D, D, D), seqaccs=(D,))

    def gate_conv_bwd_fn(i, uv, yv, dfv, y_nxt, df_nxt, cw_ref, wd_ref):
        last = ((i + 1) * tmc) % S == 0
        y = jnp.concatenate([yv, y_nxt], axis=0)
        df_ext = jnp.concatenate([dfv, df_nxt], axis=0).astype(BF16)
        ch = _pick(DFF, (256, 128))
        dav = jnp.concatenate([_dot_nt(df_ext, wd_ref[c * ch:(c + 1) * ch, :]) for c in range(DFF // ch)], axis=1)
        a, g = y[:, :DFF], y[:, DFF:]
        sg = _sigmoid(g)
        dyc_ext = jnp.concatenate([dav * (g * sg), dav * a * (sg * (1.0 + g * (1.0 - sg)))], axis=1)
        dyc = dyc_ext[:tmc]
        u1, u2 = _conv_taps_up(dyc, dyc_ext[tmc:], last)
        du_ = cw_ref[2:3, :] * dyc + cw_ref[1:2, :] * u1 + cw_ref[0:1, :] * u2
        return du_, _rsum8(dyc), _rsum8(uv * u2), _rsum8(uv * u1), _rsum8(uv * dyc)

    du, d_cb, d_cw0, d_cw1, d_cw2 = _rowwise(
        "gate_conv_bwd", gate_conv_bwd_fn, T, tmc, tiles=(u, yconv, dffn), halos=((yconv, 1), (dffn, 1)),
        consts=(conv_w8, w_down), outs=((2 * DFF, BF16),), accs=(2 * DFF,) * 4)
    g_w_down = _mm_tn("dw_down", act, dffn)
    g_w_up = _mm_tn("dw_up", x1, du, mod=(sc_f, sh_f), by_chip=True)

    def ln1_bwd_fn(i, dr2v, duv, xv, mixv, x1v, scfv, gav, l1g_ref, wu_ref):
        dh2v = _nt_rows(duv, wu_ref)
        dx1 = ALPHA * dr2v + dh2v * (1.0 + scfv)
        _, n1, rstd = _ln_fwd(ALPHA * xv + gav * mixv, l1g_ref[...], 0.0)
        dr1 = _ln_bwd(dx1, n1, rstd, l1g_ref[...])
        return (dr1, gav * dr1, _rsum8(dx1 * n1), _rsum8(dx1),
                _rsum8(dh2v * x1v), _rsum8(dh2v), _rsum8(dr1 * mixv))

    dr1, dmix, d_ln1g, d_ln1b, d_scf, d_shf, d_ga = _rowwise(
        "ln1_bwd", ln1_bwd_fn, T, 256, tiles=(dr2, du, x, mix, x1), seqvecs=(sc_f, g_a), consts=(ln1_g, w_up),
        outs=((D, F32), (D, BF16)), accs=(D, D), seqaccs=(D, D, D))

    g_w_out = _mm_tn("dw_out", merged, dmix)

    def hn_bwd_fn(i, dmixv, oav, obv, gn_ref, g_ref, wo_ref):
        dmv = _nt_rows(dmixv, wo_ref)
        o = jnp.concatenate([oav, obv], axis=1)
        rs = lax.rsqrt(_head_mean(o * o, g_ref) + RMS_EPS)
        nrm = o * rs
        dn = dmv * gn_ref[...]
        do = rs * (dn - nrm * _head_mean(dn * nrm, g_ref))
        return do, _rsum8(dmv * nrm)

    do, d_gn = _rowwise("headnorm_bwd", hn_bwd_fn, T, 256, tiles=(dmix, oa, ob), consts=(gn, gseg, w_out),
                        outs=((2 * WA, F32),), accs=(2 * WA,))
    d_cw = jnp.stack([d_cw0[0], d_cw1[0], d_cw2[0]], axis=0)
    dqa, dka, dva, dqe, dek, *arrived = _attn_bwd(
        "fox_bwd", qkv, (0, 1, 2), oa, do, 0, lse_a, True, eq=eq, ek=ek,
        ride=early(g_w_up, g_w_down, d_cw) if early else None)
    dqb, dkb, dvb = _attn_bwd("dil_bwd", qkv, (3, 4, 5), ob, do, 1, lse_b, False, bias=bias)
    hdot = lambda a, m_ref: sum(_dot(piece, m_ref[...]) for piece in _split3(a))
    dfa, d_bf = _cumsum_seq(
        "fgate_bwd", [dqe, dek, fa], [bf_pad, jnp.asarray(sq, BF16), jnp.asarray(sk, BF16)],
        lambda dq_, dk_, f, b_ref, sq_ref, sk_ref: hdot(dq_, sq_ref) - hdot(dk_, sk_ref),
        lambda cum, dq_, dk_, f, b_ref, sq_ref, sk_ref: (cum * _sigmoid(-(f + b_ref[...])),),
        ((FPAD, F32),), reverse=True, n_acc=1)

    def dz_fn(i, a0, a1, a2, b0, b1, b2, fv, cv, s1v, s2v):
        ct, s1t, s2t = (_tile_lanes(t, WA) for t in (cv, s1v, s2v))
        return jnp.concatenate([a0, a1, a2, _rope(b0, ct, s1t, s2t), _rope(b1, ct, s1t, s2t), b2, fv], axis=1)

    dz = _rowwise("dz_pack", dz_fn, T, 256, tiles=(dqa, dka, dva, dqb, dkb, dvb, dfa) + neg_rope,
                  outs=((6 * WA + FPAD, BF16),))[0]
    w_cat = jnp.concatenate([w_qkv, w_f], axis=1)
    g_w_cat = _mm_tn("dw_in", x, dz, mod=(sc_a, sh_a), tt=256, t2=dz.shape[1])

    def dx_fn(i, dr1v, dzv, xv, scav, wc_ref):
        dh1v = _nt_rows(dzv, wc_ref)
        return ALPHA * dr1v + dh1v * (1.0 + scav), _rsum8(dh1v * xv), _rsum8(dh1v)

    grad_x, d_sca, d_sha, *arrived_last = _rowwise(
        "dx_out", dx_fn, T, 256, tiles=(dr1, dz, x), seqvecs=(sc_a,), consts=(w_cat,), outs=((D, F32),),
        seqaccs=(D, D), ride=last(g_w_cat, g_w_out) if last else None)

    row0 = lambda a: a[..., 0, :]
    d_ada = jnp.stack([row0(d_sha), row0(d_sca), row0(d_ga), row0(d_shf), row0(d_scf), row0(d_gf)], axis=1)
    loss_part = (0.5 / D) * jnp.sum(loss_acc[0])
    small = dict(b_fgate=row0(d_bf)[:nha], gn=row0(d_gn), ln1_g=row0(d_ln1g), ln1_b=row0(d_ln1b),
                 conv_b=row0(d_cb), ln2_g=row0(d_ln2g), ln2_b=row0(d_ln2b))
    big = dict(w_cat=g_w_cat, w_out=g_w_out, w_up=g_w_up, conv_w=d_cw, w_down=g_w_down, early=arrived, last=arrived_last)
    return loss_part, grad_x, d_ada, small, big


def _rows_of(n, w=None):
    return -(-n // (D if w is None else w))


def _as_rows(v):
    w = D
    k = v.shape[0]
    flat = v.reshape(k, -1)
    rows = _rows_of(_rows_of(flat.shape[1], w), SUBLANES) * SUBLANES
    flat = jnp.pad(flat, ((0, 0), (0, rows * w - flat.shape[1])))
    return flat.reshape(k, rows, w)


def kernel(x, c, positions, w_ada, b_ada, w_in, b_fgate, gn_a, gn_b, w_out, ln1_g, ln1_b, w_up, conv_w, conv_b, w_down, ln2_g, ln2_b, loss_target, m_w_ada, m_b_ada, m_w_in, m_b_fgate, m_gn_a, m_gn_b, m_w_out, m_ln1_g, m_ln1_b, m_w_up, m_conv_w, m_conv_b, m_w_down, m_ln2_g, m_ln2_b, v_w_ada, v_b_ada, v_w_in, v_b_fgate, v_gn_a, v_gn_b, v_w_out, v_ln1_g, v_ln1_b, v_w_up, v_conv_w, v_conv_b, v_w_down, v_ln2_g, v_ln2_b):
    mx, my, mc = lax.axis_index("x"), lax.axis_index("y"), lax.axis_index("c")
    dev = _dev_index(mx, my, mc)
    chip = _chip_index(mx, my, mc)
    nbl = x.shape[0]
    T = nbl * S
    nha = WA // HD
    n_ada = w_ada.shape[2]

    c_pad = jnp.zeros((SUBLANES, D), F32).at[:nbl].set(c)
    c_all = _all_gather8("gather_c", c_pad)[:, :nbl].reshape(NDEV * nbl, D)
    ada_part = _ada_fwd(c_all, w_ada[0])
    n_cw = conv_w.shape[2]
    cw_rows = jnp.pad(conv_w[0], ((0, SUBLANES - conv_w.shape[1]), (0, n_ada - n_cw)))
    ada_blocks = _all_gather8("gather_ada", jnp.concatenate([ada_part, cw_rows], axis=0))
    n_c = NDEV * nbl
    ada_all = jnp.concatenate([ada_blocks[2 * k, :n_c] for k in range(NCHIP)], axis=1) + b_ada
    conv_w8 = jnp.concatenate([ada_blocks[2 * k, n_c:, :n_cw] for k in range(NCHIP)], axis=1)
    ada = lax.dynamic_slice_in_dim(ada_all, dev * nbl, nbl, axis=0).reshape(nbl, 6, D)

    w_in_sh = jnp.pad(w_in[0].astype(BF16), ((0, 0), (0, _rows_of(w_in.shape[2], LANES) * LANES - w_in.shape[2])))
    halve = lambda t: t.reshape(2, t.shape[0] // 2, t.shape[1])
    whole = lambda g, t: jnp.stack(_by_chip(halve(t), g, chip)).reshape((NCHIP,) + t.shape)
    w_out_sh, w_up_sh, w_down_sh = w_out[0].astype(BF16), w_up[0].astype(BF16), w_down[0].astype(BF16)
    g_in, g_out = _gather_halves("gather_w", [halve(w_in_sh), halve(w_out_sh)])
    g_in, g_out = whole(g_in, w_in_sh), whole(g_out, w_out_sh)
    w_in_full = jnp.concatenate([g_in[k][:, :w_in.shape[2]] for k in range(NCHIP)], axis=1)
    w_qkv = jnp.concatenate([w_in_full[:, :3 * WA], w_in_full[:, 3 * WA + nha:]], axis=1)
    w_f = jnp.pad(w_in_full[:, 3 * WA:3 * WA + nha], ((0, 0), (0, FPAD - nha)))
    w_out_full = g_out.reshape(NCHIP * w_out.shape[1], D)

    def late_weights(got):
        g_up, g_down = whole(got[0], w_up_sh), whole(got[1], w_down_sh)
        return (jnp.concatenate([g_up[k] for k in range(NCHIP)], axis=1),
                g_down.reshape(NCHIP * w_down.shape[1], D))

    n_in = w_in.shape[2]
    to_sib = lambda x, y, c: None

    def reduce_front(blocks, tag):
        cut = [t.reshape(2 * NCHIP, t.shape[1] // 2, t.shape[2]) for t in blocks]
        from_sib = _exchange(
            "pair_swap_" + tag, cut, [jax.ShapeDtypeStruct((NCHIP,) + t.shape[1:], F32) for t in cut],
            [((0, 0, 1), n, lambda x, y, c, k=k: 2 * k + 1 - c, n, lambda x, y, c, k=k: k)
             for n in range(len(cut)) for k in range(NCHIP)], [])
        flat = lambda v: v.reshape(-1, v.shape[-1])
        pair_sums = []
        for n, (t, fs) in enumerate(zip(cut, from_sib)):
            mine = lax.dynamic_index_in_dim(t.reshape((NCHIP, 2) + t.shape[1:]), mc, axis=1, keepdims=False)
            pair_sums.append(_add2("pair_sum_%s%d" % (tag, n), flat(mine), flat(fs), out_dtype=BF16).reshape(fs.shape))
        copies = []
        for n in range(len(cut)):
            for j, f in enumerate(_CHIP_FLIPS):
                src = lambda x, y, c, f=f: _chip_index(1 - x if f[0] else x, 1 - y if f[1] else y, c)
                copies.append((f, n, src, n, lambda x, y, c, j=j: j))
        shapes = [jax.ShapeDtypeStruct((NCHIP - 1,) + t.shape[1:], BF16) for t in pair_sums]
        return pair_sums, (pair_sums, shapes, copies)

    def reduce_back(pair_sums, arrived, tag):
        my_halves = []
        for n, (ps, got) in enumerate(zip(pair_sums, arrived)):
            own = lax.dynamic_index_in_dim(ps, chip, axis=0, keepdims=True)
            my_halves.append(_sum_leading("chip_sum_%s%d" % (tag, n), jnp.concatenate([own, got], axis=0)))
        sib_halves = _exchange("pair_share_" + tag, my_halves, [jax.ShapeDtypeStruct(t.shape, F32) for t in my_halves],
                               [((0, 0, 1), n, to_sib, n, to_sib) for n in range(len(my_halves))], [])
        whole_blocks = []
        for mh, shf in zip(my_halves, sib_halves):
            pair = jnp.stack([mh, shf])
            whole_blocks.append(jnp.concatenate([lax.dynamic_index_in_dim(pair, mc, axis=0, keepdims=False),
                                                 lax.dynamic_index_in_dim(pair, 1 - mc, axis=0, keepdims=False)], axis=0))
        return whole_blocks

    stash = {}

    def early_reduce(g_up, g_down, g_cw):
        sh_cw = _as_rows(g_cw.reshape(conv_w.shape[1], NCHIP, -1).transpose(1, 0, 2))
        rows = w_down.shape[1] + sh_cw.shape[1]
        pad = _rows_of(rows, 2 * LANES) * 2 * LANES - rows
        sh_a = jnp.concatenate([g_down.reshape(NCHIP, -1, D), sh_cw, jnp.zeros((NCHIP, pad, D), F32)], axis=1)
        stash["ps"], ride = reduce_front([sh_a, g_up], "e")
        return ride

    def last_reduce(g_cat, g_out):
        g_w_in_full = jnp.concatenate([g_cat[:, :3 * WA], g_cat[:, 6 * WA:6 * WA + nha], g_cat[:, 3 * WA:6 * WA]],
                                      axis=1)
        sh_in = jnp.pad(g_w_in_full.reshape(D, NCHIP, n_in).transpose(1, 0, 2),
                        ((0, 0), (0, 0), (0, _rows_of(n_in, LANES) * LANES - n_in)))
        stash["ps_l"], ride = reduce_front([g_out.reshape(NCHIP, -1, D), sh_in], "l")
        return ride

    gn = jnp.concatenate([gn_a, gn_b], axis=1)
    loss_part, grad_x, d_ada, small, big = _local_step(
        x.reshape(T, D), loss_target.reshape(T, D), positions, ada, w_qkv, w_f, w_out_full, None, conv_w8,
        None, b_fgate, gn, ln1_g, ln1_b, conv_b, ln2_g, ln2_b, late=([halve(w_up_sh), halve(w_down_sh)], late_weights), early=early_reduce, last=last_reduce)

    def row_pad(v, rows):
        flat = v.reshape(-1)
        return jnp.pad(flat, (0, rows * D - flat.shape[0]))

    n_cb = _rows_of(2 * DFF)
    small_flat = jnp.concatenate([
        row_pad(small["b_fgate"], 1), row_pad(small["gn"], 1), row_pad(small["ln1_g"], 1),
        row_pad(small["ln1_b"], 1), row_pad(small["ln2_g"], 1), row_pad(small["ln2_b"], 1),
        row_pad(jnp.full((1,), loss_part, F32), 1), row_pad(small["conv_b"], n_cb)])
    n_small = _rows_of(small_flat.shape[0], SUBLANES * D) * SUBLANES
    small_rows = jnp.pad(small_flat, (0, n_small * D - small_flat.shape[0])).reshape(n_small, D)
    ada_rows = jnp.pad(d_ada.reshape(nbl, 6, D), ((0, 0), (0, SUBLANES - 6), (0, 0))).reshape(nbl * SUBLANES, D)
    gathered = _all_gather8("gather_small", jnp.concatenate([small_rows, ada_rows], axis=0))
    red = _sum_leading("sum_small", gathered, tm=SUBLANES)
    g_b_fgate = red[0:1, :nha]
    g_gn = red[1:2, :2 * WA]
    g_ln1_g, g_ln1_b, g_ln2_g, g_ln2_b = red[2:3], red[3:4], red[4:5], red[5:6]
    loss = red[6, 0]
    g_conv_b = red[7:7 + n_cb].reshape(1, -1)[:, :2 * DFF]
    g_b_ada = _add2("sum_b_ada", red[n_small:n_small + SUBLANES], red[n_small + SUBLANES:n_small + 2 * SUBLANES],
                    tm=SUBLANES)[:6].reshape(1, 6 * D)
    dada_all = gathered[:, n_small:].reshape(NDEV, nbl, SUBLANES, D)[:, :, :6].reshape(NDEV * nbl, 6 * D)
    g_w_ada = _ada_bwd(c_all, lax.dynamic_slice_in_dim(dada_all, chip * n_ada, n_ada, axis=1))

    shards_e = reduce_back(stash["ps"], big["early"], "e")
    shards_l = reduce_back(stash["ps_l"], big["last"], "l")
    r0 = w_down.shape[1]
    g_w_down = shards_e[0][:r0]
    g_conv_w = shards_e[0][r0:r0 + SUBLANES].reshape(-1)[:int(np.prod(conv_w.shape[1:]))].reshape(conv_w.shape[1:])
    g_w_up = shards_e[1]
    g_w_out = shards_l[0]
    g_w_in = shards_l[1][:, :n_in]

    grads = dict(w_ada=g_w_ada, b_ada=g_b_ada, w_in=g_w_in, b_fgate=g_b_fgate, gn_a=g_gn[:, :WA], gn_b=g_gn[:, WA:],
                 w_out=g_w_out, ln1_g=g_ln1_g, ln1_b=g_ln1_b, w_up=g_w_up, conv_w=g_conv_w, conv_b=g_conv_b,
                 w_down=g_w_down, ln2_g=g_ln2_g, ln2_b=g_ln2_b)
    weights = dict(w_ada=w_ada, b_ada=b_ada, w_in=w_in, b_fgate=b_fgate, gn_a=gn_a, gn_b=gn_b, w_out=w_out,
                   ln1_g=ln1_g, ln1_b=ln1_b, w_up=w_up, conv_w=conv_w, conv_b=conv_b, w_down=w_down,
                   ln2_g=ln2_g, ln2_b=ln2_b)
    ms = dict(w_ada=m_w_ada, b_ada=m_b_ada, w_in=m_w_in, b_fgate=m_b_fgate, gn_a=m_gn_a, gn_b=m_gn_b,
              w_out=m_w_out, ln1_g=m_ln1_g, ln1_b=m_ln1_b, w_up=m_w_up, conv_w=m_conv_w, conv_b=m_conv_b,
              w_down=m_w_down, ln2_g=m_ln2_g, ln2_b=m_ln2_b)
    vs = dict(w_ada=v_w_ada, b_ada=v_b_ada, w_in=v_w_in, b_fgate=v_b_fgate, gn_a=v_gn_a, gn_b=v_gn_b,
              w_out=v_w_out, ln1_g=v_ln1_g, ln1_b=v_ln1_b, w_up=v_w_up, conv_w=v_conv_w, conv_b=v_conv_b,
              w_down=v_w_down, ln2_g=v_ln2_g, ln2_b=v_ln2_b)
    names = list(weights)
    big_names = ("w_ada", "w_in", "w_out", "w_up", "w_down")
    delta, new_m, new_v = {}, {}, {}
    for n in big_names:
        shp = weights[n].shape
        d, m2, v2 = _adamw("adamw_" + n, weights[n][0], grads[n].reshape(shp[1:]), ms[n][0], vs[n][0])
        delta[n], new_m[n], new_v[n] = d.reshape(shp), m2.reshape(shp), v2.reshape(shp)
    small_names = [n for n in names if n not in big_names]

    def pack_small(src):
        flats = []
        for n in small_names:
            flat = src[n].reshape(-1)
            flats.append(jnp.pad(flat, (0, _rows_of(flat.shape[0]) * D - flat.shape[0])))
        allf = jnp.concatenate(flats)
        rows = _rows_of(allf.shape[0], SUBLANES * D) * SUBLANES
        return jnp.pad(allf, (0, rows * D - allf.shape[0])).reshape(rows, D)

    sd, sm, sv_ = _adamw("adamw_small", pack_small(weights), pack_small(grads), pack_small(ms), pack_small(vs))
    off = 0
    for n in small_names:
        shp = weights[n].shape
        cnt = int(np.prod(shp))
        r = _rows_of(cnt)
        for dst, src in ((delta, sd), (new_m, sm), (new_v, sv_)):
            dst[n] = src[off:off + r].reshape(-1)[:cnt].reshape(shp)
        off += r

    out_g = {n: grads[n].reshape(weights[n].shape) for n in names}
    return (loss, grad_x.reshape(x.shape), *[out_g[n] for n in names], *[delta[n] for n in names],
            *[new_m[n] for n in names], *[new_v[n] for n in names])
```

```python
import numpy as np
import jax
import jax.numpy as jnp
from jax import lax
from jax.experimental import pallas as pl
from jax.experimental.pallas import tpu as pltpu

F32 = jnp.float32
BF16 = jnp.bfloat16

D = 1024
S = 4096
HD = 64
WA = 512
DFF = 2816
NCHIP = 4
NDEV = 8
PATTERNS = ((128, 1), (512, 4), (2048, 16))
ROPE_THETA = 500000.0
ROPE_DIMS = HD // 4
ALPHA = (2.0 * 1) ** 0.25
LN_EPS = 1e-5
RMS_EPS = 1e-6
ADAM_LR = 0.001
ADAM_B1 = 0.9
ADAM_B2 = 0.999
ADAM_EPS = 1e-08
ADAM_WD = 0.01
ADAM_STEP = 10

LANES = 128
SUBLANES = 8
TQ = 512
FPAD = LANES
NEG = -1e30
VMEM_LIMIT = 56 * 1024 * 1024
MESH = pl.DeviceIdType.MESH


def _cparams(sem):
    return pltpu.CompilerParams(dimension_semantics=sem, vmem_limit_bytes=VMEM_LIMIT)


def _pick(n, cands):
    for c in cands:
        if n % c == 0:
            return c
    return n


def _rsum8(v):
    tm, w = v.shape
    return jnp.sum(v.reshape(tm // SUBLANES, SUBLANES, w), axis=0)


def _sigmoid(x):
    return 1.0 / (1.0 + jnp.exp(-x))


def _dot(a, b):
    return jnp.dot(a, b, preferred_element_type=F32)


def _dot_nt(a, b):
    return lax.dot_general(a, b, (((1,), (1,)), ((), ())), preferred_element_type=F32)


def _dot_tn(a, b):
    return lax.dot_general(a, b, (((0,), (0,)), ((), ())), preferred_element_type=F32)


def _rowwise(name, fn, T, tm, *, tiles=(), halos=(), seqvecs=(), consts=(), outs=(), accs=(), seqaccs=(),
             seq_len=None, ride=None):
    seq_len = S if seq_len is None else seq_len
    nb = T // tm
    spb = max(seq_len // tm, 1)
    nseq = max(T // seq_len, 1)
    n8 = T // SUBLANES
    r8 = tm // SUBLANES
    in_specs, args = [], []
    for a in tiles:
        in_specs.append(pl.BlockSpec((tm, a.shape[1]), lambda i: (i, 0)))
        args.append(a)
    for a, direction in halos:
        if direction < 0:
            idx = lambda i: (jnp.maximum(i * r8 - 1, 0), 0)
        else:
            idx = lambda i: (jnp.minimum((i + 1) * r8, n8 - 1), 0)
        in_specs.append(pl.BlockSpec((SUBLANES, a.shape[1]), idx))
        args.append(a)
    for a in seqvecs:
        in_specs.append(pl.BlockSpec((1, 1, a.shape[2]), lambda i: (i // spb, 0, 0)))
        args.append(a)
    for a in consts:
        in_specs.append(pl.BlockSpec(a.shape, lambda i, nd=a.ndim: (0,) * nd))
        args.append(a)
    out_shape, out_specs = [], []
    for w, dt in outs:
        out_shape.append(jax.ShapeDtypeStruct((T, w), dt))
        out_specs.append(pl.BlockSpec((tm, w), lambda i: (i, 0)))
    for w in accs:
        out_shape.append(jax.ShapeDtypeStruct((SUBLANES, w), F32))
        out_specs.append(pl.BlockSpec((SUBLANES, w), lambda i: (0, 0)))
    for w in seqaccs:
        out_shape.append(jax.ShapeDtypeStruct((nseq, SUBLANES, w), F32))
        out_specs.append(pl.BlockSpec((1, SUBLANES, w), lambda i: (i // spb, 0, 0)))
    n_t, n_h, n_s, n_c = len(tiles), len(halos), len(seqvecs), len(consts)
    n_o, n_a, n_sa = len(outs), len(accs), len(seqaccs)
    r_in, r_out, r_copies = ride if ride else ((), (), ())
    n_bi, n_bo = n_t + n_h + n_s + n_c, n_o + n_a + n_sa

    def body(*refs):
        i = pl.program_id(0)
        ins = refs[:n_bi]
        orefs = refs[n_bi + len(r_in):n_bi + len(r_in) + n_bo]
        if ride:
            ride_start, ride_finish = _remote_steps(
                refs[n_bi:n_bi + len(r_in)], refs[n_bi + len(r_in) + n_bo:n_bi + len(r_in) + n_bo + len(r_out)],
                *refs[n_bi + len(r_in) + n_bo + len(r_out):], r_copies)
            pl.when(i == 0)(ride_start)
        vals = [r[...] for r in ins[:n_t + n_h]]
        vals += [r[0] for r in ins[n_t + n_h:n_t + n_h + n_s]]
        vals += list(ins[n_t + n_h + n_s:])
        res = fn(i, *vals)
        if not isinstance(res, (tuple, list)):
            res = (res,)
        for k in range(n_o):
            orefs[k][...] = res[k].astype(orefs[k].dtype)
        for k in range(n_a):
            r = orefs[n_o + k]

            @pl.when(i == 0)
            def _():
                r[...] = jnp.zeros_like(r)

            r[...] += res[n_o + k]

            @pl.when(i == nb - 1)
            def _():
                r[...] = jnp.broadcast_to(jnp.sum(r[...], axis=0, keepdims=True), r.shape)
        for k in range(n_sa):
            r = orefs[n_o + n_a + k]

            @pl.when(i % spb == 0)
            def _():
                r[...] = jnp.zeros_like(r)

            r[0] += res[n_o + n_a + k]

            @pl.when(i % spb == spb - 1)
            def _():
                r[0] = jnp.broadcast_to(jnp.sum(r[0], axis=0, keepdims=True), r.shape[1:])
        if ride:
            pl.when(i == nb - 1)(ride_finish)

    sem = ("arbitrary",) if (n_a or n_sa or ride) else ("parallel",)
    any_spec = pl.BlockSpec(memory_space=pl.ANY)
    scratch = [pltpu.SemaphoreType.DMA((len(r_copies),)), pltpu.SemaphoreType.DMA((len(r_copies),))] if ride else []
    return pl.pallas_call(
        body, name=name, grid=(nb,), in_specs=in_specs + [any_spec] * len(r_in),
        out_specs=out_specs + [any_spec] * len(r_out), out_shape=out_shape + list(r_out),
        scratch_shapes=scratch, compiler_params=_cparams(sem),
    )(*args, *r_in)


def _ln_fwd(r, g, b):
    mu = jnp.mean(r, axis=-1, keepdims=True)
    xc = r - mu
    var = jnp.mean(xc * xc, axis=-1, keepdims=True)
    rstd = lax.rsqrt(var + LN_EPS)
    n = xc * rstd
    return n * g + b, n, rstd


def _ln_bwd(dy, n, rstd, g):
    dn = dy * g
    return rstd * (dn - jnp.mean(dn, axis=-1, keepdims=True) - n * jnp.mean(dn * n, axis=-1, keepdims=True))


def _head_mean(t, g_ref):
    gw = g_ref.shape[0]
    hi = t.astype(BF16)
    lo = (t - hi.astype(F32)).astype(BF16)
    g = g_ref[...]
    parts = []
    for c in range(t.shape[1] // gw):
        sl = slice(c * gw, (c + 1) * gw)
        parts.append(_dot(hi[:, sl], g) + _dot(lo[:, sl], g))
    out = parts[0] if len(parts) == 1 else jnp.concatenate(parts, axis=1)
    return out * (1.0 / HD)


def _rope(z, c, s1, s2):
    w = z.shape[1]
    half = ROPE_DIMS // 2
    return z * c + pltpu.roll(z, half, 1) * s1 + pltpu.roll(z, w - half, 1) * s2


def _tile_lanes(t, w):
    reps = w // t.shape[1]
    return t if reps == 1 else jnp.concatenate([t] * reps, axis=1)


def _conv_taps(ext, prev, first):
    prev = jnp.where(first, jnp.zeros_like(prev), prev)
    r8 = lax.broadcasted_iota(jnp.int32, (SUBLANES, 1), 0)
    top = ext[0:SUBLANES]
    s1_top = jnp.where(r8 < 1, pltpu.roll(prev, 1, 0), pltpu.roll(top, 1, 0))
    s2_top = jnp.where(r8 < 2, pltpu.roll(prev, 2, 0), pltpu.roll(top, 2, 0))
    s1 = jnp.concatenate([s1_top, pltpu.roll(ext, 1, 0)[SUBLANES:]], axis=0)
    s2 = jnp.concatenate([s2_top, pltpu.roll(ext, 2, 0)[SUBLANES:]], axis=0)
    return s1, s2


def _conv_taps_up(ext, nxt, last):
    tm = ext.shape[0]
    nxt = jnp.where(last, jnp.zeros_like(nxt), nxt)
    r8 = lax.broadcasted_iota(jnp.int32, (SUBLANES, 1), 0)
    bot = ext[tm - SUBLANES:tm]
    u1_bot = jnp.where(r8 >= 7, pltpu.roll(nxt, 7, 0), pltpu.roll(bot, 7, 0))
    u2_bot = jnp.where(r8 >= 6, pltpu.roll(nxt, 6, 0), pltpu.roll(bot, 6, 0))
    u1 = jnp.concatenate([pltpu.roll(ext, tm - 1, 0)[:tm - SUBLANES], u1_bot], axis=0)
    u2 = jnp.concatenate([pltpu.roll(ext, tm - 2, 0)[:tm - SUBLANES], u2_bot], axis=0)
    return u1, u2


def _nt_rows(av, w_ref):
    n = w_ref.shape[0]
    ch = _pick(n, (512, 256, 128))
    ab = av.astype(BF16)
    parts = [_dot_nt(ab, w_ref[c * ch:(c + 1) * ch, :]) for c in range(n // ch)]
    return parts[0] if len(parts) == 1 else jnp.concatenate(parts, axis=1)


def _mm_tn(name, a, b, *, mod=None, tt=512, t2=None, by_chip=False):
    T, k1 = a.shape
    k2 = b.shape[1]
    t1 = k1 if k1 <= 1536 else _pick(k1, (1408, 1024, 512, 256, 128))
    if t2 is None:
        t2 = k2 if k2 <= 1536 else _pick(k2, (1408, 1024, 640, 512, 256, 128))
    wc = k2 // NCHIP
    if by_chip:
        t2 = 2 * wc
    tt = min(tt, S)
    spb = S // tt

    def body(*refs):
        if mod is not None:
            a_ref, sc_ref, sh_ref, b_ref, o_ref = refs
        else:
            a_ref, b_ref, o_ref = refs
        t = pl.program_id(2)

        @pl.when(t == 0)
        def _():
            o_ref[...] = jnp.zeros_like(o_ref)

        av = a_ref[...]
        if mod is not None:
            av = av * (1.0 + sc_ref[0]) + sh_ref[0]
        res = _dot_tn(av.astype(BF16), b_ref[...].astype(BF16))
        if by_chip:
            o_ref[0] += res[:, :wc]
            o_ref[1] += res[:, wc:]
        else:
            o_ref[...] += res

    in_specs = [pl.BlockSpec((tt, t1), lambda p, q, t: (t, p))]
    args = [a]
    if mod is not None:
        for v in mod:
            in_specs.append(pl.BlockSpec((1, 1, t1), lambda p, q, t: (t // spb, 0, p)))
            args.append(v)
    in_specs.append(pl.BlockSpec((tt, t2), lambda p, q, t: (t, q)))
    args.append(b)
    if by_chip:
        out_specs = pl.BlockSpec((2, t1, wc), lambda p, q, t: (q, p, 0))
        out_shape = jax.ShapeDtypeStruct((NCHIP, k1, wc), F32)
    else:
        out_specs = pl.BlockSpec((t1, t2), lambda p, q, t: (p, q))
        out_shape = jax.ShapeDtypeStruct((k1, k2), F32)
    return pl.pallas_call(
        body, name=name, grid=(k1 // t1, k2 // t2, T // tt), in_specs=in_specs, out_specs=out_specs,
        out_shape=out_shape, compiler_params=_cparams(("parallel", "parallel", "arbitrary")),
    )(*args)


def _mod_mm(name, x, sc, sh, ws, out_dtypes, rope=None, rope_secs=(), tm=256):
    T = x.shape[0]
    nw = len(ws)

    def fn(i, xv, *rest):
        if rope is not None:
            cv, s1v, s2v = rest[:3]
            rest = rest[3:]
        scv, shv = rest[:2]
        w_refs = rest[2:]
        h = (xv * (1.0 + scv) + shv).astype(BF16)
        res = []
        for k, w_ref in enumerate(w_refs):
            n = w_ref.shape[1]
            ch = WA if (k == 0 and rope is not None) else _pick(n, (512, 256, 128))
            parts = []
            for c in range(n // ch):
                z = _dot(h, w_ref[:, c * ch:(c + 1) * ch])
                if k == 0 and c in rope_secs:
                    z = _rope(z, _tile_lanes(cv, ch), _tile_lanes(s1v, ch), _tile_lanes(s2v, ch))
                parts.append(z.astype(out_dtypes[k]))
            res.append(parts[0] if len(parts) == 1 else jnp.concatenate(parts, axis=1))
        return tuple(res)

    tiles = (x,) + (tuple(rope) if rope is not None else ())
    outs = tuple((w.shape[1], dt) for w, dt in zip(ws, out_dtypes))
    return _rowwise(name, fn, T, tm, tiles=tiles, seqvecs=(sc, sh), consts=tuple(ws), outs=outs)


def _tri(tb, lower):
    r = lax.broadcasted_iota(jnp.int32, (tb, tb), 0)
    c = lax.broadcasted_iota(jnp.int32, (tb, tb), 1)
    return jnp.where((r >= c) if lower else (r <= c), 1.0, 0.0).astype(BF16)


def _split3(x):
    hi = x.astype(BF16)
    r = x - hi.astype(F32)
    mid = r.astype(BF16)
    return hi, mid, (r - mid.astype(F32)).astype(BF16)


def _cumsum_seq(name, ins, consts, fn_in, fn_out, outs, reverse, n_acc=0, tb=256):
    T = ins[0].shape[0]
    tb = min(tb, S)
    nbs = S // tb
    nseq = T // S
    n_i, n_c, n_o = len(ins), len(consts), len(outs)

    def blk(b, j):
        return (b * nbs + (nbs - 1 - j if reverse else j), 0)

    def body(*refs):
        i_refs, c_refs = refs[:n_i], refs[n_i:n_i + n_c]
        o_refs = refs[n_i + n_c:n_i + n_c + n_o]
        acc_refs = refs[n_i + n_c + n_o:n_i + n_c + n_o + n_acc]
        carry = refs[-1]
        b, j = pl.program_id(0), pl.program_id(1)

        @pl.when(j == 0)
        def _():
            carry[...] = jnp.zeros_like(carry)

        iv = [r[...] for r in i_refs]
        xin = fn_in(*iv, *c_refs)
        tri = _tri(tb, not reverse)
        cum = sum(_dot(tri, piece) for piece in _split3(xin)) + carry[0:1, :]
        carry[...] = carry[...] + jnp.sum(xin, axis=0, keepdims=True)
        res = fn_out(cum, *iv, *c_refs)
        for o, r in zip(o_refs, res):
            o[...] = r.astype(o.dtype)
        for a in acc_refs:
            @pl.when((b == 0) & (j == 0))
            def _():
                a[...] = jnp.zeros_like(a)

            a[...] += _rsum8(res[0])

            @pl.when((b == nseq - 1) & (j == nbs - 1))
            def _():
                a[...] = jnp.broadcast_to(jnp.sum(a[...], axis=0, keepdims=True), a.shape)

    in_specs = [pl.BlockSpec((tb, a.shape[1]), blk) for a in ins]
    in_specs += [pl.BlockSpec(c.shape, lambda b, j, nd=c.ndim: (0,) * nd) for c in consts]
    out_shape = [jax.ShapeDtypeStruct((T, w), dt) for w, dt in outs]
    out_shape += [jax.ShapeDtypeStruct((SUBLANES, outs[0][0]), F32)] * n_acc
    out_specs = [pl.BlockSpec((tb, w), blk) for w, _ in outs]
    out_specs += [pl.BlockSpec((SUBLANES, outs[0][0]), lambda b, j: (0, 0))] * n_acc
    return pl.pallas_call(
        body, name=name, grid=(nseq, nbs), in_specs=in_specs, out_specs=out_specs, out_shape=out_shape,
        scratch_shapes=[pltpu.VMEM((SUBLANES, FPAD), F32)],
        compiler_params=_cparams(("arbitrary", "arbitrary")),
    )(*ins, *consts)


def _log_sigmoid(x):
    return jnp.minimum(x, 0.0) - jnp.log(1.0 + jnp.exp(-jnp.abs(x)))


def _dil_bias(tq):
    max_win = max(w for w, _ in PATTERNS)
    nd = (max_win + tq - 1) // tq + 1
    qi = np.arange(tq)[:, None]
    kj = np.arange(tq)[None, :]
    tabs = []
    for dlt in range(nd):
        dist = dlt * tq + qi - kj
        mult = np.zeros((tq, tq), np.float64)
        for win, dil in PATTERNS:
            mult += (dist >= 0) & (dist % dil == 0) & (dist // dil <= win // dil)
        tabs.append(np.where(mult > 0, np.log(np.maximum(mult, 1.0)), NEG))
    return np.stack(tabs).astype(np.float32)


def _fold_tables(nha):
    hp_n = nha // 2
    pq = np.zeros((3, FPAD, hp_n * 2 * LANES), np.float32)
    pk = np.zeros((3, FPAD, hp_n * LANES), np.float32)
    oq = np.zeros((1, hp_n * 2 * LANES), np.float32)
    ok = np.zeros((1, hp_n * LANES), np.float32)
    sq = np.zeros((hp_n * LANES, FPAD), np.float32)
    sk = np.zeros((hp_n * LANES, FPAD), np.float32)
    for h in range(nha):
        hp, odd = divmod(h, 2)
        qb = hp * 2 * LANES + odd * (LANES + 8)
        kb = hp * LANES + odd * 8
        for i in range(3):
            pq[i, h, qb + i] = 1
            oq[0, qb + 3 + i] = 1
            ok[0, kb + i] = 1
            pk[i, h, kb + 3 + i] = 1
        sq[kb, h] = 1
        sk[kb + 3, h] = 1
    return pq, pk, oq, ok, sq, sk


def _stack_heads(x2, h0, extra=None):
    z = jnp.zeros_like(x2)
    a, b = jnp.where(h0, x2, z), jnp.where(h0, z, x2)
    if extra is not None:
        a = jnp.concatenate([a, extra[:, :LANES]], axis=1)
        b = jnp.concatenate([b, extra[:, LANES:]], axis=1)
    return jnp.concatenate([a, b], axis=0)


def _attn_fwd(name, qkv, secs, fox, eq=None, ek=None, bias=None, ride=()):
    T = qkv.shape[0]
    nq = S // TQ
    nbl = T // S
    hp_n = WA // LANES
    sq, sk, sv = (s * hp_n for s in secs)
    scale = HD ** -0.5
    nd = None if fox else bias.shape[0]

    n_r = len(ride)
    n_in = (5 if fox else 4) + n_r

    def body(*refs):
        if fox:
            q_ref, k_ref, v_ref, eq_ref, ek_ref = refs[:5]
        else:
            q_ref, k_ref, v_ref, b_ref = refs[:4]
        o_ref, lse_ref = refs[n_in:n_in + 2]
        i = pl.program_id(2)
        if n_r:
            ride_start, ride_finish = _gather_halves_steps(
                refs[n_in - n_r:n_in], refs[n_in + 2:n_in + 2 + n_r], *refs[n_in + 2 + n_r:])
            at = lambda b, hp, q: (pl.program_id(0) == b) & (pl.program_id(1) == hp) & (i == q)
            pl.when(at(0, 0, 0))(ride_start)
        lane = lax.broadcasted_iota(jnp.int32, (1, LANES), 1)
        h0 = lane < HD
        q2 = (q_ref[...].astype(F32) * scale).astype(BF16)
        qs = _stack_heads(q2, h0, eq_ref[...] if fox else None)

        def scores(t, diag):
            off = pl.multiple_of((i - t) * TQ, TQ)
            kk = k_ref[pl.ds(off, TQ), :]
            if fox:
                kk = jnp.concatenate([kk, ek_ref[pl.ds(off, TQ), :]], axis=1)
            s = jnp.concatenate([_dot_nt(qs[:TQ], kk), _dot_nt(qs[TQ:], kk)], axis=0)
            if not fox:
                s = (s.reshape(2, TQ, TQ) + b_ref[t]).reshape(2 * TQ, TQ)
            elif diag:
                rows = lax.broadcasted_iota(jnp.int32, (2, TQ, TQ), 1).reshape(2 * TQ, TQ)
                cols = lax.broadcasted_iota(jnp.int32, (2 * TQ, TQ), 1)
                s = jnp.where(cols <= rows, s, NEG)
            return s

        def update(t, s, m, l, acc):
            off = pl.multiple_of((i - t) * TQ, TQ)
            v2 = v_ref[pl.ds(off, TQ), :]
            m_new = jnp.maximum(m, jnp.max(s, axis=1, keepdims=True))
            p = jnp.exp(s - m_new)
            a = jnp.exp(m - m_new)
            l = a * l + jnp.sum(p, axis=1, keepdims=True)
            pb = p.astype(BF16)
            acc = a * acc + jnp.concatenate([_dot(pb[:TQ], v2), _dot(pb[TQ:], v2)], axis=0)
            return m_new, l, acc

        init = (jnp.full((2 * TQ, 1), NEG, F32), jnp.zeros((2 * TQ, 1), F32), jnp.zeros((2 * TQ, LANES), F32))
        n = i + 1 if fox else jnp.minimum(i + 1, nd)
        m, l, acc = update(0, scores(0, True), *init)
        m, l, acc = lax.fori_loop(1, n, lambda t, c: update(t, scores(t, False), *c), (m, l, acc))
        on = acc / l
        o_ref[...] = jnp.where(h0, on[:TQ], on[TQ:])
        lse = jnp.broadcast_to(m + jnp.log(l), (2 * TQ, LANES))
        lse_ref[...] = jnp.concatenate([lse[:TQ], lse[TQ:]], axis=1)
        if n_r:
            pl.when(at(nbl - 1, hp_n - 1, nq - 1))(ride_finish)

    in_specs = [
        pl.BlockSpec((TQ, LANES), lambda b, hp, i: (b * nq + i, sq + hp)),
        pl.BlockSpec((S, LANES), lambda b, hp, i: (b, sk + hp)),
        pl.BlockSpec((S, LANES), lambda b, hp, i: (b, sv + hp)),
    ]
    args = [qkv, qkv, qkv]
    if fox:
        in_specs += [pl.BlockSpec((TQ, 2 * LANES), lambda b, hp, i: (b * nq + i, hp)),
                     pl.BlockSpec((S, LANES), lambda b, hp, i: (b, hp))]
        args += [eq, ek]
    else:
        in_specs.append(pl.BlockSpec(bias.shape, lambda b, hp, i: (0, 0, 0)))
        args.append(bias)
    any_spec = pl.BlockSpec(memory_space=pl.ANY)
    out_specs = [pl.BlockSpec((TQ, LANES), lambda b, hp, i: (b * nq + i, hp)),
                 pl.BlockSpec((TQ, 2 * LANES), lambda b, hp, i: (b * nq + i, hp))] + [any_spec] * n_r
    out_shape = [jax.ShapeDtypeStruct((T, WA), F32), jax.ShapeDtypeStruct((T, 2 * WA), F32)]
    out_shape += [jax.ShapeDtypeStruct((NCHIP - 1,) + v.shape, v.dtype) for v in ride]
    sems = [pltpu.SemaphoreType.DMA((6 * n_r,)), pltpu.SemaphoreType.DMA((6 * n_r,))] if n_r else []
    sem = ("arbitrary",) * 3 if n_r else ("parallel", "parallel", "arbitrary")
    return pl.pallas_call(
        body, name=name, grid=(nbl, hp_n, nq), in_specs=in_specs + [any_spec] * n_r, out_specs=out_specs,
        out_shape=out_shape, scratch_shapes=sems, compiler_params=_cparams(sem),
    )(*args, *ride)


def _attn_bwd(name, qkv, secs, o, do, do_sec, lse, fox, eq=None, ek=None, bias=None, ride=None):
    T = qkv.shape[0]
    nq = S // TQ
    nbl = T // S
    hp_n = WA // LANES
    sq, sk, sv = (s * hp_n for s in secs)
    dsec = do_sec * hp_n
    scale = HD ** -0.5
    nd = None if fox else bias.shape[0]
    kc = 2 * LANES if fox else LANES

    r_in, r_out, r_copies = ride if ride else ((), (), ())
    n_bi, n_bo = (8, 5) if fox else (7, 3)
    n_i = n_bi + len(r_in)

    def body(*refs):
        if fox:
            q_ref, k_ref, v_ref, o_ref, do_ref, lse_ref, eq_ref, ek_ref = refs[:n_bi]
            dq_ref, dk_ref, dv_ref, dqe_ref, dek_ref = refs[n_i:n_i + n_bo]
        else:
            q_ref, k_ref, v_ref, o_ref, do_ref, lse_ref, b_ref = refs[:n_bi]
            dq_ref, dk_ref, dv_ref = refs[n_i:n_i + n_bo]
        dl_ref = refs[n_i + n_bo + len(r_out)]
        j = pl.program_id(2)
        if ride:
            ride_start, ride_finish = _remote_steps(
                refs[n_bi:n_i], refs[n_i + n_bo:n_i + n_bo + len(r_out)], *refs[n_i + n_bo + len(r_out) + 1:], r_copies)
            at = lambda b, hp, q: (pl.program_id(0) == b) & (pl.program_id(1) == hp) & (j == q)
            pl.when(at(0, 0, 0))(ride_start)
        lane = lax.broadcasted_iota(jnp.int32, (1, LANES), 1)
        h0 = lane < HD

        @pl.when(j == 0)
        def _():
            dq_ref[...] = jnp.zeros_like(dq_ref)
            if fox:
                dqe_ref[...] = jnp.zeros_like(dqe_ref)

            def dl_step(r, c):
                off = pl.multiple_of(r * TQ, TQ)
                d2 = do_ref[pl.ds(off, TQ), :] * o_ref[pl.ds(off, TQ), :]
                z2 = jnp.zeros_like(d2)
                dl0 = jnp.sum(jnp.where(h0, d2, z2), axis=1, keepdims=True)
                dl1 = jnp.sum(jnp.where(h0, z2, d2), axis=1, keepdims=True)
                dl_ref[pl.ds(off, TQ), :] = jnp.concatenate(
                    [jnp.broadcast_to(dl0, (TQ, LANES)), jnp.broadcast_to(dl1, (TQ, LANES))], axis=1)
                return c

            lax.fori_loop(0, nq, dl_step, 0)

        kk = k_ref[...]
        if fox:
            kk = jnp.concatenate([kk, ek_ref[...]], axis=1)
        v2 = v_ref[...]

        def wide(x2):
            st = jnp.concatenate([x2[:, :LANES], x2[:, LANES:]], axis=0)
            return st if TQ == LANES else jnp.concatenate([st] * (TQ // LANES), axis=1)

        def step(t, carry, diag):
            dkk, dv2 = carry
            off = pl.multiple_of((j + t) * TQ, TQ)
            q2 = (q_ref[pl.ds(off, TQ), :].astype(F32) * scale).astype(BF16)
            qs = _stack_heads(q2, h0, eq_ref[pl.ds(off, TQ), :] if fox else None)
            dos = _stack_heads(do_ref[pl.ds(off, TQ), :].astype(BF16), h0)
            s = jnp.concatenate([_dot_nt(qs[:TQ], kk), _dot_nt(qs[TQ:], kk)], axis=0)
            if not fox:
                s = (s.reshape(2, TQ, TQ) + b_ref[t]).reshape(2 * TQ, TQ)
            elif diag:
                rows = lax.broadcasted_iota(jnp.int32, (2, TQ, TQ), 1).reshape(2 * TQ, TQ)
                cols = lax.broadcasted_iota(jnp.int32, (2 * TQ, TQ), 1)
                s = jnp.where(cols <= rows, s, NEG)
            p = jnp.exp(s - wide(lse_ref[pl.ds(off, TQ), :]))
            dp = jnp.concatenate([_dot_nt(dos[:TQ], v2), _dot_nt(dos[TQ:], v2)], axis=0)
            dsb = (p * (dp - wide(dl_ref[pl.ds(off, TQ), :]))).astype(BF16)
            dv2 = dv2 + _dot_tn(p.astype(BF16), dos)
            dkk = dkk + _dot_tn(dsb, qs)
            dqq = jnp.concatenate([_dot(dsb[:TQ], kk), _dot(dsb[TQ:], kk)], axis=0)
            dq_ref[pl.ds(off, TQ), :] += jnp.where(h0, dqq[:TQ, :LANES], dqq[TQ:, :LANES])
            if fox:
                dqe_ref[pl.ds(off, TQ), :] += jnp.where(lane < SUBLANES, dqq[:TQ, LANES:], dqq[TQ:, LANES:])
            return dkk, dv2

        zero = (jnp.zeros((TQ, kc), F32), jnp.zeros((TQ, LANES), F32))
        if fox:
            dkk, dv2 = lax.fori_loop(1, nq - j, lambda t, c: step(t, c, False), step(0, zero, True))
        else:
            dkk, dv2 = lax.fori_loop(0, jnp.minimum(nq - j, nd), lambda t, c: step(t, c, False), zero)
        dk_ref[...] = dkk[:, :LANES]
        dv_ref[...] = dv2
        if fox:
            dek_ref[...] = dkk[:, LANES:]

        @pl.when(j == nq - 1)
        def _():
            dq_ref[...] = dq_ref[...] * scale

        if ride:
            pl.when(at(nbl - 1, hp_n - 1, nq - 1))(ride_finish)

    seq = lambda c, w=LANES: pl.BlockSpec((S, w), lambda b, hp, j: (b, c + hp))
    blk = lambda c: pl.BlockSpec((TQ, LANES), lambda b, hp, j: (b * nq + j, c + hp))
    in_specs = [seq(sq), blk(sk), blk(sv), seq(0), seq(dsec), seq(0, 2 * LANES)]
    args = [qkv, qkv, qkv, o, do, lse]
    if fox:
        in_specs += [seq(0, 2 * LANES), blk(0)]
        args += [eq, ek]
    else:
        in_specs.append(pl.BlockSpec(bias.shape, lambda b, hp, j: (0, 0, 0)))
        args.append(bias)
    out_specs = [seq(0), blk(0), blk(0)]
    out_shape = [jax.ShapeDtypeStruct((T, WA), F32)] * 3
    if fox:
        out_specs += [seq(0), blk(0)]
        out_shape += [jax.ShapeDtypeStruct((T, WA), F32)] * 2
    any_spec = pl.BlockSpec(memory_space=pl.ANY)
    scratch = [pltpu.VMEM((S, 2 * LANES), F32)]
    if ride:
        scratch += [pltpu.SemaphoreType.DMA((len(r_copies),)), pltpu.SemaphoreType.DMA((len(r_copies),))]
    sem = ("arbitrary",) * 3 if ride else ("parallel", "parallel", "arbitrary")
    return pl.pallas_call(
        body, name=name, grid=(nbl, hp_n, nq), in_specs=in_specs + [any_spec] * len(r_in),
        out_specs=out_specs + [any_spec] * len(r_out), out_shape=out_shape + list(r_out),
        scratch_shapes=scratch, compiler_params=_cparams(sem),
    )(*args, *r_in)


def _remote_steps(in_refs, out_refs, send_sems, recv_sems, remote):
    me = (lax.axis_index("x"), lax.axis_index("y"), lax.axis_index("c"))

    def peer_of(flip):
        return tuple(1 - v if f else v for v, f in zip(me, flip))

    def at(ref, idx):
        return ref if idx is None else ref.at[idx]

    def rcopy(k, who):
        flip, a, sfn, b, dfn = remote[k]
        return pltpu.make_async_remote_copy(
            src_ref=at(in_refs[a], sfn(*who)), dst_ref=at(out_refs[b], dfn(*who)),
            send_sem=send_sems.at[k], recv_sem=recv_sems.at[k], device_id=peer_of(flip), device_id_type=MESH)

    def start():
        for k in range(len(remote)):
            rcopy(k, me).start()

    def finish():
        for k in range(len(remote)):
            rcopy(k, peer_of(remote[k][0])).wait_recv()
        for k in range(len(remote)):
            rcopy(k, me).wait_send()

    return start, finish


def _exchange(name, ins, out_shapes, remote, local):
    n_in, n_out = len(ins), len(out_shapes)
    nr, nl = len(remote), len(local)

    def body(*refs):
        in_refs = refs[:n_in]
        out_refs = refs[n_in:n_in + n_out]
        send_sems, recv_sems, loc_sems = refs[n_in + n_out:]
        me = (lax.axis_index("x"), lax.axis_index("y"), lax.axis_index("c"))
        at = lambda ref, idx: ref if idx is None else ref.at[idx]
        locs = [pltpu.make_async_copy(at(in_refs[a], sfn(*me)), at(out_refs[b], dfn(*me)), loc_sems.at[k])
                for k, (a, sfn, b, dfn) in enumerate(local)]
        for cp in locs:
            cp.start()
        start, finish = _remote_steps(in_refs, out_refs, send_sems, recv_sems, remote)
        start()
        finish()
        for cp in locs:
            cp.wait()

    any_spec = pl.BlockSpec(memory_space=pl.ANY)
    return pl.pallas_call(
        body, name=name, in_specs=[any_spec] * n_in, out_specs=[any_spec] * n_out, out_shape=list(out_shapes),
        scratch_shapes=[pltpu.SemaphoreType.DMA((max(nr, 1),)), pltpu.SemaphoreType.DMA((max(nr, 1),)),
                        pltpu.SemaphoreType.DMA((max(nl, 1),))],
    )(*ins)


_FLIPS7 = [(0, 0, 1), (0, 1, 0), (0, 1, 1), (1, 0, 0), (1, 0, 1), (1, 1, 0), (1, 1, 1)]
_CHIP_FLIPS = [(1, 0, 0), (0, 1, 0), (1, 1, 0)]


def _dev_index(x, y, c):
    return 4 * x + 2 * y + c


def _chip_index(x, y, c):
    return 2 * x + y


def _all_gather8(name, v):
    remote = [(f, 0, lambda x, y, c: None, 0, _dev_index) for f in _FLIPS7]
    local = [(0, lambda x, y, c: None, 0, _dev_index)]
    return _exchange(name, [v], [jax.ShapeDtypeStruct((NDEV,) + v.shape, v.dtype)], remote, local)[0]


def _gather_halves_steps(in_refs, out_refs, send_sems, recv_sems):
    n_v = len(in_refs)
    x, y, c = lax.axis_index("x"), lax.axis_index("y"), lax.axis_index("c")
    sibling = (x, y, 1 - c)
    chips = [(1 - x, y), (x, 1 - y), (1 - x, 1 - y)]

    def copy(k, n, src, blk, half, to):
        return pltpu.make_async_remote_copy(
            src_ref=src, dst_ref=out_refs[n].at[blk, half], send_sem=send_sems.at[k], recv_sem=recv_sems.at[k],
            device_id=to, device_id_type=MESH)

    def first():
        return [copy(6 * n + j, n, in_refs[n].at[c], j, c, (*chip, c))
                for n in range(n_v) for j, chip in enumerate(chips)]

    def start():
        for cp in first():
            cp.start()

    def finish():
        passed = []
        for n in range(n_v):
            for j, chip in enumerate(chips):
                copy(6 * n + j, n, in_refs[n].at[c], j, c, (*chip, c)).wait_recv()
                fw = copy(6 * n + 3 + j, n, out_refs[n].at[j, c], j, c, sibling)
                fw.start()
                passed.append(fw)
        for n in range(n_v):
            for j in range(len(chips)):
                copy(6 * n + 3 + j, n, out_refs[n].at[j, 1 - c], j, 1 - c, sibling).wait_recv()
        for cp in first() + passed:
            cp.wait_send()

    return start, finish


def _gather_halves(name, vs):
    n_v = len(vs)

    def body(*refs):
        start, finish = _gather_halves_steps(refs[:n_v], refs[n_v:2 * n_v], *refs[2 * n_v:])
        start()
        finish()

    any_spec = pl.BlockSpec(memory_space=pl.ANY)
    return pl.pallas_call(
        body, name=name, in_specs=[any_spec] * n_v, out_specs=[any_spec] * n_v,
        out_shape=[jax.ShapeDtypeStruct((NCHIP - 1,) + v.shape, v.dtype) for v in vs],
        scratch_shapes=[pltpu.SemaphoreType.DMA((6 * n_v,)), pltpu.SemaphoreType.DMA((6 * n_v,))],
    )(*vs)


def _by_chip(own, others, chip):
    stacked = jnp.concatenate([own[None], others], axis=0)
    blocks = []
    for k in range(NCHIP):
        d = k ^ chip
        place = jnp.where(d == 0, 0, jnp.where(d == 2, 1, jnp.where(d == 1, 2, 3)))
        blocks.append(lax.dynamic_index_in_dim(stacked, place, axis=0, keepdims=False))
    return blocks


def _sum_leading(name, v, tm=None):
    n, r, w = v.shape
    tm = _pick(r, (256, 128, 64, 32, 16, 8)) if tm is None else tm

    def body(v_ref, o_ref):
        acc = v_ref[0].astype(F32)
        for k in range(1, n):
            acc = acc + v_ref[k].astype(F32)
        o_ref[...] = acc

    return pl.pallas_call(
        body, name=name, grid=(r // tm,), in_specs=[pl.BlockSpec((n, tm, w), lambda i: (0, i, 0))],
        out_specs=pl.BlockSpec((tm, w), lambda i: (i, 0)), out_shape=jax.ShapeDtypeStruct((r, w), F32),
        compiler_params=_cparams(("parallel",)),
    )(v)


def _add2(name, a, b, tm=None, out_dtype=F32):
    r, w = a.shape
    tm = _pick(r, (256, 128, 64, 32, 16, 8)) if tm is None else tm

    def body(a_ref, b_ref, o_ref):
        o_ref[...] = (a_ref[...] + b_ref[...]).astype(out_dtype)

    spec = pl.BlockSpec((tm, w), lambda i: (i, 0))
    return pl.pallas_call(
        body, name=name, grid=(r // tm,), in_specs=[spec, spec], out_specs=spec,
        out_shape=jax.ShapeDtypeStruct((r, w), out_dtype), compiler_params=_cparams(("parallel",)),
    )(a, b)


def _ada_fwd(call_all, w_shard):
    def body(c_ref, w_ref, o_ref):
        cv = c_ref[...]
        o_ref[...] = jnp.dot(cv * _sigmoid(cv), w_ref[...], preferred_element_type=F32,
                             precision=lax.Precision.HIGHEST)

    n = w_shard.shape[1]
    return pl.pallas_call(
        body, name="ada_fwd", out_shape=jax.ShapeDtypeStruct((call_all.shape[0], n), F32),
        compiler_params=pltpu.CompilerParams(vmem_limit_bytes=VMEM_LIMIT),
    )(call_all, w_shard)


def _ada_bwd(call_all, dada):
    def body(c_ref, d_ref, o_ref):
        cv = c_ref[...]
        o_ref[...] = lax.dot_general(cv * _sigmoid(cv), d_ref[...], (((0,), (0,)), ((), ())),
                                     preferred_element_type=F32, precision=lax.Precision.HIGHEST)

    return pl.pallas_call(
        body, name="ada_bwd", out_shape=jax.ShapeDtypeStruct((call_all.shape[1], dada.shape[1]), F32),
        compiler_params=pltpu.CompilerParams(vmem_limit_bytes=VMEM_LIMIT),
    )(call_all, dada)


def _adamw(name, w, g, m, v):
    r, wd = w.shape
    tm = _pick(r, (256, 128, 64, 32, 16, 8))
    bc1 = 1.0 - ADAM_B1 ** ADAM_STEP
    bc2 = 1.0 - ADAM_B2 ** ADAM_STEP

    def body(w_ref, g_ref, m_ref, v_ref, d_ref, mo_ref, vo_ref):
        gv = g_ref[...]
        mn = ADAM_B1 * m_ref[...] + (1.0 - ADAM_B1) * gv
        vn = ADAM_B2 * v_ref[...] + (1.0 - ADAM_B2) * (gv * gv)
        d_ref[...] = -ADAM_LR * ((mn / bc1) / (jnp.sqrt(vn / bc2) + ADAM_EPS) + ADAM_WD * w_ref[...])
        mo_ref[...] = mn
        vo_ref[...] = vn

    spec = pl.BlockSpec((tm, wd), lambda i: (i, 0))
    return pl.pallas_call(
        body, name=name, grid=(r // tm,), in_specs=[spec] * 4, out_specs=[spec] * 3,
        out_shape=[jax.ShapeDtypeStruct((r, wd), F32)] * 3, compiler_params=_cparams(("parallel",)),
    )(w, g, m, v)


def _rope_tables(positions):
    half = ROPE_DIMS // 2
    freqs = ROPE_THETA ** (-jnp.arange(0, ROPE_DIMS, 2, dtype=F32) / ROPE_DIMS)
    ang = positions.astype(F32).reshape(-1, 1) * freqs
    cos, sin = jnp.cos(ang), jnp.sin(ang)
    T = ang.shape[0]
    one = jnp.ones((T, HD - ROPE_DIMS), F32)
    zero = jnp.zeros((T, HD - ROPE_DIMS), F32)
    zh = jnp.zeros((T, half), F32)
    c64 = jnp.concatenate([cos, cos, one], axis=1)
    s1 = jnp.concatenate([zh, sin, zero], axis=1)
    s2 = jnp.concatenate([-sin, zh, zero], axis=1)
    rep = lambda t: jnp.concatenate([t] * (LANES // HD), axis=1)
    return rep(c64), rep(s1), rep(s2)


def _local_step(x, loss_target, positions, ada, w_qkv, w_f, w_out, w_up, conv_w8, w_down,
                b_fgate, gn, ln1_g, ln1_b, conv_b, ln2_g, ln2_b, late=None, early=None, last=None):
    T = x.shape[0]
    nbl = T // S
    nha = WA // HD
    sv = lambda k: ada[:, k:k + 1, :]
    sh_a, sc_a, g_a, sh_f, sc_f, g_f = (sv(k) for k in range(6))
    rope = _rope_tables(positions)
    neg_rope = (rope[0], -rope[1], -rope[2])
    gseg = jnp.asarray(np.kron(np.eye(min(256, 2 * WA) // HD), np.ones((HD, HD))), BF16)
    bias = jnp.asarray(_dil_bias(TQ))
    bf_pad = jnp.zeros((1, FPAD), F32).at[:, :nha].set(b_fgate)

    qkv, fa = _mod_mm("qkv_proj", x, sc_a, sh_a, (w_qkv, w_f), (BF16, F32), rope=rope, rope_secs=(3, 4))
    pq, pk, oq, ok, sq, sk = _fold_tables(nha)

    def fold_out(cum, f, b_ref, pq_ref, pk_ref, oq_ref, ok_ref):
        hi, mid, lo = _split3(cum)
        eqv = _dot(hi, pq_ref[0]) + _dot(mid, pq_ref[1]) + _dot(lo, pq_ref[2]) + oq_ref[...]
        ekv = ok_ref[...] - (_dot(hi, pk_ref[0]) + _dot(mid, pk_ref[1]) + _dot(lo, pk_ref[2]))
        return eqv, ekv

    eq, ek = _cumsum_seq(
        "fgate_fwd", [fa], [bf_pad, jnp.asarray(pq, BF16), jnp.asarray(pk, BF16), jnp.asarray(oq), jnp.asarray(ok)],
        lambda f, b_ref, *_: _log_sigmoid(f + b_ref[...]), fold_out, ((2 * WA, BF16), (WA, BF16)), reverse=False)
    oa, lse_a, *got = _attn_fwd("fox_fwd", qkv, (0, 1, 2), True, eq=eq, ek=ek, ride=late[0] if late else ())
    if late:
        w_up, w_down = late[1](got)
    ob, lse_b = _attn_fwd("dil_fwd", qkv, (3, 4, 5), False, bias=bias)

    def mix_fn(i, oav, obv, xv, gav, gn_ref, g_ref, wo_ref, l1g_ref, l1b_ref):
        o = jnp.concatenate([oav, obv], axis=1)
        rs = lax.rsqrt(_head_mean(o * o, g_ref) + RMS_EPS)
        merged = (o * rs * gn_ref[...]).astype(BF16)
        mix = _dot(merged, wo_ref[...])
        x1, _, _ = _ln_fwd(ALPHA * xv + gav * mix, l1g_ref[...], l1b_ref[...])
        return merged, mix, x1

    merged, mix, x1 = _rowwise("mix_out", mix_fn, T, 256, tiles=(oa, ob, x), seqvecs=(g_a,),
                               consts=(gn, gseg, w_out, ln1_g, ln1_b),
                               outs=((2 * WA, BF16), (D, F32), (D, F32)))
    u = _mod_mm("ffn_up", x1, sc_f, sh_f, (w_up,), (F32,))[0]

    def conv_y(i, uv, prev, cw_ref, cb_ref, tm):
        first = (i * tm) % S == 0
        s1, s2 = _conv_taps(uv, prev, first)
        y = cb_ref[...] + cw_ref[0:1, :] * s2 + cw_ref[1:2, :] * s1 + cw_ref[2:3, :] * uv
        return y, s1, s2

    tmc = 128

    def gate_fn(i, uv, prev, cw_ref, cb_ref):
        y, _, _ = conv_y(i, uv, prev, cw_ref, cb_ref, tmc)
        a, g = y[:, :DFF], y[:, DFF:]
        return g * _sigmoid(g) * a, y

    act, yconv = _rowwise("conv_gate", gate_fn, T, tmc, tiles=(u,), halos=((u, -1),), consts=(conv_w8, conv_b),
                          outs=((DFF, BF16), (2 * DFF, F32)))

    def down_fn(i, actv, x1v, tgt, gfv, wd_ref, g2_ref, b2_ref):
        ffn = _dot(actv, wd_ref[...])
        y, n2, rstd = _ln_fwd(ALPHA * x1v + gfv * ffn, g2_ref[...], b2_ref[...])
        err = y - tgt
        dy = err * (1.0 / D)
        dr2 = _ln_bwd(dy, n2, rstd, g2_ref[...])
        return (dr2, gfv * dr2, _rsum8(err * err), _rsum8(dy * n2), _rsum8(dy), _rsum8(dr2 * ffn))

    dr2, dffn, loss_acc, d_ln2g, d_ln2b, d_gf = _rowwise(
        "ffn_down_loss", down_fn, T, 256, tiles=(act, x1, loss_target), seqvecs=(g_f,),
        consts=(w_down, ln2_g, ln2_b), outs=((D, F32), (D, F32)), accs=(D, D, D), seqaccs=(D,))

    def gate_conv_bwd_fn(i, uv, yv, dfv, y_nxt, df_nxt, cw_ref, wd_ref):
        last = ((i + 1) * tmc) % S == 0
        y = jnp.concatenate([yv, y_nxt], axis=0)
        df_ext = jnp.concatenate([dfv, df_nxt], axis=0).astype(BF16)
        ch = _pick(DFF, (256, 128))
        dav = jnp.concatenate([_dot_nt(df_ext, wd_ref[c * ch:(c + 1) * ch, :]) for c in range(DFF // ch)], axis=1)
        a, g = y[:, :DFF], y[:, DFF:]
        sg = _sigmoid(g)
        dyc_ext = jnp.concatenate([dav * (g * sg), dav * a * (sg * (1.0 + g * (1.0 - sg)))], axis=1)
        dyc = dyc_ext[:tmc]
        u1, u2 = _conv_taps_up(dyc, dyc_ext[tmc:], last)
        du_ = cw_ref[2:3, :] * dyc + cw_ref[1:2, :] * u1 + cw_ref[0:1, :] * u2
        return du_, _rsum8(dyc), _rsum8(uv * u2), _rsum8(uv * u1), _rsum8(uv * dyc)

    du, d_cb, d_cw0, d_cw1, d_cw2 = _rowwise(
        "gate_conv_bwd", gate_conv_bwd_fn, T, tmc, tiles=(u, yconv, dffn), halos=((yconv, 1), (dffn, 1)),
        consts=(conv_w8, w_down), outs=((2 * DFF, BF16),), accs=(2 * DFF,) * 4)
    g_w_down = _mm_tn("dw_down", act, dffn)
    g_w_up = _mm_tn("dw_up", x1, du, mod=(sc_f, sh_f), by_chip=True)

    def ln1_bwd_fn(i, dr2v, duv, xv, mixv, x1v, scfv, gav, l1g_ref, wu_ref):
        dh2v = _nt_rows(duv, wu_ref)
        dx1 = ALPHA * dr2v + dh2v * (1.0 + scfv)
        _, n1, rstd = _ln_fwd(ALPHA * xv + gav * mixv, l1g_ref[...], 0.0)
        dr1 = _ln_bwd(dx1, n1, rstd, l1g_ref[...])
        return (dr1, gav * dr1, _rsum8(dx1 * n1), _rsum8(dx1),
                _rsum8(dh2v * x1v), _rsum8(dh2v), _rsum8(dr1 * mixv))

    d_cw = jnp.stack([d_cw0[0], d_cw1[0], d_cw2[0]], axis=0)
    dr1, dmix, d_ln1g, d_ln1b, d_scf, d_shf, d_ga, *swapped = _rowwise(
        "ln1_bwd", ln1_bwd_fn, T, 256, tiles=(dr2, du, x, mix, x1), seqvecs=(sc_f, g_a), consts=(ln1_g, w_up),
        outs=((D, F32), (D, BF16)), accs=(D, D), seqaccs=(D, D, D),
        ride=early[0](g_w_up, g_w_down, d_cw) if early else None)

    g_w_out = _mm_tn("dw_out", merged, dmix)

    def hn_bwd_fn(i, dmixv, oav, obv, gn_ref, g_ref, wo_ref):
        dmv = _nt_rows(dmixv, wo_ref)
        o = jnp.concatenate([oav, obv], axis=1)
        rs = lax.rsqrt(_head_mean(o * o, g_ref) + RMS_EPS)
        nrm = o * rs
        dn = dmv * gn_ref[...]
        do = rs * (dn - nrm * _head_mean(dn * nrm, g_ref))
        return do, _rsum8(dmv * nrm)

    do, d_gn = _rowwise("headnorm_bwd", hn_bwd_fn, T, 256, tiles=(dmix, oa, ob), consts=(gn, gseg, w_out),
                        outs=((2 * WA, F32),), accs=(2 * WA,))
    dqa, dka, dva, dqe, dek, *arrived = _attn_bwd(
        "fox_bwd", qkv, (0, 1, 2), oa, do, 0, lse_a, True, eq=eq, ek=ek, ride=early[1](swapped) if early else None)
    dqb, dkb, dvb = _attn_bwd("dil_bwd", qkv, (3, 4, 5), ob, do, 1, lse_b, False, bias=bias)
    hdot = lambda a, m_ref: sum(_dot(piece, m_ref[...]) for piece in _split3(a))
    dfa, d_bf = _cumsum_seq(
        "fgate_bwd", [dqe, dek, fa], [bf_pad, jnp.asarray(sq, BF16), jnp.asarray(sk, BF16)],
        lambda dq_, dk_, f, b_ref, sq_ref, sk_ref: hdot(dq_, sq_ref) - hdot(dk_, sk_ref),
        lambda cum, dq_, dk_, f, b_ref, sq_ref, sk_ref: (cum * _sigmoid(-(f + b_ref[...])),),
        ((FPAD, F32),), reverse=True, n_acc=1)

    def dz_fn(i, a0, a1, a2, b0, b1, b2, fv, cv, s1v, s2v):
        ct, s1t, s2t = (_tile_lanes(t, WA) for t in (cv, s1v, s2v))
        return jnp.concatenate([a0, a1, a2, _rope(b0, ct, s1t, s2t), _rope(b1, ct, s1t, s2t), b2, fv], axis=1)

    dz = _rowwise("dz_pack", dz_fn, T, 256, tiles=(dqa, dka, dva, dqb, dkb, dvb, dfa) + neg_rope,
                  outs=((6 * WA + FPAD, BF16),))[0]
    w_cat = jnp.concatenate([w_qkv, w_f], axis=1)
    g_w_cat = _mm_tn("dw_in", x, dz, mod=(sc_a, sh_a), tt=256, t2=dz.shape[1])

    def dx_fn(i, dr1v, dzv, xv, scav, wc_ref):
        dh1v = _nt_rows(dzv, wc_ref)
        return ALPHA * dr1v + dh1v * (1.0 + scav), _rsum8(dh1v * xv), _rsum8(dh1v)

    grad_x, d_sca, d_sha, *arrived_last = _rowwise(
        "dx_out", dx_fn, T, 256, tiles=(dr1, dz, x), seqvecs=(sc_a,), consts=(w_cat,), outs=((D, F32),),
        seqaccs=(D, D), ride=last(g_w_cat, g_w_out) if last else None)

    row0 = lambda a: a[..., 0, :]
    d_ada = jnp.stack([row0(d_sha), row0(d_sca), row0(d_ga), row0(d_shf), row0(d_scf), row0(d_gf)], axis=1)
    loss_part = (0.5 / D) * jnp.sum(loss_acc[0])
    small = dict(b_fgate=row0(d_bf)[:nha], gn=row0(d_gn), ln1_g=row0(d_ln1g), ln1_b=row0(d_ln1b),
                 conv_b=row0(d_cb), ln2_g=row0(d_ln2g), ln2_b=row0(d_ln2b))
    big = dict(w_cat=g_w_cat, w_out=g_w_out, w_up=g_w_up, conv_w=d_cw, w_down=g_w_down, early=arrived, last=arrived_last)
    return loss_part, grad_x, d_ada, small, big


def _rows_of(n, w=None):
    return -(-n // (D if w is None else w))


def _as_rows(v):
    w = D
    k = v.shape[0]
    flat = v.reshape(k, -1)
    rows = _rows_of(_rows_of(flat.shape[1], w), SUBLANES) * SUBLANES
    flat = jnp.pad(flat, ((0, 0), (0, rows * w - flat.shape[1])))
    return flat.reshape(k, rows, w)


def kernel(x, c, positions, w_ada, b_ada, w_in, b_fgate, gn_a, gn_b, w_out, ln1_g, ln1_b, w_up, conv_w, conv_b, w_down, ln2_g, ln2_b, loss_target, m_w_ada, m_b_ada, m_w_in, m_b_fgate, m_gn_a, m_gn_b, m_w_out, m_ln1_g, m_ln1_b, m_w_up, m_conv_w, m_conv_b, m_w_down, m_ln2_g, m_ln2_b, v_w_ada, v_b_ada, v_w_in, v_b_fgate, v_gn_a, v_gn_b, v_w_out, v_ln1_g, v_ln1_b, v_w_up, v_conv_w, v_conv_b, v_w_down, v_ln2_g, v_ln2_b):
    mx, my, mc = lax.axis_index("x"), lax.axis_index("y"), lax.axis_index("c")
    dev = _dev_index(mx, my, mc)
    chip = _chip_index(mx, my, mc)
    nbl = x.shape[0]
    T = nbl * S
    nha = WA // HD
    n_ada = w_ada.shape[2]

    c_pad = jnp.zeros((SUBLANES, D), F32).at[:nbl].set(c)
    c_all = _all_gather8("gather_c", c_pad)[:, :nbl].reshape(NDEV * nbl, D)
    ada_part = _ada_fwd(c_all, w_ada[0])
    n_cw = conv_w.shape[2]
    cw_rows = jnp.pad(conv_w[0], ((0, SUBLANES - conv_w.shape[1]), (0, n_ada - n_cw)))
    ada_blocks = _all_gather8("gather_ada", jnp.concatenate([ada_part, cw_rows], axis=0))
    n_c = NDEV * nbl
    ada_all = jnp.concatenate([ada_blocks[2 * k, :n_c] for k in range(NCHIP)], axis=1) + b_ada
    conv_w8 = jnp.concatenate([ada_blocks[2 * k, n_c:, :n_cw] for k in range(NCHIP)], axis=1)
    ada = lax.dynamic_slice_in_dim(ada_all, dev * nbl, nbl, axis=0).reshape(nbl, 6, D)

    w_in_sh = jnp.pad(w_in[0].astype(BF16), ((0, 0), (0, _rows_of(w_in.shape[2], LANES) * LANES - w_in.shape[2])))
    halve = lambda t: t.reshape(2, t.shape[0] // 2, t.shape[1])
    whole = lambda g, t: jnp.stack(_by_chip(halve(t), g, chip)).reshape((NCHIP,) + t.shape)
    w_out_sh, w_up_sh, w_down_sh = w_out[0].astype(BF16), w_up[0].astype(BF16), w_down[0].astype(BF16)
    g_in, g_out = _gather_halves("gather_w", [halve(w_in_sh), halve(w_out_sh)])
    g_in, g_out = whole(g_in, w_in_sh), whole(g_out, w_out_sh)
    w_in_full = jnp.concatenate([g_in[k][:, :w_in.shape[2]] for k in range(NCHIP)], axis=1)
    w_qkv = jnp.concatenate([w_in_full[:, :3 * WA], w_in_full[:, 3 * WA + nha:]], axis=1)
    w_f = jnp.pad(w_in_full[:, 3 * WA:3 * WA + nha], ((0, 0), (0, FPAD - nha)))
    w_out_full = g_out.reshape(NCHIP * w_out.shape[1], D)

    def late_weights(got):
        g_up, g_down = whole(got[0], w_up_sh), whole(got[1], w_down_sh)
        return (jnp.concatenate([g_up[k] for k in range(NCHIP)], axis=1),
                g_down.reshape(NCHIP * w_down.shape[1], D))

    n_in = w_in.shape[2]
    to_sib = lambda x, y, c: None

    def reduce_swap(blocks):
        cut = [t.reshape(2 * NCHIP, t.shape[1] // 2, t.shape[2]) for t in blocks]
        copies = [((0, 0, 1), n, lambda x, y, c, k=k: 2 * k + 1 - c, n, lambda x, y, c, k=k: k)
                  for n in range(len(cut)) for k in range(NCHIP)]
        return cut, (cut, [jax.ShapeDtypeStruct((NCHIP,) + t.shape[1:], F32) for t in cut], copies)

    def reduce_sum(cut, from_sib, tag):
        flat = lambda v: v.reshape(-1, v.shape[-1])
        pair_sums = []
        for n, (t, fs) in enumerate(zip(cut, from_sib)):
            mine = lax.dynamic_index_in_dim(t.reshape((NCHIP, 2) + t.shape[1:]), mc, axis=1, keepdims=False)
            pair_sums.append(_add2("pair_sum_%s%d" % (tag, n), flat(mine), flat(fs), out_dtype=BF16).reshape(fs.shape))
        copies = []
        for n in range(len(cut)):
            for j, f in enumerate(_CHIP_FLIPS):
                src = lambda x, y, c, f=f: _chip_index(1 - x if f[0] else x, 1 - y if f[1] else y, c)
                copies.append((f, n, src, n, lambda x, y, c, j=j: j))
        shapes = [jax.ShapeDtypeStruct((NCHIP - 1,) + t.shape[1:], BF16) for t in pair_sums]
        return pair_sums, (pair_sums, shapes, copies)

    def reduce_back(pair_sums, arrived, tag):
        my_halves = []
        for n, (ps, got) in enumerate(zip(pair_sums, arrived)):
            own = lax.dynamic_index_in_dim(ps, chip, axis=0, keepdims=True)
            my_halves.append(_sum_leading("chip_sum_%s%d" % (tag, n), jnp.concatenate([own, got], axis=0)))
        sib_halves = _exchange("pair_share_" + tag, my_halves, [jax.ShapeDtypeStruct(t.shape, F32) for t in my_halves],
                               [((0, 0, 1), n, to_sib, n, to_sib) for n in range(len(my_halves))], [])
        whole_blocks = []
        for mh, shf in zip(my_halves, sib_halves):
            pair = jnp.stack([mh, shf])
            whole_blocks.append(jnp.concatenate([lax.dynamic_index_in_dim(pair, mc, axis=0, keepdims=False),
                                                 lax.dynamic_index_in_dim(pair, 1 - mc, axis=0, keepdims=False)], axis=0))
        return whole_blocks

    stash = {}

    def early_swap(g_up, g_down, g_cw):
        sh_cw = _as_rows(g_cw.reshape(conv_w.shape[1], NCHIP, -1).transpose(1, 0, 2))
        rows = w_down.shape[1] + sh_cw.shape[1]
        pad = _rows_of(rows, 2 * LANES) * 2 * LANES - rows
        sh_a = jnp.concatenate([g_down.reshape(NCHIP, -1, D), sh_cw, jnp.zeros((NCHIP, pad, D), F32)], axis=1)
        stash["cut"], ride = reduce_swap([sh_a, g_up])
        return ride

    def early_scatter(from_sib):
        stash["ps"], ride = reduce_sum(stash["cut"], from_sib, "e")
        return ride

    def last_reduce(g_cat, g_out):
        g_w_in_full = jnp.concatenate([g_cat[:, :3 * WA], g_cat[:, 6 * WA:6 * WA + nha], g_cat[:, 3 * WA:6 * WA]],
                                      axis=1)
        sh_in = jnp.pad(g_w_in_full.reshape(D, NCHIP, n_in).transpose(1, 0, 2),
                        ((0, 0), (0, 0), (0, _rows_of(n_in, LANES) * LANES - n_in)))
        cut, swap = reduce_swap([g_out.reshape(NCHIP, -1, D), sh_in])
        stash["ps_l"], ride = reduce_sum(cut, _exchange("pair_swap_l", swap[0], swap[1], swap[2], []), "l")
        return ride

    gn = jnp.concatenate([gn_a, gn_b], axis=1)
    loss_part, grad_x, d_ada, small, big = _local_step(
        x.reshape(T, D), loss_target.reshape(T, D), positions, ada, w_qkv, w_f, w_out_full, None, conv_w8,
        None, b_fgate, gn, ln1_g, ln1_b, conv_b, ln2_g, ln2_b, late=([halve(w_up_sh), halve(w_down_sh)], late_weights), early=(early_swap, early_scatter), last=last_reduce)

    def row_pad(v, rows):
        flat = v.reshape(-1)
        return jnp.pad(flat, (0, rows * D - flat.shape[0]))

    n_cb = _rows_of(2 * DFF)
    small_flat = jnp.concatenate([
        row_pad(small["b_fgate"], 1), row_pad(small["gn"], 1), row_pad(small["ln1_g"], 1),
        row_pad(small["ln1_b"], 1), row_pad(small["ln2_g"], 1), row_pad(small["ln2_b"], 1),
        row_pad(jnp.full((1,), loss_part, F32), 1), row_pad(small["conv_b"], n_cb)])
    n_small = _rows_of(small_flat.shape[0], SUBLANES * D) * SUBLANES
    small_rows = jnp.pad(small_flat, (0, n_small * D - small_flat.shape[0])).reshape(n_small, D)
    ada_rows = jnp.pad(d_ada.reshape(nbl, 6, D), ((0, 0), (0, SUBLANES - 6), (0, 0))).reshape(nbl * SUBLANES, D)
    gathered = _all_gather8("gather_small", jnp.concatenate([small_rows, ada_rows], axis=0))
    red = _sum_leading("sum_small", gathered, tm=SUBLANES)
    g_b_fgate = red[0:1, :nha]
    g_gn = red[1:2, :2 * WA]
    g_ln1_g, g_ln1_b, g_ln2_g, g_ln2_b = red[2:3], red[3:4], red[4:5], red[5:6]
    loss = red[6, 0]
    g_conv_b = red[7:7 + n_cb].reshape(1, -1)[:, :2 * DFF]
    g_b_ada = _add2("sum_b_ada", red[n_small:n_small + SUBLANES], red[n_small + SUBLANES:n_small + 2 * SUBLANES],
                    tm=SUBLANES)[:6].reshape(1, 6 * D)
    dada_all = gathered[:, n_small:].reshape(NDEV, nbl, SUBLANES, D)[:, :, :6].reshape(NDEV * nbl, 6 * D)
    g_w_ada = _ada_bwd(c_all, lax.dynamic_slice_in_dim(dada_all, chip * n_ada, n_ada, axis=1))

    shards_e = reduce_back(stash["ps"], big["early"], "e")
    shards_l = reduce_back(stash["ps_l"], big["last"], "l")
    r0 = w_down.shape[1]
    g_w_down = shards_e[0][:r0]
    g_conv_w = shards_e[0][r0:r0 + SUBLANES].reshape(-1)[:int(np.prod(conv_w.shape[1:]))].reshape(conv_w.shape[1:])
    g_w_up = shards_e[1]
    g_w_out = shards_l[0]
    g_w_in = shards_l[1][:, :n_in]

    grads = dict(w_ada=g_w_ada, b_ada=g_b_ada, w_in=g_w_in, b_fgate=g_b_fgate, gn_a=g_gn[:, :WA], gn_b=g_gn[:, WA:],
                 w_out=g_w_out, ln1_g=g_ln1_g, ln1_b=g_ln1_b, w_up=g_w_up, conv_w=g_conv_w, conv_b=g_conv_b,
                 w_down=g_w_down, ln2_g=g_ln2_g, ln2_b=g_ln2_b)
    weights = dict(w_ada=w_ada, b_ada=b_ada, w_in=w_in, b_fgate=b_fgate, gn_a=gn_a, gn_b=gn_b, w_out=w_out,
                   ln1_g=ln1_g, ln1_b=ln1_b, w_up=w_up, conv_w=conv_w, conv_b=conv_b, w_down=w_down,
                   ln2_g=ln2_g, ln2_b=ln2_b)
    ms = dict(w_ada=m_w_ada, b_ada=m_b_ada, w_in=m_w_in, b_fgate=m_b_fgate, gn_a=m_gn_a, gn_b=m_gn_b,
              w_out=m_w_out, ln1_g=m_ln1_g, ln1_b=m_ln1_b, w_up=m_w_up, conv_w=m_conv_w, conv_b=m_conv_b,
              w_down=m_w_down, ln2_g=m_ln2_g, ln2_b=m_ln2_b)
    vs = dict(w_ada=v_w_ada, b_ada=v_b_ada, w_in=v_w_in, b_fgate=v_b_fgate, gn_a=v_gn_a, gn_b=v_gn_b,
              w_out=v_w_out, ln1_g=v_ln1_g, ln1_b=v_ln1_b, w_up=v_w_up, conv_w=v_conv_w, conv_b=v_conv_b,
              w_down=v_w_down, ln2_g=v_ln2_g, ln2_b=v_ln2_b)
    names = list(weights)
    big_names = ("w_ada", "w_in", "w_out", "w_up", "w_down")
    delta, new_m, new_v = {}, {}, {}
    for n in big_names:
        shp = weights[n].shape
        d, m2, v2 = _adamw("adamw_" + n, weights[n][0], grads[n].reshape(shp[1:]), ms[n][0], vs[n][0])
        delta[n], new_m[n], new_v[n] = d.reshape(shp), m2.reshape(shp), v2.reshape(shp)
    small_names = [n for n in names if n not in big_names]

    def pack_small(src):
        flats = []
        for n in small_names:
            flat = src[n].reshape(-1)
            flats.append(jnp.pad(flat, (0, _rows_of(flat.shape[0]) * D - flat.shape[0])))
        allf = jnp.concatenate(flats)
        rows = _rows_of(allf.shape[0], SUBLANES * D) * SUBLANES
        return jnp.pad(allf, (0, rows * D - allf.shape[0])).reshape(rows, D)

    sd, sm, sv_ = _adamw("adamw_small", pack_small(weights), pack_small(grads), pack_small(ms), pack_small(vs))
    off = 0
    for n in small_names:
        shp = weights[n].shape
        cnt = int(np.prod(shp))
        r = _rows_of(cnt)
        for dst, src in ((delta, sd), (new_m, sm), (new_v, sv_)):
            dst[n] = src[off:off + r].reshape(-1)[:cnt].reshape(shp)
        off += r

    out_g = {n: grads[n].reshape(weights[n].shape) for n in names}
    return (loss, grad_x.reshape(x.shape), *[out_g[n] for n in names], *[delta[n] for n in names],
            *[new_m[n] for n in names], *[new_v[n] for n in names])
```

```python
import numpy as np
import jax
import jax.numpy as jnp
from jax import lax
from jax.experimental import pallas as pl
from jax.experimental.pallas import tpu as pltpu

F32 = jnp.float32
BF16 = jnp.bfloat16

D = 1024
S = 4096
HD = 64
WA = 512
DFF = 2816
NCHIP = 4
NDEV = 8
PATTERNS = ((128, 1), (512, 4), (2048, 16))
ROPE_THETA = 500000.0
ROPE_DIMS = HD // 4
ALPHA = (2.0 * 1) ** 0.25
LN_EPS = 1e-5
RMS_EPS = 1e-6
ADAM_LR = 0.001
ADAM_B1 = 0.9
ADAM_B2 = 0.999
ADAM_EPS = 1e-08
ADAM_WD = 0.01
ADAM_STEP = 10

LANES = 128
SUBLANES = 8
TQ = 512
FPAD = LANES
NEG = -1e30
VMEM_LIMIT = 56 * 1024 * 1024
MESH = pl.DeviceIdType.MESH


def _cparams(sem):
    return pltpu.CompilerParams(dimension_semantics=sem, vmem_limit_bytes=VMEM_LIMIT)


def _pick(n, cands):
    for c in cands:
        if n % c == 0:
            return c
    return n


def _rsum8(v):
    tm, w = v.shape
    return jnp.sum(v.reshape(tm // SUBLANES, SUBLANES, w), axis=0)


def _sigmoid(x):
    return 1.0 / (1.0 + jnp.exp(-x))


def _dot(a, b):
    return jnp.dot(a, b, preferred_element_type=F32)


def _dot_nt(a, b):
    return lax.dot_general(a, b, (((1,), (1,)), ((), ())), preferred_element_type=F32)


def _dot_tn(a, b):
    return lax.dot_general(a, b, (((0,), (0,)), ((), ())), preferred_element_type=F32)


def _rowwise(name, fn, T, tm, *, tiles=(), halos=(), seqvecs=(), consts=(), outs=(), accs=(), seqaccs=(),
             seq_len=None, ride=None):
    seq_len = S if seq_len is None else seq_len
    nb = T // tm
    spb = max(seq_len // tm, 1)
    nseq = max(T // seq_len, 1)
    n8 = T // SUBLANES
    r8 = tm // SUBLANES
    in_specs, args = [], []
    for a in tiles:
        in_specs.append(pl.BlockSpec((tm, a.shape[1]), lambda i: (i, 0)))
        args.append(a)
    for a, direction in halos:
        if direction < 0:
            idx = lambda i: (jnp.maximum(i * r8 - 1, 0), 0)
        else:
            idx = lambda i: (jnp.minimum((i + 1) * r8, n8 - 1), 0)
        in_specs.append(pl.BlockSpec((SUBLANES, a.shape[1]), idx))
        args.append(a)
    for a in seqvecs:
        in_specs.append(pl.BlockSpec((1, 1, a.shape[2]), lambda i: (i // spb, 0, 0)))
        args.append(a)
    for a in consts:
        in_specs.append(pl.BlockSpec(a.shape, lambda i, nd=a.ndim: (0,) * nd))
        args.append(a)
    out_shape, out_specs = [], []
    for w, dt in outs:
        out_shape.append(jax.ShapeDtypeStruct((T, w), dt))
        out_specs.append(pl.BlockSpec((tm, w), lambda i: (i, 0)))
    for w in accs:
        out_shape.append(jax.ShapeDtypeStruct((SUBLANES, w), F32))
        out_specs.append(pl.BlockSpec((SUBLANES, w), lambda i: (0, 0)))
    for w in seqaccs:
        out_shape.append(jax.ShapeDtypeStruct((nseq, SUBLANES, w), F32))
        out_specs.append(pl.BlockSpec((1, SUBLANES, w), lambda i: (i // spb, 0, 0)))
    n_t, n_h, n_s, n_c = len(tiles), len(halos), len(seqvecs), len(consts)
    n_o, n_a, n_sa = len(outs), len(accs), len(seqaccs)
    r_in, r_out, r_copies = ride if ride else ((), (), ())
    n_bi, n_bo = n_t + n_h + n_s + n_c, n_o + n_a + n_sa

    def body(*refs):
        i = pl.program_id(0)
        ins = refs[:n_bi]
        orefs = refs[n_bi + len(r_in):n_bi + len(r_in) + n_bo]
        if ride:
            ride_start, ride_finish = _remote_steps(
                refs[n_bi:n_bi + len(r_in)], refs[n_bi + len(r_in) + n_bo:n_bi + len(r_in) + n_bo + len(r_out)],
                *refs[n_bi + len(r_in) + n_bo + len(r_out):], r_copies)
            pl.when(i == 0)(ride_start)
        vals = [r[...] for r in ins[:n_t + n_h]]
        vals += [r[0] for r in ins[n_t + n_h:n_t + n_h + n_s]]
        vals += list(ins[n_t + n_h + n_s:])
        res = fn(i, *vals)
        if not isinstance(res, (tuple, list)):
            res = (res,)
        for k in range(n_o):
            orefs[k][...] = res[k].astype(orefs[k].dtype)
        for k in range(n_a):
            r = orefs[n_o + k]

            @pl.when(i == 0)
            def _():
                r[...] = jnp.zeros_like(r)

            r[...] += res[n_o + k]

            @pl.when(i == nb - 1)
            def _():
                r[...] = jnp.broadcast_to(jnp.sum(r[...], axis=0, keepdims=True), r.shape)
        for k in range(n_sa):
            r = orefs[n_o + n_a + k]

            @pl.when(i % spb == 0)
            def _():
                r[...] = jnp.zeros_like(r)

            r[0] += res[n_o + n_a + k]

            @pl.when(i % spb == spb - 1)
            def _():
                r[0] = jnp.broadcast_to(jnp.sum(r[0], axis=0, keepdims=True), r.shape[1:])
        if ride:
            pl.when(i == nb - 1)(ride_finish)

    sem = ("arbitrary",) if (n_a or n_sa or ride) else ("parallel",)
    any_spec = pl.BlockSpec(memory_space=pl.ANY)
    scratch = [pltpu.SemaphoreType.DMA((len(r_copies),)), pltpu.SemaphoreType.DMA((len(r_copies),))] if ride else []
    return pl.pallas_call(
        body, name=name, grid=(nb,), in_specs=in_specs + [any_spec] * len(r_in),
        out_specs=out_specs + [any_spec] * len(r_out), out_shape=out_shape + list(r_out),
        scratch_shapes=scratch, compiler_params=_cparams(sem),
    )(*args, *r_in)


def _ln_fwd(r, g, b):
    mu = jnp.mean(r, axis=-1, keepdims=True)
    xc = r - mu
    var = jnp.mean(xc * xc, axis=-1, keepdims=True)
    rstd = lax.rsqrt(var + LN_EPS)
    n = xc * rstd
    return n * g + b, n, rstd


def _ln_bwd(dy, n, rstd, g):
    dn = dy * g
    return rstd * (dn - jnp.mean(dn, axis=-1, keepdims=True) - n * jnp.mean(dn * n, axis=-1, keepdims=True))


def _head_mean(t, g_ref):
    gw = g_ref.shape[0]
    hi = t.astype(BF16)
    lo = (t - hi.astype(F32)).astype(BF16)
    g = g_ref[...]
    parts = []
    for c in range(t.shape[1] // gw):
        sl = slice(c * gw, (c + 1) * gw)
        parts.append(_dot(hi[:, sl], g) + _dot(lo[:, sl], g))
    out = parts[0] if len(parts) == 1 else jnp.concatenate(parts, axis=1)
    return out * (1.0 / HD)


def _rope(z, c, s1, s2):
    w = z.shape[1]
    half = ROPE_DIMS // 2
    return z * c + pltpu.roll(z, half, 1) * s1 + pltpu.roll(z, w - half, 1) * s2


def _tile_lanes(t, w):
    reps = w // t.shape[1]
    return t if reps == 1 else jnp.concatenate([t] * reps, axis=1)


def _conv_taps(ext, prev, first):
    prev = jnp.where(first, jnp.zeros_like(prev), prev)
    r8 = lax.broadcasted_iota(jnp.int32, (SUBLANES, 1), 0)
    top = ext[0:SUBLANES]
    s1_top = jnp.where(r8 < 1, pltpu.roll(prev, 1, 0), pltpu.roll(top, 1, 0))
    s2_top = jnp.where(r8 < 2, pltpu.roll(prev, 2, 0), pltpu.roll(top, 2, 0))
    s1 = jnp.concatenate([s1_top, pltpu.roll(ext, 1, 0)[SUBLANES:]], axis=0)
    s2 = jnp.concatenate([s2_top, pltpu.roll(ext, 2, 0)[SUBLANES:]], axis=0)
    return s1, s2


def _conv_taps_up(ext, nxt, last):
    tm = ext.shape[0]
    nxt = jnp.where(last, jnp.zeros_like(nxt), nxt)
    r8 = lax.broadcasted_iota(jnp.int32, (SUBLANES, 1), 0)
    bot = ext[tm - SUBLANES:tm]
    u1_bot = jnp.where(r8 >= 7, pltpu.roll(nxt, 7, 0), pltpu.roll(bot, 7, 0))
    u2_bot = jnp.where(r8 >= 6, pltpu.roll(nxt, 6, 0), pltpu.roll(bot, 6, 0))
    u1 = jnp.concatenate([pltpu.roll(ext, tm - 1, 0)[:tm - SUBLANES], u1_bot], axis=0)
    u2 = jnp.concatenate([pltpu.roll(ext, tm - 2, 0)[:tm - SUBLANES], u2_bot], axis=0)
    return u1, u2


def _nt_rows(av, w_ref):
    n = w_ref.shape[0]
    ch = _pick(n, (512, 256, 128))
    ab = av.astype(BF16)
    parts = [_dot_nt(ab, w_ref[c * ch:(c + 1) * ch, :]) for c in range(n // ch)]
    return parts[0] if len(parts) == 1 else jnp.concatenate(parts, axis=1)


def _mm_tn(name, a, b, *, mod=None, tt=512, t2=None, by_chip=False):
    T, k1 = a.shape
    k2 = b.shape[1]
    t1 = k1 if k1 <= 1536 else _pick(k1, (1408, 1024, 512, 256, 128))
    if t2 is None:
        t2 = k2 if k2 <= 1536 else _pick(k2, (1408, 1024, 640, 512, 256, 128))
    wc = k2 // NCHIP
    if by_chip:
        t2 = 2 * wc
    tt = min(tt, S)
    spb = S // tt

    def body(*refs):
        if mod is not None:
            a_ref, sc_ref, sh_ref, b_ref, o_ref = refs
        else:
            a_ref, b_ref, o_ref = refs
        t = pl.program_id(2)

        @pl.when(t == 0)
        def _():
            o_ref[...] = jnp.zeros_like(o_ref)

        av = a_ref[...]
        if mod is not None:
            av = av * (1.0 + sc_ref[0]) + sh_ref[0]
        res = _dot_tn(av.astype(BF16), b_ref[...].astype(BF16))
        if by_chip:
            o_ref[0] += res[:, :wc]
            o_ref[1] += res[:, wc:]
        else:
            o_ref[...] += res

    in_specs = [pl.BlockSpec((tt, t1), lambda p, q, t: (t, p))]
    args = [a]
    if mod is not None:
        for v in mod:
            in_specs.append(pl.BlockSpec((1, 1, t1), lambda p, q, t: (t // spb, 0, p)))
            args.append(v)
    in_specs.append(pl.BlockSpec((tt, t2), lambda p, q, t: (t, q)))
    args.append(b)
    if by_chip:
        out_specs = pl.BlockSpec((2, t1, wc), lambda p, q, t: (q, p, 0))
        out_shape = jax.ShapeDtypeStruct((NCHIP, k1, wc), F32)
    else:
        out_specs = pl.BlockSpec((t1, t2), lambda p, q, t: (p, q))
        out_shape = jax.ShapeDtypeStruct((k1, k2), F32)
    return pl.pallas_call(
        body, name=name, grid=(k1 // t1, k2 // t2, T // tt), in_specs=in_specs, out_specs=out_specs,
        out_shape=out_shape, compiler_params=_cparams(("parallel", "parallel", "arbitrary")),
    )(*args)


def _mod_mm(name, x, sc, sh, ws, out_dtypes, rope=None, rope_secs=(), tm=256):
    T = x.shape[0]
    nw = len(ws)

    def fn(i, xv, *rest):
        if rope is not None:
            cv, s1v, s2v = rest[:3]
            rest = rest[3:]
        scv, shv = rest[:2]
        w_refs = rest[2:]
        h = (xv * (1.0 + scv) + shv).astype(BF16)
        res = []
        for k, w_ref in enumerate(w_refs):
            n = w_ref.shape[1]
            ch = WA if (k == 0 and rope is not None) else _pick(n, (512, 256, 128))
            parts = []
            for c in range(n // ch):
                z = _dot(h, w_ref[:, c * ch:(c + 1) * ch])
                if k == 0 and c in rope_secs:
                    z = _rope(z, _tile_lanes(cv, ch), _tile_lanes(s1v, ch), _tile_lanes(s2v, ch))
                parts.append(z.astype(out_dtypes[k]))
            res.append(parts[0] if len(parts) == 1 else jnp.concatenate(parts, axis=1))
        return tuple(res)

    tiles = (x,) + (tuple(rope) if rope is not None else ())
    outs = tuple((w.shape[1], dt) for w, dt in zip(ws, out_dtypes))
    return _rowwise(name, fn, T, tm, tiles=tiles, seqvecs=(sc, sh), consts=tuple(ws), outs=outs)


def _tri(tb, lower):
    r = lax.broadcasted_iota(jnp.int32, (tb, tb), 0)
    c = lax.broadcasted_iota(jnp.int32, (tb, tb), 1)
    return jnp.where((r >= c) if lower else (r <= c), 1.0, 0.0).astype(BF16)


def _split3(x):
    hi = x.astype(BF16)
    r = x - hi.astype(F32)
    mid = r.astype(BF16)
    return hi, mid, (r - mid.astype(F32)).astype(BF16)


def _cumsum_seq(name, ins, consts, fn_in, fn_out, outs, reverse, n_acc=0, tb=256):
    T = ins[0].shape[0]
    tb = min(tb, S)
    nbs = S // tb
    nseq = T // S
    n_i, n_c, n_o = len(ins), len(consts), len(outs)

    def blk(b, j):
        return (b * nbs + (nbs - 1 - j if reverse else j), 0)

    def body(*refs):
        i_refs, c_refs = refs[:n_i], refs[n_i:n_i + n_c]
        o_refs = refs[n_i + n_c:n_i + n_c + n_o]
        acc_refs = refs[n_i + n_c + n_o:n_i + n_c + n_o + n_acc]
        carry = refs[-1]
        b, j = pl.program_id(0), pl.program_id(1)

        @pl.when(j == 0)
        def _():
            carry[...] = jnp.zeros_like(carry)

        iv = [r[...] for r in i_refs]
        xin = fn_in(*iv, *c_refs)
        tri = _tri(tb, not reverse)
        cum = sum(_dot(tri, piece) for piece in _split3(xin)) + carry[0:1, :]
        carry[...] = carry[...] + jnp.sum(xin, axis=0, keepdims=True)
        res = fn_out(cum, *iv, *c_refs)
        for o, r in zip(o_refs, res):
            o[...] = r.astype(o.dtype)
        for a in acc_refs:
            @pl.when((b == 0) & (j == 0))
            def _():
                a[...] = jnp.zeros_like(a)

            a[...] += _rsum8(res[0])

            @pl.when((b == nseq - 1) & (j == nbs - 1))
            def _():
                a[...] = jnp.broadcast_to(jnp.sum(a[...], axis=0, keepdims=True), a.shape)

    in_specs = [pl.BlockSpec((tb, a.shape[1]), blk) for a in ins]
    in_specs += [pl.BlockSpec(c.shape, lambda b, j, nd=c.ndim: (0,) * nd) for c in consts]
    out_shape = [jax.ShapeDtypeStruct((T, w), dt) for w, dt in outs]
    out_shape += [jax.ShapeDtypeStruct((SUBLANES, outs[0][0]), F32)] * n_acc
    out_specs = [pl.BlockSpec((tb, w), blk) for w, _ in outs]
    out_specs += [pl.BlockSpec((SUBLANES, outs[0][0]), lambda b, j: (0, 0))] * n_acc
    return pl.pallas_call(
        body, name=name, grid=(nseq, nbs), in_specs=in_specs, out_specs=out_specs, out_shape=out_shape,
        scratch_shapes=[pltpu.VMEM((SUBLANES, FPAD), F32)],
        compiler_params=_cparams(("arbitrary", "arbitrary")),
    )(*ins, *consts)


def _log_sigmoid(x):
    return jnp.minimum(x, 0.0) - jnp.log(1.0 + jnp.exp(-jnp.abs(x)))


def _dil_bias(tq):
    max_win = max(w for w, _ in PATTERNS)
    nd = (max_win + tq - 1) // tq + 1
    qi = np.arange(tq)[:, None]
    kj = np.arange(tq)[None, :]
    tabs = []
    for dlt in range(nd):
        dist = dlt * tq + qi - kj
        mult = np.zeros((tq, tq), np.float64)
        for win, dil in PATTERNS:
            mult += (dist >= 0) & (dist % dil == 0) & (dist // dil <= win // dil)
        tabs.append(np.where(mult > 0, np.log(np.maximum(mult, 1.0)), NEG))
    return np.stack(tabs).astype(np.float32)


def _fold_tables(nha):
    hp_n = nha // 2
    pq = np.zeros((3, FPAD, hp_n * 2 * LANES), np.float32)
    pk = np.zeros((3, FPAD, hp_n * LANES), np.float32)
    oq = np.zeros((1, hp_n * 2 * LANES), np.float32)
    ok = np.zeros((1, hp_n * LANES), np.float32)
    sq = np.zeros((hp_n * LANES, FPAD), np.float32)
    sk = np.zeros((hp_n * LANES, FPAD), np.float32)
    for h in range(nha):
        hp, odd = divmod(h, 2)
        qb = hp * 2 * LANES + odd * (LANES + 8)
        kb = hp * LANES + odd * 8
        for i in range(3):
            pq[i, h, qb + i] = 1
            oq[0, qb + 3 + i] = 1
            ok[0, kb + i] = 1
            pk[i, h, kb + 3 + i] = 1
        sq[kb, h] = 1
        sk[kb + 3, h] = 1
    return pq, pk, oq, ok, sq, sk


def _stack_heads(x2, h0, extra=None):
    z = jnp.zeros_like(x2)
    a, b = jnp.where(h0, x2, z), jnp.where(h0, z, x2)
    if extra is not None:
        a = jnp.concatenate([a, extra[:, :LANES]], axis=1)
        b = jnp.concatenate([b, extra[:, LANES:]], axis=1)
    return jnp.concatenate([a, b], axis=0)


def _attn_fwd(name, qkv, secs, fox, eq=None, ek=None, bias=None, ride=()):
    T = qkv.shape[0]
    nq = S // TQ
    nbl = T // S
    hp_n = WA // LANES
    sq, sk, sv = (s * hp_n for s in secs)
    scale = HD ** -0.5
    nd = None if fox else bias.shape[0]

    n_r = len(ride)
    n_in = (5 if fox else 4) + n_r

    def body(*refs):
        if fox:
            q_ref, k_ref, v_ref, eq_ref, ek_ref = refs[:5]
        else:
            q_ref, k_ref, v_ref, b_ref = refs[:4]
        o_ref, lse_ref = refs[n_in:n_in + 2]
        i = pl.program_id(2)
        if n_r:
            ride_start, ride_finish = _gather_halves_steps(
                refs[n_in - n_r:n_in], refs[n_in + 2:n_in + 2 + n_r], *refs[n_in + 2 + n_r:])
            at = lambda b, hp, q: (pl.program_id(0) == b) & (pl.program_id(1) == hp) & (i == q)
            pl.when(at(0, 0, 0))(ride_start)
        lane = lax.broadcasted_iota(jnp.int32, (1, LANES), 1)
        h0 = lane < HD
        q2 = (q_ref[...].astype(F32) * scale).astype(BF16)
        qs = _stack_heads(q2, h0, eq_ref[...] if fox else None)

        def scores(t, diag):
            off = pl.multiple_of((i - t) * TQ, TQ)
            kk = k_ref[pl.ds(off, TQ), :]
            if fox:
                kk = jnp.concatenate([kk, ek_ref[pl.ds(off, TQ), :]], axis=1)
            s = jnp.concatenate([_dot_nt(qs[:TQ], kk), _dot_nt(qs[TQ:], kk)], axis=0)
            if not fox:
                s = (s.reshape(2, TQ, TQ) + b_ref[t]).reshape(2 * TQ, TQ)
            elif diag:
                rows = lax.broadcasted_iota(jnp.int32, (2, TQ, TQ), 1).reshape(2 * TQ, TQ)
                cols = lax.broadcasted_iota(jnp.int32, (2 * TQ, TQ), 1)
                s = jnp.where(cols <= rows, s, NEG)
            return s

        def update(t, s, m, l, acc):
            off = pl.multiple_of((i - t) * TQ, TQ)
            v2 = v_ref[pl.ds(off, TQ), :]
            m_new = jnp.maximum(m, jnp.max(s, axis=1, keepdims=True))
            p = jnp.exp(s - m_new)
            a = jnp.exp(m - m_new)
            l = a * l + jnp.sum(p, axis=1, keepdims=True)
            pb = p.astype(BF16)
            acc = a * acc + jnp.concatenate([_dot(pb[:TQ], v2), _dot(pb[TQ:], v2)], axis=0)
            return m_new, l, acc

        init = (jnp.full((2 * TQ, 1), NEG, F32), jnp.zeros((2 * TQ, 1), F32), jnp.zeros((2 * TQ, LANES), F32))
        n = i + 1 if fox else jnp.minimum(i + 1, nd)
        m, l, acc = update(0, scores(0, True), *init)
        m, l, acc = lax.fori_loop(1, n, lambda t, c: update(t, scores(t, False), *c), (m, l, acc))
        on = acc / l
        o_ref[...] = jnp.where(h0, on[:TQ], on[TQ:])
        lse = jnp.broadcast_to(m + jnp.log(l), (2 * TQ, LANES))
        lse_ref[...] = jnp.concatenate([lse[:TQ], lse[TQ:]], axis=1)
        if n_r:
            pl.when(at(nbl - 1, hp_n - 1, nq - 1))(ride_finish)

    in_specs = [
        pl.BlockSpec((TQ, LANES), lambda b, hp, i: (b * nq + i, sq + hp)),
        pl.BlockSpec((S, LANES), lambda b, hp, i: (b, sk + hp)),
        pl.BlockSpec((S, LANES), lambda b, hp, i: (b, sv + hp)),
    ]
    args = [qkv, qkv, qkv]
    if fox:
        in_specs += [pl.BlockSpec((TQ, 2 * LANES), lambda b, hp, i: (b * nq + i, hp)),
                     pl.BlockSpec((S, LANES), lambda b, hp, i: (b, hp))]
        args += [eq, ek]
    else:
        in_specs.append(pl.BlockSpec(bias.shape, lambda b, hp, i: (0, 0, 0)))
        args.append(bias)
    any_spec = pl.BlockSpec(memory_space=pl.ANY)
    out_specs = [pl.BlockSpec((TQ, LANES), lambda b, hp, i: (b * nq + i, hp)),
                 pl.BlockSpec((TQ, 2 * LANES), lambda b, hp, i: (b * nq + i, hp))] + [any_spec] * n_r
    out_shape = [jax.ShapeDtypeStruct((T, WA), F32), jax.ShapeDtypeStruct((T, 2 * WA), F32)]
    out_shape += [jax.ShapeDtypeStruct((NCHIP - 1,) + v.shape, v.dtype) for v in ride]
    sems = [pltpu.SemaphoreType.DMA((6 * n_r,)), pltpu.SemaphoreType.DMA((6 * n_r,))] if n_r else []
    sem = ("arbitrary",) * 3 if n_r else ("parallel", "parallel", "arbitrary")
    return pl.pallas_call(
        body, name=name, grid=(nbl, hp_n, nq), in_specs=in_specs + [any_spec] * n_r, out_specs=out_specs,
        out_shape=out_shape, scratch_shapes=sems, compiler_params=_cparams(sem),
    )(*args, *ride)


def _attn_bwd(name, qkv, secs, o, do, do_sec, lse, fox, eq=None, ek=None, bias=None, ride=None):
    T = qkv.shape[0]
    nq = S // TQ
    nbl = T // S
    hp_n = WA // LANES
    sq, sk, sv = (s * hp_n for s in secs)
    dsec = do_sec * hp_n
    scale = HD ** -0.5
    nd = None if fox else bias.shape[0]
    kc = 2 * LANES if fox else LANES

    r_in, r_out, r_copies = ride if ride else ((), (), ())
    n_bi, n_bo = (8, 5) if fox else (7, 3)
    n_i = n_bi + len(r_in)

    def body(*refs):
        if fox:
            q_ref, k_ref, v_ref, o_ref, do_ref, lse_ref, eq_ref, ek_ref = refs[:n_bi]
            dq_ref, dk_ref, dv_ref, dqe_ref, dek_ref = refs[n_i:n_i + n_bo]
        else:
            q_ref, k_ref, v_ref, o_ref, do_ref, lse_ref, b_ref = refs[:n_bi]
            dq_ref, dk_ref, dv_ref = refs[n_i:n_i + n_bo]
        dl_ref = refs[n_i + n_bo + len(r_out)]
        j = pl.program_id(2)
        if ride:
            ride_start, ride_finish = _remote_steps(
                refs[n_bi:n_i], refs[n_i + n_bo:n_i + n_bo + len(r_out)], *refs[n_i + n_bo + len(r_out) + 1:], r_copies)
            at = lambda b, hp, q: (pl.program_id(0) == b) & (pl.program_id(1) == hp) & (j == q)
            pl.when(at(0, 0, 0))(ride_start)
        lane = lax.broadcasted_iota(jnp.int32, (1, LANES), 1)
        h0 = lane < HD

        @pl.when(j == 0)
        def _():
            dq_ref[...] = jnp.zeros_like(dq_ref)
            if fox:
                dqe_ref[...] = jnp.zeros_like(dqe_ref)

            def dl_step(r, c):
                off = pl.multiple_of(r * TQ, TQ)
                d2 = do_ref[pl.ds(off, TQ), :] * o_ref[pl.ds(off, TQ), :]
                z2 = jnp.zeros_like(d2)
                dl0 = jnp.sum(jnp.where(h0, d2, z2), axis=1, keepdims=True)
                dl1 = jnp.sum(jnp.where(h0, z2, d2), axis=1, keepdims=True)
                dl_ref[pl.ds(off, TQ), :] = jnp.concatenate(
                    [jnp.broadcast_to(dl0, (TQ, LANES)), jnp.broadcast_to(dl1, (TQ, LANES))], axis=1)
                return c

            lax.fori_loop(0, nq, dl_step, 0)

        kk = k_ref[...]
        if fox:
            kk = jnp.concatenate([kk, ek_ref[...]], axis=1)
        v2 = v_ref[...]

        def wide(x2):
            st = jnp.concatenate([x2[:, :LANES], x2[:, LANES:]], axis=0)
            return st if TQ == LANES else jnp.concatenate([st] * (TQ // LANES), axis=1)

        def step(t, carry, diag):
            dkk, dv2 = carry
            off = pl.multiple_of((j + t) * TQ, TQ)
            q2 = (q_ref[pl.ds(off, TQ), :].astype(F32) * scale).astype(BF16)
            qs = _stack_heads(q2, h0, eq_ref[pl.ds(off, TQ), :] if fox else None)
            dos = _stack_heads(do_ref[pl.ds(off, TQ), :].astype(BF16), h0)
            s = jnp.concatenate([_dot_nt(qs[:TQ], kk), _dot_nt(qs[TQ:], kk)], axis=0)
            if not fox:
                s = (s.reshape(2, TQ, TQ) + b_ref[t]).reshape(2 * TQ, TQ)
            elif diag:
                rows = lax.broadcasted_iota(jnp.int32, (2, TQ, TQ), 1).reshape(2 * TQ, TQ)
                cols = lax.broadcasted_iota(jnp.int32, (2 * TQ, TQ), 1)
                s = jnp.where(cols <= rows, s, NEG)
            p = jnp.exp(s - wide(lse_ref[pl.ds(off, TQ), :]))
            dp = jnp.concatenate([_dot_nt(dos[:TQ], v2), _dot_nt(dos[TQ:], v2)], axis=0)
            dsb = (p * (dp - wide(dl_ref[pl.ds(off, TQ), :]))).astype(BF16)
            dv2 = dv2 + _dot_tn(p.astype(BF16), dos)
            dkk = dkk + _dot_tn(dsb, qs)
            dqq = jnp.concatenate([_dot(dsb[:TQ], kk), _dot(dsb[TQ:], kk)], axis=0)
            dq_ref[pl.ds(off, TQ), :] += jnp.where(h0, dqq[:TQ, :LANES], dqq[TQ:, :LANES])
            if fox:
                dqe_ref[pl.ds(off, TQ), :] += jnp.where(lane < SUBLANES, dqq[:TQ, LANES:], dqq[TQ:, LANES:])
            return dkk, dv2

        zero = (jnp.zeros((TQ, kc), F32), jnp.zeros((TQ, LANES), F32))
        if fox:
            dkk, dv2 = lax.fori_loop(1, nq - j, lambda t, c: step(t, c, False), step(0, zero, True))
        else:
            dkk, dv2 = lax.fori_loop(0, jnp.minimum(nq - j, nd), lambda t, c: step(t, c, False), zero)
        dk_ref[...] = dkk[:, :LANES]
        dv_ref[...] = dv2
        if fox:
            dek_ref[...] = dkk[:, LANES:]

        @pl.when(j == nq - 1)
        def _():
            dq_ref[...] = dq_ref[...] * scale

        if ride:
            pl.when(at(nbl - 1, hp_n - 1, nq - 1))(ride_finish)

    seq = lambda c, w=LANES: pl.BlockSpec((S, w), lambda b, hp, j: (b, c + hp))
    blk = lambda c: pl.BlockSpec((TQ, LANES), lambda b, hp, j: (b * nq + j, c + hp))
    in_specs = [seq(sq), blk(sk), blk(sv), seq(0), seq(dsec), seq(0, 2 * LANES)]
    args = [qkv, qkv, qkv, o, do, lse]
    if fox:
        in_specs += [seq(0, 2 * LANES), blk(0)]
        args += [eq, ek]
    else:
        in_specs.append(pl.BlockSpec(bias.shape, lambda b, hp, j: (0, 0, 0)))
        args.append(bias)
    out_specs = [seq(0), blk(0), blk(0)]
    out_shape = [jax.ShapeDtypeStruct((T, WA), F32)] * 3
    if fox:
        out_specs += [seq(0), blk(0)]
        out_shape += [jax.ShapeDtypeStruct((T, WA), F32)] * 2
    any_spec = pl.BlockSpec(memory_space=pl.ANY)
    scratch = [pltpu.VMEM((S, 2 * LANES), F32)]
    if ride:
        scratch += [pltpu.SemaphoreType.DMA((len(r_copies),)), pltpu.SemaphoreType.DMA((len(r_copies),))]
    sem = ("arbitrary",) * 3 if ride else ("parallel", "parallel", "arbitrary")
    return pl.pallas_call(
        body, name=name, grid=(nbl, hp_n, nq), in_specs=in_specs + [any_spec] * len(r_in),
        out_specs=out_specs + [any_spec] * len(r_out), out_shape=out_shape + list(r_out),
        scratch_shapes=scratch, compiler_params=_cparams(sem),
    )(*args, *r_in)


def _remote_steps(in_refs, out_refs, send_sems, recv_sems, remote):
    me = (lax.axis_index("x"), lax.axis_index("y"), lax.axis_index("c"))

    def peer_of(flip):
        return tuple(1 - v if f else v for v, f in zip(me, flip))

    def at(ref, idx):
        return ref if idx is None else ref.at[idx]

    def rcopy(k, who):
        flip, a, sfn, b, dfn = remote[k]
        return pltpu.make_async_remote_copy(
            src_ref=at(in_refs[a], sfn(*who)), dst_ref=at(out_refs[b], dfn(*who)),
            send_sem=send_sems.at[k], recv_sem=recv_sems.at[k], device_id=peer_of(flip), device_id_type=MESH)

    def start():
        for k in range(len(remote)):
            rcopy(k, me).start()

    def finish():
        for k in range(len(remote)):
            rcopy(k, peer_of(remote[k][0])).wait_recv()
        for k in range(len(remote)):
            rcopy(k, me).wait_send()

    return start, finish


def _exchange(name, ins, out_shapes, remote, local):
    n_in, n_out = len(ins), len(out_shapes)
    nr, nl = len(remote), len(local)

    def body(*refs):
        in_refs = refs[:n_in]
        out_refs = refs[n_in:n_in + n_out]
        send_sems, recv_sems, loc_sems = refs[n_in + n_out:]
        me = (lax.axis_index("x"), lax.axis_index("y"), lax.axis_index("c"))
        at = lambda ref, idx: ref if idx is None else ref.at[idx]
        locs = [pltpu.make_async_copy(at(in_refs[a], sfn(*me)), at(out_refs[b], dfn(*me)), loc_sems.at[k])
                for k, (a, sfn, b, dfn) in enumerate(local)]
        for cp in locs:
            cp.start()
        start, finish = _remote_steps(in_refs, out_refs, send_sems, recv_sems, remote)
        start()
        finish()
        for cp in locs:
            cp.wait()

    any_spec = pl.BlockSpec(memory_space=pl.ANY)
    return pl.pallas_call(
        body, name=name, in_specs=[any_spec] * n_in, out_specs=[any_spec] * n_out, out_shape=list(out_shapes),
        scratch_shapes=[pltpu.SemaphoreType.DMA((max(nr, 1),)), pltpu.SemaphoreType.DMA((max(nr, 1),)),
                        pltpu.SemaphoreType.DMA((max(nl, 1),))],
    )(*ins)


_FLIPS7 = [(0, 0, 1), (0, 1, 0), (0, 1, 1), (1, 0, 0), (1, 0, 1), (1, 1, 0), (1, 1, 1)]
_CHIP_FLIPS = [(1, 0, 0), (0, 1, 0), (1, 1, 0)]


def _dev_index(x, y, c):
    return 4 * x + 2 * y + c


def _chip_index(x, y, c):
    return 2 * x + y


def _all_gather8(name, v):
    remote = [(f, 0, lambda x, y, c: None, 0, _dev_index) for f in _FLIPS7]
    local = [(0, lambda x, y, c: None, 0, _dev_index)]
    return _exchange(name, [v], [jax.ShapeDtypeStruct((NDEV,) + v.shape, v.dtype)], remote, local)[0]


def _gather_halves_steps(in_refs, out_refs, send_sems, recv_sems):
    n_v = len(in_refs)
    x, y, c = lax.axis_index("x"), lax.axis_index("y"), lax.axis_index("c")
    sibling = (x, y, 1 - c)
    chips = [(1 - x, y), (x, 1 - y), (1 - x, 1 - y)]

    def copy(k, n, src, blk, half, to):
        return pltpu.make_async_remote_copy(
            src_ref=src, dst_ref=out_refs[n].at[blk, half], send_sem=send_sems.at[k], recv_sem=recv_sems.at[k],
            device_id=to, device_id_type=MESH)

    def first():
        return [copy(6 * n + j, n, in_refs[n].at[c], j, c, (*chip, c))
                for n in range(n_v) for j, chip in enumerate(chips)]

    def start():
        for cp in first():
            cp.start()

    def finish():
        passed = []
        for n in range(n_v):
            for j, chip in enumerate(chips):
                copy(6 * n + j, n, in_refs[n].at[c], j, c, (*chip, c)).wait_recv()
                fw = copy(6 * n + 3 + j, n, out_refs[n].at[j, c], j, c, sibling)
                fw.start()
                passed.append(fw)
        for n in range(n_v):
            for j in range(len(chips)):
                copy(6 * n + 3 + j, n, out_refs[n].at[j, 1 - c], j, 1 - c, sibling).wait_recv()
        for cp in first() + passed:
            cp.wait_send()

    return start, finish


def _gather_halves(name, vs):
    n_v = len(vs)

    def body(*refs):
        start, finish = _gather_halves_steps(refs[:n_v], refs[n_v:2 * n_v], *refs[2 * n_v:])
        start()
        finish()

    any_spec = pl.BlockSpec(memory_space=pl.ANY)
    return pl.pallas_call(
        body, name=name, in_specs=[any_spec] * n_v, out_specs=[any_spec] * n_v,
        out_shape=[jax.ShapeDtypeStruct((NCHIP - 1,) + v.shape, v.dtype) for v in vs],
        scratch_shapes=[pltpu.SemaphoreType.DMA((6 * n_v,)), pltpu.SemaphoreType.DMA((6 * n_v,))],
    )(*vs)


def _by_chip(own, others, chip):
    stacked = jnp.concatenate([own[None], others], axis=0)
    blocks = []
    for k in range(NCHIP):
        d = k ^ chip
        place = jnp.where(d == 0, 0, jnp.where(d == 2, 1, jnp.where(d == 1, 2, 3)))
        blocks.append(lax.dynamic_index_in_dim(stacked, place, axis=0, keepdims=False))
    return blocks


def _sum_leading(name, v, tm=None):
    n, r, w = v.shape
    tm = _pick(r, (256, 128, 64, 32, 16, 8)) if tm is None else tm

    def body(v_ref, o_ref):
        acc = v_ref[0].astype(F32)
        for k in range(1, n):
            acc = acc + v_ref[k].astype(F32)
        o_ref[...] = acc

    return pl.pallas_call(
        body, name=name, grid=(r // tm,), in_specs=[pl.BlockSpec((n, tm, w), lambda i: (0, i, 0))],
        out_specs=pl.BlockSpec((tm, w), lambda i: (i, 0)), out_shape=jax.ShapeDtypeStruct((r, w), F32),
        compiler_params=_cparams(("parallel",)),
    )(v)


def _add2(name, a, b, tm=None, out_dtype=F32):
    r, w = a.shape
    tm = _pick(r, (256, 128, 64, 32, 16, 8)) if tm is None else tm

    def body(a_ref, b_ref, o_ref):
        o_ref[...] = (a_ref[...] + b_ref[...]).astype(out_dtype)

    spec = pl.BlockSpec((tm, w), lambda i: (i, 0))
    return pl.pallas_call(
        body, name=name, grid=(r // tm,), in_specs=[spec, spec], out_specs=spec,
        out_shape=jax.ShapeDtypeStruct((r, w), out_dtype), compiler_params=_cparams(("parallel",)),
    )(a, b)


def _ada_fwd(call_all, w_shard):
    def body(c_ref, w_ref, o_ref):
        cv = c_ref[...]
        o_ref[...] = jnp.dot(cv * _sigmoid(cv), w_ref[...], preferred_element_type=F32,
                             precision=lax.Precision.HIGHEST)

    n = w_shard.shape[1]
    return pl.pallas_call(
        body, name="ada_fwd", out_shape=jax.ShapeDtypeStruct((call_all.shape[0], n), F32),
        compiler_params=pltpu.CompilerParams(vmem_limit_bytes=VMEM_LIMIT),
    )(call_all, w_shard)


def _ada_bwd(call_all, dada):
    def body(c_ref, d_ref, o_ref):
        cv = c_ref[...]
        o_ref[...] = lax.dot_general(cv * _sigmoid(cv), d_ref[...], (((0,), (0,)), ((), ())),
                                     preferred_element_type=F32, precision=lax.Precision.HIGHEST)

    return pl.pallas_call(
        body, name="ada_bwd", out_shape=jax.ShapeDtypeStruct((call_all.shape[1], dada.shape[1]), F32),
        compiler_params=pltpu.CompilerParams(vmem_limit_bytes=VMEM_LIMIT),
    )(call_all, dada)


def _adamw(name, w, g, m, v):
    r, wd = w.shape
    tm = _pick(r, (256, 128, 64, 32, 16, 8))
    bc1 = 1.0 - ADAM_B1 ** ADAM_STEP
    bc2 = 1.0 - ADAM_B2 ** ADAM_STEP

    def body(w_ref, g_ref, m_ref, v_ref, d_ref, mo_ref, vo_ref):
        gv = g_ref[...]
        mn = ADAM_B1 * m_ref[...] + (1.0 - ADAM_B1) * gv
        vn = ADAM_B2 * v_ref[...] + (1.0 - ADAM_B2) * (gv * gv)
        d_ref[...] = -ADAM_LR * ((mn / bc1) / (jnp.sqrt(vn / bc2) + ADAM_EPS) + ADAM_WD * w_ref[...])
        mo_ref[...] = mn
        vo_ref[...] = vn

    spec = pl.BlockSpec((tm, wd), lambda i: (i, 0))
    return pl.pallas_call(
        body, name=name, grid=(r // tm,), in_specs=[spec] * 4, out_specs=[spec] * 3,
        out_shape=[jax.ShapeDtypeStruct((r, wd), F32)] * 3, compiler_params=_cparams(("parallel",)),
    )(w, g, m, v)


def _rope_tables(positions):
    half = ROPE_DIMS // 2
    freqs = ROPE_THETA ** (-jnp.arange(0, ROPE_DIMS, 2, dtype=F32) / ROPE_DIMS)
    ang = positions.astype(F32).reshape(-1, 1) * freqs
    cos, sin = jnp.cos(ang), jnp.sin(ang)
    T = ang.shape[0]
    one = jnp.ones((T, HD - ROPE_DIMS), F32)
    zero = jnp.zeros((T, HD - ROPE_DIMS), F32)
    zh = jnp.zeros((T, half), F32)
    c64 = jnp.concatenate([cos, cos, one], axis=1)
    s1 = jnp.concatenate([zh, sin, zero], axis=1)
    s2 = jnp.concatenate([-sin, zh, zero], axis=1)
    rep = lambda t: jnp.concatenate([t] * (LANES // HD), axis=1)
    return rep(c64), rep(s1), rep(s2)


def _local_step(x, loss_target, positions, ada, w_qkv, w_f, w_out, w_up, conv_w8, w_down,
                b_fgate, gn, ln1_g, ln1_b, conv_b, ln2_g, ln2_b, late=None, early=None, last=None):
    T = x.shape[0]
    nbl = T // S
    nha = WA // HD
    sv = lambda k: ada[:, k:k + 1, :]
    sh_a, sc_a, g_a, sh_f, sc_f, g_f = (sv(k) for k in range(6))
    rope = _rope_tables(positions)
    neg_rope = (rope[0], -rope[1], -rope[2])
    gseg = jnp.asarray(np.kron(np.eye(min(256, 2 * WA) // HD), np.ones((HD, HD))), BF16)
    bias = jnp.asarray(_dil_bias(TQ))
    bf_pad = jnp.zeros((1, FPAD), F32).at[:, :nha].set(b_fgate)

    qkv, fa = _mod_mm("qkv_proj", x, sc_a, sh_a, (w_qkv, w_f), (BF16, F32), rope=rope, rope_secs=(3, 4))
    pq, pk, oq, ok, sq, sk = _fold_tables(nha)

    def fold_out(cum, f, b_ref, pq_ref, pk_ref, oq_ref, ok_ref):
        hi, mid, lo = _split3(cum)
        eqv = _dot(hi, pq_ref[0]) + _dot(mid, pq_ref[1]) + _dot(lo, pq_ref[2]) + oq_ref[...]
        ekv = ok_ref[...] - (_dot(hi, pk_ref[0]) + _dot(mid, pk_ref[1]) + _dot(lo, pk_ref[2]))
        return eqv, ekv

    eq, ek = _cumsum_seq(
        "fgate_fwd", [fa], [bf_pad, jnp.asarray(pq, BF16), jnp.asarray(pk, BF16), jnp.asarray(oq), jnp.asarray(ok)],
        lambda f, b_ref, *_: _log_sigmoid(f + b_ref[...]), fold_out, ((2 * WA, BF16), (WA, BF16)), reverse=False)
    oa, lse_a, *got = _attn_fwd("fox_fwd", qkv, (0, 1, 2), True, eq=eq, ek=ek, ride=late[0] if late else ())
    if late:
        w_out, w_up, w_down = late[1](got)
    ob, lse_b = _attn_fwd("dil_fwd", qkv, (3, 4, 5), False, bias=bias)

    def mix_fn(i, oav, obv, xv, gav, gn_ref, g_ref, wo_ref, l1g_ref, l1b_ref):
        o = jnp.concatenate([oav, obv], axis=1)
        rs = lax.rsqrt(_head_mean(o * o, g_ref) + RMS_EPS)
        merged = (o * rs * gn_ref[...]).astype(BF16)
        mix = _dot(merged, wo_ref[...])
        x1, _, _ = _ln_fwd(ALPHA * xv + gav * mix, l1g_ref[...], l1b_ref[...])
        return merged, mix, x1

    merged, mix, x1 = _rowwise("mix_out", mix_fn, T, 256, tiles=(oa, ob, x), seqvecs=(g_a,),
                               consts=(gn, gseg, w_out, ln1_g, ln1_b),
                               outs=((2 * WA, BF16), (D, F32), (D, F32)))
    u = _mod_mm("ffn_up", x1, sc_f, sh_f, (w_up,), (F32,))[0]

    def conv_y(i, uv, prev, cw_ref, cb_ref, tm):
        first = (i * tm) % S == 0
        s1, s2 = _conv_taps(uv, prev, first)
        y = cb_ref[...] + cw_ref[0:1, :] * s2 + cw_ref[1:2, :] * s1 + cw_ref[2:3, :] * uv
        return y, s1, s2

    tmc = 128

    def gate_fn(i, uv, prev, cw_ref, cb_ref):
        y, _, _ = conv_y(i, uv, prev, cw_ref, cb_ref, tmc)
        a, g = y[:, :DFF], y[:, DFF:]
        return g * _sigmoid(g) * a, y

    act, yconv = _rowwise("conv_gate", gate_fn, T, tmc, tiles=(u,), halos=((u, -1),), consts=(conv_w8, conv_b),
                          outs=((DFF, BF16), (2 * DFF, F32)))

    def down_fn(i, actv, x1v, tgt, gfv, wd_ref, g2_ref, b2_ref):
        ffn = _dot(actv, wd_ref[...])
        y, n2, rstd = _ln_fwd(ALPHA * x1v + gfv * ffn, g2_ref[...], b2_ref[...])
        err = y - tgt
        dy = err * (1.0 / D)
        dr2 = _ln_bwd(dy, n2, rstd, g2_ref[...])
        return (dr2, gfv * dr2, _rsum8(err * err), _rsum8(dy * n2), _rsum8(dy), _rsum8(dr2 * ffn))

    dr2, dffn, loss_acc, d_ln2g, d_ln2b, d_gf = _rowwise(
        "ffn_down_loss", down_fn, T, 256, tiles=(act, x1, loss_target), seqvecs=(g_f,),
        consts=(w_down, ln2_g, ln2_b), outs=((D, F32), (D, F32)), accs=(D, D, D), seqaccs=(D,))

    def gate_conv_bwd_fn(i, uv, yv, dfv, y_nxt, df_nxt, cw_ref, wd_ref):
        last = ((i + 1) * tmc) % S == 0
        y = jnp.concatenate([yv, y_nxt], axis=0)
        df_ext = jnp.concatenate([dfv, df_nxt], axis=0).astype(BF16)
        ch = _pick(DFF, (256, 128))
        dav = jnp.concatenate([_dot_nt(df_ext, wd_ref[c * ch:(c + 1) * ch, :]) for c in range(DFF // ch)], axis=1)
        a, g = y[:, :DFF], y[:, DFF:]
        sg = _sigmoid(g)
        dyc_ext = jnp.concatenate([dav * (g * sg), dav * a * (sg * (1.0 + g * (1.0 - sg)))], axis=1)
        dyc = dyc_ext[:tmc]
        u1, u2 = _conv_taps_up(dyc, dyc_ext[tmc:], last)
        du_ = cw_ref[2:3, :] * dyc + cw_ref[1:2, :] * u1 + cw_ref[0:1, :] * u2
        return du_, _rsum8(dyc), _rsum8(uv * u2), _rsum8(uv * u1), _rsum8(uv * dyc)

    du, d_cb, d_cw0, d_cw1, d_cw2 = _rowwise(
        "gate_conv_bwd", gate_conv_bwd_fn, T, tmc, tiles=(u, yconv, dffn), halos=((yconv, 1), (dffn, 1)),
        consts=(conv_w8, w_down), outs=((2 * DFF, BF16),), accs=(2 * DFF,) * 4)
    g_w_down = _mm_tn("dw_down", act, dffn)
    g_w_up = _mm_tn("dw_up", x1, du, mod=(sc_f, sh_f), by_chip=True)

    def ln1_bwd_fn(i, dr2v, duv, xv, mixv, x1v, scfv, gav, l1g_ref, wu_ref):
        dh2v = _nt_rows(duv, wu_ref)
        dx1 = ALPHA * dr2v + dh2v * (1.0 + scfv)
        _, n1, rstd = _ln_fwd(ALPHA * xv + gav * mixv, l1g_ref[...], 0.0)
        dr1 = _ln_bwd(dx1, n1, rstd, l1g_ref[...])
        return (dr1, gav * dr1, _rsum8(dx1 * n1), _rsum8(dx1),
                _rsum8(dh2v * x1v), _rsum8(dh2v), _rsum8(dr1 * mixv))

    d_cw = jnp.stack([d_cw0[0], d_cw1[0], d_cw2[0]], axis=0)
    dr1, dmix, d_ln1g, d_ln1b, d_scf, d_shf, d_ga, *swapped = _rowwise(
        "ln1_bwd", ln1_bwd_fn, T, 256, tiles=(dr2, du, x, mix, x1), seqvecs=(sc_f, g_a), consts=(ln1_g, w_up),
        outs=((D, F32), (D, BF16)), accs=(D, D), seqaccs=(D, D, D),
        ride=early[0](g_w_up, g_w_down, d_cw) if early else None)

    g_w_out = _mm_tn("dw_out", merged, dmix)

    def hn_bwd_fn(i, dmixv, oav, obv, gn_ref, g_ref, wo_ref):
        dmv = _nt_rows(dmixv, wo_ref)
        o = jnp.concatenate([oav, obv], axis=1)
        rs = lax.rsqrt(_head_mean(o * o, g_ref) + RMS_EPS)
        nrm = o * rs
        dn = dmv * gn_ref[...]
        do = rs * (dn - nrm * _head_mean(dn * nrm, g_ref))
        return do, _rsum8(dmv * nrm)

    do, d_gn = _rowwise("headnorm_bwd", hn_bwd_fn, T, 256, tiles=(dmix, oa, ob), consts=(gn, gseg, w_out),
                        outs=((2 * WA, F32),), accs=(2 * WA,))
    dqa, dka, dva, dqe, dek, *arrived = _attn_bwd(
        "fox_bwd", qkv, (0, 1, 2), oa, do, 0, lse_a, True, eq=eq, ek=ek, ride=early[1](swapped) if early else None)
    dqb, dkb, dvb = _attn_bwd("dil_bwd", qkv, (3, 4, 5), ob, do, 1, lse_b, False, bias=bias)
    hdot = lambda a, m_ref: sum(_dot(piece, m_ref[...]) for piece in _split3(a))
    dfa, d_bf = _cumsum_seq(
        "fgate_bwd", [dqe, dek, fa], [bf_pad, jnp.asarray(sq, BF16), jnp.asarray(sk, BF16)],
        lambda dq_, dk_, f, b_ref, sq_ref, sk_ref: hdot(dq_, sq_ref) - hdot(dk_, sk_ref),
        lambda cum, dq_, dk_, f, b_ref, sq_ref, sk_ref: (cum * _sigmoid(-(f + b_ref[...])),),
        ((FPAD, F32),), reverse=True, n_acc=1)

    def dz_fn(i, a0, a1, a2, b0, b1, b2, fv, cv, s1v, s2v):
        ct, s1t, s2t = (_tile_lanes(t, WA) for t in (cv, s1v, s2v))
        return jnp.concatenate([a0, a1, a2, _rope(b0, ct, s1t, s2t), _rope(b1, ct, s1t, s2t), b2, fv], axis=1)

    dz = _rowwise("dz_pack", dz_fn, T, 256, tiles=(dqa, dka, dva, dqb, dkb, dvb, dfa) + neg_rope,
                  outs=((6 * WA + FPAD, BF16),))[0]
    w_cat = jnp.concatenate([w_qkv, w_f], axis=1)
    g_w_cat = _mm_tn("dw_in", x, dz, mod=(sc_a, sh_a), tt=256, t2=dz.shape[1])

    def dx_fn(i, dr1v, dzv, xv, scav, wc_ref):
        dh1v = _nt_rows(dzv, wc_ref)
        return ALPHA * dr1v + dh1v * (1.0 + scav), _rsum8(dh1v * xv), _rsum8(dh1v)

    grad_x, d_sca, d_sha, *arrived_last = _rowwise(
        "dx_out", dx_fn, T, 256, tiles=(dr1, dz, x), seqvecs=(sc_a,), consts=(w_cat,), outs=((D, F32),),
        seqaccs=(D, D), ride=last(g_w_cat, g_w_out) if last else None)

    row0 = lambda a: a[..., 0, :]
    d_ada = jnp.stack([row0(d_sha), row0(d_sca), row0(d_ga), row0(d_shf), row0(d_scf), row0(d_gf)], axis=1)
    loss_part = (0.5 / D) * jnp.sum(loss_acc[0])
    small = dict(b_fgate=row0(d_bf)[:nha], gn=row0(d_gn), ln1_g=row0(d_ln1g), ln1_b=row0(d_ln1b),
                 conv_b=row0(d_cb), ln2_g=row0(d_ln2g), ln2_b=row0(d_ln2b))
    big = dict(w_cat=g_w_cat, w_out=g_w_out, w_up=g_w_up, conv_w=d_cw, w_down=g_w_down, early=arrived, last=arrived_last)
    return loss_part, grad_x, d_ada, small, big


def _rows_of(n, w=None):
    return -(-n // (D if w is None else w))


def _as_rows(v):
    w = D
    k = v.shape[0]
    flat = v.reshape(k, -1)
    rows = _rows_of(_rows_of(flat.shape[1], w), SUBLANES) * SUBLANES
    flat = jnp.pad(flat, ((0, 0), (0, rows * w - flat.shape[1])))
    return flat.reshape(k, rows, w)


def kernel(x, c, positions, w_ada, b_ada, w_in, b_fgate, gn_a, gn_b, w_out, ln1_g, ln1_b, w_up, conv_w, conv_b, w_down, ln2_g, ln2_b, loss_target, m_w_ada, m_b_ada, m_w_in, m_b_fgate, m_gn_a, m_gn_b, m_w_out, m_ln1_g, m_ln1_b, m_w_up, m_conv_w, m_conv_b, m_w_down, m_ln2_g, m_ln2_b, v_w_ada, v_b_ada, v_w_in, v_b_fgate, v_gn_a, v_gn_b, v_w_out, v_ln1_g, v_ln1_b, v_w_up, v_conv_w, v_conv_b, v_w_down, v_ln2_g, v_ln2_b):
    mx, my, mc = lax.axis_index("x"), lax.axis_index("y"), lax.axis_index("c")
    dev = _dev_index(mx, my, mc)
    chip = _chip_index(mx, my, mc)
    nbl = x.shape[0]
    T = nbl * S
    nha = WA // HD
    n_ada = w_ada.shape[2]

    c_pad = jnp.zeros((SUBLANES, D), F32).at[:nbl].set(c)
    c_all = _all_gather8("gather_c", c_pad)[:, :nbl].reshape(NDEV * nbl, D)
    ada_part = _ada_fwd(c_all, w_ada[0])
    n_cw = conv_w.shape[2]
    cw_rows = jnp.pad(conv_w[0], ((0, SUBLANES - conv_w.shape[1]), (0, n_ada - n_cw)))
    ada_blocks = _all_gather8("gather_ada", jnp.concatenate([ada_part, cw_rows], axis=0))
    n_c = NDEV * nbl
    ada_all = jnp.concatenate([ada_blocks[2 * k, :n_c] for k in range(NCHIP)], axis=1) + b_ada
    conv_w8 = jnp.concatenate([ada_blocks[2 * k, n_c:, :n_cw] for k in range(NCHIP)], axis=1)
    ada = lax.dynamic_slice_in_dim(ada_all, dev * nbl, nbl, axis=0).reshape(nbl, 6, D)

    w_in_sh = jnp.pad(w_in[0].astype(BF16), ((0, 0), (0, _rows_of(w_in.shape[2], LANES) * LANES - w_in.shape[2])))
    halve = lambda t: t.reshape(2, t.shape[0] // 2, t.shape[1])
    whole = lambda g, t: jnp.stack(_by_chip(halve(t), g, chip)).reshape((NCHIP,) + t.shape)
    w_out_sh, w_up_sh, w_down_sh = w_out[0].astype(BF16), w_up[0].astype(BF16), w_down[0].astype(BF16)
    g_in = whole(_gather_halves("gather_w", [halve(w_in_sh)])[0], w_in_sh)
    w_in_full = jnp.concatenate([g_in[k][:, :w_in.shape[2]] for k in range(NCHIP)], axis=1)
    w_qkv = jnp.concatenate([w_in_full[:, :3 * WA], w_in_full[:, 3 * WA + nha:]], axis=1)
    w_f = jnp.pad(w_in_full[:, 3 * WA:3 * WA + nha], ((0, 0), (0, FPAD - nha)))

    def late_weights(got):
        g_out, g_up, g_down = whole(got[0], w_out_sh), whole(got[1], w_up_sh), whole(got[2], w_down_sh)
        return (g_out.reshape(NCHIP * w_out.shape[1], D), jnp.concatenate([g_up[k] for k in range(NCHIP)], axis=1),
                g_down.reshape(NCHIP * w_down.shape[1], D))

    n_in = w_in.shape[2]
    to_sib = lambda x, y, c: None

    def reduce_swap(blocks):
        cut = [t.reshape(2 * NCHIP, t.shape[1] // 2, t.shape[2]) for t in blocks]
        copies = [((0, 0, 1), n, lambda x, y, c, k=k: 2 * k + 1 - c, n, lambda x, y, c, k=k: k)
                  for n in range(len(cut)) for k in range(NCHIP)]
        return cut, (cut, [jax.ShapeDtypeStruct((NCHIP,) + t.shape[1:], F32) for t in cut], copies)

    def reduce_sum(cut, from_sib, tag):
        flat = lambda v: v.reshape(-1, v.shape[-1])
        pair_sums = []
        for n, (t, fs) in enumerate(zip(cut, from_sib)):
            mine = lax.dynamic_index_in_dim(t.reshape((NCHIP, 2) + t.shape[1:]), mc, axis=1, keepdims=False)
            pair_sums.append(_add2("pair_sum_%s%d" % (tag, n), flat(mine), flat(fs), out_dtype=BF16).reshape(fs.shape))
        copies = []
        for n in range(len(cut)):
            for j, f in enumerate(_CHIP_FLIPS):
                src = lambda x, y, c, f=f: _chip_index(1 - x if f[0] else x, 1 - y if f[1] else y, c)
                copies.append((f, n, src, n, lambda x, y, c, j=j: j))
        shapes = [jax.ShapeDtypeStruct((NCHIP - 1,) + t.shape[1:], BF16) for t in pair_sums]
        return pair_sums, (pair_sums, shapes, copies)

    def reduce_back(pair_sums, arrived, tag):
        my_halves = []
        for n, (ps, got) in enumerate(zip(pair_sums, arrived)):
            own = lax.dynamic_index_in_dim(ps, chip, axis=0, keepdims=True)
            my_halves.append(_sum_leading("chip_sum_%s%d" % (tag, n), jnp.concatenate([own, got], axis=0)))
        sib_halves = _exchange("pair_share_" + tag, my_halves, [jax.ShapeDtypeStruct(t.shape, F32) for t in my_halves],
                               [((0, 0, 1), n, to_sib, n, to_sib) for n in range(len(my_halves))], [])
        whole_blocks = []
        for mh, shf in zip(my_halves, sib_halves):
            pair = jnp.stack([mh, shf])
            whole_blocks.append(jnp.concatenate([lax.dynamic_index_in_dim(pair, mc, axis=0, keepdims=False),
                                                 lax.dynamic_index_in_dim(pair, 1 - mc, axis=0, keepdims=False)], axis=0))
        return whole_blocks

    stash = {}

    def early_swap(g_up, g_down, g_cw):
        sh_cw = _as_rows(g_cw.reshape(conv_w.shape[1], NCHIP, -1).transpose(1, 0, 2))
        rows = w_down.shape[1] + sh_cw.shape[1]
        pad = _rows_of(rows, 2 * LANES) * 2 * LANES - rows
        sh_a = jnp.concatenate([g_down.reshape(NCHIP, -1, D), sh_cw, jnp.zeros((NCHIP, pad, D), F32)], axis=1)
        stash["cut"], ride = reduce_swap([sh_a, g_up])
        return ride

    def early_scatter(from_sib):
        stash["ps"], ride = reduce_sum(stash["cut"], from_sib, "e")
        return ride

    def last_reduce(g_cat, g_out):
        g_w_in_full = jnp.concatenate([g_cat[:, :3 * WA], g_cat[:, 6 * WA:6 * WA + nha], g_cat[:, 3 * WA:6 * WA]],
                                      axis=1)
        sh_in = jnp.pad(g_w_in_full.reshape(D, NCHIP, n_in).transpose(1, 0, 2),
                        ((0, 0), (0, 0), (0, _rows_of(n_in, LANES) * LANES - n_in)))
        cut, swap = reduce_swap([g_out.reshape(NCHIP, -1, D), sh_in])
        stash["ps_l"], ride = reduce_sum(cut, _exchange("pair_swap_l", swap[0], swap[1], swap[2], []), "l")
        return ride

    gn = jnp.concatenate([gn_a, gn_b], axis=1)
    loss_part, grad_x, d_ada, small, big = _local_step(
        x.reshape(T, D), loss_target.reshape(T, D), positions, ada, w_qkv, w_f, None, None, conv_w8, None,
        b_fgate, gn, ln1_g, ln1_b, conv_b, ln2_g, ln2_b,
        late=([halve(w_out_sh), halve(w_up_sh), halve(w_down_sh)], late_weights),
        early=(early_swap, early_scatter), last=last_reduce)

    def row_pad(v, rows):
        flat = v.reshape(-1)
        return jnp.pad(flat, (0, rows * D - flat.shape[0]))

    n_cb = _rows_of(2 * DFF)
    small_flat = jnp.concatenate([
        row_pad(small["b_fgate"], 1), row_pad(small["gn"], 1), row_pad(small["ln1_g"], 1),
        row_pad(small["ln1_b"], 1), row_pad(small["ln2_g"], 1), row_pad(small["ln2_b"], 1),
        row_pad(jnp.full((1,), loss_part, F32), 1), row_pad(small["conv_b"], n_cb)])
    n_small = _rows_of(small_flat.shape[0], SUBLANES * D) * SUBLANES
    small_rows = jnp.pad(small_flat, (0, n_small * D - small_flat.shape[0])).reshape(n_small, D)
    ada_rows = jnp.pad(d_ada.reshape(nbl, 6, D), ((0, 0), (0, SUBLANES - 6), (0, 0))).reshape(nbl * SUBLANES, D)
    gathered = _all_gather8("gather_small", jnp.concatenate([small_rows, ada_rows], axis=0))
    red = _sum_leading("sum_small", gathered, tm=SUBLANES)
    g_b_fgate = red[0:1, :nha]
    g_gn = red[1:2, :2 * WA]
    g_ln1_g, g_ln1_b, g_ln2_g, g_ln2_b = red[2:3], red[3:4], red[4:5], red[5:6]
    loss = red[6, 0]
    g_conv_b = red[7:7 + n_cb].reshape(1, -1)[:, :2 * DFF]
    g_b_ada = _add2("sum_b_ada", red[n_small:n_small + SUBLANES], red[n_small + SUBLANES:n_small + 2 * SUBLANES],
                    tm=SUBLANES)[:6].reshape(1, 6 * D)
    dada_all = gathered[:, n_small:].reshape(NDEV, nbl, SUBLANES, D)[:, :, :6].reshape(NDEV * nbl, 6 * D)
    g_w_ada = _ada_bwd(c_all, lax.dynamic_slice_in_dim(dada_all, chip * n_ada, n_ada, axis=1))

    shards_e = reduce_back(stash["ps"], big["early"], "e")
    shards_l = reduce_back(stash["ps_l"], big["last"], "l")
    r0 = w_down.shape[1]
    g_w_down = shards_e[0][:r0]
    g_conv_w = shards_e[0][r0:r0 + SUBLANES].reshape(-1)[:int(np.prod(conv_w.shape[1:]))].reshape(conv_w.shape[1:])
    g_w_up = shards_e[1]
    g_w_out = shards_l[0]
    g_w_in = shards_l[1][:, :n_in]

    grads = dict(w_ada=g_w_ada, b_ada=g_b_ada, w_in=g_w_in, b_fgate=g_b_fgate, gn_a=g_gn[:, :WA], gn_b=g_gn[:, WA:],
                 w_out=g_w_out, ln1_g=g_ln1_g, ln1_b=g_ln1_b, w_up=g_w_up, conv_w=g_conv_w, conv_b=g_conv_b,
                 w_down=g_w_down, ln2_g=g_ln2_g, ln2_b=g_ln2_b)
    weights = dict(w_ada=w_ada, b_ada=b_ada, w_in=w_in, b_fgate=b_fgate, gn_a=gn_a, gn_b=gn_b, w_out=w_out,
                   ln1_g=ln1_g, ln1_b=ln1_b, w_up=w_up, conv_w=conv_w, conv_b=conv_b, w_down=w_down,
                   ln2_g=ln2_g, ln2_b=ln2_b)
    ms = dict(w_ada=m_w_ada, b_ada=m_b_ada, w_in=m_w_in, b_fgate=m_b_fgate, gn_a=m_gn_a, gn_b=m_gn_b,
              w_out=m_w_out, ln1_g=m_ln1_g, ln1_b=m_ln1_b, w_up=m_w_up, conv_w=m_conv_w, conv_b=m_conv_b,
              w_down=m_w_down, ln2_g=m_ln2_g, ln2_b=m_ln2_b)
    vs = dict(w_ada=v_w_ada, b_ada=v_b_ada, w_in=v_w_in, b_fgate=v_b_fgate, gn_a=v_gn_a, gn_b=v_gn_b,
              w_out=v_w_out, ln1_g=v_ln1_g, ln1_b=v_ln1_b, w_up=v_w_up, conv_w=v_conv_w, conv_b=v_conv_b,
              w_down=v_w_down, ln2_g=v_ln2_g, ln2_b=v_ln2_b)
    names = list(weights)
    big_names = ("w_ada", "w_in", "w_out", "w_up", "w_down")
    delta, new_m, new_v = {}, {}, {}
    for n in big_names:
        shp = weights[n].shape
        d, m2, v2 = _adamw("adamw_" + n, weights[n][0], grads[n].reshape(shp[1:]), ms[n][0], vs[n][0])
        delta[n], new_m[n], new_v[n] = d.reshape(shp), m2.reshape(shp), v2.reshape(shp)
    small_names = [n for n in names if n not in big_names]

    def pack_small(src):
        flats = []
        for n in small_names:
            flat = src[n].reshape(-1)
            flats.append(jnp.pad(flat, (0, _rows_of(flat.shape[0]) * D - flat.shape[0])))
        allf = jnp.concatenate(flats)
        rows = _rows_of(allf.shape[0], SUBLANES * D) * SUBLANES
        return jnp.pad(allf, (0, rows * D - allf.shape[0])).reshape(rows, D)

    sd, sm, sv_ = _adamw("adamw_small", pack_small(weights), pack_small(grads), pack_small(ms), pack_small(vs))
    off = 0
    for n in small_names:
        shp = weights[n].shape
        cnt = int(np.prod(shp))
        r = _rows_of(cnt)
        for dst, src in ((delta, sd), (new_m, sm), (new_v, sv_)):
            dst[n] = src[off:off + r].reshape(-1)[:cnt].reshape(shp)
        off += r

    out_g = {n: grads[n].reshape(weights[n].shape) for n in names}
    return (loss, grad_x.reshape(x.shape), *[out_g[n] for n in names], *[delta[n] for n in names],
            *[new_m[n] for n in names], *[new_v[n] for n in names])
```

```python
import numpy as np
import jax
import jax.numpy as jnp
from jax import lax
from jax.experimental import pallas as pl
from jax.experimental.pallas import tpu as pltpu

F32 = jnp.float32
BF16 = jnp.bfloat16

D = 1024
S = 4096
HD = 64
WA = 512
DFF = 2816
NCHIP = 4
NDEV = 8
PATTERNS = ((128, 1), (512, 4), (2048, 16))
ROPE_THETA = 500000.0
ROPE_DIMS = HD // 4
ALPHA = (2.0 * 1) ** 0.25
LN_EPS = 1e-5
RMS_EPS = 1e-6
ADAM_LR = 0.001
ADAM_B1 = 0.9
ADAM_B2 = 0.999
ADAM_EPS = 1e-08
ADAM_WD = 0.01
ADAM_STEP = 10

LANES = 128
SUBLANES = 8
TQ = 512
FPAD = LANES
NEG = -1e30
VMEM_LIMIT = 56 * 1024 * 1024
MESH = pl.DeviceIdType.MESH


def _cparams(sem):
    return pltpu.CompilerParams(dimension_semantics=sem, vmem_limit_bytes=VMEM_LIMIT)


def _pick(n, cands):
    for c in cands:
        if n % c == 0:
            return c
    return n


def _rsum8(v):
    tm, w = v.shape
    return jnp.sum(v.reshape(tm // SUBLANES, SUBLANES, w), axis=0)


def _sigmoid(x):
    return 1.0 / (1.0 + jnp.exp(-x))


def _dot(a, b):
    return jnp.dot(a, b, preferred_element_type=F32)


def _dot_nt(a, b):
    return lax.dot_general(a, b, (((1,), (1,)), ((), ())), preferred_element_type=F32)


def _dot_tn(a, b):
    return lax.dot_general(a, b, (((0,), (0,)), ((), ())), preferred_element_type=F32)


def _rowwise(name, fn, T, tm, *, tiles=(), halos=(), seqvecs=(), consts=(), outs=(), accs=(), seqaccs=(),
             seq_len=None, ride=None):
    seq_len = S if seq_len is None else seq_len
    nb = T // tm
    spb = max(seq_len // tm, 1)
    nseq = max(T // seq_len, 1)
    n8 = T // SUBLANES
    r8 = tm // SUBLANES
    in_specs, args = [], []
    for a in tiles:
        in_specs.append(pl.BlockSpec((tm, a.shape[1]), lambda i: (i, 0)))
        args.append(a)
    for a, direction in halos:
        if direction < 0:
            idx = lambda i: (jnp.maximum(i * r8 - 1, 0), 0)
        else:
            idx = lambda i: (jnp.minimum((i + 1) * r8, n8 - 1), 0)
        in_specs.append(pl.BlockSpec((SUBLANES, a.shape[1]), idx))
        args.append(a)
    for a in seqvecs:
        in_specs.append(pl.BlockSpec((1, 1, a.shape[2]), lambda i: (i // spb, 0, 0)))
        args.append(a)
    for a in consts:
        in_specs.append(pl.BlockSpec(a.shape, lambda i, nd=a.ndim: (0,) * nd))
        args.append(a)
    out_shape, out_specs = [], []
    for w, dt in outs:
        out_shape.append(jax.ShapeDtypeStruct((T, w), dt))
        out_specs.append(pl.BlockSpec((tm, w), lambda i: (i, 0)))
    for w in accs:
        out_shape.append(jax.ShapeDtypeStruct((SUBLANES, w), F32))
        out_specs.append(pl.BlockSpec((SUBLANES, w), lambda i: (0, 0)))
    for w in seqaccs:
        out_shape.append(jax.ShapeDtypeStruct((nseq, SUBLANES, w), F32))
        out_specs.append(pl.BlockSpec((1, SUBLANES, w), lambda i: (i // spb, 0, 0)))
    n_t, n_h, n_s, n_c = len(tiles), len(halos), len(seqvecs), len(consts)
    n_o, n_a, n_sa = len(outs), len(accs), len(seqaccs)
    r_in, r_out, r_copies = ride if ride else ((), (), ())
    n_bi, n_bo = n_t + n_h + n_s + n_c, n_o + n_a + n_sa

    def body(*refs):
        i = pl.program_id(0)
        ins = refs[:n_bi]
        orefs = refs[n_bi + len(r_in):n_bi + len(r_in) + n_bo]
        if ride:
            ride_start, ride_finish = _remote_steps(
                refs[n_bi:n_bi + len(r_in)], refs[n_bi + len(r_in) + n_bo:n_bi + len(r_in) + n_bo + len(r_out)],
                *refs[n_bi + len(r_in) + n_bo + len(r_out):], r_copies)
            pl.when(i == 0)(ride_start)
        vals = [r[...] for r in ins[:n_t + n_h]]
        vals += [r[0] for r in ins[n_t + n_h:n_t + n_h + n_s]]
        vals += list(ins[n_t + n_h + n_s:])
        res = fn(i, *vals)
        if not isinstance(res, (tuple, list)):
            res = (res,)
        for k in range(n_o):
            orefs[k][...] = res[k].astype(orefs[k].dtype)
        for k in range(n_a):
            r = orefs[n_o + k]

            @pl.when(i == 0)
            def _():
                r[...] = jnp.zeros_like(r)

            r[...] += res[n_o + k]

            @pl.when(i == nb - 1)
            def _():
                r[...] = jnp.broadcast_to(jnp.sum(r[...], axis=0, keepdims=True), r.shape)
        for k in range(n_sa):
            r = orefs[n_o + n_a + k]

            @pl.when(i % spb == 0)
            def _():
                r[...] = jnp.zeros_like(r)

            r[0] += res[n_o + n_a + k]

            @pl.when(i % spb == spb - 1)
            def _():
                r[0] = jnp.broadcast_to(jnp.sum(r[0], axis=0, keepdims=True), r.shape[1:])
        if ride:
            pl.when(i == nb - 1)(ride_finish)

    sem = ("arbitrary",) if (n_a or n_sa or ride) else ("parallel",)
    any_spec = pl.BlockSpec(memory_space=pl.ANY)
    scratch = [pltpu.SemaphoreType.DMA((len(r_copies),)), pltpu.SemaphoreType.DMA((len(r_copies),))] if ride else []
    return pl.pallas_call(
        body, name=name, grid=(nb,), in_specs=in_specs + [any_spec] * len(r_in),
        out_specs=out_specs + [any_spec] * len(r_out), out_shape=out_shape + list(r_out),
        scratch_shapes=scratch, compiler_params=_cparams(sem),
    )(*args, *r_in)


def _ln_fwd(r, g, b):
    mu = jnp.mean(r, axis=-1, keepdims=True)
    xc = r - mu
    var = jnp.mean(xc * xc, axis=-1, keepdims=True)
    rstd = lax.rsqrt(var + LN_EPS)
    n = xc * rstd
    return n * g + b, n, rstd


def _ln_bwd(dy, n, rstd, g):
    dn = dy * g
    return rstd * (dn - jnp.mean(dn, axis=-1, keepdims=True) - n * jnp.mean(dn * n, axis=-1, keepdims=True))


def _head_mean(t, g_ref):
    gw = g_ref.shape[0]
    hi = t.astype(BF16)
    lo = (t - hi.astype(F32)).astype(BF16)
    g = g_ref[...]
    parts = []
    for c in range(t.shape[1] // gw):
        sl = slice(c * gw, (c + 1) * gw)
        parts.append(_dot(hi[:, sl], g) + _dot(lo[:, sl], g))
    out = parts[0] if len(parts) == 1 else jnp.concatenate(parts, axis=1)
    return out * (1.0 / HD)


def _rope(z, c, s1, s2):
    w = z.shape[1]
    half = ROPE_DIMS // 2
    return z * c + pltpu.roll(z, half, 1) * s1 + pltpu.roll(z, w - half, 1) * s2


def _tile_lanes(t, w):
    reps = w // t.shape[1]
    return t if reps == 1 else jnp.concatenate([t] * reps, axis=1)


def _conv_taps(ext, prev, first):
    prev = jnp.where(first, jnp.zeros_like(prev), prev)
    r8 = lax.broadcasted_iota(jnp.int32, (SUBLANES, 1), 0)
    top = ext[0:SUBLANES]
    s1_top = jnp.where(r8 < 1, pltpu.roll(prev, 1, 0), pltpu.roll(top, 1, 0))
    s2_top = jnp.where(r8 < 2, pltpu.roll(prev, 2, 0), pltpu.roll(top, 2, 0))
    s1 = jnp.concatenate([s1_top, pltpu.roll(ext, 1, 0)[SUBLANES:]], axis=0)
    s2 = jnp.concatenate([s2_top, pltpu.roll(ext, 2, 0)[SUBLANES:]], axis=0)
    return s1, s2


def _conv_taps_up(ext, nxt, last):
    tm = ext.shape[0]
    nxt = jnp.where(last, jnp.zeros_like(nxt), nxt)
    r8 = lax.broadcasted_iota(jnp.int32, (SUBLANES, 1), 0)
    bot = ext[tm - SUBLANES:tm]
    u1_bot = jnp.where(r8 >= 7, pltpu.roll(nxt, 7, 0), pltpu.roll(bot, 7, 0))
    u2_bot = jnp.where(r8 >= 6, pltpu.roll(nxt, 6, 0), pltpu.roll(bot, 6, 0))
    u1 = jnp.concatenate([pltpu.roll(ext, tm - 1, 0)[:tm - SUBLANES], u1_bot], axis=0)
    u2 = jnp.concatenate([pltpu.roll(ext, tm - 2, 0)[:tm - SUBLANES], u2_bot], axis=0)
    return u1, u2


def _nt_rows(av, w_ref):
    n = w_ref.shape[0]
    ch = _pick(n, (512, 256, 128))
    ab = av.astype(BF16)
    parts = [_dot_nt(ab, w_ref[c * ch:(c + 1) * ch, :]) for c in range(n // ch)]
    return parts[0] if len(parts) == 1 else jnp.concatenate(parts, axis=1)


def _mm_tn(name, a, b, *, mod=None, tt=512, t2=None, by_chip=False):
    T, k1 = a.shape
    k2 = b.shape[1]
    t1 = k1 if k1 <= 1536 else _pick(k1, (1408, 1024, 512, 256, 128))
    if t2 is None:
        t2 = k2 if k2 <= 1536 else _pick(k2, (1408, 1024, 640, 512, 256, 128))
    wc = k2 // NCHIP
    if by_chip:
        t2 = 2 * wc
    tt = min(tt, S)
    spb = S // tt

    def body(*refs):
        if mod is not None:
            a_ref, sc_ref, sh_ref, b_ref, o_ref = refs
        else:
            a_ref, b_ref, o_ref = refs
        t = pl.program_id(2)

        @pl.when(t == 0)
        def _():
            o_ref[...] = jnp.zeros_like(o_ref)

        av = a_ref[...]
        if mod is not None:
            av = av * (1.0 + sc_ref[0]) + sh_ref[0]
        res = _dot_tn(av.astype(BF16), b_ref[...].astype(BF16))
        if by_chip:
            o_ref[0] += res[:, :wc]
            o_ref[1] += res[:, wc:]
        else:
            o_ref[...] += res

    in_specs = [pl.BlockSpec((tt, t1), lambda p, q, t: (t, p))]
    args = [a]
    if mod is not None:
        for v in mod:
            in_specs.append(pl.BlockSpec((1, 1, t1), lambda p, q, t: (t // spb, 0, p)))
            args.append(v)
    in_specs.append(pl.BlockSpec((tt, t2), lambda p, q, t: (t, q)))
    args.append(b)
    if by_chip:
        out_specs = pl.BlockSpec((2, t1, wc), lambda p, q, t: (q, p, 0))
        out_shape = jax.ShapeDtypeStruct((NCHIP, k1, wc), F32)
    else:
        out_specs = pl.BlockSpec((t1, t2), lambda p, q, t: (p, q))
        out_shape = jax.ShapeDtypeStruct((k1, k2), F32)
    return pl.pallas_call(
        body, name=name, grid=(k1 // t1, k2 // t2, T // tt), in_specs=in_specs, out_specs=out_specs,
        out_shape=out_shape, compiler_params=_cparams(("parallel", "parallel", "arbitrary")),
    )(*args)


def _mod_mm(name, x, sc, sh, ws, out_dtypes, rope=None, rope_secs=(), tm=256):
    T = x.shape[0]
    nw = len(ws)

    def fn(i, xv, *rest):
        if rope is not None:
            cv, s1v, s2v = rest[:3]
            rest = rest[3:]
        scv, shv = rest[:2]
        w_refs = rest[2:]
        h = (xv * (1.0 + scv) + shv).astype(BF16)
        res = []
        for k, w_ref in enumerate(w_refs):
            n = w_ref.shape[1]
            ch = WA if (k == 0 and rope is not None) else _pick(n, (512, 256, 128))
            parts = []
            for c in range(n // ch):
                z = _dot(h, w_ref[:, c * ch:(c + 1) * ch])
                if k == 0 and c in rope_secs:
                    z = _rope(z, _tile_lanes(cv, ch), _tile_lanes(s1v, ch), _tile_lanes(s2v, ch))
                parts.append(z.astype(out_dtypes[k]))
            res.append(parts[0] if len(parts) == 1 else jnp.concatenate(parts, axis=1))
        return tuple(res)

    tiles = (x,) + (tuple(rope) if rope is not None else ())
    outs = tuple((w.shape[1], dt) for w, dt in zip(ws, out_dtypes))
    return _rowwise(name, fn, T, tm, tiles=tiles, seqvecs=(sc, sh), consts=tuple(ws), outs=outs)


def _tri(tb, lower):
    r = lax.broadcasted_iota(jnp.int32, (tb, tb), 0)
    c = lax.broadcasted_iota(jnp.int32, (tb, tb), 1)
    return jnp.where((r >= c) if lower else (r <= c), 1.0, 0.0).astype(BF16)


def _split3(x):
    hi = x.astype(BF16)
    r = x - hi.astype(F32)
    mid = r.astype(BF16)
    return hi, mid, (r - mid.astype(F32)).astype(BF16)


def _cumsum_seq(name, ins, consts, fn_in, fn_out, outs, reverse, n_acc=0, tb=256):
    T = ins[0].shape[0]
    tb = min(tb, S)
    nbs = S // tb
    nseq = T // S
    n_i, n_c, n_o = len(ins), len(consts), len(outs)

    def blk(b, j):
        return (b * nbs + (nbs - 1 - j if reverse else j), 0)

    def body(*refs):
        i_refs, c_refs = refs[:n_i], refs[n_i:n_i + n_c]
        o_refs = refs[n_i + n_c:n_i + n_c + n_o]
        acc_refs = refs[n_i + n_c + n_o:n_i + n_c + n_o + n_acc]
        carry = refs[-1]
        b, j = pl.program_id(0), pl.program_id(1)

        @pl.when(j == 0)
        def _():
            carry[...] = jnp.zeros_like(carry)

        iv = [r[...] for r in i_refs]
        xin = fn_in(*iv, *c_refs)
        tri = _tri(tb, not reverse)
        cum = sum(_dot(tri, piece) for piece in _split3(xin)) + carry[0:1, :]
        carry[...] = carry[...] + jnp.sum(xin, axis=0, keepdims=True)
        res = fn_out(cum, *iv, *c_refs)
        for o, r in zip(o_refs, res):
            o[...] = r.astype(o.dtype)
        for a in acc_refs:
            @pl.when((b == 0) & (j == 0))
            def _():
                a[...] = jnp.zeros_like(a)

            a[...] += _rsum8(res[0])

            @pl.when((b == nseq - 1) & (j == nbs - 1))
            def _():
                a[...] = jnp.broadcast_to(jnp.sum(a[...], axis=0, keepdims=True), a.shape)

    in_specs = [pl.BlockSpec((tb, a.shape[1]), blk) for a in ins]
    in_specs += [pl.BlockSpec(c.shape, lambda b, j, nd=c.ndim: (0,) * nd) for c in consts]
    out_shape = [jax.ShapeDtypeStruct((T, w), dt) for w, dt in outs]
    out_shape += [jax.ShapeDtypeStruct((SUBLANES, outs[0][0]), F32)] * n_acc
    out_specs = [pl.BlockSpec((tb, w), blk) for w, _ in outs]
    out_specs += [pl.BlockSpec((SUBLANES, outs[0][0]), lambda b, j: (0, 0))] * n_acc
    return pl.pallas_call(
        body, name=name, grid=(nseq, nbs), in_specs=in_specs, out_specs=out_specs, out_shape=out_shape,
        scratch_shapes=[pltpu.VMEM((SUBLANES, FPAD), F32)],
        compiler_params=_cparams(("arbitrary", "arbitrary")),
    )(*ins, *consts)


def _log_sigmoid(x):
    return jnp.minimum(x, 0.0) - jnp.log(1.0 + jnp.exp(-jnp.abs(x)))


def _dil_bias(tq):
    max_win = max(w for w, _ in PATTERNS)
    nd = (max_win + tq - 1) // tq + 1
    qi = np.arange(tq)[:, None]
    kj = np.arange(tq)[None, :]
    tabs = []
    for dlt in range(nd):
        dist = dlt * tq + qi - kj
        mult = np.zeros((tq, tq), np.float64)
        for win, dil in PATTERNS:
            mult += (dist >= 0) & (dist % dil == 0) & (dist // dil <= win // dil)
        tabs.append(np.where(mult > 0, np.log(np.maximum(mult, 1.0)), NEG))
    return np.stack(tabs).astype(np.float32)


def _fold_tables(nha):
    hp_n = nha // 2
    pq = np.zeros((3, FPAD, hp_n * 2 * LANES), np.float32)
    pk = np.zeros((3, FPAD, hp_n * LANES), np.float32)
    oq = np.zeros((1, hp_n * 2 * LANES), np.float32)
    ok = np.zeros((1, hp_n * LANES), np.float32)
    sq = np.zeros((hp_n * LANES, FPAD), np.float32)
    sk = np.zeros((hp_n * LANES, FPAD), np.float32)
    for h in range(nha):
        hp, odd = divmod(h, 2)
        qb = hp * 2 * LANES + odd * (LANES + 8)
        kb = hp * LANES + odd * 8
        for i in range(3):
            pq[i, h, qb + i] = 1
            oq[0, qb + 3 + i] = 1
            ok[0, kb + i] = 1
            pk[i, h, kb + 3 + i] = 1
        sq[kb, h] = 1
        sk[kb + 3, h] = 1
    return pq, pk, oq, ok, sq, sk


def _stack_heads(x2, h0, extra=None):
    z = jnp.zeros_like(x2)
    a, b = jnp.where(h0, x2, z), jnp.where(h0, z, x2)
    if extra is not None:
        a = jnp.concatenate([a, extra[:, :LANES]], axis=1)
        b = jnp.concatenate([b, extra[:, LANES:]], axis=1)
    return jnp.concatenate([a, b], axis=0)


def _attn_fwd(name, qkv, secs, fox, eq=None, ek=None, bias=None, ride=()):
    T = qkv.shape[0]
    nq = S // TQ
    nbl = T // S
    hp_n = WA // LANES
    sq, sk, sv = (s * hp_n for s in secs)
    scale = HD ** -0.5
    nd = None if fox else bias.shape[0]

    n_r = len(ride)
    n_in = (5 if fox else 4) + n_r

    def body(*refs):
        if fox:
            q_ref, k_ref, v_ref, eq_ref, ek_ref = refs[:5]
        else:
            q_ref, k_ref, v_ref, b_ref = refs[:4]
        o_ref, lse_ref = refs[n_in:n_in + 2]
        i = pl.program_id(2)
        if n_r:
            ride_start, ride_finish = _gather_halves_steps(
                refs[n_in - n_r:n_in], refs[n_in + 2:n_in + 2 + n_r], *refs[n_in + 2 + n_r:])
            at = lambda b, hp, q: (pl.program_id(0) == b) & (pl.program_id(1) == hp) & (i == q)
            pl.when(at(0, 0, 0))(ride_start)
        lane = lax.broadcasted_iota(jnp.int32, (1, LANES), 1)
        h0 = lane < HD
        q2 = (q_ref[...].astype(F32) * scale).astype(BF16)
        qs = _stack_heads(q2, h0, eq_ref[...] if fox else None)

        def scores(t, diag):
            off = pl.multiple_of((i - t) * TQ, TQ)
            kk = k_ref[pl.ds(off, TQ), :]
            if fox:
                kk = jnp.concatenate([kk, ek_ref[pl.ds(off, TQ), :]], axis=1)
            s = jnp.concatenate([_dot_nt(qs[:TQ], kk), _dot_nt(qs[TQ:], kk)], axis=0)
            if not fox:
                s = (s.reshape(2, TQ, TQ) + b_ref[t]).reshape(2 * TQ, TQ)
            elif diag:
                rows = lax.broadcasted_iota(jnp.int32, (2, TQ, TQ), 1).reshape(2 * TQ, TQ)
                cols = lax.broadcasted_iota(jnp.int32, (2 * TQ, TQ), 1)
                s = jnp.where(cols <= rows, s, NEG)
            return s

        def update(t, s, m, l, acc):
            off = pl.multiple_of((i - t) * TQ, TQ)
            v2 = v_ref[pl.ds(off, TQ), :]
            m_new = jnp.maximum(m, jnp.max(s, axis=1, keepdims=True))
            p = jnp.exp(s - m_new)
            a = jnp.exp(m - m_new)
            l = a * l + jnp.sum(p, axis=1, keepdims=True)
            pb = p.astype(BF16)
            acc = a * acc + jnp.concatenate([_dot(pb[:TQ], v2), _dot(pb[TQ:], v2)], axis=0)
            return m_new, l, acc

        init = (jnp.full((2 * TQ, 1), NEG, F32), jnp.zeros((2 * TQ, 1), F32), jnp.zeros((2 * TQ, LANES), F32))
        n = i + 1 if fox else jnp.minimum(i + 1, nd)
        m, l, acc = update(0, scores(0, True), *init)
        m, l, acc = lax.fori_loop(1, n, lambda t, c: update(t, scores(t, False), *c), (m, l, acc))
        on = acc / l
        o_ref[...] = jnp.where(h0, on[:TQ], on[TQ:])
        lse = jnp.broadcast_to(m + jnp.log(l), (2 * TQ, LANES))
        lse_ref[...] = jnp.concatenate([lse[:TQ], lse[TQ:]], axis=1)
        if n_r:
            pl.when(at(nbl - 1, hp_n - 1, nq - 1))(ride_finish)

    in_specs = [
        pl.BlockSpec((TQ, LANES), lambda b, hp, i: (b * nq + i, sq + hp)),
        pl.BlockSpec((S, LANES), lambda b, hp, i: (b, sk + hp)),
        pl.BlockSpec((S, LANES), lambda b, hp, i: (b, sv + hp)),
    ]
    args = [qkv, qkv, qkv]
    if fox:
        in_specs += [pl.BlockSpec((TQ, 2 * LANES), lambda b, hp, i: (b * nq + i, hp)),
                     pl.BlockSpec((S, LANES), lambda b, hp, i: (b, hp))]
        args += [eq, ek]
    else:
        in_specs.append(pl.BlockSpec(bias.shape, lambda b, hp, i: (0, 0, 0)))
        args.append(bias)
    any_spec = pl.BlockSpec(memory_space=pl.ANY)
    out_specs = [pl.BlockSpec((TQ, LANES), lambda b, hp, i: (b * nq + i, hp)),
                 pl.BlockSpec((TQ, 2 * LANES), lambda b, hp, i: (b * nq + i, hp))] + [any_spec] * n_r
    out_shape = [jax.ShapeDtypeStruct((T, WA), F32), jax.ShapeDtypeStruct((T, 2 * WA), F32)]
    out_shape += [jax.ShapeDtypeStruct((NCHIP - 1,) + v.shape, v.dtype) for v in ride]
    sems = [pltpu.SemaphoreType.DMA((6 * n_r,)), pltpu.SemaphoreType.DMA((6 * n_r,))] if n_r else []
    sem = ("arbitrary",) * 3 if n_r else ("parallel", "parallel", "arbitrary")
    return pl.pallas_call(
        body, name=name, grid=(nbl, hp_n, nq), in_specs=in_specs + [any_spec] * n_r, out_specs=out_specs,
        out_shape=out_shape, scratch_shapes=sems, compiler_params=_cparams(sem),
    )(*args, *ride)


def _attn_bwd(name, qkv, secs, o, do, do_sec, lse, fox, eq=None, ek=None, bias=None, ride=None):
    T = qkv.shape[0]
    nq = S // TQ
    nbl = T // S
    hp_n = WA // LANES
    sq, sk, sv = (s * hp_n for s in secs)
    dsec = do_sec * hp_n
    scale = HD ** -0.5
    nd = None if fox else bias.shape[0]
    kc = 2 * LANES if fox else LANES

    r_in, r_out, r_copies = ride if ride else ((), (), ())
    n_bi, n_bo = (8, 5) if fox else (7, 3)
    n_i = n_bi + len(r_in)

    def body(*refs):
        if fox:
            q_ref, k_ref, v_ref, o_ref, do_ref, lse_ref, eq_ref, ek_ref = refs[:n_bi]
            dq_ref, dk_ref, dv_ref, dqe_ref, dek_ref = refs[n_i:n_i + n_bo]
        else:
            q_ref, k_ref, v_ref, o_ref, do_ref, lse_ref, b_ref = refs[:n_bi]
            dq_ref, dk_ref, dv_ref = refs[n_i:n_i + n_bo]
        dl_ref = refs[n_i + n_bo + len(r_out)]
        j = pl.program_id(2)
        if ride:
            ride_start, ride_finish = _remote_steps(
                refs[n_bi:n_i], refs[n_i + n_bo:n_i + n_bo + len(r_out)], *refs[n_i + n_bo + len(r_out) + 1:], r_copies)
            at = lambda b, hp, q: (pl.program_id(0) == b) & (pl.program_id(1) == hp) & (j == q)
            pl.when(at(0, 0, 0))(ride_start)
        lane = lax.broadcasted_iota(jnp.int32, (1, LANES), 1)
        h0 = lane < HD

        @pl.when(j == 0)
        def _():
            dq_ref[...] = jnp.zeros_like(dq_ref)
            if fox:
                dqe_ref[...] = jnp.zeros_like(dqe_ref)

            def dl_step(r, c):
                off = pl.multiple_of(r * TQ, TQ)
                d2 = do_ref[pl.ds(off, TQ), :] * o_ref[pl.ds(off, TQ), :]
                z2 = jnp.zeros_like(d2)
                dl0 = jnp.sum(jnp.where(h0, d2, z2), axis=1, keepdims=True)
                dl1 = jnp.sum(jnp.where(h0, z2, d2), axis=1, keepdims=True)
                dl_ref[pl.ds(off, TQ), :] = jnp.concatenate(
                    [jnp.broadcast_to(dl0, (TQ, LANES)), jnp.broadcast_to(dl1, (TQ, LANES))], axis=1)
                return c

            lax.fori_loop(0, nq, dl_step, 0)

        kk = k_ref[...]
        if fox:
            kk = jnp.concatenate([kk, ek_ref[...]], axis=1)
        v2 = v_ref[...]

        def wide(x2):
            st = jnp.concatenate([x2[:, :LANES], x2[:, LANES:]], axis=0)
            return st if TQ == LANES else jnp.concatenate([st] * (TQ // LANES), axis=1)

        def step(t, carry, diag):
            dkk, dv2 = carry
            off = pl.multiple_of((j + t) * TQ, TQ)
            q2 = (q_ref[pl.ds(off, TQ), :].astype(F32) * scale).astype(BF16)
            qs = _stack_heads(q2, h0, eq_ref[pl.ds(off, TQ), :] if fox else None)
            dos = _stack_heads(do_ref[pl.ds(off, TQ), :].astype(BF16), h0)
            s = jnp.concatenate([_dot_nt(qs[:TQ], kk), _dot_nt(qs[TQ:], kk)], axis=0)
            if not fox:
                s = (s.reshape(2, TQ, TQ) + b_ref[t]).reshape(2 * TQ, TQ)
            elif diag:
                rows = lax.broadcasted_iota(jnp.int32, (2, TQ, TQ), 1).reshape(2 * TQ, TQ)
                cols = lax.broadcasted_iota(jnp.int32, (2 * TQ, TQ), 1)
                s = jnp.where(cols <= rows, s, NEG)
            p = jnp.exp(s - wide(lse_ref[pl.ds(off, TQ), :]))
            dp = jnp.concatenate([_dot_nt(dos[:TQ], v2), _dot_nt(dos[TQ:], v2)], axis=0)
            dsb = (p * (dp - wide(dl_ref[pl.ds(off, TQ), :]))).astype(BF16)
            dv2 = dv2 + _dot_tn(p.astype(BF16), dos)
            dkk = dkk + _dot_tn(dsb, qs)
            dqq = jnp.concatenate([_dot(dsb[:TQ], kk), _dot(dsb[TQ:], kk)], axis=0)
            dq_ref[pl.ds(off, TQ), :] += jnp.where(h0, dqq[:TQ, :LANES], dqq[TQ:, :LANES])
            if fox:
                dqe_ref[pl.ds(off, TQ), :] += jnp.where(lane < SUBLANES, dqq[:TQ, LANES:], dqq[TQ:, LANES:])
            return dkk, dv2

        zero = (jnp.zeros((TQ, kc), F32), jnp.zeros((TQ, LANES), F32))
        if fox:
            dkk, dv2 = lax.fori_loop(1, nq - j, lambda t, c: step(t, c, False), step(0, zero, True))
        else:
            dkk, dv2 = lax.fori_loop(0, jnp.minimum(nq - j, nd), lambda t, c: step(t, c, False), zero)
        dk_ref[...] = dkk[:, :LANES]
        dv_ref[...] = dv2
        if fox:
            dek_ref[...] = dkk[:, LANES:]

        @pl.when(j == nq - 1)
        def _():
            dq_ref[...] = dq_ref[...] * scale

        if ride:
            pl.when(at(nbl - 1, hp_n - 1, nq - 1))(ride_finish)

    seq = lambda c, w=LANES: pl.BlockSpec((S, w), lambda b, hp, j: (b, c + hp))
    blk = lambda c: pl.BlockSpec((TQ, LANES), lambda b, hp, j: (b * nq + j, c + hp))
    in_specs = [seq(sq), blk(sk), blk(sv), seq(0), seq(dsec), seq(0, 2 * LANES)]
    args = [qkv, qkv, qkv, o, do, lse]
    if fox:
        in_specs += [seq(0, 2 * LANES), blk(0)]
        args += [eq, ek]
    else:
        in_specs.append(pl.BlockSpec(bias.shape, lambda b, hp, j: (0, 0, 0)))
        args.append(bias)
    out_specs = [seq(0), blk(0), blk(0)]
    out_shape = [jax.ShapeDtypeStruct((T, WA), F32)] * 3
    if fox:
        out_specs += [seq(0), blk(0)]
        out_shape += [jax.ShapeDtypeStruct((T, WA), F32)] * 2
    any_spec = pl.BlockSpec(memory_space=pl.ANY)
    scratch = [pltpu.VMEM((S, 2 * LANES), F32)]
    if ride:
        scratch += [pltpu.SemaphoreType.DMA((len(r_copies),)), pltpu.SemaphoreType.DMA((len(r_copies),))]
    sem = ("arbitrary",) * 3 if ride else ("parallel", "parallel", "arbitrary")
    return pl.pallas_call(
        body, name=name, grid=(nbl, hp_n, nq), in_specs=in_specs + [any_spec] * len(r_in),
        out_specs=out_specs + [any_spec] * len(r_out), out_shape=out_shape + list(r_out),
        scratch_shapes=scratch, compiler_params=_cparams(sem),
    )(*args, *r_in)


def _remote_steps(in_refs, out_refs, send_sems, recv_sems, remote):
    me = (lax.axis_index("x"), lax.axis_index("y"), lax.axis_index("c"))

    def peer_of(flip):
        return tuple(1 - v if f else v for v, f in zip(me, flip))

    def at(ref, idx):
        return ref if idx is None else ref.at[idx]

    def rcopy(k, who):
        flip, a, sfn, b, dfn = remote[k]
        return pltpu.make_async_remote_copy(
            src_ref=at(in_refs[a], sfn(*who)), dst_ref=at(out_refs[b], dfn(*who)),
            send_sem=send_sems.at[k], recv_sem=recv_sems.at[k], device_id=peer_of(flip), device_id_type=MESH)

    def start():
        for k in range(len(remote)):
            rcopy(k, me).start()

    def finish():
        for k in range(len(remote)):
            rcopy(k, peer_of(remote[k][0])).wait_recv()
        for k in range(len(remote)):
            rcopy(k, me).wait_send()

    return start, finish


def _exchange(name, ins, out_shapes, remote, local):
    n_in, n_out = len(ins), len(out_shapes)
    nr, nl = len(remote), len(local)

    def body(*refs):
        in_refs = refs[:n_in]
        out_refs = refs[n_in:n_in + n_out]
        send_sems, recv_sems, loc_sems = refs[n_in + n_out:]
        me = (lax.axis_index("x"), lax.axis_index("y"), lax.axis_index("c"))
        at = lambda ref, idx: ref if idx is None else ref.at[idx]
        locs = [pltpu.make_async_copy(at(in_refs[a], sfn(*me)), at(out_refs[b], dfn(*me)), loc_sems.at[k])
                for k, (a, sfn, b, dfn) in enumerate(local)]
        for cp in locs:
            cp.start()
        start, finish = _remote_steps(in_refs, out_refs, send_sems, recv_sems, remote)
        start()
        finish()
        for cp in locs:
            cp.wait()

    any_spec = pl.BlockSpec(memory_space=pl.ANY)
    return pl.pallas_call(
        body, name=name, in_specs=[any_spec] * n_in, out_specs=[any_spec] * n_out, out_shape=list(out_shapes),
        scratch_shapes=[pltpu.SemaphoreType.DMA((max(nr, 1),)), pltpu.SemaphoreType.DMA((max(nr, 1),)),
                        pltpu.SemaphoreType.DMA((max(nl, 1),))],
    )(*ins)


_FLIPS7 = [(0, 0, 1), (0, 1, 0), (0, 1, 1), (1, 0, 0), (1, 0, 1), (1, 1, 0), (1, 1, 1)]
_CHIP_FLIPS = [(1, 0, 0), (0, 1, 0), (1, 1, 0)]


def _dev_index(x, y, c):
    return 4 * x + 2 * y + c


def _chip_index(x, y, c):
    return 2 * x + y


def _all_gather8(name, v):
    remote = [(f, 0, lambda x, y, c: None, 0, _dev_index) for f in _FLIPS7]
    local = [(0, lambda x, y, c: None, 0, _dev_index)]
    return _exchange(name, [v], [jax.ShapeDtypeStruct((NDEV,) + v.shape, v.dtype)], remote, local)[0]


def _gather_halves_steps(in_refs, out_refs, send_sems, recv_sems):
    n_v = len(in_refs)
    x, y, c = lax.axis_index("x"), lax.axis_index("y"), lax.axis_index("c")
    sibling = (x, y, 1 - c)
    chips = [(1 - x, y), (x, 1 - y), (1 - x, 1 - y)]

    def copy(k, n, src, blk, half, to):
        return pltpu.make_async_remote_copy(
            src_ref=src, dst_ref=out_refs[n].at[blk, half], send_sem=send_sems.at[k], recv_sem=recv_sems.at[k],
            device_id=to, device_id_type=MESH)

    def first():
        return [copy(6 * n + j, n, in_refs[n].at[c], j, c, (*chip, c))
                for n in range(n_v) for j, chip in enumerate(chips)]

    def start():
        for cp in first():
            cp.start()

    def finish():
        passed = []
        for n in range(n_v):
            for j, chip in enumerate(chips):
                copy(6 * n + j, n, in_refs[n].at[c], j, c, (*chip, c)).wait_recv()
                fw = copy(6 * n + 3 + j, n, out_refs[n].at[j, c], j, c, sibling)
                fw.start()
                passed.append(fw)
        for n in range(n_v):
            for j in range(len(chips)):
                copy(6 * n + 3 + j, n, out_refs[n].at[j, 1 - c], j, 1 - c, sibling).wait_recv()
        for cp in first() + passed:
            cp.wait_send()

    return start, finish


def _gather_halves(name, vs):
    n_v = len(vs)

    def body(*refs):
        start, finish = _gather_halves_steps(refs[:n_v], refs[n_v:2 * n_v], *refs[2 * n_v:])
        start()
        finish()

    any_spec = pl.BlockSpec(memory_space=pl.ANY)
    return pl.pallas_call(
        body, name=name, in_specs=[any_spec] * n_v, out_specs=[any_spec] * n_v,
        out_shape=[jax.ShapeDtypeStruct((NCHIP - 1,) + v.shape, v.dtype) for v in vs],
        scratch_shapes=[pltpu.SemaphoreType.DMA((6 * n_v,)), pltpu.SemaphoreType.DMA((6 * n_v,))],
    )(*vs)


def _by_chip(own, others, chip):
    stacked = jnp.concatenate([own[None], others], axis=0)
    blocks = []
    for k in range(NCHIP):
        d = k ^ chip
        place = jnp.where(d == 0, 0, jnp.where(d == 2, 1, jnp.where(d == 1, 2, 3)))
        blocks.append(lax.dynamic_index_in_dim(stacked, place, axis=0, keepdims=False))
    return blocks


def _sum_leading(name, v, tm=None):
    n, r, w = v.shape
    tm = _pick(r, (256, 128, 64, 32, 16, 8)) if tm is None else tm

    def body(v_ref, o_ref):
        acc = v_ref[0].astype(F32)
        for k in range(1, n):
            acc = acc + v_ref[k].astype(F32)
        o_ref[...] = acc

    return pl.pallas_call(
        body, name=name, grid=(r // tm,), in_specs=[pl.BlockSpec((n, tm, w), lambda i: (0, i, 0))],
        out_specs=pl.BlockSpec((tm, w), lambda i: (i, 0)), out_shape=jax.ShapeDtypeStruct((r, w), F32),
        compiler_params=_cparams(("parallel",)),
    )(v)


def _add2(name, a, b, tm=None, out_dtype=F32):
    r, w = a.shape
    tm = _pick(r, (256, 128, 64, 32, 16, 8)) if tm is None else tm

    def body(a_ref, b_ref, o_ref):
        o_ref[...] = (a_ref[...] + b_ref[...]).astype(out_dtype)

    spec = pl.BlockSpec((tm, w), lambda i: (i, 0))
    return pl.pallas_call(
        body, name=name, grid=(r // tm,), in_specs=[spec, spec], out_specs=spec,
        out_shape=jax.ShapeDtypeStruct((r, w), out_dtype), compiler_params=_cparams(("parallel",)),
    )(a, b)


def _ada_fwd(call_all, w_shard):
    def body(c_ref, w_ref, o_ref):
        cv = c_ref[...]
        o_ref[...] = jnp.dot(cv * _sigmoid(cv), w_ref[...], preferred_element_type=F32,
                             precision=lax.Precision.HIGHEST)

    n = w_shard.shape[1]
    return pl.pallas_call(
        body, name="ada_fwd", out_shape=jax.ShapeDtypeStruct((call_all.shape[0], n), F32),
        compiler_params=pltpu.CompilerParams(vmem_limit_bytes=VMEM_LIMIT),
    )(call_all, w_shard)


def _ada_bwd(call_all, dada):
    def body(c_ref, d_ref, o_ref):
        cv = c_ref[...]
        o_ref[...] = lax.dot_general(cv * _sigmoid(cv), d_ref[...], (((0,), (0,)), ((), ())),
                                     preferred_element_type=F32, precision=lax.Precision.HIGHEST)

    return pl.pallas_call(
        body, name="ada_bwd", out_shape=jax.ShapeDtypeStruct((call_all.shape[1], dada.shape[1]), F32),
        compiler_params=pltpu.CompilerParams(vmem_limit_bytes=VMEM_LIMIT),
    )(call_all, dada)


def _adamw(name, w, g, m, v):
    r, wd = w.shape
    tm = _pick(r, (256, 128, 64, 32, 16, 8))
    bc1 = 1.0 - ADAM_B1 ** ADAM_STEP
    bc2 = 1.0 - ADAM_B2 ** ADAM_STEP

    def body(w_ref, g_ref, m_ref, v_ref, d_ref, mo_ref, vo_ref):
        gv = g_ref[...]
        mn = ADAM_B1 * m_ref[...] + (1.0 - ADAM_B1) * gv
        vn = ADAM_B2 * v_ref[...] + (1.0 - ADAM_B2) * (gv * gv)
        d_ref[...] = -ADAM_LR * ((mn / bc1) / (jnp.sqrt(vn / bc2) + ADAM_EPS) + ADAM_WD * w_ref[...])
        mo_ref[...] = mn
        vo_ref[...] = vn

    spec = pl.BlockSpec((tm, wd), lambda i: (i, 0))
    return pl.pallas_call(
        body, name=name, grid=(r // tm,), in_specs=[spec] * 4, out_specs=[spec] * 3,
        out_shape=[jax.ShapeDtypeStruct((r, wd), F32)] * 3, compiler_params=_cparams(("parallel",)),
    )(w, g, m, v)


def _rope_tables(positions):
    half = ROPE_DIMS // 2
    freqs = ROPE_THETA ** (-jnp.arange(0, ROPE_DIMS, 2, dtype=F32) / ROPE_DIMS)
    ang = positions.astype(F32).reshape(-1, 1) * freqs
    cos, sin = jnp.cos(ang), jnp.sin(ang)
    T = ang.shape[0]
    one = jnp.ones((T, HD - ROPE_DIMS), F32)
    zero = jnp.zeros((T, HD - ROPE_DIMS), F32)
    zh = jnp.zeros((T, half), F32)
    c64 = jnp.concatenate([cos, cos, one], axis=1)
    s1 = jnp.concatenate([zh, sin, zero], axis=1)
    s2 = jnp.concatenate([-sin, zh, zero], axis=1)
    rep = lambda t: jnp.concatenate([t] * (LANES // HD), axis=1)
    return rep(c64), rep(s1), rep(s2)


def _local_step(x, loss_target, positions, ada, w_qkv, w_f, w_out, w_up, conv_w8, w_down,
                b_fgate, gn, ln1_g, ln1_b, conv_b, ln2_g, ln2_b, late=None, early=None, last=None):
    T = x.shape[0]
    nbl = T // S
    nha = WA // HD
    sv = lambda k: ada[:, k:k + 1, :]
    sh_a, sc_a, g_a, sh_f, sc_f, g_f = (sv(k) for k in range(6))
    rope = _rope_tables(positions)
    neg_rope = (rope[0], -rope[1], -rope[2])
    gseg = jnp.asarray(np.kron(np.eye(min(256, 2 * WA) // HD), np.ones((HD, HD))), BF16)
    bias = jnp.asarray(_dil_bias(TQ))
    bf_pad = jnp.zeros((1, FPAD), F32).at[:, :nha].set(b_fgate)

    qkv, fa = _mod_mm("qkv_proj", x, sc_a, sh_a, (w_qkv, w_f), (BF16, F32), rope=rope, rope_secs=(3, 4))
    pq, pk, oq, ok, sq, sk = _fold_tables(nha)

    def fold_out(cum, f, b_ref, pq_ref, pk_ref, oq_ref, ok_ref):
        hi, mid, lo = _split3(cum)
        eqv = _dot(hi, pq_ref[0]) + _dot(mid, pq_ref[1]) + _dot(lo, pq_ref[2]) + oq_ref[...]
        ekv = ok_ref[...] - (_dot(hi, pk_ref[0]) + _dot(mid, pk_ref[1]) + _dot(lo, pk_ref[2]))
        return eqv, ekv

    eq, ek = _cumsum_seq(
        "fgate_fwd", [fa], [bf_pad, jnp.asarray(pq, BF16), jnp.asarray(pk, BF16), jnp.asarray(oq), jnp.asarray(ok)],
        lambda f, b_ref, *_: _log_sigmoid(f + b_ref[...]), fold_out, ((2 * WA, BF16), (WA, BF16)), reverse=False)
    oa, lse_a, *got = _attn_fwd("fox_fwd", qkv, (0, 1, 2), True, eq=eq, ek=ek, ride=late[0] if late else ())
    if late:
        w_out, w_up, w_down = late[1](got)
    ob, lse_b = _attn_fwd("dil_fwd", qkv, (3, 4, 5), False, bias=bias)

    def mix_fn(i, oav, obv, xv, gav, gn_ref, g_ref, wo_ref, l1g_ref, l1b_ref):
        o = jnp.concatenate([oav, obv], axis=1)
        rs = lax.rsqrt(_head_mean(o * o, g_ref) + RMS_EPS)
        merged = (o * rs * gn_ref[...]).astype(BF16)
        mix = _dot(merged, wo_ref[...])
        x1, _, _ = _ln_fwd(ALPHA * xv + gav * mix, l1g_ref[...], l1b_ref[...])
        return merged, mix, x1

    merged, mix, x1 = _rowwise("mix_out", mix_fn, T, 256, tiles=(oa, ob, x), seqvecs=(g_a,),
                               consts=(gn, gseg, w_out, ln1_g, ln1_b),
                               outs=((2 * WA, BF16), (D, F32), (D, F32)))
    u = _mod_mm("ffn_up", x1, sc_f, sh_f, (w_up,), (F32,))[0]

    def conv_y(i, uv, prev, cw_ref, cb_ref, tm):
        first = (i * tm) % S == 0
        s1, s2 = _conv_taps(uv, prev, first)
        y = cb_ref[...] + cw_ref[0:1, :] * s2 + cw_ref[1:2, :] * s1 + cw_ref[2:3, :] * uv
        return y, s1, s2

    tmc = 128

    def gate_fn(i, uv, prev, cw_ref, cb_ref):
        y, _, _ = conv_y(i, uv, prev, cw_ref, cb_ref, tmc)
        a, g = y[:, :DFF], y[:, DFF:]
        return g * _sigmoid(g) * a, y

    act, yconv = _rowwise("conv_gate", gate_fn, T, tmc, tiles=(u,), halos=((u, -1),), consts=(conv_w8, conv_b),
                          outs=((DFF, BF16), (2 * DFF, F32)))

    def down_fn(i, actv, x1v, tgt, gfv, wd_ref, g2_ref, b2_ref):
        ffn = _dot(actv, wd_ref[...])
        y, n2, rstd = _ln_fwd(ALPHA * x1v + gfv * ffn, g2_ref[...], b2_ref[...])
        err = y - tgt
        dy = err * (1.0 / D)
        dr2 = _ln_bwd(dy, n2, rstd, g2_ref[...])
        return (dr2, gfv * dr2, _rsum8(err * err), _rsum8(dy * n2), _rsum8(dy), _rsum8(dr2 * ffn))

    dr2, dffn, loss_acc, d_ln2g, d_ln2b, d_gf = _rowwise(
        "ffn_down_loss", down_fn, T, 256, tiles=(act, x1, loss_target), seqvecs=(g_f,),
        consts=(w_down, ln2_g, ln2_b), outs=((D, F32), (D, F32)), accs=(D, D, D), seqaccs=(D,))

    def gate_conv_bwd_fn(i, uv, yv, dfv, y_nxt, df_nxt, cw_ref, wd_ref):
        last = ((i + 1) * tmc) % S == 0
        y = jnp.concatenate([yv, y_nxt], axis=0)
        df_ext = jnp.concatenate([dfv, df_nxt], axis=0).astype(BF16)
        ch = _pick(DFF, (256, 128))
        dav = jnp.concatenate([_dot_nt(df_ext, wd_ref[c * ch:(c + 1) * ch, :]) for c in range(DFF // ch)], axis=1)
        a, g = y[:, :DFF], y[:, DFF:]
        sg = _sigmoid(g)
        dyc_ext = jnp.concatenate([dav * (g * sg), dav * a * (sg * (1.0 + g * (1.0 - sg)))], axis=1)
        dyc = dyc_ext[:tmc]
        u1, u2 = _conv_taps_up(dyc, dyc_ext[tmc:], last)
        du_ = cw_ref[2:3, :] * dyc + cw_ref[1:2, :] * u1 + cw_ref[0:1, :] * u2
        return du_, _rsum8(dyc), _rsum8(uv * u2), _rsum8(uv * u1), _rsum8(uv * dyc)

    du, d_cb, d_cw0, d_cw1, d_cw2 = _rowwise(
        "gate_conv_bwd", gate_conv_bwd_fn, T, tmc, tiles=(u, yconv, dffn), halos=((yconv, 1), (dffn, 1)),
        consts=(conv_w8, w_down), outs=((2 * DFF, BF16),), accs=(2 * DFF,) * 4)
    g_w_down = _mm_tn("dw_down", act, dffn)
    g_w_up = _mm_tn("dw_up", x1, du, mod=(sc_f, sh_f), by_chip=True)

    def ln1_bwd_fn(i, dr2v, duv, xv, mixv, x1v, scfv, gav, l1g_ref, wu_ref):
        dh2v = _nt_rows(duv, wu_ref)
        dx1 = ALPHA * dr2v + dh2v * (1.0 + scfv)
        _, n1, rstd = _ln_fwd(ALPHA * xv + gav * mixv, l1g_ref[...], 0.0)
        dr1 = _ln_bwd(dx1, n1, rstd, l1g_ref[...])
        return (dr1, gav * dr1, _rsum8(dx1 * n1), _rsum8(dx1),
                _rsum8(dh2v * x1v), _rsum8(dh2v), _rsum8(dr1 * mixv))

    d_cw = jnp.stack([d_cw0[0], d_cw1[0], d_cw2[0]], axis=0)
    dr1, dmix, d_ln1g, d_ln1b, d_scf, d_shf, d_ga, *swapped = _rowwise(
        "ln1_bwd", ln1_bwd_fn, T, 256, tiles=(dr2, du, x, mix, x1), seqvecs=(sc_f, g_a), consts=(ln1_g, w_up),
        outs=((D, F32), (D, BF16)), accs=(D, D), seqaccs=(D, D, D),
        ride=early[0](g_w_up, g_w_down, d_cw) if early else None)

    g_w_out = _mm_tn("dw_out", merged, dmix)

    def hn_bwd_fn(i, dmixv, oav, obv, gn_ref, g_ref, wo_ref):
        dmv = _nt_rows(dmixv, wo_ref)
        o = jnp.concatenate([oav, obv], axis=1)
        rs = lax.rsqrt(_head_mean(o * o, g_ref) + RMS_EPS)
        nrm = o * rs
        dn = dmv * gn_ref[...]
        do = rs * (dn - nrm * _head_mean(dn * nrm, g_ref))
        return do, _rsum8(dmv * nrm)

    do, d_gn = _rowwise("headnorm_bwd", hn_bwd_fn, T, 256, tiles=(dmix, oa, ob), consts=(gn, gseg, w_out),
                        outs=((2 * WA, F32),), accs=(2 * WA,))
    dqa, dka, dva, dqe, dek, *arrived = _attn_bwd(
        "fox_bwd", qkv, (0, 1, 2), oa, do, 0, lse_a, True, eq=eq, ek=ek, ride=early[1](swapped) if early else None)
    dqb, dkb, dvb, *shared = _attn_bwd("dil_bwd", qkv, (3, 4, 5), ob, do, 1, lse_b, False, bias=bias,
                                       ride=early[2](arrived) if early else None)
    hdot = lambda a, m_ref: sum(_dot(piece, m_ref[...]) for piece in _split3(a))
    dfa, d_bf = _cumsum_seq(
        "fgate_bwd", [dqe, dek, fa], [bf_pad, jnp.asarray(sq, BF16), jnp.asarray(sk, BF16)],
        lambda dq_, dk_, f, b_ref, sq_ref, sk_ref: hdot(dq_, sq_ref) - hdot(dk_, sk_ref),
        lambda cum, dq_, dk_, f, b_ref, sq_ref, sk_ref: (cum * _sigmoid(-(f + b_ref[...])),),
        ((FPAD, F32),), reverse=True, n_acc=1)

    def dz_fn(i, a0, a1, a2, b0, b1, b2, fv, cv, s1v, s2v):
        ct, s1t, s2t = (_tile_lanes(t, WA) for t in (cv, s1v, s2v))
        return jnp.concatenate([a0, a1, a2, _rope(b0, ct, s1t, s2t), _rope(b1, ct, s1t, s2t), b2, fv], axis=1)

    dz = _rowwise("dz_pack", dz_fn, T, 256, tiles=(dqa, dka, dva, dqb, dkb, dvb, dfa) + neg_rope,
                  outs=((6 * WA + FPAD, BF16),))[0]
    w_cat = jnp.concatenate([w_qkv, w_f], axis=1)
    g_w_cat = _mm_tn("dw_in", x, dz, mod=(sc_a, sh_a), tt=256, t2=dz.shape[1])

    def dx_fn(i, dr1v, dzv, xv, scav, wc_ref):
        dh1v = _nt_rows(dzv, wc_ref)
        return ALPHA * dr1v + dh1v * (1.0 + scav), _rsum8(dh1v * xv), _rsum8(dh1v)

    grad_x, d_sca, d_sha, *arrived_last = _rowwise(
        "dx_out", dx_fn, T, 256, tiles=(dr1, dz, x), seqvecs=(sc_a,), consts=(w_cat,), outs=((D, F32),),
        seqaccs=(D, D), ride=last(g_w_cat, g_w_out) if last else None)

    row0 = lambda a: a[..., 0, :]
    d_ada = jnp.stack([row0(d_sha), row0(d_sca), row0(d_ga), row0(d_shf), row0(d_scf), row0(d_gf)], axis=1)
    loss_part = (0.5 / D) * jnp.sum(loss_acc[0])
    small = dict(b_fgate=row0(d_bf)[:nha], gn=row0(d_gn), ln1_g=row0(d_ln1g), ln1_b=row0(d_ln1b),
                 conv_b=row0(d_cb), ln2_g=row0(d_ln2g), ln2_b=row0(d_ln2b))
    big = dict(w_cat=g_w_cat, w_out=g_w_out, w_up=g_w_up, conv_w=d_cw, w_down=g_w_down, early=shared, last=arrived_last)
    return loss_part, grad_x, d_ada, small, big


def _rows_of(n, w=None):
    return -(-n // (D if w is None else w))


def _as_rows(v):
    w = D
    k = v.shape[0]
    flat = v.reshape(k, -1)
    rows = _rows_of(_rows_of(flat.shape[1], w), SUBLANES) * SUBLANES
    flat = jnp.pad(flat, ((0, 0), (0, rows * w - flat.shape[1])))
    return flat.reshape(k, rows, w)


def kernel(x, c, positions, w_ada, b_ada, w_in, b_fgate, gn_a, gn_b, w_out, ln1_g, ln1_b, w_up, conv_w, conv_b, w_down, ln2_g, ln2_b, loss_target, m_w_ada, m_b_ada, m_w_in, m_b_fgate, m_gn_a, m_gn_b, m_w_out, m_ln1_g, m_ln1_b, m_w_up, m_conv_w, m_conv_b, m_w_down, m_ln2_g, m_ln2_b, v_w_ada, v_b_ada, v_w_in, v_b_fgate, v_gn_a, v_gn_b, v_w_out, v_ln1_g, v_ln1_b, v_w_up, v_conv_w, v_conv_b, v_w_down, v_ln2_g, v_ln2_b):
    mx, my, mc = lax.axis_index("x"), lax.axis_index("y"), lax.axis_index("c")
    dev = _dev_index(mx, my, mc)
    chip = _chip_index(mx, my, mc)
    nbl = x.shape[0]
    T = nbl * S
    nha = WA // HD
    n_ada = w_ada.shape[2]

    c_pad = jnp.zeros((SUBLANES, D), F32).at[:nbl].set(c)
    c_all = _all_gather8("gather_c", c_pad)[:, :nbl].reshape(NDEV * nbl, D)
    ada_part = _ada_fwd(c_all, w_ada[0])
    n_cw = conv_w.shape[2]
    cw_rows = jnp.pad(conv_w[0], ((0, SUBLANES - conv_w.shape[1]), (0, n_ada - n_cw)))
    ada_blocks = _all_gather8("gather_ada", jnp.concatenate([ada_part, cw_rows], axis=0))
    n_c = NDEV * nbl
    ada_all = jnp.concatenate([ada_blocks[2 * k, :n_c] for k in range(NCHIP)], axis=1) + b_ada
    conv_w8 = jnp.concatenate([ada_blocks[2 * k, n_c:, :n_cw] for k in range(NCHIP)], axis=1)
    ada = lax.dynamic_slice_in_dim(ada_all, dev * nbl, nbl, axis=0).reshape(nbl, 6, D)

    w_in_sh = jnp.pad(w_in[0].astype(BF16), ((0, 0), (0, _rows_of(w_in.shape[2], LANES) * LANES - w_in.shape[2])))
    halve = lambda t: t.reshape(2, t.shape[0] // 2, t.shape[1])
    whole = lambda g, t: jnp.stack(_by_chip(halve(t), g, chip)).reshape((NCHIP,) + t.shape)
    w_out_sh, w_up_sh, w_down_sh = w_out[0].astype(BF16), w_up[0].astype(BF16), w_down[0].astype(BF16)
    g_in = whole(_gather_halves("gather_w", [halve(w_in_sh)])[0], w_in_sh)
    w_in_full = jnp.concatenate([g_in[k][:, :w_in.shape[2]] for k in range(NCHIP)], axis=1)
    w_qkv = jnp.concatenate([w_in_full[:, :3 * WA], w_in_full[:, 3 * WA + nha:]], axis=1)
    w_f = jnp.pad(w_in_full[:, 3 * WA:3 * WA + nha], ((0, 0), (0, FPAD - nha)))

    def late_weights(got):
        g_out, g_up, g_down = whole(got[0], w_out_sh), whole(got[1], w_up_sh), whole(got[2], w_down_sh)
        return (g_out.reshape(NCHIP * w_out.shape[1], D), jnp.concatenate([g_up[k] for k in range(NCHIP)], axis=1),
                g_down.reshape(NCHIP * w_down.shape[1], D))

    n_in = w_in.shape[2]
    to_sib = lambda x, y, c: None

    def reduce_swap(blocks):
        cut = [t.reshape(2 * NCHIP, t.shape[1] // 2, t.shape[2]) for t in blocks]
        copies = [((0, 0, 1), n, lambda x, y, c, k=k: 2 * k + 1 - c, n, lambda x, y, c, k=k: k)
                  for n in range(len(cut)) for k in range(NCHIP)]
        return cut, (cut, [jax.ShapeDtypeStruct((NCHIP,) + t.shape[1:], F32) for t in cut], copies)

    def reduce_sum(cut, from_sib, tag):
        flat = lambda v: v.reshape(-1, v.shape[-1])
        pair_sums = []
        for n, (t, fs) in enumerate(zip(cut, from_sib)):
            mine = lax.dynamic_index_in_dim(t.reshape((NCHIP, 2) + t.shape[1:]), mc, axis=1, keepdims=False)
            pair_sums.append(_add2("pair_sum_%s%d" % (tag, n), flat(mine), flat(fs), out_dtype=BF16).reshape(fs.shape))
        copies = []
        for n in range(len(cut)):
            for j, f in enumerate(_CHIP_FLIPS):
                src = lambda x, y, c, f=f: _chip_index(1 - x if f[0] else x, 1 - y if f[1] else y, c)
                copies.append((f, n, src, n, lambda x, y, c, j=j: j))
        shapes = [jax.ShapeDtypeStruct((NCHIP - 1,) + t.shape[1:], BF16) for t in pair_sums]
        return pair_sums, (pair_sums, shapes, copies)

    def reduce_chip(pair_sums, arrived, tag):
        my_halves = []
        for n, (ps, got) in enumerate(zip(pair_sums, arrived)):
            own = lax.dynamic_index_in_dim(ps, chip, axis=0, keepdims=True)
            my_halves.append(_sum_leading("chip_sum_%s%d" % (tag, n), jnp.concatenate([own, got], axis=0)))
        share = (my_halves, [jax.ShapeDtypeStruct(t.shape, F32) for t in my_halves],
                 [((0, 0, 1), n, to_sib, n, to_sib) for n in range(len(my_halves))])
        return my_halves, share

    def reduce_whole(my_halves, sib_halves):
        whole_blocks = []
        for mh, shf in zip(my_halves, sib_halves):
            pair = jnp.stack([mh, shf])
            whole_blocks.append(jnp.concatenate([lax.dynamic_index_in_dim(pair, mc, axis=0, keepdims=False),
                                                 lax.dynamic_index_in_dim(pair, 1 - mc, axis=0, keepdims=False)], axis=0))
        return whole_blocks

    stash = {}

    def early_swap(g_up, g_down, g_cw):
        sh_cw = _as_rows(g_cw.reshape(conv_w.shape[1], NCHIP, -1).transpose(1, 0, 2))
        rows = w_down.shape[1] + sh_cw.shape[1]
        pad = _rows_of(rows, 2 * LANES) * 2 * LANES - rows
        sh_a = jnp.concatenate([g_down.reshape(NCHIP, -1, D), sh_cw, jnp.zeros((NCHIP, pad, D), F32)], axis=1)
        stash["cut"], ride = reduce_swap([sh_a, g_up])
        return ride

    def early_scatter(from_sib):
        stash["ps"], ride = reduce_sum(stash["cut"], from_sib, "e")
        return ride

    def early_share(arrived):
        stash["mh"], ride = reduce_chip(stash["ps"], arrived, "e")
        return ride

    def last_reduce(g_cat, g_out):
        g_w_in_full = jnp.concatenate([g_cat[:, :3 * WA], g_cat[:, 6 * WA:6 * WA + nha], g_cat[:, 3 * WA:6 * WA]],
                                      axis=1)
        sh_in = jnp.pad(g_w_in_full.reshape(D, NCHIP, n_in).transpose(1, 0, 2),
                        ((0, 0), (0, 0), (0, _rows_of(n_in, LANES) * LANES - n_in)))
        cut, swap = reduce_swap([g_out.reshape(NCHIP, -1, D), sh_in])
        stash["ps_l"], ride = reduce_sum(cut, _exchange("pair_swap_l", swap[0], swap[1], swap[2], []), "l")
        return ride

    gn = jnp.concatenate([gn_a, gn_b], axis=1)
    loss_part, grad_x, d_ada, small, big = _local_step(
        x.reshape(T, D), loss_target.reshape(T, D), positions, ada, w_qkv, w_f, None, None, conv_w8, None,
        b_fgate, gn, ln1_g, ln1_b, conv_b, ln2_g, ln2_b,
        late=([halve(w_out_sh), halve(w_up_sh), halve(w_down_sh)], late_weights),
        early=(early_swap, early_scatter, early_share), last=last_reduce)

    def row_pad(v, rows):
        flat = v.reshape(-1)
        return jnp.pad(flat, (0, rows * D - flat.shape[0]))

    n_cb = _rows_of(2 * DFF)
    small_flat = jnp.concatenate([
        row_pad(small["b_fgate"], 1), row_pad(small["gn"], 1), row_pad(small["ln1_g"], 1),
        row_pad(small["ln1_b"], 1), row_pad(small["ln2_g"], 1), row_pad(small["ln2_b"], 1),
        row_pad(jnp.full((1,), loss_part, F32), 1), row_pad(small["conv_b"], n_cb)])
    n_small = _rows_of(small_flat.shape[0], SUBLANES * D) * SUBLANES
    small_rows = jnp.pad(small_flat, (0, n_small * D - small_flat.shape[0])).reshape(n_small, D)
    ada_rows = jnp.pad(d_ada.reshape(nbl, 6, D), ((0, 0), (0, SUBLANES - 6), (0, 0))).reshape(nbl * SUBLANES, D)
    gathered = _all_gather8("gather_small", jnp.concatenate([small_rows, ada_rows], axis=0))
    red = _sum_leading("sum_small", gathered, tm=SUBLANES)
    g_b_fgate = red[0:1, :nha]
    g_gn = red[1:2, :2 * WA]
    g_ln1_g, g_ln1_b, g_ln2_g, g_ln2_b = red[2:3], red[3:4], red[4:5], red[5:6]
    loss = red[6, 0]
    g_conv_b = red[7:7 + n_cb].reshape(1, -1)[:, :2 * DFF]
    g_b_ada = _add2("sum_b_ada", red[n_small:n_small + SUBLANES], red[n_small + SUBLANES:n_small + 2 * SUBLANES],
                    tm=SUBLANES)[:6].reshape(1, 6 * D)
    dada_all = gathered[:, n_small:].reshape(NDEV, nbl, SUBLANES, D)[:, :, :6].reshape(NDEV * nbl, 6 * D)
    g_w_ada = _ada_bwd(c_all, lax.dynamic_slice_in_dim(dada_all, chip * n_ada, n_ada, axis=1))

    shards_e = reduce_whole(stash["mh"], big["early"])
    mh_l, share_l = reduce_chip(stash["ps_l"], big["last"], "l")
    shards_l = reduce_whole(mh_l, _exchange("pair_share_l", share_l[0], share_l[1], share_l[2], []))
    r0 = w_down.shape[1]
    g_w_down = shards_e[0][:r0]
    g_conv_w = shards_e[0][r0:r0 + SUBLANES].reshape(-1)[:int(np.prod(conv_w.shape[1:]))].reshape(conv_w.shape[1:])
    g_w_up = shards_e[1]
    g_w_out = shards_l[0]
    g_w_in = shards_l[1][:, :n_in]

    grads = dict(w_ada=g_w_ada, b_ada=g_b_ada, w_in=g_w_in, b_fgate=g_b_fgate, gn_a=g_gn[:, :WA], gn_b=g_gn[:, WA:],
                 w_out=g_w_out, ln1_g=g_ln1_g, ln1_b=g_ln1_b, w_up=g_w_up, conv_w=g_conv_w, conv_b=g_conv_b,
                 w_down=g_w_down, ln2_g=g_ln2_g, ln2_b=g_ln2_b)
    weights = dict(w_ada=w_ada, b_ada=b_ada, w_in=w_in, b_fgate=b_fgate, gn_a=gn_a, gn_b=gn_b, w_out=w_out,
                   ln1_g=ln1_g, ln1_b=ln1_b, w_up=w_up, conv_w=conv_w, conv_b=conv_b, w_down=w_down,
                   ln2_g=ln2_g, ln2_b=ln2_b)
    ms = dict(w_ada=m_w_ada, b_ada=m_b_ada, w_in=m_w_in, b_fgate=m_b_fgate, gn_a=m_gn_a, gn_b=m_gn_b,
              w_out=m_w_out, ln1_g=m_ln1_g, ln1_b=m_ln1_b, w_up=m_w_up, conv_w=m_conv_w, conv_b=m_conv_b,
              w_down=m_w_down, ln2_g=m_ln2_g, ln2_b=m_ln2_b)
    vs = dict(w_ada=v_w_ada, b_ada=v_b_ada, w_in=v_w_in, b_fgate=v_b_fgate, gn_a=v_gn_a, gn_b=v_gn_b,
              w_out=v_w_out, ln1_g=v_ln1_g, ln1_b=v_ln1_b, w_up=v_w_up, conv_w=v_conv_w, conv_b=v_conv_b,
              w_down=v_w_down, ln2_g=v_ln2_g, ln2_b=v_ln2_b)
    names = list(weights)
    big_names = ("w_ada", "w_in", "w_out", "w_up", "w_down")
    delta, new_m, new_v = {}, {}, {}
    for n in big_names:
        shp = weights[n].shape
        d, m2, v2 = _adamw("adamw_" + n, weights[n][0], grads[n].reshape(shp[1:]), ms[n][0], vs[n][0])
        delta[n], new_m[n], new_v[n] = d.reshape(shp), m2.reshape(shp), v2.reshape(shp)
    small_names = [n for n in names if n not in big_names]

    def pack_small(src):
        flats = []
        for n in small_names:
            flat = src[n].reshape(-1)
            flats.append(jnp.pad(flat, (0, _rows_of(flat.shape[0]) * D - flat.shape[0])))
        allf = jnp.concatenate(flats)
        rows = _rows_of(allf.shape[0], SUBLANES * D) * SUBLANES
        return jnp.pad(allf, (0, rows * D - allf.shape[0])).reshape(rows, D)

    sd, sm, sv_ = _adamw("adamw_small", pack_small(weights), pack_small(grads), pack_small(ms), pack_small(vs))
    off = 0
    for n in small_names:
        shp = weights[n].shape
        cnt = int(np.prod(shp))
        r = _rows_of(cnt)
        for dst, src in ((delta, sd), (new_m, sm), (new_v, sv_)):
            dst[n] = src[off:off + r].reshape(-1)[:cnt].reshape(shp)
        off += r

    out_g = {n: grads[n].reshape(weights[n].shape) for n in names}
    return (loss, grad_x.reshape(x.shape), *[out_g[n] for n in names], *[delta[n] for n in names],
            *[new_m[n] for n in names], *[new_v[n] for n in names])
```
